```python
import math
import jax, jax.numpy as jnp
from jax import lax
import numpy as np

D_MODEL = 1024
BATCH = 32
SEQ = 256
DEPTH = 2
DEC_BATCH = 2
DEC_SEQ = 2048
PAST_LEN = 512

GRID_W = 64
N_EVEN = (DEPTH + 1) // 2
N_ODD = DEPTH // 2
D_A = D_MODEL // 2
D_B = D_MODEL // 2
B_GROUPS = 4
CHUNK = 128
D_EVEN_IN = 3 * D_A + 2 * D_B
D_C = D_MODEL // 4
C_GROUPS = 4
C_GW = D_C // C_GROUPS
N_HEADS = 12
QK_NOPE = 64
QK_ROPE = 32
V_DIM = 64
Q_LORA = 384
KV_LORA = 256
D_ODD_IN = D_C + Q_LORA + KV_LORA + QK_ROPE
ROPE_BASE = 10000.0
Q_BLOCK = 128
N_GROUPS_MOE = 4
EXPERTS_PER_GROUP = 4
N_EXPERTS = N_GROUPS_MOE * EXPERTS_PER_GROUP
TOP_K = 2
D_EXPERT = 256
EPS = 1e-6
MOD_SCALE = 0.5

kernel_name = "hybrid_diffusion_prefix_step"


def rms_norm(x, g):
    xf = x.astype(jnp.float32)
    y = xf * lax.rsqrt(jnp.mean(xf * xf, axis=-1, keepdims=True) + EPS)
    return (y * g.astype(jnp.float32)).astype(x.dtype)


def adaln_params(cond, w, b):
    m = (jax.nn.silu(cond) @ w + b)[:, None, :]
    return jnp.split(m, 6, axis=-1)


def grid_positions(n_tokens):
    rows = n_tokens // GRID_W
    row = jnp.repeat(jnp.arange(rows), GRID_W)
    col = jnp.tile(jnp.arange(GRID_W), rows)
    return row, col


def axial_rope(x, row, col):
    half = QK_ROPE // 2
    nf = half // 2
    inv = ROPE_BASE ** (-jnp.arange(nf, dtype=jnp.float32) / nf)
    parts = []
    for pos, xq in ((row, x[..., :half]), (col, x[..., half:])):
        ang = pos.astype(jnp.float32)[:, None] * inv[None, :]
        if x.ndim == 4:
            ang = ang[:, None, :]
        cos = jnp.cos(ang).astype(x.dtype)
        sin = jnp.sin(ang).astype(x.dtype)
        a, b = xq[..., :nf], xq[..., nf:]
        parts += [a * cos - b * sin, b * cos + a * sin]
    return jnp.concatenate(parts, axis=-1)


def short_conv3(z, w):
    zp = jnp.pad(z, ((0, 0), (1, 1), (0, 0)))
    return zp[:, :-2] * w[0] + zp[:, 1:-1] * w[1] + zp[:, 2:] * w[2]


def even_mixer(h, w_in, conv_w, sgu_norm_g, sgu_w, sgu_b, w_out):
    bsz, n, _ = h.shape
    z = h @ w_in
    gate_b, gate_c, xa, u, v = jnp.split(z, [D_A, 2 * D_A, 3 * D_A, 3 * D_A + D_B], axis=-1)
    y_a = gate_b * short_conv3(gate_c * xa, conv_w)
    v = rms_norm(v, sgu_norm_g).reshape(bsz, n // CHUNK, CHUNK, B_GROUPS, D_B // B_GROUPS)
    sv = jnp.einsum('gpq,bcqgd->bcpgd', sgu_w, v) + sgu_b.T[:, :, None]
    y_b = u * sv.reshape(bsz, n, D_B)
    return jnp.concatenate([y_a, y_b], axis=-1) @ w_out


def fourier_mix(z):
    bsz, n, _ = z.shape
    zg = z.reshape(bsz, n, C_GROUPS, C_GW).astype(jnp.float32)
    f = jnp.fft.fft2(zg, axes=(1, 3)).real * (1.0 / math.sqrt(n * C_GW))
    return f.reshape(bsz, n, D_C).astype(z.dtype)


def odd_project(h, w_in, q_norm_g, w_uq, kv_norm_g):
    bsz, n, _ = h.shape
    z = h @ w_in
    zc, qc, kvc, k_pe = jnp.split(z, [D_C, D_C + Q_LORA, D_C + Q_LORA + KV_LORA], axis=-1)
    q = (rms_norm(qc, q_norm_g) @ w_uq).reshape(bsz, n, N_HEADS, QK_NOPE + QK_ROPE)
    return fourier_mix(zc), q[..., :QK_NOPE], q[..., QK_NOPE:], rms_norm(kvc, kv_norm_g), k_pe


def mla_expand(kv_c, w_ukv):
    bsz, n, _ = kv_c.shape
    kv = (kv_c @ w_ukv).reshape(bsz, n, N_HEADS, QK_NOPE + V_DIM)
    return kv[..., :QK_NOPE], kv[..., QK_NOPE:]


def mla_attend(q_nope, q_pe, k_nope, k_pe, v):
    bsz, nq = q_nope.shape[:2]
    nb = nq // Q_BLOCK
    scale = 1.0 / math.sqrt(QK_NOPE + QK_ROPE)

    def to_blocks(t):
        return jnp.moveaxis(t.reshape(bsz, nb, Q_BLOCK, *t.shape[2:]), 1, 0)

    def block(qs):
        qn, qp = qs
        s = jnp.einsum('bqhd,bkhd->bhqk', qn, k_nope) + jnp.einsum('bqhr,bkr->bhqk', qp, k_pe)
        p = jax.nn.softmax(s.astype(jnp.float32) * scale, axis=-1).astype(v.dtype)
        return jnp.einsum('bhqk,bkhd->bqhd', p, v)

    o = lax.map(block, (to_blocks(q_nope), to_blocks(q_pe)))
    return jnp.moveaxis(o, 0, 1).reshape(bsz, nq, N_HEADS * V_DIM)


def hier_moe(h, wg, bg, we, be, w1, w3, w2):
    g_logits = (jnp.einsum('bld,dg->blg', h, wg) + bg).astype(jnp.float32)
    g_prob = jax.nn.softmax(g_logits, axis=-1)
    g_idx = jnp.argmax(g_logits, axis=-1)
    g_w = jnp.take_along_axis(g_prob, g_idx[..., None], axis=-1)
    e_logits = jnp.einsum('bld,dge->blge', h, we) + be
    e_sel = jnp.take_along_axis(e_logits, g_idx[..., None, None], axis=2)[..., 0, :].astype(jnp.float32)
    top_v, top_i = lax.top_k(e_sel, TOP_K)
    w_k = jax.nn.softmax(top_v, axis=-1) * g_w
    e_idx = g_idx[..., None] * EXPERTS_PER_GROUP + top_i
    gates = jnp.sum(jax.nn.one_hot(e_idx, N_EXPERTS, dtype=jnp.float32) * w_k[..., None], axis=-2).astype(h.dtype)
    hid = jax.nn.silu(jnp.einsum('bld,edf->blef', h, w1)) * jnp.einsum('bld,edf->blef', h, w3)
    return jnp.einsum('blef,efd->bld', hid * gates[..., None], w2)


def setup_inputs(seed: int = 0) -> dict:
    key = jax.random.key(seed)
    ks = jax.random.split(key, 32)

    def nrm(k, shape, scale):
        return jax.random.normal(k, shape, jnp.float32) * scale

    d = D_MODEL
    return {
        "x_prompt": nrm(ks[0], (BATCH, SEQ, d), 1.0),
        "x_sample": nrm(ks[1], (DEC_BATCH, DEC_SEQ, d), 1.0),
        "cache_ckv": nrm(ks[2], (DEC_BATCH, N_ODD, PAST_LEN, KV_LORA), 1.0),
        "cache_kpe": nrm(ks[3], (DEC_BATCH, N_ODD, PAST_LEN, QK_ROPE), 1.0),
        "c": nrm(ks[4], (DEC_BATCH, d), 1.0),
        "c_ctx": nrm(ks[5], (d,), 1.0),
        "mod_w": nrm(ks[6], (DEPTH, d, 6 * d), MOD_SCALE * d ** -0.5),
        "mod_b": nrm(ks[7], (DEPTH, 6 * d), 0.02),
        "norm1_g": 1.0 + nrm(ks[8], (DEPTH, d), 0.02),
        "norm2_g": 1.0 + nrm(ks[9], (DEPTH, d), 0.02),
        "ev_w_in": nrm(ks[10], (N_EVEN, d, D_EVEN_IN), d ** -0.5),
        "ev_conv_w": nrm(ks[11], (N_EVEN, 3, D_A), 3 ** -0.5),
        "ev_sgu_norm_g": 1.0 + nrm(ks[12], (N_EVEN, D_B), 0.02),
        "ev_sgu_w": nrm(ks[13], (N_EVEN, B_GROUPS, CHUNK, CHUNK), CHUNK ** -0.5),
        "ev_sgu_b": 1.0 + nrm(ks[14], (N_EVEN, B_GROUPS, CHUNK), 0.02),
        "ev_w_out": nrm(ks[15], (N_EVEN, D_A + D_B, d), (D_A + D_B) ** -0.5),
        "od_w_in": nrm(ks[16], (N_ODD, d, D_ODD_IN), d ** -0.5),
        "od_q_norm_g": 1.0 + nrm(ks[17], (N_ODD, Q_LORA), 0.02),
        "od_w_uq": nrm(ks[18], (N_ODD, Q_LORA, N_HEADS * (QK_NOPE + QK_ROPE)), Q_LORA ** -0.5),
        "od_kv_norm_g": 1.0 + nrm(ks[19], (N_ODD, KV_LORA), 0.02),
        "od_w_ukv": nrm(ks[20], (N_ODD, KV_LORA, N_HEADS * (QK_NOPE + V_DIM)), KV_LORA ** -0.5),
        "od_w_out": nrm(ks[21], (N_ODD, D_C + N_HEADS * V_DIM, d), (D_C + N_HEADS * V_DIM) ** -0.5),
        "moe_wg": nrm(ks[22], (DEPTH, d, N_GROUPS_MOE), d ** -0.5),
        "moe_bg": nrm(ks[23], (DEPTH, N_GROUPS_MOE), 0.01),
        "moe_we": nrm(ks[24], (DEPTH, d, N_GROUPS_MOE, EXPERTS_PER_GROUP), d ** -0.5),
        "moe_be": nrm(ks[25], (DEPTH, N_GROUPS_MOE, EXPERTS_PER_GROUP), 0.01),
        "moe_w1": nrm(ks[26], (DEPTH, N_EXPERTS, d, D_EXPERT), d ** -0.5),
        "moe_w3": nrm(ks[27], (DEPTH, N_EXPERTS, d, D_EXPERT), d ** -0.5),
        "moe_w2": nrm(ks[28], (DEPTH, N_EXPERTS, D_EXPERT, d), D_EXPERT ** -0.5),
        "final_norm_g": 1.0 + nrm(ks[29], (d,), 0.02),
    }


def reference(x_prompt, x_sample, cache_ckv, cache_kpe, c, c_ctx, mod_w, mod_b, norm1_g, norm2_g,
              ev_w_in, ev_conv_w, ev_sgu_norm_g, ev_sgu_w, ev_sgu_b, ev_w_out,
              od_w_in, od_q_norm_g, od_w_uq, od_kv_norm_g, od_w_ukv, od_w_out,
              moe_wg, moe_bg, moe_we, moe_be, moe_w1, moe_w3, moe_w2, final_norm_g):
    xp, xs = x_prompt, x_sample
    bp, n_p, _ = xp.shape
    bs, n_s, _ = xs.shape
    row, col = grid_positions(n_s)
    new_ckv, new_kpe = [], []
    for l in range(DEPTH):
        j = l // 2
        sh1p, sc1p, ga1p, sh2p, sc2p, ga2p = adaln_params(c_ctx[None, :], mod_w[l], mod_b[l])
        sh1s, sc1s, ga1s, sh2s, sc2s, ga2s = adaln_params(c, mod_w[l], mod_b[l])
        hp = rms_norm(xp, norm1_g[l]) * (1.0 + sc1p) + sh1p
        hs = rms_norm(xs, norm1_g[l]) * (1.0 + sc1s) + sh1s
        if l % 2 == 0:
            op = even_mixer(hp, ev_w_in[j], ev_conv_w[j], ev_sgu_norm_g[j], ev_sgu_w[j], ev_sgu_b[j], ev_w_out[j])
            os_ = even_mixer(hs, ev_w_in[j], ev_conv_w[j], ev_sgu_norm_g[j], ev_sgu_w[j], ev_sgu_b[j], ev_w_out[j])
        else:
            fp, qn_p, qp_p, kvc_p, kpe_p = odd_project(hp, od_w_in[j], od_q_norm_g[j], od_w_uq[j], od_kv_norm_g[j])
            kn_p, v_p = mla_expand(kvc_p, od_w_ukv[j])
            ap = mla_attend(qn_p, qp_p, kn_p, kpe_p, v_p)
            op = jnp.concatenate([fp, ap], axis=-1) @ od_w_out[j]
            new_ckv.append(kvc_p)
            new_kpe.append(kpe_p)
            fs, qn_s, qp_s, kvc_s, kpe_s = odd_project(hs, od_w_in[j], od_q_norm_g[j], od_w_uq[j], od_kv_norm_g[j])
            qp_s = axial_rope(qp_s, row, col)
            kpe_s = axial_rope(kpe_s, row, col)
            kvc_all = jnp.concatenate([cache_ckv[:, j], kvc_s], axis=1)
            kpe_all = jnp.concatenate([cache_kpe[:, j], kpe_s], axis=1)
            kn_s, v_s = mla_expand(kvc_all, od_w_ukv[j])
            a_s = mla_attend(qn_s, qp_s, kn_s, kpe_all, v_s)
            os_ = jnp.concatenate([fs, a_s], axis=-1) @ od_w_out[j]
        xp = xp + ga1p * op
        xs = xs + ga1s * os_
        hp = rms_norm(xp, norm2_g[l]) * (1.0 + sc2p) + sh2p
        hs = rms_norm(xs, norm2_g[l]) * (1.0 + sc2s) + sh2s
        xp = xp + ga2p * hier_moe(hp, moe_wg[l], moe_bg[l], moe_we[l], moe_be[l], moe_w1[l], moe_w3[l], moe_w2[l])
        xs = xs + ga2s * hier_moe(hs, moe_wg[l], moe_bg[l], moe_we[l], moe_be[l], moe_w1[l], moe_w3[l], moe_w2[l])
    y_prompt = rms_norm(xp, final_norm_g)
    y_sample = rms_norm(xs, final_norm_g)
    new_ckv_arr = jnp.stack(new_ckv, axis=1)
    new_kpe_arr = jnp.stack(new_kpe, axis=1)
    return (y_prompt, y_sample, new_ckv_arr, new_kpe_arr)
```

```python
import functools
import math

import numpy as np
import jax
import jax.numpy as jnp
from jax import lax
from jax.experimental import pallas as pl
from jax.experimental.pallas import tpu as pltpu

D_MODEL = 1024
DEPTH = 2
GRID_W = 64
D_A = D_MODEL // 2
D_B = D_MODEL // 2
B_GROUPS = 4
CHUNK = 128
D_EVEN_IN = 3 * D_A + 2 * D_B
D_C = D_MODEL // 4
C_GROUPS = 4
C_GW = D_C // C_GROUPS
N_HEADS = 12
QK_NOPE = 64
QK_ROPE = 32
V_DIM = 64
Q_LORA = 384
KV_LORA = 256
ROPE_BASE = 10000.0
N_GROUPS_MOE = 4
EXPERTS_PER_GROUP = 4
N_EXPERTS = N_GROUPS_MOE * EXPERTS_PER_GROUP
D_EXPERT = 256
EPS = 1e-6

LANES = 128
HEAD_PAD = 128
ROPE_OFF = QK_NOPE
GATE_OFF = N_GROUPS_MOE
NEG_BIG = -1e30
F32 = jnp.float32
BF16 = jnp.bfloat16
VMEM_LIMIT = 56 * 1024 * 1024


def _cparams(*sem):
    return pltpu.CompilerParams(dimension_semantics=sem, vmem_limit_bytes=VMEM_LIMIT)


def _rms(x, g):
    return x * lax.rsqrt(jnp.mean(x * x, axis=-1, keepdims=True) + EPS) * g


def _bdot(a, b):
    return jnp.dot(a, b, preferred_element_type=F32)


def _mod_kernel(ct_ref, w_ref, b_ref, o_ref):
    c = ct_ref[...]
    s = c * jax.nn.sigmoid(c)
    w = w_ref[...]
    b = b_ref[...]
    for r in range(o_ref.shape[0]):
        o_ref[r:r + 1, :] = jnp.sum(s[:, r:r + 1] * w, axis=0, keepdims=True) + b


def _adaln(cond_t, mod_w, mod_b, n_rows):
    nt = 1024
    d6 = mod_w.shape[-1]
    out = pl.pallas_call(
        _mod_kernel,
        grid=(DEPTH, d6 // nt),
        in_specs=[
            pl.BlockSpec(cond_t.shape, lambda l, n: (0, 0)),
            pl.BlockSpec((None, D_MODEL, nt), lambda l, n: (l, 0, n)),
            pl.BlockSpec((None, 1, nt), lambda l, n: (l, 0, n)),
        ],
        out_specs=pl.BlockSpec((None, n_rows, nt), lambda l, n: (l, 0, n)),
        out_shape=jax.ShapeDtypeStruct((DEPTH, n_rows, d6), F32),
        compiler_params=_cparams("parallel", "parallel"),
        name="adaln",
    )(cond_t, mod_w, mod_b.reshape(DEPTH, 1, d6))
    return out.reshape(DEPTH, n_rows, 6, D_MODEL)


class _Stream:
    def __init__(self, batch, seq, shared_cond):
        self.batch = batch
        self.seq = seq
        self.tokens = batch * seq
        self.shared_cond = shared_cond

    def row_of_batch(self, b):
        return 0 if self.shared_cond else b + 1

    def row_of_tile(self, i, tm):
        return 0 if self.shared_cond else (i * tm) // self.seq + 1


def _mod_spec(stream, layer, tm=None):
    if tm is None:
        return pl.BlockSpec((None, None, 6, D_MODEL), lambda b, i: (layer, stream.row_of_batch(b), 0, 0))
    return pl.BlockSpec((None, None, 6, D_MODEL), lambda i, *_: (layer, stream.row_of_tile(i, tm), 0, 0))


def _even_in_kernel(x_ref, mod_ref, g_ref, w_ref, z_ref, *, nc):
    m = mod_ref[...]
    h = _rms(x_ref[...], g_ref[...]) * (1.0 + m[1:2]) + m[0:1]
    hb = h.astype(BF16)
    for n in range(z_ref.shape[1] // nc):
        z_ref[:, n * nc:(n + 1) * nc] = _bdot(hb, w_ref[:, n * nc:(n + 1) * nc]).astype(z_ref.dtype)


def _even_in(x2d, mod, g, w_in, stream, layer, tm=512):
    t = x2d.shape[0]
    n = w_in.shape[1]
    return pl.pallas_call(
        functools.partial(_even_in_kernel, nc=512),
        grid=(t // tm,),
        in_specs=[
            pl.BlockSpec((tm, D_MODEL), lambda i: (i, 0)),
            _mod_spec(stream, layer, tm),
            pl.BlockSpec((1, D_MODEL), lambda i: (0, 0)),
            pl.BlockSpec((D_MODEL, n), lambda i: (0, 0)),
        ],
        out_specs=pl.BlockSpec((tm, n), lambda i: (i, 0)),
        out_shape=jax.ShapeDtypeStruct((t, n), BF16),
        compiler_params=_cparams("parallel"),
        name="even_in",
    )(x2d, mod, g, w_in)


HALO = 16


def _even_mix_kernel(z_ref, zp_ref, zn_ref, x_ref, mod_ref, cw_ref, sg_ref, sw_ref, sb_ref, wo_ref,
                     o_ref, y_ref):
    i = pl.program_id(1)
    n_i = pl.num_programs(1)
    ts = z_ref.shape[0]
    gate_b = z_ref[:, 0:D_A].astype(F32)
    gate_c = z_ref[:, D_A:2 * D_A].astype(F32)
    xa = z_ref[:, 2 * D_A:3 * D_A].astype(F32)
    t = gate_c * xa
    zp = zp_ref[...].astype(F32)[HALO - 1:HALO, :]
    zn = zn_ref[...].astype(F32)[0:1, :]
    tp = zp[:, D_A:2 * D_A] * zp[:, 2 * D_A:3 * D_A] * (i > 0).astype(F32)
    tn = zn[:, D_A:2 * D_A] * zn[:, 2 * D_A:3 * D_A] * (i < n_i - 1).astype(F32)
    row = lax.broadcasted_iota(jnp.int32, (ts, 1), 0)
    t_prev = jnp.where(row == 0, tp, pltpu.roll(t, 1, axis=0))
    t_next = jnp.where(row == ts - 1, tn, pltpu.roll(t, ts - 1, axis=0))
    cw = cw_ref[...]
    y_a = gate_b * (t_prev * cw[0:1] + t * cw[1:2] + t_next * cw[2:3])
    y_ref[:, 0:D_A] = y_a.astype(BF16)

    u = z_ref[:, 3 * D_A:3 * D_A + D_B].astype(F32)
    v = z_ref[:, 3 * D_A + D_B:3 * D_A + 2 * D_B].astype(F32)
    vb = _rms(v, sg_ref[...]).astype(BF16)
    gw = D_B // B_GROUPS
    for c in range(ts // CHUNK):
        rows = slice(c * CHUNK, (c + 1) * CHUNK)
        for g in range(B_GROUPS):
            cols = slice(g * gw, (g + 1) * gw)
            sv = _bdot(sw_ref[g], vb[rows, cols]) + sb_ref[:, cols]
            y_ref[rows, D_A + g * gw:D_A + (g + 1) * gw] = (u[rows, cols] * sv).astype(BF16)

    o = _bdot(y_ref[...], wo_ref[...])
    o_ref[...] = x_ref[...] + mod_ref[2:3, :] * o


def _even_mix(z3, x3, mod, conv_w, sgu_g, sgu_w, sgu_bias, w_out, stream, layer):
    b, s, n = z3.shape
    ts = min(s, 256)
    n_i = s // ts
    hb = ts // HALO
    last_h = s // HALO - 1
    return pl.pallas_call(
        _even_mix_kernel,
        grid=(b, n_i),
        in_specs=[
            pl.BlockSpec((None, ts, n), lambda bi, i: (bi, i, 0)),
            pl.BlockSpec((None, HALO, n), lambda bi, i: (bi, jnp.maximum(i * hb - 1, 0), 0)),
            pl.BlockSpec((None, HALO, n), lambda bi, i: (bi, jnp.minimum((i + 1) * hb, last_h), 0)),
            pl.BlockSpec((None, ts, D_MODEL), lambda bi, i: (bi, i, 0)),
            _mod_spec(stream, layer),
            pl.BlockSpec(conv_w.shape, lambda bi, i: (0, 0)),
            pl.BlockSpec(sgu_g.shape, lambda bi, i: (0, 0)),
            pl.BlockSpec(sgu_w.shape, lambda bi, i: (0, 0, 0)),
            pl.BlockSpec(sgu_bias.shape, lambda bi, i: (0, 0)),
            pl.BlockSpec(w_out.shape, lambda bi, i: (0, 0)),
        ],
        out_specs=pl.BlockSpec((None, ts, D_MODEL), lambda bi, i: (bi, i, 0)),
        out_shape=jax.ShapeDtypeStruct((b, s, D_MODEL), F32),
        scratch_shapes=[pltpu.VMEM((ts, D_A + D_B), BF16)],
        compiler_params=_cparams("parallel", "parallel"),
        name="even_mix",
    )(z3, z3, z3, x3, mod, conv_w, sgu_g, sgu_w, sgu_bias, w_out)


def _router_kernel(x_ref, mod_ref, g_ref, wr_ref, br_ref, h_ref, gt_ref):
    m = mod_ref[...]
    h = _rms(x_ref[...], g_ref[...]) * (1.0 + m[4:5]) + m[3:4]
    hb = h.astype(BF16)
    h_ref[...] = hb
    logits = _bdot(hb, wr_ref[...]) + br_ref[...]
    tm = logits.shape[0]
    lane = lax.broadcasted_iota(jnp.int32, (tm, LANES), 1)
    lane_f = lane.astype(F32)
    far = float(LANES)

    def first_argmax(vals):
        top = jnp.max(vals, axis=-1, keepdims=True)
        idx = jnp.min(jnp.where(vals == top, lane_f, far), axis=-1, keepdims=True)
        return top, idx

    is_group = lane < N_GROUPS_MOE
    g_top, g_idx = first_argmax(jnp.where(is_group, logits, NEG_BIG))
    g_sum = jnp.sum(jnp.where(is_group, jnp.exp(logits - g_top), 0.0), axis=-1, keepdims=True)
    g_w = 1.0 / g_sum
    lane_group = lax.shift_right_arithmetic(lane - GATE_OFF, 2).astype(F32)
    ev = jnp.where(lane_group == g_idx, logits, NEG_BIG)
    v1, i1 = first_argmax(ev)
    v2, i2 = first_argmax(jnp.where(lane_f == i1, NEG_BIG, ev))
    e2 = jnp.exp(v2 - v1)
    w1 = 1.0 / (1.0 + e2)
    w2 = e2 * w1
    gt_ref[...] = g_w * (jnp.where(lane_f == i1, w1, 0.0) + jnp.where(lane_f == i2, w2, 0.0))


def _router(x2d, mod, g, w_r, b_r, stream, layer, tm=512):
    t = x2d.shape[0]
    return pl.pallas_call(
        _router_kernel,
        grid=(t // tm,),
        in_specs=[
            pl.BlockSpec((tm, D_MODEL), lambda i: (i, 0)),
            _mod_spec(stream, layer, tm),
            pl.BlockSpec((1, D_MODEL), lambda i: (0, 0)),
            pl.BlockSpec((D_MODEL, LANES), lambda i: (0, 0)),
            pl.BlockSpec((1, LANES), lambda i: (0, 0)),
        ],
        out_specs=[
            pl.BlockSpec((tm, D_MODEL), lambda i: (i, 0)),
            pl.BlockSpec((tm, LANES), lambda i: (i, 0)),
        ],
        out_shape=[
            jax.ShapeDtypeStruct((t, D_MODEL), BF16),
            jax.ShapeDtypeStruct((t, LANES), F32),
        ],
        compiler_params=_cparams("parallel"),
        name="router",
    )(x2d, mod, g, w_r, b_r)


def _ffn_dense_kernel(h_ref, gt_ref, x_ref, mod_ref, w1_ref, w3_ref, w2_ref, fg_ref, o_ref, acc_ref,
                      *, final_norm):
    e = pl.program_id(1)

    @pl.when(e == 0)
    def _():
        acc_ref[...] = jnp.zeros_like(acc_ref)

    hb = h_ref[...]
    a = _bdot(hb, w1_ref[...])
    b = _bdot(hb, w3_ref[...])
    lane = lax.broadcasted_iota(jnp.int32, gt_ref.shape, 1)
    gate = jnp.sum(jnp.where(lane == e + GATE_OFF, gt_ref[...], 0.0), axis=-1, keepdims=True)
    hid = (a * jax.nn.sigmoid(a)) * b * gate
    acc_ref[...] += _bdot(hid.astype(BF16), w2_ref[...])

    @pl.when(e == N_EXPERTS - 1)
    def _():
        x2 = x_ref[...] + mod_ref[5:6, :] * acc_ref[...]
        if final_norm:
            x2 = _rms(x2, fg_ref[...])
        o_ref[...] = x2


def _ffn_dense(h2, gates, x2d, mod, w1, w3, w2, final_g, stream, layer, final_norm, tm=512):
    t = x2d.shape[0]
    return pl.pallas_call(
        functools.partial(_ffn_dense_kernel, final_norm=final_norm),
        grid=(t // tm, N_EXPERTS),
        in_specs=[
            pl.BlockSpec((tm, D_MODEL), lambda i, e: (i, 0)),
            pl.BlockSpec((tm, LANES), lambda i, e: (i, 0)),
            pl.BlockSpec((tm, D_MODEL), lambda i, e: (i, 0)),
            _mod_spec(stream, layer, tm),
            pl.BlockSpec((None, None, D_MODEL, D_EXPERT), lambda i, e: (layer, e, 0, 0)),
            pl.BlockSpec((None, None, D_MODEL, D_EXPERT), lambda i, e: (layer, e, 0, 0)),
            pl.BlockSpec((None, None, D_EXPERT, D_MODEL), lambda i, e: (layer, e, 0, 0)),
            pl.BlockSpec((1, D_MODEL), lambda i, e: (0, 0)),
        ],
        out_specs=pl.BlockSpec((tm, D_MODEL), lambda i, e: (i, 0)),
        out_shape=jax.ShapeDtypeStruct((t, D_MODEL), F32),
        scratch_shapes=[pltpu.VMEM((tm, D_MODEL), F32)],
        compiler_params=_cparams("parallel", "arbitrary"),
        name="ffn_dense",
    )(h2, gates, x2d, mod, w1, w3, w2, final_g)


def _moe(x2d, mod, g2, w_r, b_r, w1, w3, w2, final_g, stream, layer, final_norm):
    h2, gates = _router(x2d, mod, g2, w_r, b_r, stream, layer)
    return _ffn_dense(h2, gates, x2d, mod, w1, w3, w2, final_g, stream, layer, final_norm)


def _rope_tables(seq):
    half = QK_ROPE // 2
    nf = half // 2
    inv = ROPE_BASE ** (-np.arange(nf, dtype=np.float64) / nf)
    pos = np.arange(seq)
    row = (pos // GRID_W).astype(np.float64)
    col = (pos % GRID_W).astype(np.float64)
    cos = np.ones((seq, HEAD_PAD), np.float64)
    sin_a = np.zeros((seq, HEAD_PAD), np.float64)
    sin_b = np.zeros((seq, HEAD_PAD), np.float64)
    for part, p in enumerate((row, col)):
        ang = p[:, None] * inv[None, :]
        base = ROPE_OFF + part * half
        cos[:, base:base + nf] = np.cos(ang)
        cos[:, base + nf:base + half] = np.cos(ang)
        sin_a[:, base:base + nf] = -np.sin(ang)
        sin_b[:, base + nf:base + half] = np.sin(ang)
    return tuple(jnp.asarray(a, F32) for a in (cos, sin_a, sin_b))


def _apply_rope(x, cos, sin_a, sin_b, reps):
    nf = QK_ROPE // 4
    width = x.shape[1]
    if reps > 1:
        cos, sin_a, sin_b = (jnp.concatenate([a] * reps, axis=1) for a in (cos, sin_a, sin_b))
    return x * cos + pltpu.roll(x, width - nf, axis=1) * sin_a + pltpu.roll(x, nf, axis=1) * sin_b


def _odd_in_kernel(*refs, rope, emit_cache):
    x_ref, mod_ref, g_ref, w_ref, qg_ref, wq_ref, kg_ref, wk_ref, wv_ref, cs_ref = refs[:10]
    refs = refs[10:]
    if rope:
        cos_ref, sa_ref, sb_ref = refs[:3]
        refs = refs[3:]
    y_ref, q_ref, k_ref, v_ref = refs[:4]
    refs = refs[4:]
    m = mod_ref[...]
    h = _rms(x_ref[...], g_ref[...]) * (1.0 + m[1:2]) + m[0:1]
    z = _bdot(h.astype(BF16), w_ref[...])
    zc = z[:, 0:D_C]
    qc = z[:, D_C:D_C + Q_LORA]
    kvc = z[:, D_C + Q_LORA:D_C + Q_LORA + KV_LORA]
    kpe = z[:, D_C + Q_LORA + KV_LORA:]
    q = _bdot(_rms(qc, qg_ref[...]).astype(BF16), wq_ref[...])
    kvn = _rms(kvc, kg_ref[...])
    if emit_cache:
        ckv_ref, kpe_ref = refs
        ckv_ref[...] = kvn
        kpe_ref[...] = kpe[:, ROPE_OFF:ROPE_OFF + QK_ROPE]
    if rope:
        tabs = (cos_ref[...], sa_ref[...], sb_ref[...])
        q = _apply_rope(q, *tabs, reps=N_HEADS)
        kpe = _apply_rope(kpe, *tabs, reps=1)
    kvb = kvn.astype(BF16)
    k = _bdot(kvb, wk_ref[...]) + jnp.concatenate([kpe] * N_HEADS, axis=1)
    scale = 1.0 / math.sqrt(QK_NOPE + QK_ROPE)
    q_ref[...] = (q * scale).astype(BF16)
    k_ref[...] = k.astype(BF16)
    v_ref[...] = _bdot(kvb, wv_ref[...]).astype(BF16)
    y = _bdot(zc.astype(BF16), cs_ref[...])
    y_ref[0, :, :] = y[:, 0:D_C].astype(BF16)
    y_ref[1, :, :] = y[:, D_C:2 * D_C].astype(BF16)


def _odd_in(x3, mod, g, w_in, q_g, w_q, kv_g, w_k, w_v, cs, stream, layer, rope_tabs, emit_cache):
    b, s, _ = x3.shape
    tm = min(s, 512)
    n_i = s // tm
    rope = rope_tabs is not None
    const = lambda a: pl.BlockSpec(a.shape, lambda bi, i: (0,) * a.ndim)
    in_specs = [
        pl.BlockSpec((None, tm, D_MODEL), lambda bi, i: (bi, i, 0)),
        _mod_spec(stream, layer),
        const(g), const(w_in), const(q_g), const(w_q), const(kv_g), const(w_k), const(w_v), const(cs),
    ]
    args = [x3, mod, g, w_in, q_g, w_q, kv_g, w_k, w_v, cs]
    if rope:
        in_specs += [pl.BlockSpec((tm, HEAD_PAD), lambda bi, i: (i, 0))] * 3
        args += list(rope_tabs)
    hq = N_HEADS * HEAD_PAD
    out_specs = [
        pl.BlockSpec((None, 2, tm, D_C), lambda bi, i: (bi, 0, i, 0)),
        pl.BlockSpec((None, tm, hq), lambda bi, i: (bi, i, 0)),
        pl.BlockSpec((None, tm, hq), lambda bi, i: (bi, i, 0)),
        pl.BlockSpec((None, tm, N_HEADS * V_DIM), lambda bi, i: (bi, i, 0)),
    ]
    out_shape = [
        jax.ShapeDtypeStruct((b, 2, s, D_C), BF16),
        jax.ShapeDtypeStruct((b, s, hq), BF16),
        jax.ShapeDtypeStruct((b, s, hq), BF16),
        jax.ShapeDtypeStruct((b, s, N_HEADS * V_DIM), BF16),
    ]
    if emit_cache:
        out_specs += [
            pl.BlockSpec((None, tm, KV_LORA), lambda bi, i: (bi, i, 0)),
            pl.BlockSpec((None, tm, QK_ROPE), lambda bi, i: (bi, i, 0)),
        ]
        out_shape += [
            jax.ShapeDtypeStruct((b, s, KV_LORA), F32),
            jax.ShapeDtypeStruct((b, s, QK_ROPE), F32),
        ]
    return pl.pallas_call(
        functools.partial(_odd_in_kernel, rope=rope, emit_cache=emit_cache),
        grid=(b, n_i),
        in_specs=in_specs,
        out_specs=out_specs,
        out_shape=out_shape,
        compiler_params=_cparams("parallel", "parallel"),
        name="odd_in",
    )(*args)


def _cache_kv_kernel(c_ref, p_ref, wk_ref, wv_ref, k_ref, v_ref):
    cb = c_ref[...].astype(BF16)
    k = _bdot(cb, wk_ref[...]) + jnp.concatenate([p_ref[...]] * N_HEADS, axis=1)
    k_ref[...] = k.astype(BF16)
    v_ref[...] = _bdot(cb, wv_ref[...]).astype(BF16)


def _cache_kv(ckv, kpe_blk, w_k, w_v):
    b, p, _ = ckv.shape
    hq = N_HEADS * HEAD_PAD
    return pl.pallas_call(
        _cache_kv_kernel,
        grid=(b,),
        in_specs=[
            pl.BlockSpec((None, p, KV_LORA), lambda bi: (bi, 0, 0)),
            pl.BlockSpec((None, p, HEAD_PAD), lambda bi: (bi, 0, 0)),
            pl.BlockSpec(w_k.shape, lambda bi: (0, 0)),
            pl.BlockSpec(w_v.shape, lambda bi: (0, 0)),
        ],
        out_specs=[
            pl.BlockSpec((None, p, hq), lambda bi: (bi, 0, 0)),
            pl.BlockSpec((None, p, N_HEADS * V_DIM), lambda bi: (bi, 0, 0)),
        ],
        out_shape=[
            jax.ShapeDtypeStruct((b, p, hq), BF16),
            jax.ShapeDtypeStruct((b, p, N_HEADS * V_DIM), BF16),
        ],
        compiler_params=_cparams("parallel"),
        name="cache_kv",
    )(ckv, kpe_blk, w_k, w_v)


def _odd_mix_kernel(*refs, with_cache):
    q_ref, k_ref, v_ref = refs[:3]
    refs = refs[3:]
    if with_cache:
        kc_ref, vc_ref = refs[:2]
        refs = refs[2:]
    y_ref, f_ref, x_ref, mod_ref, wo_ref, o_ref, a_ref = refs
    tq = q_ref.shape[0]
    qk_dims = (((1,), (1,)), ((), ()))
    lane = lax.broadcasted_iota(jnp.int32, (tq, 2 * V_DIM), 1)
    for pair in range(N_HEADS // 2):
        vcols = slice(pair * 2 * V_DIM, (pair + 1) * 2 * V_DIM)
        outs = []
        for h in (2 * pair, 2 * pair + 1):
            hcols = slice(h * HEAD_PAD, (h + 1) * HEAD_PAD)
            qh = q_ref[:, hcols]
            s = lax.dot_general(qh, k_ref[:, hcols], qk_dims, preferred_element_type=F32)
            top = jnp.max(s, axis=-1, keepdims=True)
            if with_cache:
                sc = lax.dot_general(qh, kc_ref[:, hcols], qk_dims, preferred_element_type=F32)
                top = jnp.maximum(top, jnp.max(sc, axis=-1, keepdims=True))
            p = jnp.exp(s - top)
            den = jnp.sum(p, axis=-1, keepdims=True)
            o = _bdot(p.astype(BF16), v_ref[:, vcols])
            if with_cache:
                pc = jnp.exp(sc - top)
                den = den + jnp.sum(pc, axis=-1, keepdims=True)
                o = o + _bdot(pc.astype(BF16), vc_ref[:, vcols])
            outs.append(o / den)
        a_ref[:, vcols] = jnp.where(lane < V_DIM, outs[0], outs[1]).astype(BF16)
    f = _bdot(f_ref[...], y_ref[...])
    o = _bdot(f.astype(BF16), wo_ref[0:D_C, :]) + _bdot(a_ref[...], wo_ref[D_C:, :])
    o_ref[...] = x_ref[...] + mod_ref[2:3, :] * o


def _odd_mix(q, k, v, kc, vc, y, fmat, x3, mod, w_out, stream, layer):
    b, s, hq = q.shape
    tq = min(s, 256)
    n_i = s // tq
    with_cache = kc is not None
    hv = N_HEADS * V_DIM
    def per_batch(rows, cols):
        return pl.BlockSpec((None, rows, cols), lambda bi, i: (bi, 0, 0), pipeline_mode=pl.Buffered(1))

    in_specs = [pl.BlockSpec((None, tq, hq), lambda bi, i: (bi, i, 0)), per_batch(s, hq), per_batch(s, hv)]
    args = [q, k, v]
    if with_cache:
        p = kc.shape[1]
        in_specs += [per_batch(p, hq), per_batch(p, hv)]
        args += [kc, vc]
    in_specs += [
        per_batch(2 * s, D_C),
        pl.BlockSpec((tq, 2 * s), lambda bi, i: (i, 0)),
        pl.BlockSpec((None, tq, D_MODEL), lambda bi, i: (bi, i, 0)),
        _mod_spec(stream, layer),
        pl.BlockSpec(w_out.shape, lambda bi, i: (0, 0), pipeline_mode=pl.Buffered(1)),
    ]
    args += [y, fmat, x3, mod, w_out]
    return pl.pallas_call(
        functools.partial(_odd_mix_kernel, with_cache=with_cache),
        grid=(b, n_i),
        in_specs=in_specs,
        out_specs=pl.BlockSpec((None, tq, D_MODEL), lambda bi, i: (bi, i, 0)),
        out_shape=jax.ShapeDtypeStruct((b, s, D_MODEL), F32),
        scratch_shapes=[pltpu.VMEM((tq, hv), BF16)],
        compiler_params=_cparams("parallel", "arbitrary"),
        name="odd_mix",
    )(*args)


def _dft_tables(seq):
    jc = np.arange(C_GW)
    ang_c = 2.0 * np.pi * np.outer(jc, jc) / C_GW
    eye = np.eye(C_GROUPS)
    cs = np.concatenate([np.kron(eye, np.cos(ang_c)), np.kron(eye, np.sin(ang_c))], axis=1)
    jn = np.arange(seq)
    ang_n = 2.0 * np.pi * (np.outer(jn, jn) % seq) / seq
    scale = 1.0 / math.sqrt(seq * C_GW)
    fmat = np.concatenate([np.cos(ang_n), -np.sin(ang_n)], axis=1) * scale
    return jnp.asarray(cs, F32).astype(BF16), jnp.asarray(fmat, F32).astype(BF16)


def _odd_weights(w_in, w_uq, w_ukv):
    d = w_in.shape[0]
    base = D_C + Q_LORA + KV_LORA
    kpe_blk = jnp.zeros((d, HEAD_PAD), w_in.dtype).at[:, ROPE_OFF:ROPE_OFF + QK_ROPE].set(w_in[:, base:])
    w_in_p = jnp.concatenate([w_in[:, :base], kpe_blk], axis=1).astype(BF16)
    qh = w_uq.reshape(Q_LORA, N_HEADS, QK_NOPE + QK_ROPE)
    w_q = jnp.pad(qh, ((0, 0), (0, 0), (0, HEAD_PAD - QK_NOPE - QK_ROPE))).reshape(Q_LORA, -1).astype(BF16)
    kvh = w_ukv.reshape(KV_LORA, N_HEADS, QK_NOPE + V_DIM)
    w_k = jnp.pad(kvh[:, :, :QK_NOPE], ((0, 0), (0, 0), (0, HEAD_PAD - QK_NOPE))).reshape(KV_LORA, -1)
    w_v = kvh[:, :, QK_NOPE:].reshape(KV_LORA, -1)
    return w_in_p, w_q, w_k.astype(BF16), w_v.astype(BF16)


def _router_weights(wg, bg, we, be):
    d = wg.shape[0]
    w = jnp.concatenate([wg, we.reshape(d, N_EXPERTS)], axis=1)
    w = jnp.pad(w, ((0, 0), (0, LANES - w.shape[1]))).astype(BF16)
    b = jnp.concatenate([bg, be.reshape(N_EXPERTS)])
    b = jnp.pad(b, (0, LANES - b.shape[0])).reshape(1, LANES).astype(F32)
    return w, b


def kernel(x_prompt, x_sample, cache_ckv, cache_kpe, c, c_ctx, mod_w, mod_b, norm1_g, norm2_g,
           ev_w_in, ev_conv_w, ev_sgu_norm_g, ev_sgu_w, ev_sgu_b, ev_w_out,
           od_w_in, od_q_norm_g, od_w_uq, od_kv_norm_g, od_w_ukv, od_w_out,
           moe_wg, moe_bg, moe_we, moe_be, moe_w1, moe_w3, moe_w2, final_norm_g):
    bp, n_p, d = x_prompt.shape
    bs, n_s, _ = x_sample.shape
    streams = [(_Stream(bp, n_p, True), x_prompt), (_Stream(bs, n_s, False), x_sample)]

    n_rows = 1 + bs
    cond_t = jnp.concatenate([c_ctx[None, :], c], axis=0).T
    mod = _adaln(cond_t, mod_w, mod_b, n_rows)

    w1b, w3b, w2b = moe_w1.astype(BF16), moe_w3.astype(BF16), moe_w2.astype(BF16)
    final_g = final_norm_g.reshape(1, d)
    xs = [x for _, x in streams]
    new_ckv, new_kpe = [], []
    for l in range(DEPTH):
        j = l // 2
        g1 = norm1_g[l].reshape(1, d)
        g2 = norm2_g[l].reshape(1, d)
        w_r, b_r = _router_weights(moe_wg[l], moe_bg[l], moe_we[l], moe_be[l])
        last = l == DEPTH - 1
        if l % 2 == 0:
            w_in = ev_w_in[j].astype(BF16)
            w_out = ev_w_out[j].astype(BF16)
            sgu_w = ev_sgu_w[j].astype(BF16)
            sgu_g = ev_sgu_norm_g[j].reshape(1, D_B)
            sgu_bias = jnp.repeat(ev_sgu_b[j].T, D_B // B_GROUPS, axis=1)
            for si, (st, _) in enumerate(streams):
                x3 = xs[si]
                z = _even_in(x3.reshape(st.tokens, d), mod, g1, w_in, st, l)
                xs[si] = _even_mix(z.reshape(st.batch, st.seq, -1), x3, mod, ev_conv_w[j], sgu_g, sgu_w,
                                   sgu_bias, w_out, st, l)
        else:
            w_in, w_q, w_k, w_v = _odd_weights(od_w_in[j], od_w_uq[j], od_w_ukv[j])
            w_out = od_w_out[j].astype(BF16)
            q_g = od_q_norm_g[j].reshape(1, Q_LORA)
            kv_g = od_kv_norm_g[j].reshape(1, KV_LORA)
            for si, (st, _) in enumerate(streams):
                x3 = xs[si]
                cs, fmat = _dft_tables(st.seq)
                is_prompt = st.shared_cond
                tabs = None if is_prompt else _rope_tables(st.seq)
                outs = _odd_in(x3, mod, g1, w_in, q_g, w_q, kv_g, w_k, w_v, cs, st, l, tabs, is_prompt)
                y, q, k, v = outs[:4]
                if is_prompt:
                    new_ckv.append(outs[4])
                    new_kpe.append(outs[5])
                    kc = vc = None
                else:
                    kpe_blk = jnp.pad(cache_kpe[:, j], ((0, 0), (0, 0), (ROPE_OFF, HEAD_PAD - ROPE_OFF - QK_ROPE)))
                    kc, vc = _cache_kv(cache_ckv[:, j], kpe_blk, w_k, w_v)
                xs[si] = _odd_mix(q, k, v, kc, vc, y.reshape(st.batch, 2 * st.seq, D_C), fmat, x3, mod,
                                  w_out, st, l)
        for si, (st, _) in enumerate(streams):
            x2 = _moe(xs[si].reshape(st.tokens, d), mod, g2, w_r, b_r, w1b, w3b, w2b, final_g, st, l, last)
            xs[si] = x2.reshape(st.batch, st.seq, d)
    return (xs[0], xs[1], jnp.stack(new_ckv, axis=1), jnp.stack(new_kpe, axis=1))
```

```python
import functools
import math

import numpy as np
import jax
import jax.numpy as jnp
from jax import lax
from jax.experimental import pallas as pl
from jax.experimental.pallas import tpu as pltpu

D_MODEL = 1024
DEPTH = 2
GRID_W = 64
D_A = D_MODEL // 2
D_B = D_MODEL // 2
B_GROUPS = 4
CHUNK = 128
D_EVEN_IN = 3 * D_A + 2 * D_B
D_C = D_MODEL // 4
C_GROUPS = 4
C_GW = D_C // C_GROUPS
N_HEADS = 12
QK_NOPE = 64
QK_ROPE = 32
V_DIM = 64
Q_LORA = 384
KV_LORA = 256
ROPE_BASE = 10000.0
N_GROUPS_MOE = 4
EXPERTS_PER_GROUP = 4
N_EXPERTS = N_GROUPS_MOE * EXPERTS_PER_GROUP
D_EXPERT = 256
EPS = 1e-6

LANES = 128
HEAD_PAD = 128
ROPE_OFF = QK_NOPE
GATE_OFF = N_GROUPS_MOE
NEG_BIG = -1e30
F32 = jnp.float32
BF16 = jnp.bfloat16
VMEM_LIMIT = 56 * 1024 * 1024


def _cparams(*sem):
    return pltpu.CompilerParams(dimension_semantics=sem, vmem_limit_bytes=VMEM_LIMIT)


def _rms(x, g):
    return x * lax.rsqrt(jnp.mean(x * x, axis=-1, keepdims=True) + EPS) * g


def _bdot(a, b):
    return jnp.dot(a, b, preferred_element_type=F32)


def _mod_kernel(ct_ref, w_ref, b_ref, o_ref):
    c = ct_ref[...]
    s = c * jax.nn.sigmoid(c)
    w = w_ref[...]
    b = b_ref[...]
    for r in range(o_ref.shape[0]):
        o_ref[r:r + 1, :] = jnp.sum(s[:, r:r + 1] * w, axis=0, keepdims=True) + b


def _adaln(cond_t, mod_w, mod_b, n_rows):
    nt = 1024
    d6 = mod_w.shape[-1]
    out = pl.pallas_call(
        _mod_kernel,
        grid=(DEPTH, d6 // nt),
        in_specs=[
            pl.BlockSpec(cond_t.shape, lambda l, n: (0, 0)),
            pl.BlockSpec((None, D_MODEL, nt), lambda l, n: (l, 0, n)),
            pl.BlockSpec((None, 1, nt), lambda l, n: (l, 0, n)),
        ],
        out_specs=pl.BlockSpec((None, n_rows, nt), lambda l, n: (l, 0, n)),
        out_shape=jax.ShapeDtypeStruct((DEPTH, n_rows, d6), F32),
        compiler_params=_cparams("parallel", "parallel"),
        name="adaln",
    )(cond_t, mod_w, mod_b.reshape(DEPTH, 1, d6))
    return out.reshape(DEPTH, n_rows, 6, D_MODEL)


class _Stream:
    def __init__(self, batch, seq, shared_cond):
        self.batch = batch
        self.seq = seq
        self.tokens = batch * seq
        self.shared_cond = shared_cond

    def row_of_batch(self, b):
        return 0 if self.shared_cond else b + 1

    def row_of_tile(self, i, tm):
        return 0 if self.shared_cond else (i * tm) // self.seq + 1


def _mod_spec(stream, layer, tm=None):
    if tm is None:
        return pl.BlockSpec((None, None, 6, D_MODEL), lambda b, i: (layer, stream.row_of_batch(b), 0, 0))
    return pl.BlockSpec((None, None, 6, D_MODEL), lambda i, *_: (layer, stream.row_of_tile(i, tm), 0, 0))


def _even_in_kernel(x_ref, mod_ref, g_ref, w_ref, z_ref, *, nc):
    m = mod_ref[...]
    h = _rms(x_ref[...], g_ref[...]) * (1.0 + m[1:2]) + m[0:1]
    hb = h.astype(BF16)
    for n in range(z_ref.shape[1] // nc):
        z_ref[:, n * nc:(n + 1) * nc] = _bdot(hb, w_ref[:, n * nc:(n + 1) * nc]).astype(z_ref.dtype)


def _even_in(x2d, mod, g, w_in, stream, layer, tm=512):
    t = x2d.shape[0]
    n = w_in.shape[1]
    return pl.pallas_call(
        functools.partial(_even_in_kernel, nc=512),
        grid=(t // tm,),
        in_specs=[
            pl.BlockSpec((tm, D_MODEL), lambda i: (i, 0)),
            _mod_spec(stream, layer, tm),
            pl.BlockSpec((1, D_MODEL), lambda i: (0, 0)),
            pl.BlockSpec((D_MODEL, n), lambda i: (0, 0)),
        ],
        out_specs=pl.BlockSpec((tm, n), lambda i: (i, 0)),
        out_shape=jax.ShapeDtypeStruct((t, n), BF16),
        compiler_params=_cparams("parallel"),
        name="even_in",
    )(x2d, mod, g, w_in)


HALO = 16


def _even_mix_kernel(z_ref, zp_ref, zn_ref, x_ref, mod_ref, cw_ref, sg_ref, sw_ref, sb_ref, wo_ref,
                     o_ref, y_ref):
    i = pl.program_id(1)
    n_i = pl.num_programs(1)
    ts = z_ref.shape[0]
    gate_b = z_ref[:, 0:D_A].astype(F32)
    gate_c = z_ref[:, D_A:2 * D_A].astype(F32)
    xa = z_ref[:, 2 * D_A:3 * D_A].astype(F32)
    t = gate_c * xa
    zp = zp_ref[...].astype(F32)[HALO - 1:HALO, :]
    zn = zn_ref[...].astype(F32)[0:1, :]
    tp = zp[:, D_A:2 * D_A] * zp[:, 2 * D_A:3 * D_A] * (i > 0).astype(F32)
    tn = zn[:, D_A:2 * D_A] * zn[:, 2 * D_A:3 * D_A] * (i < n_i - 1).astype(F32)
    row = lax.broadcasted_iota(jnp.int32, (ts, 1), 0)
    t_prev = jnp.where(row == 0, tp, pltpu.roll(t, 1, axis=0))
    t_next = jnp.where(row == ts - 1, tn, pltpu.roll(t, ts - 1, axis=0))
    cw = cw_ref[...]
    y_a = gate_b * (t_prev * cw[0:1] + t * cw[1:2] + t_next * cw[2:3])
    y_ref[:, 0:D_A] = y_a.astype(BF16)

    u = z_ref[:, 3 * D_A:3 * D_A + D_B].astype(F32)
    v = z_ref[:, 3 * D_A + D_B:3 * D_A + 2 * D_B].astype(F32)
    vb = _rms(v, sg_ref[...]).astype(BF16)
    gw = D_B // B_GROUPS
    for c in range(ts // CHUNK):
        rows = slice(c * CHUNK, (c + 1) * CHUNK)
        for g in range(B_GROUPS):
            cols = slice(g * gw, (g + 1) * gw)
            sv = _bdot(sw_ref[g], vb[rows, cols]) + sb_ref[:, cols]
            y_ref[rows, D_A + g * gw:D_A + (g + 1) * gw] = (u[rows, cols] * sv).astype(BF16)

    o = _bdot(y_ref[...], wo_ref[...])
    o_ref[...] = x_ref[...] + mod_ref[2:3, :] * o


def _even_mix(z3, x3, mod, conv_w, sgu_g, sgu_w, sgu_bias, w_out, stream, layer):
    b, s, n = z3.shape
    ts = min(s, 256)
    n_i = s // ts
    hb = ts // HALO
    last_h = s // HALO - 1
    return pl.pallas_call(
        _even_mix_kernel,
        grid=(b, n_i),
        in_specs=[
            pl.BlockSpec((None, ts, n), lambda bi, i: (bi, i, 0)),
            pl.BlockSpec((None, HALO, n), lambda bi, i: (bi, jnp.maximum(i * hb - 1, 0), 0)),
            pl.BlockSpec((None, HALO, n), lambda bi, i: (bi, jnp.minimum((i + 1) * hb, last_h), 0)),
            pl.BlockSpec((None, ts, D_MODEL), lambda bi, i: (bi, i, 0)),
            _mod_spec(stream, layer),
            pl.BlockSpec(conv_w.shape, lambda bi, i: (0, 0)),
            pl.BlockSpec(sgu_g.shape, lambda bi, i: (0, 0)),
            pl.BlockSpec(sgu_w.shape, lambda bi, i: (0, 0, 0)),
            pl.BlockSpec(sgu_bias.shape, lambda bi, i: (0, 0)),
            pl.BlockSpec(w_out.shape, lambda bi, i: (0, 0)),
        ],
        out_specs=pl.BlockSpec((None, ts, D_MODEL), lambda bi, i: (bi, i, 0)),
        out_shape=jax.ShapeDtypeStruct((b, s, D_MODEL), F32),
        scratch_shapes=[pltpu.VMEM((ts, D_A + D_B), BF16)],
        compiler_params=_cparams("parallel", "parallel"),
        name="even_mix",
    )(z3, z3, z3, x3, mod, conv_w, sgu_g, sgu_w, sgu_bias, w_out)


ROUTE_TM = 256
ROUTE_PAD = 8
SORT_ROWS = ROUTE_TM + LANES
RUN_ROWS = SORT_ROWS + 32
XS_W = D_MODEL + LANES
FFN_BM = 256
RUN_SIZES = (256, 128, 64, 32, 16, 8)
TAB_W = 2 * N_GROUPS_MOE


def _run_copies(tab_ref, tile, hbm_ref, vmem_ref, sem, cap, to_hbm, wait):
    off = 0
    for g in range(N_GROUPS_MOE):
        start = tab_ref[tile * TAB_W + g]
        n = tab_ref[tile * TAB_W + N_GROUPS_MOE + g]
        for p in RUN_SIZES:
            done = n & (-2 * p)

            @pl.when((n & p) != 0)
            def _():
                v = vmem_ref.at[pl.ds(pl.multiple_of(off + done, ROUTE_PAD), p)]
                h = hbm_ref.at[pl.ds(pl.multiple_of(g * cap + start + done, ROUTE_PAD), p)]
                cp = pltpu.make_async_copy(v, h, sem) if to_hbm else pltpu.make_async_copy(h, v, sem)
                if wait:
                    cp.wait()
                else:
                    cp.start()
        off = off + n


def _route_kernel(x_ref, mod_ref, g_ref, wr_ref, br_ref, ltri_ref, upper_ref,
                  dloc_ref, tab_ref, xs_ref, sorted_ref, zeros_ref, cnt_ref, sem_ref, zsem_ref,
                  *, cap, n_tiles):
    i = pl.program_id(0)
    slot = lax.rem(i, 2)
    tm = x_ref.shape[0]
    copies = functools.partial(_run_copies, tab_ref, hbm_ref=xs_ref, cap=cap, to_hbm=True)

    @pl.when(i == 0)
    def _():
        for g in range(N_GROUPS_MOE):
            cnt_ref[g] = 0
        zeros_ref[...] = jnp.zeros_like(zeros_ref)

    @pl.when(i >= 2)
    def _():
        copies(tile=i - 2, vmem_ref=sorted_ref.at[slot], sem=sem_ref.at[slot], wait=True)

    m = mod_ref[...]
    h = _rms(x_ref[...], g_ref[...]) * (1.0 + m[4:5]) + m[3:4]
    hb = h.astype(BF16)
    logits = _bdot(hb, wr_ref[...]) + br_ref[...]
    lane = lax.broadcasted_iota(jnp.int32, (tm, LANES), 1)
    lane_f = lane.astype(F32)
    far = float(LANES)

    def first_argmax(vals):
        top = jnp.max(vals, axis=-1, keepdims=True)
        idx = jnp.min(jnp.where(vals == top, lane_f, far), axis=-1, keepdims=True)
        return top, idx

    is_group = lane < N_GROUPS_MOE
    g_top, g_idx = first_argmax(jnp.where(is_group, logits, NEG_BIG))
    g_sum = jnp.sum(jnp.where(is_group, jnp.exp(logits - g_top), 0.0), axis=-1, keepdims=True)
    g_w = 1.0 / g_sum
    lane_group = lax.shift_right_arithmetic(lane - GATE_OFF, 2).astype(F32)
    ev = jnp.where(lane_group == g_idx, logits, NEG_BIG)
    v1, i1 = first_argmax(ev)
    v2, i2 = first_argmax(jnp.where(lane_f == i1, NEG_BIG, ev))
    e2 = jnp.exp(v2 - v1)
    w1 = 1.0 / (1.0 + e2)
    w2 = e2 * w1
    first_lane = GATE_OFF + EXPERTS_PER_GROUP * g_idx
    gates = g_w * (jnp.where(lane_f == i1 - first_lane, w1, 0.0) + jnp.where(lane_f == i2 - first_lane, w2, 0.0))
    g_hi = gates.astype(BF16).astype(F32)
    aux = g_hi + pltpu.roll(gates - g_hi, EXPERTS_PER_GROUP, axis=1)

    g_hot = jnp.where(lane_f == g_idx, 1.0, 0.0)
    before = _bdot(ltri_ref[...], g_hot.astype(BF16))
    rank = jnp.sum(before * g_hot, axis=-1, keepdims=True)
    count = jnp.sum(g_hot, axis=0, keepdims=True)
    padded = jnp.floor((count + (ROUTE_PAD - 1)) * (1.0 / ROUTE_PAD)) * ROUTE_PAD
    offs = _bdot(jnp.broadcast_to(padded, (8, LANES)).astype(BF16), upper_ref[...])[0:1]
    dloc = jnp.sum(g_hot * offs, axis=-1, keepdims=True) + rank
    dloc_ref[...] = jnp.broadcast_to(dloc, (tm, LANES))
    row_f = lax.broadcasted_iota(jnp.int32, (tm, SORT_ROWS), 1).astype(F32)
    pt = jnp.where(row_f == dloc, 1.0, 0.0).astype(BF16)
    haug = jnp.concatenate([hb, aux.astype(BF16)], axis=1)
    sorted_ref[slot] = lax.dot_general(pt, haug, (((0,), (0,)), ((), ())), preferred_element_type=F32)

    lane1 = lax.broadcasted_iota(jnp.int32, (1, LANES), 1)
    for g in range(N_GROUPS_MOE):
        n_g = jnp.sum(jnp.where(lane1 == g, padded, 0.0)).astype(jnp.int32)
        tab_ref[i * TAB_W + g] = cnt_ref[g]
        tab_ref[i * TAB_W + N_GROUPS_MOE + g] = n_g
        cnt_ref[g] = cnt_ref[g] + n_g
    copies(tile=i, vmem_ref=sorted_ref.at[slot], sem=sem_ref.at[slot], wait=False)

    @pl.when(i == n_tiles - 1)
    def _():
        tails = []
        for g in range(N_GROUPS_MOE):
            tab_ref[n_tiles * TAB_W + g] = cnt_ref[g]
            tab_ref[n_tiles * TAB_W + N_GROUPS_MOE + g] = 0
            dst = xs_ref.at[pl.ds(pl.multiple_of(g * cap + cnt_ref[g], ROUTE_PAD), FFN_BM)]
            tails.append(pltpu.make_async_copy(zeros_ref, dst, zsem_ref))
        for cp in tails:
            cp.start()
        for cp in tails:
            cp.wait()
        copies(tile=i, vmem_ref=sorted_ref.at[slot], sem=sem_ref.at[slot], wait=True)

        @pl.when(i >= 1)
        def _():
            copies(tile=i - 1, vmem_ref=sorted_ref.at[1 - slot], sem=sem_ref.at[1 - slot], wait=True)


def _route(x2d, mod, g2, w_r, b_r, stream, layer, cap):
    t = x2d.shape[0]
    tm = ROUTE_TM
    n_tiles = t // tm
    ltri = jnp.asarray(np.tril(np.ones((tm, tm), np.float32), -1), BF16)
    upper = jnp.asarray(np.triu(np.ones((LANES, LANES), np.float32), 1), BF16)
    return pl.pallas_call(
        functools.partial(_route_kernel, cap=cap, n_tiles=n_tiles),
        grid=(n_tiles,),
        in_specs=[
            pl.BlockSpec((tm, D_MODEL), lambda i: (i, 0)),
            _mod_spec(stream, layer, tm),
            pl.BlockSpec((1, D_MODEL), lambda i: (0, 0)),
            pl.BlockSpec((D_MODEL, LANES), lambda i: (0, 0)),
            pl.BlockSpec((1, LANES), lambda i: (0, 0)),
            pl.BlockSpec((tm, tm), lambda i: (0, 0)),
            pl.BlockSpec((LANES, LANES), lambda i: (0, 0)),
        ],
        out_specs=[
            pl.BlockSpec((tm, LANES), lambda i: (i, 0)),
            pl.BlockSpec(memory_space=pltpu.SMEM),
            pl.BlockSpec(memory_space=pl.ANY),
        ],
        out_shape=[
            jax.ShapeDtypeStruct((t, LANES), F32),
            jax.ShapeDtypeStruct(((n_tiles + 1) * TAB_W,), jnp.int32),
            jax.ShapeDtypeStruct((N_GROUPS_MOE * cap, XS_W), F32),
        ],
        scratch_shapes=[
            pltpu.VMEM((2, SORT_ROWS, XS_W), F32),
            pltpu.VMEM((FFN_BM, XS_W), F32),
            pltpu.SMEM((N_GROUPS_MOE,), jnp.int32),
            pltpu.SemaphoreType.DMA((2,)),
            pltpu.SemaphoreType.DMA(()),
        ],
        compiler_params=_cparams("arbitrary"),
        name="moe_route",
    )(x2d, mod, g2, w_r, b_r, ltri, upper)


def _block_lookup(i, tab_ref, n_tiles, cap):
    base = n_tiles * TAB_W
    edges = []
    acc = 0
    for g in range(N_GROUPS_MOE):
        acc = acc + lax.div(tab_ref[base + g] + (FFN_BM - 1), FFN_BM)
        edges.append(acc)
    total = edges[-1]
    ii = jnp.minimum(i, total - 1)
    grp = sum((ii >= e).astype(jnp.int32) for e in edges[:-1])
    first = jnp.where(grp == 0, 0, jnp.where(grp == 1, edges[0], jnp.where(grp == 2, edges[1], edges[2])))
    return grp, grp * (cap // FFN_BM) + ii - first, total


def _ffn_group_kernel(tab_ref, xs_ref, w1_ref, w3_ref, w2_ref, ys_ref, *, cap, n_tiles):
    i = pl.program_id(0)
    _, _, total = _block_lookup(i, tab_ref, n_tiles, cap)

    @pl.when(i < total)
    def _():
        hb = xs_ref[:, 0:D_MODEL].astype(BF16)
        aux = xs_ref[:, D_MODEL:XS_W]
        acc = None
        for e in range(EXPERTS_PER_GROUP):
            a = _bdot(hb, w1_ref[e])
            b = _bdot(hb, w3_ref[e])
            gate = aux[:, e:e + 1] + aux[:, EXPERTS_PER_GROUP + e:EXPERTS_PER_GROUP + e + 1]
            hid = (a * jax.nn.sigmoid(a)) * b * gate
            part = _bdot(hid.astype(BF16), w2_ref[e])
            acc = part if acc is None else acc + part
        ys_ref[...] = acc


def _ffn_group(tab, xs, w1, w3, w2, layer, cap, n_tiles, n_blocks):
    lookup = functools.partial(_block_lookup, n_tiles=n_tiles, cap=cap)
    e4 = EXPERTS_PER_GROUP
    grid_spec = pltpu.PrefetchScalarGridSpec(
        num_scalar_prefetch=1,
        grid=(n_blocks,),
        in_specs=[
            pl.BlockSpec((FFN_BM, XS_W), lambda i, tab: (lookup(i, tab)[1], 0)),
            pl.BlockSpec((None, e4, D_MODEL, D_EXPERT), lambda i, tab: (layer, lookup(i, tab)[0], 0, 0)),
            pl.BlockSpec((None, e4, D_MODEL, D_EXPERT), lambda i, tab: (layer, lookup(i, tab)[0], 0, 0)),
            pl.BlockSpec((None, e4, D_EXPERT, D_MODEL), lambda i, tab: (layer, lookup(i, tab)[0], 0, 0)),
        ],
        out_specs=pl.BlockSpec((FFN_BM, D_MODEL), lambda i, tab: (lookup(i, tab)[1], 0)),
    )
    return pl.pallas_call(
        functools.partial(_ffn_group_kernel, cap=cap, n_tiles=n_tiles),
        grid_spec=grid_spec,
        out_shape=jax.ShapeDtypeStruct((xs.shape[0], D_MODEL), F32),
        compiler_params=_cparams("arbitrary"),
        name="moe_ffn",
    )(tab, xs, w1, w3, w2)


def _combine_kernel(tab_ref, x_ref, mod_ref, dloc_ref, fg_ref, ys_ref, o_ref, runs_ref, sem_ref,
                    *, cap, n_tiles, final_norm):
    i = pl.program_id(0)
    slot = lax.rem(i, 2)
    tm = x_ref.shape[0]
    copies = functools.partial(_run_copies, tab_ref, hbm_ref=ys_ref, cap=cap, to_hbm=False)

    @pl.when(i == 0)
    def _():
        copies(tile=0, vmem_ref=runs_ref.at[0], sem=sem_ref.at[0], wait=False)

    @pl.when(i + 1 < n_tiles)
    def _():
        copies(tile=i + 1, vmem_ref=runs_ref.at[1 - slot], sem=sem_ref.at[1 - slot], wait=False)

    copies(tile=i, vmem_ref=runs_ref.at[slot], sem=sem_ref.at[slot], wait=True)
    covered = 0
    for g in range(N_GROUPS_MOE):
        covered = covered + tab_ref[i * TAB_W + N_GROUPS_MOE + g]
    runs_ref[slot, pl.ds(pl.multiple_of(covered, ROUTE_PAD), LANES), :] = jnp.zeros((LANES, D_MODEL), F32)
    yb = runs_ref[slot, 0:SORT_ROWS, :].astype(BF16)
    row_f = lax.broadcasted_iota(jnp.int32, (tm, SORT_ROWS), 1).astype(F32)
    pt = jnp.where(row_f == dloc_ref[:, 0:1], 1.0, 0.0).astype(BF16)
    x2 = x_ref[...] + mod_ref[5:6, :] * _bdot(pt, yb)
    if final_norm:
        x2 = _rms(x2, fg_ref[...])
    o_ref[...] = x2


def _combine(tab, x2d, mod, dloc, final_g, ys, stream, layer, cap, final_norm):
    t = x2d.shape[0]
    tm = ROUTE_TM
    n_tiles = t // tm
    grid_spec = pltpu.PrefetchScalarGridSpec(
        num_scalar_prefetch=1,
        grid=(n_tiles,),
        in_specs=[
            pl.BlockSpec((tm, D_MODEL), lambda i, tab: (i, 0)),
            _mod_spec(stream, layer, tm),
            pl.BlockSpec((tm, LANES), lambda i, tab: (i, 0)),
            pl.BlockSpec((1, D_MODEL), lambda i, tab: (0, 0)),
            pl.BlockSpec(memory_space=pl.ANY),
        ],
        out_specs=pl.BlockSpec((tm, D_MODEL), lambda i, tab: (i, 0)),
        scratch_shapes=[
            pltpu.VMEM((2, RUN_ROWS, D_MODEL), F32),
            pltpu.SemaphoreType.DMA((2,)),
        ],
    )
    return pl.pallas_call(
        functools.partial(_combine_kernel, cap=cap, n_tiles=n_tiles, final_norm=final_norm),
        grid_spec=grid_spec,
        out_shape=jax.ShapeDtypeStruct((t, D_MODEL), F32),
        compiler_params=_cparams("arbitrary"),
        name="moe_combine",
    )(tab, x2d, mod, dloc, final_g, ys)


def _moe(x2d, mod, g2, w_r, b_r, w1, w3, w2, final_g, stream, layer, final_norm):
    t = x2d.shape[0]
    n_tiles = t // ROUTE_TM
    max_pad = (ROUTE_PAD - 1) * n_tiles
    cap = -(-(t + max_pad + FFN_BM) // FFN_BM) * FFN_BM
    n_blocks = -(-(t + N_GROUPS_MOE * max_pad) // FFN_BM) + N_GROUPS_MOE
    dloc, tab, xs = _route(x2d, mod, g2, w_r, b_r, stream, layer, cap)
    ys = _ffn_group(tab, xs, w1, w3, w2, layer, cap, n_tiles, n_blocks)
    return _combine(tab, x2d, mod, dloc, final_g, ys, stream, layer, cap, final_norm)


def _rope_tables(seq):
    half = QK_ROPE // 2
    nf = half // 2
    inv = ROPE_BASE ** (-np.arange(nf, dtype=np.float64) / nf)
    pos = np.arange(seq)
    row = (pos // GRID_W).astype(np.float64)
    col = (pos % GRID_W).astype(np.float64)
    cos = np.ones((seq, HEAD_PAD), np.float64)
    sin_a = np.zeros((seq, HEAD_PAD), np.float64)
    sin_b = np.zeros((seq, HEAD_PAD), np.float64)
    for part, p in enumerate((row, col)):
        ang = p[:, None] * inv[None, :]
        base = ROPE_OFF + part * half
        cos[:, base:base + nf] = np.cos(ang)
        cos[:, base + nf:base + half] = np.cos(ang)
        sin_a[:, base:base + nf] = -np.sin(ang)
        sin_b[:, base + nf:base + half] = np.sin(ang)
    return tuple(jnp.asarray(a, F32) for a in (cos, sin_a, sin_b))


def _apply_rope(x, cos, sin_a, sin_b, reps):
    nf = QK_ROPE // 4
    width = x.shape[1]
    if reps > 1:
        cos, sin_a, sin_b = (jnp.concatenate([a] * reps, axis=1) for a in (cos, sin_a, sin_b))
    return x * cos + pltpu.roll(x, width - nf, axis=1) * sin_a + pltpu.roll(x, nf, axis=1) * sin_b


def _odd_in_kernel(*refs, rope, emit_cache):
    x_ref, mod_ref, g_ref, w_ref, qg_ref, wq_ref, kg_ref, wk_ref, wv_ref, cs_ref = refs[:10]
    refs = refs[10:]
    if rope:
        cos_ref, sa_ref, sb_ref = refs[:3]
        refs = refs[3:]
    y_ref, q_ref, k_ref, v_ref = refs[:4]
    refs = refs[4:]
    m = mod_ref[...]
    h = _rms(x_ref[...], g_ref[...]) * (1.0 + m[1:2]) + m[0:1]
    z = _bdot(h.astype(BF16), w_ref[...])
    zc = z[:, 0:D_C]
    qc = z[:, D_C:D_C + Q_LORA]
    kvc = z[:, D_C + Q_LORA:D_C + Q_LORA + KV_LORA]
    kpe = z[:, D_C + Q_LORA + KV_LORA:]
    q = _bdot(_rms(qc, qg_ref[...]).astype(BF16), wq_ref[...])
    kvn = _rms(kvc, kg_ref[...])
    if emit_cache:
        ckv_ref, kpe_ref = refs
        ckv_ref[...] = kvn
        kpe_ref[...] = kpe[:, ROPE_OFF:ROPE_OFF + QK_ROPE]
    if rope:
        tabs = (cos_ref[...], sa_ref[...], sb_ref[...])
        q = _apply_rope(q, *tabs, reps=N_HEADS)
        kpe = _apply_rope(kpe, *tabs, reps=1)
    kvb = kvn.astype(BF16)
    k = _bdot(kvb, wk_ref[...]) + jnp.concatenate([kpe] * N_HEADS, axis=1)
    scale = 1.0 / math.sqrt(QK_NOPE + QK_ROPE)
    q_ref[...] = (q * scale).astype(BF16)
    k_ref[...] = k.astype(BF16)
    v_ref[...] = _bdot(kvb, wv_ref[...]).astype(BF16)
    y = _bdot(zc.astype(BF16), cs_ref[...])
    y_ref[0, :, :] = y[:, 0:D_C].astype(BF16)
    y_ref[1, :, :] = y[:, D_C:2 * D_C].astype(BF16)


def _odd_in(x3, mod, g, w_in, q_g, w_q, kv_g, w_k, w_v, cs, stream, layer, rope_tabs, emit_cache):
    b, s, _ = x3.shape
    tm = min(s, 512)
    n_i = s // tm
    rope = rope_tabs is not None
    const = lambda a: pl.BlockSpec(a.shape, lambda bi, i: (0,) * a.ndim)
    in_specs = [
        pl.BlockSpec((None, tm, D_MODEL), lambda bi, i: (bi, i, 0)),
        _mod_spec(stream, layer),
        const(g), const(w_in), const(q_g), const(w_q), const(kv_g), const(w_k), const(w_v), const(cs),
    ]
    args = [x3, mod, g, w_in, q_g, w_q, kv_g, w_k, w_v, cs]
    if rope:
        in_specs += [pl.BlockSpec((tm, HEAD_PAD), lambda bi, i: (i, 0))] * 3
        args += list(rope_tabs)
    hq = N_HEADS * HEAD_PAD
    out_specs = [
        pl.BlockSpec((None, 2, tm, D_C), lambda bi, i: (bi, 0, i, 0)),
        pl.BlockSpec((None, tm, hq), lambda bi, i: (bi, i, 0)),
        pl.BlockSpec((None, tm, hq), lambda bi, i: (bi, i, 0)),
        pl.BlockSpec((None, tm, N_HEADS * V_DIM), lambda bi, i: (bi, i, 0)),
    ]
    out_shape = [
        jax.ShapeDtypeStruct((b, 2, s, D_C), BF16),
        jax.ShapeDtypeStruct((b, s, hq), BF16),
        jax.ShapeDtypeStruct((b, s, hq), BF16),
        jax.ShapeDtypeStruct((b, s, N_HEADS * V_DIM), BF16),
    ]
    if emit_cache:
        out_specs += [
            pl.BlockSpec((None, tm, KV_LORA), lambda bi, i: (bi, i, 0)),
            pl.BlockSpec((None, tm, QK_ROPE), lambda bi, i: (bi, i, 0)),
        ]
        out_shape += [
            jax.ShapeDtypeStruct((b, s, KV_LORA), F32),
            jax.ShapeDtypeStruct((b, s, QK_ROPE), F32),
        ]
    return pl.pallas_call(
        functools.partial(_odd_in_kernel, rope=rope, emit_cache=emit_cache),
        grid=(b, n_i),
        in_specs=in_specs,
        out_specs=out_specs,
        out_shape=out_shape,
        compiler_params=_cparams("parallel", "parallel"),
        name="odd_in",
    )(*args)


def _cache_kv_kernel(c_ref, p_ref, wk_ref, wv_ref, k_ref, v_ref):
    cb = c_ref[...].astype(BF16)
    k = _bdot(cb, wk_ref[...]) + jnp.concatenate([p_ref[...]] * N_HEADS, axis=1)
    k_ref[...] = k.astype(BF16)
    v_ref[...] = _bdot(cb, wv_ref[...]).astype(BF16)


def _cache_kv(ckv, kpe_blk, w_k, w_v):
    b, p, _ = ckv.shape
    hq = N_HEADS * HEAD_PAD
    return pl.pallas_call(
        _cache_kv_kernel,
        grid=(b,),
        in_specs=[
            pl.BlockSpec((None, p, KV_LORA), lambda bi: (bi, 0, 0)),
            pl.BlockSpec((None, p, HEAD_PAD), lambda bi: (bi, 0, 0)),
            pl.BlockSpec(w_k.shape, lambda bi: (0, 0)),
            pl.BlockSpec(w_v.shape, lambda bi: (0, 0)),
        ],
        out_specs=[
            pl.BlockSpec((None, p, hq), lambda bi: (bi, 0, 0)),
            pl.BlockSpec((None, p, N_HEADS * V_DIM), lambda bi: (bi, 0, 0)),
        ],
        out_shape=[
            jax.ShapeDtypeStruct((b, p, hq), BF16),
            jax.ShapeDtypeStruct((b, p, N_HEADS * V_DIM), BF16),
        ],
        compiler_params=_cparams("parallel"),
        name="cache_kv",
    )(ckv, kpe_blk, w_k, w_v)


def _odd_mix_kernel(*refs, with_cache):
    q_ref, k_ref, v_ref = refs[:3]
    refs = refs[3:]
    if with_cache:
        kc_ref, vc_ref = refs[:2]
        refs = refs[2:]
    y_ref, f_ref, x_ref, mod_ref, wo_ref, o_ref, a_ref = refs
    tq = q_ref.shape[0]
    qk_dims = (((1,), (1,)), ((), ()))
    lane = lax.broadcasted_iota(jnp.int32, (tq, 2 * V_DIM), 1)
    for pair in range(N_HEADS // 2):
        vcols = slice(pair * 2 * V_DIM, (pair + 1) * 2 * V_DIM)
        outs = []
        for h in (2 * pair, 2 * pair + 1):
            hcols = slice(h * HEAD_PAD, (h + 1) * HEAD_PAD)
            qh = q_ref[:, hcols]
            s = lax.dot_general(qh, k_ref[:, hcols], qk_dims, preferred_element_type=F32)
            top = jnp.max(s, axis=-1, keepdims=True)
            if with_cache:
                sc = lax.dot_general(qh, kc_ref[:, hcols], qk_dims, preferred_element_type=F32)
                top = jnp.maximum(top, jnp.max(sc, axis=-1, keepdims=True))
            p = jnp.exp(s - top)
            den = jnp.sum(p, axis=-1, keepdims=True)
            o = _bdot(p.astype(BF16), v_ref[:, vcols])
            if with_cache:
                pc = jnp.exp(sc - top)
                den = den + jnp.sum(pc, axis=-1, keepdims=True)
                o = o + _bdot(pc.astype(BF16), vc_ref[:, vcols])
            outs.append(o / den)
        a_ref[:, vcols] = jnp.where(lane < V_DIM, outs[0], outs[1]).astype(BF16)
    f = _bdot(f_ref[...], y_ref[...])
    o = _bdot(f.astype(BF16), wo_ref[0:D_C, :]) + _bdot(a_ref[...], wo_ref[D_C:, :])
    o_ref[...] = x_ref[...] + mod_ref[2:3, :] * o


def _odd_mix(q, k, v, kc, vc, y, fmat, x3, mod, w_out, stream, layer):
    b, s, hq = q.shape
    tq = min(s, 256)
    n_i = s // tq
    with_cache = kc is not None
    hv = N_HEADS * V_DIM
    def per_batch(rows, cols):
        return pl.BlockSpec((None, rows, cols), lambda bi, i: (bi, 0, 0), pipeline_mode=pl.Buffered(1))

    in_specs = [pl.BlockSpec((None, tq, hq), lambda bi, i: (bi, i, 0)), per_batch(s, hq), per_batch(s, hv)]
    args = [q, k, v]
    if with_cache:
        p = kc.shape[1]
        in_specs += [per_batch(p, hq), per_batch(p, hv)]
        args += [kc, vc]
    in_specs += [
        per_batch(2 * s, D_C),
        pl.BlockSpec((tq, 2 * s), lambda bi, i: (i, 0)),
        pl.BlockSpec((None, tq, D_MODEL), lambda bi, i: (bi, i, 0)),
        _mod_spec(stream, layer),
        pl.BlockSpec(w_out.shape, lambda bi, i: (0, 0), pipeline_mode=pl.Buffered(1)),
    ]
    args += [y, fmat, x3, mod, w_out]
    return pl.pallas_call(
        functools.partial(_odd_mix_kernel, with_cache=with_cache),
        grid=(b, n_i),
        in_specs=in_specs,
        out_specs=pl.BlockSpec((None, tq, D_MODEL), lambda bi, i: (bi, i, 0)),
        out_shape=jax.ShapeDtypeStruct((b, s, D_MODEL), F32),
        scratch_shapes=[pltpu.VMEM((tq, hv), BF16)],
        compiler_params=_cparams("parallel", "arbitrary"),
        name="odd_mix",
    )(*args)


def _dft_tables(seq):
    jc = np.arange(C_GW)
    ang_c = 2.0 * np.pi * np.outer(jc, jc) / C_GW
    eye = np.eye(C_GROUPS)
    cs = np.concatenate([np.kron(eye, np.cos(ang_c)), np.kron(eye, np.sin(ang_c))], axis=1)
    jn = np.arange(seq)
    ang_n = 2.0 * np.pi * (np.outer(jn, jn) % seq) / seq
    scale = 1.0 / math.sqrt(seq * C_GW)
    fmat = np.concatenate([np.cos(ang_n), -np.sin(ang_n)], axis=1) * scale
    return jnp.asarray(cs, F32).astype(BF16), jnp.asarray(fmat, F32).astype(BF16)


def _odd_weights(w_in, w_uq, w_ukv):
    d = w_in.shape[0]
    base = D_C + Q_LORA + KV_LORA
    kpe_blk = jnp.zeros((d, HEAD_PAD), w_in.dtype).at[:, ROPE_OFF:ROPE_OFF + QK_ROPE].set(w_in[:, base:])
    w_in_p = jnp.concatenate([w_in[:, :base], kpe_blk], axis=1).astype(BF16)
    qh = w_uq.reshape(Q_LORA, N_HEADS, QK_NOPE + QK_ROPE)
    w_q = jnp.pad(qh, ((0, 0), (0, 0), (0, HEAD_PAD - QK_NOPE - QK_ROPE))).reshape(Q_LORA, -1).astype(BF16)
    kvh = w_ukv.reshape(KV_LORA, N_HEADS, QK_NOPE + V_DIM)
    w_k = jnp.pad(kvh[:, :, :QK_NOPE], ((0, 0), (0, 0), (0, HEAD_PAD - QK_NOPE))).reshape(KV_LORA, -1)
    w_v = kvh[:, :, QK_NOPE:].reshape(KV_LORA, -1)
    return w_in_p, w_q, w_k.astype(BF16), w_v.astype(BF16)


def _router_weights(wg, bg, we, be):
    d = wg.shape[0]
    w = jnp.concatenate([wg, we.reshape(d, N_EXPERTS)], axis=1)
    w = jnp.pad(w, ((0, 0), (0, LANES - w.shape[1]))).astype(BF16)
    b = jnp.concatenate([bg, be.reshape(N_EXPERTS)])
    b = jnp.pad(b, (0, LANES - b.shape[0])).reshape(1, LANES).astype(F32)
    return w, b


def kernel(x_prompt, x_sample, cache_ckv, cache_kpe, c, c_ctx, mod_w, mod_b, norm1_g, norm2_g,
           ev_w_in, ev_conv_w, ev_sgu_norm_g, ev_sgu_w, ev_sgu_b, ev_w_out,
           od_w_in, od_q_norm_g, od_w_uq, od_kv_norm_g, od_w_ukv, od_w_out,
           moe_wg, moe_bg, moe_we, moe_be, moe_w1, moe_w3, moe_w2, final_norm_g):
    bp, n_p, d = x_prompt.shape
    bs, n_s, _ = x_sample.shape
    streams = [(_Stream(bp, n_p, True), x_prompt), (_Stream(bs, n_s, False), x_sample)]

    n_rows = 1 + bs
    cond_t = jnp.concatenate([c_ctx[None, :], c], axis=0).T
    mod = _adaln(cond_t, mod_w, mod_b, n_rows)

    w1b, w3b, w2b = moe_w1.astype(BF16), moe_w3.astype(BF16), moe_w2.astype(BF16)
    final_g = final_norm_g.reshape(1, d)
    xs = [x for _, x in streams]
    new_ckv, new_kpe = [], []
    for l in range(DEPTH):
        j = l // 2
        g1 = norm1_g[l].reshape(1, d)
        g2 = norm2_g[l].reshape(1, d)
        w_r, b_r = _router_weights(moe_wg[l], moe_bg[l], moe_we[l], moe_be[l])
        last = l == DEPTH - 1
        if l % 2 == 0:
            w_in = ev_w_in[j].astype(BF16)
            w_out = ev_w_out[j].astype(BF16)
            sgu_w = ev_sgu_w[j].astype(BF16)
            sgu_g = ev_sgu_norm_g[j].reshape(1, D_B)
            sgu_bias = jnp.repeat(ev_sgu_b[j].T, D_B // B_GROUPS, axis=1)
            for si, (st, _) in enumerate(streams):
                x3 = xs[si]
                z = _even_in(x3.reshape(st.tokens, d), mod, g1, w_in, st, l)
                xs[si] = _even_mix(z.reshape(st.batch, st.seq, -1), x3, mod, ev_conv_w[j], sgu_g, sgu_w,
                                   sgu_bias, w_out, st, l)
        else:
            w_in, w_q, w_k, w_v = _odd_weights(od_w_in[j], od_w_uq[j], od_w_ukv[j])
            w_out = od_w_out[j].astype(BF16)
            q_g = od_q_norm_g[j].reshape(1, Q_LORA)
            kv_g = od_kv_norm_g[j].reshape(1, KV_LORA)
            for si, (st, _) in enumerate(streams):
                x3 = xs[si]
                cs, fmat = _dft_tables(st.seq)
                is_prompt = st.shared_cond
                tabs = None if is_prompt else _rope_tables(st.seq)
                outs = _odd_in(x3, mod, g1, w_in, q_g, w_q, kv_g, w_k, w_v, cs, st, l, tabs, is_prompt)
                y, q, k, v = outs[:4]
                if is_prompt:
                    new_ckv.append(outs[4])
                    new_kpe.append(outs[5])
                    kc = vc = None
                else:
                    kpe_blk = jnp.pad(cache_kpe[:, j], ((0, 0), (0, 0), (ROPE_OFF, HEAD_PAD - ROPE_OFF - QK_ROPE)))
                    kc, vc = _cache_kv(cache_ckv[:, j], kpe_blk, w_k, w_v)
                xs[si] = _odd_mix(q, k, v, kc, vc, y.reshape(st.batch, 2 * st.seq, D_C), fmat, x3, mod,
                                  w_out, st, l)
        for si, (st, _) in enumerate(streams):
            x2 = _moe(xs[si].reshape(st.tokens, d), mod, g2, w_r, b_r, w1b, w3b, w2b, final_g, st, l, last)
            xs[si] = x2.reshape(st.batch, st.seq, d)
    return (xs[0], xs[1], jnp.stack(new_ckv, axis=1), jnp.stack(new_kpe, axis=1))
```

```python
import functools
import math

import numpy as np
import jax
import jax.numpy as jnp
from jax import lax
from jax.experimental import pallas as pl
from jax.experimental.pallas import tpu as pltpu

D_MODEL = 1024
DEPTH = 2
GRID_W = 64
D_A = D_MODEL // 2
D_B = D_MODEL // 2
B_GROUPS = 4
CHUNK = 128
D_EVEN_IN = 3 * D_A + 2 * D_B
D_C = D_MODEL // 4
C_GROUPS = 4
C_GW = D_C // C_GROUPS
N_HEADS = 12
QK_NOPE = 64
QK_ROPE = 32
V_DIM = 64
Q_LORA = 384
KV_LORA = 256
ROPE_BASE = 10000.0
N_GROUPS_MOE = 4
EXPERTS_PER_GROUP = 4
N_EXPERTS = N_GROUPS_MOE * EXPERTS_PER_GROUP
D_EXPERT = 256
EPS = 1e-6

LANES = 128
HEAD_PAD = 128
ROPE_OFF = QK_NOPE
GATE_OFF = N_GROUPS_MOE
NEG_BIG = -1e30
F32 = jnp.float32
BF16 = jnp.bfloat16
VMEM_LIMIT = 56 * 1024 * 1024


def _cparams(*sem):
    return pltpu.CompilerParams(dimension_semantics=sem, vmem_limit_bytes=VMEM_LIMIT)


def _rms(x, g):
    return x * lax.rsqrt(jnp.mean(x * x, axis=-1, keepdims=True) + EPS) * g


def _bdot(a, b):
    return jnp.dot(a, b, preferred_element_type=F32)


def _mod_kernel(ct_ref, w_ref, b_ref, o_ref):
    c = ct_ref[...]
    s = c * jax.nn.sigmoid(c)
    w = w_ref[...]
    b = b_ref[...]
    for r in range(o_ref.shape[0]):
        o_ref[r:r + 1, :] = jnp.sum(s[:, r:r + 1] * w, axis=0, keepdims=True) + b


def _adaln(cond_t, mod_w, mod_b, n_rows):
    nt = 1024
    d6 = mod_w.shape[-1]
    out = pl.pallas_call(
        _mod_kernel,
        grid=(DEPTH, d6 // nt),
        in_specs=[
            pl.BlockSpec(cond_t.shape, lambda l, n: (0, 0)),
            pl.BlockSpec((None, D_MODEL, nt), lambda l, n: (l, 0, n)),
            pl.BlockSpec((None, 1, nt), lambda l, n: (l, 0, n)),
        ],
        out_specs=pl.BlockSpec((None, n_rows, nt), lambda l, n: (l, 0, n)),
        out_shape=jax.ShapeDtypeStruct((DEPTH, n_rows, d6), F32),
        compiler_params=_cparams("parallel", "parallel"),
        name="adaln",
    )(cond_t, mod_w, mod_b.reshape(DEPTH, 1, d6))
    return out.reshape(DEPTH, n_rows, 6, D_MODEL)


class _Stream:
    def __init__(self, batch, seq, shared_cond):
        self.batch = batch
        self.seq = seq
        self.tokens = batch * seq
        self.shared_cond = shared_cond

    def row_of_batch(self, b):
        return 0 if self.shared_cond else b + 1

    def row_of_tile(self, i, tm):
        return 0 if self.shared_cond else (i * tm) // self.seq + 1


def _mod_spec(stream, layer, tm=None):
    if tm is None:
        return pl.BlockSpec((None, None, 6, D_MODEL), lambda b, i: (layer, stream.row_of_batch(b), 0, 0))
    return pl.BlockSpec((None, None, 6, D_MODEL), lambda i, *_: (layer, stream.row_of_tile(i, tm), 0, 0))


def _even_in_kernel(x_ref, mod_ref, g_ref, w_ref, z_ref, *, nc):
    m = mod_ref[...]
    h = _rms(x_ref[...], g_ref[...]) * (1.0 + m[1:2]) + m[0:1]
    hb = h.astype(BF16)
    for n in range(z_ref.shape[1] // nc):
        z_ref[:, n * nc:(n + 1) * nc] = _bdot(hb, w_ref[:, n * nc:(n + 1) * nc]).astype(z_ref.dtype)


def _even_in(x2d, mod, g, w_in, stream, layer, tm=512):
    t = x2d.shape[0]
    n = w_in.shape[1]
    return pl.pallas_call(
        functools.partial(_even_in_kernel, nc=512),
        grid=(t // tm,),
        in_specs=[
            pl.BlockSpec((tm, D_MODEL), lambda i: (i, 0)),
            _mod_spec(stream, layer, tm),
            pl.BlockSpec((1, D_MODEL), lambda i: (0, 0)),
            pl.BlockSpec((D_MODEL, n), lambda i: (0, 0)),
        ],
        out_specs=pl.BlockSpec((tm, n), lambda i: (i, 0)),
        out_shape=jax.ShapeDtypeStruct((t, n), BF16),
        compiler_params=_cparams("parallel"),
        name="even_in",
    )(x2d, mod, g, w_in)


HALO = 16


def _even_mix_kernel(z_ref, zp_ref, zn_ref, x_ref, mod_ref, cw_ref, sg_ref, sw_ref, sb_ref, wo_ref,
                     o_ref, y_ref):
    i = pl.program_id(1)
    n_i = pl.num_programs(1)
    ts = z_ref.shape[0]
    gate_b = z_ref[:, 0:D_A].astype(F32)
    gate_c = z_ref[:, D_A:2 * D_A].astype(F32)
    xa = z_ref[:, 2 * D_A:3 * D_A].astype(F32)
    t = gate_c * xa
    zp = zp_ref[...].astype(F32)[HALO - 1:HALO, :]
    zn = zn_ref[...].astype(F32)[0:1, :]
    tp = zp[:, D_A:2 * D_A] * zp[:, 2 * D_A:3 * D_A] * (i > 0).astype(F32)
    tn = zn[:, D_A:2 * D_A] * zn[:, 2 * D_A:3 * D_A] * (i < n_i - 1).astype(F32)
    row = lax.broadcasted_iota(jnp.int32, (ts, 1), 0)
    t_prev = jnp.where(row == 0, tp, pltpu.roll(t, 1, axis=0))
    t_next = jnp.where(row == ts - 1, tn, pltpu.roll(t, ts - 1, axis=0))
    cw = cw_ref[...]
    y_a = gate_b * (t_prev * cw[0:1] + t * cw[1:2] + t_next * cw[2:3])
    y_ref[:, 0:D_A] = y_a.astype(BF16)

    u = z_ref[:, 3 * D_A:3 * D_A + D_B].astype(F32)
    v = z_ref[:, 3 * D_A + D_B:3 * D_A + 2 * D_B].astype(F32)
    vb = _rms(v, sg_ref[...]).astype(BF16)
    gw = D_B // B_GROUPS
    for c in range(ts // CHUNK):
        rows = slice(c * CHUNK, (c + 1) * CHUNK)
        for g in range(B_GROUPS):
            cols = slice(g * gw, (g + 1) * gw)
            sv = _bdot(sw_ref[g], vb[rows, cols]) + sb_ref[:, cols]
            y_ref[rows, D_A + g * gw:D_A + (g + 1) * gw] = (u[rows, cols] * sv).astype(BF16)

    o = _bdot(y_ref[...], wo_ref[...])
    o_ref[...] = x_ref[...] + mod_ref[2:3, :] * o


def _even_mix(z3, x3, mod, conv_w, sgu_g, sgu_w, sgu_bias, w_out, stream, layer):
    b, s, n = z3.shape
    ts = min(s, 256)
    n_i = s // ts
    hb = ts // HALO
    last_h = s // HALO - 1
    return pl.pallas_call(
        _even_mix_kernel,
        grid=(b, n_i),
        in_specs=[
            pl.BlockSpec((None, ts, n), lambda bi, i: (bi, i, 0)),
            pl.BlockSpec((None, HALO, n), lambda bi, i: (bi, jnp.maximum(i * hb - 1, 0), 0)),
            pl.BlockSpec((None, HALO, n), lambda bi, i: (bi, jnp.minimum((i + 1) * hb, last_h), 0)),
            pl.BlockSpec((None, ts, D_MODEL), lambda bi, i: (bi, i, 0)),
            _mod_spec(stream, layer),
            pl.BlockSpec(conv_w.shape, lambda bi, i: (0, 0)),
            pl.BlockSpec(sgu_g.shape, lambda bi, i: (0, 0)),
            pl.BlockSpec(sgu_w.shape, lambda bi, i: (0, 0, 0)),
            pl.BlockSpec(sgu_bias.shape, lambda bi, i: (0, 0)),
            pl.BlockSpec(w_out.shape, lambda bi, i: (0, 0)),
        ],
        out_specs=pl.BlockSpec((None, ts, D_MODEL), lambda bi, i: (bi, i, 0)),
        out_shape=jax.ShapeDtypeStruct((b, s, D_MODEL), F32),
        scratch_shapes=[pltpu.VMEM((ts, D_A + D_B), BF16)],
        compiler_params=_cparams("parallel", "parallel"),
        name="even_mix",
    )(z3, z3, z3, x3, mod, conv_w, sgu_g, sgu_w, sgu_bias, w_out)


ROUTE_TM = 256
ROUTE_SUBS = 2
ROUTE_STEP = ROUTE_TM * ROUTE_SUBS
ROUTE_PAD = 8
SORT_ROWS = ROUTE_TM + LANES
RUN_ROWS = SORT_ROWS + 32
XS_W = D_MODEL + LANES
GATE_LO = EXPERTS_PER_GROUP
DLOC_HI = 2 * EXPERTS_PER_GROUP
DLOC_RADIX = 16.0
FFN_BM = 512
FFN_HALF = FFN_BM // 2
RUN_SIZES = (256, 128, 64, 32, 16, 8)
TAB_W = 2 * N_GROUPS_MOE


def _round_up(x, m):
    return lax.div(x + (m - 1), m) * m


def _run_copies(tab_ref, tile, hbm_ref, vmem_ref, sem, to_hbm, wait):
    off = 0
    for g in range(N_GROUPS_MOE):
        start = tab_ref[tile * TAB_W + g]
        n = tab_ref[tile * TAB_W + N_GROUPS_MOE + g]
        for p in RUN_SIZES:
            done = n & (-2 * p)

            @pl.when((n & p) != 0)
            def _():
                v = vmem_ref.at[pl.ds(pl.multiple_of(off + done, ROUTE_PAD), p)]
                h = hbm_ref.at[pl.ds(pl.multiple_of(start + done, ROUTE_PAD), p)]
                cp = pltpu.make_async_copy(v, h, sem) if to_hbm else pltpu.make_async_copy(h, v, sem)
                if wait:
                    cp.wait()
                else:
                    cp.start()
        off = off + n


def _zero_fill(tab_ref, meta, zeros_ref, hbm_ref, sem, n_rows, wait):
    def copy(rows, dst_row):
        cp = pltpu.make_async_copy(zeros_ref.at[pl.ds(0, rows)],
                                   hbm_ref.at[pl.ds(pl.multiple_of(dst_row, ROUTE_PAD), rows)], sem)
        if wait:
            cp.wait()
        else:
            cp.start()

    end = 0
    for g in range(N_GROUPS_MOE):
        fill = tab_ref[meta + g]
        start = tab_ref[meta + N_GROUPS_MOE + g]
        end = start + _round_up(fill, FFN_BM)
        tail = end - start - fill
        for p in RUN_SIZES:
            pl.when((tail & p) != 0)(functools.partial(copy, p, start + fill + (tail & (-2 * p))))
    for k in range(n_rows // FFN_BM):
        pl.when(end + k * FFN_BM < n_rows)(functools.partial(copy, FFN_BM, end + k * FFN_BM))


def _route_kernel(xp_ref, xs_ref, mod_ref, g_ref, wr_ref, br_ref, ltri_ref, upper_ref,
                  dloc_ref, tab_ref, sorted_hbm, hbuf_ref, sorted_ref, zeros_ref, cnt_ref, fill_ref,
                  sem_ref, zsem_ref, *, n_steps, n_p_steps):
    sweep = pl.program_id(0)
    i = pl.program_id(1)
    tm = ROUTE_TM
    meta = n_steps * ROUTE_SUBS * TAB_W
    n_rows = sorted_hbm.shape[0]

    @pl.when(sweep == 0)
    def _():
        @pl.when(i == 0)
        def _():
            for g in range(N_GROUPS_MOE):
                fill_ref[g] = 0

        m = mod_ref[...]
        x = jnp.where(i < n_p_steps, xp_ref[...], xs_ref[...])
        h = _rms(x, g_ref[...]) * (1.0 + m[4:5]) + m[3:4]
        hb = h.astype(BF16)
        logits = _bdot(hb, wr_ref[...]) + br_ref[...]
        lane = lax.broadcasted_iota(jnp.int32, (ROUTE_STEP, LANES), 1)
        lane_f = lane.astype(F32)
        far = float(LANES)

        def first_argmax(vals):
            top = jnp.max(vals, axis=-1, keepdims=True)
            idx = jnp.min(jnp.where(vals == top, lane_f, far), axis=-1, keepdims=True)
            return top, idx

        is_group = lane < N_GROUPS_MOE
        g_top, g_idx = first_argmax(jnp.where(is_group, logits, NEG_BIG))
        g_sum = jnp.sum(jnp.where(is_group, jnp.exp(logits - g_top), 0.0), axis=-1, keepdims=True)
        g_w = 1.0 / g_sum
        lane_group = lax.shift_right_arithmetic(lane - GATE_OFF, 2).astype(F32)
        ev = jnp.where(lane_group == g_idx, logits, NEG_BIG)
        v1, i1 = first_argmax(ev)
        v2, i2 = first_argmax(jnp.where(lane_f == i1, NEG_BIG, ev))
        e2 = jnp.exp(v2 - v1)
        w1 = 1.0 / (1.0 + e2)
        w2 = e2 * w1
        first_lane = GATE_OFF + EXPERTS_PER_GROUP * g_idx
        gates = g_w * (jnp.where(lane_f == i1 - first_lane, w1, 0.0)
                       + jnp.where(lane_f == i2 - first_lane, w2, 0.0))
        g_hi = gates.astype(BF16).astype(F32)
        extras = g_hi + pltpu.roll(gates - g_hi, GATE_LO, axis=1)

        g_hot = jnp.where(lane_f == g_idx, 1.0, 0.0)
        lane1 = lax.broadcasted_iota(jnp.int32, (1, LANES), 1)
        dlocs = []
        for sub in range(ROUTE_SUBS):
            hot = g_hot[sub * tm:(sub + 1) * tm]
            before = _bdot(ltri_ref[...], hot.astype(BF16))
            rank = jnp.sum(before * hot, axis=-1, keepdims=True)
            count = jnp.sum(hot, axis=0, keepdims=True)
            padded = jnp.floor((count + (ROUTE_PAD - 1)) * (1.0 / ROUTE_PAD)) * ROUTE_PAD
            offs = _bdot(jnp.broadcast_to(padded, (8, LANES)).astype(BF16), upper_ref[...])[0:1]
            dlocs.append(jnp.sum(hot * offs, axis=-1, keepdims=True) + rank)
            for g in range(N_GROUPS_MOE):
                n_g = jnp.sum(jnp.where(lane1 == g, padded, 0.0)).astype(jnp.int32)
                cnt_ref[(i * ROUTE_SUBS + sub) * N_GROUPS_MOE + g] = n_g
                fill_ref[g] = fill_ref[g] + n_g
        dloc = jnp.concatenate(dlocs, axis=0)
        dloc_ref[...] = jnp.broadcast_to(dloc, (ROUTE_STEP, LANES))
        d_hi = jnp.floor(dloc * (1.0 / DLOC_RADIX))
        extras = (extras + jnp.where(lane == DLOC_HI, d_hi, 0.0)
                  + jnp.where(lane == DLOC_HI + 1, dloc - DLOC_RADIX * d_hi, 0.0))
        hbuf_ref[pl.ds(pl.multiple_of(i * ROUTE_STEP, ROUTE_STEP), ROUTE_STEP), :] = (
            jnp.concatenate([hb, extras.astype(BF16)], axis=1))

    @pl.when(sweep == 1)
    def _():
        @pl.when(i == 0)
        def _():
            start = 0
            for g in range(N_GROUPS_MOE):
                tab_ref[meta + g] = fill_ref[g]
                tab_ref[meta + N_GROUPS_MOE + g] = start
                start = start + _round_up(fill_ref[g], FFN_BM)
                fill_ref[g] = 0
            zeros_ref[...] = jnp.zeros_like(zeros_ref)

        row_f = lax.broadcasted_iota(jnp.int32, (tm, SORT_ROWS), 1).astype(F32)
        for sub in range(ROUTE_SUBS):
            tile = i * ROUTE_SUBS + sub
            copies = functools.partial(_run_copies, tab_ref, hbm_ref=sorted_hbm, vmem_ref=sorted_ref.at[sub],
                                       sem=sem_ref.at[sub], to_hbm=True)

            @pl.when(i >= 1)
            def _():
                copies(tile=tile - ROUTE_SUBS, wait=True)

            rows = hbuf_ref[pl.ds(pl.multiple_of(i * ROUTE_STEP + sub * tm, tm), tm), :]
            ex = rows[:, D_MODEL:].astype(F32)
            dloc = DLOC_RADIX * ex[:, DLOC_HI:DLOC_HI + 1] + ex[:, DLOC_HI + 1:DLOC_HI + 2]
            pt = jnp.where(row_f == dloc, 1.0, 0.0).astype(BF16)
            sorted_ref[sub] = lax.dot_general(pt, rows, (((0,), (0,)), ((), ())), preferred_element_type=F32)
            for g in range(N_GROUPS_MOE):
                n_g = cnt_ref[tile * N_GROUPS_MOE + g]
                tab_ref[tile * TAB_W + g] = tab_ref[meta + N_GROUPS_MOE + g] + fill_ref[g]
                tab_ref[tile * TAB_W + N_GROUPS_MOE + g] = n_g
                fill_ref[g] = fill_ref[g] + n_g
            copies(tile=tile, wait=False)

            @pl.when(i == n_steps - 1)
            def _():
                copies(tile=tile, wait=True)

        @pl.when(i == n_steps - 1)
        def _():
            _zero_fill(tab_ref, meta, zeros_ref, sorted_hbm, zsem_ref, n_rows, wait=False)
            _zero_fill(tab_ref, meta, zeros_ref, sorted_hbm, zsem_ref, n_rows, wait=True)


def _merged_specs(n_p_steps, n_s_steps, layer, sample_seq, step_of):
    def p_map(*idx):
        return (jnp.minimum(step_of(*idx), n_p_steps - 1), 0)

    def s_map(*idx):
        return (jnp.clip(step_of(*idx) - n_p_steps, 0, n_s_steps - 1), 0)

    def mod_map(*idx):
        j = step_of(*idx)
        row = jnp.where(j < n_p_steps, 0, 1 + lax.div(jnp.maximum(j - n_p_steps, 0) * ROUTE_STEP, sample_seq))
        return (layer, row, 0, 0)

    return (pl.BlockSpec((ROUTE_STEP, D_MODEL), p_map), pl.BlockSpec((ROUTE_STEP, D_MODEL), s_map),
            pl.BlockSpec((None, None, 6, D_MODEL), mod_map))


def _route(xp, xs, mod, g2, w_r, b_r, layer, sample_seq, n_rows):
    n_p_steps = xp.shape[0] // ROUTE_STEP
    n_s_steps = xs.shape[0] // ROUTE_STEP
    n_steps = n_p_steps + n_s_steps
    t = n_steps * ROUTE_STEP
    tm = ROUTE_TM
    ltri = jnp.asarray(np.tril(np.ones((tm, tm), np.float32), -1), BF16)
    upper = jnp.asarray(np.triu(np.ones((LANES, LANES), np.float32), 1), BF16)
    step_of = lambda s, i: jnp.where(s == 0, i, n_steps - 1)
    p_spec, s_spec, mod_spec = _merged_specs(n_p_steps, n_s_steps, layer, sample_seq, step_of)
    const = lambda a: pl.BlockSpec(a.shape, lambda s, i: (0,) * a.ndim)
    n_tiles = n_steps * ROUTE_SUBS
    return pl.pallas_call(
        functools.partial(_route_kernel, n_steps=n_steps, n_p_steps=n_p_steps),
        grid=(2, n_steps),
        in_specs=[p_spec, s_spec, mod_spec, const(g2), const(w_r), const(b_r), const(ltri), const(upper)],
        out_specs=[
            pl.BlockSpec((ROUTE_STEP, LANES), lambda s, i: (step_of(s, i), 0)),
            pl.BlockSpec(memory_space=pltpu.SMEM),
            pl.BlockSpec(memory_space=pl.ANY),
        ],
        out_shape=[
            jax.ShapeDtypeStruct((t, LANES), F32),
            jax.ShapeDtypeStruct(((n_tiles + 1) * TAB_W,), jnp.int32),
            jax.ShapeDtypeStruct((n_rows, XS_W), F32),
        ],
        scratch_shapes=[
            pltpu.VMEM((t, XS_W), BF16),
            pltpu.VMEM((ROUTE_SUBS, SORT_ROWS, XS_W), F32),
            pltpu.VMEM((FFN_BM, XS_W), F32),
            pltpu.SMEM((n_tiles * N_GROUPS_MOE,), jnp.int32),
            pltpu.SMEM((N_GROUPS_MOE,), jnp.int32),
            pltpu.SemaphoreType.DMA((ROUTE_SUBS,)),
            pltpu.SemaphoreType.DMA(()),
        ],
        compiler_params=_cparams("arbitrary", "arbitrary"),
        name="moe_route",
    )(xp, xs, mod, g2, w_r, b_r, ltri, upper)


def _ffn_lookup(i, tab_ref, meta):
    fills = [tab_ref[meta + g] for g in range(N_GROUPS_MOE)]
    edges = []
    acc = 0
    for f in fills:
        acc = acc + lax.div(f + (FFN_BM - 1), FFN_BM)
        edges.append(acc)
    total = edges[-1]
    ii = jnp.minimum(i, total - 1)
    grp = sum((ii >= e).astype(jnp.int32) for e in edges[:-1])

    def pick(vals):
        return jnp.where(grp == 0, vals[0], jnp.where(grp == 1, vals[1], jnp.where(grp == 2, vals[2], vals[3])))

    first = pick([0] + edges[:-1])
    return grp, total, ii == first, pick(fills) - (ii - first) * FFN_BM


def _ffn_group_kernel(tab_ref, xs_ref, w1_ref, w3_ref, w2_ref, ys_ref, w1b_ref, w3b_ref, w2b_ref, *, meta):
    i = pl.program_id(0)
    _, total, first_of_group, valid = _ffn_lookup(i, tab_ref, meta)
    active = i < total

    @pl.when(jnp.logical_and(active, first_of_group))
    def _():
        w1b_ref[...] = w1_ref[...].astype(BF16)
        w3b_ref[...] = w3_ref[...].astype(BF16)
        w2b_ref[...] = w2_ref[...].astype(BF16)

    def run(rows):
        hb = xs_ref[0:rows, 0:D_MODEL].astype(BF16)
        ex = xs_ref[0:rows, D_MODEL:XS_W]
        acc = None
        for e in range(EXPERTS_PER_GROUP):
            a = _bdot(hb, w1b_ref[e])
            b = _bdot(hb, w3b_ref[e])
            gate = ex[:, e:e + 1] + ex[:, GATE_LO + e:GATE_LO + e + 1]
            hid = (a * jax.nn.sigmoid(a)) * b * gate
            part = _bdot(hid.astype(BF16), w2b_ref[e])
            acc = part if acc is None else acc + part
        ys_ref[0:rows, :] = acc

    @pl.when(jnp.logical_and(active, valid > FFN_HALF))
    def _():
        run(FFN_BM)

    @pl.when(jnp.logical_and(active, valid <= FFN_HALF))
    def _():
        run(FFN_HALF)
        ys_ref[FFN_HALF:, :] = jnp.zeros((FFN_BM - FFN_HALF, D_MODEL), F32)

    @pl.when(jnp.logical_not(active))
    def _():
        ys_ref[...] = jnp.zeros_like(ys_ref)


def _ffn_group(tab, xs, w1, w3, w2, layer, meta):
    e4 = EXPERTS_PER_GROUP
    n_blocks = xs.shape[0] // FFN_BM
    group_of = lambda i, tab_ref: _ffn_lookup(i, tab_ref, meta)[0]
    grid_spec = pltpu.PrefetchScalarGridSpec(
        num_scalar_prefetch=1,
        grid=(n_blocks,),
        in_specs=[
            pl.BlockSpec((FFN_BM, XS_W), lambda i, tab_ref: (i, 0)),
            pl.BlockSpec((None, e4, D_MODEL, D_EXPERT), lambda i, tab_ref: (layer, group_of(i, tab_ref), 0, 0)),
            pl.BlockSpec((None, e4, D_MODEL, D_EXPERT), lambda i, tab_ref: (layer, group_of(i, tab_ref), 0, 0)),
            pl.BlockSpec((None, e4, D_EXPERT, D_MODEL), lambda i, tab_ref: (layer, group_of(i, tab_ref), 0, 0)),
        ],
        out_specs=pl.BlockSpec((FFN_BM, D_MODEL), lambda i, tab_ref: (i, 0)),
        scratch_shapes=[
            pltpu.VMEM((e4, D_MODEL, D_EXPERT), BF16),
            pltpu.VMEM((e4, D_MODEL, D_EXPERT), BF16),
            pltpu.VMEM((e4, D_EXPERT, D_MODEL), BF16),
        ],
    )
    return pl.pallas_call(
        functools.partial(_ffn_group_kernel, meta=meta),
        grid_spec=grid_spec,
        out_shape=jax.ShapeDtypeStruct((xs.shape[0], D_MODEL), F32),
        compiler_params=_cparams("arbitrary"),
        name="moe_ffn",
    )(tab, xs, w1, w3, w2)


def _combine_kernel(tab_ref, xp_ref, xs_ref, mod_ref, dloc_ref, fg_ref, ys_ref, op_ref, os_ref, runs_ref, sem_ref,
                    *, n_steps, n_p_steps, final_norm):
    i = pl.program_id(0)
    par = lax.rem(i, 2)
    tm = ROUTE_TM

    def copies(step, parity, wait):
        for sub in range(ROUTE_SUBS):
            _run_copies(tab_ref, tile=step * ROUTE_SUBS + sub, hbm_ref=ys_ref, vmem_ref=runs_ref.at[parity, sub],
                        sem=sem_ref.at[parity, sub], to_hbm=False, wait=wait)

    @pl.when(i == 0)
    def _():
        copies(0, 0, wait=False)

    @pl.when(i + 1 < n_steps)
    def _():
        copies(i + 1, 1 - par, wait=False)

    copies(i, par, wait=True)
    row_f = lax.broadcasted_iota(jnp.int32, (tm, SORT_ROWS), 1).astype(F32)
    parts = []
    for sub in range(ROUTE_SUBS):
        tile = i * ROUTE_SUBS + sub
        covered = 0
        for g in range(N_GROUPS_MOE):
            covered = covered + tab_ref[tile * TAB_W + N_GROUPS_MOE + g]
        runs_ref[par, sub, pl.ds(pl.multiple_of(covered, ROUTE_PAD), LANES), :] = jnp.zeros((LANES, D_MODEL), F32)
        yb = runs_ref[par, sub, 0:SORT_ROWS, :].astype(BF16)
        pt = jnp.where(row_f == dloc_ref[sub * tm:(sub + 1) * tm, 0:1], 1.0, 0.0).astype(BF16)
        parts.append(_bdot(pt, yb))
    is_prompt = i < n_p_steps
    x = jnp.where(is_prompt, xp_ref[...], xs_ref[...])
    x2 = x + mod_ref[5:6, :] * jnp.concatenate(parts, axis=0)
    if final_norm:
        x2 = _rms(x2, fg_ref[...])

    @pl.when(is_prompt)
    def _():
        op_ref[...] = x2

    @pl.when(jnp.logical_not(is_prompt))
    def _():
        os_ref[...] = x2


def _combine(tab, xp, xs, mod, dloc, final_g, ys, layer, sample_seq, final_norm):
    n_p_steps = xp.shape[0] // ROUTE_STEP
    n_s_steps = xs.shape[0] // ROUTE_STEP
    n_steps = n_p_steps + n_s_steps
    step_of = lambda i, tab_ref: i
    p_spec, s_spec, mod_spec = _merged_specs(n_p_steps, n_s_steps, layer, sample_seq, step_of)
    grid_spec = pltpu.PrefetchScalarGridSpec(
        num_scalar_prefetch=1,
        grid=(n_steps,),
        in_specs=[
            p_spec, s_spec, mod_spec,
            pl.BlockSpec((ROUTE_STEP, LANES), lambda i, tab_ref: (i, 0)),
            pl.BlockSpec((1, D_MODEL), lambda i, tab_ref: (0, 0)),
            pl.BlockSpec(memory_space=pl.ANY),
        ],
        out_specs=[p_spec, s_spec],
        scratch_shapes=[
            pltpu.VMEM((2, ROUTE_SUBS, RUN_ROWS, D_MODEL), F32),
            pltpu.SemaphoreType.DMA((2, ROUTE_SUBS)),
        ],
    )
    return pl.pallas_call(
        functools.partial(_combine_kernel, n_steps=n_steps, n_p_steps=n_p_steps, final_norm=final_norm),
        grid_spec=grid_spec,
        out_shape=[jax.ShapeDtypeStruct(xp.shape, F32), jax.ShapeDtypeStruct(xs.shape, F32)],
        compiler_params=_cparams("arbitrary"),
        name="moe_combine",
    )(tab, xp, xs, mod, dloc, final_g, ys)


def _moe(xp, xs, mod, g2, w_r, b_r, w1, w3, w2, final_g, layer, sample_seq, final_norm):
    t = xp.shape[0] + xs.shape[0]
    n_tiles = t // ROUTE_TM
    max_rows = t + N_GROUPS_MOE * (ROUTE_PAD - 1) * n_tiles
    n_rows = (-(-max_rows // FFN_BM) + N_GROUPS_MOE) * FFN_BM
    dloc, tab, sorted_x = _route(xp, xs, mod, g2, w_r, b_r, layer, sample_seq, n_rows)
    ys = _ffn_group(tab, sorted_x, w1, w3, w2, layer, n_tiles * TAB_W)
    return _combine(tab, xp, xs, mod, dloc, final_g, ys, layer, sample_seq, final_norm)


def _rope_tables(seq):
    half = QK_ROPE // 2
    nf = half // 2
    inv = ROPE_BASE ** (-np.arange(nf, dtype=np.float64) / nf)
    pos = np.arange(seq)
    row = (pos // GRID_W).astype(np.float64)
    col = (pos % GRID_W).astype(np.float64)
    cos = np.ones((seq, HEAD_PAD), np.float64)
    sin_a = np.zeros((seq, HEAD_PAD), np.float64)
    sin_b = np.zeros((seq, HEAD_PAD), np.float64)
    for part, p in enumerate((row, col)):
        ang = p[:, None] * inv[None, :]
        base = ROPE_OFF + part * half
        cos[:, base:base + nf] = np.cos(ang)
        cos[:, base + nf:base + half] = np.cos(ang)
        sin_a[:, base:base + nf] = -np.sin(ang)
        sin_b[:, base + nf:base + half] = np.sin(ang)
    return tuple(jnp.asarray(a, F32) for a in (cos, sin_a, sin_b))


def _apply_rope(x, cos, sin_a, sin_b, reps):
    nf = QK_ROPE // 4
    width = x.shape[1]
    if reps > 1:
        cos, sin_a, sin_b = (jnp.concatenate([a] * reps, axis=1) for a in (cos, sin_a, sin_b))
    return x * cos + pltpu.roll(x, width - nf, axis=1) * sin_a + pltpu.roll(x, nf, axis=1) * sin_b


def _odd_in_kernel(*refs, rope, emit_cache):
    x_ref, mod_ref, g_ref, w_ref, qg_ref, wq_ref, kg_ref, wk_ref, wv_ref, cs_ref = refs[:10]
    refs = refs[10:]
    if rope:
        cos_ref, sa_ref, sb_ref = refs[:3]
        refs = refs[3:]
    y_ref, q_ref, k_ref, v_ref = refs[:4]
    refs = refs[4:]
    m = mod_ref[...]
    h = _rms(x_ref[...], g_ref[...]) * (1.0 + m[1:2]) + m[0:1]
    z = _bdot(h.astype(BF16), w_ref[...])
    zc = z[:, 0:D_C]
    qc = z[:, D_C:D_C + Q_LORA]
    kvc = z[:, D_C + Q_LORA:D_C + Q_LORA + KV_LORA]
    kpe = z[:, D_C + Q_LORA + KV_LORA:]
    q = _bdot(_rms(qc, qg_ref[...]).astype(BF16), wq_ref[...])
    kvn = _rms(kvc, kg_ref[...])
    if emit_cache:
        ckv_ref, kpe_ref = refs
        ckv_ref[...] = kvn
        kpe_ref[...] = kpe[:, ROPE_OFF:ROPE_OFF + QK_ROPE]
    if rope:
        tabs = (cos_ref[...], sa_ref[...], sb_ref[...])
        q = _apply_rope(q, *tabs, reps=N_HEADS)
        kpe = _apply_rope(kpe, *tabs, reps=1)
    kvb = kvn.astype(BF16)
    k = _bdot(kvb, wk_ref[...]) + jnp.concatenate([kpe] * N_HEADS, axis=1)
    scale = 1.0 / math.sqrt(QK_NOPE + QK_ROPE)
    q_ref[...] = (q * scale).astype(BF16)
    k_ref[...] = k.astype(BF16)
    v_ref[...] = _bdot(kvb, wv_ref[...]).astype(BF16)
    y = _bdot(zc.astype(BF16), cs_ref[...])
    y_ref[0, :, :] = y[:, 0:D_C].astype(BF16)
    y_ref[1, :, :] = y[:, D_C:2 * D_C].astype(BF16)


def _odd_in(x3, mod, g, w_in, q_g, w_q, kv_g, w_k, w_v, cs, stream, layer, rope_tabs, emit_cache):
    b, s, _ = x3.shape
    tm = min(s, 512)
    n_i = s // tm
    rope = rope_tabs is not None
    const = lambda a: pl.BlockSpec(a.shape, lambda bi, i: (0,) * a.ndim)
    in_specs = [
        pl.BlockSpec((None, tm, D_MODEL), lambda bi, i: (bi, i, 0)),
        _mod_spec(stream, layer),
        const(g), const(w_in), const(q_g), const(w_q), const(kv_g), const(w_k), const(w_v), const(cs),
    ]
    args = [x3, mod, g, w_in, q_g, w_q, kv_g, w_k, w_v, cs]
    if rope:
        in_specs += [pl.BlockSpec((tm, HEAD_PAD), lambda bi, i: (i, 0))] * 3
        args += list(rope_tabs)
    hq = N_HEADS * HEAD_PAD
    out_specs = [
        pl.BlockSpec((None, 2, tm, D_C), lambda bi, i: (bi, 0, i, 0)),
        pl.BlockSpec((None, tm, hq), lambda bi, i: (bi, i, 0)),
        pl.BlockSpec((None, tm, hq), lambda bi, i: (bi, i, 0)),
        pl.BlockSpec((None, tm, N_HEADS * V_DIM), lambda bi, i: (bi, i, 0)),
    ]
    out_shape = [
        jax.ShapeDtypeStruct((b, 2, s, D_C), BF16),
        jax.ShapeDtypeStruct((b, s, hq), BF16),
        jax.ShapeDtypeStruct((b, s, hq), BF16),
        jax.ShapeDtypeStruct((b, s, N_HEADS * V_DIM), BF16),
    ]
    if emit_cache:
        out_specs += [
            pl.BlockSpec((None, tm, KV_LORA), lambda bi, i: (bi, i, 0)),
            pl.BlockSpec((None, tm, QK_ROPE), lambda bi, i: (bi, i, 0)),
        ]
        out_shape += [
            jax.ShapeDtypeStruct((b, s, KV_LORA), F32),
            jax.ShapeDtypeStruct((b, s, QK_ROPE), F32),
        ]
    return pl.pallas_call(
        functools.partial(_odd_in_kernel, rope=rope, emit_cache=emit_cache),
        grid=(b, n_i),
        in_specs=in_specs,
        out_specs=out_specs,
        out_shape=out_shape,
        compiler_params=_cparams("parallel", "parallel"),
        name="odd_in",
    )(*args)


def _cache_kv_kernel(c_ref, p_ref, wk_ref, wv_ref, k_ref, v_ref):
    cb = c_ref[...].astype(BF16)
    k = _bdot(cb, wk_ref[...]) + jnp.concatenate([p_ref[...]] * N_HEADS, axis=1)
    k_ref[...] = k.astype(BF16)
    v_ref[...] = _bdot(cb, wv_ref[...]).astype(BF16)


def _cache_kv(ckv, kpe_blk, w_k, w_v):
    b, p, _ = ckv.shape
    hq = N_HEADS * HEAD_PAD
    return pl.pallas_call(
        _cache_kv_kernel,
        grid=(b,),
        in_specs=[
            pl.BlockSpec((None, p, KV_LORA), lambda bi: (bi, 0, 0)),
            pl.BlockSpec((None, p, HEAD_PAD), lambda bi: (bi, 0, 0)),
            pl.BlockSpec(w_k.shape, lambda bi: (0, 0)),
            pl.BlockSpec(w_v.shape, lambda bi: (0, 0)),
        ],
        out_specs=[
            pl.BlockSpec((None, p, hq), lambda bi: (bi, 0, 0)),
            pl.BlockSpec((None, p, N_HEADS * V_DIM), lambda bi: (bi, 0, 0)),
        ],
        out_shape=[
            jax.ShapeDtypeStruct((b, p, hq), BF16),
            jax.ShapeDtypeStruct((b, p, N_HEADS * V_DIM), BF16),
        ],
        compiler_params=_cparams("parallel"),
        name="cache_kv",
    )(ckv, kpe_blk, w_k, w_v)


def _odd_mix_kernel(*refs, with_cache):
    q_ref, k_ref, v_ref = refs[:3]
    refs = refs[3:]
    if with_cache:
        kc_ref, vc_ref = refs[:2]
        refs = refs[2:]
    y_ref, f_ref, x_ref, mod_ref, wo_ref, o_ref, a_ref = refs
    tq = q_ref.shape[0]
    qk_dims = (((1,), (1,)), ((), ()))
    lane = lax.broadcasted_iota(jnp.int32, (tq, 2 * V_DIM), 1)
    for pair in range(N_HEADS // 2):
        vcols = slice(pair * 2 * V_DIM, (pair + 1) * 2 * V_DIM)
        outs = []
        for h in (2 * pair, 2 * pair + 1):
            hcols = slice(h * HEAD_PAD, (h + 1) * HEAD_PAD)
            qh = q_ref[:, hcols]
            s = lax.dot_general(qh, k_ref[:, hcols], qk_dims, preferred_element_type=F32)
            top = jnp.max(s, axis=-1, keepdims=True)
            if with_cache:
                sc = lax.dot_general(qh, kc_ref[:, hcols], qk_dims, preferred_element_type=F32)
                top = jnp.maximum(top, jnp.max(sc, axis=-1, keepdims=True))
            p = jnp.exp(s - top)
            den = jnp.sum(p, axis=-1, keepdims=True)
            o = _bdot(p.astype(BF16), v_ref[:, vcols])
            if with_cache:
                pc = jnp.exp(sc - top)
                den = den + jnp.sum(pc, axis=-1, keepdims=True)
                o = o + _bdot(pc.astype(BF16), vc_ref[:, vcols])
            outs.append(o / den)
        a_ref[:, vcols] = jnp.where(lane < V_DIM, outs[0], outs[1]).astype(BF16)
    f = _bdot(f_ref[...], y_ref[...])
    o = _bdot(f.astype(BF16), wo_ref[0:D_C, :]) + _bdot(a_ref[...], wo_ref[D_C:, :])
    o_ref[...] = x_ref[...] + mod_ref[2:3, :] * o


def _odd_mix(q, k, v, kc, vc, y, fmat, x3, mod, w_out, stream, layer):
    b, s, hq = q.shape
    tq = min(s, 256)
    n_i = s // tq
    with_cache = kc is not None
    hv = N_HEADS * V_DIM
    def per_batch(rows, cols):
        return pl.BlockSpec((None, rows, cols), lambda bi, i: (bi, 0, 0), pipeline_mode=pl.Buffered(1))

    in_specs = [pl.BlockSpec((None, tq, hq), lambda bi, i: (bi, i, 0)), per_batch(s, hq), per_batch(s, hv)]
    args = [q, k, v]
    if with_cache:
        p = kc.shape[1]
        in_specs += [per_batch(p, hq), per_batch(p, hv)]
        args += [kc, vc]
    in_specs += [
        per_batch(2 * s, D_C),
        pl.BlockSpec((tq, 2 * s), lambda bi, i: (i, 0)),
        pl.BlockSpec((None, tq, D_MODEL), lambda bi, i: (bi, i, 0)),
        _mod_spec(stream, layer),
        pl.BlockSpec(w_out.shape, lambda bi, i: (0, 0), pipeline_mode=pl.Buffered(1)),
    ]
    args += [y, fmat, x3, mod, w_out]
    return pl.pallas_call(
        functools.partial(_odd_mix_kernel, with_cache=with_cache),
        grid=(b, n_i),
        in_specs=in_specs,
        out_specs=pl.BlockSpec((None, tq, D_MODEL), lambda bi, i: (bi, i, 0)),
        out_shape=jax.ShapeDtypeStruct((b, s, D_MODEL), F32),
        scratch_shapes=[pltpu.VMEM((tq, hv), BF16)],
        compiler_params=_cparams("parallel", "arbitrary"),
        name="odd_mix",
    )(*args)


def _dft_tables(seq):
    jc = np.arange(C_GW)
    ang_c = 2.0 * np.pi * np.outer(jc, jc) / C_GW
    eye = np.eye(C_GROUPS)
    cs = np.concatenate([np.kron(eye, np.cos(ang_c)), np.kron(eye, np.sin(ang_c))], axis=1)
    jn = np.arange(seq)
    ang_n = 2.0 * np.pi * (np.outer(jn, jn) % seq) / seq
    scale = 1.0 / math.sqrt(seq * C_GW)
    fmat = np.concatenate([np.cos(ang_n), -np.sin(ang_n)], axis=1) * scale
    return jnp.asarray(cs, F32).astype(BF16), jnp.asarray(fmat, F32).astype(BF16)


def _odd_weights(w_in, w_uq, w_ukv):
    d = w_in.shape[0]
    base = D_C + Q_LORA + KV_LORA
    kpe_blk = jnp.zeros((d, HEAD_PAD), w_in.dtype).at[:, ROPE_OFF:ROPE_OFF + QK_ROPE].set(w_in[:, base:])
    w_in_p = jnp.concatenate([w_in[:, :base], kpe_blk], axis=1).astype(BF16)
    qh = w_uq.reshape(Q_LORA, N_HEADS, QK_NOPE + QK_ROPE)
    w_q = jnp.pad(qh, ((0, 0), (0, 0), (0, HEAD_PAD - QK_NOPE - QK_ROPE))).reshape(Q_LORA, -1).astype(BF16)
    kvh = w_ukv.reshape(KV_LORA, N_HEADS, QK_NOPE + V_DIM)
    w_k = jnp.pad(kvh[:, :, :QK_NOPE], ((0, 0), (0, 0), (0, HEAD_PAD - QK_NOPE))).reshape(KV_LORA, -1)
    w_v = kvh[:, :, QK_NOPE:].reshape(KV_LORA, -1)
    return w_in_p, w_q, w_k.astype(BF16), w_v.astype(BF16)


def _router_weights(wg, bg, we, be):
    d = wg.shape[0]
    w = jnp.concatenate([wg, we.reshape(d, N_EXPERTS)], axis=1)
    w = jnp.pad(w, ((0, 0), (0, LANES - w.shape[1]))).astype(BF16)
    b = jnp.concatenate([bg, be.reshape(N_EXPERTS)])
    b = jnp.pad(b, (0, LANES - b.shape[0])).reshape(1, LANES).astype(F32)
    return w, b


def kernel(x_prompt, x_sample, cache_ckv, cache_kpe, c, c_ctx, mod_w, mod_b, norm1_g, norm2_g,
           ev_w_in, ev_conv_w, ev_sgu_norm_g, ev_sgu_w, ev_sgu_b, ev_w_out,
           od_w_in, od_q_norm_g, od_w_uq, od_kv_norm_g, od_w_ukv, od_w_out,
           moe_wg, moe_bg, moe_we, moe_be, moe_w1, moe_w3, moe_w2, final_norm_g):
    bp, n_p, d = x_prompt.shape
    bs, n_s, _ = x_sample.shape
    streams = [(_Stream(bp, n_p, True), x_prompt), (_Stream(bs, n_s, False), x_sample)]

    n_rows = 1 + bs
    cond_t = jnp.concatenate([c_ctx[None, :], c], axis=0).T
    mod = _adaln(cond_t, mod_w, mod_b, n_rows)

    final_g = final_norm_g.reshape(1, d)
    xs = [x for _, x in streams]
    new_ckv, new_kpe = [], []
    for l in range(DEPTH):
        j = l // 2
        g1 = norm1_g[l].reshape(1, d)
        g2 = norm2_g[l].reshape(1, d)
        w_r, b_r = _router_weights(moe_wg[l], moe_bg[l], moe_we[l], moe_be[l])
        last = l == DEPTH - 1
        if l % 2 == 0:
            w_in = ev_w_in[j].astype(BF16)
            w_out = ev_w_out[j].astype(BF16)
            sgu_w = ev_sgu_w[j].astype(BF16)
            sgu_g = ev_sgu_norm_g[j].reshape(1, D_B)
            sgu_bias = jnp.repeat(ev_sgu_b[j].T, D_B // B_GROUPS, axis=1)
            for si, (st, _) in enumerate(streams):
                x3 = xs[si]
                z = _even_in(x3.reshape(st.tokens, d), mod, g1, w_in, st, l)
                xs[si] = _even_mix(z.reshape(st.batch, st.seq, -1), x3, mod, ev_conv_w[j], sgu_g, sgu_w,
                                   sgu_bias, w_out, st, l)
        else:
            w_in, w_q, w_k, w_v = _odd_weights(od_w_in[j], od_w_uq[j], od_w_ukv[j])
            w_out = od_w_out[j].astype(BF16)
            q_g = od_q_norm_g[j].reshape(1, Q_LORA)
            kv_g = od_kv_norm_g[j].reshape(1, KV_LORA)
            for si, (st, _) in enumerate(streams):
                x3 = xs[si]
                cs, fmat = _dft_tables(st.seq)
                is_prompt = st.shared_cond
                tabs = None if is_prompt else _rope_tables(st.seq)
                outs = _odd_in(x3, mod, g1, w_in, q_g, w_q, kv_g, w_k, w_v, cs, st, l, tabs, is_prompt)
                y, q, k, v = outs[:4]
                if is_prompt:
                    new_ckv.append(outs[4])
                    new_kpe.append(outs[5])
                    kc = vc = None
                else:
                    kpe_blk = jnp.pad(cache_kpe[:, j], ((0, 0), (0, 0), (ROPE_OFF, HEAD_PAD - ROPE_OFF - QK_ROPE)))
                    kc, vc = _cache_kv(cache_ckv[:, j], kpe_blk, w_k, w_v)
                xs[si] = _odd_mix(q, k, v, kc, vc, y.reshape(st.batch, 2 * st.seq, D_C), fmat, x3, mod,
                                  w_out, st, l)
        x2p, x2s = _moe(xs[0].reshape(bp * n_p, d), xs[1].reshape(bs * n_s, d), mod, g2, w_r, b_r,
                        moe_w1, moe_w3, moe_w2, final_g, l, n_s, last)
        xs = [x2p.reshape(bp, n_p, d), x2s.reshape(bs, n_s, d)]
    return (xs[0], xs[1], jnp.stack(new_ckv, axis=1), jnp.stack(new_kpe, axis=1))
```

```python
import functools
import math

import numpy as np
import jax
import jax.numpy as jnp
from jax import lax
from jax.experimental import pallas as pl
from jax.experimental.pallas import tpu as pltpu

D_MODEL = 1024
DEPTH = 2
GRID_W = 64
D_A = D_MODEL // 2
D_B = D_MODEL // 2
B_GROUPS = 4
CHUNK = 128
D_EVEN_IN = 3 * D_A + 2 * D_B
D_C = D_MODEL // 4
C_GROUPS = 4
C_GW = D_C // C_GROUPS
N_HEADS = 12
QK_NOPE = 64
QK_ROPE = 32
V_DIM = 64
Q_LORA = 384
KV_LORA = 256
ROPE_BASE = 10000.0
N_GROUPS_MOE = 4
EXPERTS_PER_GROUP = 4
N_EXPERTS = N_GROUPS_MOE * EXPERTS_PER_GROUP
D_EXPERT = 256
EPS = 1e-6

LANES = 128
HEAD_PAD = 128
ROPE_OFF = QK_NOPE
GATE_OFF = N_GROUPS_MOE
NEG_BIG = -1e30
F32 = jnp.float32
BF16 = jnp.bfloat16
VMEM_LIMIT = 56 * 1024 * 1024


def _cparams(*sem):
    return pltpu.CompilerParams(dimension_semantics=sem, vmem_limit_bytes=VMEM_LIMIT)


def _rms(x, g):
    return x * lax.rsqrt(jnp.mean(x * x, axis=-1, keepdims=True) + EPS) * g


def _bdot(a, b):
    return jnp.dot(a, b, preferred_element_type=F32)


def _mod_kernel(ct_ref, w_ref, b_ref, o_ref):
    c = ct_ref[...]
    s = c * jax.nn.sigmoid(c)
    w = w_ref[...]
    b = b_ref[...]
    for r in range(o_ref.shape[0]):
        o_ref[r:r + 1, :] = jnp.sum(s[:, r:r + 1] * w, axis=0, keepdims=True) + b


def _adaln(cond_t, mod_w, mod_b, n_rows):
    nt = 1024
    d6 = mod_w.shape[-1]
    out = pl.pallas_call(
        _mod_kernel,
        grid=(DEPTH, d6 // nt),
        in_specs=[
            pl.BlockSpec(cond_t.shape, lambda l, n: (0, 0)),
            pl.BlockSpec((None, D_MODEL, nt), lambda l, n: (l, 0, n)),
            pl.BlockSpec((None, 1, nt), lambda l, n: (l, 0, n)),
        ],
        out_specs=pl.BlockSpec((None, n_rows, nt), lambda l, n: (l, 0, n)),
        out_shape=jax.ShapeDtypeStruct((DEPTH, n_rows, d6), F32),
        compiler_params=_cparams("parallel", "parallel"),
        name="adaln",
    )(cond_t, mod_w, mod_b.reshape(DEPTH, 1, d6))
    return out.reshape(DEPTH, n_rows, 6, D_MODEL)


class _Stream:
    def __init__(self, batch, seq, shared_cond):
        self.batch = batch
        self.seq = seq
        self.tokens = batch * seq
        self.shared_cond = shared_cond

    def row_of_batch(self, b):
        return 0 if self.shared_cond else b + 1

    def row_of_tile(self, i, tm):
        return 0 if self.shared_cond else (i * tm) // self.seq + 1


def _mod_spec(stream, layer, tm=None):
    if tm is None:
        return pl.BlockSpec((None, None, 6, D_MODEL), lambda b, i: (layer, stream.row_of_batch(b), 0, 0))
    return pl.BlockSpec((None, None, 6, D_MODEL), lambda i, *_: (layer, stream.row_of_tile(i, tm), 0, 0))


def _even_in_kernel(x_ref, mod_ref, g_ref, w_ref, z_ref, *, nc):
    m = mod_ref[...]
    h = _rms(x_ref[...], g_ref[...]) * (1.0 + m[1:2]) + m[0:1]
    hb = h.astype(BF16)
    for n in range(z_ref.shape[1] // nc):
        z_ref[:, n * nc:(n + 1) * nc] = _bdot(hb, w_ref[:, n * nc:(n + 1) * nc]).astype(z_ref.dtype)


def _even_in(x2d, mod, g, w_in, stream, layer, tm=512):
    t = x2d.shape[0]
    n = w_in.shape[1]
    return pl.pallas_call(
        functools.partial(_even_in_kernel, nc=512),
        grid=(t // tm,),
        in_specs=[
            pl.BlockSpec((tm, D_MODEL), lambda i: (i, 0)),
            _mod_spec(stream, layer, tm),
            pl.BlockSpec((1, D_MODEL), lambda i: (0, 0)),
            pl.BlockSpec((D_MODEL, n), lambda i: (0, 0)),
        ],
        out_specs=pl.BlockSpec((tm, n), lambda i: (i, 0)),
        out_shape=jax.ShapeDtypeStruct((t, n), BF16),
        compiler_params=_cparams("parallel"),
        name="even_in",
    )(x2d, mod, g, w_in)


HALO = 16


def _even_mix_kernel(z_ref, zp_ref, zn_ref, x_ref, mod_ref, cw_ref, sg_ref, sw_ref, sb_ref, wo_ref,
                     o_ref, y_ref):
    i = pl.program_id(1)
    n_i = pl.num_programs(1)
    ts = z_ref.shape[0]
    gate_b = z_ref[:, 0:D_A].astype(F32)
    gate_c = z_ref[:, D_A:2 * D_A].astype(F32)
    xa = z_ref[:, 2 * D_A:3 * D_A].astype(F32)
    t = gate_c * xa
    zp = zp_ref[...].astype(F32)[HALO - 1:HALO, :]
    zn = zn_ref[...].astype(F32)[0:1, :]
    tp = zp[:, D_A:2 * D_A] * zp[:, 2 * D_A:3 * D_A] * (i > 0).astype(F32)
    tn = zn[:, D_A:2 * D_A] * zn[:, 2 * D_A:3 * D_A] * (i < n_i - 1).astype(F32)
    row = lax.broadcasted_iota(jnp.int32, (ts, 1), 0)
    t_prev = jnp.where(row == 0, tp, pltpu.roll(t, 1, axis=0))
    t_next = jnp.where(row == ts - 1, tn, pltpu.roll(t, ts - 1, axis=0))
    cw = cw_ref[...]
    y_a = gate_b * (t_prev * cw[0:1] + t * cw[1:2] + t_next * cw[2:3])
    y_ref[:, 0:D_A] = y_a.astype(BF16)

    u = z_ref[:, 3 * D_A:3 * D_A + D_B].astype(F32)
    v = z_ref[:, 3 * D_A + D_B:3 * D_A + 2 * D_B].astype(F32)
    vb = _rms(v, sg_ref[...]).astype(BF16)
    gw = D_B // B_GROUPS
    for c in range(ts // CHUNK):
        rows = slice(c * CHUNK, (c + 1) * CHUNK)
        for g in range(B_GROUPS):
            cols = slice(g * gw, (g + 1) * gw)
            sv = _bdot(sw_ref[g], vb[rows, cols]) + sb_ref[:, cols]
            y_ref[rows, D_A + g * gw:D_A + (g + 1) * gw] = (u[rows, cols] * sv).astype(BF16)

    o = _bdot(y_ref[...], wo_ref[...])
    o_ref[...] = x_ref[...] + mod_ref[2:3, :] * o


def _even_mix(z3, x3, mod, conv_w, sgu_g, sgu_w, sgu_bias, w_out, stream, layer):
    b, s, n = z3.shape
    ts = min(s, 256)
    n_i = s // ts
    hb = ts // HALO
    last_h = s // HALO - 1
    return pl.pallas_call(
        _even_mix_kernel,
        grid=(b, n_i),
        in_specs=[
            pl.BlockSpec((None, ts, n), lambda bi, i: (bi, i, 0)),
            pl.BlockSpec((None, HALO, n), lambda bi, i: (bi, jnp.maximum(i * hb - 1, 0), 0)),
            pl.BlockSpec((None, HALO, n), lambda bi, i: (bi, jnp.minimum((i + 1) * hb, last_h), 0)),
            pl.BlockSpec((None, ts, D_MODEL), lambda bi, i: (bi, i, 0)),
            _mod_spec(stream, layer),
            pl.BlockSpec(conv_w.shape, lambda bi, i: (0, 0)),
            pl.BlockSpec(sgu_g.shape, lambda bi, i: (0, 0)),
            pl.BlockSpec(sgu_w.shape, lambda bi, i: (0, 0, 0)),
            pl.BlockSpec(sgu_bias.shape, lambda bi, i: (0, 0)),
            pl.BlockSpec(w_out.shape, lambda bi, i: (0, 0)),
        ],
        out_specs=pl.BlockSpec((None, ts, D_MODEL), lambda bi, i: (bi, i, 0)),
        out_shape=jax.ShapeDtypeStruct((b, s, D_MODEL), F32),
        scratch_shapes=[pltpu.VMEM((ts, D_A + D_B), BF16)],
        compiler_params=_cparams("parallel", "parallel"),
        name="even_mix",
    )(z3, z3, z3, x3, mod, conv_w, sgu_g, sgu_w, sgu_bias, w_out)


ROUTE_TM = 256
ROUTE_SUBS = 2
ROUTE_STEP = ROUTE_TM * ROUTE_SUBS
ROUTE_PAD = 8
SORT_ROWS = ROUTE_TM + LANES
RUN_ROWS = SORT_ROWS + 32
XS_W = D_MODEL + LANES
GATE_LO = EXPERTS_PER_GROUP
DLOC_HI = 2 * EXPERTS_PER_GROUP
DLOC_RADIX = 16.0
FFN_BM = 512
FFN_HALF = FFN_BM // 2
RUN_SIZES = (256, 128, 64, 32, 16, 8)
TAB_W = 2 * N_GROUPS_MOE


def _round_up(x, m):
    return lax.div(x + (m - 1), m) * m


def _run_copies(tab_ref, tile, hbm_ref, vmem_ref, sem, to_hbm, wait):
    off = 0
    for g in range(N_GROUPS_MOE):
        start = tab_ref[tile * TAB_W + g]
        n = tab_ref[tile * TAB_W + N_GROUPS_MOE + g]
        for p in RUN_SIZES:
            done = n & (-2 * p)

            @pl.when((n & p) != 0)
            def _():
                v = vmem_ref.at[pl.ds(pl.multiple_of(off + done, ROUTE_PAD), p)]
                h = hbm_ref.at[pl.ds(pl.multiple_of(start + done, ROUTE_PAD), p)]
                cp = pltpu.make_async_copy(v, h, sem) if to_hbm else pltpu.make_async_copy(h, v, sem)
                if wait:
                    cp.wait()
                else:
                    cp.start()
        off = off + n


def _zero_fill(tab_ref, meta, zeros_ref, hbm_ref, sem, n_rows, wait):
    def copy(rows, dst_row):
        cp = pltpu.make_async_copy(zeros_ref.at[pl.ds(0, rows)],
                                   hbm_ref.at[pl.ds(pl.multiple_of(dst_row, ROUTE_PAD), rows)], sem)
        if wait:
            cp.wait()
        else:
            cp.start()

    end = 0
    for g in range(N_GROUPS_MOE):
        fill = tab_ref[meta + g]
        start = tab_ref[meta + N_GROUPS_MOE + g]
        end = start + _round_up(fill, FFN_BM)
        tail = end - start - fill
        for p in RUN_SIZES:
            pl.when((tail & p) != 0)(functools.partial(copy, p, start + fill + (tail & (-2 * p))))
    for k in range(n_rows // FFN_BM):
        pl.when(end + k * FFN_BM < n_rows)(functools.partial(copy, FFN_BM, end + k * FFN_BM))


def _wait_rows(n, hbm_ref, vmem_ref, sem, to_hbm):
    for p in RUN_SIZES:
        @pl.when((n & p) != 0)
        def _():
            v = vmem_ref.at[pl.ds(0, p)]
            h = hbm_ref.at[pl.ds(0, p)]
            (pltpu.make_async_copy(v, h, sem) if to_hbm else pltpu.make_async_copy(h, v, sem)).wait()


def _tile_rows(tab_ref, tile):
    n = 0
    for g in range(N_GROUPS_MOE):
        n = n + tab_ref[tile * TAB_W + N_GROUPS_MOE + g]
    return n


def _max4(v):
    return jnp.maximum(jnp.maximum(v[0], v[1]), jnp.maximum(v[2], v[3]))


def _first_of4(v, top):
    return jnp.where(v[0] == top, 0.0, jnp.where(v[1] == top, 1.0, jnp.where(v[2] == top, 2.0, 3.0)))


def _route_kernel(xp_ref, xs_ref, mod_ref, g_ref, wrt_ref, brt_ref, upper_ref,
                  dloc_ref, tab_ref, sorted_hbm, hbuf_ref, drow_ref, sorted_ref, zeros_ref, cnt_ref, fill_ref,
                  sem_ref, zsem_ref, *, n_steps, n_p_steps):
    sweep = pl.program_id(0)
    i = pl.program_id(1)
    tm = ROUTE_TM
    meta = n_steps * ROUTE_SUBS * TAB_W
    n_rows = sorted_hbm.shape[0]
    ng = N_GROUPS_MOE

    @pl.when(sweep == 0)
    def _():
        @pl.when(i == 0)
        def _():
            for g in range(ng):
                fill_ref[g] = 0

        m = mod_ref[...]
        x = jnp.where(i < n_p_steps, xp_ref[...], xs_ref[...])
        h = _rms(x, g_ref[...]) * (1.0 + m[4:5]) + m[3:4]
        hb = h.astype(BF16)
        lt = lax.dot_general(wrt_ref[...], hb, (((1,), (1,)), ((), ())), preferred_element_type=F32) + brt_ref[...]
        gl = [lt[r:r + 1, :] for r in range(ng)]
        g_top = _max4(gl)
        g_idx = _first_of4(gl, g_top)
        g_w = 1.0 / (jnp.exp(gl[0] - g_top) + jnp.exp(gl[1] - g_top) + jnp.exp(gl[2] - g_top) + jnp.exp(gl[3] - g_top))
        ev = []
        for k in range(EXPERTS_PER_GROUP):
            cand = [lt[GATE_OFF + EXPERTS_PER_GROUP * r + k:GATE_OFF + EXPERTS_PER_GROUP * r + k + 1, :]
                    for r in range(ng)]
            ev.append(jnp.where(g_idx == 0.0, cand[0], jnp.where(g_idx == 1.0, cand[1],
                                jnp.where(g_idx == 2.0, cand[2], cand[3]))))
        v1 = _max4(ev)
        i1 = _first_of4(ev, v1)
        rest = [jnp.where(i1 == float(k), NEG_BIG, ev[k]) for k in range(EXPERTS_PER_GROUP)]
        v2 = _max4(rest)
        i2 = _first_of4(rest, v2)
        e2 = jnp.exp(v2 - v1)
        w1 = 1.0 / (1.0 + e2)
        w2 = e2 * w1
        gates = [g_w * (jnp.where(i1 == float(k), w1, 0.0) + jnp.where(i2 == float(k), w2, 0.0))
                 for k in range(EXPERTS_PER_GROUP)]

        sub8 = lax.broadcasted_iota(jnp.int32, (8, tm), 0).astype(F32)
        dlocs = []
        for sub in range(ROUTE_SUBS):
            gi = g_idx[:, sub * tm:(sub + 1) * tm]
            hot = jnp.where(sub8 == gi, 1.0, 0.0)
            before = _bdot(hot.astype(BF16), upper_ref[...])
            dl = jnp.sum(before * hot, axis=0, keepdims=True)
            off = 0
            for g in range(ng):
                n_g = _round_up(jnp.sum(hot[g:g + 1, :]).astype(jnp.int32), ROUTE_PAD)
                cnt_ref[(i * ROUTE_SUBS + sub) * ng + g] = n_g
                fill_ref[g] = fill_ref[g] + n_g
                dl = dl + jnp.where(gi == float(g), off.astype(F32) if g else 0.0, 0.0)
                off = off + n_g
            drow_ref[i * ROUTE_SUBS + sub] = jnp.broadcast_to(dl, (8, tm))
            dlocs.append(dl)
        dloc = jnp.concatenate(dlocs, axis=1)
        d_hi = jnp.floor(dloc * (1.0 / DLOC_RADIX))
        g_hi = [gt.astype(BF16).astype(F32) for gt in gates]
        ex_rows = g_hi + [gt - gh for gt, gh in zip(gates, g_hi)] + [d_hi, dloc - DLOC_RADIX * d_hi]
        sub16 = lax.broadcasted_iota(jnp.int32, (16, ROUTE_STEP), 0)
        ex_t = jnp.zeros((16, ROUTE_STEP), F32)
        for r, row in enumerate(ex_rows):
            ex_t = jnp.where(sub16 == r, row, ex_t)
        ex_t = jnp.concatenate([ex_t, jnp.zeros((LANES - 16, ROUTE_STEP), F32)], axis=0)
        extras = ex_t.T
        dloc_ref[...] = jnp.broadcast_to(
            DLOC_RADIX * extras[:, DLOC_HI:DLOC_HI + 1] + extras[:, DLOC_HI + 1:DLOC_HI + 2], (ROUTE_STEP, LANES))
        hbuf_ref[pl.ds(pl.multiple_of(i * ROUTE_STEP, ROUTE_STEP), ROUTE_STEP), :] = (
            jnp.concatenate([hb, extras.astype(BF16)], axis=1))

    @pl.when(sweep == 1)
    def _():
        @pl.when(i == 0)
        def _():
            start = 0
            for g in range(ng):
                tab_ref[meta + g] = fill_ref[g]
                tab_ref[meta + ng + g] = start
                start = start + _round_up(fill_ref[g], FFN_BM)
                fill_ref[g] = 0
            zeros_ref[...] = jnp.zeros_like(zeros_ref)

        row_f = lax.broadcasted_iota(jnp.int32, (SORT_ROWS, tm), 0).astype(F32)
        for sub in range(ROUTE_SUBS):
            tile = i * ROUTE_SUBS + sub

            @pl.when(i >= 1)
            def _():
                _wait_rows(_tile_rows(tab_ref, tile - ROUTE_SUBS), sorted_hbm, sorted_ref.at[sub], sem_ref.at[sub], True)

            rows = hbuf_ref[pl.ds(pl.multiple_of(i * ROUTE_STEP + sub * tm, tm), tm), :]
            onehot = jnp.where(row_f == drow_ref[tile][0:1, :], 1.0, 0.0).astype(BF16)
            sorted_ref[sub] = _bdot(onehot, rows)
            for g in range(ng):
                n_g = cnt_ref[tile * ng + g]
                tab_ref[tile * TAB_W + g] = tab_ref[meta + ng + g] + fill_ref[g]
                tab_ref[tile * TAB_W + ng + g] = n_g
                fill_ref[g] = fill_ref[g] + n_g
            _run_copies(tab_ref, tile, sorted_hbm, sorted_ref.at[sub], sem_ref.at[sub], to_hbm=True, wait=False)

            @pl.when(i == n_steps - 1)
            def _():
                _wait_rows(_tile_rows(tab_ref, tile), sorted_hbm, sorted_ref.at[sub], sem_ref.at[sub], True)

        @pl.when(i == n_steps - 1)
        def _():
            _zero_fill(tab_ref, meta, zeros_ref, sorted_hbm, zsem_ref, n_rows, wait=False)
            _zero_fill(tab_ref, meta, zeros_ref, sorted_hbm, zsem_ref, n_rows, wait=True)


def _merged_specs(n_p_steps, n_s_steps, layer, sample_seq, step_of):
    def p_map(*idx):
        return (jnp.minimum(step_of(*idx), n_p_steps - 1), 0)

    def s_map(*idx):
        return (jnp.clip(step_of(*idx) - n_p_steps, 0, n_s_steps - 1), 0)

    def mod_map(*idx):
        j = step_of(*idx)
        row = jnp.where(j < n_p_steps, 0, 1 + lax.div(jnp.maximum(j - n_p_steps, 0) * ROUTE_STEP, sample_seq))
        return (layer, row, 0, 0)

    return (pl.BlockSpec((ROUTE_STEP, D_MODEL), p_map), pl.BlockSpec((ROUTE_STEP, D_MODEL), s_map),
            pl.BlockSpec((None, None, 6, D_MODEL), mod_map))


def _route(xp, xs, mod, g2, w_rt, b_rt, layer, sample_seq, n_rows):
    n_p_steps = xp.shape[0] // ROUTE_STEP
    n_s_steps = xs.shape[0] // ROUTE_STEP
    n_steps = n_p_steps + n_s_steps
    t = n_steps * ROUTE_STEP
    tm = ROUTE_TM
    upper = jnp.asarray(np.triu(np.ones((tm, tm), np.float32), 1), BF16)
    step_of = lambda s, i: jnp.where(s == 0, i, n_steps - 1)
    p_spec, s_spec, mod_spec = _merged_specs(n_p_steps, n_s_steps, layer, sample_seq, step_of)
    const = lambda a: pl.BlockSpec(a.shape, lambda s, i: (0,) * a.ndim)
    n_tiles = n_steps * ROUTE_SUBS
    return pl.pallas_call(
        functools.partial(_route_kernel, n_steps=n_steps, n_p_steps=n_p_steps),
        grid=(2, n_steps),
        in_specs=[p_spec, s_spec, mod_spec, const(g2), const(w_rt), const(b_rt), const(upper)],
        out_specs=[
            pl.BlockSpec((ROUTE_STEP, LANES), lambda s, i: (step_of(s, i), 0)),
            pl.BlockSpec(memory_space=pltpu.SMEM),
            pl.BlockSpec(memory_space=pl.ANY),
        ],
        out_shape=[
            jax.ShapeDtypeStruct((t, LANES), F32),
            jax.ShapeDtypeStruct(((n_tiles + 1) * TAB_W,), jnp.int32),
            jax.ShapeDtypeStruct((n_rows, XS_W), F32),
        ],
        scratch_shapes=[
            pltpu.VMEM((t, XS_W), BF16),
            pltpu.VMEM((n_tiles, 8, tm), F32),
            pltpu.VMEM((ROUTE_SUBS, SORT_ROWS, XS_W), F32),
            pltpu.VMEM((FFN_BM, XS_W), F32),
            pltpu.SMEM((n_tiles * N_GROUPS_MOE,), jnp.int32),
            pltpu.SMEM((N_GROUPS_MOE,), jnp.int32),
            pltpu.SemaphoreType.DMA((ROUTE_SUBS,)),
            pltpu.SemaphoreType.DMA(()),
        ],
        compiler_params=_cparams("arbitrary", "arbitrary"),
        name="moe_route",
    )(xp, xs, mod, g2, w_rt, b_rt, upper)


def _ffn_lookup(i, tab_ref, meta):
    fills = [tab_ref[meta + g] for g in range(N_GROUPS_MOE)]
    edges = []
    acc = 0
    for f in fills:
        acc = acc + lax.div(f + (FFN_BM - 1), FFN_BM)
        edges.append(acc)
    total = edges[-1]
    ii = jnp.minimum(i, total - 1)
    grp = sum((ii >= e).astype(jnp.int32) for e in edges[:-1])

    def pick(vals):
        return jnp.where(grp == 0, vals[0], jnp.where(grp == 1, vals[1], jnp.where(grp == 2, vals[2], vals[3])))

    first = pick([0] + edges[:-1])
    return grp, total, ii == first, pick(fills) - (ii - first) * FFN_BM


def _ffn_group_kernel(tab_ref, xs_ref, w1_ref, w3_ref, w2_ref, ys_ref, w1b_ref, w3b_ref, w2b_ref, *, meta):
    i = pl.program_id(0)
    _, total, first_of_group, valid = _ffn_lookup(i, tab_ref, meta)
    active = i < total

    @pl.when(jnp.logical_and(active, first_of_group))
    def _():
        w1b_ref[...] = w1_ref[...].astype(BF16)
        w3b_ref[...] = w3_ref[...].astype(BF16)
        w2b_ref[...] = w2_ref[...].astype(BF16)

    def run(rows):
        hb = xs_ref[0:rows, 0:D_MODEL].astype(BF16)
        ex = xs_ref[0:rows, D_MODEL:XS_W]
        acc = None
        for e in range(EXPERTS_PER_GROUP):
            a = _bdot(hb, w1b_ref[e])
            b = _bdot(hb, w3b_ref[e])
            gate = ex[:, e:e + 1] + ex[:, GATE_LO + e:GATE_LO + e + 1]
            hid = (a * jax.nn.sigmoid(a)) * b * gate
            part = _bdot(hid.astype(BF16), w2b_ref[e])
            acc = part if acc is None else acc + part
        ys_ref[0:rows, :] = acc

    @pl.when(jnp.logical_and(active, valid > FFN_HALF))
    def _():
        run(FFN_BM)

    @pl.when(jnp.logical_and(active, valid <= FFN_HALF))
    def _():
        run(FFN_HALF)
        ys_ref[FFN_HALF:, :] = jnp.zeros((FFN_BM - FFN_HALF, D_MODEL), F32)

    @pl.when(jnp.logical_not(active))
    def _():
        ys_ref[...] = jnp.zeros_like(ys_ref)


def _ffn_group(tab, xs, w1, w3, w2, layer, meta):
    e4 = EXPERTS_PER_GROUP
    n_blocks = xs.shape[0] // FFN_BM
    group_of = lambda i, tab_ref: _ffn_lookup(i, tab_ref, meta)[0]
    grid_spec = pltpu.PrefetchScalarGridSpec(
        num_scalar_prefetch=1,
        grid=(n_blocks,),
        in_specs=[
            pl.BlockSpec((FFN_BM, XS_W), lambda i, tab_ref: (i, 0)),
            pl.BlockSpec((None, e4, D_MODEL, D_EXPERT), lambda i, tab_ref: (layer, group_of(i, tab_ref), 0, 0)),
            pl.BlockSpec((None, e4, D_MODEL, D_EXPERT), lambda i, tab_ref: (layer, group_of(i, tab_ref), 0, 0)),
            pl.BlockSpec((None, e4, D_EXPERT, D_MODEL), lambda i, tab_ref: (layer, group_of(i, tab_ref), 0, 0)),
        ],
        out_specs=pl.BlockSpec((FFN_BM, D_MODEL), lambda i, tab_ref: (i, 0)),
        scratch_shapes=[
            pltpu.VMEM((e4, D_MODEL, D_EXPERT), BF16),
            pltpu.VMEM((e4, D_MODEL, D_EXPERT), BF16),
            pltpu.VMEM((e4, D_EXPERT, D_MODEL), BF16),
        ],
    )
    return pl.pallas_call(
        functools.partial(_ffn_group_kernel, meta=meta),
        grid_spec=grid_spec,
        out_shape=jax.ShapeDtypeStruct((xs.shape[0], D_MODEL), F32),
        compiler_params=_cparams("arbitrary"),
        name="moe_ffn",
    )(tab, xs, w1, w3, w2)


def _combine_kernel(tab_ref, xp_ref, xs_ref, mod_ref, dloc_ref, fg_ref, ys_ref, op_ref, os_ref, runs_ref, sem_ref,
                    *, n_steps, n_p_steps, final_norm):
    i = pl.program_id(0)
    par = lax.rem(i, 2)
    tm = ROUTE_TM

    def copies(step, parity, wait):
        for sub in range(ROUTE_SUBS):
            _run_copies(tab_ref, tile=step * ROUTE_SUBS + sub, hbm_ref=ys_ref, vmem_ref=runs_ref.at[parity, sub],
                        sem=sem_ref.at[parity, sub], to_hbm=False, wait=wait)

    @pl.when(i == 0)
    def _():
        copies(0, 0, wait=False)

    @pl.when(i + 1 < n_steps)
    def _():
        copies(i + 1, 1 - par, wait=False)

    for sub in range(ROUTE_SUBS):
        _wait_rows(_tile_rows(tab_ref, i * ROUTE_SUBS + sub), ys_ref, runs_ref.at[par, sub], sem_ref.at[par, sub], False)
    row_f = lax.broadcasted_iota(jnp.int32, (tm, SORT_ROWS), 1).astype(F32)
    parts = []
    for sub in range(ROUTE_SUBS):
        tile = i * ROUTE_SUBS + sub
        covered = 0
        for g in range(N_GROUPS_MOE):
            covered = covered + tab_ref[tile * TAB_W + N_GROUPS_MOE + g]
        runs_ref[par, sub, pl.ds(pl.multiple_of(covered, ROUTE_PAD), LANES), :] = jnp.zeros((LANES, D_MODEL), F32)
        yb = runs_ref[par, sub, 0:SORT_ROWS, :].astype(BF16)
        pt = jnp.where(row_f == dloc_ref[sub * tm:(sub + 1) * tm, 0:1], 1.0, 0.0).astype(BF16)
        parts.append(_bdot(pt, yb))
    is_prompt = i < n_p_steps
    x = jnp.where(is_prompt, xp_ref[...], xs_ref[...])
    x2 = x + mod_ref[5:6, :] * jnp.concatenate(parts, axis=0)
    if final_norm:
        x2 = _rms(x2, fg_ref[...])

    @pl.when(is_prompt)
    def _():
        op_ref[...] = x2

    @pl.when(jnp.logical_not(is_prompt))
    def _():
        os_ref[...] = x2


def _combine(tab, xp, xs, mod, dloc, final_g, ys, layer, sample_seq, final_norm):
    n_p_steps = xp.shape[0] // ROUTE_STEP
    n_s_steps = xs.shape[0] // ROUTE_STEP
    n_steps = n_p_steps + n_s_steps
    step_of = lambda i, tab_ref: i
    p_spec, s_spec, mod_spec = _merged_specs(n_p_steps, n_s_steps, layer, sample_seq, step_of)
    grid_spec = pltpu.PrefetchScalarGridSpec(
        num_scalar_prefetch=1,
        grid=(n_steps,),
        in_specs=[
            p_spec, s_spec, mod_spec,
            pl.BlockSpec((ROUTE_STEP, LANES), lambda i, tab_ref: (i, 0)),
            pl.BlockSpec((1, D_MODEL), lambda i, tab_ref: (0, 0)),
            pl.BlockSpec(memory_space=pl.ANY),
        ],
        out_specs=[p_spec, s_spec],
        scratch_shapes=[
            pltpu.VMEM((2, ROUTE_SUBS, RUN_ROWS, D_MODEL), F32),
            pltpu.SemaphoreType.DMA((2, ROUTE_SUBS)),
        ],
    )
    return pl.pallas_call(
        functools.partial(_combine_kernel, n_steps=n_steps, n_p_steps=n_p_steps, final_norm=final_norm),
        grid_spec=grid_spec,
        out_shape=[jax.ShapeDtypeStruct(xp.shape, F32), jax.ShapeDtypeStruct(xs.shape, F32)],
        compiler_params=_cparams("arbitrary"),
        name="moe_combine",
    )(tab, xp, xs, mod, dloc, final_g, ys)


def _moe(xp, xs, mod, g2, w_r, b_r, w1, w3, w2, final_g, layer, sample_seq, final_norm):
    t = xp.shape[0] + xs.shape[0]
    n_tiles = t // ROUTE_TM
    max_rows = t + N_GROUPS_MOE * (ROUTE_PAD - 1) * n_tiles
    n_rows = (-(-max_rows // FFN_BM) + N_GROUPS_MOE) * FFN_BM
    dloc, tab, sorted_x = _route(xp, xs, mod, g2, w_r, b_r, layer, sample_seq, n_rows)
    ys = _ffn_group(tab, sorted_x, w1, w3, w2, layer, n_tiles * TAB_W)
    return _combine(tab, xp, xs, mod, dloc, final_g, ys, layer, sample_seq, final_norm)


def _rope_tables(seq):
    half = QK_ROPE // 2
    nf = half // 2
    inv = ROPE_BASE ** (-np.arange(nf, dtype=np.float64) / nf)
    pos = np.arange(seq)
    row = (pos // GRID_W).astype(np.float64)
    col = (pos % GRID_W).astype(np.float64)
    cos = np.ones((seq, HEAD_PAD), np.float64)
    sin_a = np.zeros((seq, HEAD_PAD), np.float64)
    sin_b = np.zeros((seq, HEAD_PAD), np.float64)
    for part, p in enumerate((row, col)):
        ang = p[:, None] * inv[None, :]
        base = ROPE_OFF + part * half
        cos[:, base:base + nf] = np.cos(ang)
        cos[:, base + nf:base + half] = np.cos(ang)
        sin_a[:, base:base + nf] = -np.sin(ang)
        sin_b[:, base + nf:base + half] = np.sin(ang)
    return tuple(jnp.asarray(a, F32) for a in (cos, sin_a, sin_b))


def _apply_rope(x, cos, sin_a, sin_b, reps):
    nf = QK_ROPE // 4
    width = x.shape[1]
    if reps > 1:
        cos, sin_a, sin_b = (jnp.concatenate([a] * reps, axis=1) for a in (cos, sin_a, sin_b))
    return x * cos + pltpu.roll(x, width - nf, axis=1) * sin_a + pltpu.roll(x, nf, axis=1) * sin_b


def _odd_in_kernel(*refs, rope, emit_cache):
    x_ref, mod_ref, g_ref, w_ref, qg_ref, wq_ref, kg_ref, wk_ref, wv_ref, cs_ref = refs[:10]
    refs = refs[10:]
    if rope:
        cos_ref, sa_ref, sb_ref = refs[:3]
        refs = refs[3:]
    y_ref, q_ref, k_ref, v_ref = refs[:4]
    refs = refs[4:]
    m = mod_ref[...]
    h = _rms(x_ref[...], g_ref[...]) * (1.0 + m[1:2]) + m[0:1]
    z = _bdot(h.astype(BF16), w_ref[...])
    zc = z[:, 0:D_C]
    qc = z[:, D_C:D_C + Q_LORA]
    kvc = z[:, D_C + Q_LORA:D_C + Q_LORA + KV_LORA]
    kpe = z[:, D_C + Q_LORA + KV_LORA:]
    q = _bdot(_rms(qc, qg_ref[...]).astype(BF16), wq_ref[...])
    kvn = _rms(kvc, kg_ref[...])
    if emit_cache:
        ckv_ref, kpe_ref = refs
        ckv_ref[...] = kvn
        kpe_ref[...] = kpe[:, ROPE_OFF:ROPE_OFF + QK_ROPE]
    if rope:
        tabs = (cos_ref[...], sa_ref[...], sb_ref[...])
        q = _apply_rope(q, *tabs, reps=N_HEADS)
        kpe = _apply_rope(kpe, *tabs, reps=1)
    kvb = kvn.astype(BF16)
    k = _bdot(kvb, wk_ref[...]) + jnp.concatenate([kpe] * N_HEADS, axis=1)
    scale = 1.0 / math.sqrt(QK_NOPE + QK_ROPE)
    q_ref[...] = (q * scale).astype(BF16)
    k_ref[...] = k.astype(BF16)
    v_ref[...] = _bdot(kvb, wv_ref[...]).astype(BF16)
    y = _bdot(zc.astype(BF16), cs_ref[...])
    y_ref[0, :, :] = y[:, 0:D_C].astype(BF16)
    y_ref[1, :, :] = y[:, D_C:2 * D_C].astype(BF16)


def _odd_in(x3, mod, g, w_in, q_g, w_q, kv_g, w_k, w_v, cs, stream, layer, rope_tabs, emit_cache):
    b, s, _ = x3.shape
    tm = min(s, 512)
    n_i = s // tm
    rope = rope_tabs is not None
    const = lambda a: pl.BlockSpec(a.shape, lambda bi, i: (0,) * a.ndim)
    in_specs = [
        pl.BlockSpec((None, tm, D_MODEL), lambda bi, i: (bi, i, 0)),
        _mod_spec(stream, layer),
        const(g), const(w_in), const(q_g), const(w_q), const(kv_g), const(w_k), const(w_v), const(cs),
    ]
    args = [x3, mod, g, w_in, q_g, w_q, kv_g, w_k, w_v, cs]
    if rope:
        in_specs += [pl.BlockSpec((tm, HEAD_PAD), lambda bi, i: (i, 0))] * 3
        args += list(rope_tabs)
    hq = N_HEADS * HEAD_PAD
    out_specs = [
        pl.BlockSpec((None, 2, tm, D_C), lambda bi, i: (bi, 0, i, 0)),
        pl.BlockSpec((None, tm, hq), lambda bi, i: (bi, i, 0)),
        pl.BlockSpec((None, tm, hq), lambda bi, i: (bi, i, 0)),
        pl.BlockSpec((None, tm, N_HEADS * V_DIM), lambda bi, i: (bi, i, 0)),
    ]
    out_shape = [
        jax.ShapeDtypeStruct((b, 2, s, D_C), BF16),
        jax.ShapeDtypeStruct((b, s, hq), BF16),
        jax.ShapeDtypeStruct((b, s, hq), BF16),
        jax.ShapeDtypeStruct((b, s, N_HEADS * V_DIM), BF16),
    ]
    if emit_cache:
        out_specs += [
            pl.BlockSpec((None, tm, KV_LORA), lambda bi, i: (bi, i, 0)),
            pl.BlockSpec((None, tm, QK_ROPE), lambda bi, i: (bi, i, 0)),
        ]
        out_shape += [
            jax.ShapeDtypeStruct((b, s, KV_LORA), F32),
            jax.ShapeDtypeStruct((b, s, QK_ROPE), F32),
        ]
    return pl.pallas_call(
        functools.partial(_odd_in_kernel, rope=rope, emit_cache=emit_cache),
        grid=(b, n_i),
        in_specs=in_specs,
        out_specs=out_specs,
        out_shape=out_shape,
        compiler_params=_cparams("parallel", "parallel"),
        name="odd_in",
    )(*args)


def _cache_kv_kernel(c_ref, p_ref, wk_ref, wv_ref, k_ref, v_ref):
    cb = c_ref[...].astype(BF16)
    k = _bdot(cb, wk_ref[...]) + jnp.concatenate([p_ref[...]] * N_HEADS, axis=1)
    k_ref[...] = k.astype(BF16)
    v_ref[...] = _bdot(cb, wv_ref[...]).astype(BF16)


def _cache_kv(ckv, kpe_blk, w_k, w_v):
    b, p, _ = ckv.shape
    hq = N_HEADS * HEAD_PAD
    return pl.pallas_call(
        _cache_kv_kernel,
        grid=(b,),
        in_specs=[
            pl.BlockSpec((None, p, KV_LORA), lambda bi: (bi, 0, 0)),
            pl.BlockSpec((None, p, HEAD_PAD), lambda bi: (bi, 0, 0)),
            pl.BlockSpec(w_k.shape, lambda bi: (0, 0)),
            pl.BlockSpec(w_v.shape, lambda bi: (0, 0)),
        ],
        out_specs=[
            pl.BlockSpec((None, p, hq), lambda bi: (bi, 0, 0)),
            pl.BlockSpec((None, p, N_HEADS * V_DIM), lambda bi: (bi, 0, 0)),
        ],
        out_shape=[
            jax.ShapeDtypeStruct((b, p, hq), BF16),
            jax.ShapeDtypeStruct((b, p, N_HEADS * V_DIM), BF16),
        ],
        compiler_params=_cparams("parallel"),
        name="cache_kv",
    )(ckv, kpe_blk, w_k, w_v)


def _odd_mix_kernel(*refs, with_cache):
    q_ref, k_ref, v_ref = refs[:3]
    refs = refs[3:]
    if with_cache:
        kc_ref, vc_ref = refs[:2]
        refs = refs[2:]
    y_ref, f_ref, x_ref, mod_ref, wo_ref, o_ref, a_ref = refs
    tq = q_ref.shape[0]
    qk_dims = (((1,), (1,)), ((), ()))
    lane = lax.broadcasted_iota(jnp.int32, (tq, 2 * V_DIM), 1)
    for pair in range(N_HEADS // 2):
        vcols = slice(pair * 2 * V_DIM, (pair + 1) * 2 * V_DIM)
        outs = []
        for h in (2 * pair, 2 * pair + 1):
            hcols = slice(h * HEAD_PAD, (h + 1) * HEAD_PAD)
            qh = q_ref[:, hcols]
            s = lax.dot_general(qh, k_ref[:, hcols], qk_dims, preferred_element_type=F32)
            top = jnp.max(s, axis=-1, keepdims=True)
            if with_cache:
                sc = lax.dot_general(qh, kc_ref[:, hcols], qk_dims, preferred_element_type=F32)
                top = jnp.maximum(top, jnp.max(sc, axis=-1, keepdims=True))
            p = jnp.exp(s - top)
            den = jnp.sum(p, axis=-1, keepdims=True)
            o = _bdot(p.astype(BF16), v_ref[:, vcols])
            if with_cache:
                pc = jnp.exp(sc - top)
                den = den + jnp.sum(pc, axis=-1, keepdims=True)
                o = o + _bdot(pc.astype(BF16), vc_ref[:, vcols])
            outs.append(o / den)
        a_ref[:, vcols] = jnp.where(lane < V_DIM, outs[0], outs[1]).astype(BF16)
    f = _bdot(f_ref[...], y_ref[...])
    o = _bdot(f.astype(BF16), wo_ref[0:D_C, :]) + _bdot(a_ref[...], wo_ref[D_C:, :])
    o_ref[...] = x_ref[...] + mod_ref[2:3, :] * o


def _odd_mix(q, k, v, kc, vc, y, fmat, x3, mod, w_out, stream, layer):
    b, s, hq = q.shape
    tq = min(s, 256)
    n_i = s // tq
    with_cache = kc is not None
    hv = N_HEADS * V_DIM
    def per_batch(rows, cols):
        return pl.BlockSpec((None, rows, cols), lambda bi, i: (bi, 0, 0), pipeline_mode=pl.Buffered(1))

    in_specs = [pl.BlockSpec((None, tq, hq), lambda bi, i: (bi, i, 0)), per_batch(s, hq), per_batch(s, hv)]
    args = [q, k, v]
    if with_cache:
        p = kc.shape[1]
        in_specs += [per_batch(p, hq), per_batch(p, hv)]
        args += [kc, vc]
    in_specs += [
        per_batch(2 * s, D_C),
        pl.BlockSpec((tq, 2 * s), lambda bi, i: (i, 0)),
        pl.BlockSpec((None, tq, D_MODEL), lambda bi, i: (bi, i, 0)),
        _mod_spec(stream, layer),
        pl.BlockSpec(w_out.shape, lambda bi, i: (0, 0), pipeline_mode=pl.Buffered(1)),
    ]
    args += [y, fmat, x3, mod, w_out]
    return pl.pallas_call(
        functools.partial(_odd_mix_kernel, with_cache=with_cache),
        grid=(b, n_i),
        in_specs=in_specs,
        out_specs=pl.BlockSpec((None, tq, D_MODEL), lambda bi, i: (bi, i, 0)),
        out_shape=jax.ShapeDtypeStruct((b, s, D_MODEL), F32),
        scratch_shapes=[pltpu.VMEM((tq, hv), BF16)],
        compiler_params=_cparams("parallel", "arbitrary"),
        name="odd_mix",
    )(*args)


def _dft_tables(seq):
    jc = np.arange(C_GW)
    ang_c = 2.0 * np.pi * np.outer(jc, jc) / C_GW
    eye = np.eye(C_GROUPS)
    cs = np.concatenate([np.kron(eye, np.cos(ang_c)), np.kron(eye, np.sin(ang_c))], axis=1)
    jn = np.arange(seq)
    ang_n = 2.0 * np.pi * (np.outer(jn, jn) % seq) / seq
    scale = 1.0 / math.sqrt(seq * C_GW)
    fmat = np.concatenate([np.cos(ang_n), -np.sin(ang_n)], axis=1) * scale
    return jnp.asarray(cs, F32).astype(BF16), jnp.asarray(fmat, F32).astype(BF16)


def _odd_weights(w_in, w_uq, w_ukv):
    d = w_in.shape[0]
    base = D_C + Q_LORA + KV_LORA
    kpe_blk = jnp.zeros((d, HEAD_PAD), w_in.dtype).at[:, ROPE_OFF:ROPE_OFF + QK_ROPE].set(w_in[:, base:])
    w_in_p = jnp.concatenate([w_in[:, :base], kpe_blk], axis=1).astype(BF16)
    qh = w_uq.reshape(Q_LORA, N_HEADS, QK_NOPE + QK_ROPE)
    w_q = jnp.pad(qh, ((0, 0), (0, 0), (0, HEAD_PAD - QK_NOPE - QK_ROPE))).reshape(Q_LORA, -1).astype(BF16)
    kvh = w_ukv.reshape(KV_LORA, N_HEADS, QK_NOPE + V_DIM)
    w_k = jnp.pad(kvh[:, :, :QK_NOPE], ((0, 0), (0, 0), (0, HEAD_PAD - QK_NOPE))).reshape(KV_LORA, -1)
    w_v = kvh[:, :, QK_NOPE:].reshape(KV_LORA, -1)
    return w_in_p, w_q, w_k.astype(BF16), w_v.astype(BF16)


ROUTER_ROWS = 32


def _router_weights(wg, bg, we, be):
    d = wg.shape[0]
    w = jnp.concatenate([wg, we.reshape(d, N_EXPERTS)], axis=1).T
    w = jnp.pad(w, ((0, ROUTER_ROWS - w.shape[0]), (0, 0))).astype(BF16)
    b = jnp.concatenate([bg, be.reshape(N_EXPERTS)])
    b = jnp.pad(b, (0, ROUTER_ROWS - b.shape[0])).reshape(ROUTER_ROWS, 1).astype(F32)
    return w, b


def kernel(x_prompt, x_sample, cache_ckv, cache_kpe, c, c_ctx, mod_w, mod_b, norm1_g, norm2_g,
           ev_w_in, ev_conv_w, ev_sgu_norm_g, ev_sgu_w, ev_sgu_b, ev_w_out,
           od_w_in, od_q_norm_g, od_w_uq, od_kv_norm_g, od_w_ukv, od_w_out,
           moe_wg, moe_bg, moe_we, moe_be, moe_w1, moe_w3, moe_w2, final_norm_g):
    bp, n_p, d = x_prompt.shape
    bs, n_s, _ = x_sample.shape
    streams = [(_Stream(bp, n_p, True), x_prompt), (_Stream(bs, n_s, False), x_sample)]

    n_rows = 1 + bs
    cond_t = jnp.concatenate([c_ctx[None, :], c], axis=0).T
    mod = _adaln(cond_t, mod_w, mod_b, n_rows)

    final_g = final_norm_g.reshape(1, d)
    xs = [x for _, x in streams]
    new_ckv, new_kpe = [], []
    for l in range(DEPTH):
        j = l // 2
        g1 = norm1_g[l].reshape(1, d)
        g2 = norm2_g[l].reshape(1, d)
        w_r, b_r = _router_weights(moe_wg[l], moe_bg[l], moe_we[l], moe_be[l])
        last = l == DEPTH - 1
        if l % 2 == 0:
            w_in = ev_w_in[j].astype(BF16)
            w_out = ev_w_out[j].astype(BF16)
            sgu_w = ev_sgu_w[j].astype(BF16)
            sgu_g = ev_sgu_norm_g[j].reshape(1, D_B)
            sgu_bias = jnp.repeat(ev_sgu_b[j].T, D_B // B_GROUPS, axis=1)
            for si, (st, _) in enumerate(streams):
                x3 = xs[si]
                z = _even_in(x3.reshape(st.tokens, d), mod, g1, w_in, st, l)
                xs[si] = _even_mix(z.reshape(st.batch, st.seq, -1), x3, mod, ev_conv_w[j], sgu_g, sgu_w,
                                   sgu_bias, w_out, st, l)
        else:
            w_in, w_q, w_k, w_v = _odd_weights(od_w_in[j], od_w_uq[j], od_w_ukv[j])
            w_out = od_w_out[j].astype(BF16)
            q_g = od_q_norm_g[j].reshape(1, Q_LORA)
            kv_g = od_kv_norm_g[j].reshape(1, KV_LORA)
            for si, (st, _) in enumerate(streams):
                x3 = xs[si]
                cs, fmat = _dft_tables(st.seq)
                is_prompt = st.shared_cond
                tabs = None if is_prompt else _rope_tables(st.seq)
                outs = _odd_in(x3, mod, g1, w_in, q_g, w_q, kv_g, w_k, w_v, cs, st, l, tabs, is_prompt)
                y, q, k, v = outs[:4]
                if is_prompt:
                    new_ckv.append(outs[4])
                    new_kpe.append(outs[5])
                    kc = vc = None
                else:
                    kpe_blk = jnp.pad(cache_kpe[:, j], ((0, 0), (0, 0), (ROPE_OFF, HEAD_PAD - ROPE_OFF - QK_ROPE)))
                    kc, vc = _cache_kv(cache_ckv[:, j], kpe_blk, w_k, w_v)
                xs[si] = _odd_mix(q, k, v, kc, vc, y.reshape(st.batch, 2 * st.seq, D_C), fmat, x3, mod,
                                  w_out, st, l)
        x2p, x2s = _moe(xs[0].reshape(bp * n_p, d), xs[1].reshape(bs * n_s, d), mod, g2, w_r, b_r,
                        moe_w1, moe_w3, moe_w2, final_g, l, n_s, last)
        xs = [x2p.reshape(bp, n_p, d), x2s.reshape(bs, n_s, d)]
    return (xs[0], xs[1], jnp.stack(new_ckv, axis=1), jnp.stack(new_kpe, axis=1))
```

```python
import functools
import math

import numpy as np
import jax
import jax.numpy as jnp
from jax import lax
from jax.experimental import pallas as pl
from jax.experimental.pallas import tpu as pltpu

D_MODEL = 1024
DEPTH = 2
GRID_W = 64
D_A = D_MODEL // 2
D_B = D_MODEL // 2
B_GROUPS = 4
CHUNK = 128
D_EVEN_IN = 3 * D_A + 2 * D_B
D_C = D_MODEL // 4
C_GROUPS = 4
C_GW = D_C // C_GROUPS
N_HEADS = 12
QK_NOPE = 64
QK_ROPE = 32
V_DIM = 64
Q_LORA = 384
KV_LORA = 256
ROPE_BASE = 10000.0
N_GROUPS_MOE = 4
EXPERTS_PER_GROUP = 4
N_EXPERTS = N_GROUPS_MOE * EXPERTS_PER_GROUP
D_EXPERT = 256
EPS = 1e-6

LANES = 128
HEAD_PAD = 128
ROPE_OFF = QK_NOPE
GATE_OFF = N_GROUPS_MOE
NEG_BIG = -1e30
F32 = jnp.float32
BF16 = jnp.bfloat16
VMEM_LIMIT = 56 * 1024 * 1024


def _cparams(*sem):
    return pltpu.CompilerParams(dimension_semantics=sem, vmem_limit_bytes=VMEM_LIMIT)


def _rms(x, g):
    return x * lax.rsqrt(jnp.mean(x * x, axis=-1, keepdims=True) + EPS) * g


def _bdot(a, b):
    return jnp.dot(a, b, preferred_element_type=F32)


NT_DIMS = (((1,), (1,)), ((), ()))
TN_DIMS = (((0,), (0,)), ((), ()))


def _mod_kernel(ct_ref, w_ref, b_ref, o_ref):
    c = ct_ref[...]
    s = c * jax.nn.sigmoid(c)
    w = w_ref[...]
    b = b_ref[...]
    for r in range(o_ref.shape[0]):
        o_ref[r:r + 1, :] = jnp.sum(s[:, r:r + 1] * w, axis=0, keepdims=True) + b


def _adaln(cond_t, mod_w, mod_b, n_rows):
    nt = 1024
    d6 = mod_w.shape[-1]
    out = pl.pallas_call(
        _mod_kernel,
        grid=(DEPTH, d6 // nt),
        in_specs=[
            pl.BlockSpec(cond_t.shape, lambda l, n: (0, 0)),
            pl.BlockSpec((None, D_MODEL, nt), lambda l, n: (l, 0, n)),
            pl.BlockSpec((None, 1, nt), lambda l, n: (l, 0, n)),
        ],
        out_specs=pl.BlockSpec((None, n_rows, nt), lambda l, n: (l, 0, n)),
        out_shape=jax.ShapeDtypeStruct((DEPTH, n_rows, d6), F32),
        compiler_params=_cparams("parallel", "parallel"),
        name="adaln",
    )(cond_t, mod_w, mod_b.reshape(DEPTH, 1, d6))
    return out.reshape(DEPTH, n_rows, 6, D_MODEL)


class _Stream:
    def __init__(self, batch, seq, shared_cond):
        self.batch = batch
        self.seq = seq
        self.tokens = batch * seq
        self.shared_cond = shared_cond

    def row_of_batch(self, b):
        return 0 if self.shared_cond else b + 1

    def row_of_tile(self, i, tm):
        return 0 if self.shared_cond else (i * tm) // self.seq + 1


def _mod_spec(stream, layer, tm=None):
    if tm is None:
        return pl.BlockSpec((None, None, 6, D_MODEL), lambda b, i: (layer, stream.row_of_batch(b), 0, 0))
    return pl.BlockSpec((None, None, 6, D_MODEL), lambda i, *_: (layer, stream.row_of_tile(i, tm), 0, 0))


def _even_in_kernel(x_ref, mod_ref, g_ref, w_ref, z_ref, *, nc):
    m = mod_ref[...]
    h = _rms(x_ref[...], g_ref[...]) * (1.0 + m[1:2]) + m[0:1]
    hb = h.astype(BF16)
    for n in range(z_ref.shape[1] // nc):
        z_ref[:, n * nc:(n + 1) * nc] = _bdot(hb, w_ref[:, n * nc:(n + 1) * nc]).astype(z_ref.dtype)


def _even_in(x2d, mod, g, w_in, stream, layer, tm=512):
    t = x2d.shape[0]
    n = w_in.shape[1]
    return pl.pallas_call(
        functools.partial(_even_in_kernel, nc=512),
        grid=(t // tm,),
        in_specs=[
            pl.BlockSpec((tm, D_MODEL), lambda i: (i, 0)),
            _mod_spec(stream, layer, tm),
            pl.BlockSpec((1, D_MODEL), lambda i: (0, 0)),
            pl.BlockSpec((D_MODEL, n), lambda i: (0, 0)),
        ],
        out_specs=pl.BlockSpec((tm, n), lambda i: (i, 0)),
        out_shape=jax.ShapeDtypeStruct((t, n), BF16),
        compiler_params=_cparams("parallel"),
        name="even_in",
    )(x2d, mod, g, w_in)


HALO = 16


def _even_mix_kernel(z_ref, zp_ref, zn_ref, x_ref, mod_ref, cw_ref, sg_ref, sw_ref, sb_ref, wo_ref,
                     o_ref, y_ref):
    i = pl.program_id(1)
    n_i = pl.num_programs(1)
    ts = z_ref.shape[0]
    gate_b = z_ref[:, 0:D_A].astype(F32)
    gate_c = z_ref[:, D_A:2 * D_A].astype(F32)
    xa = z_ref[:, 2 * D_A:3 * D_A].astype(F32)
    t = gate_c * xa
    zp = zp_ref[...].astype(F32)[HALO - 1:HALO, :]
    zn = zn_ref[...].astype(F32)[0:1, :]
    tp = zp[:, D_A:2 * D_A] * zp[:, 2 * D_A:3 * D_A] * (i > 0).astype(F32)
    tn = zn[:, D_A:2 * D_A] * zn[:, 2 * D_A:3 * D_A] * (i < n_i - 1).astype(F32)
    row = lax.broadcasted_iota(jnp.int32, (ts, 1), 0)
    t_prev = jnp.where(row == 0, tp, pltpu.roll(t, 1, axis=0))
    t_next = jnp.where(row == ts - 1, tn, pltpu.roll(t, ts - 1, axis=0))
    cw = cw_ref[...]
    y_a = gate_b * (t_prev * cw[0:1] + t * cw[1:2] + t_next * cw[2:3])
    y_ref[:, 0:D_A] = y_a.astype(BF16)

    u = z_ref[:, 3 * D_A:3 * D_A + D_B].astype(F32)
    v = z_ref[:, 3 * D_A + D_B:3 * D_A + 2 * D_B].astype(F32)
    vb = _rms(v, sg_ref[...]).astype(BF16)
    gw = D_B // B_GROUPS
    for c in range(ts // CHUNK):
        rows = slice(c * CHUNK, (c + 1) * CHUNK)
        for g in range(B_GROUPS):
            cols = slice(g * gw, (g + 1) * gw)
            sv = _bdot(sw_ref[g], vb[rows, cols]) + sb_ref[:, cols]
            y_ref[rows, D_A + g * gw:D_A + (g + 1) * gw] = (u[rows, cols] * sv).astype(BF16)

    o = _bdot(y_ref[...], wo_ref[...])
    o_ref[...] = x_ref[...] + mod_ref[2:3, :] * o


def _even_mix(z3, x3, mod, conv_w, sgu_g, sgu_w, sgu_bias, w_out, stream, layer):
    b, s, n = z3.shape
    ts = min(s, 256)
    n_i = s // ts
    hb = ts // HALO
    last_h = s // HALO - 1
    return pl.pallas_call(
        _even_mix_kernel,
        grid=(b, n_i),
        in_specs=[
            pl.BlockSpec((None, ts, n), lambda bi, i: (bi, i, 0)),
            pl.BlockSpec((None, HALO, n), lambda bi, i: (bi, jnp.maximum(i * hb - 1, 0), 0)),
            pl.BlockSpec((None, HALO, n), lambda bi, i: (bi, jnp.minimum((i + 1) * hb, last_h), 0)),
            pl.BlockSpec((None, ts, D_MODEL), lambda bi, i: (bi, i, 0)),
            _mod_spec(stream, layer),
            pl.BlockSpec(conv_w.shape, lambda bi, i: (0, 0)),
            pl.BlockSpec(sgu_g.shape, lambda bi, i: (0, 0)),
            pl.BlockSpec(sgu_w.shape, lambda bi, i: (0, 0, 0)),
            pl.BlockSpec(sgu_bias.shape, lambda bi, i: (0, 0)),
            pl.BlockSpec(w_out.shape, lambda bi, i: (0, 0)),
        ],
        out_specs=pl.BlockSpec((None, ts, D_MODEL), lambda bi, i: (bi, i, 0)),
        out_shape=jax.ShapeDtypeStruct((b, s, D_MODEL), F32),
        scratch_shapes=[pltpu.VMEM((ts, D_A + D_B), BF16)],
        compiler_params=_cparams("parallel", "parallel"),
        name="even_mix",
    )(z3, z3, z3, x3, mod, conv_w, sgu_g, sgu_w, sgu_bias, w_out)


ROUTE_TM = 256
ROUTE_SUBS = 2
ROUTE_STEP = ROUTE_TM * ROUTE_SUBS
ROUTE_PAD = 8
SORT_ROWS = ROUTE_TM + LANES
RUN_ROWS = SORT_ROWS + 32
XS_W = D_MODEL + LANES
GATE_LO = EXPERTS_PER_GROUP
DLOC_HI = 2 * EXPERTS_PER_GROUP
DLOC_RADIX = 16.0
FFN_BM = 512
FFN_HALF = FFN_BM // 2
RUN_SIZES = (256, 128, 64, 32, 16, 8)
TAB_W = 2 * N_GROUPS_MOE


def _round_up(x, m):
    return lax.div(x + (m - 1), m) * m


def _run_copies(tab_ref, tile, hbm_ref, vmem_ref, sem, to_hbm, wait):
    off = 0
    for g in range(N_GROUPS_MOE):
        start = tab_ref[tile * TAB_W + g]
        n = tab_ref[tile * TAB_W + N_GROUPS_MOE + g]
        for p in RUN_SIZES:
            done = n & (-2 * p)

            @pl.when((n & p) != 0)
            def _():
                v = vmem_ref.at[pl.ds(pl.multiple_of(off + done, ROUTE_PAD), p)]
                h = hbm_ref.at[pl.ds(pl.multiple_of(start + done, ROUTE_PAD), p)]
                cp = pltpu.make_async_copy(v, h, sem) if to_hbm else pltpu.make_async_copy(h, v, sem)
                if wait:
                    cp.wait()
                else:
                    cp.start()
        off = off + n


def _zero_fill(tab_ref, meta, zeros_ref, hbm_ref, sem, n_rows, wait):
    def copy(rows, dst_row):
        cp = pltpu.make_async_copy(zeros_ref.at[pl.ds(0, rows)],
                                   hbm_ref.at[pl.ds(pl.multiple_of(dst_row, ROUTE_PAD), rows)], sem)
        if wait:
            cp.wait()
        else:
            cp.start()

    end = 0
    for g in range(N_GROUPS_MOE):
        fill = tab_ref[meta + g]
        start = tab_ref[meta + N_GROUPS_MOE + g]
        end = start + _round_up(fill, FFN_BM)
        tail = end - start - fill
        for p in RUN_SIZES:
            pl.when((tail & p) != 0)(functools.partial(copy, p, start + fill + (tail & (-2 * p))))
    for k in range(n_rows // FFN_BM):
        pl.when(end + k * FFN_BM < n_rows)(functools.partial(copy, FFN_BM, end + k * FFN_BM))


def _wait_rows(n, hbm_ref, vmem_ref, sem, to_hbm):
    for p in RUN_SIZES:
        @pl.when((n & p) != 0)
        def _():
            v = vmem_ref.at[pl.ds(0, p)]
            h = hbm_ref.at[pl.ds(0, p)]
            (pltpu.make_async_copy(v, h, sem) if to_hbm else pltpu.make_async_copy(h, v, sem)).wait()


def _tile_rows(tab_ref, tile):
    n = 0
    for g in range(N_GROUPS_MOE):
        n = n + tab_ref[tile * TAB_W + N_GROUPS_MOE + g]
    return n


def _max4(v):
    return jnp.maximum(jnp.maximum(v[0], v[1]), jnp.maximum(v[2], v[3]))


def _first_of4(v, top):
    return jnp.where(v[0] == top, 0.0, jnp.where(v[1] == top, 1.0, jnp.where(v[2] == top, 2.0, 3.0)))


def _route_kernel(xp_ref, xs_ref, mod_ref, g_ref, wrt_ref, brt_ref, upper_ref,
                  dloc_ref, tab_ref, sorted_hbm, hbuf_ref, drow_ref, sorted_ref, zeros_ref, cnt_ref, fill_ref,
                  sem_ref, zsem_ref, *, n_steps, n_p_steps):
    sweep = pl.program_id(0)
    i = pl.program_id(1)
    tm = ROUTE_TM
    meta = n_steps * ROUTE_SUBS * TAB_W
    n_rows = sorted_hbm.shape[0]
    ng = N_GROUPS_MOE

    @pl.when(sweep == 0)
    def _():
        @pl.when(i == 0)
        def _():
            for g in range(ng):
                fill_ref[g] = 0

        m = mod_ref[...]
        x = jnp.where(i < n_p_steps, xp_ref[...], xs_ref[...])
        h = _rms(x, g_ref[...]) * (1.0 + m[4:5]) + m[3:4]
        hb = h.astype(BF16)
        lt = lax.dot_general(wrt_ref[...], hb, (((1,), (1,)), ((), ())), preferred_element_type=F32) + brt_ref[...]
        gl = [lt[r:r + 1, :] for r in range(ng)]
        g_top = _max4(gl)
        g_idx = _first_of4(gl, g_top)
        g_w = 1.0 / (jnp.exp(gl[0] - g_top) + jnp.exp(gl[1] - g_top) + jnp.exp(gl[2] - g_top) + jnp.exp(gl[3] - g_top))
        ev = []
        for k in range(EXPERTS_PER_GROUP):
            cand = [lt[GATE_OFF + EXPERTS_PER_GROUP * r + k:GATE_OFF + EXPERTS_PER_GROUP * r + k + 1, :]
                    for r in range(ng)]
            ev.append(jnp.where(g_idx == 0.0, cand[0], jnp.where(g_idx == 1.0, cand[1],
                                jnp.where(g_idx == 2.0, cand[2], cand[3]))))
        v1 = _max4(ev)
        i1 = _first_of4(ev, v1)
        rest = [jnp.where(i1 == float(k), NEG_BIG, ev[k]) for k in range(EXPERTS_PER_GROUP)]
        v2 = _max4(rest)
        i2 = _first_of4(rest, v2)
        e2 = jnp.exp(v2 - v1)
        w1 = 1.0 / (1.0 + e2)
        w2 = e2 * w1
        gates = [g_w * (jnp.where(i1 == float(k), w1, 0.0) + jnp.where(i2 == float(k), w2, 0.0))
                 for k in range(EXPERTS_PER_GROUP)]

        sub8 = lax.broadcasted_iota(jnp.int32, (8, tm), 0).astype(F32)
        dlocs = []
        for sub in range(ROUTE_SUBS):
            gi = g_idx[:, sub * tm:(sub + 1) * tm]
            hot = jnp.where(sub8 == gi, 1.0, 0.0)
            before = _bdot(hot.astype(BF16), upper_ref[...])
            dl = jnp.sum(before * hot, axis=0, keepdims=True)
            off = 0
            for g in range(ng):
                n_g = _round_up(jnp.sum(hot[g:g + 1, :]).astype(jnp.int32), ROUTE_PAD)
                cnt_ref[(i * ROUTE_SUBS + sub) * ng + g] = n_g
                fill_ref[g] = fill_ref[g] + n_g
                dl = dl + jnp.where(gi == float(g), off.astype(F32) if g else 0.0, 0.0)
                off = off + n_g
            drow_ref[i * ROUTE_SUBS + sub] = jnp.broadcast_to(dl, (8, tm))
            dlocs.append(dl)
        dloc = jnp.concatenate(dlocs, axis=1)
        d_hi = jnp.floor(dloc * (1.0 / DLOC_RADIX))
        g_hi = [gt.astype(BF16).astype(F32) for gt in gates]
        ex_rows = g_hi + [gt - gh for gt, gh in zip(gates, g_hi)] + [d_hi, dloc - DLOC_RADIX * d_hi]
        sub16 = lax.broadcasted_iota(jnp.int32, (16, ROUTE_STEP), 0)
        ex_t = jnp.zeros((16, ROUTE_STEP), F32)
        for r, row in enumerate(ex_rows):
            ex_t = jnp.where(sub16 == r, row, ex_t)
        ex_t = jnp.concatenate([ex_t, jnp.zeros((LANES - 16, ROUTE_STEP), F32)], axis=0)
        extras = ex_t.T
        dloc_ref[...] = jnp.broadcast_to(
            DLOC_RADIX * extras[:, DLOC_HI:DLOC_HI + 1] + extras[:, DLOC_HI + 1:DLOC_HI + 2], (ROUTE_STEP, LANES))
        hbuf_ref[pl.ds(pl.multiple_of(i * ROUTE_STEP, ROUTE_STEP), ROUTE_STEP), :] = (
            jnp.concatenate([hb, extras.astype(BF16)], axis=1))

    @pl.when(sweep == 1)
    def _():
        @pl.when(i == 0)
        def _():
            start = 0
            for g in range(ng):
                tab_ref[meta + g] = fill_ref[g]
                tab_ref[meta + ng + g] = start
                start = start + _round_up(fill_ref[g], FFN_BM)
                fill_ref[g] = 0
            zeros_ref[...] = jnp.zeros_like(zeros_ref)

        row_f = lax.broadcasted_iota(jnp.int32, (SORT_ROWS, tm), 0).astype(F32)
        for sub in range(ROUTE_SUBS):
            tile = i * ROUTE_SUBS + sub

            @pl.when(i >= 1)
            def _():
                _wait_rows(_tile_rows(tab_ref, tile - ROUTE_SUBS), sorted_hbm, sorted_ref.at[sub], sem_ref.at[sub], True)

            rows = hbuf_ref[pl.ds(pl.multiple_of(i * ROUTE_STEP + sub * tm, tm), tm), :]
            onehot = jnp.where(row_f == drow_ref[tile][0:1, :], 1.0, 0.0).astype(BF16)
            sorted_ref[sub] = _bdot(onehot, rows)
            for g in range(ng):
                n_g = cnt_ref[tile * ng + g]
                tab_ref[tile * TAB_W + g] = tab_ref[meta + ng + g] + fill_ref[g]
                tab_ref[tile * TAB_W + ng + g] = n_g
                fill_ref[g] = fill_ref[g] + n_g
            _run_copies(tab_ref, tile, sorted_hbm, sorted_ref.at[sub], sem_ref.at[sub], to_hbm=True, wait=False)

            @pl.when(i == n_steps - 1)
            def _():
                _wait_rows(_tile_rows(tab_ref, tile), sorted_hbm, sorted_ref.at[sub], sem_ref.at[sub], True)

        @pl.when(i == n_steps - 1)
        def _():
            _zero_fill(tab_ref, meta, zeros_ref, sorted_hbm, zsem_ref, n_rows, wait=False)
            _zero_fill(tab_ref, meta, zeros_ref, sorted_hbm, zsem_ref, n_rows, wait=True)


def _merged_specs(n_p_steps, n_s_steps, layer, sample_seq, step_of):
    def p_map(*idx):
        return (jnp.minimum(step_of(*idx), n_p_steps - 1), 0)

    def s_map(*idx):
        return (jnp.clip(step_of(*idx) - n_p_steps, 0, n_s_steps - 1), 0)

    def mod_map(*idx):
        j = step_of(*idx)
        row = jnp.where(j < n_p_steps, 0, 1 + lax.div(jnp.maximum(j - n_p_steps, 0) * ROUTE_STEP, sample_seq))
        return (layer, row, 0, 0)

    return (pl.BlockSpec((ROUTE_STEP, D_MODEL), p_map), pl.BlockSpec((ROUTE_STEP, D_MODEL), s_map),
            pl.BlockSpec((None, None, 6, D_MODEL), mod_map))


def _route(xp, xs, mod, g2, w_rt, b_rt, layer, sample_seq, n_rows):
    n_p_steps = xp.shape[0] // ROUTE_STEP
    n_s_steps = xs.shape[0] // ROUTE_STEP
    n_steps = n_p_steps + n_s_steps
    t = n_steps * ROUTE_STEP
    tm = ROUTE_TM
    upper = jnp.asarray(np.triu(np.ones((tm, tm), np.float32), 1), BF16)
    step_of = lambda s, i: jnp.where(s == 0, i, n_steps - 1)
    p_spec, s_spec, mod_spec = _merged_specs(n_p_steps, n_s_steps, layer, sample_seq, step_of)
    const = lambda a: pl.BlockSpec(a.shape, lambda s, i: (0,) * a.ndim)
    n_tiles = n_steps * ROUTE_SUBS
    return pl.pallas_call(
        functools.partial(_route_kernel, n_steps=n_steps, n_p_steps=n_p_steps),
        grid=(2, n_steps),
        in_specs=[p_spec, s_spec, mod_spec, const(g2), const(w_rt), const(b_rt), const(upper)],
        out_specs=[
            pl.BlockSpec((ROUTE_STEP, LANES), lambda s, i: (step_of(s, i), 0)),
            pl.BlockSpec(memory_space=pltpu.SMEM),
            pl.BlockSpec(memory_space=pl.ANY),
        ],
        out_shape=[
            jax.ShapeDtypeStruct((t, LANES), F32),
            jax.ShapeDtypeStruct(((n_tiles + 1) * TAB_W,), jnp.int32),
            jax.ShapeDtypeStruct((n_rows, XS_W), F32),
        ],
        scratch_shapes=[
            pltpu.VMEM((t, XS_W), BF16),
            pltpu.VMEM((n_tiles, 8, tm), F32),
            pltpu.VMEM((ROUTE_SUBS, SORT_ROWS, XS_W), F32),
            pltpu.VMEM((FFN_BM, XS_W), F32),
            pltpu.SMEM((n_tiles * N_GROUPS_MOE,), jnp.int32),
            pltpu.SMEM((N_GROUPS_MOE,), jnp.int32),
            pltpu.SemaphoreType.DMA((ROUTE_SUBS,)),
            pltpu.SemaphoreType.DMA(()),
        ],
        compiler_params=_cparams("arbitrary", "arbitrary"),
        name="moe_route",
    )(xp, xs, mod, g2, w_rt, b_rt, upper)


def _ffn_lookup(i, tab_ref, meta):
    fills = [tab_ref[meta + g] for g in range(N_GROUPS_MOE)]
    edges = []
    acc = 0
    for f in fills:
        acc = acc + lax.div(f + (FFN_BM - 1), FFN_BM)
        edges.append(acc)
    total = edges[-1]
    ii = jnp.minimum(i, total - 1)
    grp = sum((ii >= e).astype(jnp.int32) for e in edges[:-1])

    def pick(vals):
        return jnp.where(grp == 0, vals[0], jnp.where(grp == 1, vals[1], jnp.where(grp == 2, vals[2], vals[3])))

    first = pick([0] + edges[:-1])
    return grp, total, ii == first, pick(fills) - (ii - first) * FFN_BM


def _ffn_group_kernel(tab_ref, xs_ref, w1_ref, w3_ref, w2_ref, ys_ref, w1b_ref, w3b_ref, w2b_ref, *, meta):
    i = pl.program_id(0)
    _, total, first_of_group, valid = _ffn_lookup(i, tab_ref, meta)
    active = i < total

    @pl.when(jnp.logical_and(active, first_of_group))
    def _():
        w1b_ref[...] = w1_ref[...].astype(BF16)
        w3b_ref[...] = w3_ref[...].astype(BF16)
        w2b_ref[...] = w2_ref[...].astype(BF16)

    def run(rows):
        hb = xs_ref[0:rows, 0:D_MODEL].astype(BF16)
        ex = xs_ref[0:rows, D_MODEL:XS_W]
        acc = None
        for e in range(EXPERTS_PER_GROUP):
            a = _bdot(hb, w1b_ref[e])
            b = _bdot(hb, w3b_ref[e])
            gate = ex[:, e:e + 1] + ex[:, GATE_LO + e:GATE_LO + e + 1]
            hid = (a * jax.nn.sigmoid(a)) * b * gate
            part = _bdot(hid.astype(BF16), w2b_ref[e])
            acc = part if acc is None else acc + part
        ys_ref[0:rows, :] = acc

    @pl.when(jnp.logical_and(active, valid > FFN_HALF))
    def _():
        run(FFN_BM)

    @pl.when(jnp.logical_and(active, valid <= FFN_HALF))
    def _():
        run(FFN_HALF)
        ys_ref[FFN_HALF:, :] = jnp.zeros((FFN_BM - FFN_HALF, D_MODEL), F32)

    @pl.when(jnp.logical_not(active))
    def _():
        ys_ref[...] = jnp.zeros_like(ys_ref)


def _ffn_group(tab, xs, w1, w3, w2, layer, meta):
    e4 = EXPERTS_PER_GROUP
    n_blocks = xs.shape[0] // FFN_BM
    group_of = lambda i, tab_ref: _ffn_lookup(i, tab_ref, meta)[0]
    grid_spec = pltpu.PrefetchScalarGridSpec(
        num_scalar_prefetch=1,
        grid=(n_blocks,),
        in_specs=[
            pl.BlockSpec((FFN_BM, XS_W), lambda i, tab_ref: (i, 0)),
            pl.BlockSpec((None, e4, D_MODEL, D_EXPERT), lambda i, tab_ref: (layer, group_of(i, tab_ref), 0, 0)),
            pl.BlockSpec((None, e4, D_MODEL, D_EXPERT), lambda i, tab_ref: (layer, group_of(i, tab_ref), 0, 0)),
            pl.BlockSpec((None, e4, D_EXPERT, D_MODEL), lambda i, tab_ref: (layer, group_of(i, tab_ref), 0, 0)),
        ],
        out_specs=pl.BlockSpec((FFN_BM, D_MODEL), lambda i, tab_ref: (i, 0)),
        scratch_shapes=[
            pltpu.VMEM((e4, D_MODEL, D_EXPERT), BF16),
            pltpu.VMEM((e4, D_MODEL, D_EXPERT), BF16),
            pltpu.VMEM((e4, D_EXPERT, D_MODEL), BF16),
        ],
    )
    return pl.pallas_call(
        functools.partial(_ffn_group_kernel, meta=meta),
        grid_spec=grid_spec,
        out_shape=jax.ShapeDtypeStruct((xs.shape[0], D_MODEL), F32),
        compiler_params=_cparams("arbitrary"),
        name="moe_ffn",
    )(tab, xs, w1, w3, w2)


def _combine_kernel(tab_ref, xp_ref, xs_ref, mod_ref, dloc_ref, fg_ref, ys_ref, op_ref, os_ref, runs_ref, sem_ref,
                    *, n_steps, n_p_steps, final_norm):
    i = pl.program_id(0)
    par = lax.rem(i, 2)
    tm = ROUTE_TM

    def copies(step, parity, wait):
        for sub in range(ROUTE_SUBS):
            _run_copies(tab_ref, tile=step * ROUTE_SUBS + sub, hbm_ref=ys_ref, vmem_ref=runs_ref.at[parity, sub],
                        sem=sem_ref.at[parity, sub], to_hbm=False, wait=wait)

    @pl.when(i == 0)
    def _():
        copies(0, 0, wait=False)

    @pl.when(i + 1 < n_steps)
    def _():
        copies(i + 1, 1 - par, wait=False)

    for sub in range(ROUTE_SUBS):
        _wait_rows(_tile_rows(tab_ref, i * ROUTE_SUBS + sub), ys_ref, runs_ref.at[par, sub], sem_ref.at[par, sub], False)
    row_f = lax.broadcasted_iota(jnp.int32, (tm, SORT_ROWS), 1).astype(F32)
    parts = []
    for sub in range(ROUTE_SUBS):
        tile = i * ROUTE_SUBS + sub
        covered = 0
        for g in range(N_GROUPS_MOE):
            covered = covered + tab_ref[tile * TAB_W + N_GROUPS_MOE + g]
        runs_ref[par, sub, pl.ds(pl.multiple_of(covered, ROUTE_PAD), LANES), :] = jnp.zeros((LANES, D_MODEL), F32)
        yb = runs_ref[par, sub, 0:SORT_ROWS, :].astype(BF16)
        pt = jnp.where(row_f == dloc_ref[sub * tm:(sub + 1) * tm, 0:1], 1.0, 0.0).astype(BF16)
        parts.append(_bdot(pt, yb))
    is_prompt = i < n_p_steps
    x = jnp.where(is_prompt, xp_ref[...], xs_ref[...])
    x2 = x + mod_ref[5:6, :] * jnp.concatenate(parts, axis=0)
    if final_norm:
        x2 = _rms(x2, fg_ref[...])

    @pl.when(is_prompt)
    def _():
        op_ref[...] = x2

    @pl.when(jnp.logical_not(is_prompt))
    def _():
        os_ref[...] = x2


def _combine(tab, xp, xs, mod, dloc, final_g, ys, layer, sample_seq, final_norm):
    n_p_steps = xp.shape[0] // ROUTE_STEP
    n_s_steps = xs.shape[0] // ROUTE_STEP
    n_steps = n_p_steps + n_s_steps
    step_of = lambda i, tab_ref: i
    p_spec, s_spec, mod_spec = _merged_specs(n_p_steps, n_s_steps, layer, sample_seq, step_of)
    grid_spec = pltpu.PrefetchScalarGridSpec(
        num_scalar_prefetch=1,
        grid=(n_steps,),
        in_specs=[
            p_spec, s_spec, mod_spec,
            pl.BlockSpec((ROUTE_STEP, LANES), lambda i, tab_ref: (i, 0)),
            pl.BlockSpec((1, D_MODEL), lambda i, tab_ref: (0, 0)),
            pl.BlockSpec(memory_space=pl.ANY),
        ],
        out_specs=[p_spec, s_spec],
        scratch_shapes=[
            pltpu.VMEM((2, ROUTE_SUBS, RUN_ROWS, D_MODEL), F32),
            pltpu.SemaphoreType.DMA((2, ROUTE_SUBS)),
        ],
    )
    return pl.pallas_call(
        functools.partial(_combine_kernel, n_steps=n_steps, n_p_steps=n_p_steps, final_norm=final_norm),
        grid_spec=grid_spec,
        out_shape=[jax.ShapeDtypeStruct(xp.shape, F32), jax.ShapeDtypeStruct(xs.shape, F32)],
        compiler_params=_cparams("arbitrary"),
        name="moe_combine",
    )(tab, xp, xs, mod, dloc, final_g, ys)


def _moe(xp, xs, mod, g2, w_r, b_r, w1, w3, w2, final_g, layer, sample_seq, final_norm):
    t = xp.shape[0] + xs.shape[0]
    n_tiles = t // ROUTE_TM
    max_rows = t + N_GROUPS_MOE * (ROUTE_PAD - 1) * n_tiles
    n_rows = (-(-max_rows // FFN_BM) + N_GROUPS_MOE) * FFN_BM
    dloc, tab, sorted_x = _route(xp, xs, mod, g2, w_r, b_r, layer, sample_seq, n_rows)
    ys = _ffn_group(tab, sorted_x, w1, w3, w2, layer, n_tiles * TAB_W)
    return _combine(tab, xp, xs, mod, dloc, final_g, ys, layer, sample_seq, final_norm)


def _rope_tables(seq):
    half = QK_ROPE // 2
    nf = half // 2
    inv = ROPE_BASE ** (-np.arange(nf, dtype=np.float64) / nf)
    pos = np.arange(seq)
    row = (pos // GRID_W).astype(np.float64)
    col = (pos % GRID_W).astype(np.float64)
    cos = np.ones((seq, HEAD_PAD), np.float64)
    sin_a = np.zeros((seq, HEAD_PAD), np.float64)
    sin_b = np.zeros((seq, HEAD_PAD), np.float64)
    for part, p in enumerate((row, col)):
        ang = p[:, None] * inv[None, :]
        base = ROPE_OFF + part * half
        cos[:, base:base + nf] = np.cos(ang)
        cos[:, base + nf:base + half] = np.cos(ang)
        sin_a[:, base:base + nf] = -np.sin(ang)
        sin_b[:, base + nf:base + half] = np.sin(ang)
    return tuple(jnp.asarray(a, F32) for a in (cos, sin_a, sin_b))


def _apply_rope(x, cos, sin_a, sin_b, reps):
    nf = QK_ROPE // 4
    width = x.shape[1]
    if reps > 1:
        cos, sin_a, sin_b = (jnp.concatenate([a] * reps, axis=1) for a in (cos, sin_a, sin_b))
    return x * cos + pltpu.roll(x, width - nf, axis=1) * sin_a + pltpu.roll(x, nf, axis=1) * sin_b


def _odd_in_kernel(*refs, rope, emit_cache):
    x_ref, mod_ref, g_ref, w_ref, qg_ref, wq_ref, kg_ref, wk_ref, wv_ref, cs_ref = refs[:10]
    refs = refs[10:]
    if rope:
        cos_ref, sa_ref, sb_ref = refs[:3]
        refs = refs[3:]
    y_ref, q_ref, k_ref, v_ref = refs[:4]
    refs = refs[4:]
    m = mod_ref[...]
    h = _rms(x_ref[...], g_ref[...]) * (1.0 + m[1:2]) + m[0:1]
    z = _bdot(h.astype(BF16), w_ref[...])
    zc = z[:, 0:D_C]
    qc = z[:, D_C:D_C + Q_LORA]
    kvc = z[:, D_C + Q_LORA:D_C + Q_LORA + KV_LORA]
    kpe = z[:, D_C + Q_LORA + KV_LORA:]
    q = _bdot(_rms(qc, qg_ref[...]).astype(BF16), wq_ref[...])
    kvn = _rms(kvc, kg_ref[...])
    if emit_cache:
        ckv_ref, kpe_ref = refs
        ckv_ref[...] = kvn
        kpe_ref[...] = kpe[:, ROPE_OFF:ROPE_OFF + QK_ROPE]
    if rope:
        tabs = (cos_ref[...], sa_ref[...], sb_ref[...])
        q = _apply_rope(q, *tabs, reps=N_HEADS)
        kpe = _apply_rope(kpe, *tabs, reps=1)
    kvb = kvn.astype(BF16)
    k = _bdot(kvb, wk_ref[...]) + jnp.concatenate([kpe] * N_HEADS, axis=1)
    scale = 1.0 / math.sqrt(QK_NOPE + QK_ROPE)
    q_ref[...] = (q * scale).astype(BF16)
    k_ref[...] = k.astype(BF16)
    v_ref[...] = lax.dot_general(wv_ref[...], kvb, NT_DIMS, preferred_element_type=F32).astype(BF16)
    y = _bdot(zc.astype(BF16), cs_ref[...])
    y_ref[0, :, :] = y[:, 0:D_C].astype(BF16)
    y_ref[1, :, :] = y[:, D_C:2 * D_C].astype(BF16)


def _odd_in(x3, mod, g, w_in, q_g, w_q, kv_g, w_k, w_v, cs, stream, layer, rope_tabs, emit_cache):
    b, s, _ = x3.shape
    tm = min(s, 512)
    n_i = s // tm
    rope = rope_tabs is not None
    const = lambda a: pl.BlockSpec(a.shape, lambda bi, i: (0,) * a.ndim)
    in_specs = [
        pl.BlockSpec((None, tm, D_MODEL), lambda bi, i: (bi, i, 0)),
        _mod_spec(stream, layer),
        const(g), const(w_in), const(q_g), const(w_q), const(kv_g), const(w_k), const(w_v), const(cs),
    ]
    args = [x3, mod, g, w_in, q_g, w_q, kv_g, w_k, w_v, cs]
    if rope:
        in_specs += [pl.BlockSpec((tm, HEAD_PAD), lambda bi, i: (i, 0))] * 3
        args += list(rope_tabs)
    hq = N_HEADS * HEAD_PAD
    out_specs = [
        pl.BlockSpec((None, 2, tm, D_C), lambda bi, i: (bi, 0, i, 0)),
        pl.BlockSpec((None, tm, hq), lambda bi, i: (bi, i, 0)),
        pl.BlockSpec((None, tm, hq), lambda bi, i: (bi, i, 0)),
        pl.BlockSpec((None, N_HEADS * V_DIM, tm), lambda bi, i: (bi, 0, i)),
    ]
    out_shape = [
        jax.ShapeDtypeStruct((b, 2, s, D_C), BF16),
        jax.ShapeDtypeStruct((b, s, hq), BF16),
        jax.ShapeDtypeStruct((b, s, hq), BF16),
        jax.ShapeDtypeStruct((b, N_HEADS * V_DIM, s), BF16),
    ]
    if emit_cache:
        out_specs += [
            pl.BlockSpec((None, tm, KV_LORA), lambda bi, i: (bi, i, 0)),
            pl.BlockSpec((None, tm, QK_ROPE), lambda bi, i: (bi, i, 0)),
        ]
        out_shape += [
            jax.ShapeDtypeStruct((b, s, KV_LORA), F32),
            jax.ShapeDtypeStruct((b, s, QK_ROPE), F32),
        ]
    return pl.pallas_call(
        functools.partial(_odd_in_kernel, rope=rope, emit_cache=emit_cache),
        grid=(b, n_i),
        in_specs=in_specs,
        out_specs=out_specs,
        out_shape=out_shape,
        compiler_params=_cparams("parallel", "parallel"),
        name="odd_in",
    )(*args)


def _cache_kv_kernel(c_ref, p_ref, wk_ref, wv_ref, k_ref, v_ref):
    cb = c_ref[...].astype(BF16)
    k = _bdot(cb, wk_ref[...]) + jnp.concatenate([p_ref[...]] * N_HEADS, axis=1)
    k_ref[...] = k.astype(BF16)
    v_ref[...] = lax.dot_general(wv_ref[...], cb, NT_DIMS, preferred_element_type=F32).astype(BF16)


def _cache_kv(ckv, kpe_blk, w_k, w_v):
    b, p, _ = ckv.shape
    hq = N_HEADS * HEAD_PAD
    return pl.pallas_call(
        _cache_kv_kernel,
        grid=(b,),
        in_specs=[
            pl.BlockSpec((None, p, KV_LORA), lambda bi: (bi, 0, 0)),
            pl.BlockSpec((None, p, HEAD_PAD), lambda bi: (bi, 0, 0)),
            pl.BlockSpec(w_k.shape, lambda bi: (0, 0)),
            pl.BlockSpec(w_v.shape, lambda bi: (0, 0)),
        ],
        out_specs=[
            pl.BlockSpec((None, p, hq), lambda bi: (bi, 0, 0)),
            pl.BlockSpec((None, N_HEADS * V_DIM, p), lambda bi: (bi, 0, 0)),
        ],
        out_shape=[
            jax.ShapeDtypeStruct((b, p, hq), BF16),
            jax.ShapeDtypeStruct((b, N_HEADS * V_DIM, p), BF16),
        ],
        compiler_params=_cparams("parallel"),
        name="cache_kv",
    )(ckv, kpe_blk, w_k, w_v)


def _odd_mix_kernel(*refs, with_cache):
    q_ref, k_ref, vt_ref = refs[:3]
    refs = refs[3:]
    if with_cache:
        kc_ref, vct_ref = refs[:2]
        refs = refs[2:]
    y_ref, f_ref, x_ref, mod_ref, wo_ref, o_ref, at_ref = refs
    for h in range(N_HEADS):
        hcols = slice(h * HEAD_PAD, (h + 1) * HEAD_PAD)
        vrows = slice(h * V_DIM, (h + 1) * V_DIM)
        qh = q_ref[:, hcols]
        st = lax.dot_general(k_ref[:, hcols], qh, NT_DIMS, preferred_element_type=F32)
        top = jnp.max(st, axis=0, keepdims=True)
        if with_cache:
            sc = lax.dot_general(kc_ref[:, hcols], qh, NT_DIMS, preferred_element_type=F32)
            top = jnp.maximum(top, jnp.max(sc, axis=0, keepdims=True))
        p = jnp.exp(st - top)
        den = jnp.sum(p, axis=0, keepdims=True)
        ot = _bdot(vt_ref[vrows, :], p.astype(BF16))
        if with_cache:
            pc = jnp.exp(sc - top)
            den = den + jnp.sum(pc, axis=0, keepdims=True)
            ot = ot + _bdot(vct_ref[vrows, :], pc.astype(BF16))
        at_ref[vrows, :] = (ot / den).astype(BF16)
    f = _bdot(f_ref[...], y_ref[...])
    o = _bdot(f.astype(BF16), wo_ref[0:D_C, :]) + lax.dot_general(
        at_ref[...], wo_ref[D_C:, :], TN_DIMS, preferred_element_type=F32)
    o_ref[...] = x_ref[...] + mod_ref[2:3, :] * o


def _odd_mix(q, k, vt, kc, vct, y, fmat, x3, mod, w_out, stream, layer):
    b, s, hq = q.shape
    tq = min(s, 256)
    n_i = s // tq
    with_cache = kc is not None
    hv = N_HEADS * V_DIM
    mode = dict(pipeline_mode=pl.Buffered(1)) if with_cache else {}

    def per_batch(rows, cols):
        return pl.BlockSpec((None, rows, cols), lambda bi, i: (bi, 0, 0), **mode)

    in_specs = [pl.BlockSpec((None, tq, hq), lambda bi, i: (bi, i, 0)), per_batch(s, hq), per_batch(hv, s)]
    args = [q, k, vt]
    if with_cache:
        p = kc.shape[1]
        in_specs += [per_batch(p, hq), per_batch(hv, p)]
        args += [kc, vct]
    in_specs += [
        per_batch(2 * s, D_C),
        pl.BlockSpec((tq, 2 * s), lambda bi, i: (i, 0)),
        pl.BlockSpec((None, tq, D_MODEL), lambda bi, i: (bi, i, 0)),
        _mod_spec(stream, layer),
        pl.BlockSpec(w_out.shape, lambda bi, i: (0, 0), **mode),
    ]
    args += [y, fmat, x3, mod, w_out]
    return pl.pallas_call(
        functools.partial(_odd_mix_kernel, with_cache=with_cache),
        grid=(b, n_i),
        in_specs=in_specs,
        out_specs=pl.BlockSpec((None, tq, D_MODEL), lambda bi, i: (bi, i, 0)),
        out_shape=jax.ShapeDtypeStruct((b, s, D_MODEL), F32),
        scratch_shapes=[pltpu.VMEM((hv, tq), BF16)],
        compiler_params=_cparams("parallel", "arbitrary"),
        name="odd_mix",
    )(*args)


def _dft_tables(seq):
    jc = np.arange(C_GW)
    ang_c = 2.0 * np.pi * np.outer(jc, jc) / C_GW
    eye = np.eye(C_GROUPS)
    cs = np.concatenate([np.kron(eye, np.cos(ang_c)), np.kron(eye, np.sin(ang_c))], axis=1)
    jn = np.arange(seq)
    ang_n = 2.0 * np.pi * (np.outer(jn, jn) % seq) / seq
    scale = 1.0 / math.sqrt(seq * C_GW)
    fmat = np.concatenate([np.cos(ang_n), -np.sin(ang_n)], axis=1) * scale
    return jnp.asarray(cs, F32).astype(BF16), jnp.asarray(fmat, F32).astype(BF16)


def _odd_weights(w_in, w_uq, w_ukv):
    d = w_in.shape[0]
    base = D_C + Q_LORA + KV_LORA
    kpe_blk = jnp.zeros((d, HEAD_PAD), w_in.dtype).at[:, ROPE_OFF:ROPE_OFF + QK_ROPE].set(w_in[:, base:])
    w_in_p = jnp.concatenate([w_in[:, :base], kpe_blk], axis=1).astype(BF16)
    qh = w_uq.reshape(Q_LORA, N_HEADS, QK_NOPE + QK_ROPE)
    w_q = jnp.pad(qh, ((0, 0), (0, 0), (0, HEAD_PAD - QK_NOPE - QK_ROPE))).reshape(Q_LORA, -1).astype(BF16)
    kvh = w_ukv.reshape(KV_LORA, N_HEADS, QK_NOPE + V_DIM)
    w_k = jnp.pad(kvh[:, :, :QK_NOPE], ((0, 0), (0, 0), (0, HEAD_PAD - QK_NOPE))).reshape(KV_LORA, -1)
    w_vt = kvh[:, :, QK_NOPE:].reshape(KV_LORA, -1).T
    return w_in_p, w_q, w_k.astype(BF16), w_vt.astype(BF16)


ROUTER_ROWS = 32


def _router_weights(wg, bg, we, be):
    d = wg.shape[0]
    w = jnp.concatenate([wg, we.reshape(d, N_EXPERTS)], axis=1).T
    w = jnp.pad(w, ((0, ROUTER_ROWS - w.shape[0]), (0, 0))).astype(BF16)
    b = jnp.concatenate([bg, be.reshape(N_EXPERTS)])
    b = jnp.pad(b, (0, ROUTER_ROWS - b.shape[0])).reshape(ROUTER_ROWS, 1).astype(F32)
    return w, b


def kernel(x_prompt, x_sample, cache_ckv, cache_kpe, c, c_ctx, mod_w, mod_b, norm1_g, norm2_g,
           ev_w_in, ev_conv_w, ev_sgu_norm_g, ev_sgu_w, ev_sgu_b, ev_w_out,
           od_w_in, od_q_norm_g, od_w_uq, od_kv_norm_g, od_w_ukv, od_w_out,
           moe_wg, moe_bg, moe_we, moe_be, moe_w1, moe_w3, moe_w2, final_norm_g):
    bp, n_p, d = x_prompt.shape
    bs, n_s, _ = x_sample.shape
    streams = [(_Stream(bp, n_p, True), x_prompt), (_Stream(bs, n_s, False), x_sample)]

    n_rows = 1 + bs
    cond_t = jnp.concatenate([c_ctx[None, :], c], axis=0).T
    mod = _adaln(cond_t, mod_w, mod_b, n_rows)

    final_g = final_norm_g.reshape(1, d)
    xs = [x for _, x in streams]
    new_ckv, new_kpe = [], []
    for l in range(DEPTH):
        j = l // 2
        g1 = norm1_g[l].reshape(1, d)
        g2 = norm2_g[l].reshape(1, d)
        w_r, b_r = _router_weights(moe_wg[l], moe_bg[l], moe_we[l], moe_be[l])
        last = l == DEPTH - 1
        if l % 2 == 0:
            w_in = ev_w_in[j].astype(BF16)
            w_out = ev_w_out[j].astype(BF16)
            sgu_w = ev_sgu_w[j].astype(BF16)
            sgu_g = ev_sgu_norm_g[j].reshape(1, D_B)
            sgu_bias = jnp.repeat(ev_sgu_b[j].T, D_B // B_GROUPS, axis=1)
            for si, (st, _) in enumerate(streams):
                x3 = xs[si]
                z = _even_in(x3.reshape(st.tokens, d), mod, g1, w_in, st, l)
                xs[si] = _even_mix(z.reshape(st.batch, st.seq, -1), x3, mod, ev_conv_w[j], sgu_g, sgu_w,
                                   sgu_bias, w_out, st, l)
        else:
            w_in, w_q, w_k, w_v = _odd_weights(od_w_in[j], od_w_uq[j], od_w_ukv[j])
            w_out = od_w_out[j].astype(BF16)
            q_g = od_q_norm_g[j].reshape(1, Q_LORA)
            kv_g = od_kv_norm_g[j].reshape(1, KV_LORA)
            for si, (st, _) in enumerate(streams):
                x3 = xs[si]
                cs, fmat = _dft_tables(st.seq)
                is_prompt = st.shared_cond
                tabs = None if is_prompt else _rope_tables(st.seq)
                outs = _odd_in(x3, mod, g1, w_in, q_g, w_q, kv_g, w_k, w_v, cs, st, l, tabs, is_prompt)
                y, q, k, v = outs[:4]
                if is_prompt:
                    new_ckv.append(outs[4])
                    new_kpe.append(outs[5])
                    kc = vc = None
                else:
                    kpe_blk = jnp.pad(cache_kpe[:, j], ((0, 0), (0, 0), (ROPE_OFF, HEAD_PAD - ROPE_OFF - QK_ROPE)))
                    kc, vc = _cache_kv(cache_ckv[:, j], kpe_blk, w_k, w_v)
                xs[si] = _odd_mix(q, k, v, kc, vc, y.reshape(st.batch, 2 * st.seq, D_C), fmat, x3, mod,
                                  w_out, st, l)
        x2p, x2s = _moe(xs[0].reshape(bp * n_p, d), xs[1].reshape(bs * n_s, d), mod, g2, w_r, b_r,
                        moe_w1, moe_w3, moe_w2, final_g, l, n_s, last)
        xs = [x2p.reshape(bp, n_p, d), x2s.reshape(bs, n_s, d)]
    return (xs[0], xs[1], jnp.stack(new_ckv, axis=1), jnp.stack(new_kpe, axis=1))
```

```python
import functools
import math

import numpy as np
import jax
import jax.numpy as jnp
from jax import lax
from jax.experimental import pallas as pl
from jax.experimental.pallas import tpu as pltpu

D_MODEL = 1024
DEPTH = 2
GRID_W = 64
D_A = D_MODEL // 2
D_B = D_MODEL // 2
B_GROUPS = 4
CHUNK = 128
D_EVEN_IN = 3 * D_A + 2 * D_B
D_C = D_MODEL // 4
C_GROUPS = 4
C_GW = D_C // C_GROUPS
N_HEADS = 12
QK_NOPE = 64
QK_ROPE = 32
V_DIM = 64
Q_LORA = 384
KV_LORA = 256
ROPE_BASE = 10000.0
N_GROUPS_MOE = 4
EXPERTS_PER_GROUP = 4
N_EXPERTS = N_GROUPS_MOE * EXPERTS_PER_GROUP
D_EXPERT = 256
EPS = 1e-6

LANES = 128
HEAD_PAD = 128
ROPE_OFF = QK_NOPE
GATE_OFF = N_GROUPS_MOE
NEG_BIG = -1e30
F32 = jnp.float32
BF16 = jnp.bfloat16
VMEM_LIMIT = 56 * 1024 * 1024


def _cparams(*sem):
    return pltpu.CompilerParams(dimension_semantics=sem, vmem_limit_bytes=VMEM_LIMIT)


def _rms(x, g):
    return x * lax.rsqrt(jnp.mean(x * x, axis=-1, keepdims=True) + EPS) * g


def _bdot(a, b):
    return jnp.dot(a, b, preferred_element_type=F32)


NT_DIMS = (((1,), (1,)), ((), ()))
TN_DIMS = (((0,), (0,)), ((), ()))


def _mod_kernel(ct_ref, w_ref, b_ref, o_ref):
    c = ct_ref[...]
    s = c * jax.nn.sigmoid(c)
    w = w_ref[...]
    b = b_ref[...]
    for r in range(o_ref.shape[0]):
        o_ref[r:r + 1, :] = jnp.sum(s[:, r:r + 1] * w, axis=0, keepdims=True) + b


def _adaln(cond_t, mod_w, mod_b, n_rows):
    nt = 1024
    d6 = mod_w.shape[-1]
    out = pl.pallas_call(
        _mod_kernel,
        grid=(DEPTH, d6 // nt),
        in_specs=[
            pl.BlockSpec(cond_t.shape, lambda l, n: (0, 0)),
            pl.BlockSpec((None, D_MODEL, nt), lambda l, n: (l, 0, n)),
            pl.BlockSpec((None, 1, nt), lambda l, n: (l, 0, n)),
        ],
        out_specs=pl.BlockSpec((None, n_rows, nt), lambda l, n: (l, 0, n)),
        out_shape=jax.ShapeDtypeStruct((DEPTH, n_rows, d6), F32),
        compiler_params=_cparams("parallel", "parallel"),
        name="adaln",
    )(cond_t, mod_w, mod_b.reshape(DEPTH, 1, d6))
    return out.reshape(DEPTH, n_rows, 6, D_MODEL)


class _Stream:
    def __init__(self, batch, seq, shared_cond):
        self.batch = batch
        self.seq = seq
        self.tokens = batch * seq
        self.shared_cond = shared_cond

    def row_of_batch(self, b):
        return 0 if self.shared_cond else b + 1

    def row_of_tile(self, i, tm):
        return 0 if self.shared_cond else (i * tm) // self.seq + 1


def _mod_spec(stream, layer, tm=None):
    if tm is None:
        return pl.BlockSpec((None, None, 6, D_MODEL), lambda b, i: (layer, stream.row_of_batch(b), 0, 0))
    return pl.BlockSpec((None, None, 6, D_MODEL), lambda i, *_: (layer, stream.row_of_tile(i, tm), 0, 0))


HALO = 8


def _even_kernel(*refs, has_halo, nc):
    if has_halo:
        x_ref, xp_ref, xn_ref = refs[:3]
        refs = refs[3:]
    else:
        x_ref = refs[0]
        refs = refs[1:]
    mod_ref, g_ref, wi_ref, cw_ref, sg_ref, sw_ref, sb_ref, wo_ref, o_ref, z_ref, y_ref = refs
    i = pl.program_id(1)
    n_i = pl.num_programs(1)
    ts = x_ref.shape[0]
    m = mod_ref[...]
    g = g_ref[...]

    def modulate(x):
        return (_rms(x, g) * (1.0 + m[1:2]) + m[0:1]).astype(BF16)

    x = x_ref[...]
    hb = modulate(x)
    for n in range(D_EVEN_IN // nc):
        z_ref[:, n * nc:(n + 1) * nc] = _bdot(hb, wi_ref[:, n * nc:(n + 1) * nc]).astype(BF16)

    gate_b = z_ref[:, 0:D_A].astype(F32)
    gate_c = z_ref[:, D_A:2 * D_A].astype(F32)
    xa = z_ref[:, 2 * D_A:3 * D_A].astype(F32)
    t = gate_c * xa
    t_prev = pltpu.roll(t, 1, axis=0)
    t_next = pltpu.roll(t, ts - 1, axis=0)
    row = lax.broadcasted_iota(jnp.int32, (ts, 1), 0)
    if has_halo:
        hh = modulate(jnp.concatenate([xp_ref[...], xn_ref[...]], axis=0))
        zh = _bdot(hh, wi_ref[:, D_A:3 * D_A]).astype(BF16).astype(F32)
        th = zh[:, 0:D_A] * zh[:, D_A:2 * D_A]
        tp = th[HALO - 1:HALO] * (i > 0).astype(F32)
        tn = th[HALO:HALO + 1] * (i < n_i - 1).astype(F32)
    else:
        tp = tn = 0.0
    t_prev = jnp.where(row == 0, tp, t_prev)
    t_next = jnp.where(row == ts - 1, tn, t_next)
    cw = cw_ref[...]
    y_a = gate_b * (t_prev * cw[0:1] + t * cw[1:2] + t_next * cw[2:3])
    y_ref[:, 0:D_A] = y_a.astype(BF16)

    u = z_ref[:, 3 * D_A:3 * D_A + D_B].astype(F32)
    v = z_ref[:, 3 * D_A + D_B:3 * D_A + 2 * D_B].astype(F32)
    vb = _rms(v, sg_ref[...]).astype(BF16)
    gw = D_B // B_GROUPS
    for c in range(ts // CHUNK):
        rows = slice(c * CHUNK, (c + 1) * CHUNK)
        for gi in range(B_GROUPS):
            cols = slice(gi * gw, (gi + 1) * gw)
            sv = _bdot(sw_ref[gi], vb[rows, cols]) + sb_ref[:, cols]
            y_ref[rows, D_A + gi * gw:D_A + (gi + 1) * gw] = (u[rows, cols] * sv).astype(BF16)

    o_ref[...] = x + m[2:3] * _bdot(y_ref[...], wo_ref[...])


def _even_layer(x3, mod, g, w_in, conv_w, sgu_g, sgu_w, sgu_bias, w_out, stream, layer):
    b, s, _ = x3.shape
    ts = min(s, 256)
    n_i = s // ts
    has_halo = n_i > 1
    hb = ts // HALO
    last_h = s // HALO - 1
    const = lambda a: pl.BlockSpec(a.shape, lambda bi, i: (0,) * a.ndim)
    in_specs = [pl.BlockSpec((None, ts, D_MODEL), lambda bi, i: (bi, i, 0))]
    args = [x3]
    if has_halo:
        in_specs += [
            pl.BlockSpec((None, HALO, D_MODEL), lambda bi, i: (bi, jnp.maximum(i * hb - 1, 0), 0)),
            pl.BlockSpec((None, HALO, D_MODEL), lambda bi, i: (bi, jnp.minimum((i + 1) * hb, last_h), 0)),
        ]
        args += [x3, x3]
    in_specs += [_mod_spec(stream, layer), const(g), const(w_in), const(conv_w), const(sgu_g), const(sgu_w),
                 const(sgu_bias), const(w_out)]
    args += [mod, g, w_in, conv_w, sgu_g, sgu_w, sgu_bias, w_out]
    return pl.pallas_call(
        functools.partial(_even_kernel, has_halo=has_halo, nc=512),
        grid=(b, n_i),
        in_specs=in_specs,
        out_specs=pl.BlockSpec((None, ts, D_MODEL), lambda bi, i: (bi, i, 0)),
        out_shape=jax.ShapeDtypeStruct((b, s, D_MODEL), F32),
        scratch_shapes=[pltpu.VMEM((ts, D_EVEN_IN), BF16), pltpu.VMEM((ts, D_A + D_B), BF16)],
        compiler_params=_cparams("parallel", "parallel"),
        name="even_layer",
    )(*args)


ROUTE_TM = 256
ROUTE_SUBS = 2
ROUTE_STEP = ROUTE_TM * ROUTE_SUBS
ROUTE_PAD = 8
SORT_ROWS = ROUTE_TM + LANES
RUN_ROWS = SORT_ROWS + 32
XS_W = D_MODEL + LANES
GATE_LO = EXPERTS_PER_GROUP
DLOC_HI = 2 * EXPERTS_PER_GROUP
DLOC_RADIX = 16.0
FFN_BM = 512
FFN_HALF = FFN_BM // 2
RUN_SIZES = (256, 128, 64, 32, 16, 8)
TAB_W = 2 * N_GROUPS_MOE


def _round_up(x, m):
    return lax.div(x + (m - 1), m) * m


def _run_copies(tab_ref, tile, hbm_ref, vmem_ref, sem, to_hbm, wait):
    off = 0
    for g in range(N_GROUPS_MOE):
        start = tab_ref[tile * TAB_W + g]
        n = tab_ref[tile * TAB_W + N_GROUPS_MOE + g]
        for p in RUN_SIZES:
            done = n & (-2 * p)

            @pl.when((n & p) != 0)
            def _():
                v = vmem_ref.at[pl.ds(pl.multiple_of(off + done, ROUTE_PAD), p)]
                h = hbm_ref.at[pl.ds(pl.multiple_of(start + done, ROUTE_PAD), p)]
                cp = pltpu.make_async_copy(v, h, sem) if to_hbm else pltpu.make_async_copy(h, v, sem)
                if wait:
                    cp.wait()
                else:
                    cp.start()
        off = off + n


def _zero_fill(tab_ref, meta, zeros_ref, hbm_ref, sem, n_rows, wait):
    def copy(rows, dst_row):
        cp = pltpu.make_async_copy(zeros_ref.at[pl.ds(0, rows)],
                                   hbm_ref.at[pl.ds(pl.multiple_of(dst_row, ROUTE_PAD), rows)], sem)
        if wait:
            cp.wait()
        else:
            cp.start()

    end = 0
    for g in range(N_GROUPS_MOE):
        fill = tab_ref[meta + g]
        start = tab_ref[meta + N_GROUPS_MOE + g]
        end = start + _round_up(fill, FFN_BM)
        tail = end - start - fill
        for p in RUN_SIZES:
            pl.when((tail & p) != 0)(functools.partial(copy, p, start + fill + (tail & (-2 * p))))
    for k in range(n_rows // FFN_BM):
        pl.when(end + k * FFN_BM < n_rows)(functools.partial(copy, FFN_BM, end + k * FFN_BM))


def _wait_rows(n, hbm_ref, vmem_ref, sem, to_hbm):
    for p in RUN_SIZES:
        @pl.when((n & p) != 0)
        def _():
            v = vmem_ref.at[pl.ds(0, p)]
            h = hbm_ref.at[pl.ds(0, p)]
            (pltpu.make_async_copy(v, h, sem) if to_hbm else pltpu.make_async_copy(h, v, sem)).wait()


def _tile_rows(tab_ref, tile):
    n = 0
    for g in range(N_GROUPS_MOE):
        n = n + tab_ref[tile * TAB_W + N_GROUPS_MOE + g]
    return n


def _max4(v):
    return jnp.maximum(jnp.maximum(v[0], v[1]), jnp.maximum(v[2], v[3]))


def _first_of4(v, top):
    return jnp.where(v[0] == top, 0.0, jnp.where(v[1] == top, 1.0, jnp.where(v[2] == top, 2.0, 3.0)))


def _route_kernel(xp_ref, xs_ref, mod_ref, g_ref, wrt_ref, brt_ref, upper_ref,
                  dloc_ref, tab_ref, sorted_hbm, hbuf_ref, drow_ref, sorted_ref, zeros_ref, cnt_ref, fill_ref,
                  sem_ref, zsem_ref, *, n_steps, n_p_steps):
    sweep = pl.program_id(0)
    i = pl.program_id(1)
    tm = ROUTE_TM
    meta = n_steps * ROUTE_SUBS * TAB_W
    n_rows = sorted_hbm.shape[0]
    ng = N_GROUPS_MOE

    @pl.when(sweep == 0)
    def _():
        @pl.when(i == 0)
        def _():
            for g in range(ng):
                fill_ref[g] = 0

        m = mod_ref[...]
        x = jnp.where(i < n_p_steps, xp_ref[...], xs_ref[...])
        h = _rms(x, g_ref[...]) * (1.0 + m[4:5]) + m[3:4]
        hb = h.astype(BF16)
        lt = lax.dot_general(wrt_ref[...], hb, (((1,), (1,)), ((), ())), preferred_element_type=F32) + brt_ref[...]
        gl = [lt[r:r + 1, :] for r in range(ng)]
        g_top = _max4(gl)
        g_idx = _first_of4(gl, g_top)
        g_w = 1.0 / (jnp.exp(gl[0] - g_top) + jnp.exp(gl[1] - g_top) + jnp.exp(gl[2] - g_top) + jnp.exp(gl[3] - g_top))
        ev = []
        for k in range(EXPERTS_PER_GROUP):
            cand = [lt[GATE_OFF + EXPERTS_PER_GROUP * r + k:GATE_OFF + EXPERTS_PER_GROUP * r + k + 1, :]
                    for r in range(ng)]
            ev.append(jnp.where(g_idx == 0.0, cand[0], jnp.where(g_idx == 1.0, cand[1],
                                jnp.where(g_idx == 2.0, cand[2], cand[3]))))
        v1 = _max4(ev)
        i1 = _first_of4(ev, v1)
        rest = [jnp.where(i1 == float(k), NEG_BIG, ev[k]) for k in range(EXPERTS_PER_GROUP)]
        v2 = _max4(rest)
        i2 = _first_of4(rest, v2)
        e2 = jnp.exp(v2 - v1)
        w1 = 1.0 / (1.0 + e2)
        w2 = e2 * w1
        gates = [g_w * (jnp.where(i1 == float(k), w1, 0.0) + jnp.where(i2 == float(k), w2, 0.0))
                 for k in range(EXPERTS_PER_GROUP)]

        sub8 = lax.broadcasted_iota(jnp.int32, (8, tm), 0).astype(F32)
        dlocs = []
        for sub in range(ROUTE_SUBS):
            gi = g_idx[:, sub * tm:(sub + 1) * tm]
            hot = jnp.where(sub8 == gi, 1.0, 0.0)
            before = _bdot(hot.astype(BF16), upper_ref[...])
            dl = jnp.sum(before * hot, axis=0, keepdims=True)
            off = 0
            for g in range(ng):
                n_g = _round_up(jnp.sum(hot[g:g + 1, :]).astype(jnp.int32), ROUTE_PAD)
                cnt_ref[(i * ROUTE_SUBS + sub) * ng + g] = n_g
                fill_ref[g] = fill_ref[g] + n_g
                dl = dl + jnp.where(gi == float(g), off.astype(F32) if g else 0.0, 0.0)
                off = off + n_g
            drow_ref[i * ROUTE_SUBS + sub] = jnp.broadcast_to(dl, (8, tm))
            dlocs.append(dl)
        dloc = jnp.concatenate(dlocs, axis=1)
        d_hi = jnp.floor(dloc * (1.0 / DLOC_RADIX))
        g_hi = [gt.astype(BF16).astype(F32) for gt in gates]
        ex_rows = g_hi + [gt - gh for gt, gh in zip(gates, g_hi)] + [d_hi, dloc - DLOC_RADIX * d_hi]
        sub16 = lax.broadcasted_iota(jnp.int32, (16, ROUTE_STEP), 0)
        ex_t = jnp.zeros((16, ROUTE_STEP), F32)
        for r, row in enumerate(ex_rows):
            ex_t = jnp.where(sub16 == r, row, ex_t)
        ex_t = jnp.concatenate([ex_t, jnp.zeros((LANES - 16, ROUTE_STEP), F32)], axis=0)
        extras = ex_t.T
        dloc_ref[...] = jnp.broadcast_to(
            DLOC_RADIX * extras[:, DLOC_HI:DLOC_HI + 1] + extras[:, DLOC_HI + 1:DLOC_HI + 2], (ROUTE_STEP, LANES))
        hbuf_ref[pl.ds(pl.multiple_of(i * ROUTE_STEP, ROUTE_STEP), ROUTE_STEP), :] = (
            jnp.concatenate([hb, extras.astype(BF16)], axis=1))

    @pl.when(sweep == 1)
    def _():
        @pl.when(i == 0)
        def _():
            start = 0
            for g in range(ng):
                tab_ref[meta + g] = fill_ref[g]
                tab_ref[meta + ng + g] = start
                start = start + _round_up(fill_ref[g], FFN_BM)
                fill_ref[g] = 0
            zeros_ref[...] = jnp.zeros_like(zeros_ref)

        row_f = lax.broadcasted_iota(jnp.int32, (SORT_ROWS, tm), 0).astype(F32)
        for sub in range(ROUTE_SUBS):
            tile = i * ROUTE_SUBS + sub

            @pl.when(i >= 1)
            def _():
                _wait_rows(_tile_rows(tab_ref, tile - ROUTE_SUBS), sorted_hbm, sorted_ref.at[sub], sem_ref.at[sub], True)

            rows = hbuf_ref[pl.ds(pl.multiple_of(i * ROUTE_STEP + sub * tm, tm), tm), :]
            onehot = jnp.where(row_f == drow_ref[tile][0:1, :], 1.0, 0.0).astype(BF16)
            sorted_ref[sub] = _bdot(onehot, rows)
            for g in range(ng):
                n_g = cnt_ref[tile * ng + g]
                tab_ref[tile * TAB_W + g] = tab_ref[meta + ng + g] + fill_ref[g]
                tab_ref[tile * TAB_W + ng + g] = n_g
                fill_ref[g] = fill_ref[g] + n_g
            _run_copies(tab_ref, tile, sorted_hbm, sorted_ref.at[sub], sem_ref.at[sub], to_hbm=True, wait=False)

            @pl.when(i == n_steps - 1)
            def _():
                _wait_rows(_tile_rows(tab_ref, tile), sorted_hbm, sorted_ref.at[sub], sem_ref.at[sub], True)

        @pl.when(i == n_steps - 1)
        def _():
            _zero_fill(tab_ref, meta, zeros_ref, sorted_hbm, zsem_ref, n_rows, wait=False)
            _zero_fill(tab_ref, meta, zeros_ref, sorted_hbm, zsem_ref, n_rows, wait=True)


def _merged_specs(n_p_steps, n_s_steps, layer, sample_seq, step_of):
    def p_map(*idx):
        return (jnp.minimum(step_of(*idx), n_p_steps - 1), 0)

    def s_map(*idx):
        return (jnp.clip(step_of(*idx) - n_p_steps, 0, n_s_steps - 1), 0)

    def mod_map(*idx):
        j = step_of(*idx)
        row = jnp.where(j < n_p_steps, 0, 1 + lax.div(jnp.maximum(j - n_p_steps, 0) * ROUTE_STEP, sample_seq))
        return (layer, row, 0, 0)

    return (pl.BlockSpec((ROUTE_STEP, D_MODEL), p_map), pl.BlockSpec((ROUTE_STEP, D_MODEL), s_map),
            pl.BlockSpec((None, None, 6, D_MODEL), mod_map))


def _route(xp, xs, mod, g2, w_rt, b_rt, layer, sample_seq, n_rows):
    n_p_steps = xp.shape[0] // ROUTE_STEP
    n_s_steps = xs.shape[0] // ROUTE_STEP
    n_steps = n_p_steps + n_s_steps
    t = n_steps * ROUTE_STEP
    tm = ROUTE_TM
    upper = jnp.asarray(np.triu(np.ones((tm, tm), np.float32), 1), BF16)
    step_of = lambda s, i: jnp.where(s == 0, i, n_steps - 1)
    p_spec, s_spec, mod_spec = _merged_specs(n_p_steps, n_s_steps, layer, sample_seq, step_of)
    const = lambda a: pl.BlockSpec(a.shape, lambda s, i: (0,) * a.ndim)
    n_tiles = n_steps * ROUTE_SUBS
    return pl.pallas_call(
        functools.partial(_route_kernel, n_steps=n_steps, n_p_steps=n_p_steps),
        grid=(2, n_steps),
        in_specs=[p_spec, s_spec, mod_spec, const(g2), const(w_rt), const(b_rt), const(upper)],
        out_specs=[
            pl.BlockSpec((ROUTE_STEP, LANES), lambda s, i: (step_of(s, i), 0)),
            pl.BlockSpec(memory_space=pltpu.SMEM),
            pl.BlockSpec(memory_space=pl.ANY),
        ],
        out_shape=[
            jax.ShapeDtypeStruct((t, LANES), F32),
            jax.ShapeDtypeStruct(((n_tiles + 1) * TAB_W,), jnp.int32),
            jax.ShapeDtypeStruct((n_rows, XS_W), F32),
        ],
        scratch_shapes=[
            pltpu.VMEM((t, XS_W), BF16),
            pltpu.VMEM((n_tiles, 8, tm), F32),
            pltpu.VMEM((ROUTE_SUBS, SORT_ROWS, XS_W), F32),
            pltpu.VMEM((FFN_BM, XS_W), F32),
            pltpu.SMEM((n_tiles * N_GROUPS_MOE,), jnp.int32),
            pltpu.SMEM((N_GROUPS_MOE,), jnp.int32),
            pltpu.SemaphoreType.DMA((ROUTE_SUBS,)),
            pltpu.SemaphoreType.DMA(()),
        ],
        compiler_params=_cparams("arbitrary", "arbitrary"),
        name="moe_route",
    )(xp, xs, mod, g2, w_rt, b_rt, upper)


def _ffn_lookup(i, tab_ref, meta):
    fills = [tab_ref[meta + g] for g in range(N_GROUPS_MOE)]
    edges = []
    acc = 0
    for f in fills:
        acc = acc + lax.div(f + (FFN_BM - 1), FFN_BM)
        edges.append(acc)
    total = edges[-1]
    ii = jnp.minimum(i, total - 1)
    grp = sum((ii >= e).astype(jnp.int32) for e in edges[:-1])

    def pick(vals):
        return jnp.where(grp == 0, vals[0], jnp.where(grp == 1, vals[1], jnp.where(grp == 2, vals[2], vals[3])))

    first = pick([0] + edges[:-1])
    return grp, total, ii == first, pick(fills) - (ii - first) * FFN_BM


def _ffn_group_kernel(tab_ref, xs_ref, w1_ref, w3_ref, w2_ref, ys_ref, w1b_ref, w3b_ref, w2b_ref, *, meta):
    i = pl.program_id(0)
    _, total, first_of_group, valid = _ffn_lookup(i, tab_ref, meta)
    active = i < total

    @pl.when(jnp.logical_and(active, first_of_group))
    def _():
        w1b_ref[...] = w1_ref[...].astype(BF16)
        w3b_ref[...] = w3_ref[...].astype(BF16)
        w2b_ref[...] = w2_ref[...].astype(BF16)

    def run(rows):
        hb = xs_ref[0:rows, 0:D_MODEL].astype(BF16)
        ex = xs_ref[0:rows, D_MODEL:XS_W]
        acc = None
        for e in range(EXPERTS_PER_GROUP):
            a = _bdot(hb, w1b_ref[e])
            b = _bdot(hb, w3b_ref[e])
            gate = ex[:, e:e + 1] + ex[:, GATE_LO + e:GATE_LO + e + 1]
            hid = (a * jax.nn.sigmoid(a)) * b * gate
            part = _bdot(hid.astype(BF16), w2b_ref[e])
            acc = part if acc is None else acc + part
        ys_ref[0:rows, :] = acc

    @pl.when(jnp.logical_and(active, valid > FFN_HALF))
    def _():
        run(FFN_BM)

    @pl.when(jnp.logical_and(active, valid <= FFN_HALF))
    def _():
        run(FFN_HALF)
        ys_ref[FFN_HALF:, :] = jnp.zeros((FFN_BM - FFN_HALF, D_MODEL), F32)

    @pl.when(jnp.logical_not(active))
    def _():
        ys_ref[...] = jnp.zeros_like(ys_ref)


def _ffn_group(tab, xs, w1, w3, w2, layer, meta):
    e4 = EXPERTS_PER_GROUP
    n_blocks = xs.shape[0] // FFN_BM
    group_of = lambda i, tab_ref: _ffn_lookup(i, tab_ref, meta)[0]
    grid_spec = pltpu.PrefetchScalarGridSpec(
        num_scalar_prefetch=1,
        grid=(n_blocks,),
        in_specs=[
            pl.BlockSpec((FFN_BM, XS_W), lambda i, tab_ref: (i, 0)),
            pl.BlockSpec((None, e4, D_MODEL, D_EXPERT), lambda i, tab_ref: (layer, group_of(i, tab_ref), 0, 0)),
            pl.BlockSpec((None, e4, D_MODEL, D_EXPERT), lambda i, tab_ref: (layer, group_of(i, tab_ref), 0, 0)),
            pl.BlockSpec((None, e4, D_EXPERT, D_MODEL), lambda i, tab_ref: (layer, group_of(i, tab_ref), 0, 0)),
        ],
        out_specs=pl.BlockSpec((FFN_BM, D_MODEL), lambda i, tab_ref: (i, 0)),
        scratch_shapes=[
            pltpu.VMEM((e4, D_MODEL, D_EXPERT), BF16),
            pltpu.VMEM((e4, D_MODEL, D_EXPERT), BF16),
            pltpu.VMEM((e4, D_EXPERT, D_MODEL), BF16),
        ],
    )
    return pl.pallas_call(
        functools.partial(_ffn_group_kernel, meta=meta),
        grid_spec=grid_spec,
        out_shape=jax.ShapeDtypeStruct((xs.shape[0], D_MODEL), F32),
        compiler_params=_cparams("arbitrary"),
        name="moe_ffn",
    )(tab, xs, w1, w3, w2)


def _combine_kernel(tab_ref, xp_ref, xs_ref, mod_ref, dloc_ref, fg_ref, ys_ref, op_ref, os_ref, runs_ref, sem_ref,
                    *, n_steps, n_p_steps, final_norm):
    i = pl.program_id(0)
    par = lax.rem(i, 2)
    tm = ROUTE_TM

    def copies(step, parity, wait):
        for sub in range(ROUTE_SUBS):
            _run_copies(tab_ref, tile=step * ROUTE_SUBS + sub, hbm_ref=ys_ref, vmem_ref=runs_ref.at[parity, sub],
                        sem=sem_ref.at[parity, sub], to_hbm=False, wait=wait)

    @pl.when(i == 0)
    def _():
        copies(0, 0, wait=False)

    @pl.when(i + 1 < n_steps)
    def _():
        copies(i + 1, 1 - par, wait=False)

    for sub in range(ROUTE_SUBS):
        _wait_rows(_tile_rows(tab_ref, i * ROUTE_SUBS + sub), ys_ref, runs_ref.at[par, sub], sem_ref.at[par, sub], False)
    row_f = lax.broadcasted_iota(jnp.int32, (tm, SORT_ROWS), 1).astype(F32)
    parts = []
    for sub in range(ROUTE_SUBS):
        tile = i * ROUTE_SUBS + sub
        covered = 0
        for g in range(N_GROUPS_MOE):
            covered = covered + tab_ref[tile * TAB_W + N_GROUPS_MOE + g]
        runs_ref[par, sub, pl.ds(pl.multiple_of(covered, ROUTE_PAD), LANES), :] = jnp.zeros((LANES, D_MODEL), F32)
        yb = runs_ref[par, sub, 0:SORT_ROWS, :].astype(BF16)
        pt = jnp.where(row_f == dloc_ref[sub * tm:(sub + 1) * tm, 0:1], 1.0, 0.0).astype(BF16)
        parts.append(_bdot(pt, yb))
    is_prompt = i < n_p_steps
    x = jnp.where(is_prompt, xp_ref[...], xs_ref[...])
    x2 = x + mod_ref[5:6, :] * jnp.concatenate(parts, axis=0)
    if final_norm:
        x2 = _rms(x2, fg_ref[...])

    @pl.when(is_prompt)
    def _():
        op_ref[...] = x2

    @pl.when(jnp.logical_not(is_prompt))
    def _():
        os_ref[...] = x2


def _combine(tab, xp, xs, mod, dloc, final_g, ys, layer, sample_seq, final_norm):
    n_p_steps = xp.shape[0] // ROUTE_STEP
    n_s_steps = xs.shape[0] // ROUTE_STEP
    n_steps = n_p_steps + n_s_steps
    step_of = lambda i, tab_ref: i
    p_spec, s_spec, mod_spec = _merged_specs(n_p_steps, n_s_steps, layer, sample_seq, step_of)
    grid_spec = pltpu.PrefetchScalarGridSpec(
        num_scalar_prefetch=1,
        grid=(n_steps,),
        in_specs=[
            p_spec, s_spec, mod_spec,
            pl.BlockSpec((ROUTE_STEP, LANES), lambda i, tab_ref: (i, 0)),
            pl.BlockSpec((1, D_MODEL), lambda i, tab_ref: (0, 0)),
            pl.BlockSpec(memory_space=pl.ANY),
        ],
        out_specs=[p_spec, s_spec],
        scratch_shapes=[
            pltpu.VMEM((2, ROUTE_SUBS, RUN_ROWS, D_MODEL), F32),
            pltpu.SemaphoreType.DMA((2, ROUTE_SUBS)),
        ],
    )
    return pl.pallas_call(
        functools.partial(_combine_kernel, n_steps=n_steps, n_p_steps=n_p_steps, final_norm=final_norm),
        grid_spec=grid_spec,
        out_shape=[jax.ShapeDtypeStruct(xp.shape, F32), jax.ShapeDtypeStruct(xs.shape, F32)],
        compiler_params=_cparams("arbitrary"),
        name="moe_combine",
    )(tab, xp, xs, mod, dloc, final_g, ys)


def _moe(xp, xs, mod, g2, w_r, b_r, w1, w3, w2, final_g, layer, sample_seq, final_norm):
    t = xp.shape[0] + xs.shape[0]
    n_tiles = t // ROUTE_TM
    max_rows = t + N_GROUPS_MOE * (ROUTE_PAD - 1) * n_tiles
    n_rows = (-(-max_rows // FFN_BM) + N_GROUPS_MOE) * FFN_BM
    dloc, tab, sorted_x = _route(xp, xs, mod, g2, w_r, b_r, layer, sample_seq, n_rows)
    ys = _ffn_group(tab, sorted_x, w1, w3, w2, layer, n_tiles * TAB_W)
    return _combine(tab, xp, xs, mod, dloc, final_g, ys, layer, sample_seq, final_norm)


def _rope_tables(seq):
    half = QK_ROPE // 2
    nf = half // 2
    inv = ROPE_BASE ** (-np.arange(nf, dtype=np.float64) / nf)
    pos = np.arange(seq)
    row = (pos // GRID_W).astype(np.float64)
    col = (pos % GRID_W).astype(np.float64)
    cos = np.ones((seq, HEAD_PAD), np.float64)
    sin_a = np.zeros((seq, HEAD_PAD), np.float64)
    sin_b = np.zeros((seq, HEAD_PAD), np.float64)
    for part, p in enumerate((row, col)):
        ang = p[:, None] * inv[None, :]
        base = ROPE_OFF + part * half
        cos[:, base:base + nf] = np.cos(ang)
        cos[:, base + nf:base + half] = np.cos(ang)
        sin_a[:, base:base + nf] = -np.sin(ang)
        sin_b[:, base + nf:base + half] = np.sin(ang)
    return tuple(jnp.asarray(a, F32) for a in (cos, sin_a, sin_b))


def _apply_rope(x, cos, sin_a, sin_b, reps):
    nf = QK_ROPE // 4
    width = x.shape[1]
    if reps > 1:
        cos, sin_a, sin_b = (jnp.concatenate([a] * reps, axis=1) for a in (cos, sin_a, sin_b))
    return x * cos + pltpu.roll(x, width - nf, axis=1) * sin_a + pltpu.roll(x, nf, axis=1) * sin_b


def _odd_in_kernel(*refs, rope, emit_cache):
    x_ref, mod_ref, g_ref, w_ref, qg_ref, wq_ref, kg_ref, wk_ref, wv_ref, cs_ref = refs[:10]
    refs = refs[10:]
    if rope:
        cos_ref, sa_ref, sb_ref = refs[:3]
        refs = refs[3:]
    y_ref, q_ref, k_ref, v_ref = refs[:4]
    refs = refs[4:]
    m = mod_ref[...]
    h = _rms(x_ref[...], g_ref[...]) * (1.0 + m[1:2]) + m[0:1]
    z = _bdot(h.astype(BF16), w_ref[...])
    zc = z[:, 0:D_C]
    qc = z[:, D_C:D_C + Q_LORA]
    kvc = z[:, D_C + Q_LORA:D_C + Q_LORA + KV_LORA]
    kpe = z[:, D_C + Q_LORA + KV_LORA:]
    q = _bdot(_rms(qc, qg_ref[...]).astype(BF16), wq_ref[...])
    kvn = _rms(kvc, kg_ref[...])
    if emit_cache:
        ckv_ref, kpe_ref = refs
        ckv_ref[...] = kvn
        kpe_ref[...] = kpe[:, ROPE_OFF:ROPE_OFF + QK_ROPE]
    if rope:
        tabs = (cos_ref[...], sa_ref[...], sb_ref[...])
        q = _apply_rope(q, *tabs, reps=N_HEADS)
        kpe = _apply_rope(kpe, *tabs, reps=1)
    kvb = kvn.astype(BF16)
    k = _bdot(kvb, wk_ref[...]) + jnp.concatenate([kpe] * N_HEADS, axis=1)
    scale = math.log2(math.e) / math.sqrt(QK_NOPE + QK_ROPE)
    q_ref[...] = (q * scale).astype(BF16)
    k_ref[...] = k.astype(BF16)
    v_ref[...] = _bdot(kvb, wv_ref[...]).astype(BF16)
    y = _bdot(zc.astype(BF16), cs_ref[...])
    y_ref[0, :, :] = y[:, 0:D_C].astype(BF16)
    y_ref[1, :, :] = y[:, D_C:2 * D_C].astype(BF16)


def _odd_in(x3, mod, g, w_in, q_g, w_q, kv_g, w_k, w_v, cs, stream, layer, rope_tabs, emit_cache):
    b, s, _ = x3.shape
    tm = min(s, 512)
    n_i = s // tm
    rope = rope_tabs is not None
    const = lambda a: pl.BlockSpec(a.shape, lambda bi, i: (0,) * a.ndim)
    in_specs = [
        pl.BlockSpec((None, tm, D_MODEL), lambda bi, i: (bi, i, 0)),
        _mod_spec(stream, layer),
        const(g), const(w_in), const(q_g), const(w_q), const(kv_g), const(w_k), const(w_v), const(cs),
    ]
    args = [x3, mod, g, w_in, q_g, w_q, kv_g, w_k, w_v, cs]
    if rope:
        in_specs += [pl.BlockSpec((tm, HEAD_PAD), lambda bi, i: (i, 0))] * 3
        args += list(rope_tabs)
    hq = N_HEADS * HEAD_PAD
    out_specs = [
        pl.BlockSpec((None, 2, tm, D_C), lambda bi, i: (bi, 0, i, 0)),
        pl.BlockSpec((None, tm, hq), lambda bi, i: (bi, i, 0)),
        pl.BlockSpec((None, tm, hq), lambda bi, i: (bi, i, 0)),
        pl.BlockSpec((None, tm, N_HEADS * V_DIM), lambda bi, i: (bi, i, 0)),
    ]
    out_shape = [
        jax.ShapeDtypeStruct((b, 2, s, D_C), BF16),
        jax.ShapeDtypeStruct((b, s, hq), BF16),
        jax.ShapeDtypeStruct((b, s, hq), BF16),
        jax.ShapeDtypeStruct((b, s, N_HEADS * V_DIM), BF16),
    ]
    if emit_cache:
        out_specs += [
            pl.BlockSpec((None, tm, KV_LORA), lambda bi, i: (bi, i, 0)),
            pl.BlockSpec((None, tm, QK_ROPE), lambda bi, i: (bi, i, 0)),
        ]
        out_shape += [
            jax.ShapeDtypeStruct((b, s, KV_LORA), F32),
            jax.ShapeDtypeStruct((b, s, QK_ROPE), F32),
        ]
    return pl.pallas_call(
        functools.partial(_odd_in_kernel, rope=rope, emit_cache=emit_cache),
        grid=(b, n_i),
        in_specs=in_specs,
        out_specs=out_specs,
        out_shape=out_shape,
        compiler_params=_cparams("parallel", "parallel"),
        name="odd_in",
    )(*args)


def _cache_kv_kernel(c_ref, p_ref, wk_ref, wv_ref, k_ref, v_ref):
    cb = c_ref[...].astype(BF16)
    k = _bdot(cb, wk_ref[...]) + jnp.concatenate([p_ref[...]] * N_HEADS, axis=1)
    k_ref[...] = k.astype(BF16)
    v_ref[...] = _bdot(cb, wv_ref[...]).astype(BF16)


def _cache_kv(ckv, kpe_blk, w_k, w_v):
    b, p, _ = ckv.shape
    hq = N_HEADS * HEAD_PAD
    return pl.pallas_call(
        _cache_kv_kernel,
        grid=(b,),
        in_specs=[
            pl.BlockSpec((None, p, KV_LORA), lambda bi: (bi, 0, 0)),
            pl.BlockSpec((None, p, HEAD_PAD), lambda bi: (bi, 0, 0)),
            pl.BlockSpec(w_k.shape, lambda bi: (0, 0)),
            pl.BlockSpec(w_v.shape, lambda bi: (0, 0)),
        ],
        out_specs=[
            pl.BlockSpec((None, p, hq), lambda bi: (bi, 0, 0)),
            pl.BlockSpec((None, p, N_HEADS * V_DIM), lambda bi: (bi, 0, 0)),
        ],
        out_shape=[
            jax.ShapeDtypeStruct((b, p, hq), BF16),
            jax.ShapeDtypeStruct((b, p, N_HEADS * V_DIM), BF16),
        ],
        compiler_params=_cparams("parallel"),
        name="cache_kv",
    )(ckv, kpe_blk, w_k, w_v)


def _odd_mix_kernel(*refs, with_cache):
    q_ref, k_ref, v_ref = refs[:3]
    refs = refs[3:]
    if with_cache:
        kc_ref, vc_ref = refs[:2]
        refs = refs[2:]
    y_ref, f_ref, x_ref, mod_ref, wo_ref, o_ref, a_ref = refs
    tq = q_ref.shape[0]
    lane = lax.broadcasted_iota(jnp.int32, (tq, 2 * V_DIM), 1)
    for pair in range(N_HEADS // 2):
        vcols = slice(pair * 2 * V_DIM, (pair + 1) * 2 * V_DIM)
        outs = []
        for h in (2 * pair, 2 * pair + 1):
            hcols = slice(h * HEAD_PAD, (h + 1) * HEAD_PAD)
            qh = q_ref[:, hcols]
            s = lax.dot_general(qh, k_ref[:, hcols], NT_DIMS, preferred_element_type=F32)
            top = jnp.max(s, axis=-1, keepdims=True)
            if with_cache:
                sc = lax.dot_general(qh, kc_ref[:, hcols], NT_DIMS, preferred_element_type=F32)
                top = jnp.maximum(top, jnp.max(sc, axis=-1, keepdims=True))
            p = jnp.exp2(s - top)
            den = jnp.sum(p, axis=-1, keepdims=True)
            o = _bdot(p.astype(BF16), v_ref[:, vcols])
            if with_cache:
                pc = jnp.exp2(sc - top)
                den = den + jnp.sum(pc, axis=-1, keepdims=True)
                o = o + _bdot(pc.astype(BF16), vc_ref[:, vcols])
            outs.append(o / den)
        a_ref[:, vcols] = jnp.where(lane < V_DIM, outs[0], outs[1]).astype(BF16)
    f = _bdot(f_ref[...], y_ref[...])
    o = _bdot(f.astype(BF16), wo_ref[0:D_C, :]) + _bdot(a_ref[...], wo_ref[D_C:, :])
    o_ref[...] = x_ref[...] + mod_ref[2:3, :] * o


def _odd_mix(q, k, v, kc, vc, y, fmat, x3, mod, w_out, stream, layer):
    b, s, hq = q.shape
    tq = min(s, 256)
    n_i = s // tq
    with_cache = kc is not None
    hv = N_HEADS * V_DIM
    mode = dict(pipeline_mode=pl.Buffered(1)) if with_cache else {}

    def per_batch(rows, cols):
        return pl.BlockSpec((None, rows, cols), lambda bi, i: (bi, 0, 0), **mode)

    in_specs = [pl.BlockSpec((None, tq, hq), lambda bi, i: (bi, i, 0)), per_batch(s, hq), per_batch(s, hv)]
    args = [q, k, v]
    if with_cache:
        p = kc.shape[1]
        in_specs += [per_batch(p, hq), per_batch(p, hv)]
        args += [kc, vc]
    in_specs += [
        per_batch(2 * s, D_C),
        pl.BlockSpec((tq, 2 * s), lambda bi, i: (i, 0)),
        pl.BlockSpec((None, tq, D_MODEL), lambda bi, i: (bi, i, 0)),
        _mod_spec(stream, layer),
        pl.BlockSpec(w_out.shape, lambda bi, i: (0, 0), **mode),
    ]
    args += [y, fmat, x3, mod, w_out]
    return pl.pallas_call(
        functools.partial(_odd_mix_kernel, with_cache=with_cache),
        grid=(b, n_i),
        in_specs=in_specs,
        out_specs=pl.BlockSpec((None, tq, D_MODEL), lambda bi, i: (bi, i, 0)),
        out_shape=jax.ShapeDtypeStruct((b, s, D_MODEL), F32),
        scratch_shapes=[pltpu.VMEM((tq, hv), BF16)],
        compiler_params=_cparams("parallel", "arbitrary"),
        name="odd_mix",
    )(*args)


def _dft_tables(seq):
    jc = np.arange(C_GW)
    ang_c = 2.0 * np.pi * np.outer(jc, jc) / C_GW
    eye = np.eye(C_GROUPS)
    cs = np.concatenate([np.kron(eye, np.cos(ang_c)), np.kron(eye, np.sin(ang_c))], axis=1)
    jn = np.arange(seq)
    ang_n = 2.0 * np.pi * (np.outer(jn, jn) % seq) / seq
    scale = 1.0 / math.sqrt(seq * C_GW)
    fmat = np.concatenate([np.cos(ang_n), -np.sin(ang_n)], axis=1) * scale
    return jnp.asarray(cs, F32).astype(BF16), jnp.asarray(fmat, F32).astype(BF16)


def _odd_weights(w_in, w_uq, w_ukv):
    d = w_in.shape[0]
    base = D_C + Q_LORA + KV_LORA
    kpe_blk = jnp.zeros((d, HEAD_PAD), w_in.dtype).at[:, ROPE_OFF:ROPE_OFF + QK_ROPE].set(w_in[:, base:])
    w_in_p = jnp.concatenate([w_in[:, :base], kpe_blk], axis=1).astype(BF16)
    qh = w_uq.reshape(Q_LORA, N_HEADS, QK_NOPE + QK_ROPE)
    w_q = jnp.pad(qh, ((0, 0), (0, 0), (0, HEAD_PAD - QK_NOPE - QK_ROPE))).reshape(Q_LORA, -1).astype(BF16)
    kvh = w_ukv.reshape(KV_LORA, N_HEADS, QK_NOPE + V_DIM)
    w_k = jnp.pad(kvh[:, :, :QK_NOPE], ((0, 0), (0, 0), (0, HEAD_PAD - QK_NOPE))).reshape(KV_LORA, -1)
    w_v = kvh[:, :, QK_NOPE:].reshape(KV_LORA, -1)
    return w_in_p, w_q, w_k.astype(BF16), w_v.astype(BF16)


ROUTER_ROWS = 32


def _router_weights(wg, bg, we, be):
    d = wg.shape[0]
    w = jnp.concatenate([wg, we.reshape(d, N_EXPERTS)], axis=1).T
    w = jnp.pad(w, ((0, ROUTER_ROWS - w.shape[0]), (0, 0))).astype(BF16)
    b = jnp.concatenate([bg, be.reshape(N_EXPERTS)])
    b = jnp.pad(b, (0, ROUTER_ROWS - b.shape[0])).reshape(ROUTER_ROWS, 1).astype(F32)
    return w, b


def kernel(x_prompt, x_sample, cache_ckv, cache_kpe, c, c_ctx, mod_w, mod_b, norm1_g, norm2_g,
           ev_w_in, ev_conv_w, ev_sgu_norm_g, ev_sgu_w, ev_sgu_b, ev_w_out,
           od_w_in, od_q_norm_g, od_w_uq, od_kv_norm_g, od_w_ukv, od_w_out,
           moe_wg, moe_bg, moe_we, moe_be, moe_w1, moe_w3, moe_w2, final_norm_g):
    bp, n_p, d = x_prompt.shape
    bs, n_s, _ = x_sample.shape
    streams = [(_Stream(bp, n_p, True), x_prompt), (_Stream(bs, n_s, False), x_sample)]

    n_rows = 1 + bs
    cond_t = jnp.concatenate([c_ctx[None, :], c], axis=0).T
    mod = _adaln(cond_t, mod_w, mod_b, n_rows)

    final_g = final_norm_g.reshape(1, d)
    xs = [x for _, x in streams]
    new_ckv, new_kpe = [], []
    for l in range(DEPTH):
        j = l // 2
        g1 = norm1_g[l].reshape(1, d)
        g2 = norm2_g[l].reshape(1, d)
        w_r, b_r = _router_weights(moe_wg[l], moe_bg[l], moe_we[l], moe_be[l])
        last = l == DEPTH - 1
        if l % 2 == 0:
            w_in = ev_w_in[j].astype(BF16)
            w_out = ev_w_out[j].astype(BF16)
            sgu_w = ev_sgu_w[j].astype(BF16)
            sgu_g = ev_sgu_norm_g[j].reshape(1, D_B)
            sgu_bias = jnp.repeat(ev_sgu_b[j].T, D_B // B_GROUPS, axis=1)
            for si, (st, _) in enumerate(streams):
                xs[si] = _even_layer(xs[si], mod, g1, w_in, ev_conv_w[j], sgu_g, sgu_w, sgu_bias, w_out, st, l)
        else:
            w_in, w_q, w_k, w_v = _odd_weights(od_w_in[j], od_w_uq[j], od_w_ukv[j])
            w_out = od_w_out[j].astype(BF16)
            q_g = od_q_norm_g[j].reshape(1, Q_LORA)
            kv_g = od_kv_norm_g[j].reshape(1, KV_LORA)
            for si, (st, _) in enumerate(streams):
                x3 = xs[si]
                cs, fmat = _dft_tables(st.seq)
                is_prompt = st.shared_cond
                tabs = None if is_prompt else _rope_tables(st.seq)
                outs = _odd_in(x3, mod, g1, w_in, q_g, w_q, kv_g, w_k, w_v, cs, st, l, tabs, is_prompt)
                y, q, k, v = outs[:4]
                if is_prompt:
                    new_ckv.append(outs[4])
                    new_kpe.append(outs[5])
                    kc = vc = None
                else:
                    kpe_blk = jnp.pad(cache_kpe[:, j], ((0, 0), (0, 0), (ROPE_OFF, HEAD_PAD - ROPE_OFF - QK_ROPE)))
                    kc, vc = _cache_kv(cache_ckv[:, j], kpe_blk, w_k, w_v)
                xs[si] = _odd_mix(q, k, v, kc, vc, y.reshape(st.batch, 2 * st.seq, D_C), fmat, x3, mod,
                                  w_out, st, l)
        x2p, x2s = _moe(xs[0].reshape(bp * n_p, d), xs[1].reshape(bs * n_s, d), mod, g2, w_r, b_r,
                        moe_w1, moe_w3, moe_w2, final_g, l, n_s, last)
        xs = [x2p.reshape(bp, n_p, d), x2s.reshape(bs, n_s, d)]
    return (xs[0], xs[1], jnp.stack(new_ckv, axis=1), jnp.stack(new_kpe, axis=1))
```

```python
import functools
import math

import numpy as np
import jax
import jax.numpy as jnp
from jax import lax
from jax.experimental import pallas as pl
from jax.experimental.pallas import tpu as pltpu

D_MODEL = 1024
DEPTH = 2
GRID_W = 64
D_A = D_MODEL // 2
D_B = D_MODEL // 2
B_GROUPS = 4
CHUNK = 128
D_EVEN_IN = 3 * D_A + 2 * D_B
D_C = D_MODEL // 4
C_GROUPS = 4
C_GW = D_C // C_GROUPS
N_HEADS = 12
QK_NOPE = 64
QK_ROPE = 32
V_DIM = 64
Q_LORA = 384
KV_LORA = 256
ROPE_BASE = 10000.0
N_GROUPS_MOE = 4
EXPERTS_PER_GROUP = 4
N_EXPERTS = N_GROUPS_MOE * EXPERTS_PER_GROUP
D_EXPERT = 256
EPS = 1e-6

LANES = 128
HEAD_PAD = 128
ROPE_OFF = QK_NOPE
GATE_OFF = N_GROUPS_MOE
NEG_BIG = -1e30
F32 = jnp.float32
BF16 = jnp.bfloat16
VMEM_LIMIT = 56 * 1024 * 1024


def _cparams(*sem):
    return pltpu.CompilerParams(dimension_semantics=sem, vmem_limit_bytes=VMEM_LIMIT)


def _rms(x, g):
    return x * lax.rsqrt(jnp.mean(x * x, axis=-1, keepdims=True) + EPS) * g


def _bdot(a, b):
    return jnp.dot(a, b, preferred_element_type=F32)


NT_DIMS = (((1,), (1,)), ((), ()))
TN_DIMS = (((0,), (0,)), ((), ()))


def _mod_kernel(ct_ref, w_ref, b_ref, o_ref):
    c = ct_ref[...]
    s = c * jax.nn.sigmoid(c)
    w = w_ref[...]
    b = b_ref[...]
    for r in range(o_ref.shape[0]):
        o_ref[r:r + 1, :] = jnp.sum(s[:, r:r + 1] * w, axis=0, keepdims=True) + b


def _adaln(cond_t, mod_w, mod_b, n_rows):
    nt = 1024
    d6 = mod_w.shape[-1]
    out = pl.pallas_call(
        _mod_kernel,
        grid=(DEPTH, d6 // nt),
        in_specs=[
            pl.BlockSpec(cond_t.shape, lambda l, n: (0, 0)),
            pl.BlockSpec((None, D_MODEL, nt), lambda l, n: (l, 0, n)),
            pl.BlockSpec((None, 1, nt), lambda l, n: (l, 0, n)),
        ],
        out_specs=pl.BlockSpec((None, n_rows, nt), lambda l, n: (l, 0, n)),
        out_shape=jax.ShapeDtypeStruct((DEPTH, n_rows, d6), F32),
        compiler_params=_cparams("parallel", "parallel"),
        name="adaln",
    )(cond_t, mod_w, mod_b.reshape(DEPTH, 1, d6))
    return out.reshape(DEPTH, n_rows, 6, D_MODEL)


class _Stream:
    def __init__(self, batch, seq, shared_cond):
        self.batch = batch
        self.seq = seq
        self.tokens = batch * seq
        self.shared_cond = shared_cond

    def row_of_batch(self, b):
        return 0 if self.shared_cond else b + 1

    def row_of_tile(self, i, tm):
        return 0 if self.shared_cond else (i * tm) // self.seq + 1


def _mod_spec(stream, layer, tm=None):
    if tm is None:
        return pl.BlockSpec((None, None, 6, D_MODEL), lambda b, i: (layer, stream.row_of_batch(b), 0, 0))
    return pl.BlockSpec((None, None, 6, D_MODEL), lambda i, *_: (layer, stream.row_of_tile(i, tm), 0, 0))


HALO = 8


def _even_kernel(*refs, has_halo, nc):
    if has_halo:
        x_ref, xp_ref, xn_ref = refs[:3]
        refs = refs[3:]
    else:
        x_ref = refs[0]
        refs = refs[1:]
    mod_ref, g_ref, wi_ref, cw_ref, sg_ref, sw_ref, sb_ref, wo_ref, o_ref, z_ref, y_ref = refs
    i = pl.program_id(1)
    n_i = pl.num_programs(1)
    ts = x_ref.shape[0]
    m = mod_ref[...]
    g = g_ref[...]

    def modulate(x):
        return (_rms(x, g) * (1.0 + m[1:2]) + m[0:1]).astype(BF16)

    x = x_ref[...]
    hb = modulate(x)
    for n in range(D_EVEN_IN // nc):
        z_ref[:, n * nc:(n + 1) * nc] = _bdot(hb, wi_ref[:, n * nc:(n + 1) * nc]).astype(BF16)

    gate_b = z_ref[:, 0:D_A].astype(F32)
    gate_c = z_ref[:, D_A:2 * D_A].astype(F32)
    xa = z_ref[:, 2 * D_A:3 * D_A].astype(F32)
    t = gate_c * xa
    t_prev = pltpu.roll(t, 1, axis=0)
    t_next = pltpu.roll(t, ts - 1, axis=0)
    row = lax.broadcasted_iota(jnp.int32, (ts, 1), 0)
    if has_halo:
        hh = modulate(jnp.concatenate([xp_ref[...], xn_ref[...]], axis=0))
        zh = _bdot(hh, wi_ref[:, D_A:3 * D_A]).astype(BF16).astype(F32)
        th = zh[:, 0:D_A] * zh[:, D_A:2 * D_A]
        tp = th[HALO - 1:HALO] * (i > 0).astype(F32)
        tn = th[HALO:HALO + 1] * (i < n_i - 1).astype(F32)
    else:
        tp = tn = 0.0
    t_prev = jnp.where(row == 0, tp, t_prev)
    t_next = jnp.where(row == ts - 1, tn, t_next)
    cw = cw_ref[...]
    y_a = gate_b * (t_prev * cw[0:1] + t * cw[1:2] + t_next * cw[2:3])
    y_ref[:, 0:D_A] = y_a.astype(BF16)

    u = z_ref[:, 3 * D_A:3 * D_A + D_B].astype(F32)
    v = z_ref[:, 3 * D_A + D_B:3 * D_A + 2 * D_B].astype(F32)
    vb = _rms(v, sg_ref[...]).astype(BF16)
    gw = D_B // B_GROUPS
    for c in range(ts // CHUNK):
        rows = slice(c * CHUNK, (c + 1) * CHUNK)
        for gi in range(B_GROUPS):
            cols = slice(gi * gw, (gi + 1) * gw)
            sv = _bdot(sw_ref[gi], vb[rows, cols]) + sb_ref[:, cols]
            y_ref[rows, D_A + gi * gw:D_A + (gi + 1) * gw] = (u[rows, cols] * sv).astype(BF16)

    o_ref[...] = x + m[2:3] * _bdot(y_ref[...], wo_ref[...])


def _even_layer(x3, mod, g, w_in, conv_w, sgu_g, sgu_w, sgu_bias, w_out, stream, layer):
    b, s, _ = x3.shape
    ts = min(s, 256)
    n_i = s // ts
    has_halo = n_i > 1
    hb = ts // HALO
    last_h = s // HALO - 1
    const = lambda a: pl.BlockSpec(a.shape, lambda bi, i: (0,) * a.ndim)
    in_specs = [pl.BlockSpec((None, ts, D_MODEL), lambda bi, i: (bi, i, 0))]
    args = [x3]
    if has_halo:
        in_specs += [
            pl.BlockSpec((None, HALO, D_MODEL), lambda bi, i: (bi, jnp.maximum(i * hb - 1, 0), 0)),
            pl.BlockSpec((None, HALO, D_MODEL), lambda bi, i: (bi, jnp.minimum((i + 1) * hb, last_h), 0)),
        ]
        args += [x3, x3]
    in_specs += [_mod_spec(stream, layer), const(g), const(w_in), const(conv_w), const(sgu_g), const(sgu_w),
                 const(sgu_bias), const(w_out)]
    args += [mod, g, w_in, conv_w, sgu_g, sgu_w, sgu_bias, w_out]
    return pl.pallas_call(
        functools.partial(_even_kernel, has_halo=has_halo, nc=512),
        grid=(b, n_i),
        in_specs=in_specs,
        out_specs=pl.BlockSpec((None, ts, D_MODEL), lambda bi, i: (bi, i, 0)),
        out_shape=jax.ShapeDtypeStruct((b, s, D_MODEL), F32),
        scratch_shapes=[pltpu.VMEM((ts, D_EVEN_IN), BF16), pltpu.VMEM((ts, D_A + D_B), BF16)],
        compiler_params=_cparams("parallel", "parallel"),
        name="even_layer",
    )(*args)


ROUTE_TM = 256
ROUTE_SUBS = 2
ROUTE_STEP = ROUTE_TM * ROUTE_SUBS
ROUTE_PAD = 8
SORT_ROWS = ROUTE_TM + LANES
RUN_ROWS = SORT_ROWS + 32
XS_W = D_MODEL + LANES
GATE_LO = EXPERTS_PER_GROUP
DLOC_HI = 2 * EXPERTS_PER_GROUP
DLOC_RADIX = 16.0
FFN_BM = 512
FFN_HALF = FFN_BM // 2
RUN_SIZES = (256, 128, 64, 32, 16, 8)
TAB_W = 2 * N_GROUPS_MOE


def _round_up(x, m):
    return lax.div(x + (m - 1), m) * m


def _run_copies(tab_ref, tile, hbm_ref, vmem_ref, sem, to_hbm, wait):
    off = 0
    for g in range(N_GROUPS_MOE):
        start = tab_ref[tile * TAB_W + g]
        n = tab_ref[tile * TAB_W + N_GROUPS_MOE + g]
        for p in RUN_SIZES:
            done = n & (-2 * p)

            @pl.when((n & p) != 0)
            def _():
                v = vmem_ref.at[pl.ds(pl.multiple_of(off + done, ROUTE_PAD), p)]
                h = hbm_ref.at[pl.ds(pl.multiple_of(start + done, ROUTE_PAD), p)]
                cp = pltpu.make_async_copy(v, h, sem) if to_hbm else pltpu.make_async_copy(h, v, sem)
                if wait:
                    cp.wait()
                else:
                    cp.start()
        off = off + n


def _zero_fill(tab_ref, meta, zeros_ref, hbm_ref, sem, n_rows, wait):
    def copy(rows, dst_row):
        cp = pltpu.make_async_copy(zeros_ref.at[pl.ds(0, rows)],
                                   hbm_ref.at[pl.ds(pl.multiple_of(dst_row, ROUTE_PAD), rows)], sem)
        if wait:
            cp.wait()
        else:
            cp.start()

    end = 0
    for g in range(N_GROUPS_MOE):
        fill = tab_ref[meta + g]
        start = tab_ref[meta + N_GROUPS_MOE + g]
        end = start + _round_up(fill, FFN_BM)
        tail = end - start - fill
        for p in RUN_SIZES:
            pl.when((tail & p) != 0)(functools.partial(copy, p, start + fill + (tail & (-2 * p))))
    for k in range(n_rows // FFN_BM):
        pl.when(end + k * FFN_BM < n_rows)(functools.partial(copy, FFN_BM, end + k * FFN_BM))


def _wait_rows(n, hbm_ref, vmem_ref, sem, to_hbm):
    for p in RUN_SIZES:
        @pl.when((n & p) != 0)
        def _():
            v = vmem_ref.at[pl.ds(0, p)]
            h = hbm_ref.at[pl.ds(0, p)]
            (pltpu.make_async_copy(v, h, sem) if to_hbm else pltpu.make_async_copy(h, v, sem)).wait()


def _tile_rows(tab_ref, tile):
    n = 0
    for g in range(N_GROUPS_MOE):
        n = n + tab_ref[tile * TAB_W + N_GROUPS_MOE + g]
    return n


def _max4(v):
    return jnp.maximum(jnp.maximum(v[0], v[1]), jnp.maximum(v[2], v[3]))


def _first_of4(v, top):
    return jnp.where(v[0] == top, 0.0, jnp.where(v[1] == top, 1.0, jnp.where(v[2] == top, 2.0, 3.0)))


def _route_kernel(xp_ref, xs_ref, mod_ref, g_ref, wrt_ref, brt_ref, upper_ref,
                  dloc_ref, tab_ref, sorted_hbm, hbuf_ref, drow_ref, sorted_ref, zeros_ref, cnt_ref, fill_ref,
                  sem_ref, zsem_ref, *, n_steps, n_p_steps):
    sweep = pl.program_id(0)
    i = pl.program_id(1)
    tm = ROUTE_TM
    meta = n_steps * ROUTE_SUBS * TAB_W
    n_rows = sorted_hbm.shape[0]
    ng = N_GROUPS_MOE

    @pl.when(sweep == 0)
    def _():
        @pl.when(i == 0)
        def _():
            for g in range(ng):
                fill_ref[g] = 0

        m = mod_ref[...]
        x = jnp.where(i < n_p_steps, xp_ref[...], xs_ref[...])
        h = _rms(x, g_ref[...]) * (1.0 + m[4:5]) + m[3:4]
        hb = h.astype(BF16)
        lt = lax.dot_general(wrt_ref[...], hb, (((1,), (1,)), ((), ())), preferred_element_type=F32) + brt_ref[...]
        gl = [lt[r:r + 1, :] for r in range(ng)]
        g_top = _max4(gl)
        g_idx = _first_of4(gl, g_top)
        g_w = 1.0 / (jnp.exp(gl[0] - g_top) + jnp.exp(gl[1] - g_top) + jnp.exp(gl[2] - g_top) + jnp.exp(gl[3] - g_top))
        ev = []
        for k in range(EXPERTS_PER_GROUP):
            cand = [lt[GATE_OFF + EXPERTS_PER_GROUP * r + k:GATE_OFF + EXPERTS_PER_GROUP * r + k + 1, :]
                    for r in range(ng)]
            ev.append(jnp.where(g_idx == 0.0, cand[0], jnp.where(g_idx == 1.0, cand[1],
                                jnp.where(g_idx == 2.0, cand[2], cand[3]))))
        v1 = _max4(ev)
        i1 = _first_of4(ev, v1)
        rest = [jnp.where(i1 == float(k), NEG_BIG, ev[k]) for k in range(EXPERTS_PER_GROUP)]
        v2 = _max4(rest)
        i2 = _first_of4(rest, v2)
        e2 = jnp.exp(v2 - v1)
        w1 = 1.0 / (1.0 + e2)
        w2 = e2 * w1
        gates = [g_w * (jnp.where(i1 == float(k), w1, 0.0) + jnp.where(i2 == float(k), w2, 0.0))
                 for k in range(EXPERTS_PER_GROUP)]

        sub8 = lax.broadcasted_iota(jnp.int32, (8, tm), 0).astype(F32)
        dlocs = []
        for sub in range(ROUTE_SUBS):
            gi = g_idx[:, sub * tm:(sub + 1) * tm]
            hot = jnp.where(sub8 == gi, 1.0, 0.0)
            before = _bdot(hot.astype(BF16), upper_ref[...])
            dl = jnp.sum(before * hot, axis=0, keepdims=True)
            off = 0
            for g in range(ng):
                n_g = _round_up(jnp.sum(hot[g:g + 1, :]).astype(jnp.int32), ROUTE_PAD)
                cnt_ref[(i * ROUTE_SUBS + sub) * ng + g] = n_g
                fill_ref[g] = fill_ref[g] + n_g
                dl = dl + jnp.where(gi == float(g), off.astype(F32) if g else 0.0, 0.0)
                off = off + n_g
            drow_ref[i * ROUTE_SUBS + sub] = jnp.broadcast_to(dl, (8, tm))
            dlocs.append(dl)
        dloc = jnp.concatenate(dlocs, axis=1)
        d_hi = jnp.floor(dloc * (1.0 / DLOC_RADIX))
        g_hi = [gt.astype(BF16).astype(F32) for gt in gates]
        ex_rows = g_hi + [gt - gh for gt, gh in zip(gates, g_hi)] + [d_hi, dloc - DLOC_RADIX * d_hi]
        sub16 = lax.broadcasted_iota(jnp.int32, (16, ROUTE_STEP), 0)
        ex_t = jnp.zeros((16, ROUTE_STEP), F32)
        for r, row in enumerate(ex_rows):
            ex_t = jnp.where(sub16 == r, row, ex_t)
        ex_t = jnp.concatenate([ex_t, jnp.zeros((LANES - 16, ROUTE_STEP), F32)], axis=0)
        extras = ex_t.T
        dloc_ref[...] = jnp.broadcast_to(
            DLOC_RADIX * extras[:, DLOC_HI:DLOC_HI + 1] + extras[:, DLOC_HI + 1:DLOC_HI + 2], (ROUTE_STEP, LANES))
        hbuf_ref[pl.ds(pl.multiple_of(i * ROUTE_STEP, ROUTE_STEP), ROUTE_STEP), :] = (
            jnp.concatenate([hb, extras.astype(BF16)], axis=1))

    @pl.when(sweep == 1)
    def _():
        @pl.when(i == 0)
        def _():
            start = 0
            for g in range(ng):
                tab_ref[meta + g] = fill_ref[g]
                tab_ref[meta + ng + g] = start
                start = start + _round_up(fill_ref[g], FFN_BM)
                fill_ref[g] = 0
            zeros_ref[...] = jnp.zeros_like(zeros_ref)

        row_f = lax.broadcasted_iota(jnp.int32, (SORT_ROWS, tm), 0).astype(F32)
        for sub in range(ROUTE_SUBS):
            tile = i * ROUTE_SUBS + sub

            @pl.when(i >= 1)
            def _():
                _wait_rows(_tile_rows(tab_ref, tile - ROUTE_SUBS), sorted_hbm, sorted_ref.at[sub], sem_ref.at[sub], True)

            rows = hbuf_ref[pl.ds(pl.multiple_of(i * ROUTE_STEP + sub * tm, tm), tm), :]
            onehot = jnp.where(row_f == drow_ref[tile][0:1, :], 1.0, 0.0).astype(BF16)
            sorted_ref[sub] = _bdot(onehot, rows)
            for g in range(ng):
                n_g = cnt_ref[tile * ng + g]
                tab_ref[tile * TAB_W + g] = tab_ref[meta + ng + g] + fill_ref[g]
                tab_ref[tile * TAB_W + ng + g] = n_g
                fill_ref[g] = fill_ref[g] + n_g
            _run_copies(tab_ref, tile, sorted_hbm, sorted_ref.at[sub], sem_ref.at[sub], to_hbm=True, wait=False)

            @pl.when(i == n_steps - 1)
            def _():
                _wait_rows(_tile_rows(tab_ref, tile), sorted_hbm, sorted_ref.at[sub], sem_ref.at[sub], True)

        @pl.when(i == n_steps - 1)
        def _():
            _zero_fill(tab_ref, meta, zeros_ref, sorted_hbm, zsem_ref, n_rows, wait=False)
            _zero_fill(tab_ref, meta, zeros_ref, sorted_hbm, zsem_ref, n_rows, wait=True)


def _merged_specs(n_p_steps, n_s_steps, layer, sample_seq, step_of):
    def p_map(*idx):
        return (jnp.minimum(step_of(*idx), n_p_steps - 1), 0)

    def s_map(*idx):
        return (jnp.clip(step_of(*idx) - n_p_steps, 0, n_s_steps - 1), 0)

    def mod_map(*idx):
        j = step_of(*idx)
        row = jnp.where(j < n_p_steps, 0, 1 + lax.div(jnp.maximum(j - n_p_steps, 0) * ROUTE_STEP, sample_seq))
        return (layer, row, 0, 0)

    return (pl.BlockSpec((ROUTE_STEP, D_MODEL), p_map), pl.BlockSpec((ROUTE_STEP, D_MODEL), s_map),
            pl.BlockSpec((None, None, 6, D_MODEL), mod_map))


def _route(xp, xs, mod, g2, w_rt, b_rt, layer, sample_seq, n_rows):
    n_p_steps = xp.shape[0] // ROUTE_STEP
    n_s_steps = xs.shape[0] // ROUTE_STEP
    n_steps = n_p_steps + n_s_steps
    t = n_steps * ROUTE_STEP
    tm = ROUTE_TM
    upper = jnp.asarray(np.triu(np.ones((tm, tm), np.float32), 1), BF16)
    step_of = lambda s, i: jnp.where(s == 0, i, n_steps - 1)
    p_spec, s_spec, mod_spec = _merged_specs(n_p_steps, n_s_steps, layer, sample_seq, step_of)
    const = lambda a: pl.BlockSpec(a.shape, lambda s, i: (0,) * a.ndim)
    n_tiles = n_steps * ROUTE_SUBS
    return pl.pallas_call(
        functools.partial(_route_kernel, n_steps=n_steps, n_p_steps=n_p_steps),
        grid=(2, n_steps),
        in_specs=[p_spec, s_spec, mod_spec, const(g2), const(w_rt), const(b_rt), const(upper)],
        out_specs=[
            pl.BlockSpec((ROUTE_STEP, LANES), lambda s, i: (step_of(s, i), 0)),
            pl.BlockSpec(memory_space=pltpu.SMEM),
            pl.BlockSpec(memory_space=pl.ANY),
        ],
        out_shape=[
            jax.ShapeDtypeStruct((t, LANES), F32),
            jax.ShapeDtypeStruct(((n_tiles + 1) * TAB_W,), jnp.int32),
            jax.ShapeDtypeStruct((n_rows, XS_W), F32),
        ],
        scratch_shapes=[
            pltpu.VMEM((t, XS_W), BF16),
            pltpu.VMEM((n_tiles, 8, tm), F32),
            pltpu.VMEM((ROUTE_SUBS, SORT_ROWS, XS_W), F32),
            pltpu.VMEM((FFN_BM, XS_W), F32),
            pltpu.SMEM((n_tiles * N_GROUPS_MOE,), jnp.int32),
            pltpu.SMEM((N_GROUPS_MOE,), jnp.int32),
            pltpu.SemaphoreType.DMA((ROUTE_SUBS,)),
            pltpu.SemaphoreType.DMA(()),
        ],
        compiler_params=_cparams("arbitrary", "arbitrary"),
        name="moe_route",
    )(xp, xs, mod, g2, w_rt, b_rt, upper)


def _ffn_lookup(i, tab_ref, meta):
    fills = [tab_ref[meta + g] for g in range(N_GROUPS_MOE)]
    edges = []
    acc = 0
    for f in fills:
        acc = acc + lax.div(f + (FFN_BM - 1), FFN_BM)
        edges.append(acc)
    total = edges[-1]
    ii = jnp.minimum(i, total - 1)
    grp = sum((ii >= e).astype(jnp.int32) for e in edges[:-1])

    def pick(vals):
        return jnp.where(grp == 0, vals[0], jnp.where(grp == 1, vals[1], jnp.where(grp == 2, vals[2], vals[3])))

    first = pick([0] + edges[:-1])
    return grp, total, ii == first, pick(fills) - (ii - first) * FFN_BM


def _ffn_group_kernel(tab_ref, xs_ref, w1_ref, w3_ref, w2_ref, ys_ref, w1b_ref, w3b_ref, w2b_ref, *, meta):
    i = pl.program_id(0)
    _, total, first_of_group, valid = _ffn_lookup(i, tab_ref, meta)
    active = i < total

    @pl.when(jnp.logical_and(active, first_of_group))
    def _():
        w1b_ref[...] = w1_ref[...].astype(BF16)
        w3b_ref[...] = w3_ref[...].astype(BF16)
        w2b_ref[...] = w2_ref[...].astype(BF16)

    def run(rows):
        hb = xs_ref[0:rows, 0:D_MODEL].astype(BF16)
        ex = xs_ref[0:rows, D_MODEL:XS_W]
        acc = None
        for e in range(EXPERTS_PER_GROUP):
            a = _bdot(hb, w1b_ref[e])
            b = _bdot(hb, w3b_ref[e])
            gate = ex[:, e:e + 1] + ex[:, GATE_LO + e:GATE_LO + e + 1]
            hid = (a * jax.nn.sigmoid(a)) * b * gate
            part = _bdot(hid.astype(BF16), w2b_ref[e])
            acc = part if acc is None else acc + part
        ys_ref[0:rows, :] = acc

    @pl.when(jnp.logical_and(active, valid > FFN_HALF))
    def _():
        run(FFN_BM)

    @pl.when(jnp.logical_and(active, valid <= FFN_HALF))
    def _():
        run(FFN_HALF)
        ys_ref[FFN_HALF:, :] = jnp.zeros((FFN_BM - FFN_HALF, D_MODEL), F32)

    @pl.when(jnp.logical_not(active))
    def _():
        ys_ref[...] = jnp.zeros_like(ys_ref)


def _ffn_group(tab, xs, w1, w3, w2, layer, meta):
    e4 = EXPERTS_PER_GROUP
    n_blocks = xs.shape[0] // FFN_BM
    group_of = lambda i, tab_ref: _ffn_lookup(i, tab_ref, meta)[0]
    grid_spec = pltpu.PrefetchScalarGridSpec(
        num_scalar_prefetch=1,
        grid=(n_blocks,),
        in_specs=[
            pl.BlockSpec((FFN_BM, XS_W), lambda i, tab_ref: (i, 0)),
            pl.BlockSpec((None, e4, D_MODEL, D_EXPERT), lambda i, tab_ref: (layer, group_of(i, tab_ref), 0, 0)),
            pl.BlockSpec((None, e4, D_MODEL, D_EXPERT), lambda i, tab_ref: (layer, group_of(i, tab_ref), 0, 0)),
            pl.BlockSpec((None, e4, D_EXPERT, D_MODEL), lambda i, tab_ref: (layer, group_of(i, tab_ref), 0, 0)),
        ],
        out_specs=pl.BlockSpec((FFN_BM, D_MODEL), lambda i, tab_ref: (i, 0)),
        scratch_shapes=[
            pltpu.VMEM((e4, D_MODEL, D_EXPERT), BF16),
            pltpu.VMEM((e4, D_MODEL, D_EXPERT), BF16),
            pltpu.VMEM((e4, D_EXPERT, D_MODEL), BF16),
        ],
    )
    return pl.pallas_call(
        functools.partial(_ffn_group_kernel, meta=meta),
        grid_spec=grid_spec,
        out_shape=jax.ShapeDtypeStruct((xs.shape[0], D_MODEL), F32),
        compiler_params=_cparams("arbitrary"),
        name="moe_ffn",
    )(tab, xs, w1, w3, w2)


def _combine_kernel(tab_ref, xp_ref, xs_ref, mod_ref, dloc_ref, fg_ref, ys_ref, op_ref, os_ref, runs_ref, sem_ref,
                    *, n_steps, n_p_steps, final_norm):
    i = pl.program_id(0)
    par = lax.rem(i, 2)
    tm = ROUTE_TM

    def copies(step, parity, wait):
        for sub in range(ROUTE_SUBS):
            _run_copies(tab_ref, tile=step * ROUTE_SUBS + sub, hbm_ref=ys_ref, vmem_ref=runs_ref.at[parity, sub],
                        sem=sem_ref.at[parity, sub], to_hbm=False, wait=wait)

    @pl.when(i == 0)
    def _():
        copies(0, 0, wait=False)

    @pl.when(i + 1 < n_steps)
    def _():
        copies(i + 1, 1 - par, wait=False)

    for sub in range(ROUTE_SUBS):
        _wait_rows(_tile_rows(tab_ref, i * ROUTE_SUBS + sub), ys_ref, runs_ref.at[par, sub], sem_ref.at[par, sub], False)
    row_f = lax.broadcasted_iota(jnp.int32, (tm, SORT_ROWS), 1).astype(F32)
    parts = []
    for sub in range(ROUTE_SUBS):
        tile = i * ROUTE_SUBS + sub
        covered = 0
        for g in range(N_GROUPS_MOE):
            covered = covered + tab_ref[tile * TAB_W + N_GROUPS_MOE + g]
        runs_ref[par, sub, pl.ds(pl.multiple_of(covered, ROUTE_PAD), LANES), :] = jnp.zeros((LANES, D_MODEL), F32)
        yb = runs_ref[par, sub, 0:SORT_ROWS, :].astype(BF16)
        pt = jnp.where(row_f == dloc_ref[sub * tm:(sub + 1) * tm, 0:1], 1.0, 0.0).astype(BF16)
        parts.append(_bdot(pt, yb))
    is_prompt = i < n_p_steps
    x = jnp.where(is_prompt, xp_ref[...], xs_ref[...])
    x2 = x + mod_ref[5:6, :] * jnp.concatenate(parts, axis=0)
    if final_norm:
        x2 = _rms(x2, fg_ref[...])

    @pl.when(is_prompt)
    def _():
        op_ref[...] = x2

    @pl.when(jnp.logical_not(is_prompt))
    def _():
        os_ref[...] = x2


def _combine(tab, xp, xs, mod, dloc, final_g, ys, layer, sample_seq, final_norm):
    n_p_steps = xp.shape[0] // ROUTE_STEP
    n_s_steps = xs.shape[0] // ROUTE_STEP
    n_steps = n_p_steps + n_s_steps
    step_of = lambda i, tab_ref: i
    p_spec, s_spec, mod_spec = _merged_specs(n_p_steps, n_s_steps, layer, sample_seq, step_of)
    grid_spec = pltpu.PrefetchScalarGridSpec(
        num_scalar_prefetch=1,
        grid=(n_steps,),
        in_specs=[
            p_spec, s_spec, mod_spec,
            pl.BlockSpec((ROUTE_STEP, LANES), lambda i, tab_ref: (i, 0)),
            pl.BlockSpec((1, D_MODEL), lambda i, tab_ref: (0, 0)),
            pl.BlockSpec(memory_space=pl.ANY),
        ],
        out_specs=[p_spec, s_spec],
        scratch_shapes=[
            pltpu.VMEM((2, ROUTE_SUBS, RUN_ROWS, D_MODEL), F32),
            pltpu.SemaphoreType.DMA((2, ROUTE_SUBS)),
        ],
    )
    return pl.pallas_call(
        functools.partial(_combine_kernel, n_steps=n_steps, n_p_steps=n_p_steps, final_norm=final_norm),
        grid_spec=grid_spec,
        out_shape=[jax.ShapeDtypeStruct(xp.shape, F32), jax.ShapeDtypeStruct(xs.shape, F32)],
        compiler_params=_cparams("arbitrary"),
        name="moe_combine",
    )(tab, xp, xs, mod, dloc, final_g, ys)


def _moe(xp, xs, mod, g2, w_r, b_r, w1, w3, w2, final_g, layer, sample_seq, final_norm):
    t = xp.shape[0] + xs.shape[0]
    n_tiles = t // ROUTE_TM
    max_rows = t + N_GROUPS_MOE * (ROUTE_PAD - 1) * n_tiles
    n_rows = (-(-max_rows // FFN_BM) + N_GROUPS_MOE) * FFN_BM
    dloc, tab, sorted_x = _route(xp, xs, mod, g2, w_r, b_r, layer, sample_seq, n_rows)
    ys = _ffn_group(tab, sorted_x, w1, w3, w2, layer, n_tiles * TAB_W)
    return _combine(tab, xp, xs, mod, dloc, final_g, ys, layer, sample_seq, final_norm)


def _rope_tables(seq):
    half = QK_ROPE // 2
    nf = half // 2
    inv = ROPE_BASE ** (-np.arange(nf, dtype=np.float64) / nf)
    pos = np.arange(seq)
    row = (pos // GRID_W).astype(np.float64)
    col = (pos % GRID_W).astype(np.float64)
    cos = np.ones((seq, HEAD_PAD), np.float64)
    sin_a = np.zeros((seq, HEAD_PAD), np.float64)
    sin_b = np.zeros((seq, HEAD_PAD), np.float64)
    for part, p in enumerate((row, col)):
        ang = p[:, None] * inv[None, :]
        base = ROPE_OFF + part * half
        cos[:, base:base + nf] = np.cos(ang)
        cos[:, base + nf:base + half] = np.cos(ang)
        sin_a[:, base:base + nf] = -np.sin(ang)
        sin_b[:, base + nf:base + half] = np.sin(ang)
    return tuple(jnp.asarray(a, F32) for a in (cos, sin_a, sin_b))


def _apply_rope(x, cos, sin_a, sin_b, reps):
    nf = QK_ROPE // 4
    width = x.shape[1]
    if reps > 1:
        cos, sin_a, sin_b = (jnp.concatenate([a] * reps, axis=1) for a in (cos, sin_a, sin_b))
    return x * cos + pltpu.roll(x, width - nf, axis=1) * sin_a + pltpu.roll(x, nf, axis=1) * sin_b


def _odd_in_kernel(*refs, rope, emit_cache):
    x_ref, mod_ref, g_ref, w_ref, qg_ref, wq_ref, kg_ref, wk_ref, wv_ref, cs_ref = refs[:10]
    refs = refs[10:]
    if rope:
        cos_ref, sa_ref, sb_ref = refs[:3]
        refs = refs[3:]
    y_ref, q_ref, k_ref, v_ref = refs[:4]
    refs = refs[4:]
    m = mod_ref[...]
    h = _rms(x_ref[...], g_ref[...]) * (1.0 + m[1:2]) + m[0:1]
    z = _bdot(h.astype(BF16), w_ref[...])
    zc = z[:, 0:D_C]
    qc = z[:, D_C:D_C + Q_LORA]
    kvc = z[:, D_C + Q_LORA:D_C + Q_LORA + KV_LORA]
    kpe = z[:, D_C + Q_LORA + KV_LORA:]
    q = _bdot(_rms(qc, qg_ref[...]).astype(BF16), wq_ref[...])
    kvn = _rms(kvc, kg_ref[...])
    if emit_cache:
        ckv_ref, kpe_ref = refs
        ckv_ref[...] = kvn
        kpe_ref[...] = kpe[:, ROPE_OFF:ROPE_OFF + QK_ROPE]
    if rope:
        tabs = (cos_ref[...], sa_ref[...], sb_ref[...])
        q = _apply_rope(q, *tabs, reps=N_HEADS)
        kpe = _apply_rope(kpe, *tabs, reps=1)
    kvb = kvn.astype(BF16)
    k = _bdot(kvb, wk_ref[...]) + jnp.concatenate([kpe] * N_HEADS, axis=1)
    scale = math.log2(math.e) / math.sqrt(QK_NOPE + QK_ROPE)
    q_ref[...] = (q * scale).astype(BF16)
    k_ref[...] = k.astype(BF16)
    v_ref[...] = _bdot(kvb, wv_ref[...]).astype(BF16)
    y = _bdot(zc.astype(BF16), cs_ref[...])
    y_ref[0, :, :] = y[:, 0:D_C].astype(BF16)
    y_ref[1, :, :] = y[:, D_C:2 * D_C].astype(BF16)


def _odd_in(x3, mod, g, w_in, q_g, w_q, kv_g, w_k, w_v, cs, stream, layer, rope_tabs, emit_cache):
    b, s, _ = x3.shape
    tm = min(s, 512)
    n_i = s // tm
    rope = rope_tabs is not None
    const = lambda a: pl.BlockSpec(a.shape, lambda bi, i: (0,) * a.ndim)
    in_specs = [
        pl.BlockSpec((None, tm, D_MODEL), lambda bi, i: (bi, i, 0)),
        _mod_spec(stream, layer),
        const(g), const(w_in), const(q_g), const(w_q), const(kv_g), const(w_k), const(w_v), const(cs),
    ]
    args = [x3, mod, g, w_in, q_g, w_q, kv_g, w_k, w_v, cs]
    if rope:
        in_specs += [pl.BlockSpec((tm, HEAD_PAD), lambda bi, i: (i, 0))] * 3
        args += list(rope_tabs)
    hq = N_HEADS * HEAD_PAD
    out_specs = [
        pl.BlockSpec((None, 2, tm, D_C), lambda bi, i: (bi, 0, i, 0)),
        pl.BlockSpec((None, tm, hq), lambda bi, i: (bi, i, 0)),
        pl.BlockSpec((None, tm, hq), lambda bi, i: (bi, i, 0)),
        pl.BlockSpec((None, tm, N_HEADS * V_DIM), lambda bi, i: (bi, i, 0)),
    ]
    out_shape = [
        jax.ShapeDtypeStruct((b, 2, s, D_C), BF16),
        jax.ShapeDtypeStruct((b, s, hq), BF16),
        jax.ShapeDtypeStruct((b, s, hq), BF16),
        jax.ShapeDtypeStruct((b, s, N_HEADS * V_DIM), BF16),
    ]
    if emit_cache:
        out_specs += [
            pl.BlockSpec((None, tm, KV_LORA), lambda bi, i: (bi, i, 0)),
            pl.BlockSpec((None, tm, QK_ROPE), lambda bi, i: (bi, i, 0)),
        ]
        out_shape += [
            jax.ShapeDtypeStruct((b, s, KV_LORA), F32),
            jax.ShapeDtypeStruct((b, s, QK_ROPE), F32),
        ]
    return pl.pallas_call(
        functools.partial(_odd_in_kernel, rope=rope, emit_cache=emit_cache),
        grid=(b, n_i),
        in_specs=in_specs,
        out_specs=out_specs,
        out_shape=out_shape,
        compiler_params=_cparams("parallel", "parallel"),
        name="odd_in",
    )(*args)


def _cache_kv_kernel(c_ref, p_ref, wk_ref, wv_ref, k_ref, v_ref):
    cb = c_ref[...].astype(BF16)
    k = _bdot(cb, wk_ref[...]) + jnp.concatenate([p_ref[...]] * N_HEADS, axis=1)
    k_ref[...] = k.astype(BF16)
    v_ref[...] = _bdot(cb, wv_ref[...]).astype(BF16)


def _cache_kv(ckv, kpe_blk, w_k, w_v):
    b, p, _ = ckv.shape
    hq = N_HEADS * HEAD_PAD
    return pl.pallas_call(
        _cache_kv_kernel,
        grid=(b,),
        in_specs=[
            pl.BlockSpec((None, p, KV_LORA), lambda bi: (bi, 0, 0)),
            pl.BlockSpec((None, p, HEAD_PAD), lambda bi: (bi, 0, 0)),
            pl.BlockSpec(w_k.shape, lambda bi: (0, 0)),
            pl.BlockSpec(w_v.shape, lambda bi: (0, 0)),
        ],
        out_specs=[
            pl.BlockSpec((None, p, hq), lambda bi: (bi, 0, 0)),
            pl.BlockSpec((None, p, N_HEADS * V_DIM), lambda bi: (bi, 0, 0)),
        ],
        out_shape=[
            jax.ShapeDtypeStruct((b, p, hq), BF16),
            jax.ShapeDtypeStruct((b, p, N_HEADS * V_DIM), BF16),
        ],
        compiler_params=_cparams("parallel"),
        name="cache_kv",
    )(ckv, kpe_blk, w_k, w_v)


def _odd_mix_kernel(*refs, with_cache):
    q_ref, k_ref, v_ref = refs[:3]
    refs = refs[3:]
    if with_cache:
        kc_ref, vc_ref = refs[:2]
        refs = refs[2:]
    y_ref, f_ref, x_ref, mod_ref, wo_ref, o_ref, a_ref = refs
    tq = q_ref.shape[0]
    lane = lax.broadcasted_iota(jnp.int32, (tq, 2 * V_DIM), 1)
    for pair in range(N_HEADS // 2):
        vcols = slice(pair * 2 * V_DIM, (pair + 1) * 2 * V_DIM)
        outs = []
        for h in (2 * pair, 2 * pair + 1):
            hcols = slice(h * HEAD_PAD, (h + 1) * HEAD_PAD)
            qh = q_ref[:, hcols]
            s = lax.dot_general(qh, k_ref[:, hcols], NT_DIMS, preferred_element_type=F32)
            top = jnp.max(s, axis=-1, keepdims=True)
            if with_cache:
                sc = lax.dot_general(qh, kc_ref[:, hcols], NT_DIMS, preferred_element_type=F32)
                top = jnp.maximum(top, jnp.max(sc, axis=-1, keepdims=True))
            p = jnp.exp2(s - top)
            den = jnp.sum(p, axis=-1, keepdims=True)
            o = _bdot(p.astype(BF16), v_ref[:, vcols])
            if with_cache:
                pc = jnp.exp2(sc - top)
                den = den + jnp.sum(pc, axis=-1, keepdims=True)
                o = o + _bdot(pc.astype(BF16), vc_ref[:, vcols])
            outs.append(o / den)
        a_ref[:, vcols] = jnp.where(lane < V_DIM, outs[0], outs[1]).astype(BF16)
    f = _bdot(f_ref[...], y_ref[...].reshape(-1, D_C))
    o = _bdot(f.astype(BF16), wo_ref[0:D_C, :]) + _bdot(a_ref[...], wo_ref[D_C:, :])
    o_ref[...] = x_ref[...] + mod_ref[2:3, :] * o


def _odd_whole_kernel(x_ref, mod_ref, g_ref, w_ref, qg_ref, wq_ref, kg_ref, wk_ref, wv_ref, cs_ref, f_ref, wo_ref,
                      o_ref, ckv_ref, kpe_ref, y_ref, q_ref, k_ref, v_ref, a_ref):
    _odd_in_kernel(x_ref, mod_ref, g_ref, w_ref, qg_ref, wq_ref, kg_ref, wk_ref, wv_ref, cs_ref,
                   y_ref, q_ref, k_ref, v_ref, ckv_ref, kpe_ref, rope=False, emit_cache=True)
    _odd_mix_kernel(q_ref, k_ref, v_ref, y_ref, f_ref, x_ref, mod_ref, wo_ref, o_ref, a_ref, with_cache=False)


def _odd_whole(x3, mod, g, w_in, q_g, w_q, kv_g, w_k, w_v, cs, fmat, w_out, stream, layer):
    b, s, _ = x3.shape
    hq = N_HEADS * HEAD_PAD
    hv = N_HEADS * V_DIM
    const = lambda a: pl.BlockSpec(a.shape, lambda bi: (0,) * a.ndim)
    row_block = lambda w: pl.BlockSpec((None, s, w), lambda bi: (bi, 0, 0))
    return pl.pallas_call(
        _odd_whole_kernel,
        grid=(b,),
        in_specs=[
            row_block(D_MODEL),
            pl.BlockSpec((None, None, 6, D_MODEL), lambda bi: (layer, stream.row_of_batch(bi), 0, 0)),
            const(g), const(w_in), const(q_g), const(w_q), const(kv_g), const(w_k), const(w_v), const(cs),
            const(fmat), const(w_out),
        ],
        out_specs=[row_block(D_MODEL), row_block(KV_LORA), row_block(QK_ROPE)],
        out_shape=[
            jax.ShapeDtypeStruct((b, s, D_MODEL), F32),
            jax.ShapeDtypeStruct((b, s, KV_LORA), F32),
            jax.ShapeDtypeStruct((b, s, QK_ROPE), F32),
        ],
        scratch_shapes=[
            pltpu.VMEM((2, s, D_C), BF16),
            pltpu.VMEM((s, hq), BF16),
            pltpu.VMEM((s, hq), BF16),
            pltpu.VMEM((s, hv), BF16),
            pltpu.VMEM((s, hv), BF16),
        ],
        compiler_params=_cparams("parallel"),
        name="odd_whole",
    )(x3, mod, g, w_in, q_g, w_q, kv_g, w_k, w_v, cs, fmat, w_out)


def _odd_mix(q, k, v, kc, vc, y, fmat, x3, mod, w_out, stream, layer):
    b, s, hq = q.shape
    tq = min(s, 256)
    n_i = s // tq
    with_cache = kc is not None
    hv = N_HEADS * V_DIM
    mode = dict(pipeline_mode=pl.Buffered(1)) if with_cache else {}

    def per_batch(rows, cols):
        return pl.BlockSpec((None, rows, cols), lambda bi, i: (bi, 0, 0), **mode)

    in_specs = [pl.BlockSpec((None, tq, hq), lambda bi, i: (bi, i, 0)), per_batch(s, hq), per_batch(s, hv)]
    args = [q, k, v]
    if with_cache:
        p = kc.shape[1]
        in_specs += [per_batch(p, hq), per_batch(p, hv)]
        args += [kc, vc]
    in_specs += [
        per_batch(2 * s, D_C),
        pl.BlockSpec((tq, 2 * s), lambda bi, i: (i, 0)),
        pl.BlockSpec((None, tq, D_MODEL), lambda bi, i: (bi, i, 0)),
        _mod_spec(stream, layer),
        pl.BlockSpec(w_out.shape, lambda bi, i: (0, 0), **mode),
    ]
    args += [y, fmat, x3, mod, w_out]
    return pl.pallas_call(
        functools.partial(_odd_mix_kernel, with_cache=with_cache),
        grid=(b, n_i),
        in_specs=in_specs,
        out_specs=pl.BlockSpec((None, tq, D_MODEL), lambda bi, i: (bi, i, 0)),
        out_shape=jax.ShapeDtypeStruct((b, s, D_MODEL), F32),
        scratch_shapes=[pltpu.VMEM((tq, hv), BF16)],
        compiler_params=_cparams("parallel", "arbitrary"),
        name="odd_mix",
    )(*args)


def _dft_tables(seq):
    jc = np.arange(C_GW)
    ang_c = 2.0 * np.pi * np.outer(jc, jc) / C_GW
    eye = np.eye(C_GROUPS)
    cs = np.concatenate([np.kron(eye, np.cos(ang_c)), np.kron(eye, np.sin(ang_c))], axis=1)
    jn = np.arange(seq)
    ang_n = 2.0 * np.pi * (np.outer(jn, jn) % seq) / seq
    scale = 1.0 / math.sqrt(seq * C_GW)
    fmat = np.concatenate([np.cos(ang_n), -np.sin(ang_n)], axis=1) * scale
    return jnp.asarray(cs, F32).astype(BF16), jnp.asarray(fmat, F32).astype(BF16)


def _odd_weights(w_in, w_uq, w_ukv):
    d = w_in.shape[0]
    base = D_C + Q_LORA + KV_LORA
    kpe_blk = jnp.zeros((d, HEAD_PAD), w_in.dtype).at[:, ROPE_OFF:ROPE_OFF + QK_ROPE].set(w_in[:, base:])
    w_in_p = jnp.concatenate([w_in[:, :base], kpe_blk], axis=1).astype(BF16)
    qh = w_uq.reshape(Q_LORA, N_HEADS, QK_NOPE + QK_ROPE)
    w_q = jnp.pad(qh, ((0, 0), (0, 0), (0, HEAD_PAD - QK_NOPE - QK_ROPE))).reshape(Q_LORA, -1).astype(BF16)
    kvh = w_ukv.reshape(KV_LORA, N_HEADS, QK_NOPE + V_DIM)
    w_k = jnp.pad(kvh[:, :, :QK_NOPE], ((0, 0), (0, 0), (0, HEAD_PAD - QK_NOPE))).reshape(KV_LORA, -1)
    w_v = kvh[:, :, QK_NOPE:].reshape(KV_LORA, -1)
    return w_in_p, w_q, w_k.astype(BF16), w_v.astype(BF16)


ROUTER_ROWS = 32


def _router_weights(wg, bg, we, be):
    d = wg.shape[0]
    w = jnp.concatenate([wg, we.reshape(d, N_EXPERTS)], axis=1).T
    w = jnp.pad(w, ((0, ROUTER_ROWS - w.shape[0]), (0, 0))).astype(BF16)
    b = jnp.concatenate([bg, be.reshape(N_EXPERTS)])
    b = jnp.pad(b, (0, ROUTER_ROWS - b.shape[0])).reshape(ROUTER_ROWS, 1).astype(F32)
    return w, b


def kernel(x_prompt, x_sample, cache_ckv, cache_kpe, c, c_ctx, mod_w, mod_b, norm1_g, norm2_g,
           ev_w_in, ev_conv_w, ev_sgu_norm_g, ev_sgu_w, ev_sgu_b, ev_w_out,
           od_w_in, od_q_norm_g, od_w_uq, od_kv_norm_g, od_w_ukv, od_w_out,
           moe_wg, moe_bg, moe_we, moe_be, moe_w1, moe_w3, moe_w2, final_norm_g):
    bp, n_p, d = x_prompt.shape
    bs, n_s, _ = x_sample.shape
    streams = [(_Stream(bp, n_p, True), x_prompt), (_Stream(bs, n_s, False), x_sample)]

    n_rows = 1 + bs
    cond_t = jnp.concatenate([c_ctx[None, :], c], axis=0).T
    mod = _adaln(cond_t, mod_w, mod_b, n_rows)

    final_g = final_norm_g.reshape(1, d)
    xs = [x for _, x in streams]
    new_ckv, new_kpe = [], []
    for l in range(DEPTH):
        j = l // 2
        g1 = norm1_g[l].reshape(1, d)
        g2 = norm2_g[l].reshape(1, d)
        w_r, b_r = _router_weights(moe_wg[l], moe_bg[l], moe_we[l], moe_be[l])
        last = l == DEPTH - 1
        if l % 2 == 0:
            w_in = ev_w_in[j].astype(BF16)
            w_out = ev_w_out[j].astype(BF16)
            sgu_w = ev_sgu_w[j].astype(BF16)
            sgu_g = ev_sgu_norm_g[j].reshape(1, D_B)
            sgu_bias = jnp.repeat(ev_sgu_b[j].T, D_B // B_GROUPS, axis=1)
            for si, (st, _) in enumerate(streams):
                xs[si] = _even_layer(xs[si], mod, g1, w_in, ev_conv_w[j], sgu_g, sgu_w, sgu_bias, w_out, st, l)
        else:
            w_in, w_q, w_k, w_v = _odd_weights(od_w_in[j], od_w_uq[j], od_w_ukv[j])
            w_out = od_w_out[j].astype(BF16)
            q_g = od_q_norm_g[j].reshape(1, Q_LORA)
            kv_g = od_kv_norm_g[j].reshape(1, KV_LORA)
            for si, (st, _) in enumerate(streams):
                x3 = xs[si]
                cs, fmat = _dft_tables(st.seq)
                if st.shared_cond:
                    xs[si], ckv, kpe = _odd_whole(x3, mod, g1, w_in, q_g, w_q, kv_g, w_k, w_v, cs, fmat, w_out, st, l)
                    new_ckv.append(ckv)
                    new_kpe.append(kpe)
                    continue
                y, q, k, v = _odd_in(x3, mod, g1, w_in, q_g, w_q, kv_g, w_k, w_v, cs, st, l,
                                     _rope_tables(st.seq), False)
                kpe_blk = jnp.pad(cache_kpe[:, j], ((0, 0), (0, 0), (ROPE_OFF, HEAD_PAD - ROPE_OFF - QK_ROPE)))
                kc, vc = _cache_kv(cache_ckv[:, j], kpe_blk, w_k, w_v)
                xs[si] = _odd_mix(q, k, v, kc, vc, y.reshape(st.batch, 2 * st.seq, D_C), fmat, x3, mod,
                                  w_out, st, l)
        x2p, x2s = _moe(xs[0].reshape(bp * n_p, d), xs[1].reshape(bs * n_s, d), mod, g2, w_r, b_r,
                        moe_w1, moe_w3, moe_w2, final_g, l, n_s, last)
        xs = [x2p.reshape(bp, n_p, d), x2s.reshape(bs, n_s, d)]
    return (xs[0], xs[1], jnp.stack(new_ckv, axis=1), jnp.stack(new_kpe, axis=1))
```

```python
import functools
import math

import numpy as np
import jax
import jax.numpy as jnp
from jax import lax
from jax.experimental import pallas as pl
from jax.experimental.pallas import tpu as pltpu

D_MODEL = 1024
DEPTH = 2
GRID_W = 64
D_A = D_MODEL // 2
D_B = D_MODEL // 2
B_GROUPS = 4
CHUNK = 128
D_EVEN_IN = 3 * D_A + 2 * D_B
D_C = D_MODEL // 4
C_GROUPS = 4
C_GW = D_C // C_GROUPS
N_HEADS = 12
QK_NOPE = 64
QK_ROPE = 32
V_DIM = 64
Q_LORA = 384
KV_LORA = 256
ROPE_BASE = 10000.0
N_GROUPS_MOE = 4
EXPERTS_PER_GROUP = 4
N_EXPERTS = N_GROUPS_MOE * EXPERTS_PER_GROUP
D_EXPERT = 256
EPS = 1e-6

LANES = 128
HEAD_PAD = 128
ROPE_OFF = QK_NOPE
GATE_OFF = N_GROUPS_MOE
NEG_BIG = -1e30
F32 = jnp.float32
BF16 = jnp.bfloat16
VMEM_LIMIT = 56 * 1024 * 1024


def _cparams(*sem):
    return pltpu.CompilerParams(dimension_semantics=sem, vmem_limit_bytes=VMEM_LIMIT)


def _rms(x, g):
    return x * lax.rsqrt(jnp.mean(x * x, axis=-1, keepdims=True) + EPS) * g


def _bdot(a, b):
    return jnp.dot(a, b, preferred_element_type=F32)


NT_DIMS = (((1,), (1,)), ((), ()))
TN_DIMS = (((0,), (0,)), ((), ()))


def _mod_kernel(ct_ref, w_ref, b_ref, o_ref):
    c = ct_ref[...]
    s = c * jax.nn.sigmoid(c)
    w = w_ref[...]
    b = b_ref[...]
    for r in range(o_ref.shape[0]):
        o_ref[r:r + 1, :] = jnp.sum(s[:, r:r + 1] * w, axis=0, keepdims=True) + b


def _adaln(cond_t, mod_w, mod_b, n_rows):
    nt = 1024
    d6 = mod_w.shape[-1]
    out = pl.pallas_call(
        _mod_kernel,
        grid=(DEPTH, d6 // nt),
        in_specs=[
            pl.BlockSpec(cond_t.shape, lambda l, n: (0, 0)),
            pl.BlockSpec((None, D_MODEL, nt), lambda l, n: (l, 0, n)),
            pl.BlockSpec((None, 1, nt), lambda l, n: (l, 0, n)),
        ],
        out_specs=pl.BlockSpec((None, n_rows, nt), lambda l, n: (l, 0, n)),
        out_shape=jax.ShapeDtypeStruct((DEPTH, n_rows, d6), F32),
        compiler_params=_cparams("parallel", "parallel"),
        name="adaln",
    )(cond_t, mod_w, mod_b.reshape(DEPTH, 1, d6))
    return out.reshape(DEPTH, n_rows, 6, D_MODEL)


class _Stream:
    def __init__(self, batch, seq, shared_cond):
        self.batch = batch
        self.seq = seq
        self.tokens = batch * seq
        self.shared_cond = shared_cond

    def row_of_batch(self, b):
        return 0 if self.shared_cond else b + 1

    def row_of_tile(self, i, tm):
        return 0 if self.shared_cond else (i * tm) // self.seq + 1


def _mod_spec(stream, layer, tm=None):
    if tm is None:
        return pl.BlockSpec((None, None, 6, D_MODEL), lambda b, i: (layer, stream.row_of_batch(b), 0, 0))
    return pl.BlockSpec((None, None, 6, D_MODEL), lambda i, *_: (layer, stream.row_of_tile(i, tm), 0, 0))


HALO = 8


def _even_kernel(*refs, has_halo, nc):
    if has_halo:
        x_ref, xp_ref, xn_ref = refs[:3]
        refs = refs[3:]
    else:
        x_ref = refs[0]
        refs = refs[1:]
    mod_ref, g_ref, wi_ref, cw_ref, sg_ref, sw_ref, sb_ref, wo_ref, o_ref, z_ref, y_ref = refs
    i = pl.program_id(1)
    n_i = pl.num_programs(1)
    ts = x_ref.shape[0]
    m = mod_ref[...]
    g = g_ref[...]

    def modulate(x):
        return (_rms(x, g) * (1.0 + m[1:2]) + m[0:1]).astype(BF16)

    x = x_ref[...]
    hb = modulate(x)
    for n in range(D_EVEN_IN // nc):
        z_ref[:, n * nc:(n + 1) * nc] = _bdot(hb, wi_ref[:, n * nc:(n + 1) * nc]).astype(BF16)

    gate_b = z_ref[:, 0:D_A].astype(F32)
    gate_c = z_ref[:, D_A:2 * D_A].astype(F32)
    xa = z_ref[:, 2 * D_A:3 * D_A].astype(F32)
    t = gate_c * xa
    t_prev = pltpu.roll(t, 1, axis=0)
    t_next = pltpu.roll(t, ts - 1, axis=0)
    row = lax.broadcasted_iota(jnp.int32, (ts, 1), 0)
    if has_halo:
        hh = modulate(jnp.concatenate([xp_ref[...], xn_ref[...]], axis=0))
        zh = _bdot(hh, wi_ref[:, D_A:3 * D_A]).astype(BF16).astype(F32)
        th = zh[:, 0:D_A] * zh[:, D_A:2 * D_A]
        tp = th[HALO - 1:HALO] * (i > 0).astype(F32)
        tn = th[HALO:HALO + 1] * (i < n_i - 1).astype(F32)
    else:
        tp = tn = 0.0
    t_prev = jnp.where(row == 0, tp, t_prev)
    t_next = jnp.where(row == ts - 1, tn, t_next)
    cw = cw_ref[...]
    y_a = gate_b * (t_prev * cw[0:1] + t * cw[1:2] + t_next * cw[2:3])
    y_ref[:, 0:D_A] = y_a.astype(BF16)

    u = z_ref[:, 3 * D_A:3 * D_A + D_B].astype(F32)
    v = z_ref[:, 3 * D_A + D_B:3 * D_A + 2 * D_B].astype(F32)
    vb = _rms(v, sg_ref[...]).astype(BF16)
    gw = D_B // B_GROUPS
    for c in range(ts // CHUNK):
        rows = slice(c * CHUNK, (c + 1) * CHUNK)
        for gi in range(B_GROUPS):
            cols = slice(gi * gw, (gi + 1) * gw)
            sv = _bdot(sw_ref[gi], vb[rows, cols]) + sb_ref[:, cols]
            y_ref[rows, D_A + gi * gw:D_A + (gi + 1) * gw] = (u[rows, cols] * sv).astype(BF16)

    o_ref[...] = x + m[2:3] * _bdot(y_ref[...], wo_ref[...])


def _even_layer(x3, mod, g, w_in, conv_w, sgu_g, sgu_w, sgu_bias, w_out, stream, layer):
    b, s, _ = x3.shape
    ts = min(s, 256)
    n_i = s // ts
    has_halo = n_i > 1
    hb = ts // HALO
    last_h = s // HALO - 1
    const = lambda a: pl.BlockSpec(a.shape, lambda bi, i: (0,) * a.ndim)
    in_specs = [pl.BlockSpec((None, ts, D_MODEL), lambda bi, i: (bi, i, 0))]
    args = [x3]
    if has_halo:
        in_specs += [
            pl.BlockSpec((None, HALO, D_MODEL), lambda bi, i: (bi, jnp.maximum(i * hb - 1, 0), 0)),
            pl.BlockSpec((None, HALO, D_MODEL), lambda bi, i: (bi, jnp.minimum((i + 1) * hb, last_h), 0)),
        ]
        args += [x3, x3]
    in_specs += [_mod_spec(stream, layer), const(g), const(w_in), const(conv_w), const(sgu_g), const(sgu_w),
                 const(sgu_bias), const(w_out)]
    args += [mod, g, w_in, conv_w, sgu_g, sgu_w, sgu_bias, w_out]
    return pl.pallas_call(
        functools.partial(_even_kernel, has_halo=has_halo, nc=512),
        grid=(b, n_i),
        in_specs=in_specs,
        out_specs=pl.BlockSpec((None, ts, D_MODEL), lambda bi, i: (bi, i, 0)),
        out_shape=jax.ShapeDtypeStruct((b, s, D_MODEL), F32),
        scratch_shapes=[pltpu.VMEM((ts, D_EVEN_IN), BF16), pltpu.VMEM((ts, D_A + D_B), BF16)],
        compiler_params=_cparams("parallel", "parallel"),
        name="even_layer",
    )(*args)


ROUTE_TM = 256
ROUTE_SUBS = 2
ROUTE_STEP = ROUTE_TM * ROUTE_SUBS
ROUTE_PAD = 8
SORT_ROWS = ROUTE_TM + LANES
RUN_ROWS = SORT_ROWS + 32
XS_W = D_MODEL + LANES
GATE_LO = EXPERTS_PER_GROUP
DLOC_HI = 2 * EXPERTS_PER_GROUP
DLOC_RADIX = 16.0
FFN_BM = 512
FFN_HALF = FFN_BM // 2
RUN_SIZES = (256, 128, 64, 32, 16, 8)
TAB_W = 2 * N_GROUPS_MOE


def _round_up(x, m):
    return lax.div(x + (m - 1), m) * m


def _run_copies(tab_ref, tile, hbm_ref, vmem_ref, sem, to_hbm, wait):
    off = 0
    for g in range(N_GROUPS_MOE):
        start = tab_ref[tile * TAB_W + g]
        n = tab_ref[tile * TAB_W + N_GROUPS_MOE + g]
        for p in RUN_SIZES:
            done = n & (-2 * p)

            @pl.when((n & p) != 0)
            def _():
                v = vmem_ref.at[pl.ds(pl.multiple_of(off + done, ROUTE_PAD), p)]
                h = hbm_ref.at[pl.ds(pl.multiple_of(start + done, ROUTE_PAD), p)]
                cp = pltpu.make_async_copy(v, h, sem) if to_hbm else pltpu.make_async_copy(h, v, sem)
                if wait:
                    cp.wait()
                else:
                    cp.start()
        off = off + n


def _zero_fill(tab_ref, meta, zeros_ref, hbm_ref, sem, n_rows, wait):
    def copy(rows, dst_row):
        cp = pltpu.make_async_copy(zeros_ref.at[pl.ds(0, rows)],
                                   hbm_ref.at[pl.ds(pl.multiple_of(dst_row, ROUTE_PAD), rows)], sem)
        if wait:
            cp.wait()
        else:
            cp.start()

    end = 0
    for g in range(N_GROUPS_MOE):
        fill = tab_ref[meta + g]
        start = tab_ref[meta + N_GROUPS_MOE + g]
        end = start + _round_up(fill, FFN_BM)
        tail = end - start - fill
        for p in RUN_SIZES:
            pl.when((tail & p) != 0)(functools.partial(copy, p, start + fill + (tail & (-2 * p))))
    for k in range(n_rows // FFN_BM):
        pl.when(end + k * FFN_BM < n_rows)(functools.partial(copy, FFN_BM, end + k * FFN_BM))


def _wait_rows(n, hbm_ref, vmem_ref, sem, to_hbm):
    for p in RUN_SIZES:
        @pl.when((n & p) != 0)
        def _():
            v = vmem_ref.at[pl.ds(0, p)]
            h = hbm_ref.at[pl.ds(0, p)]
            (pltpu.make_async_copy(v, h, sem) if to_hbm else pltpu.make_async_copy(h, v, sem)).wait()


def _tile_rows(tab_ref, tile):
    n = 0
    for g in range(N_GROUPS_MOE):
        n = n + tab_ref[tile * TAB_W + N_GROUPS_MOE + g]
    return n


def _max4(v):
    return jnp.maximum(jnp.maximum(v[0], v[1]), jnp.maximum(v[2], v[3]))


def _first_of4(v, top):
    return jnp.where(v[0] == top, 0.0, jnp.where(v[1] == top, 1.0, jnp.where(v[2] == top, 2.0, 3.0)))


def _route_kernel(xp_ref, xs_ref, mod_ref, g_ref, wrt_ref, brt_ref, upper_ref,
                  dloc_ref, tab_ref, sorted_hbm, hbuf_ref, drow_ref, sorted_ref, zeros_ref, cnt_ref, fill_ref,
                  sem_ref, zsem_ref, *, n_steps, n_p_steps):
    sweep = pl.program_id(0)
    i = pl.program_id(1)
    tm = ROUTE_TM
    meta = n_steps * ROUTE_SUBS * TAB_W
    n_rows = sorted_hbm.shape[0]
    ng = N_GROUPS_MOE

    @pl.when(sweep == 0)
    def _():
        @pl.when(i == 0)
        def _():
            for g in range(ng):
                fill_ref[g] = 0

        m = mod_ref[...]
        x = jnp.where(i < n_p_steps, xp_ref[...], xs_ref[...])
        h = _rms(x, g_ref[...]) * (1.0 + m[4:5]) + m[3:4]
        hb = h.astype(BF16)
        lt = lax.dot_general(wrt_ref[...], hb, (((1,), (1,)), ((), ())), preferred_element_type=F32) + brt_ref[...]
        gl = [lt[r:r + 1, :] for r in range(ng)]
        g_top = _max4(gl)
        g_idx = _first_of4(gl, g_top)
        g_w = 1.0 / (jnp.exp(gl[0] - g_top) + jnp.exp(gl[1] - g_top) + jnp.exp(gl[2] - g_top) + jnp.exp(gl[3] - g_top))
        ev = []
        for k in range(EXPERTS_PER_GROUP):
            cand = [lt[GATE_OFF + EXPERTS_PER_GROUP * r + k:GATE_OFF + EXPERTS_PER_GROUP * r + k + 1, :]
                    for r in range(ng)]
            ev.append(jnp.where(g_idx == 0.0, cand[0], jnp.where(g_idx == 1.0, cand[1],
                                jnp.where(g_idx == 2.0, cand[2], cand[3]))))
        v1 = _max4(ev)
        i1 = _first_of4(ev, v1)
        rest = [jnp.where(i1 == float(k), NEG_BIG, ev[k]) for k in range(EXPERTS_PER_GROUP)]
        v2 = _max4(rest)
        i2 = _first_of4(rest, v2)
        e2 = jnp.exp(v2 - v1)
        w1 = 1.0 / (1.0 + e2)
        w2 = e2 * w1
        gates = [g_w * (jnp.where(i1 == float(k), w1, 0.0) + jnp.where(i2 == float(k), w2, 0.0))
                 for k in range(EXPERTS_PER_GROUP)]

        sub8 = lax.broadcasted_iota(jnp.int32, (8, tm), 0).astype(F32)
        dlocs = []
        for sub in range(ROUTE_SUBS):
            gi = g_idx[:, sub * tm:(sub + 1) * tm]
            hot = jnp.where(sub8 == gi, 1.0, 0.0)
            before = _bdot(hot.astype(BF16), upper_ref[...])
            dl = jnp.sum(before * hot, axis=0, keepdims=True)
            off = 0
            for g in range(ng):
                n_g = _round_up(jnp.sum(hot[g:g + 1, :]).astype(jnp.int32), ROUTE_PAD)
                cnt_ref[(i * ROUTE_SUBS + sub) * ng + g] = n_g
                fill_ref[g] = fill_ref[g] + n_g
                dl = dl + jnp.where(gi == float(g), off.astype(F32) if g else 0.0, 0.0)
                off = off + n_g
            drow_ref[i * ROUTE_SUBS + sub] = jnp.broadcast_to(dl, (8, tm))
            dlocs.append(dl)
        dloc = jnp.concatenate(dlocs, axis=1)
        d_hi = jnp.floor(dloc * (1.0 / DLOC_RADIX))
        g_hi = [gt.astype(BF16).astype(F32) for gt in gates]
        ex_rows = g_hi + [gt - gh for gt, gh in zip(gates, g_hi)] + [d_hi, dloc - DLOC_RADIX * d_hi]
        sub16 = lax.broadcasted_iota(jnp.int32, (16, ROUTE_STEP), 0)
        ex_t = jnp.zeros((16, ROUTE_STEP), F32)
        for r, row in enumerate(ex_rows):
            ex_t = jnp.where(sub16 == r, row, ex_t)
        ex_t = jnp.concatenate([ex_t, jnp.zeros((LANES - 16, ROUTE_STEP), F32)], axis=0)
        extras = ex_t.T
        dloc_ref[...] = jnp.broadcast_to(
            DLOC_RADIX * extras[:, DLOC_HI:DLOC_HI + 1] + extras[:, DLOC_HI + 1:DLOC_HI + 2], (ROUTE_STEP, LANES))
        hbuf_ref[pl.ds(pl.multiple_of(i * ROUTE_STEP, ROUTE_STEP), ROUTE_STEP), :] = (
            jnp.concatenate([hb, extras.astype(BF16)], axis=1))

    @pl.when(sweep == 1)
    def _():
        @pl.when(i == 0)
        def _():
            start = 0
            for g in range(ng):
                tab_ref[meta + g] = fill_ref[g]
                tab_ref[meta + ng + g] = start
                start = start + _round_up(fill_ref[g], FFN_BM)
                fill_ref[g] = 0
            zeros_ref[...] = jnp.zeros_like(zeros_ref)

        row_f = lax.broadcasted_iota(jnp.int32, (SORT_ROWS, tm), 0).astype(F32)
        for sub in range(ROUTE_SUBS):
            tile = i * ROUTE_SUBS + sub

            @pl.when(i >= 1)
            def _():
                _wait_rows(_tile_rows(tab_ref, tile - ROUTE_SUBS), sorted_hbm, sorted_ref.at[sub], sem_ref.at[sub], True)

            rows = hbuf_ref[pl.ds(pl.multiple_of(i * ROUTE_STEP + sub * tm, tm), tm), :]
            onehot = jnp.where(row_f == drow_ref[tile][0:1, :], 1.0, 0.0).astype(BF16)
            sorted_ref[sub] = _bdot(onehot, rows)
            for g in range(ng):
                n_g = cnt_ref[tile * ng + g]
                tab_ref[tile * TAB_W + g] = tab_ref[meta + ng + g] + fill_ref[g]
                tab_ref[tile * TAB_W + ng + g] = n_g
                fill_ref[g] = fill_ref[g] + n_g
            _run_copies(tab_ref, tile, sorted_hbm, sorted_ref.at[sub], sem_ref.at[sub], to_hbm=True, wait=False)

            @pl.when(i == n_steps - 1)
            def _():
                _wait_rows(_tile_rows(tab_ref, tile), sorted_hbm, sorted_ref.at[sub], sem_ref.at[sub], True)

        @pl.when(i == n_steps - 1)
        def _():
            _zero_fill(tab_ref, meta, zeros_ref, sorted_hbm, zsem_ref, n_rows, wait=False)
            _zero_fill(tab_ref, meta, zeros_ref, sorted_hbm, zsem_ref, n_rows, wait=True)


def _merged_specs(n_p_steps, n_s_steps, layer, sample_seq, step_of):
    def p_map(*idx):
        return (jnp.minimum(step_of(*idx), n_p_steps - 1), 0)

    def s_map(*idx):
        return (jnp.clip(step_of(*idx) - n_p_steps, 0, n_s_steps - 1), 0)

    def mod_map(*idx):
        j = step_of(*idx)
        row = jnp.where(j < n_p_steps, 0, 1 + lax.div(jnp.maximum(j - n_p_steps, 0) * ROUTE_STEP, sample_seq))
        return (layer, row, 0, 0)

    return (pl.BlockSpec((ROUTE_STEP, D_MODEL), p_map), pl.BlockSpec((ROUTE_STEP, D_MODEL), s_map),
            pl.BlockSpec((None, None, 6, D_MODEL), mod_map))


def _route(xp, xs, mod, g2, w_rt, b_rt, layer, sample_seq, n_rows):
    n_p_steps = xp.shape[0] // ROUTE_STEP
    n_s_steps = xs.shape[0] // ROUTE_STEP
    n_steps = n_p_steps + n_s_steps
    t = n_steps * ROUTE_STEP
    tm = ROUTE_TM
    upper = jnp.asarray(np.triu(np.ones((tm, tm), np.float32), 1), BF16)
    step_of = lambda s, i: jnp.where(s == 0, i, n_steps - 1)
    p_spec, s_spec, mod_spec = _merged_specs(n_p_steps, n_s_steps, layer, sample_seq, step_of)
    const = lambda a: pl.BlockSpec(a.shape, lambda s, i: (0,) * a.ndim)
    n_tiles = n_steps * ROUTE_SUBS
    return pl.pallas_call(
        functools.partial(_route_kernel, n_steps=n_steps, n_p_steps=n_p_steps),
        grid=(2, n_steps),
        in_specs=[p_spec, s_spec, mod_spec, const(g2), const(w_rt), const(b_rt), const(upper)],
        out_specs=[
            pl.BlockSpec((ROUTE_STEP, LANES), lambda s, i: (step_of(s, i), 0)),
            pl.BlockSpec(memory_space=pltpu.SMEM),
            pl.BlockSpec(memory_space=pl.ANY),
        ],
        out_shape=[
            jax.ShapeDtypeStruct((t, LANES), F32),
            jax.ShapeDtypeStruct(((n_tiles + 1) * TAB_W,), jnp.int32),
            jax.ShapeDtypeStruct((n_rows, XS_W), F32),
        ],
        scratch_shapes=[
            pltpu.VMEM((t, XS_W), BF16),
            pltpu.VMEM((n_tiles, 8, tm), F32),
            pltpu.VMEM((ROUTE_SUBS, SORT_ROWS, XS_W), F32),
            pltpu.VMEM((FFN_BM, XS_W), F32),
            pltpu.SMEM((n_tiles * N_GROUPS_MOE,), jnp.int32),
            pltpu.SMEM((N_GROUPS_MOE,), jnp.int32),
            pltpu.SemaphoreType.DMA((ROUTE_SUBS,)),
            pltpu.SemaphoreType.DMA(()),
        ],
        compiler_params=_cparams("arbitrary", "arbitrary"),
        name="moe_route",
    )(xp, xs, mod, g2, w_rt, b_rt, upper)


def _ffn_lookup(i, tab_ref, meta):
    fills = [tab_ref[meta + g] for g in range(N_GROUPS_MOE)]
    edges = []
    acc = 0
    for f in fills:
        acc = acc + lax.div(f + (FFN_BM - 1), FFN_BM)
        edges.append(acc)
    total = edges[-1]
    ii = jnp.minimum(i, total - 1)
    grp = sum((ii >= e).astype(jnp.int32) for e in edges[:-1])

    def pick(vals):
        return jnp.where(grp == 0, vals[0], jnp.where(grp == 1, vals[1], jnp.where(grp == 2, vals[2], vals[3])))

    first = pick([0] + edges[:-1])
    return grp, total, ii == first, pick(fills) - (ii - first) * FFN_BM


def _ffn_group_kernel(tab_ref, xs_ref, w1_ref, w3_ref, w2_ref, ys_ref, w1b_ref, w3b_ref, w2b_ref, *, meta):
    i = pl.program_id(0)
    _, total, first_of_group, valid = _ffn_lookup(i, tab_ref, meta)
    active = i < total

    @pl.when(jnp.logical_and(active, first_of_group))
    def _():
        w1b_ref[...] = w1_ref[...].astype(BF16)
        w3b_ref[...] = w3_ref[...].astype(BF16)
        w2b_ref[...] = w2_ref[...].astype(BF16)

    def run(rows):
        hb = xs_ref[0:rows, 0:D_MODEL].astype(BF16)
        ex = xs_ref[0:rows, D_MODEL:XS_W]
        acc = None
        for e in range(EXPERTS_PER_GROUP):
            a = _bdot(hb, w1b_ref[e])
            b = _bdot(hb, w3b_ref[e])
            gate = ex[:, e:e + 1] + ex[:, GATE_LO + e:GATE_LO + e + 1]
            hid = (a * jax.nn.sigmoid(a)) * b * gate
            part = _bdot(hid.astype(BF16), w2b_ref[e])
            acc = part if acc is None else acc + part
        ys_ref[0:rows, :] = acc

    @pl.when(jnp.logical_and(active, valid > FFN_HALF))
    def _():
        run(FFN_BM)

    @pl.when(jnp.logical_and(active, valid <= FFN_HALF))
    def _():
        run(FFN_HALF)
        ys_ref[FFN_HALF:, :] = jnp.zeros((FFN_BM - FFN_HALF, D_MODEL), F32)

    @pl.when(jnp.logical_not(active))
    def _():
        ys_ref[...] = jnp.zeros_like(ys_ref)


def _ffn_group(tab, xs, w1, w3, w2, layer, meta):
    e4 = EXPERTS_PER_GROUP
    n_blocks = xs.shape[0] // FFN_BM
    group_of = lambda i, tab_ref: _ffn_lookup(i, tab_ref, meta)[0]
    grid_spec = pltpu.PrefetchScalarGridSpec(
        num_scalar_prefetch=1,
        grid=(n_blocks,),
        in_specs=[
            pl.BlockSpec((FFN_BM, XS_W), lambda i, tab_ref: (i, 0)),
            pl.BlockSpec((None, e4, D_MODEL, D_EXPERT), lambda i, tab_ref: (layer, group_of(i, tab_ref), 0, 0)),
            pl.BlockSpec((None, e4, D_MODEL, D_EXPERT), lambda i, tab_ref: (layer, group_of(i, tab_ref), 0, 0)),
            pl.BlockSpec((None, e4, D_EXPERT, D_MODEL), lambda i, tab_ref: (layer, group_of(i, tab_ref), 0, 0)),
        ],
        out_specs=pl.BlockSpec((FFN_BM, D_MODEL), lambda i, tab_ref: (i, 0)),
        scratch_shapes=[
            pltpu.VMEM((e4, D_MODEL, D_EXPERT), BF16),
            pltpu.VMEM((e4, D_MODEL, D_EXPERT), BF16),
            pltpu.VMEM((e4, D_EXPERT, D_MODEL), BF16),
        ],
    )
    return pl.pallas_call(
        functools.partial(_ffn_group_kernel, meta=meta),
        grid_spec=grid_spec,
        out_shape=jax.ShapeDtypeStruct((xs.shape[0], D_MODEL), F32),
        compiler_params=_cparams("arbitrary"),
        name="moe_ffn",
    )(tab, xs, w1, w3, w2)


def _fetch_runs(tab_ref, ys_ref, runs_ref, sem_ref, tile, slot):
    _run_copies(tab_ref, tile, ys_ref, runs_ref.at[slot], sem_ref.at[slot], to_hbm=False, wait=False)


def _unsort(tab_ref, ys_ref, runs_ref, sem_ref, dloc_col, tile, slot):
    covered = _tile_rows(tab_ref, tile)
    _wait_rows(covered, ys_ref, runs_ref.at[slot], sem_ref.at[slot], False)
    runs_ref[slot, pl.ds(pl.multiple_of(covered, ROUTE_PAD), LANES), :] = jnp.zeros((LANES, D_MODEL), F32)
    yb = runs_ref[slot, 0:SORT_ROWS, :].astype(BF16)
    row_f = lax.broadcasted_iota(jnp.int32, (ROUTE_TM, SORT_ROWS), 1).astype(F32)
    return _bdot(jnp.where(row_f == dloc_col, 1.0, 0.0).astype(BF16), yb)


def _combine_kernel(tab_ref, *refs, n_steps, first_step, n_p_steps, final_norm):
    with_prompt = first_step < n_p_steps
    if with_prompt:
        xp_ref, xs_ref, mod_ref, dloc_ref, fg_ref, ys_ref, op_ref, os_ref, runs_ref, sem_ref = refs
    else:
        xs_ref, mod_ref, dloc_ref, fg_ref, ys_ref, os_ref, runs_ref, sem_ref = refs
    i = pl.program_id(0)
    par = lax.rem(i, 2)
    step = first_step + i
    tm = ROUTE_TM

    def fetch(st, parity):
        for sub in range(ROUTE_SUBS):
            _fetch_runs(tab_ref, ys_ref, runs_ref, sem_ref, st * ROUTE_SUBS + sub, parity * ROUTE_SUBS + sub)

    @pl.when(i == 0)
    def _():
        fetch(first_step, 0)

    @pl.when(i + 1 < n_steps)
    def _():
        fetch(step + 1, 1 - par)

    parts = [_unsort(tab_ref, ys_ref, runs_ref, sem_ref, dloc_ref[sub * tm:(sub + 1) * tm, 0:1],
                     step * ROUTE_SUBS + sub, par * ROUTE_SUBS + sub) for sub in range(ROUTE_SUBS)]
    is_prompt = step < n_p_steps
    x = jnp.where(is_prompt, xp_ref[...], xs_ref[...]) if with_prompt else xs_ref[...]
    x2 = x + mod_ref[5:6, :] * jnp.concatenate(parts, axis=0)
    if final_norm:
        x2 = _rms(x2, fg_ref[...])
    if with_prompt:
        @pl.when(is_prompt)
        def _():
            op_ref[...] = x2

        @pl.when(jnp.logical_not(is_prompt))
        def _():
            os_ref[...] = x2
    else:
        os_ref[...] = x2


def _combine(tab, xp, xs, mod, dloc, final_g, ys, layer, sample_seq, final_norm, with_prompt):
    n_p_steps = xp.shape[0] // ROUTE_STEP
    n_s_steps = xs.shape[0] // ROUTE_STEP
    first_step = 0 if with_prompt else n_p_steps
    n_steps = n_p_steps + n_s_steps - first_step
    step_of = lambda i, tab_ref: i + first_step
    p_spec, s_spec, mod_spec = _merged_specs(n_p_steps, n_s_steps, layer, sample_seq, step_of)
    streams = [p_spec, s_spec] if with_prompt else [s_spec]
    grid_spec = pltpu.PrefetchScalarGridSpec(
        num_scalar_prefetch=1,
        grid=(n_steps,),
        in_specs=streams + [
            mod_spec,
            pl.BlockSpec((ROUTE_STEP, LANES), lambda i, tab_ref: (i + first_step, 0)),
            pl.BlockSpec((1, D_MODEL), lambda i, tab_ref: (0, 0)),
            pl.BlockSpec(memory_space=pl.ANY),
        ],
        out_specs=streams,
        scratch_shapes=[
            pltpu.VMEM((2 * ROUTE_SUBS, RUN_ROWS, D_MODEL), F32),
            pltpu.SemaphoreType.DMA((2 * ROUTE_SUBS,)),
        ],
    )
    arrays = [xp, xs] if with_prompt else [xs]
    return pl.pallas_call(
        functools.partial(_combine_kernel, n_steps=n_steps, first_step=first_step, n_p_steps=n_p_steps,
                          final_norm=final_norm),
        grid_spec=grid_spec,
        out_shape=[jax.ShapeDtypeStruct(a.shape, F32) for a in arrays],
        compiler_params=_cparams("arbitrary"),
        name="moe_combine",
    )(tab, *arrays, mod, dloc, final_g, ys)


def _moe(xp, xs, mod, g2, w_r, b_r, w1, w3, w2, final_g, layer, sample_seq, final_norm, defer_prompt):
    t = xp.shape[0] + xs.shape[0]
    n_tiles = t // ROUTE_TM
    max_rows = t + N_GROUPS_MOE * (ROUTE_PAD - 1) * n_tiles
    n_rows = (-(-max_rows // FFN_BM) + N_GROUPS_MOE) * FFN_BM
    dloc, tab, sorted_x = _route(xp, xs, mod, g2, w_r, b_r, layer, sample_seq, n_rows)
    ys = _ffn_group(tab, sorted_x, w1, w3, w2, layer, n_tiles * TAB_W)
    out = _combine(tab, xp, xs, mod, dloc, final_g, ys, layer, sample_seq, final_norm, not defer_prompt)
    if defer_prompt:
        return None, out[0], (tab, dloc, ys)
    return out[0], out[1], None


def _rope_tables(seq):
    half = QK_ROPE // 2
    nf = half // 2
    inv = ROPE_BASE ** (-np.arange(nf, dtype=np.float64) / nf)
    pos = np.arange(seq)
    row = (pos // GRID_W).astype(np.float64)
    col = (pos % GRID_W).astype(np.float64)
    cos = np.ones((seq, HEAD_PAD), np.float64)
    sin_a = np.zeros((seq, HEAD_PAD), np.float64)
    sin_b = np.zeros((seq, HEAD_PAD), np.float64)
    for part, p in enumerate((row, col)):
        ang = p[:, None] * inv[None, :]
        base = ROPE_OFF + part * half
        cos[:, base:base + nf] = np.cos(ang)
        cos[:, base + nf:base + half] = np.cos(ang)
        sin_a[:, base:base + nf] = -np.sin(ang)
        sin_b[:, base + nf:base + half] = np.sin(ang)
    return tuple(jnp.asarray(a, F32) for a in (cos, sin_a, sin_b))


def _apply_rope(x, cos, sin_a, sin_b, reps):
    nf = QK_ROPE // 4
    width = x.shape[1]
    if reps > 1:
        cos, sin_a, sin_b = (jnp.concatenate([a] * reps, axis=1) for a in (cos, sin_a, sin_b))
    return x * cos + pltpu.roll(x, width - nf, axis=1) * sin_a + pltpu.roll(x, nf, axis=1) * sin_b


def _odd_in_kernel(*refs, rope, emit_cache):
    x_ref, mod_ref, g_ref, w_ref, qg_ref, wq_ref, kg_ref, wk_ref, wv_ref, cs_ref = refs[:10]
    refs = refs[10:]
    if rope:
        cos_ref, sa_ref, sb_ref = refs[:3]
        refs = refs[3:]
    y_ref, q_ref, k_ref, v_ref = refs[:4]
    refs = refs[4:]
    m = mod_ref[...]
    h = _rms(x_ref[...], g_ref[...]) * (1.0 + m[1:2]) + m[0:1]
    z = _bdot(h.astype(BF16), w_ref[...])
    zc = z[:, 0:D_C]
    qc = z[:, D_C:D_C + Q_LORA]
    kvc = z[:, D_C + Q_LORA:D_C + Q_LORA + KV_LORA]
    kpe = z[:, D_C + Q_LORA + KV_LORA:]
    q = _bdot(_rms(qc, qg_ref[...]).astype(BF16), wq_ref[...])
    kvn = _rms(kvc, kg_ref[...])
    if emit_cache:
        ckv_ref, kpe_ref = refs
        ckv_ref[...] = kvn
        kpe_ref[...] = kpe[:, ROPE_OFF:ROPE_OFF + QK_ROPE]
    if rope:
        tabs = (cos_ref[...], sa_ref[...], sb_ref[...])
        q = _apply_rope(q, *tabs, reps=N_HEADS)
        kpe = _apply_rope(kpe, *tabs, reps=1)
    kvb = kvn.astype(BF16)
    k = _bdot(kvb, wk_ref[...]) + jnp.concatenate([kpe] * N_HEADS, axis=1)
    scale = math.log2(math.e) / math.sqrt(QK_NOPE + QK_ROPE)
    q_ref[...] = (q * scale).astype(BF16)
    k_ref[...] = k.astype(BF16)
    v_ref[...] = _bdot(kvb, wv_ref[...]).astype(BF16)
    y = _bdot(zc.astype(BF16), cs_ref[...])
    y_ref[0, :, :] = y[:, 0:D_C].astype(BF16)
    y_ref[1, :, :] = y[:, D_C:2 * D_C].astype(BF16)


def _odd_in(x3, mod, g, w_in, q_g, w_q, kv_g, w_k, w_v, cs, stream, layer, rope_tabs, emit_cache):
    b, s, _ = x3.shape
    tm = min(s, 512)
    n_i = s // tm
    rope = rope_tabs is not None
    const = lambda a: pl.BlockSpec(a.shape, lambda bi, i: (0,) * a.ndim)
    in_specs = [
        pl.BlockSpec((None, tm, D_MODEL), lambda bi, i: (bi, i, 0)),
        _mod_spec(stream, layer),
        const(g), const(w_in), const(q_g), const(w_q), const(kv_g), const(w_k), const(w_v), const(cs),
    ]
    args = [x3, mod, g, w_in, q_g, w_q, kv_g, w_k, w_v, cs]
    if rope:
        in_specs += [pl.BlockSpec((tm, HEAD_PAD), lambda bi, i: (i, 0))] * 3
        args += list(rope_tabs)
    hq = N_HEADS * HEAD_PAD
    out_specs = [
        pl.BlockSpec((None, 2, tm, D_C), lambda bi, i: (bi, 0, i, 0)),
        pl.BlockSpec((None, tm, hq), lambda bi, i: (bi, i, 0)),
        pl.BlockSpec((None, tm, hq), lambda bi, i: (bi, i, 0)),
        pl.BlockSpec((None, tm, N_HEADS * V_DIM), lambda bi, i: (bi, i, 0)),
    ]
    out_shape = [
        jax.ShapeDtypeStruct((b, 2, s, D_C), BF16),
        jax.ShapeDtypeStruct((b, s, hq), BF16),
        jax.ShapeDtypeStruct((b, s, hq), BF16),
        jax.ShapeDtypeStruct((b, s, N_HEADS * V_DIM), BF16),
    ]
    if emit_cache:
        out_specs += [
            pl.BlockSpec((None, tm, KV_LORA), lambda bi, i: (bi, i, 0)),
            pl.BlockSpec((None, tm, QK_ROPE), lambda bi, i: (bi, i, 0)),
        ]
        out_shape += [
            jax.ShapeDtypeStruct((b, s, KV_LORA), F32),
            jax.ShapeDtypeStruct((b, s, QK_ROPE), F32),
        ]
    return pl.pallas_call(
        functools.partial(_odd_in_kernel, rope=rope, emit_cache=emit_cache),
        grid=(b, n_i),
        in_specs=in_specs,
        out_specs=out_specs,
        out_shape=out_shape,
        compiler_params=_cparams("parallel", "parallel"),
        name="odd_in",
    )(*args)


def _cache_kv_kernel(c_ref, p_ref, wk_ref, wv_ref, k_ref, v_ref):
    cb = c_ref[...].astype(BF16)
    k = _bdot(cb, wk_ref[...]) + jnp.concatenate([p_ref[...]] * N_HEADS, axis=1)
    k_ref[...] = k.astype(BF16)
    v_ref[...] = _bdot(cb, wv_ref[...]).astype(BF16)


def _cache_kv(ckv, kpe_blk, w_k, w_v):
    b, p, _ = ckv.shape
    hq = N_HEADS * HEAD_PAD
    return pl.pallas_call(
        _cache_kv_kernel,
        grid=(b,),
        in_specs=[
            pl.BlockSpec((None, p, KV_LORA), lambda bi: (bi, 0, 0)),
            pl.BlockSpec((None, p, HEAD_PAD), lambda bi: (bi, 0, 0)),
            pl.BlockSpec(w_k.shape, lambda bi: (0, 0)),
            pl.BlockSpec(w_v.shape, lambda bi: (0, 0)),
        ],
        out_specs=[
            pl.BlockSpec((None, p, hq), lambda bi: (bi, 0, 0)),
            pl.BlockSpec((None, p, N_HEADS * V_DIM), lambda bi: (bi, 0, 0)),
        ],
        out_shape=[
            jax.ShapeDtypeStruct((b, p, hq), BF16),
            jax.ShapeDtypeStruct((b, p, N_HEADS * V_DIM), BF16),
        ],
        compiler_params=_cparams("parallel"),
        name="cache_kv",
    )(ckv, kpe_blk, w_k, w_v)


def _odd_mix_kernel(*refs, with_cache):
    q_ref, k_ref, v_ref = refs[:3]
    refs = refs[3:]
    if with_cache:
        kc_ref, vc_ref = refs[:2]
        refs = refs[2:]
    y_ref, f_ref, x_ref, mod_ref, wo_ref, o_ref, a_ref = refs
    tq = q_ref.shape[0]
    lane = lax.broadcasted_iota(jnp.int32, (tq, 2 * V_DIM), 1)
    for pair in range(N_HEADS // 2):
        vcols = slice(pair * 2 * V_DIM, (pair + 1) * 2 * V_DIM)
        outs = []
        for h in (2 * pair, 2 * pair + 1):
            hcols = slice(h * HEAD_PAD, (h + 1) * HEAD_PAD)
            qh = q_ref[:, hcols]
            s = lax.dot_general(qh, k_ref[:, hcols], NT_DIMS, preferred_element_type=F32)
            top = jnp.max(s, axis=-1, keepdims=True)
            if with_cache:
                sc = lax.dot_general(qh, kc_ref[:, hcols], NT_DIMS, preferred_element_type=F32)
                top = jnp.maximum(top, jnp.max(sc, axis=-1, keepdims=True))
            p = jnp.exp2(s - top)
            den = jnp.sum(p, axis=-1, keepdims=True)
            o = _bdot(p.astype(BF16), v_ref[:, vcols])
            if with_cache:
                pc = jnp.exp2(sc - top)
                den = den + jnp.sum(pc, axis=-1, keepdims=True)
                o = o + _bdot(pc.astype(BF16), vc_ref[:, vcols])
            outs.append(o / den)
        a_ref[:, vcols] = jnp.where(lane < V_DIM, outs[0], outs[1]).astype(BF16)
    f = _bdot(f_ref[...], y_ref[...].reshape(-1, D_C))
    o = _bdot(f.astype(BF16), wo_ref[0:D_C, :]) + _bdot(a_ref[...], wo_ref[D_C:, :])
    o_ref[...] = x_ref[...] + mod_ref[2:3, :] * o


def _odd_whole_kernel(tab_ref, x_ref, mod_ref, g_ref, w_ref, qg_ref, wq_ref, kg_ref, wk_ref, wv_ref, cs_ref, f_ref,
                      wo_ref, modp_ref, dloc_ref, ys_ref, o_ref, ckv_ref, kpe_ref,
                      y_ref, q_ref, k_ref, v_ref, a_ref, x2_ref, runs_ref, sem_ref):
    bi = pl.program_id(0)
    par = lax.rem(bi, 2)

    @pl.when(bi == 0)
    def _():
        _fetch_runs(tab_ref, ys_ref, runs_ref, sem_ref, 0, 0)

    @pl.when(bi + 1 < pl.num_programs(0))
    def _():
        _fetch_runs(tab_ref, ys_ref, runs_ref, sem_ref, bi + 1, 1 - par)

    moe = _unsort(tab_ref, ys_ref, runs_ref, sem_ref, dloc_ref[:, 0:1], bi, par)
    x2_ref[...] = x_ref[...] + modp_ref[5:6, :] * moe
    _odd_in_kernel(x2_ref, mod_ref, g_ref, w_ref, qg_ref, wq_ref, kg_ref, wk_ref, wv_ref, cs_ref,
                   y_ref, q_ref, k_ref, v_ref, ckv_ref, kpe_ref, rope=False, emit_cache=True)
    _odd_mix_kernel(q_ref, k_ref, v_ref, y_ref, f_ref, x2_ref, mod_ref, wo_ref, o_ref, a_ref, with_cache=False)


def _odd_whole(x3, pending, mod, g, w_in, q_g, w_q, kv_g, w_k, w_v, cs, fmat, w_out, stream, layer):
    tab, dloc, ys = pending
    b, s, _ = x3.shape
    assert s == ROUTE_TM, "one sequence must be one MoE sort tile"
    hq = N_HEADS * HEAD_PAD
    hv = N_HEADS * V_DIM
    const = lambda a: pl.BlockSpec(a.shape, lambda bi, tab_ref: (0,) * a.ndim)
    row_block = lambda w: pl.BlockSpec((None, s, w), lambda bi, tab_ref: (bi, 0, 0))
    mod_block = lambda lyr: pl.BlockSpec((None, None, 6, D_MODEL),
                                         lambda bi, tab_ref: (lyr, stream.row_of_batch(bi), 0, 0))
    grid_spec = pltpu.PrefetchScalarGridSpec(
        num_scalar_prefetch=1,
        grid=(b,),
        in_specs=[
            row_block(D_MODEL), mod_block(layer),
            const(g), const(w_in), const(q_g), const(w_q), const(kv_g), const(w_k), const(w_v), const(cs),
            const(fmat), const(w_out),
            mod_block(layer - 1),
            pl.BlockSpec((s, LANES), lambda bi, tab_ref: (bi, 0)),
            pl.BlockSpec(memory_space=pl.ANY),
        ],
        out_specs=[row_block(D_MODEL), row_block(KV_LORA), row_block(QK_ROPE)],
        scratch_shapes=[
            pltpu.VMEM((2, s, D_C), BF16),
            pltpu.VMEM((s, hq), BF16),
            pltpu.VMEM((s, hq), BF16),
            pltpu.VMEM((s, hv), BF16),
            pltpu.VMEM((s, hv), BF16),
            pltpu.VMEM((s, D_MODEL), F32),
            pltpu.VMEM((2, RUN_ROWS, D_MODEL), F32),
            pltpu.SemaphoreType.DMA((2,)),
        ],
    )
    return pl.pallas_call(
        _odd_whole_kernel,
        grid_spec=grid_spec,
        out_shape=[
            jax.ShapeDtypeStruct((b, s, D_MODEL), F32),
            jax.ShapeDtypeStruct((b, s, KV_LORA), F32),
            jax.ShapeDtypeStruct((b, s, QK_ROPE), F32),
        ],
        compiler_params=_cparams("arbitrary"),
        name="odd_whole",
    )(tab, x3, mod, g, w_in, q_g, w_q, kv_g, w_k, w_v, cs, fmat, w_out, mod, dloc, ys)


def _odd_mix(q, k, v, kc, vc, y, fmat, x3, mod, w_out, stream, layer):
    b, s, hq = q.shape
    tq = min(s, 256)
    n_i = s // tq
    with_cache = kc is not None
    hv = N_HEADS * V_DIM
    mode = dict(pipeline_mode=pl.Buffered(1)) if with_cache else {}

    def per_batch(rows, cols):
        return pl.BlockSpec((None, rows, cols), lambda bi, i: (bi, 0, 0), **mode)

    in_specs = [pl.BlockSpec((None, tq, hq), lambda bi, i: (bi, i, 0)), per_batch(s, hq), per_batch(s, hv)]
    args = [q, k, v]
    if with_cache:
        p = kc.shape[1]
        in_specs += [per_batch(p, hq), per_batch(p, hv)]
        args += [kc, vc]
    in_specs += [
        per_batch(2 * s, D_C),
        pl.BlockSpec((tq, 2 * s), lambda bi, i: (i, 0)),
        pl.BlockSpec((None, tq, D_MODEL), lambda bi, i: (bi, i, 0)),
        _mod_spec(stream, layer),
        pl.BlockSpec(w_out.shape, lambda bi, i: (0, 0), **mode),
    ]
    args += [y, fmat, x3, mod, w_out]
    return pl.pallas_call(
        functools.partial(_odd_mix_kernel, with_cache=with_cache),
        grid=(b, n_i),
        in_specs=in_specs,
        out_specs=pl.BlockSpec((None, tq, D_MODEL), lambda bi, i: (bi, i, 0)),
        out_shape=jax.ShapeDtypeStruct((b, s, D_MODEL), F32),
        scratch_shapes=[pltpu.VMEM((tq, hv), BF16)],
        compiler_params=_cparams("parallel", "arbitrary"),
        name="odd_mix",
    )(*args)


def _dft_tables(seq):
    jc = np.arange(C_GW)
    ang_c = 2.0 * np.pi * np.outer(jc, jc) / C_GW
    eye = np.eye(C_GROUPS)
    cs = np.concatenate([np.kron(eye, np.cos(ang_c)), np.kron(eye, np.sin(ang_c))], axis=1)
    jn = np.arange(seq)
    ang_n = 2.0 * np.pi * (np.outer(jn, jn) % seq) / seq
    scale = 1.0 / math.sqrt(seq * C_GW)
    fmat = np.concatenate([np.cos(ang_n), -np.sin(ang_n)], axis=1) * scale
    return jnp.asarray(cs, F32).astype(BF16), jnp.asarray(fmat, F32).astype(BF16)


def _odd_weights(w_in, w_uq, w_ukv):
    d = w_in.shape[0]
    base = D_C + Q_LORA + KV_LORA
    kpe_blk = jnp.zeros((d, HEAD_PAD), w_in.dtype).at[:, ROPE_OFF:ROPE_OFF + QK_ROPE].set(w_in[:, base:])
    w_in_p = jnp.concatenate([w_in[:, :base], kpe_blk], axis=1).astype(BF16)
    qh = w_uq.reshape(Q_LORA, N_HEADS, QK_NOPE + QK_ROPE)
    w_q = jnp.pad(qh, ((0, 0), (0, 0), (0, HEAD_PAD - QK_NOPE - QK_ROPE))).reshape(Q_LORA, -1).astype(BF16)
    kvh = w_ukv.reshape(KV_LORA, N_HEADS, QK_NOPE + V_DIM)
    w_k = jnp.pad(kvh[:, :, :QK_NOPE], ((0, 0), (0, 0), (0, HEAD_PAD - QK_NOPE))).reshape(KV_LORA, -1)
    w_v = kvh[:, :, QK_NOPE:].reshape(KV_LORA, -1)
    return w_in_p, w_q, w_k.astype(BF16), w_v.astype(BF16)


ROUTER_ROWS = 32


def _router_weights(wg, bg, we, be):
    d = wg.shape[0]
    w = jnp.concatenate([wg, we.reshape(d, N_EXPERTS)], axis=1).T
    w = jnp.pad(w, ((0, ROUTER_ROWS - w.shape[0]), (0, 0))).astype(BF16)
    b = jnp.concatenate([bg, be.reshape(N_EXPERTS)])
    b = jnp.pad(b, (0, ROUTER_ROWS - b.shape[0])).reshape(ROUTER_ROWS, 1).astype(F32)
    return w, b


def kernel(x_prompt, x_sample, cache_ckv, cache_kpe, c, c_ctx, mod_w, mod_b, norm1_g, norm2_g,
           ev_w_in, ev_conv_w, ev_sgu_norm_g, ev_sgu_w, ev_sgu_b, ev_w_out,
           od_w_in, od_q_norm_g, od_w_uq, od_kv_norm_g, od_w_ukv, od_w_out,
           moe_wg, moe_bg, moe_we, moe_be, moe_w1, moe_w3, moe_w2, final_norm_g):
    bp, n_p, d = x_prompt.shape
    bs, n_s, _ = x_sample.shape
    streams = [(_Stream(bp, n_p, True), x_prompt), (_Stream(bs, n_s, False), x_sample)]

    n_rows = 1 + bs
    cond_t = jnp.concatenate([c_ctx[None, :], c], axis=0).T
    mod = _adaln(cond_t, mod_w, mod_b, n_rows)

    final_g = final_norm_g.reshape(1, d)
    xs = [x for _, x in streams]
    new_ckv, new_kpe = [], []
    pending = None
    for l in range(DEPTH):
        j = l // 2
        g1 = norm1_g[l].reshape(1, d)
        g2 = norm2_g[l].reshape(1, d)
        w_r, b_r = _router_weights(moe_wg[l], moe_bg[l], moe_we[l], moe_be[l])
        last = l == DEPTH - 1
        if l % 2 == 0:
            w_in = ev_w_in[j].astype(BF16)
            w_out = ev_w_out[j].astype(BF16)
            sgu_w = ev_sgu_w[j].astype(BF16)
            sgu_g = ev_sgu_norm_g[j].reshape(1, D_B)
            sgu_bias = jnp.repeat(ev_sgu_b[j].T, D_B // B_GROUPS, axis=1)
            for si, (st, _) in enumerate(streams):
                xs[si] = _even_layer(xs[si], mod, g1, w_in, ev_conv_w[j], sgu_g, sgu_w, sgu_bias, w_out, st, l)
        else:
            w_in, w_q, w_k, w_v = _odd_weights(od_w_in[j], od_w_uq[j], od_w_ukv[j])
            w_out = od_w_out[j].astype(BF16)
            q_g = od_q_norm_g[j].reshape(1, Q_LORA)
            kv_g = od_kv_norm_g[j].reshape(1, KV_LORA)
            for si, (st, _) in enumerate(streams):
                x3 = xs[si]
                cs, fmat = _dft_tables(st.seq)
                if st.shared_cond:
                    xs[si], ckv, kpe = _odd_whole(x3, pending, mod, g1, w_in, q_g, w_q, kv_g, w_k, w_v, cs, fmat,
                                                  w_out, st, l)
                    new_ckv.append(ckv)
                    new_kpe.append(kpe)
                    continue
                y, q, k, v = _odd_in(x3, mod, g1, w_in, q_g, w_q, kv_g, w_k, w_v, cs, st, l,
                                     _rope_tables(st.seq), False)
                kpe_blk = jnp.pad(cache_kpe[:, j], ((0, 0), (0, 0), (ROPE_OFF, HEAD_PAD - ROPE_OFF - QK_ROPE)))
                kc, vc = _cache_kv(cache_ckv[:, j], kpe_blk, w_k, w_v)
                xs[si] = _odd_mix(q, k, v, kc, vc, y.reshape(st.batch, 2 * st.seq, D_C), fmat, x3, mod,
                                  w_out, st, l)
        defer = not last and (l + 1) % 2 == 1
        x2p, x2s, pending = _moe(xs[0].reshape(bp * n_p, d), xs[1].reshape(bs * n_s, d), mod, g2, w_r, b_r,
                                 moe_w1, moe_w3, moe_w2, final_g, l, n_s, last, defer)
        xs = [xs[0] if defer else x2p.reshape(bp, n_p, d), x2s.reshape(bs, n_s, d)]
    return (xs[0], xs[1], jnp.stack(new_ckv, axis=1), jnp.stack(new_kpe, axis=1))
```

```python
import functools
import math

import numpy as np
import jax
import jax.numpy as jnp
from jax import lax
from jax.experimental import pallas as pl
from jax.experimental.pallas import tpu as pltpu

D_MODEL = 1024
DEPTH = 2
GRID_W = 64
D_A = D_MODEL // 2
D_B = D_MODEL // 2
B_GROUPS = 4
CHUNK = 128
D_EVEN_IN = 3 * D_A + 2 * D_B
D_C = D_MODEL // 4
C_GROUPS = 4
C_GW = D_C // C_GROUPS
N_HEADS = 12
QK_NOPE = 64
QK_ROPE = 32
V_DIM = 64
Q_LORA = 384
KV_LORA = 256
ROPE_BASE = 10000.0
N_GROUPS_MOE = 4
EXPERTS_PER_GROUP = 4
N_EXPERTS = N_GROUPS_MOE * EXPERTS_PER_GROUP
D_EXPERT = 256
EPS = 1e-6

LANES = 128
HEAD_PAD = 128
PAIR_W = 256
DEN_COL = 2 * 64
VX_W = 6 * PAIR_W
ROPE_OFF = QK_NOPE
GATE_OFF = N_GROUPS_MOE
NEG_BIG = -1e30
F32 = jnp.float32
BF16 = jnp.bfloat16
VMEM_LIMIT = 56 * 1024 * 1024


def _cparams(*sem):
    return pltpu.CompilerParams(dimension_semantics=sem, vmem_limit_bytes=VMEM_LIMIT)


def _rms(x, g):
    return x * lax.rsqrt(jnp.mean(x * x, axis=-1, keepdims=True) + EPS) * g


def _bdot(a, b):
    return jnp.dot(a, b, preferred_element_type=F32)


NT_DIMS = (((1,), (1,)), ((), ()))
TN_DIMS = (((0,), (0,)), ((), ()))


def _mod_kernel(ct_ref, w_ref, b_ref, o_ref):
    c = ct_ref[...]
    s = c * jax.nn.sigmoid(c)
    w = w_ref[...]
    b = b_ref[...]
    for r in range(o_ref.shape[0]):
        o_ref[r:r + 1, :] = jnp.sum(s[:, r:r + 1] * w, axis=0, keepdims=True) + b


def _adaln(cond_t, mod_w, mod_b, n_rows):
    nt = 1024
    d6 = mod_w.shape[-1]
    out = pl.pallas_call(
        _mod_kernel,
        grid=(DEPTH, d6 // nt),
        in_specs=[
            pl.BlockSpec(cond_t.shape, lambda l, n: (0, 0)),
            pl.BlockSpec((None, D_MODEL, nt), lambda l, n: (l, 0, n)),
            pl.BlockSpec((None, 1, nt), lambda l, n: (l, 0, n)),
        ],
        out_specs=pl.BlockSpec((None, n_rows, nt), lambda l, n: (l, 0, n)),
        out_shape=jax.ShapeDtypeStruct((DEPTH, n_rows, d6), F32),
        compiler_params=_cparams("parallel", "parallel"),
        name="adaln",
    )(cond_t, mod_w, mod_b.reshape(DEPTH, 1, d6))
    return out.reshape(DEPTH, n_rows, 6, D_MODEL)


class _Stream:
    def __init__(self, batch, seq, shared_cond):
        self.batch = batch
        self.seq = seq
        self.tokens = batch * seq
        self.shared_cond = shared_cond

    def row_of_batch(self, b):
        return 0 if self.shared_cond else b + 1

    def row_of_tile(self, i, tm):
        return 0 if self.shared_cond else (i * tm) // self.seq + 1


def _mod_spec(stream, layer, tm=None):
    if tm is None:
        return pl.BlockSpec((None, None, 6, D_MODEL), lambda b, i: (layer, stream.row_of_batch(b), 0, 0))
    return pl.BlockSpec((None, None, 6, D_MODEL), lambda i, *_: (layer, stream.row_of_tile(i, tm), 0, 0))


HALO = 8


def _even_kernel(*refs, has_halo, nc):
    if has_halo:
        x_ref, xp_ref, xn_ref = refs[:3]
        refs = refs[3:]
    else:
        x_ref = refs[0]
        refs = refs[1:]
    mod_ref, g_ref, wi_ref, cw_ref, sg_ref, sw_ref, sb_ref, wo_ref, o_ref, z_ref, y_ref = refs
    i = pl.program_id(1)
    n_i = pl.num_programs(1)
    ts = x_ref.shape[0]
    m = mod_ref[...]
    g = g_ref[...]

    def modulate(x):
        return (_rms(x, g) * (1.0 + m[1:2]) + m[0:1]).astype(BF16)

    x = x_ref[...]
    hb = modulate(x)
    for n in range(D_EVEN_IN // nc):
        z_ref[:, n * nc:(n + 1) * nc] = _bdot(hb, wi_ref[:, n * nc:(n + 1) * nc]).astype(BF16)

    gate_b = z_ref[:, 0:D_A].astype(F32)
    gate_c = z_ref[:, D_A:2 * D_A].astype(F32)
    xa = z_ref[:, 2 * D_A:3 * D_A].astype(F32)
    t = gate_c * xa
    t_prev = pltpu.roll(t, 1, axis=0)
    t_next = pltpu.roll(t, ts - 1, axis=0)
    row = lax.broadcasted_iota(jnp.int32, (ts, 1), 0)
    if has_halo:
        hh = modulate(jnp.concatenate([xp_ref[...], xn_ref[...]], axis=0))
        zh = _bdot(hh, wi_ref[:, D_A:3 * D_A]).astype(BF16).astype(F32)
        th = zh[:, 0:D_A] * zh[:, D_A:2 * D_A]
        tp = th[HALO - 1:HALO] * (i > 0).astype(F32)
        tn = th[HALO:HALO + 1] * (i < n_i - 1).astype(F32)
    else:
        tp = tn = 0.0
    t_prev = jnp.where(row == 0, tp, t_prev)
    t_next = jnp.where(row == ts - 1, tn, t_next)
    cw = cw_ref[...]
    y_a = gate_b * (t_prev * cw[0:1] + t * cw[1:2] + t_next * cw[2:3])
    y_ref[:, 0:D_A] = y_a.astype(BF16)

    u = z_ref[:, 3 * D_A:3 * D_A + D_B].astype(F32)
    v = z_ref[:, 3 * D_A + D_B:3 * D_A + 2 * D_B].astype(F32)
    vb = _rms(v, sg_ref[...]).astype(BF16)
    gw = D_B // B_GROUPS
    for c in range(ts // CHUNK):
        rows = slice(c * CHUNK, (c + 1) * CHUNK)
        for gi in range(B_GROUPS):
            cols = slice(gi * gw, (gi + 1) * gw)
            sv = _bdot(sw_ref[gi], vb[rows, cols]) + sb_ref[:, cols]
            y_ref[rows, D_A + gi * gw:D_A + (gi + 1) * gw] = (u[rows, cols] * sv).astype(BF16)

    o_ref[...] = x + m[2:3] * _bdot(y_ref[...], wo_ref[...])


def _even_layer(x3, mod, g, w_in, conv_w, sgu_g, sgu_w, sgu_bias, w_out, stream, layer):
    b, s, _ = x3.shape
    ts = min(s, 256)
    n_i = s // ts
    has_halo = n_i > 1
    hb = ts // HALO
    last_h = s // HALO - 1
    const = lambda a: pl.BlockSpec(a.shape, lambda bi, i: (0,) * a.ndim)
    in_specs = [pl.BlockSpec((None, ts, D_MODEL), lambda bi, i: (bi, i, 0))]
    args = [x3]
    if has_halo:
        in_specs += [
            pl.BlockSpec((None, HALO, D_MODEL), lambda bi, i: (bi, jnp.maximum(i * hb - 1, 0), 0)),
            pl.BlockSpec((None, HALO, D_MODEL), lambda bi, i: (bi, jnp.minimum((i + 1) * hb, last_h), 0)),
        ]
        args += [x3, x3]
    in_specs += [_mod_spec(stream, layer), const(g), const(w_in), const(conv_w), const(sgu_g), const(sgu_w),
                 const(sgu_bias), const(w_out)]
    args += [mod, g, w_in, conv_w, sgu_g, sgu_w, sgu_bias, w_out]
    return pl.pallas_call(
        functools.partial(_even_kernel, has_halo=has_halo, nc=512),
        grid=(b, n_i),
        in_specs=in_specs,
        out_specs=pl.BlockSpec((None, ts, D_MODEL), lambda bi, i: (bi, i, 0)),
        out_shape=jax.ShapeDtypeStruct((b, s, D_MODEL), F32),
        scratch_shapes=[pltpu.VMEM((ts, D_EVEN_IN), BF16), pltpu.VMEM((ts, D_A + D_B), BF16)],
        compiler_params=_cparams("parallel", "parallel"),
        name="even_layer",
    )(*args)


ROUTE_TM = 256
ROUTE_SUBS = 2
ROUTE_STEP = ROUTE_TM * ROUTE_SUBS
ROUTE_PAD = 8
SORT_ROWS = ROUTE_TM + LANES
RUN_ROWS = SORT_ROWS + 32
XS_W = D_MODEL + LANES
GATE_LO = EXPERTS_PER_GROUP
DLOC_HI = 2 * EXPERTS_PER_GROUP
DLOC_RADIX = 16.0
FFN_BM = 512
FFN_HALF = FFN_BM // 2
RUN_SIZES = (256, 128, 64, 32, 16, 8)
TAB_W = 2 * N_GROUPS_MOE


def _round_up(x, m):
    return lax.div(x + (m - 1), m) * m


def _run_copies(tab_ref, tile, hbm_ref, vmem_ref, sem, to_hbm, wait):
    off = 0
    for g in range(N_GROUPS_MOE):
        start = tab_ref[tile * TAB_W + g]
        n = tab_ref[tile * TAB_W + N_GROUPS_MOE + g]
        for p in RUN_SIZES:
            done = n & (-2 * p)

            @pl.when((n & p) != 0)
            def _():
                v = vmem_ref.at[pl.ds(pl.multiple_of(off + done, ROUTE_PAD), p)]
                h = hbm_ref.at[pl.ds(pl.multiple_of(start + done, ROUTE_PAD), p)]
                cp = pltpu.make_async_copy(v, h, sem) if to_hbm else pltpu.make_async_copy(h, v, sem)
                if wait:
                    cp.wait()
                else:
                    cp.start()
        off = off + n


def _zero_fill(tab_ref, meta, zeros_ref, hbm_ref, sem, n_rows, wait):
    def copy(rows, dst_row):
        cp = pltpu.make_async_copy(zeros_ref.at[pl.ds(0, rows)],
                                   hbm_ref.at[pl.ds(pl.multiple_of(dst_row, ROUTE_PAD), rows)], sem)
        if wait:
            cp.wait()
        else:
            cp.start()

    end = 0
    for g in range(N_GROUPS_MOE):
        fill = tab_ref[meta + g]
        start = tab_ref[meta + N_GROUPS_MOE + g]
        end = start + _round_up(fill, FFN_BM)
        tail = end - start - fill
        for p in RUN_SIZES:
            pl.when((tail & p) != 0)(functools.partial(copy, p, start + fill + (tail & (-2 * p))))
    for k in range(n_rows // FFN_BM):
        pl.when(end + k * FFN_BM < n_rows)(functools.partial(copy, FFN_BM, end + k * FFN_BM))


def _wait_rows(n, hbm_ref, vmem_ref, sem, to_hbm):
    for p in RUN_SIZES:
        @pl.when((n & p) != 0)
        def _():
            v = vmem_ref.at[pl.ds(0, p)]
            h = hbm_ref.at[pl.ds(0, p)]
            (pltpu.make_async_copy(v, h, sem) if to_hbm else pltpu.make_async_copy(h, v, sem)).wait()


def _tile_rows(tab_ref, tile):
    n = 0
    for g in range(N_GROUPS_MOE):
        n = n + tab_ref[tile * TAB_W + N_GROUPS_MOE + g]
    return n


def _max4(v):
    return jnp.maximum(jnp.maximum(v[0], v[1]), jnp.maximum(v[2], v[3]))


def _first_of4(v, top):
    return jnp.where(v[0] == top, 0.0, jnp.where(v[1] == top, 1.0, jnp.where(v[2] == top, 2.0, 3.0)))


def _route_kernel(xp_ref, xs_ref, mod_ref, g_ref, wrt_ref, brt_ref, upper_ref,
                  dloc_ref, tab_ref, sorted_hbm, hbuf_ref, drow_ref, sorted_ref, zeros_ref, cnt_ref, fill_ref,
                  sem_ref, zsem_ref, *, n_steps, n_p_steps):
    sweep = pl.program_id(0)
    i = pl.program_id(1)
    tm = ROUTE_TM
    meta = n_steps * ROUTE_SUBS * TAB_W
    n_rows = sorted_hbm.shape[0]
    ng = N_GROUPS_MOE

    @pl.when(sweep == 0)
    def _():
        @pl.when(i == 0)
        def _():
            for g in range(ng):
                fill_ref[g] = 0

        m = mod_ref[...]
        x = jnp.where(i < n_p_steps, xp_ref[...], xs_ref[...])
        h = _rms(x, g_ref[...]) * (1.0 + m[4:5]) + m[3:4]
        hb = h.astype(BF16)
        lt = lax.dot_general(wrt_ref[...], hb, (((1,), (1,)), ((), ())), preferred_element_type=F32) + brt_ref[...]
        gl = [lt[r:r + 1, :] for r in range(ng)]
        g_top = _max4(gl)
        g_idx = _first_of4(gl, g_top)
        g_w = 1.0 / (jnp.exp(gl[0] - g_top) + jnp.exp(gl[1] - g_top) + jnp.exp(gl[2] - g_top) + jnp.exp(gl[3] - g_top))
        ev = []
        for k in range(EXPERTS_PER_GROUP):
            cand = [lt[GATE_OFF + EXPERTS_PER_GROUP * r + k:GATE_OFF + EXPERTS_PER_GROUP * r + k + 1, :]
                    for r in range(ng)]
            ev.append(jnp.where(g_idx == 0.0, cand[0], jnp.where(g_idx == 1.0, cand[1],
                                jnp.where(g_idx == 2.0, cand[2], cand[3]))))
        v1 = _max4(ev)
        i1 = _first_of4(ev, v1)
        rest = [jnp.where(i1 == float(k), NEG_BIG, ev[k]) for k in range(EXPERTS_PER_GROUP)]
        v2 = _max4(rest)
        i2 = _first_of4(rest, v2)
        e2 = jnp.exp(v2 - v1)
        w1 = 1.0 / (1.0 + e2)
        w2 = e2 * w1
        gates = [g_w * (jnp.where(i1 == float(k), w1, 0.0) + jnp.where(i2 == float(k), w2, 0.0))
                 for k in range(EXPERTS_PER_GROUP)]

        sub8 = lax.broadcasted_iota(jnp.int32, (8, tm), 0).astype(F32)
        dlocs = []
        for sub in range(ROUTE_SUBS):
            gi = g_idx[:, sub * tm:(sub + 1) * tm]
            hot = jnp.where(sub8 == gi, 1.0, 0.0)
            before = _bdot(hot.astype(BF16), upper_ref[...])
            dl = jnp.sum(before * hot, axis=0, keepdims=True)
            off = 0
            for g in range(ng):
                n_g = _round_up(jnp.sum(hot[g:g + 1, :]).astype(jnp.int32), ROUTE_PAD)
                cnt_ref[(i * ROUTE_SUBS + sub) * ng + g] = n_g
                fill_ref[g] = fill_ref[g] + n_g
                dl = dl + jnp.where(gi == float(g), off.astype(F32) if g else 0.0, 0.0)
                off = off + n_g
            drow_ref[i * ROUTE_SUBS + sub] = jnp.broadcast_to(dl, (8, tm))
            dlocs.append(dl)
        dloc = jnp.concatenate(dlocs, axis=1)
        d_hi = jnp.floor(dloc * (1.0 / DLOC_RADIX))
        g_hi = [gt.astype(BF16).astype(F32) for gt in gates]
        ex_rows = g_hi + [gt - gh for gt, gh in zip(gates, g_hi)] + [d_hi, dloc - DLOC_RADIX * d_hi]
        sub16 = lax.broadcasted_iota(jnp.int32, (16, ROUTE_STEP), 0)
        ex_t = jnp.zeros((16, ROUTE_STEP), F32)
        for r, row in enumerate(ex_rows):
            ex_t = jnp.where(sub16 == r, row, ex_t)
        ex_t = jnp.concatenate([ex_t, jnp.zeros((LANES - 16, ROUTE_STEP), F32)], axis=0)
        extras = ex_t.T
        dloc_ref[...] = jnp.broadcast_to(
            DLOC_RADIX * extras[:, DLOC_HI:DLOC_HI + 1] + extras[:, DLOC_HI + 1:DLOC_HI + 2], (ROUTE_STEP, LANES))
        hbuf_ref[pl.ds(pl.multiple_of(i * ROUTE_STEP, ROUTE_STEP), ROUTE_STEP), :] = (
            jnp.concatenate([hb, extras.astype(BF16)], axis=1))

    @pl.when(sweep == 1)
    def _():
        @pl.when(i == 0)
        def _():
            start = 0
            for g in range(ng):
                tab_ref[meta + g] = fill_ref[g]
                tab_ref[meta + ng + g] = start
                start = start + _round_up(fill_ref[g], FFN_BM)
                fill_ref[g] = 0
            zeros_ref[...] = jnp.zeros_like(zeros_ref)

        row_f = lax.broadcasted_iota(jnp.int32, (SORT_ROWS, tm), 0).astype(F32)
        for sub in range(ROUTE_SUBS):
            tile = i * ROUTE_SUBS + sub

            @pl.when(i >= 1)
            def _():
                _wait_rows(_tile_rows(tab_ref, tile - ROUTE_SUBS), sorted_hbm, sorted_ref.at[sub], sem_ref.at[sub], True)

            rows = hbuf_ref[pl.ds(pl.multiple_of(i * ROUTE_STEP + sub * tm, tm), tm), :]
            onehot = jnp.where(row_f == drow_ref[tile][0:1, :], 1.0, 0.0).astype(BF16)
            sorted_ref[sub] = _bdot(onehot, rows)
            for g in range(ng):
                n_g = cnt_ref[tile * ng + g]
                tab_ref[tile * TAB_W + g] = tab_ref[meta + ng + g] + fill_ref[g]
                tab_ref[tile * TAB_W + ng + g] = n_g
                fill_ref[g] = fill_ref[g] + n_g
            _run_copies(tab_ref, tile, sorted_hbm, sorted_ref.at[sub], sem_ref.at[sub], to_hbm=True, wait=False)

            @pl.when(i == n_steps - 1)
            def _():
                _wait_rows(_tile_rows(tab_ref, tile), sorted_hbm, sorted_ref.at[sub], sem_ref.at[sub], True)

        @pl.when(i == n_steps - 1)
        def _():
            _zero_fill(tab_ref, meta, zeros_ref, sorted_hbm, zsem_ref, n_rows, wait=False)
            _zero_fill(tab_ref, meta, zeros_ref, sorted_hbm, zsem_ref, n_rows, wait=True)


def _merged_specs(n_p_steps, n_s_steps, layer, sample_seq, step_of):
    def p_map(*idx):
        return (jnp.minimum(step_of(*idx), n_p_steps - 1), 0)

    def s_map(*idx):
        return (jnp.clip(step_of(*idx) - n_p_steps, 0, n_s_steps - 1), 0)

    def mod_map(*idx):
        j = step_of(*idx)
        row = jnp.where(j < n_p_steps, 0, 1 + lax.div(jnp.maximum(j - n_p_steps, 0) * ROUTE_STEP, sample_seq))
        return (layer, row, 0, 0)

    return (pl.BlockSpec((ROUTE_STEP, D_MODEL), p_map), pl.BlockSpec((ROUTE_STEP, D_MODEL), s_map),
            pl.BlockSpec((None, None, 6, D_MODEL), mod_map))


def _route(xp, xs, mod, g2, w_rt, b_rt, layer, sample_seq, n_rows):
    n_p_steps = xp.shape[0] // ROUTE_STEP
    n_s_steps = xs.shape[0] // ROUTE_STEP
    n_steps = n_p_steps + n_s_steps
    t = n_steps * ROUTE_STEP
    tm = ROUTE_TM
    upper = jnp.asarray(np.triu(np.ones((tm, tm), np.float32), 1), BF16)
    step_of = lambda s, i: jnp.where(s == 0, i, n_steps - 1)
    p_spec, s_spec, mod_spec = _merged_specs(n_p_steps, n_s_steps, layer, sample_seq, step_of)
    const = lambda a: pl.BlockSpec(a.shape, lambda s, i: (0,) * a.ndim)
    n_tiles = n_steps * ROUTE_SUBS
    return pl.pallas_call(
        functools.partial(_route_kernel, n_steps=n_steps, n_p_steps=n_p_steps),
        grid=(2, n_steps),
        in_specs=[p_spec, s_spec, mod_spec, const(g2), const(w_rt), const(b_rt), const(upper)],
        out_specs=[
            pl.BlockSpec((ROUTE_STEP, LANES), lambda s, i: (step_of(s, i), 0)),
            pl.BlockSpec(memory_space=pltpu.SMEM),
            pl.BlockSpec(memory_space=pl.ANY),
        ],
        out_shape=[
            jax.ShapeDtypeStruct((t, LANES), F32),
            jax.ShapeDtypeStruct(((n_tiles + 1) * TAB_W,), jnp.int32),
            jax.ShapeDtypeStruct((n_rows, XS_W), F32),
        ],
        scratch_shapes=[
            pltpu.VMEM((t, XS_W), BF16),
            pltpu.VMEM((n_tiles, 8, tm), F32),
            pltpu.VMEM((ROUTE_SUBS, SORT_ROWS, XS_W), F32),
            pltpu.VMEM((FFN_BM, XS_W), F32),
            pltpu.SMEM((n_tiles * N_GROUPS_MOE,), jnp.int32),
            pltpu.SMEM((N_GROUPS_MOE,), jnp.int32),
            pltpu.SemaphoreType.DMA((ROUTE_SUBS,)),
            pltpu.SemaphoreType.DMA(()),
        ],
        compiler_params=_cparams("arbitrary", "arbitrary"),
        name="moe_route",
    )(xp, xs, mod, g2, w_rt, b_rt, upper)


def _ffn_lookup(i, tab_ref, meta):
    fills = [tab_ref[meta + g] for g in range(N_GROUPS_MOE)]
    edges = []
    acc = 0
    for f in fills:
        acc = acc + lax.div(f + (FFN_BM - 1), FFN_BM)
        edges.append(acc)
    total = edges[-1]
    ii = jnp.minimum(i, total - 1)
    grp = sum((ii >= e).astype(jnp.int32) for e in edges[:-1])

    def pick(vals):
        return jnp.where(grp == 0, vals[0], jnp.where(grp == 1, vals[1], jnp.where(grp == 2, vals[2], vals[3])))

    first = pick([0] + edges[:-1])
    return grp, total, ii == first, pick(fills) - (ii - first) * FFN_BM


def _ffn_group_kernel(tab_ref, xs_ref, w1_ref, w3_ref, w2_ref, ys_ref, w1b_ref, w3b_ref, w2b_ref, *, meta):
    i = pl.program_id(0)
    _, total, first_of_group, valid = _ffn_lookup(i, tab_ref, meta)
    active = i < total

    @pl.when(jnp.logical_and(active, first_of_group))
    def _():
        w1b_ref[...] = w1_ref[...].astype(BF16)
        w3b_ref[...] = w3_ref[...].astype(BF16)
        w2b_ref[...] = w2_ref[...].astype(BF16)

    def run(rows):
        hb = xs_ref[0:rows, 0:D_MODEL].astype(BF16)
        ex = xs_ref[0:rows, D_MODEL:XS_W]
        acc = None
        for e in range(EXPERTS_PER_GROUP):
            a = _bdot(hb, w1b_ref[e])
            b = _bdot(hb, w3b_ref[e])
            gate = ex[:, e:e + 1] + ex[:, GATE_LO + e:GATE_LO + e + 1]
            hid = (a * jax.nn.sigmoid(a)) * b * gate
            part = _bdot(hid.astype(BF16), w2b_ref[e])
            acc = part if acc is None else acc + part
        ys_ref[0:rows, :] = acc

    @pl.when(jnp.logical_and(active, valid > FFN_HALF))
    def _():
        run(FFN_BM)

    @pl.when(jnp.logical_and(active, valid <= FFN_HALF))
    def _():
        run(FFN_HALF)
        ys_ref[FFN_HALF:, :] = jnp.zeros((FFN_BM - FFN_HALF, D_MODEL), F32)

    @pl.when(jnp.logical_not(active))
    def _():
        ys_ref[...] = jnp.zeros_like(ys_ref)


def _ffn_group(tab, xs, w1, w3, w2, layer, meta):
    e4 = EXPERTS_PER_GROUP
    n_blocks = xs.shape[0] // FFN_BM
    group_of = lambda i, tab_ref: _ffn_lookup(i, tab_ref, meta)[0]
    grid_spec = pltpu.PrefetchScalarGridSpec(
        num_scalar_prefetch=1,
        grid=(n_blocks,),
        in_specs=[
            pl.BlockSpec((FFN_BM, XS_W), lambda i, tab_ref: (i, 0)),
            pl.BlockSpec((None, e4, D_MODEL, D_EXPERT), lambda i, tab_ref: (layer, group_of(i, tab_ref), 0, 0)),
            pl.BlockSpec((None, e4, D_MODEL, D_EXPERT), lambda i, tab_ref: (layer, group_of(i, tab_ref), 0, 0)),
            pl.BlockSpec((None, e4, D_EXPERT, D_MODEL), lambda i, tab_ref: (layer, group_of(i, tab_ref), 0, 0)),
        ],
        out_specs=pl.BlockSpec((FFN_BM, D_MODEL), lambda i, tab_ref: (i, 0)),
        scratch_shapes=[
            pltpu.VMEM((e4, D_MODEL, D_EXPERT), BF16),
            pltpu.VMEM((e4, D_MODEL, D_EXPERT), BF16),
            pltpu.VMEM((e4, D_EXPERT, D_MODEL), BF16),
        ],
    )
    return pl.pallas_call(
        functools.partial(_ffn_group_kernel, meta=meta),
        grid_spec=grid_spec,
        out_shape=jax.ShapeDtypeStruct((xs.shape[0], D_MODEL), F32),
        compiler_params=_cparams("arbitrary"),
        name="moe_ffn",
    )(tab, xs, w1, w3, w2)


def _fetch_runs(tab_ref, ys_ref, runs_ref, sem_ref, tile, slot):
    _run_copies(tab_ref, tile, ys_ref, runs_ref.at[slot], sem_ref.at[slot], to_hbm=False, wait=False)


def _unsort(tab_ref, ys_ref, runs_ref, sem_ref, dloc_col, tile, slot):
    covered = _tile_rows(tab_ref, tile)
    _wait_rows(covered, ys_ref, runs_ref.at[slot], sem_ref.at[slot], False)
    runs_ref[slot, pl.ds(pl.multiple_of(covered, ROUTE_PAD), LANES), :] = jnp.zeros((LANES, D_MODEL), F32)
    yb = runs_ref[slot, 0:SORT_ROWS, :].astype(BF16)
    row_f = lax.broadcasted_iota(jnp.int32, (ROUTE_TM, SORT_ROWS), 1).astype(F32)
    return _bdot(jnp.where(row_f == dloc_col, 1.0, 0.0).astype(BF16), yb)


def _combine_kernel(tab_ref, *refs, n_steps, first_step, n_p_steps, final_norm):
    with_prompt = first_step < n_p_steps
    if with_prompt:
        xp_ref, xs_ref, mod_ref, dloc_ref, fg_ref, ys_ref, op_ref, os_ref, runs_ref, sem_ref = refs
    else:
        xs_ref, mod_ref, dloc_ref, fg_ref, ys_ref, os_ref, runs_ref, sem_ref = refs
    i = pl.program_id(0)
    par = lax.rem(i, 2)
    step = first_step + i
    tm = ROUTE_TM

    def fetch(st, parity):
        for sub in range(ROUTE_SUBS):
            _fetch_runs(tab_ref, ys_ref, runs_ref, sem_ref, st * ROUTE_SUBS + sub, parity * ROUTE_SUBS + sub)

    @pl.when(i == 0)
    def _():
        fetch(first_step, 0)

    @pl.when(i + 1 < n_steps)
    def _():
        fetch(step + 1, 1 - par)

    parts = [_unsort(tab_ref, ys_ref, runs_ref, sem_ref, dloc_ref[sub * tm:(sub + 1) * tm, 0:1],
                     step * ROUTE_SUBS + sub, par * ROUTE_SUBS + sub) for sub in range(ROUTE_SUBS)]
    is_prompt = step < n_p_steps
    x = jnp.where(is_prompt, xp_ref[...], xs_ref[...]) if with_prompt else xs_ref[...]
    x2 = x + mod_ref[5:6, :] * jnp.concatenate(parts, axis=0)
    if final_norm:
        x2 = _rms(x2, fg_ref[...])
    if with_prompt:
        @pl.when(is_prompt)
        def _():
            op_ref[...] = x2

        @pl.when(jnp.logical_not(is_prompt))
        def _():
            os_ref[...] = x2
    else:
        os_ref[...] = x2


def _combine(tab, xp, xs, mod, dloc, final_g, ys, layer, sample_seq, final_norm, with_prompt):
    n_p_steps = xp.shape[0] // ROUTE_STEP
    n_s_steps = xs.shape[0] // ROUTE_STEP
    first_step = 0 if with_prompt else n_p_steps
    n_steps = n_p_steps + n_s_steps - first_step
    step_of = lambda i, tab_ref: i + first_step
    p_spec, s_spec, mod_spec = _merged_specs(n_p_steps, n_s_steps, layer, sample_seq, step_of)
    streams = [p_spec, s_spec] if with_prompt else [s_spec]
    grid_spec = pltpu.PrefetchScalarGridSpec(
        num_scalar_prefetch=1,
        grid=(n_steps,),
        in_specs=streams + [
            mod_spec,
            pl.BlockSpec((ROUTE_STEP, LANES), lambda i, tab_ref: (i + first_step, 0)),
            pl.BlockSpec((1, D_MODEL), lambda i, tab_ref: (0, 0)),
            pl.BlockSpec(memory_space=pl.ANY),
        ],
        out_specs=streams,
        scratch_shapes=[
            pltpu.VMEM((2 * ROUTE_SUBS, RUN_ROWS, D_MODEL), F32),
            pltpu.SemaphoreType.DMA((2 * ROUTE_SUBS,)),
        ],
    )
    arrays = [xp, xs] if with_prompt else [xs]
    return pl.pallas_call(
        functools.partial(_combine_kernel, n_steps=n_steps, first_step=first_step, n_p_steps=n_p_steps,
                          final_norm=final_norm),
        grid_spec=grid_spec,
        out_shape=[jax.ShapeDtypeStruct(a.shape, F32) for a in arrays],
        compiler_params=_cparams("arbitrary"),
        name="moe_combine",
    )(tab, *arrays, mod, dloc, final_g, ys)


def _moe(xp, xs, mod, g2, w_r, b_r, w1, w3, w2, final_g, layer, sample_seq, final_norm, defer_prompt):
    t = xp.shape[0] + xs.shape[0]
    n_tiles = t // ROUTE_TM
    max_rows = t + N_GROUPS_MOE * (ROUTE_PAD - 1) * n_tiles
    n_rows = (-(-max_rows // FFN_BM) + N_GROUPS_MOE) * FFN_BM
    dloc, tab, sorted_x = _route(xp, xs, mod, g2, w_r, b_r, layer, sample_seq, n_rows)
    ys = _ffn_group(tab, sorted_x, w1, w3, w2, layer, n_tiles * TAB_W)
    out = _combine(tab, xp, xs, mod, dloc, final_g, ys, layer, sample_seq, final_norm, not defer_prompt)
    if defer_prompt:
        return None, out[0], (tab, dloc, ys)
    return out[0], out[1], None


def _rope_tables(seq):
    half = QK_ROPE // 2
    nf = half // 2
    inv = ROPE_BASE ** (-np.arange(nf, dtype=np.float64) / nf)
    pos = np.arange(seq)
    row = (pos // GRID_W).astype(np.float64)
    col = (pos % GRID_W).astype(np.float64)
    cos = np.ones((seq, HEAD_PAD), np.float64)
    sin_a = np.zeros((seq, HEAD_PAD), np.float64)
    sin_b = np.zeros((seq, HEAD_PAD), np.float64)
    for part, p in enumerate((row, col)):
        ang = p[:, None] * inv[None, :]
        base = ROPE_OFF + part * half
        cos[:, base:base + nf] = np.cos(ang)
        cos[:, base + nf:base + half] = np.cos(ang)
        sin_a[:, base:base + nf] = -np.sin(ang)
        sin_b[:, base + nf:base + half] = np.sin(ang)
    return tuple(jnp.asarray(a, F32) for a in (cos, sin_a, sin_b))


def _apply_rope(x, cos, sin_a, sin_b, reps):
    nf = QK_ROPE // 4
    width = x.shape[1]
    if reps > 1:
        cos, sin_a, sin_b = (jnp.concatenate([a] * reps, axis=1) for a in (cos, sin_a, sin_b))
    return x * cos + pltpu.roll(x, width - nf, axis=1) * sin_a + pltpu.roll(x, nf, axis=1) * sin_b


def _den_ones():
    lane = lax.broadcasted_iota(jnp.int32, (1, VX_W), 1)
    return jnp.where((lane & (PAIR_W - 1)) == DEN_COL, 1.0, 0.0)


def _odd_in_kernel(*refs, rope, emit_cache):
    x_ref, mod_ref, g_ref, w_ref, qg_ref, wq_ref, kg_ref, wk_ref, wv_ref, cs_ref = refs[:10]
    refs = refs[10:]
    if rope:
        cos_ref, sa_ref, sb_ref = refs[:3]
        refs = refs[3:]
    y_ref, q_ref, k_ref, v_ref = refs[:4]
    refs = refs[4:]
    m = mod_ref[...]
    h = _rms(x_ref[...], g_ref[...]) * (1.0 + m[1:2]) + m[0:1]
    z = _bdot(h.astype(BF16), w_ref[...])
    zc = z[:, 0:D_C]
    qc = z[:, D_C:D_C + Q_LORA]
    kvc = z[:, D_C + Q_LORA:D_C + Q_LORA + KV_LORA]
    kpe = z[:, D_C + Q_LORA + KV_LORA:]
    q = _bdot(_rms(qc, qg_ref[...]).astype(BF16), wq_ref[...])
    kvn = _rms(kvc, kg_ref[...])
    if emit_cache:
        ckv_ref, kpe_ref = refs
        ckv_ref[...] = kvn
        kpe_ref[...] = kpe[:, ROPE_OFF:ROPE_OFF + QK_ROPE]
    if rope:
        tabs = (cos_ref[...], sa_ref[...], sb_ref[...])
        q = _apply_rope(q, *tabs, reps=N_HEADS)
        kpe = _apply_rope(kpe, *tabs, reps=1)
    kvb = kvn.astype(BF16)
    k = _bdot(kvb, wk_ref[...]) + jnp.concatenate([kpe] * N_HEADS, axis=1)
    scale = math.log2(math.e) / math.sqrt(QK_NOPE + QK_ROPE)
    q_ref[...] = (q * scale).astype(BF16)
    k_ref[...] = k.astype(BF16)
    v_ref[...] = (_bdot(kvb, wv_ref[...]) + _den_ones()).astype(BF16)
    y = _bdot(zc.astype(BF16), cs_ref[...])
    y_ref[0, :, :] = y[:, 0:D_C].astype(BF16)
    y_ref[1, :, :] = y[:, D_C:2 * D_C].astype(BF16)


def _odd_in(x3, mod, g, w_in, q_g, w_q, kv_g, w_k, w_v, cs, stream, layer, rope_tabs, emit_cache):
    b, s, _ = x3.shape
    tm = min(s, 512)
    n_i = s // tm
    rope = rope_tabs is not None
    const = lambda a: pl.BlockSpec(a.shape, lambda bi, i: (0,) * a.ndim)
    in_specs = [
        pl.BlockSpec((None, tm, D_MODEL), lambda bi, i: (bi, i, 0)),
        _mod_spec(stream, layer),
        const(g), const(w_in), const(q_g), const(w_q), const(kv_g), const(w_k), const(w_v), const(cs),
    ]
    args = [x3, mod, g, w_in, q_g, w_q, kv_g, w_k, w_v, cs]
    if rope:
        in_specs += [pl.BlockSpec((tm, HEAD_PAD), lambda bi, i: (i, 0))] * 3
        args += list(rope_tabs)
    hq = N_HEADS * HEAD_PAD
    out_specs = [
        pl.BlockSpec((None, 2, tm, D_C), lambda bi, i: (bi, 0, i, 0)),
        pl.BlockSpec((None, tm, hq), lambda bi, i: (bi, i, 0)),
        pl.BlockSpec((None, tm, hq), lambda bi, i: (bi, i, 0)),
        pl.BlockSpec((None, tm, VX_W), lambda bi, i: (bi, i, 0)),
    ]
    out_shape = [
        jax.ShapeDtypeStruct((b, 2, s, D_C), BF16),
        jax.ShapeDtypeStruct((b, s, hq), BF16),
        jax.ShapeDtypeStruct((b, s, hq), BF16),
        jax.ShapeDtypeStruct((b, s, VX_W), BF16),
    ]
    if emit_cache:
        out_specs += [
            pl.BlockSpec((None, tm, KV_LORA), lambda bi, i: (bi, i, 0)),
            pl.BlockSpec((None, tm, QK_ROPE), lambda bi, i: (bi, i, 0)),
        ]
        out_shape += [
            jax.ShapeDtypeStruct((b, s, KV_LORA), F32),
            jax.ShapeDtypeStruct((b, s, QK_ROPE), F32),
        ]
    return pl.pallas_call(
        functools.partial(_odd_in_kernel, rope=rope, emit_cache=emit_cache),
        grid=(b, n_i),
        in_specs=in_specs,
        out_specs=out_specs,
        out_shape=out_shape,
        compiler_params=_cparams("parallel", "parallel"),
        name="odd_in",
    )(*args)


def _cache_kv_kernel(c_ref, p_ref, wk_ref, wv_ref, k_ref, v_ref):
    cb = c_ref[...].astype(BF16)
    k = _bdot(cb, wk_ref[...]) + jnp.concatenate([p_ref[...]] * N_HEADS, axis=1)
    k_ref[...] = k.astype(BF16)
    v_ref[...] = (_bdot(cb, wv_ref[...]) + _den_ones()).astype(BF16)


def _cache_kv(ckv, kpe_blk, w_k, w_v):
    b, p, _ = ckv.shape
    hq = N_HEADS * HEAD_PAD
    return pl.pallas_call(
        _cache_kv_kernel,
        grid=(b,),
        in_specs=[
            pl.BlockSpec((None, p, KV_LORA), lambda bi: (bi, 0, 0)),
            pl.BlockSpec((None, p, HEAD_PAD), lambda bi: (bi, 0, 0)),
            pl.BlockSpec(w_k.shape, lambda bi: (0, 0)),
            pl.BlockSpec(w_v.shape, lambda bi: (0, 0)),
        ],
        out_specs=[
            pl.BlockSpec((None, p, hq), lambda bi: (bi, 0, 0)),
            pl.BlockSpec((None, p, VX_W), lambda bi: (bi, 0, 0)),
        ],
        out_shape=[
            jax.ShapeDtypeStruct((b, p, hq), BF16),
            jax.ShapeDtypeStruct((b, p, VX_W), BF16),
        ],
        compiler_params=_cparams("parallel"),
        name="cache_kv",
    )(ckv, kpe_blk, w_k, w_v)


def _odd_mix_kernel(*refs, with_cache):
    q_ref, k_ref, v_ref = refs[:3]
    refs = refs[3:]
    if with_cache:
        kc_ref, vc_ref = refs[:2]
        refs = refs[2:]
    y_ref, f_ref, x_ref, mod_ref, wo_ref, o_ref, a_ref = refs
    tq = q_ref.shape[0]
    lane = lax.broadcasted_iota(jnp.int32, (tq, 2 * V_DIM), 1)
    for pair in range(N_HEADS // 2):
        vcols = slice(pair * PAIR_W, (pair + 1) * PAIR_W)
        outs = []
        for h in (2 * pair, 2 * pair + 1):
            hcols = slice(h * HEAD_PAD, (h + 1) * HEAD_PAD)
            qh = q_ref[:, hcols]
            s = lax.dot_general(qh, k_ref[:, hcols], NT_DIMS, preferred_element_type=F32)
            top = jnp.max(s, axis=-1, keepdims=True)
            if with_cache:
                sc = lax.dot_general(qh, kc_ref[:, hcols], NT_DIMS, preferred_element_type=F32)
                top = jnp.maximum(top, jnp.max(sc, axis=-1, keepdims=True))
            acc = _bdot(jnp.exp2((s - top).astype(BF16)), v_ref[:, vcols])
            if with_cache:
                acc = acc + _bdot(jnp.exp2((sc - top).astype(BF16)), vc_ref[:, vcols])
            outs.append(acc[:, 0:2 * V_DIM] / acc[:, DEN_COL:DEN_COL + 1])
        a_ref[:, pair * 2 * V_DIM:(pair + 1) * 2 * V_DIM] = jnp.where(lane < V_DIM, outs[0], outs[1]).astype(BF16)
    f = _bdot(f_ref[...], y_ref[...].reshape(-1, D_C))
    o = _bdot(f.astype(BF16), wo_ref[0:D_C, :]) + _bdot(a_ref[...], wo_ref[D_C:, :])
    o_ref[...] = x_ref[...] + mod_ref[2:3, :] * o


def _odd_whole_kernel(tab_ref, x_ref, mod_ref, g_ref, w_ref, qg_ref, wq_ref, kg_ref, wk_ref, wv_ref, cs_ref, f_ref,
                      wo_ref, modp_ref, dloc_ref, ys_ref, o_ref, ckv_ref, kpe_ref,
                      y_ref, q_ref, k_ref, v_ref, a_ref, x2_ref, runs_ref, sem_ref):
    bi = pl.program_id(0)
    par = lax.rem(bi, 2)

    @pl.when(bi == 0)
    def _():
        _fetch_runs(tab_ref, ys_ref, runs_ref, sem_ref, 0, 0)

    @pl.when(bi + 1 < pl.num_programs(0))
    def _():
        _fetch_runs(tab_ref, ys_ref, runs_ref, sem_ref, bi + 1, 1 - par)

    moe = _unsort(tab_ref, ys_ref, runs_ref, sem_ref, dloc_ref[:, 0:1], bi, par)
    x2_ref[...] = x_ref[...] + modp_ref[5:6, :] * moe
    _odd_in_kernel(x2_ref, mod_ref, g_ref, w_ref, qg_ref, wq_ref, kg_ref, wk_ref, wv_ref, cs_ref,
                   y_ref, q_ref, k_ref, v_ref, ckv_ref, kpe_ref, rope=False, emit_cache=True)
    _odd_mix_kernel(q_ref, k_ref, v_ref, y_ref, f_ref, x2_ref, mod_ref, wo_ref, o_ref, a_ref, with_cache=False)


def _odd_whole(x3, pending, mod, g, w_in, q_g, w_q, kv_g, w_k, w_v, cs, fmat, w_out, stream, layer):
    tab, dloc, ys = pending
    b, s, _ = x3.shape
    assert s == ROUTE_TM, "one sequence must be one MoE sort tile"
    hq = N_HEADS * HEAD_PAD
    hv = N_HEADS * V_DIM
    const = lambda a: pl.BlockSpec(a.shape, lambda bi, tab_ref: (0,) * a.ndim)
    row_block = lambda w: pl.BlockSpec((None, s, w), lambda bi, tab_ref: (bi, 0, 0))
    mod_block = lambda lyr: pl.BlockSpec((None, None, 6, D_MODEL),
                                         lambda bi, tab_ref: (lyr, stream.row_of_batch(bi), 0, 0))
    grid_spec = pltpu.PrefetchScalarGridSpec(
        num_scalar_prefetch=1,
        grid=(b,),
        in_specs=[
            row_block(D_MODEL), mod_block(layer),
            const(g), const(w_in), const(q_g), const(w_q), const(kv_g), const(w_k), const(w_v), const(cs),
            const(fmat), const(w_out),
            mod_block(layer - 1),
            pl.BlockSpec((s, LANES), lambda bi, tab_ref: (bi, 0)),
            pl.BlockSpec(memory_space=pl.ANY),
        ],
        out_specs=[row_block(D_MODEL), row_block(KV_LORA), row_block(QK_ROPE)],
        scratch_shapes=[
            pltpu.VMEM((2, s, D_C), BF16),
            pltpu.VMEM((s, hq), BF16),
            pltpu.VMEM((s, hq), BF16),
            pltpu.VMEM((s, VX_W), BF16),
            pltpu.VMEM((s, hv), BF16),
            pltpu.VMEM((s, D_MODEL), F32),
            pltpu.VMEM((2, RUN_ROWS, D_MODEL), F32),
            pltpu.SemaphoreType.DMA((2,)),
        ],
    )
    return pl.pallas_call(
        _odd_whole_kernel,
        grid_spec=grid_spec,
        out_shape=[
            jax.ShapeDtypeStruct((b, s, D_MODEL), F32),
            jax.ShapeDtypeStruct((b, s, KV_LORA), F32),
            jax.ShapeDtypeStruct((b, s, QK_ROPE), F32),
        ],
        compiler_params=_cparams("arbitrary"),
        name="odd_whole",
    )(tab, x3, mod, g, w_in, q_g, w_q, kv_g, w_k, w_v, cs, fmat, w_out, mod, dloc, ys)


def _odd_mix(q, k, v, kc, vc, y, fmat, x3, mod, w_out, stream, layer):
    b, s, hq = q.shape
    tq = min(s, 256)
    n_i = s // tq
    with_cache = kc is not None
    hv = N_HEADS * V_DIM
    mode = dict(pipeline_mode=pl.Buffered(1)) if with_cache else {}

    def per_batch(rows, cols):
        return pl.BlockSpec((None, rows, cols), lambda bi, i: (bi, 0, 0), **mode)

    in_specs = [pl.BlockSpec((None, tq, hq), lambda bi, i: (bi, i, 0)), per_batch(s, hq), per_batch(s, VX_W)]
    args = [q, k, v]
    if with_cache:
        p = kc.shape[1]
        in_specs += [per_batch(p, hq), per_batch(p, VX_W)]
        args += [kc, vc]
    in_specs += [
        per_batch(2 * s, D_C),
        pl.BlockSpec((tq, 2 * s), lambda bi, i: (i, 0)),
        pl.BlockSpec((None, tq, D_MODEL), lambda bi, i: (bi, i, 0)),
        _mod_spec(stream, layer),
        pl.BlockSpec(w_out.shape, lambda bi, i: (0, 0), **mode),
    ]
    args += [y, fmat, x3, mod, w_out]
    return pl.pallas_call(
        functools.partial(_odd_mix_kernel, with_cache=with_cache),
        grid=(b, n_i),
        in_specs=in_specs,
        out_specs=pl.BlockSpec((None, tq, D_MODEL), lambda bi, i: (bi, i, 0)),
        out_shape=jax.ShapeDtypeStruct((b, s, D_MODEL), F32),
        scratch_shapes=[pltpu.VMEM((tq, hv), BF16)],
        compiler_params=_cparams("parallel", "arbitrary"),
        name="odd_mix",
    )(*args)


def _dft_tables(seq):
    jc = np.arange(C_GW)
    ang_c = 2.0 * np.pi * np.outer(jc, jc) / C_GW
    eye = np.eye(C_GROUPS)
    cs = np.concatenate([np.kron(eye, np.cos(ang_c)), np.kron(eye, np.sin(ang_c))], axis=1)
    jn = np.arange(seq)
    ang_n = 2.0 * np.pi * (np.outer(jn, jn) % seq) / seq
    scale = 1.0 / math.sqrt(seq * C_GW)
    fmat = np.concatenate([np.cos(ang_n), -np.sin(ang_n)], axis=1) * scale
    return jnp.asarray(cs, F32).astype(BF16), jnp.asarray(fmat, F32).astype(BF16)


def _odd_weights(w_in, w_uq, w_ukv):
    d = w_in.shape[0]
    base = D_C + Q_LORA + KV_LORA
    kpe_blk = jnp.zeros((d, HEAD_PAD), w_in.dtype).at[:, ROPE_OFF:ROPE_OFF + QK_ROPE].set(w_in[:, base:])
    w_in_p = jnp.concatenate([w_in[:, :base], kpe_blk], axis=1).astype(BF16)
    qh = w_uq.reshape(Q_LORA, N_HEADS, QK_NOPE + QK_ROPE)
    w_q = jnp.pad(qh, ((0, 0), (0, 0), (0, HEAD_PAD - QK_NOPE - QK_ROPE))).reshape(Q_LORA, -1).astype(BF16)
    kvh = w_ukv.reshape(KV_LORA, N_HEADS, QK_NOPE + V_DIM)
    w_k = jnp.pad(kvh[:, :, :QK_NOPE], ((0, 0), (0, 0), (0, HEAD_PAD - QK_NOPE))).reshape(KV_LORA, -1)
    w_v = kvh[:, :, QK_NOPE:].reshape(KV_LORA, N_HEADS // 2, 2 * V_DIM)
    w_v = jnp.pad(w_v, ((0, 0), (0, 0), (0, PAIR_W - 2 * V_DIM))).reshape(KV_LORA, VX_W)
    return w_in_p, w_q, w_k.astype(BF16), w_v.astype(BF16)


ROUTER_ROWS = 32


def _router_weights(wg, bg, we, be):
    d = wg.shape[0]
    w = jnp.concatenate([wg, we.reshape(d, N_EXPERTS)], axis=1).T
    w = jnp.pad(w, ((0, ROUTER_ROWS - w.shape[0]), (0, 0))).astype(BF16)
    b = jnp.concatenate([bg, be.reshape(N_EXPERTS)])
    b = jnp.pad(b, (0, ROUTER_ROWS - b.shape[0])).reshape(ROUTER_ROWS, 1).astype(F32)
    return w, b


def kernel(x_prompt, x_sample, cache_ckv, cache_kpe, c, c_ctx, mod_w, mod_b, norm1_g, norm2_g,
           ev_w_in, ev_conv_w, ev_sgu_norm_g, ev_sgu_w, ev_sgu_b, ev_w_out,
           od_w_in, od_q_norm_g, od_w_uq, od_kv_norm_g, od_w_ukv, od_w_out,
           moe_wg, moe_bg, moe_we, moe_be, moe_w1, moe_w3, moe_w2, final_norm_g):
    bp, n_p, d = x_prompt.shape
    bs, n_s, _ = x_sample.shape
    streams = [(_Stream(bp, n_p, True), x_prompt), (_Stream(bs, n_s, False), x_sample)]

    n_rows = 1 + bs
    cond_t = jnp.concatenate([c_ctx[None, :], c], axis=0).T
    mod = _adaln(cond_t, mod_w, mod_b, n_rows)

    final_g = final_norm_g.reshape(1, d)
    xs = [x for _, x in streams]
    new_ckv, new_kpe = [], []
    pending = None
    for l in range(DEPTH):
        j = l // 2
        g1 = norm1_g[l].reshape(1, d)
        g2 = norm2_g[l].reshape(1, d)
        w_r, b_r = _router_weights(moe_wg[l], moe_bg[l], moe_we[l], moe_be[l])
        last = l == DEPTH - 1
        if l % 2 == 0:
            w_in = ev_w_in[j].astype(BF16)
            w_out = ev_w_out[j].astype(BF16)
            sgu_w = ev_sgu_w[j].astype(BF16)
            sgu_g = ev_sgu_norm_g[j].reshape(1, D_B)
            sgu_bias = jnp.repeat(ev_sgu_b[j].T, D_B // B_GROUPS, axis=1)
            for si, (st, _) in enumerate(streams):
                xs[si] = _even_layer(xs[si], mod, g1, w_in, ev_conv_w[j], sgu_g, sgu_w, sgu_bias, w_out, st, l)
        else:
            w_in, w_q, w_k, w_v = _odd_weights(od_w_in[j], od_w_uq[j], od_w_ukv[j])
            w_out = od_w_out[j].astype(BF16)
            q_g = od_q_norm_g[j].reshape(1, Q_LORA)
            kv_g = od_kv_norm_g[j].reshape(1, KV_LORA)
            for si, (st, _) in enumerate(streams):
                x3 = xs[si]
                cs, fmat = _dft_tables(st.seq)
                if st.shared_cond:
                    xs[si], ckv, kpe = _odd_whole(x3, pending, mod, g1, w_in, q_g, w_q, kv_g, w_k, w_v, cs, fmat,
                                                  w_out, st, l)
                    new_ckv.append(ckv)
                    new_kpe.append(kpe)
                    continue
                y, q, k, v = _odd_in(x3, mod, g1, w_in, q_g, w_q, kv_g, w_k, w_v, cs, st, l,
                                     _rope_tables(st.seq), False)
                kpe_blk = jnp.pad(cache_kpe[:, j], ((0, 0), (0, 0), (ROPE_OFF, HEAD_PAD - ROPE_OFF - QK_ROPE)))
                kc, vc = _cache_kv(cache_ckv[:, j], kpe_blk, w_k, w_v)
                xs[si] = _odd_mix(q, k, v, kc, vc, y.reshape(st.batch, 2 * st.seq, D_C), fmat, x3, mod,
                                  w_out, st, l)
        defer = not last and (l + 1) % 2 == 1
        x2p, x2s, pending = _moe(xs[0].reshape(bp * n_p, d), xs[1].reshape(bs * n_s, d), mod, g2, w_r, b_r,
                                 moe_w1, moe_w3, moe_w2, final_g, l, n_s, last, defer)
        xs = [xs[0] if defer else x2p.reshape(bp, n_p, d), x2s.reshape(bs, n_s, d)]
    return (xs[0], xs[1], jnp.stack(new_ckv, axis=1), jnp.stack(new_kpe, axis=1))
```

```python
import functools
import math

import numpy as np
import jax
import jax.numpy as jnp
from jax import lax
from jax.experimental import pallas as pl
from jax.experimental.pallas import tpu as pltpu

D_MODEL = 1024
DEPTH = 2
GRID_W = 64
D_A = D_MODEL // 2
D_B = D_MODEL // 2
B_GROUPS = 4
CHUNK = 128
D_EVEN_IN = 3 * D_A + 2 * D_B
D_C = D_MODEL // 4
C_GROUPS = 4
C_GW = D_C // C_GROUPS
N_HEADS = 12
QK_NOPE = 64
QK_ROPE = 32
V_DIM = 64
Q_LORA = 384
KV_LORA = 256
ROPE_BASE = 10000.0
N_GROUPS_MOE = 4
EXPERTS_PER_GROUP = 4
N_EXPERTS = N_GROUPS_MOE * EXPERTS_PER_GROUP
D_EXPERT = 256
EPS = 1e-6

LANES = 128
HEAD_PAD = 128
PAIR_W = 256
DEN_COL = 2 * 64
VX_W = 6 * PAIR_W
ROPE_OFF = QK_NOPE
GATE_OFF = N_GROUPS_MOE
NEG_BIG = -1e30
F32 = jnp.float32
BF16 = jnp.bfloat16
VMEM_LIMIT = 56 * 1024 * 1024


def _cparams(*sem):
    return pltpu.CompilerParams(dimension_semantics=sem, vmem_limit_bytes=VMEM_LIMIT)


def _rms(x, g):
    return x * lax.rsqrt(jnp.mean(x * x, axis=-1, keepdims=True) + EPS) * g


def _bdot(a, b):
    return jnp.dot(a, b, preferred_element_type=F32)


NT_DIMS = (((1,), (1,)), ((), ()))
TN_DIMS = (((0,), (0,)), ((), ()))


def _mod_kernel(ct_ref, w_ref, b_ref, o_ref):
    c = ct_ref[...]
    s = c * jax.nn.sigmoid(c)
    w = w_ref[...]
    b = b_ref[...]
    for r in range(o_ref.shape[0]):
        o_ref[r:r + 1, :] = jnp.sum(s[:, r:r + 1] * w, axis=0, keepdims=True) + b


def _adaln(cond_t, mod_w, mod_b, n_rows):
    nt = 1024
    d6 = mod_w.shape[-1]
    out = pl.pallas_call(
        _mod_kernel,
        grid=(DEPTH, d6 // nt),
        in_specs=[
            pl.BlockSpec(cond_t.shape, lambda l, n: (0, 0)),
            pl.BlockSpec((None, D_MODEL, nt), lambda l, n: (l, 0, n)),
            pl.BlockSpec((None, 1, nt), lambda l, n: (l, 0, n)),
        ],
        out_specs=pl.BlockSpec((None, n_rows, nt), lambda l, n: (l, 0, n)),
        out_shape=jax.ShapeDtypeStruct((DEPTH, n_rows, d6), F32),
        compiler_params=_cparams("parallel", "parallel"),
        name="adaln",
    )(cond_t, mod_w, mod_b.reshape(DEPTH, 1, d6))
    return out.reshape(DEPTH, n_rows, 6, D_MODEL)


class _Stream:
    def __init__(self, batch, seq, shared_cond):
        self.batch = batch
        self.seq = seq
        self.tokens = batch * seq
        self.shared_cond = shared_cond

    def row_of_batch(self, b):
        return 0 if self.shared_cond else b + 1

    def row_of_tile(self, i, tm):
        return 0 if self.shared_cond else (i * tm) // self.seq + 1


def _mod_spec(stream, layer, tm=None):
    if tm is None:
        return pl.BlockSpec((None, None, 6, D_MODEL), lambda b, i: (layer, stream.row_of_batch(b), 0, 0))
    return pl.BlockSpec((None, None, 6, D_MODEL), lambda i, *_: (layer, stream.row_of_tile(i, tm), 0, 0))


HALO = 8


def _even_kernel(*refs, has_halo, nc):
    if has_halo:
        x_ref, xp_ref, xn_ref = refs[:3]
        refs = refs[3:]
    else:
        x_ref = refs[0]
        refs = refs[1:]
    mod_ref, g_ref, wi_ref, cw_ref, sg_ref, sw_ref, sb_ref, wo_ref, o_ref, z_ref, y_ref = refs
    i = pl.program_id(1)
    n_i = pl.num_programs(1)
    ts = x_ref.shape[0]
    m = mod_ref[...]
    g = g_ref[...]

    def modulate(x):
        return (_rms(x, g) * (1.0 + m[1:2]) + m[0:1]).astype(BF16)

    x = x_ref[...]
    hb = modulate(x)
    for n in range(D_EVEN_IN // nc):
        z_ref[:, n * nc:(n + 1) * nc] = _bdot(hb, wi_ref[:, n * nc:(n + 1) * nc]).astype(BF16)

    gate_b = z_ref[:, 0:D_A].astype(F32)
    gate_c = z_ref[:, D_A:2 * D_A].astype(F32)
    xa = z_ref[:, 2 * D_A:3 * D_A].astype(F32)
    t = gate_c * xa
    t_prev = pltpu.roll(t, 1, axis=0)
    t_next = pltpu.roll(t, ts - 1, axis=0)
    row = lax.broadcasted_iota(jnp.int32, (ts, 1), 0)
    if has_halo:
        hh = modulate(jnp.concatenate([xp_ref[...], xn_ref[...]], axis=0))
        zh = _bdot(hh, wi_ref[:, D_A:3 * D_A]).astype(BF16).astype(F32)
        th = zh[:, 0:D_A] * zh[:, D_A:2 * D_A]
        tp = th[HALO - 1:HALO] * (i > 0).astype(F32)
        tn = th[HALO:HALO + 1] * (i < n_i - 1).astype(F32)
    else:
        tp = tn = 0.0
    t_prev = jnp.where(row == 0, tp, t_prev)
    t_next = jnp.where(row == ts - 1, tn, t_next)
    cw = cw_ref[...]
    y_a = gate_b * (t_prev * cw[0:1] + t * cw[1:2] + t_next * cw[2:3])
    y_ref[:, 0:D_A] = y_a.astype(BF16)

    u = z_ref[:, 3 * D_A:3 * D_A + D_B].astype(F32)
    v = z_ref[:, 3 * D_A + D_B:3 * D_A + 2 * D_B].astype(F32)
    vb = _rms(v, sg_ref[...]).astype(BF16)
    gw = D_B // B_GROUPS
    for c in range(ts // CHUNK):
        rows = slice(c * CHUNK, (c + 1) * CHUNK)
        for gi in range(B_GROUPS):
            cols = slice(gi * gw, (gi + 1) * gw)
            sv = _bdot(sw_ref[gi], vb[rows, cols]) + sb_ref[:, cols]
            y_ref[rows, D_A + gi * gw:D_A + (gi + 1) * gw] = (u[rows, cols] * sv).astype(BF16)

    o_ref[...] = x + m[2:3] * _bdot(y_ref[...], wo_ref[...])


def _even_layer(x3, mod, g, w_in, conv_w, sgu_g, sgu_w, sgu_bias, w_out, stream, layer):
    b, s, _ = x3.shape
    ts = min(s, 256)
    n_i = s // ts
    has_halo = n_i > 1
    hb = ts // HALO
    last_h = s // HALO - 1
    const = lambda a: pl.BlockSpec(a.shape, lambda bi, i: (0,) * a.ndim)
    in_specs = [pl.BlockSpec((None, ts, D_MODEL), lambda bi, i: (bi, i, 0))]
    args = [x3]
    if has_halo:
        in_specs += [
            pl.BlockSpec((None, HALO, D_MODEL), lambda bi, i: (bi, jnp.maximum(i * hb - 1, 0), 0)),
            pl.BlockSpec((None, HALO, D_MODEL), lambda bi, i: (bi, jnp.minimum((i + 1) * hb, last_h), 0)),
        ]
        args += [x3, x3]
    in_specs += [_mod_spec(stream, layer), const(g), const(w_in), const(conv_w), const(sgu_g), const(sgu_w),
                 const(sgu_bias), const(w_out)]
    args += [mod, g, w_in, conv_w, sgu_g, sgu_w, sgu_bias, w_out]
    return pl.pallas_call(
        functools.partial(_even_kernel, has_halo=has_halo, nc=512),
        grid=(b, n_i),
        in_specs=in_specs,
        out_specs=pl.BlockSpec((None, ts, D_MODEL), lambda bi, i: (bi, i, 0)),
        out_shape=jax.ShapeDtypeStruct((b, s, D_MODEL), F32),
        scratch_shapes=[pltpu.VMEM((ts, D_EVEN_IN), BF16), pltpu.VMEM((ts, D_A + D_B), BF16)],
        compiler_params=_cparams("parallel", "parallel"),
        name="even_layer",
    )(*args)


ROUTE_TM = 256
ROUTE_SUBS = 2
ROUTE_STEP = ROUTE_TM * ROUTE_SUBS
ROUTE_PAD = 8
SORT_ROWS = ROUTE_TM + LANES
RUN_ROWS = SORT_ROWS + 32
XS_W = D_MODEL + LANES
GATE_LO = EXPERTS_PER_GROUP
DLOC_HI = 2 * EXPERTS_PER_GROUP
DLOC_RADIX = 16.0
FFN_BM = 512
FFN_HALF = FFN_BM // 2
RUN_SIZES = (256, 128, 64, 32, 16, 8)
TAB_W = 2 * N_GROUPS_MOE


def _round_up(x, m):
    return lax.div(x + (m - 1), m) * m


def _run_copies(tab_ref, tile, hbm_ref, vmem_ref, sem, to_hbm, wait):
    off = 0
    for g in range(N_GROUPS_MOE):
        start = tab_ref[tile * TAB_W + g]
        n = tab_ref[tile * TAB_W + N_GROUPS_MOE + g]
        for p in RUN_SIZES:
            done = n & (-2 * p)

            @pl.when((n & p) != 0)
            def _():
                v = vmem_ref.at[pl.ds(pl.multiple_of(off + done, ROUTE_PAD), p)]
                h = hbm_ref.at[pl.ds(pl.multiple_of(start + done, ROUTE_PAD), p)]
                cp = pltpu.make_async_copy(v, h, sem) if to_hbm else pltpu.make_async_copy(h, v, sem)
                if wait:
                    cp.wait()
                else:
                    cp.start()
        off = off + n


def _zero_fill(tab_ref, meta, zeros_ref, hbm_ref, sem, n_rows, wait):
    def copy(rows, dst_row):
        cp = pltpu.make_async_copy(zeros_ref.at[pl.ds(0, rows)],
                                   hbm_ref.at[pl.ds(pl.multiple_of(dst_row, ROUTE_PAD), rows)], sem)
        if wait:
            cp.wait()
        else:
            cp.start()

    end = 0
    for g in range(N_GROUPS_MOE):
        fill = tab_ref[meta + g]
        start = tab_ref[meta + N_GROUPS_MOE + g]
        end = start + _round_up(fill, FFN_BM)
        tail = end - start - fill
        for p in RUN_SIZES:
            pl.when((tail & p) != 0)(functools.partial(copy, p, start + fill + (tail & (-2 * p))))
    for k in range(n_rows // FFN_BM):
        pl.when(end + k * FFN_BM < n_rows)(functools.partial(copy, FFN_BM, end + k * FFN_BM))


def _wait_rows(n, hbm_ref, vmem_ref, sem, to_hbm):
    for p in RUN_SIZES:
        @pl.when((n & p) != 0)
        def _():
            v = vmem_ref.at[pl.ds(0, p)]
            h = hbm_ref.at[pl.ds(0, p)]
            (pltpu.make_async_copy(v, h, sem) if to_hbm else pltpu.make_async_copy(h, v, sem)).wait()


def _tile_rows(tab_ref, tile):
    n = 0
    for g in range(N_GROUPS_MOE):
        n = n + tab_ref[tile * TAB_W + N_GROUPS_MOE + g]
    return n


def _max4(v):
    return jnp.maximum(jnp.maximum(v[0], v[1]), jnp.maximum(v[2], v[3]))


def _first_of4(v, top):
    return jnp.where(v[0] == top, 0.0, jnp.where(v[1] == top, 1.0, jnp.where(v[2] == top, 2.0, 3.0)))


def _route_kernel(xp_ref, xs_ref, mod_ref, g_ref, wrt_ref, brt_ref, upper_ref,
                  dloc_ref, tab_ref, sorted_hbm, hbuf_ref, drow_ref, sorted_ref, zeros_ref, cnt_ref, fill_ref,
                  sem_ref, zsem_ref, *, n_steps, n_p_steps):
    sweep = pl.program_id(0)
    i = pl.program_id(1)
    tm = ROUTE_TM
    meta = n_steps * ROUTE_SUBS * TAB_W
    n_rows = sorted_hbm.shape[0]
    ng = N_GROUPS_MOE

    @pl.when(sweep == 0)
    def _():
        @pl.when(i == 0)
        def _():
            for g in range(ng):
                fill_ref[g] = 0

        m = mod_ref[...]
        x = jnp.where(i < n_p_steps, xp_ref[...], xs_ref[...])
        h = _rms(x, g_ref[...]) * (1.0 + m[4:5]) + m[3:4]
        hb = h.astype(BF16)
        lt = lax.dot_general(wrt_ref[...], hb, (((1,), (1,)), ((), ())), preferred_element_type=F32) + brt_ref[...]
        gl = [lt[r:r + 1, :] for r in range(ng)]
        g_top = _max4(gl)
        g_idx = _first_of4(gl, g_top)
        g_w = 1.0 / (jnp.exp(gl[0] - g_top) + jnp.exp(gl[1] - g_top) + jnp.exp(gl[2] - g_top) + jnp.exp(gl[3] - g_top))
        ev = []
        for k in range(EXPERTS_PER_GROUP):
            cand = [lt[GATE_OFF + EXPERTS_PER_GROUP * r + k:GATE_OFF + EXPERTS_PER_GROUP * r + k + 1, :]
                    for r in range(ng)]
            ev.append(jnp.where(g_idx == 0.0, cand[0], jnp.where(g_idx == 1.0, cand[1],
                                jnp.where(g_idx == 2.0, cand[2], cand[3]))))
        v1 = _max4(ev)
        i1 = _first_of4(ev, v1)
        rest = [jnp.where(i1 == float(k), NEG_BIG, ev[k]) for k in range(EXPERTS_PER_GROUP)]
        v2 = _max4(rest)
        i2 = _first_of4(rest, v2)
        e2 = jnp.exp(v2 - v1)
        w1 = 1.0 / (1.0 + e2)
        w2 = e2 * w1
        gates = [g_w * (jnp.where(i1 == float(k), w1, 0.0) + jnp.where(i2 == float(k), w2, 0.0))
                 for k in range(EXPERTS_PER_GROUP)]

        sub8 = lax.broadcasted_iota(jnp.int32, (8, tm), 0).astype(F32)
        dlocs = []
        for sub in range(ROUTE_SUBS):
            gi = g_idx[:, sub * tm:(sub + 1) * tm]
            hot = jnp.where(sub8 == gi, 1.0, 0.0)
            before = _bdot(hot.astype(BF16), upper_ref[...])
            dl = jnp.sum(before * hot, axis=0, keepdims=True)
            off = 0
            for g in range(ng):
                n_g = _round_up(jnp.sum(hot[g:g + 1, :]).astype(jnp.int32), ROUTE_PAD)
                cnt_ref[(i * ROUTE_SUBS + sub) * ng + g] = n_g
                fill_ref[g] = fill_ref[g] + n_g
                dl = dl + jnp.where(gi == float(g), off.astype(F32) if g else 0.0, 0.0)
                off = off + n_g
            drow_ref[i * ROUTE_SUBS + sub] = jnp.broadcast_to(dl, (8, tm))
            dlocs.append(dl)
        dloc = jnp.concatenate(dlocs, axis=1)
        d_hi = jnp.floor(dloc * (1.0 / DLOC_RADIX))
        g_hi = [gt.astype(BF16).astype(F32) for gt in gates]
        ex_rows = g_hi + [gt - gh for gt, gh in zip(gates, g_hi)] + [d_hi, dloc - DLOC_RADIX * d_hi]
        sub16 = lax.broadcasted_iota(jnp.int32, (16, ROUTE_STEP), 0)
        ex_t = jnp.zeros((16, ROUTE_STEP), F32)
        for r, row in enumerate(ex_rows):
            ex_t = jnp.where(sub16 == r, row, ex_t)
        ex_t = jnp.concatenate([ex_t, jnp.zeros((LANES - 16, ROUTE_STEP), F32)], axis=0)
        extras = ex_t.T
        dloc_ref[...] = jnp.broadcast_to(
            DLOC_RADIX * extras[:, DLOC_HI:DLOC_HI + 1] + extras[:, DLOC_HI + 1:DLOC_HI + 2], (ROUTE_STEP, LANES))
        hbuf_ref[pl.ds(pl.multiple_of(i * ROUTE_STEP, ROUTE_STEP), ROUTE_STEP), :] = (
            jnp.concatenate([hb, extras.astype(BF16)], axis=1))

    @pl.when(sweep == 1)
    def _():
        @pl.when(i == 0)
        def _():
            start = 0
            for g in range(ng):
                tab_ref[meta + g] = fill_ref[g]
                tab_ref[meta + ng + g] = start
                start = start + _round_up(fill_ref[g], FFN_BM)
                fill_ref[g] = 0
            zeros_ref[...] = jnp.zeros_like(zeros_ref)

        row_f = lax.broadcasted_iota(jnp.int32, (SORT_ROWS, tm), 0).astype(F32)
        for sub in range(ROUTE_SUBS):
            tile = i * ROUTE_SUBS + sub

            @pl.when(i >= 1)
            def _():
                _wait_rows(_tile_rows(tab_ref, tile - ROUTE_SUBS), sorted_hbm, sorted_ref.at[sub], sem_ref.at[sub], True)

            rows = hbuf_ref[pl.ds(pl.multiple_of(i * ROUTE_STEP + sub * tm, tm), tm), :]
            onehot = jnp.where(row_f == drow_ref[tile][0:1, :], 1.0, 0.0).astype(BF16)
            sorted_ref[sub] = _bdot(onehot, rows)
            for g in range(ng):
                n_g = cnt_ref[tile * ng + g]
                tab_ref[tile * TAB_W + g] = tab_ref[meta + ng + g] + fill_ref[g]
                tab_ref[tile * TAB_W + ng + g] = n_g
                fill_ref[g] = fill_ref[g] + n_g
            _run_copies(tab_ref, tile, sorted_hbm, sorted_ref.at[sub], sem_ref.at[sub], to_hbm=True, wait=False)

            @pl.when(i == n_steps - 1)
            def _():
                _wait_rows(_tile_rows(tab_ref, tile), sorted_hbm, sorted_ref.at[sub], sem_ref.at[sub], True)

        @pl.when(i == n_steps - 1)
        def _():
            _zero_fill(tab_ref, meta, zeros_ref, sorted_hbm, zsem_ref, n_rows, wait=False)
            _zero_fill(tab_ref, meta, zeros_ref, sorted_hbm, zsem_ref, n_rows, wait=True)


def _merged_specs(n_p_steps, n_s_steps, layer, sample_seq, step_of):
    def p_map(*idx):
        return (jnp.minimum(step_of(*idx), n_p_steps - 1), 0)

    def s_map(*idx):
        return (jnp.clip(step_of(*idx) - n_p_steps, 0, n_s_steps - 1), 0)

    def mod_map(*idx):
        j = step_of(*idx)
        row = jnp.where(j < n_p_steps, 0, 1 + lax.div(jnp.maximum(j - n_p_steps, 0) * ROUTE_STEP, sample_seq))
        return (layer, row, 0, 0)

    return (pl.BlockSpec((ROUTE_STEP, D_MODEL), p_map), pl.BlockSpec((ROUTE_STEP, D_MODEL), s_map),
            pl.BlockSpec((None, None, 6, D_MODEL), mod_map))


def _route(xp, xs, mod, g2, w_rt, b_rt, layer, sample_seq, n_rows):
    n_p_steps = xp.shape[0] // ROUTE_STEP
    n_s_steps = xs.shape[0] // ROUTE_STEP
    n_steps = n_p_steps + n_s_steps
    t = n_steps * ROUTE_STEP
    tm = ROUTE_TM
    upper = jnp.asarray(np.triu(np.ones((tm, tm), np.float32), 1), BF16)
    step_of = lambda s, i: jnp.where(s == 0, i, n_steps - 1)
    p_spec, s_spec, mod_spec = _merged_specs(n_p_steps, n_s_steps, layer, sample_seq, step_of)
    const = lambda a: pl.BlockSpec(a.shape, lambda s, i: (0,) * a.ndim)
    n_tiles = n_steps * ROUTE_SUBS
    return pl.pallas_call(
        functools.partial(_route_kernel, n_steps=n_steps, n_p_steps=n_p_steps),
        grid=(2, n_steps),
        in_specs=[p_spec, s_spec, mod_spec, const(g2), const(w_rt), const(b_rt), const(upper)],
        out_specs=[
            pl.BlockSpec((ROUTE_STEP, LANES), lambda s, i: (step_of(s, i), 0)),
            pl.BlockSpec(memory_space=pltpu.SMEM),
            pl.BlockSpec(memory_space=pl.ANY),
        ],
        out_shape=[
            jax.ShapeDtypeStruct((t, LANES), F32),
            jax.ShapeDtypeStruct(((n_tiles + 1) * TAB_W,), jnp.int32),
            jax.ShapeDtypeStruct((n_rows, XS_W), F32),
        ],
        scratch_shapes=[
            pltpu.VMEM((t, XS_W), BF16),
            pltpu.VMEM((n_tiles, 8, tm), F32),
            pltpu.VMEM((ROUTE_SUBS, SORT_ROWS, XS_W), F32),
            pltpu.VMEM((FFN_BM, XS_W), F32),
            pltpu.SMEM((n_tiles * N_GROUPS_MOE,), jnp.int32),
            pltpu.SMEM((N_GROUPS_MOE,), jnp.int32),
            pltpu.SemaphoreType.DMA((ROUTE_SUBS,)),
            pltpu.SemaphoreType.DMA(()),
        ],
        compiler_params=_cparams("arbitrary", "arbitrary"),
        name="moe_route",
    )(xp, xs, mod, g2, w_rt, b_rt, upper)


def _ffn_lookup(i, tab_ref, meta):
    fills = [tab_ref[meta + g] for g in range(N_GROUPS_MOE)]
    edges = []
    acc = 0
    for f in fills:
        acc = acc + lax.div(f + (FFN_BM - 1), FFN_BM)
        edges.append(acc)
    total = edges[-1]
    ii = jnp.minimum(i, total - 1)
    grp = sum((ii >= e).astype(jnp.int32) for e in edges[:-1])

    def pick(vals):
        return jnp.where(grp == 0, vals[0], jnp.where(grp == 1, vals[1], jnp.where(grp == 2, vals[2], vals[3])))

    first = pick([0] + edges[:-1])
    return grp, total, ii == first, pick(fills) - (ii - first) * FFN_BM


def _ffn_group_kernel(tab_ref, xs_ref, w1_ref, w3_ref, w2_ref, ys_ref, w1b_ref, w3b_ref, w2b_ref, *, meta):
    i = pl.program_id(0)
    _, total, first_of_group, valid = _ffn_lookup(i, tab_ref, meta)
    active = i < total

    @pl.when(jnp.logical_and(active, first_of_group))
    def _():
        w1b_ref[...] = w1_ref[...].astype(BF16)
        w3b_ref[...] = w3_ref[...].astype(BF16)
        w2b_ref[...] = w2_ref[...].astype(BF16)

    def run(rows):
        hb = xs_ref[0:rows, 0:D_MODEL].astype(BF16)
        ex = xs_ref[0:rows, D_MODEL:XS_W]
        acc = None
        for e in range(EXPERTS_PER_GROUP):
            a = _bdot(hb, w1b_ref[e])
            b = _bdot(hb, w3b_ref[e])
            gate = ex[:, e:e + 1] + ex[:, GATE_LO + e:GATE_LO + e + 1]
            hid = (a * jax.nn.sigmoid(a)) * b * gate
            part = _bdot(hid.astype(BF16), w2b_ref[e])
            acc = part if acc is None else acc + part
        ys_ref[0:rows, :] = acc

    @pl.when(jnp.logical_and(active, valid > FFN_HALF))
    def _():
        run(FFN_BM)

    @pl.when(jnp.logical_and(active, valid <= FFN_HALF))
    def _():
        run(FFN_HALF)
        ys_ref[FFN_HALF:, :] = jnp.zeros((FFN_BM - FFN_HALF, D_MODEL), F32)

    @pl.when(jnp.logical_not(active))
    def _():
        ys_ref[...] = jnp.zeros_like(ys_ref)


def _ffn_group(tab, xs, w1, w3, w2, layer, meta):
    e4 = EXPERTS_PER_GROUP
    n_blocks = xs.shape[0] // FFN_BM
    group_of = lambda i, tab_ref: _ffn_lookup(i, tab_ref, meta)[0]
    grid_spec = pltpu.PrefetchScalarGridSpec(
        num_scalar_prefetch=1,
        grid=(n_blocks,),
        in_specs=[
            pl.BlockSpec((FFN_BM, XS_W), lambda i, tab_ref: (i, 0)),
            pl.BlockSpec((None, e4, D_MODEL, D_EXPERT), lambda i, tab_ref: (layer, group_of(i, tab_ref), 0, 0)),
            pl.BlockSpec((None, e4, D_MODEL, D_EXPERT), lambda i, tab_ref: (layer, group_of(i, tab_ref), 0, 0)),
            pl.BlockSpec((None, e4, D_EXPERT, D_MODEL), lambda i, tab_ref: (layer, group_of(i, tab_ref), 0, 0)),
        ],
        out_specs=pl.BlockSpec((FFN_BM, D_MODEL), lambda i, tab_ref: (i, 0)),
        scratch_shapes=[
            pltpu.VMEM((e4, D_MODEL, D_EXPERT), BF16),
            pltpu.VMEM((e4, D_MODEL, D_EXPERT), BF16),
            pltpu.VMEM((e4, D_EXPERT, D_MODEL), BF16),
        ],
    )
    return pl.pallas_call(
        functools.partial(_ffn_group_kernel, meta=meta),
        grid_spec=grid_spec,
        out_shape=jax.ShapeDtypeStruct((xs.shape[0], D_MODEL), F32),
        compiler_params=_cparams("arbitrary"),
        name="moe_ffn",
    )(tab, xs, w1, w3, w2)


def _fetch_runs(tab_ref, ys_ref, runs_ref, sem_ref, tile, slot):
    _run_copies(tab_ref, tile, ys_ref, runs_ref.at[slot], sem_ref.at[slot], to_hbm=False, wait=False)


def _unsort(tab_ref, ys_ref, runs_ref, sem_ref, dloc_col, tile, slot):
    covered = _tile_rows(tab_ref, tile)
    _wait_rows(covered, ys_ref, runs_ref.at[slot], sem_ref.at[slot], False)
    runs_ref[slot, pl.ds(pl.multiple_of(covered, ROUTE_PAD), LANES), :] = jnp.zeros((LANES, D_MODEL), F32)
    yb = runs_ref[slot, 0:SORT_ROWS, :].astype(BF16)
    row_f = lax.broadcasted_iota(jnp.int32, (ROUTE_TM, SORT_ROWS), 1).astype(F32)
    return _bdot(jnp.where(row_f == dloc_col, 1.0, 0.0).astype(BF16), yb)


def _combine_kernel(tab_ref, *refs, n_steps, first_step, n_p_steps, final_norm):
    with_prompt = first_step < n_p_steps
    if with_prompt:
        xp_ref, xs_ref, mod_ref, dloc_ref, fg_ref, ys_ref, op_ref, os_ref, runs_ref, sem_ref = refs
    else:
        xs_ref, mod_ref, dloc_ref, fg_ref, ys_ref, os_ref, runs_ref, sem_ref = refs
    i = pl.program_id(0)
    par = lax.rem(i, 2)
    step = first_step + i
    tm = ROUTE_TM

    def fetch(st, parity):
        for sub in range(ROUTE_SUBS):
            _fetch_runs(tab_ref, ys_ref, runs_ref, sem_ref, st * ROUTE_SUBS + sub, parity * ROUTE_SUBS + sub)

    @pl.when(i == 0)
    def _():
        fetch(first_step, 0)

    @pl.when(i + 1 < n_steps)
    def _():
        fetch(step + 1, 1 - par)

    parts = [_unsort(tab_ref, ys_ref, runs_ref, sem_ref, dloc_ref[sub * tm:(sub + 1) * tm, 0:1],
                     step * ROUTE_SUBS + sub, par * ROUTE_SUBS + sub) for sub in range(ROUTE_SUBS)]
    is_prompt = step < n_p_steps
    x = jnp.where(is_prompt, xp_ref[...], xs_ref[...]) if with_prompt else xs_ref[...]
    x2 = x + mod_ref[5:6, :] * jnp.concatenate(parts, axis=0)
    if final_norm:
        x2 = _rms(x2, fg_ref[...])
    if with_prompt:
        @pl.when(is_prompt)
        def _():
            op_ref[...] = x2

        @pl.when(jnp.logical_not(is_prompt))
        def _():
            os_ref[...] = x2
    else:
        os_ref[...] = x2


def _combine(tab, xp, xs, mod, dloc, final_g, ys, layer, sample_seq, final_norm, with_prompt):
    n_p_steps = xp.shape[0] // ROUTE_STEP
    n_s_steps = xs.shape[0] // ROUTE_STEP
    first_step = 0 if with_prompt else n_p_steps
    n_steps = n_p_steps + n_s_steps - first_step
    step_of = lambda i, tab_ref: i + first_step
    p_spec, s_spec, mod_spec = _merged_specs(n_p_steps, n_s_steps, layer, sample_seq, step_of)
    streams = [p_spec, s_spec] if with_prompt else [s_spec]
    grid_spec = pltpu.PrefetchScalarGridSpec(
        num_scalar_prefetch=1,
        grid=(n_steps,),
        in_specs=streams + [
            mod_spec,
            pl.BlockSpec((ROUTE_STEP, LANES), lambda i, tab_ref: (i + first_step, 0)),
            pl.BlockSpec((1, D_MODEL), lambda i, tab_ref: (0, 0)),
            pl.BlockSpec(memory_space=pl.ANY),
        ],
        out_specs=streams,
        scratch_shapes=[
            pltpu.VMEM((2 * ROUTE_SUBS, RUN_ROWS, D_MODEL), F32),
            pltpu.SemaphoreType.DMA((2 * ROUTE_SUBS,)),
        ],
    )
    arrays = [xp, xs] if with_prompt else [xs]
    return pl.pallas_call(
        functools.partial(_combine_kernel, n_steps=n_steps, first_step=first_step, n_p_steps=n_p_steps,
                          final_norm=final_norm),
        grid_spec=grid_spec,
        out_shape=[jax.ShapeDtypeStruct(a.shape, F32) for a in arrays],
        compiler_params=_cparams("arbitrary"),
        name="moe_combine",
    )(tab, *arrays, mod, dloc, final_g, ys)


def _moe(xp, xs, mod, g2, w_r, b_r, w1, w3, w2, final_g, layer, sample_seq, final_norm, defer_prompt):
    t = xp.shape[0] + xs.shape[0]
    n_tiles = t // ROUTE_TM
    max_rows = t + N_GROUPS_MOE * (ROUTE_PAD - 1) * n_tiles
    n_rows = (-(-max_rows // FFN_BM) + N_GROUPS_MOE) * FFN_BM
    dloc, tab, sorted_x = _route(xp, xs, mod, g2, w_r, b_r, layer, sample_seq, n_rows)
    ys = _ffn_group(tab, sorted_x, w1, w3, w2, layer, n_tiles * TAB_W)
    out = _combine(tab, xp, xs, mod, dloc, final_g, ys, layer, sample_seq, final_norm, not defer_prompt)
    if defer_prompt:
        return None, out[0], (tab, dloc, ys)
    return out[0], out[1], None


def _rope_tables(seq):
    half = QK_ROPE // 2
    nf = half // 2
    inv = ROPE_BASE ** (-np.arange(nf, dtype=np.float64) / nf)
    pos = np.arange(seq)
    row = (pos // GRID_W).astype(np.float64)
    col = (pos % GRID_W).astype(np.float64)
    cos = np.ones((seq, HEAD_PAD), np.float64)
    sin_a = np.zeros((seq, HEAD_PAD), np.float64)
    sin_b = np.zeros((seq, HEAD_PAD), np.float64)
    for part, p in enumerate((row, col)):
        ang = p[:, None] * inv[None, :]
        base = ROPE_OFF + part * half
        cos[:, base:base + nf] = np.cos(ang)
        cos[:, base + nf:base + half] = np.cos(ang)
        sin_a[:, base:base + nf] = -np.sin(ang)
        sin_b[:, base + nf:base + half] = np.sin(ang)
    return tuple(jnp.asarray(a, F32) for a in (cos, sin_a, sin_b))


def _apply_rope(x, cos, sin_a, sin_b, reps):
    nf = QK_ROPE // 4
    width = x.shape[1]
    if reps > 1:
        cos, sin_a, sin_b = (jnp.concatenate([a] * reps, axis=1) for a in (cos, sin_a, sin_b))
    return x * cos + pltpu.roll(x, width - nf, axis=1) * sin_a + pltpu.roll(x, nf, axis=1) * sin_b


def _den_ones():
    lane = lax.broadcasted_iota(jnp.int32, (1, VX_W), 1)
    return jnp.where((lane & (PAIR_W - 1)) == DEN_COL, 1.0, 0.0)


def _odd_in_kernel(*refs, rope, emit_cache):
    x_ref, mod_ref, g_ref, w_ref, qg_ref, wq_ref, kg_ref, wk_ref, wv_ref, cs_ref = refs[:10]
    refs = refs[10:]
    if rope:
        cos_ref, sa_ref, sb_ref = refs[:3]
        refs = refs[3:]
    y_ref, q_ref, k_ref, v_ref = refs[:4]
    refs = refs[4:]
    m = mod_ref[...]
    h = _rms(x_ref[...], g_ref[...]) * (1.0 + m[1:2]) + m[0:1]
    z = _bdot(h.astype(BF16), w_ref[...])
    zc = z[:, 0:D_C]
    qc = z[:, D_C:D_C + Q_LORA]
    kvc = z[:, D_C + Q_LORA:D_C + Q_LORA + KV_LORA]
    kpe = z[:, D_C + Q_LORA + KV_LORA:]
    q = _bdot(_rms(qc, qg_ref[...]).astype(BF16), wq_ref[...])
    kvn = _rms(kvc, kg_ref[...])
    if emit_cache:
        ckv_ref, kpe_ref = refs
        ckv_ref[...] = kvn
        kpe_ref[...] = kpe[:, ROPE_OFF:ROPE_OFF + QK_ROPE]
    if rope:
        tabs = (cos_ref[...], sa_ref[...], sb_ref[...])
        q = _apply_rope(q, *tabs, reps=N_HEADS)
        kpe = _apply_rope(kpe, *tabs, reps=1)
    kvb = kvn.astype(BF16)
    k = _bdot(kvb, wk_ref[...]) + jnp.concatenate([kpe] * N_HEADS, axis=1)
    scale = math.log2(math.e) / math.sqrt(QK_NOPE + QK_ROPE)
    q_ref[...] = (q * scale).astype(BF16)
    k_ref[...] = k.astype(BF16)
    v_ref[...] = (_bdot(kvb, wv_ref[...]) + _den_ones()).astype(BF16)
    y = _bdot(zc.astype(BF16), cs_ref[...])
    y_ref[0, :, :] = y[:, 0:D_C].astype(BF16)
    y_ref[1, :, :] = y[:, D_C:2 * D_C].astype(BF16)


def _odd_in(x3, mod, g, w_in, q_g, w_q, kv_g, w_k, w_v, cs, stream, layer, rope_tabs, emit_cache):
    b, s, _ = x3.shape
    tm = min(s, 512)
    n_i = s // tm
    rope = rope_tabs is not None
    const = lambda a: pl.BlockSpec(a.shape, lambda bi, i: (0,) * a.ndim)
    in_specs = [
        pl.BlockSpec((None, tm, D_MODEL), lambda bi, i: (bi, i, 0)),
        _mod_spec(stream, layer),
        const(g), const(w_in), const(q_g), const(w_q), const(kv_g), const(w_k), const(w_v), const(cs),
    ]
    args = [x3, mod, g, w_in, q_g, w_q, kv_g, w_k, w_v, cs]
    if rope:
        in_specs += [pl.BlockSpec((tm, HEAD_PAD), lambda bi, i: (i, 0))] * 3
        args += list(rope_tabs)
    hq = N_HEADS * HEAD_PAD
    out_specs = [
        pl.BlockSpec((None, 2, tm, D_C), lambda bi, i: (bi, 0, i, 0)),
        pl.BlockSpec((None, tm, hq), lambda bi, i: (bi, i, 0)),
        pl.BlockSpec((None, tm, hq), lambda bi, i: (bi, i, 0)),
        pl.BlockSpec((None, tm, VX_W), lambda bi, i: (bi, i, 0)),
    ]
    out_shape = [
        jax.ShapeDtypeStruct((b, 2, s, D_C), BF16),
        jax.ShapeDtypeStruct((b, s, hq), BF16),
        jax.ShapeDtypeStruct((b, s, hq), BF16),
        jax.ShapeDtypeStruct((b, s, VX_W), BF16),
    ]
    if emit_cache:
        out_specs += [
            pl.BlockSpec((None, tm, KV_LORA), lambda bi, i: (bi, i, 0)),
            pl.BlockSpec((None, tm, QK_ROPE), lambda bi, i: (bi, i, 0)),
        ]
        out_shape += [
            jax.ShapeDtypeStruct((b, s, KV_LORA), F32),
            jax.ShapeDtypeStruct((b, s, QK_ROPE), F32),
        ]
    return pl.pallas_call(
        functools.partial(_odd_in_kernel, rope=rope, emit_cache=emit_cache),
        grid=(b, n_i),
        in_specs=in_specs,
        out_specs=out_specs,
        out_shape=out_shape,
        compiler_params=_cparams("parallel", "parallel"),
        name="odd_in",
    )(*args)


def _cache_kv_kernel(c_ref, p_ref, wk_ref, wv_ref, k_ref, v_ref):
    cb = c_ref[...].astype(BF16)
    k = _bdot(cb, wk_ref[...]) + jnp.concatenate([p_ref[...]] * N_HEADS, axis=1)
    k_ref[...] = k.astype(BF16)
    v_ref[...] = (_bdot(cb, wv_ref[...]) + _den_ones()).astype(BF16)


def _cache_kv(ckv, kpe_blk, w_k, w_v):
    b, p, _ = ckv.shape
    hq = N_HEADS * HEAD_PAD
    return pl.pallas_call(
        _cache_kv_kernel,
        grid=(b,),
        in_specs=[
            pl.BlockSpec((None, p, KV_LORA), lambda bi: (bi, 0, 0)),
            pl.BlockSpec((None, p, HEAD_PAD), lambda bi: (bi, 0, 0)),
            pl.BlockSpec(w_k.shape, lambda bi: (0, 0)),
            pl.BlockSpec(w_v.shape, lambda bi: (0, 0)),
        ],
        out_specs=[
            pl.BlockSpec((None, p, hq), lambda bi: (bi, 0, 0)),
            pl.BlockSpec((None, p, VX_W), lambda bi: (bi, 0, 0)),
        ],
        out_shape=[
            jax.ShapeDtypeStruct((b, p, hq), BF16),
            jax.ShapeDtypeStruct((b, p, VX_W), BF16),
        ],
        compiler_params=_cparams("parallel"),
        name="cache_kv",
    )(ckv, kpe_blk, w_k, w_v)


def _odd_mix_kernel(*refs, with_cache):
    q_ref, k_ref, v_ref = refs[:3]
    refs = refs[3:]
    if with_cache:
        kc_ref, vc_ref = refs[:2]
        refs = refs[2:]
    y_ref, f_ref, x_ref, mod_ref, wo_ref, o_ref, a_ref = refs
    tq = q_ref.shape[0]
    lane = lax.broadcasted_iota(jnp.int32, (tq, 2 * V_DIM), 1)
    for pair in range(N_HEADS // 2):
        vcols = slice(pair * PAIR_W, (pair + 1) * PAIR_W)
        outs = []
        for h in (2 * pair, 2 * pair + 1):
            hcols = slice(h * HEAD_PAD, (h + 1) * HEAD_PAD)
            qh = q_ref[:, hcols]
            s = lax.dot_general(qh, k_ref[:, hcols], NT_DIMS, preferred_element_type=F32)
            top = jnp.max(s, axis=-1, keepdims=True)
            if with_cache:
                sc = lax.dot_general(qh, kc_ref[:, hcols], NT_DIMS, preferred_element_type=F32)
                top = jnp.maximum(top, jnp.max(sc, axis=-1, keepdims=True))
            if with_cache:
                acc = _bdot(jnp.exp2((s - top).astype(BF16)), v_ref[:, vcols])
                acc = acc + _bdot(jnp.exp2((sc - top).astype(BF16)), vc_ref[:, vcols])
                outs.append(acc[:, 0:2 * V_DIM] / acc[:, DEN_COL:DEN_COL + 1])
            else:
                p = jnp.exp2(s - top)
                den = jnp.sum(p, axis=-1, keepdims=True)
                outs.append(_bdot(p.astype(BF16), v_ref[:, pair * PAIR_W:pair * PAIR_W + 2 * V_DIM]) / den)
        a_ref[:, pair * 2 * V_DIM:(pair + 1) * 2 * V_DIM] = jnp.where(lane < V_DIM, outs[0], outs[1]).astype(BF16)
    f = _bdot(f_ref[...], y_ref[...].reshape(-1, D_C))
    o = _bdot(f.astype(BF16), wo_ref[0:D_C, :]) + _bdot(a_ref[...], wo_ref[D_C:, :])
    o_ref[...] = x_ref[...] + mod_ref[2:3, :] * o


def _odd_whole_kernel(tab_ref, x_ref, mod_ref, g_ref, w_ref, qg_ref, wq_ref, kg_ref, wk_ref, wv_ref, cs_ref, f_ref,
                      wo_ref, modp_ref, dloc_ref, ys_ref, o_ref, ckv_ref, kpe_ref,
                      y_ref, q_ref, k_ref, v_ref, a_ref, x2_ref, runs_ref, sem_ref):
    bi = pl.program_id(0)
    par = lax.rem(bi, 2)

    @pl.when(bi == 0)
    def _():
        _fetch_runs(tab_ref, ys_ref, runs_ref, sem_ref, 0, 0)

    @pl.when(bi + 1 < pl.num_programs(0))
    def _():
        _fetch_runs(tab_ref, ys_ref, runs_ref, sem_ref, bi + 1, 1 - par)

    moe = _unsort(tab_ref, ys_ref, runs_ref, sem_ref, dloc_ref[:, 0:1], bi, par)
    x2_ref[...] = x_ref[...] + modp_ref[5:6, :] * moe
    _odd_in_kernel(x2_ref, mod_ref, g_ref, w_ref, qg_ref, wq_ref, kg_ref, wk_ref, wv_ref, cs_ref,
                   y_ref, q_ref, k_ref, v_ref, ckv_ref, kpe_ref, rope=False, emit_cache=True)
    _odd_mix_kernel(q_ref, k_ref, v_ref, y_ref, f_ref, x2_ref, mod_ref, wo_ref, o_ref, a_ref, with_cache=False)


def _odd_whole(x3, pending, mod, g, w_in, q_g, w_q, kv_g, w_k, w_v, cs, fmat, w_out, stream, layer):
    tab, dloc, ys = pending
    b, s, _ = x3.shape
    assert s == ROUTE_TM, "one sequence must be one MoE sort tile"
    hq = N_HEADS * HEAD_PAD
    hv = N_HEADS * V_DIM
    const = lambda a: pl.BlockSpec(a.shape, lambda bi, tab_ref: (0,) * a.ndim)
    row_block = lambda w: pl.BlockSpec((None, s, w), lambda bi, tab_ref: (bi, 0, 0))
    mod_block = lambda lyr: pl.BlockSpec((None, None, 6, D_MODEL),
                                         lambda bi, tab_ref: (lyr, stream.row_of_batch(bi), 0, 0))
    grid_spec = pltpu.PrefetchScalarGridSpec(
        num_scalar_prefetch=1,
        grid=(b,),
        in_specs=[
            row_block(D_MODEL), mod_block(layer),
            const(g), const(w_in), const(q_g), const(w_q), const(kv_g), const(w_k), const(w_v), const(cs),
            const(fmat), const(w_out),
            mod_block(layer - 1),
            pl.BlockSpec((s, LANES), lambda bi, tab_ref: (bi, 0)),
            pl.BlockSpec(memory_space=pl.ANY),
        ],
        out_specs=[row_block(D_MODEL), row_block(KV_LORA), row_block(QK_ROPE)],
        scratch_shapes=[
            pltpu.VMEM((2, s, D_C), BF16),
            pltpu.VMEM((s, hq), BF16),
            pltpu.VMEM((s, hq), BF16),
            pltpu.VMEM((s, VX_W), BF16),
            pltpu.VMEM((s, hv), BF16),
            pltpu.VMEM((s, D_MODEL), F32),
            pltpu.VMEM((2, RUN_ROWS, D_MODEL), F32),
            pltpu.SemaphoreType.DMA((2,)),
        ],
    )
    return pl.pallas_call(
        _odd_whole_kernel,
        grid_spec=grid_spec,
        out_shape=[
            jax.ShapeDtypeStruct((b, s, D_MODEL), F32),
            jax.ShapeDtypeStruct((b, s, KV_LORA), F32),
            jax.ShapeDtypeStruct((b, s, QK_ROPE), F32),
        ],
        compiler_params=_cparams("arbitrary"),
        name="odd_whole",
    )(tab, x3, mod, g, w_in, q_g, w_q, kv_g, w_k, w_v, cs, fmat, w_out, mod, dloc, ys)


def _odd_mix(q, k, v, kc, vc, y, fmat, x3, mod, w_out, stream, layer):
    b, s, hq = q.shape
    tq = min(s, 256)
    n_i = s // tq
    with_cache = kc is not None
    hv = N_HEADS * V_DIM
    mode = dict(pipeline_mode=pl.Buffered(1)) if with_cache else {}

    def per_batch(rows, cols):
        return pl.BlockSpec((None, rows, cols), lambda bi, i: (bi, 0, 0), **mode)

    in_specs = [pl.BlockSpec((None, tq, hq), lambda bi, i: (bi, i, 0)), per_batch(s, hq), per_batch(s, VX_W)]
    args = [q, k, v]
    if with_cache:
        p = kc.shape[1]
        in_specs += [per_batch(p, hq), per_batch(p, VX_W)]
        args += [kc, vc]
    in_specs += [
        per_batch(2 * s, D_C),
        pl.BlockSpec((tq, 2 * s), lambda bi, i: (i, 0)),
        pl.BlockSpec((None, tq, D_MODEL), lambda bi, i: (bi, i, 0)),
        _mod_spec(stream, layer),
        pl.BlockSpec(w_out.shape, lambda bi, i: (0, 0), **mode),
    ]
    args += [y, fmat, x3, mod, w_out]
    return pl.pallas_call(
        functools.partial(_odd_mix_kernel, with_cache=with_cache),
        grid=(b, n_i),
        in_specs=in_specs,
        out_specs=pl.BlockSpec((None, tq, D_MODEL), lambda bi, i: (bi, i, 0)),
        out_shape=jax.ShapeDtypeStruct((b, s, D_MODEL), F32),
        scratch_shapes=[pltpu.VMEM((tq, hv), BF16)],
        compiler_params=_cparams("parallel", "arbitrary"),
        name="odd_mix",
    )(*args)


def _dft_tables(seq):
    jc = np.arange(C_GW)
    ang_c = 2.0 * np.pi * np.outer(jc, jc) / C_GW
    eye = np.eye(C_GROUPS)
    cs = np.concatenate([np.kron(eye, np.cos(ang_c)), np.kron(eye, np.sin(ang_c))], axis=1)
    jn = np.arange(seq)
    ang_n = 2.0 * np.pi * (np.outer(jn, jn) % seq) / seq
    scale = 1.0 / math.sqrt(seq * C_GW)
    fmat = np.concatenate([np.cos(ang_n), -np.sin(ang_n)], axis=1) * scale
    return jnp.asarray(cs, F32).astype(BF16), jnp.asarray(fmat, F32).astype(BF16)


def _odd_weights(w_in, w_uq, w_ukv):
    d = w_in.shape[0]
    base = D_C + Q_LORA + KV_LORA
    kpe_blk = jnp.zeros((d, HEAD_PAD), w_in.dtype).at[:, ROPE_OFF:ROPE_OFF + QK_ROPE].set(w_in[:, base:])
    w_in_p = jnp.concatenate([w_in[:, :base], kpe_blk], axis=1).astype(BF16)
    qh = w_uq.reshape(Q_LORA, N_HEADS, QK_NOPE + QK_ROPE)
    w_q = jnp.pad(qh, ((0, 0), (0, 0), (0, HEAD_PAD - QK_NOPE - QK_ROPE))).reshape(Q_LORA, -1).astype(BF16)
    kvh = w_ukv.reshape(KV_LORA, N_HEADS, QK_NOPE + V_DIM)
    w_k = jnp.pad(kvh[:, :, :QK_NOPE], ((0, 0), (0, 0), (0, HEAD_PAD - QK_NOPE))).reshape(KV_LORA, -1)
    w_v = kvh[:, :, QK_NOPE:].reshape(KV_LORA, N_HEADS // 2, 2 * V_DIM)
    w_v = jnp.pad(w_v, ((0, 0), (0, 0), (0, PAIR_W - 2 * V_DIM))).reshape(KV_LORA, VX_W)
    return w_in_p, w_q, w_k.astype(BF16), w_v.astype(BF16)


ROUTER_ROWS = 32


def _router_weights(wg, bg, we, be):
    d = wg.shape[0]
    w = jnp.concatenate([wg, we.reshape(d, N_EXPERTS)], axis=1).T
    w = jnp.pad(w, ((0, ROUTER_ROWS - w.shape[0]), (0, 0))).astype(BF16)
    b = jnp.concatenate([bg, be.reshape(N_EXPERTS)])
    b = jnp.pad(b, (0, ROUTER_ROWS - b.shape[0])).reshape(ROUTER_ROWS, 1).astype(F32)
    return w, b


def kernel(x_prompt, x_sample, cache_ckv, cache_kpe, c, c_ctx, mod_w, mod_b, norm1_g, norm2_g,
           ev_w_in, ev_conv_w, ev_sgu_norm_g, ev_sgu_w, ev_sgu_b, ev_w_out,
           od_w_in, od_q_norm_g, od_w_uq, od_kv_norm_g, od_w_ukv, od_w_out,
           moe_wg, moe_bg, moe_we, moe_be, moe_w1, moe_w3, moe_w2, final_norm_g):
    bp, n_p, d = x_prompt.shape
    bs, n_s, _ = x_sample.shape
    streams = [(_Stream(bp, n_p, True), x_prompt), (_Stream(bs, n_s, False), x_sample)]

    n_rows = 1 + bs
    cond_t = jnp.concatenate([c_ctx[None, :], c], axis=0).T
    mod = _adaln(cond_t, mod_w, mod_b, n_rows)

    final_g = final_norm_g.reshape(1, d)
    xs = [x for _, x in streams]
    new_ckv, new_kpe = [], []
    pending = None
    for l in range(DEPTH):
        j = l // 2
        g1 = norm1_g[l].reshape(1, d)
        g2 = norm2_g[l].reshape(1, d)
        w_r, b_r = _router_weights(moe_wg[l], moe_bg[l], moe_we[l], moe_be[l])
        last = l == DEPTH - 1
        if l % 2 == 0:
            w_in = ev_w_in[j].astype(BF16)
            w_out = ev_w_out[j].astype(BF16)
            sgu_w = ev_sgu_w[j].astype(BF16)
            sgu_g = ev_sgu_norm_g[j].reshape(1, D_B)
            sgu_bias = jnp.repeat(ev_sgu_b[j].T, D_B // B_GROUPS, axis=1)
            for si, (st, _) in enumerate(streams):
                xs[si] = _even_layer(xs[si], mod, g1, w_in, ev_conv_w[j], sgu_g, sgu_w, sgu_bias, w_out, st, l)
        else:
            w_in, w_q, w_k, w_v = _odd_weights(od_w_in[j], od_w_uq[j], od_w_ukv[j])
            w_out = od_w_out[j].astype(BF16)
            q_g = od_q_norm_g[j].reshape(1, Q_LORA)
            kv_g = od_kv_norm_g[j].reshape(1, KV_LORA)
            for si, (st, _) in enumerate(streams):
                x3 = xs[si]
                cs, fmat = _dft_tables(st.seq)
                if st.shared_cond:
                    xs[si], ckv, kpe = _odd_whole(x3, pending, mod, g1, w_in, q_g, w_q, kv_g, w_k, w_v, cs, fmat,
                                                  w_out, st, l)
                    new_ckv.append(ckv)
                    new_kpe.append(kpe)
                    continue
                y, q, k, v = _odd_in(x3, mod, g1, w_in, q_g, w_q, kv_g, w_k, w_v, cs, st, l,
                                     _rope_tables(st.seq), False)
                kpe_blk = jnp.pad(cache_kpe[:, j], ((0, 0), (0, 0), (ROPE_OFF, HEAD_PAD - ROPE_OFF - QK_ROPE)))
                kc, vc = _cache_kv(cache_ckv[:, j], kpe_blk, w_k, w_v)
                xs[si] = _odd_mix(q, k, v, kc, vc, y.reshape(st.batch, 2 * st.seq, D_C), fmat, x3, mod,
                                  w_out, st, l)
        defer = not last and (l + 1) % 2 == 1
        x2p, x2s, pending = _moe(xs[0].reshape(bp * n_p, d), xs[1].reshape(bs * n_s, d), mod, g2, w_r, b_r,
                                 moe_w1, moe_w3, moe_w2, final_g, l, n_s, last, defer)
        xs = [xs[0] if defer else x2p.reshape(bp, n_p, d), x2s.reshape(bs, n_s, d)]
    return (xs[0], xs[1], jnp.stack(new_ckv, axis=1), jnp.stack(new_kpe, axis=1))
```

```python
import functools
import math

import numpy as np
import jax
import jax.numpy as jnp
from jax import lax
from jax.experimental import pallas as pl
from jax.experimental.pallas import tpu as pltpu

D_MODEL = 1024
DEPTH = 2
GRID_W = 64
D_A = D_MODEL // 2
D_B = D_MODEL // 2
B_GROUPS = 4
CHUNK = 128
D_EVEN_IN = 3 * D_A + 2 * D_B
D_C = D_MODEL // 4
C_GROUPS = 4
C_GW = D_C // C_GROUPS
N_HEADS = 12
QK_NOPE = 64
QK_ROPE = 32
V_DIM = 64
Q_LORA = 384
KV_LORA = 256
ROPE_BASE = 10000.0
N_GROUPS_MOE = 4
EXPERTS_PER_GROUP = 4
N_EXPERTS = N_GROUPS_MOE * EXPERTS_PER_GROUP
D_EXPERT = 256
EPS = 1e-6

LANES = 128
HEAD_PAD = 128
PAIR_W = 256
DEN_COL = 2 * 64
VX_W = 6 * PAIR_W
ROPE_OFF = QK_NOPE
GATE_OFF = N_GROUPS_MOE
NEG_BIG = -1e30
F32 = jnp.float32
BF16 = jnp.bfloat16
VMEM_LIMIT = 56 * 1024 * 1024


def _cparams(*sem):
    return pltpu.CompilerParams(dimension_semantics=sem, vmem_limit_bytes=VMEM_LIMIT)


def _rms(x, g):
    return x * lax.rsqrt(jnp.mean(x * x, axis=-1, keepdims=True) + EPS) * g


def _bdot(a, b):
    return jnp.dot(a, b, preferred_element_type=F32)


NT_DIMS = (((1,), (1,)), ((), ()))
TN_DIMS = (((0,), (0,)), ((), ()))


def _mod_kernel(ct_ref, w_ref, b_ref, o_ref):
    c = ct_ref[...]
    s = c * jax.nn.sigmoid(c)
    w = w_ref[...]
    b = b_ref[...]
    for r in range(o_ref.shape[0]):
        o_ref[r:r + 1, :] = jnp.sum(s[:, r:r + 1] * w, axis=0, keepdims=True) + b


def _adaln(cond_t, mod_w, mod_b, n_rows):
    nt = 2048
    d6 = mod_w.shape[-1]
    out = pl.pallas_call(
        _mod_kernel,
        grid=(DEPTH, d6 // nt),
        in_specs=[
            pl.BlockSpec(cond_t.shape, lambda l, n: (0, 0)),
            pl.BlockSpec((None, D_MODEL, nt), lambda l, n: (l, 0, n)),
            pl.BlockSpec((None, 1, nt), lambda l, n: (l, 0, n)),
        ],
        out_specs=pl.BlockSpec((None, n_rows, nt), lambda l, n: (l, 0, n)),
        out_shape=jax.ShapeDtypeStruct((DEPTH, n_rows, d6), F32),
        compiler_params=_cparams("parallel", "parallel"),
        name="adaln",
    )(cond_t, mod_w, mod_b.reshape(DEPTH, 1, d6))
    return out.reshape(DEPTH, n_rows, 6, D_MODEL)


class _Stream:
    def __init__(self, batch, seq, shared_cond):
        self.batch = batch
        self.seq = seq
        self.tokens = batch * seq
        self.shared_cond = shared_cond

    def row_of_batch(self, b):
        return 0 if self.shared_cond else b + 1

    def row_of_tile(self, i, tm):
        return 0 if self.shared_cond else (i * tm) // self.seq + 1


def _mod_spec(stream, layer, tm=None):
    if tm is None:
        return pl.BlockSpec((None, None, 6, D_MODEL), lambda b, i: (layer, stream.row_of_batch(b), 0, 0))
    return pl.BlockSpec((None, None, 6, D_MODEL), lambda i, *_: (layer, stream.row_of_tile(i, tm), 0, 0))


HALO = 8


def _even_kernel(*refs, has_halo, seq_rows, nc):
    if has_halo:
        x_ref, xp_ref, xn_ref = refs[:3]
        refs = refs[3:]
    else:
        x_ref = refs[0]
        refs = refs[1:]
    mod_ref, g_ref, wi_ref, cw_ref, sg_ref, sw_ref, sb_ref, wo_ref, o_ref, z_ref, y_ref = refs
    i = pl.program_id(1)
    n_i = pl.num_programs(1)
    ts = x_ref.shape[0]
    m = mod_ref[...]
    g = g_ref[...]

    def modulate(x):
        return (_rms(x, g) * (1.0 + m[1:2]) + m[0:1]).astype(BF16)

    x = x_ref[...]
    hb = modulate(x)
    for n in range(D_EVEN_IN // nc):
        z_ref[:, n * nc:(n + 1) * nc] = _bdot(hb, wi_ref[:, n * nc:(n + 1) * nc]).astype(BF16)

    gate_b = z_ref[:, 0:D_A].astype(F32)
    gate_c = z_ref[:, D_A:2 * D_A].astype(F32)
    xa = z_ref[:, 2 * D_A:3 * D_A].astype(F32)
    t = gate_c * xa
    t_prev = pltpu.roll(t, 1, axis=0)
    t_next = pltpu.roll(t, ts - 1, axis=0)
    row = lax.broadcasted_iota(jnp.int32, (ts, 1), 0) & (seq_rows - 1)
    if has_halo:
        hh = modulate(jnp.concatenate([xp_ref[...], xn_ref[...]], axis=0))
        zh = _bdot(hh, wi_ref[:, D_A:3 * D_A]).astype(BF16).astype(F32)
        th = zh[:, 0:D_A] * zh[:, D_A:2 * D_A]
        tp = th[HALO - 1:HALO] * (i > 0).astype(F32)
        tn = th[HALO:HALO + 1] * (i < n_i - 1).astype(F32)
    else:
        tp = tn = 0.0
    t_prev = jnp.where(row == 0, tp, t_prev)
    t_next = jnp.where(row == seq_rows - 1, tn, t_next)
    cw = cw_ref[...]
    y_a = gate_b * (t_prev * cw[0:1] + t * cw[1:2] + t_next * cw[2:3])
    y_ref[:, 0:D_A] = y_a.astype(BF16)

    u = z_ref[:, 3 * D_A:3 * D_A + D_B].astype(F32)
    v = z_ref[:, 3 * D_A + D_B:3 * D_A + 2 * D_B].astype(F32)
    vb = _rms(v, sg_ref[...]).astype(BF16)
    gw = D_B // B_GROUPS
    for c in range(ts // CHUNK):
        rows = slice(c * CHUNK, (c + 1) * CHUNK)
        for gi in range(B_GROUPS):
            cols = slice(gi * gw, (gi + 1) * gw)
            sv = _bdot(sw_ref[gi], vb[rows, cols]) + sb_ref[:, cols]
            y_ref[rows, D_A + gi * gw:D_A + (gi + 1) * gw] = (u[rows, cols] * sv).astype(BF16)

    o_ref[...] = x + m[2:3] * _bdot(y_ref[...], wo_ref[...])


def _even_layer(x3, mod, g, w_in, conv_w, sgu_g, sgu_w, sgu_bias, w_out, stream, layer):
    out_shape = x3.shape
    b, s, _ = x3.shape
    ts = min(s, 256)
    n_i = s // ts
    has_halo = n_i > 1
    seq_rows = ts
    if not has_halo and stream.shared_cond and b % 2 == 0:
        b, s, ts = b // 2, 2 * s, 2 * ts
        x3 = x3.reshape(b, s, D_MODEL)
    hb = ts // HALO
    last_h = s // HALO - 1
    const = lambda a: pl.BlockSpec(a.shape, lambda bi, i: (0,) * a.ndim)
    in_specs = [pl.BlockSpec((None, ts, D_MODEL), lambda bi, i: (bi, i, 0))]
    args = [x3]
    if has_halo:
        in_specs += [
            pl.BlockSpec((None, HALO, D_MODEL), lambda bi, i: (bi, jnp.maximum(i * hb - 1, 0), 0)),
            pl.BlockSpec((None, HALO, D_MODEL), lambda bi, i: (bi, jnp.minimum((i + 1) * hb, last_h), 0)),
        ]
        args += [x3, x3]
    in_specs += [_mod_spec(stream, layer), const(g), const(w_in), const(conv_w), const(sgu_g), const(sgu_w),
                 const(sgu_bias), const(w_out)]
    args += [mod, g, w_in, conv_w, sgu_g, sgu_w, sgu_bias, w_out]
    return pl.pallas_call(
        functools.partial(_even_kernel, has_halo=has_halo, seq_rows=seq_rows, nc=512),
        grid=(b, n_i),
        in_specs=in_specs,
        out_specs=pl.BlockSpec((None, ts, D_MODEL), lambda bi, i: (bi, i, 0)),
        out_shape=jax.ShapeDtypeStruct((b, s, D_MODEL), F32),
        scratch_shapes=[pltpu.VMEM((ts, D_EVEN_IN), BF16), pltpu.VMEM((ts, D_A + D_B), BF16)],
        compiler_params=_cparams("parallel", "parallel"),
        name="even_layer",
    )(*args).reshape(out_shape)


ROUTE_TM = 256
ROUTE_SUBS = 2
ROUTE_STEP = ROUTE_TM * ROUTE_SUBS
ROUTE_PAD = 8
SORT_ROWS = ROUTE_TM + LANES
RUN_ROWS = SORT_ROWS + 32
ROUTE_ROWS = ROUTE_TM + 4 * ROUTE_PAD
XS_W = D_MODEL + LANES
GATE_LO = EXPERTS_PER_GROUP
DLOC_HI = 2 * EXPERTS_PER_GROUP
DLOC_RADIX = 16.0
FFN_BM = 512
FFN_HALF = FFN_BM // 2
RUN_SIZES = (256, 128, 64, 32, 16, 8)
TAB_W = 2 * N_GROUPS_MOE


def _round_up(x, m):
    return lax.div(x + (m - 1), m) * m


def _run_copies(tab_ref, tile, hbm_ref, vmem_ref, sem, to_hbm, wait):
    off = 0
    for g in range(N_GROUPS_MOE):
        start = tab_ref[tile * TAB_W + g]
        n = tab_ref[tile * TAB_W + N_GROUPS_MOE + g]
        for p in RUN_SIZES:
            done = n & (-2 * p)

            @pl.when((n & p) != 0)
            def _():
                v = vmem_ref.at[pl.ds(pl.multiple_of(off + done, ROUTE_PAD), p)]
                h = hbm_ref.at[pl.ds(pl.multiple_of(start + done, ROUTE_PAD), p)]
                cp = pltpu.make_async_copy(v, h, sem) if to_hbm else pltpu.make_async_copy(h, v, sem)
                if wait:
                    cp.wait()
                else:
                    cp.start()
        off = off + n


def _zero_fill(tab_ref, meta, zeros_ref, hbm_ref, sem, n_rows, wait):
    def copy(rows, dst_row):
        cp = pltpu.make_async_copy(zeros_ref.at[pl.ds(0, rows)],
                                   hbm_ref.at[pl.ds(pl.multiple_of(dst_row, ROUTE_PAD), rows)], sem)
        if wait:
            cp.wait()
        else:
            cp.start()

    end = 0
    for g in range(N_GROUPS_MOE):
        fill = tab_ref[meta + g]
        start = tab_ref[meta + N_GROUPS_MOE + g]
        end = start + _round_up(fill, FFN_BM)
        tail = end - start - fill
        for p in RUN_SIZES:
            pl.when((tail & p) != 0)(functools.partial(copy, p, start + fill + (tail & (-2 * p))))
    for k in range(n_rows // FFN_BM):
        pl.when(end + k * FFN_BM < n_rows)(functools.partial(copy, FFN_BM, end + k * FFN_BM))


def _wait_rows(n, hbm_ref, vmem_ref, sem, to_hbm):
    for p in RUN_SIZES:
        @pl.when((n & p) != 0)
        def _():
            v = vmem_ref.at[pl.ds(0, p)]
            h = hbm_ref.at[pl.ds(0, p)]
            (pltpu.make_async_copy(v, h, sem) if to_hbm else pltpu.make_async_copy(h, v, sem)).wait()


def _tile_rows(tab_ref, tile):
    n = 0
    for g in range(N_GROUPS_MOE):
        n = n + tab_ref[tile * TAB_W + N_GROUPS_MOE + g]
    return n


def _max4(v):
    return jnp.maximum(jnp.maximum(v[0], v[1]), jnp.maximum(v[2], v[3]))


def _first_of4(v, top):
    return jnp.where(v[0] == top, 0.0, jnp.where(v[1] == top, 1.0, jnp.where(v[2] == top, 2.0, 3.0)))


def _route_kernel(xp_ref, xs_ref, mod_ref, g_ref, wrt_ref, brt_ref, upper_ref,
                  dloc_ref, tab_ref, sorted_hbm, hbuf_ref, drow_ref, sorted_ref, zeros_ref, cnt_ref, fill_ref,
                  sem_ref, zsem_ref, *, n_steps, n_p_steps):
    sweep = pl.program_id(0)
    i = pl.program_id(1)
    tm = ROUTE_TM
    meta = n_steps * ROUTE_SUBS * TAB_W
    n_rows = sorted_hbm.shape[0]
    ng = N_GROUPS_MOE

    @pl.when(sweep == 0)
    def _():
        @pl.when(i == 0)
        def _():
            for g in range(ng):
                fill_ref[g] = 0

        m = mod_ref[...]
        x = jnp.where(i < n_p_steps, xp_ref[...], xs_ref[...])
        h = _rms(x, g_ref[...]) * (1.0 + m[4:5]) + m[3:4]
        hb = h.astype(BF16)
        lt = lax.dot_general(wrt_ref[...], hb, (((1,), (1,)), ((), ())), preferred_element_type=F32) + brt_ref[...]
        gl = [lt[r:r + 1, :] for r in range(ng)]
        g_top = _max4(gl)
        g_idx = _first_of4(gl, g_top)
        g_w = 1.0 / (jnp.exp(gl[0] - g_top) + jnp.exp(gl[1] - g_top) + jnp.exp(gl[2] - g_top) + jnp.exp(gl[3] - g_top))
        ev = []
        for k in range(EXPERTS_PER_GROUP):
            cand = [lt[GATE_OFF + EXPERTS_PER_GROUP * r + k:GATE_OFF + EXPERTS_PER_GROUP * r + k + 1, :]
                    for r in range(ng)]
            ev.append(jnp.where(g_idx == 0.0, cand[0], jnp.where(g_idx == 1.0, cand[1],
                                jnp.where(g_idx == 2.0, cand[2], cand[3]))))
        v1 = _max4(ev)
        i1 = _first_of4(ev, v1)
        rest = [jnp.where(i1 == float(k), NEG_BIG, ev[k]) for k in range(EXPERTS_PER_GROUP)]
        v2 = _max4(rest)
        i2 = _first_of4(rest, v2)
        e2 = jnp.exp(v2 - v1)
        w1 = 1.0 / (1.0 + e2)
        w2 = e2 * w1
        gates = [g_w * (jnp.where(i1 == float(k), w1, 0.0) + jnp.where(i2 == float(k), w2, 0.0))
                 for k in range(EXPERTS_PER_GROUP)]

        sub8 = lax.broadcasted_iota(jnp.int32, (8, tm), 0).astype(F32)
        dlocs = []
        for sub in range(ROUTE_SUBS):
            gi = g_idx[:, sub * tm:(sub + 1) * tm]
            hot = jnp.where(sub8 == gi, 1.0, 0.0)
            before = _bdot(hot.astype(BF16), upper_ref[...])
            dl = jnp.sum(before * hot, axis=0, keepdims=True)
            off = 0
            for g in range(ng):
                n_g = _round_up(jnp.sum(hot[g:g + 1, :]).astype(jnp.int32), ROUTE_PAD)
                cnt_ref[(i * ROUTE_SUBS + sub) * ng + g] = n_g
                fill_ref[g] = fill_ref[g] + n_g
                dl = dl + jnp.where(gi == float(g), off.astype(F32) if g else 0.0, 0.0)
                off = off + n_g
            drow_ref[i * ROUTE_SUBS + sub] = jnp.broadcast_to(dl, (8, tm))
            dlocs.append(dl)
        dloc = jnp.concatenate(dlocs, axis=1)
        d_hi = jnp.floor(dloc * (1.0 / DLOC_RADIX))
        g_hi = [gt.astype(BF16).astype(F32) for gt in gates]
        ex_rows = g_hi + [gt - gh for gt, gh in zip(gates, g_hi)] + [d_hi, dloc - DLOC_RADIX * d_hi]
        sub16 = lax.broadcasted_iota(jnp.int32, (16, ROUTE_STEP), 0)
        ex_t = jnp.zeros((16, ROUTE_STEP), F32)
        for r, row in enumerate(ex_rows):
            ex_t = jnp.where(sub16 == r, row, ex_t)
        ex_t = jnp.concatenate([ex_t, jnp.zeros((LANES - 16, ROUTE_STEP), F32)], axis=0)
        extras = ex_t.T
        dloc_ref[...] = jnp.broadcast_to(
            DLOC_RADIX * extras[:, DLOC_HI:DLOC_HI + 1] + extras[:, DLOC_HI + 1:DLOC_HI + 2], (ROUTE_STEP, LANES))
        hbuf_ref[pl.ds(pl.multiple_of(i * ROUTE_STEP, ROUTE_STEP), ROUTE_STEP), :] = (
            jnp.concatenate([hb, extras.astype(BF16)], axis=1))

    @pl.when(sweep == 1)
    def _():
        @pl.when(i == 0)
        def _():
            start = 0
            for g in range(ng):
                tab_ref[meta + g] = fill_ref[g]
                tab_ref[meta + ng + g] = start
                start = start + _round_up(fill_ref[g], FFN_BM)
                fill_ref[g] = 0
            zeros_ref[...] = jnp.zeros_like(zeros_ref)

        row_f = lax.broadcasted_iota(jnp.int32, (ROUTE_ROWS, tm), 0).astype(F32)
        for sub in range(ROUTE_SUBS):
            tile = i * ROUTE_SUBS + sub

            @pl.when(i >= 1)
            def _():
                _wait_rows(_tile_rows(tab_ref, tile - ROUTE_SUBS), sorted_hbm, sorted_ref.at[sub], sem_ref.at[sub], True)

            rows = hbuf_ref[pl.ds(pl.multiple_of(i * ROUTE_STEP + sub * tm, tm), tm), :]
            onehot = jnp.where(row_f == drow_ref[tile][0:1, :], 1.0, 0.0).astype(BF16)
            sorted_ref[sub] = _bdot(onehot, rows)
            for g in range(ng):
                n_g = cnt_ref[tile * ng + g]
                tab_ref[tile * TAB_W + g] = tab_ref[meta + ng + g] + fill_ref[g]
                tab_ref[tile * TAB_W + ng + g] = n_g
                fill_ref[g] = fill_ref[g] + n_g
            _run_copies(tab_ref, tile, sorted_hbm, sorted_ref.at[sub], sem_ref.at[sub], to_hbm=True, wait=False)

            @pl.when(i == n_steps - 1)
            def _():
                _wait_rows(_tile_rows(tab_ref, tile), sorted_hbm, sorted_ref.at[sub], sem_ref.at[sub], True)

        @pl.when(i == n_steps - 1)
        def _():
            _zero_fill(tab_ref, meta, zeros_ref, sorted_hbm, zsem_ref, n_rows, wait=False)
            _zero_fill(tab_ref, meta, zeros_ref, sorted_hbm, zsem_ref, n_rows, wait=True)


def _merged_specs(n_p_steps, n_s_steps, layer, sample_seq, step_of):
    def p_map(*idx):
        return (jnp.minimum(step_of(*idx), n_p_steps - 1), 0)

    def s_map(*idx):
        return (jnp.clip(step_of(*idx) - n_p_steps, 0, n_s_steps - 1), 0)

    def mod_map(*idx):
        j = step_of(*idx)
        row = jnp.where(j < n_p_steps, 0, 1 + lax.div(jnp.maximum(j - n_p_steps, 0) * ROUTE_STEP, sample_seq))
        return (layer, row, 0, 0)

    return (pl.BlockSpec((ROUTE_STEP, D_MODEL), p_map), pl.BlockSpec((ROUTE_STEP, D_MODEL), s_map),
            pl.BlockSpec((None, None, 6, D_MODEL), mod_map))


def _route(xp, xs, mod, g2, w_rt, b_rt, layer, sample_seq, n_rows):
    n_p_steps = xp.shape[0] // ROUTE_STEP
    n_s_steps = xs.shape[0] // ROUTE_STEP
    n_steps = n_p_steps + n_s_steps
    t = n_steps * ROUTE_STEP
    tm = ROUTE_TM
    upper = jnp.asarray(np.triu(np.ones((tm, tm), np.float32), 1), BF16)
    step_of = lambda s, i: jnp.where(s == 0, i, n_steps - 1)
    p_spec, s_spec, mod_spec = _merged_specs(n_p_steps, n_s_steps, layer, sample_seq, step_of)
    const = lambda a: pl.BlockSpec(a.shape, lambda s, i: (0,) * a.ndim)
    n_tiles = n_steps * ROUTE_SUBS
    return pl.pallas_call(
        functools.partial(_route_kernel, n_steps=n_steps, n_p_steps=n_p_steps),
        grid=(2, n_steps),
        in_specs=[p_spec, s_spec, mod_spec, const(g2), const(w_rt), const(b_rt), const(upper)],
        out_specs=[
            pl.BlockSpec((ROUTE_STEP, LANES), lambda s, i: (step_of(s, i), 0)),
            pl.BlockSpec(memory_space=pltpu.SMEM),
            pl.BlockSpec(memory_space=pl.ANY),
        ],
        out_shape=[
            jax.ShapeDtypeStruct((t, LANES), F32),
            jax.ShapeDtypeStruct(((n_tiles + 1) * TAB_W,), jnp.int32),
            jax.ShapeDtypeStruct((n_rows, XS_W), F32),
        ],
        scratch_shapes=[
            pltpu.VMEM((t, XS_W), BF16),
            pltpu.VMEM((n_tiles, 8, tm), F32),
            pltpu.VMEM((ROUTE_SUBS, ROUTE_ROWS, XS_W), F32),
            pltpu.VMEM((FFN_BM, XS_W), F32),
            pltpu.SMEM((n_tiles * N_GROUPS_MOE,), jnp.int32),
            pltpu.SMEM((N_GROUPS_MOE,), jnp.int32),
            pltpu.SemaphoreType.DMA((ROUTE_SUBS,)),
            pltpu.SemaphoreType.DMA(()),
        ],
        compiler_params=_cparams("arbitrary", "arbitrary"),
        name="moe_route",
    )(xp, xs, mod, g2, w_rt, b_rt, upper)


def _ffn_lookup(i, tab_ref, meta):
    fills = [tab_ref[meta + g] for g in range(N_GROUPS_MOE)]
    edges = []
    acc = 0
    for f in fills:
        acc = acc + lax.div(f + (FFN_BM - 1), FFN_BM)
        edges.append(acc)
    total = edges[-1]
    ii = jnp.minimum(i, total - 1)
    grp = sum((ii >= e).astype(jnp.int32) for e in edges[:-1])

    def pick(vals):
        return jnp.where(grp == 0, vals[0], jnp.where(grp == 1, vals[1], jnp.where(grp == 2, vals[2], vals[3])))

    first = pick([0] + edges[:-1])
    return grp, total, ii == first, pick(fills) - (ii - first) * FFN_BM


def _ffn_group_kernel(tab_ref, xs_ref, w1_ref, w3_ref, w2_ref, ys_ref, w1b_ref, w3b_ref, w2b_ref, *, meta):
    i = pl.program_id(0)
    _, total, first_of_group, valid = _ffn_lookup(i, tab_ref, meta)
    active = i < total

    @pl.when(jnp.logical_and(active, first_of_group))
    def _():
        w1b_ref[...] = w1_ref[...].astype(BF16)
        w3b_ref[...] = w3_ref[...].astype(BF16)
        w2b_ref[...] = w2_ref[...].astype(BF16)

    def run(rows):
        hb = xs_ref[0:rows, 0:D_MODEL].astype(BF16)
        ex = xs_ref[0:rows, D_MODEL:XS_W]
        acc = None
        for e in range(EXPERTS_PER_GROUP):
            a = _bdot(hb, w1b_ref[e])
            b = _bdot(hb, w3b_ref[e])
            gate = ex[:, e:e + 1] + ex[:, GATE_LO + e:GATE_LO + e + 1]
            hid = (a * jax.nn.sigmoid(a)) * b * gate
            part = _bdot(hid.astype(BF16), w2b_ref[e])
            acc = part if acc is None else acc + part
        ys_ref[0:rows, :] = acc

    @pl.when(jnp.logical_and(active, valid > FFN_HALF))
    def _():
        run(FFN_BM)

    @pl.when(jnp.logical_and(active, valid <= FFN_HALF))
    def _():
        run(FFN_HALF)
        ys_ref[FFN_HALF:, :] = jnp.zeros((FFN_BM - FFN_HALF, D_MODEL), F32)

    @pl.when(jnp.logical_not(active))
    def _():
        ys_ref[...] = jnp.zeros_like(ys_ref)


def _ffn_group(tab, xs, w1, w3, w2, layer, meta):
    e4 = EXPERTS_PER_GROUP
    n_blocks = xs.shape[0] // FFN_BM
    group_of = lambda i, tab_ref: _ffn_lookup(i, tab_ref, meta)[0]
    grid_spec = pltpu.PrefetchScalarGridSpec(
        num_scalar_prefetch=1,
        grid=(n_blocks,),
        in_specs=[
            pl.BlockSpec((FFN_BM, XS_W), lambda i, tab_ref: (i, 0)),
            pl.BlockSpec((None, e4, D_MODEL, D_EXPERT), lambda i, tab_ref: (layer, group_of(i, tab_ref), 0, 0)),
            pl.BlockSpec((None, e4, D_MODEL, D_EXPERT), lambda i, tab_ref: (layer, group_of(i, tab_ref), 0, 0)),
            pl.BlockSpec((None, e4, D_EXPERT, D_MODEL), lambda i, tab_ref: (layer, group_of(i, tab_ref), 0, 0)),
        ],
        out_specs=pl.BlockSpec((FFN_BM, D_MODEL), lambda i, tab_ref: (i, 0)),
        scratch_shapes=[
            pltpu.VMEM((e4, D_MODEL, D_EXPERT), BF16),
            pltpu.VMEM((e4, D_MODEL, D_EXPERT), BF16),
            pltpu.VMEM((e4, D_EXPERT, D_MODEL), BF16),
        ],
    )
    return pl.pallas_call(
        functools.partial(_ffn_group_kernel, meta=meta),
        grid_spec=grid_spec,
        out_shape=jax.ShapeDtypeStruct((xs.shape[0], D_MODEL), F32),
        compiler_params=_cparams("arbitrary"),
        name="moe_ffn",
    )(tab, xs, w1, w3, w2)


def _fetch_runs(tab_ref, ys_ref, runs_ref, sem_ref, tile, slot):
    _run_copies(tab_ref, tile, ys_ref, runs_ref.at[slot], sem_ref.at[slot], to_hbm=False, wait=False)


def _unsort(tab_ref, ys_ref, runs_ref, sem_ref, dloc_col, tile, slot):
    covered = _tile_rows(tab_ref, tile)
    _wait_rows(covered, ys_ref, runs_ref.at[slot], sem_ref.at[slot], False)
    runs_ref[slot, pl.ds(pl.multiple_of(covered, ROUTE_PAD), LANES), :] = jnp.zeros((LANES, D_MODEL), F32)
    yb = runs_ref[slot, 0:SORT_ROWS, :].astype(BF16)
    row_f = lax.broadcasted_iota(jnp.int32, (ROUTE_TM, SORT_ROWS), 1).astype(F32)
    return _bdot(jnp.where(row_f == dloc_col, 1.0, 0.0).astype(BF16), yb)


def _combine_kernel(tab_ref, *refs, n_steps, first_step, n_p_steps, final_norm):
    with_prompt = first_step < n_p_steps
    if with_prompt:
        xp_ref, xs_ref, mod_ref, dloc_ref, fg_ref, ys_ref, op_ref, os_ref, runs_ref, sem_ref = refs
    else:
        xs_ref, mod_ref, dloc_ref, fg_ref, ys_ref, os_ref, runs_ref, sem_ref = refs
    i = pl.program_id(0)
    par = lax.rem(i, 2)
    step = first_step + i
    tm = ROUTE_TM

    def fetch(st, parity):
        for sub in range(ROUTE_SUBS):
            _fetch_runs(tab_ref, ys_ref, runs_ref, sem_ref, st * ROUTE_SUBS + sub, parity * ROUTE_SUBS + sub)

    @pl.when(i == 0)
    def _():
        fetch(first_step, 0)

    @pl.when(i + 1 < n_steps)
    def _():
        fetch(step + 1, 1 - par)

    parts = [_unsort(tab_ref, ys_ref, runs_ref, sem_ref, dloc_ref[sub * tm:(sub + 1) * tm, 0:1],
                     step * ROUTE_SUBS + sub, par * ROUTE_SUBS + sub) for sub in range(ROUTE_SUBS)]
    is_prompt = step < n_p_steps
    x = jnp.where(is_prompt, xp_ref[...], xs_ref[...]) if with_prompt else xs_ref[...]
    x2 = x + mod_ref[5:6, :] * jnp.concatenate(parts, axis=0)
    if final_norm:
        x2 = _rms(x2, fg_ref[...])
    if with_prompt:
        @pl.when(is_prompt)
        def _():
            op_ref[...] = x2

        @pl.when(jnp.logical_not(is_prompt))
        def _():
            os_ref[...] = x2
    else:
        os_ref[...] = x2


def _combine(tab, xp, xs, mod, dloc, final_g, ys, layer, sample_seq, final_norm, with_prompt):
    n_p_steps = xp.shape[0] // ROUTE_STEP
    n_s_steps = xs.shape[0] // ROUTE_STEP
    first_step = 0 if with_prompt else n_p_steps
    n_steps = n_p_steps + n_s_steps - first_step
    step_of = lambda i, tab_ref: i + first_step
    p_spec, s_spec, mod_spec = _merged_specs(n_p_steps, n_s_steps, layer, sample_seq, step_of)
    streams = [p_spec, s_spec] if with_prompt else [s_spec]
    grid_spec = pltpu.PrefetchScalarGridSpec(
        num_scalar_prefetch=1,
        grid=(n_steps,),
        in_specs=streams + [
            mod_spec,
            pl.BlockSpec((ROUTE_STEP, LANES), lambda i, tab_ref: (i + first_step, 0)),
            pl.BlockSpec((1, D_MODEL), lambda i, tab_ref: (0, 0)),
            pl.BlockSpec(memory_space=pl.ANY),
        ],
        out_specs=streams,
        scratch_shapes=[
            pltpu.VMEM((2 * ROUTE_SUBS, RUN_ROWS, D_MODEL), F32),
            pltpu.SemaphoreType.DMA((2 * ROUTE_SUBS,)),
        ],
    )
    arrays = [xp, xs] if with_prompt else [xs]
    return pl.pallas_call(
        functools.partial(_combine_kernel, n_steps=n_steps, first_step=first_step, n_p_steps=n_p_steps,
                          final_norm=final_norm),
        grid_spec=grid_spec,
        out_shape=[jax.ShapeDtypeStruct(a.shape, F32) for a in arrays],
        compiler_params=_cparams("arbitrary"),
        name="moe_combine",
    )(tab, *arrays, mod, dloc, final_g, ys)


def _moe(xp, xs, mod, g2, w_r, b_r, w1, w3, w2, final_g, layer, sample_seq, final_norm, defer_prompt):
    t = xp.shape[0] + xs.shape[0]
    n_tiles = t // ROUTE_TM
    max_rows = t + N_GROUPS_MOE * (ROUTE_PAD - 1) * n_tiles
    n_rows = (-(-max_rows // FFN_BM) + N_GROUPS_MOE) * FFN_BM
    dloc, tab, sorted_x = _route(xp, xs, mod, g2, w_r, b_r, layer, sample_seq, n_rows)
    ys = _ffn_group(tab, sorted_x, w1, w3, w2, layer, n_tiles * TAB_W)
    out = _combine(tab, xp, xs, mod, dloc, final_g, ys, layer, sample_seq, final_norm, not defer_prompt)
    if defer_prompt:
        return None, out[0], (tab, dloc, ys)
    return out[0], out[1], None


def _rope_tables(seq):
    half = QK_ROPE // 2
    nf = half // 2
    inv = ROPE_BASE ** (-np.arange(nf, dtype=np.float64) / nf)
    pos = np.arange(seq)
    row = (pos // GRID_W).astype(np.float64)
    col = (pos % GRID_W).astype(np.float64)
    cos = np.ones((seq, HEAD_PAD), np.float64)
    sin_a = np.zeros((seq, HEAD_PAD), np.float64)
    sin_b = np.zeros((seq, HEAD_PAD), np.float64)
    for part, p in enumerate((row, col)):
        ang = p[:, None] * inv[None, :]
        base = ROPE_OFF + part * half
        cos[:, base:base + nf] = np.cos(ang)
        cos[:, base + nf:base + half] = np.cos(ang)
        sin_a[:, base:base + nf] = -np.sin(ang)
        sin_b[:, base + nf:base + half] = np.sin(ang)
    return tuple(jnp.asarray(a, F32) for a in (cos, sin_a, sin_b))


def _apply_rope(x, cos, sin_a, sin_b, reps):
    nf = QK_ROPE // 4
    width = x.shape[1]
    if reps > 1:
        cos, sin_a, sin_b = (jnp.concatenate([a] * reps, axis=1) for a in (cos, sin_a, sin_b))
    return x * cos + pltpu.roll(x, width - nf, axis=1) * sin_a + pltpu.roll(x, nf, axis=1) * sin_b


def _odd_in_kernel(*refs, rope, emit_cache):
    x_ref, mod_ref, g_ref, w_ref, qg_ref, wq_ref, kg_ref, wk_ref, wv_ref, cs_ref = refs[:10]
    refs = refs[10:]
    if rope:
        cos_ref, sa_ref, sb_ref = refs[:3]
        refs = refs[3:]
    y_ref, q_ref, k_ref, v_ref = refs[:4]
    refs = refs[4:]
    m = mod_ref[...]
    h = _rms(x_ref[...], g_ref[...]) * (1.0 + m[1:2]) + m[0:1]
    z = _bdot(h.astype(BF16), w_ref[...])
    zc = z[:, 0:D_C]
    qc = z[:, D_C:D_C + Q_LORA]
    kvc = z[:, D_C + Q_LORA:D_C + Q_LORA + KV_LORA]
    kpe = z[:, D_C + Q_LORA + KV_LORA:]
    q = _bdot(_rms(qc, qg_ref[...]).astype(BF16), wq_ref[...])
    kvn = _rms(kvc, kg_ref[...])
    if emit_cache:
        ckv_ref, kpe_ref = refs
        ckv_ref[...] = kvn
        kpe_ref[...] = kpe[:, ROPE_OFF:ROPE_OFF + QK_ROPE]
    if rope:
        tabs = (cos_ref[...], sa_ref[...], sb_ref[...])
        q = _apply_rope(q, *tabs, reps=N_HEADS)
        kpe = _apply_rope(kpe, *tabs, reps=1)
    kvb = kvn.astype(BF16)
    k = _bdot(kvb, wk_ref[...]) + jnp.concatenate([kpe] * N_HEADS, axis=1)
    scale = math.log2(math.e) / math.sqrt(QK_NOPE + QK_ROPE)
    q_ref[...] = (q * scale).astype(BF16)
    k_ref[...] = k.astype(BF16)
    v_ref[...] = _bdot(kvb, wv_ref[...]).astype(BF16)
    y = _bdot(zc.astype(BF16), cs_ref[...])
    y_ref[0, :, :] = y[:, 0:D_C].astype(BF16)
    y_ref[1, :, :] = y[:, D_C:2 * D_C].astype(BF16)


def _odd_in(x3, mod, g, w_in, q_g, w_q, kv_g, w_k, w_v, cs, stream, layer, rope_tabs, emit_cache):
    b, s, _ = x3.shape
    tm = min(s, 512)
    n_i = s // tm
    rope = rope_tabs is not None
    const = lambda a: pl.BlockSpec(a.shape, lambda bi, i: (0,) * a.ndim)
    in_specs = [
        pl.BlockSpec((None, tm, D_MODEL), lambda bi, i: (bi, i, 0)),
        _mod_spec(stream, layer),
        const(g), const(w_in), const(q_g), const(w_q), const(kv_g), const(w_k), const(w_v), const(cs),
    ]
    args = [x3, mod, g, w_in, q_g, w_q, kv_g, w_k, w_v, cs]
    if rope:
        in_specs += [pl.BlockSpec((tm, HEAD_PAD), lambda bi, i: (i, 0))] * 3
        args += list(rope_tabs)
    hq = N_HEADS * HEAD_PAD
    out_specs = [
        pl.BlockSpec((None, 2, tm, D_C), lambda bi, i: (bi, 0, i, 0)),
        pl.BlockSpec((None, tm, hq), lambda bi, i: (bi, i, 0)),
        pl.BlockSpec((None, tm, hq), lambda bi, i: (bi, i, 0)),
        pl.BlockSpec((None, tm, N_HEADS * V_DIM), lambda bi, i: (bi, i, 0)),
    ]
    out_shape = [
        jax.ShapeDtypeStruct((b, 2, s, D_C), BF16),
        jax.ShapeDtypeStruct((b, s, hq), BF16),
        jax.ShapeDtypeStruct((b, s, hq), BF16),
        jax.ShapeDtypeStruct((b, s, N_HEADS * V_DIM), BF16),
    ]
    if emit_cache:
        out_specs += [
            pl.BlockSpec((None, tm, KV_LORA), lambda bi, i: (bi, i, 0)),
            pl.BlockSpec((None, tm, QK_ROPE), lambda bi, i: (bi, i, 0)),
        ]
        out_shape += [
            jax.ShapeDtypeStruct((b, s, KV_LORA), F32),
            jax.ShapeDtypeStruct((b, s, QK_ROPE), F32),
        ]
    return pl.pallas_call(
        functools.partial(_odd_in_kernel, rope=rope, emit_cache=emit_cache),
        grid=(b, n_i),
        in_specs=in_specs,
        out_specs=out_specs,
        out_shape=out_shape,
        compiler_params=_cparams("parallel", "parallel"),
        name="odd_in",
    )(*args)


def _cache_kv_kernel(c_ref, p_ref, wk_ref, wv_ref, k_ref, v_ref):
    cb = c_ref[...].astype(BF16)
    k = _bdot(cb, wk_ref[...]) + jnp.concatenate([p_ref[...]] * N_HEADS, axis=1)
    k_ref[...] = k.astype(BF16)
    v_ref[...] = _bdot(cb, wv_ref[...]).astype(BF16)


def _cache_kv(ckv, kpe_blk, w_k, w_v):
    b, p, _ = ckv.shape
    hq = N_HEADS * HEAD_PAD
    return pl.pallas_call(
        _cache_kv_kernel,
        grid=(b,),
        in_specs=[
            pl.BlockSpec((None, p, KV_LORA), lambda bi: (bi, 0, 0)),
            pl.BlockSpec((None, p, HEAD_PAD), lambda bi: (bi, 0, 0)),
            pl.BlockSpec(w_k.shape, lambda bi: (0, 0)),
            pl.BlockSpec(w_v.shape, lambda bi: (0, 0)),
        ],
        out_specs=[
            pl.BlockSpec((None, p, hq), lambda bi: (bi, 0, 0)),
            pl.BlockSpec((None, p, N_HEADS * V_DIM), lambda bi: (bi, 0, 0)),
        ],
        out_shape=[
            jax.ShapeDtypeStruct((b, p, hq), BF16),
            jax.ShapeDtypeStruct((b, p, N_HEADS * V_DIM), BF16),
        ],
        compiler_params=_cparams("parallel"),
        name="cache_kv",
    )(ckv, kpe_blk, w_k, w_v)


def _odd_mix_kernel(*refs, with_cache):
    q_ref, k_ref, v_ref = refs[:3]
    refs = refs[3:]
    if with_cache:
        kc_ref, vc_ref = refs[:2]
        refs = refs[2:]
    if with_cache:
        y_ref, f_ref, x_ref, mod_ref, wo_ref, o_ref, a_ref, vx_ref, vcx_ref = refs

        @pl.when(pl.program_id(1) == 0)
        def _():
            for src, dst in ((v_ref, vx_ref), (vc_ref, vcx_ref)):
                one = lax.broadcasted_iota(jnp.int32, (src.shape[0], PAIR_W - DEN_COL), 1) == 0
                for pair in range(N_HEADS // 2):
                    dst[:, pair * PAIR_W:pair * PAIR_W + DEN_COL] = src[:, pair * DEN_COL:(pair + 1) * DEN_COL]
                    dst[:, pair * PAIR_W + DEN_COL:(pair + 1) * PAIR_W] = jnp.where(one, 1.0, 0.0).astype(BF16)
    else:
        y_ref, f_ref, x_ref, mod_ref, wo_ref, o_ref, a_ref = refs
    tq = q_ref.shape[0]
    lane = lax.broadcasted_iota(jnp.int32, (tq, 2 * V_DIM), 1)
    for pair in range(N_HEADS // 2):
        vcols = slice(pair * PAIR_W, (pair + 1) * PAIR_W)
        outs = []
        for h in (2 * pair, 2 * pair + 1):
            hcols = slice(h * HEAD_PAD, (h + 1) * HEAD_PAD)
            qh = q_ref[:, hcols]
            s = lax.dot_general(qh, k_ref[:, hcols], NT_DIMS, preferred_element_type=F32)
            top = jnp.max(s, axis=-1, keepdims=True)
            if with_cache:
                sc = lax.dot_general(qh, kc_ref[:, hcols], NT_DIMS, preferred_element_type=F32)
                top = jnp.maximum(top, jnp.max(sc, axis=-1, keepdims=True))
            if with_cache:
                acc = _bdot(jnp.exp2((s - top).astype(BF16)), vx_ref[:, vcols])
                acc = acc + _bdot(jnp.exp2((sc - top).astype(BF16)), vcx_ref[:, vcols])
                outs.append(acc[:, 0:2 * V_DIM] / acc[:, DEN_COL:DEN_COL + 1])
            else:
                p = jnp.exp2(s - top)
                den = jnp.sum(p, axis=-1, keepdims=True)
                outs.append(_bdot(p.astype(BF16), v_ref[:, pair * DEN_COL:(pair + 1) * DEN_COL]) / den)
        a_ref[:, pair * 2 * V_DIM:(pair + 1) * 2 * V_DIM] = jnp.where(lane < V_DIM, outs[0], outs[1]).astype(BF16)
    f = _bdot(f_ref[...], y_ref[...].reshape(-1, D_C))
    o = _bdot(f.astype(BF16), wo_ref[0:D_C, :]) + _bdot(a_ref[...], wo_ref[D_C:, :])
    o_ref[...] = x_ref[...] + mod_ref[2:3, :] * o


def _odd_whole_kernel(tab_ref, x_ref, mod_ref, g_ref, w_ref, qg_ref, wq_ref, kg_ref, wk_ref, wv_ref, cs_ref, f_ref,
                      wo_ref, modp_ref, dloc_ref, ys_ref, o_ref, ckv_ref, kpe_ref,
                      y_ref, q_ref, k_ref, v_ref, a_ref, x2_ref, runs_ref, sem_ref):
    bi = pl.program_id(0)
    par = lax.rem(bi, 2)

    @pl.when(bi == 0)
    def _():
        _fetch_runs(tab_ref, ys_ref, runs_ref, sem_ref, 0, 0)

    @pl.when(bi + 1 < pl.num_programs(0))
    def _():
        _fetch_runs(tab_ref, ys_ref, runs_ref, sem_ref, bi + 1, 1 - par)

    moe = _unsort(tab_ref, ys_ref, runs_ref, sem_ref, dloc_ref[:, 0:1], bi, par)
    x2_ref[...] = x_ref[...] + modp_ref[5:6, :] * moe
    _odd_in_kernel(x2_ref, mod_ref, g_ref, w_ref, qg_ref, wq_ref, kg_ref, wk_ref, wv_ref, cs_ref,
                   y_ref, q_ref, k_ref, v_ref, ckv_ref, kpe_ref, rope=False, emit_cache=True)
    _odd_mix_kernel(q_ref, k_ref, v_ref, y_ref, f_ref, x2_ref, mod_ref, wo_ref, o_ref, a_ref, with_cache=False)


def _odd_whole(x3, pending, mod, g, w_in, q_g, w_q, kv_g, w_k, w_v, cs, fmat, w_out, stream, layer):
    tab, dloc, ys = pending
    b, s, _ = x3.shape
    assert s == ROUTE_TM, "one sequence must be one MoE sort tile"
    hq = N_HEADS * HEAD_PAD
    hv = N_HEADS * V_DIM
    const = lambda a: pl.BlockSpec(a.shape, lambda bi, tab_ref: (0,) * a.ndim)
    row_block = lambda w: pl.BlockSpec((None, s, w), lambda bi, tab_ref: (bi, 0, 0))
    mod_block = lambda lyr: pl.BlockSpec((None, None, 6, D_MODEL),
                                         lambda bi, tab_ref: (lyr, stream.row_of_batch(bi), 0, 0))
    grid_spec = pltpu.PrefetchScalarGridSpec(
        num_scalar_prefetch=1,
        grid=(b,),
        in_specs=[
            row_block(D_MODEL), mod_block(layer),
            const(g), const(w_in), const(q_g), const(w_q), const(kv_g), const(w_k), const(w_v), const(cs),
            const(fmat), const(w_out),
            mod_block(layer - 1),
            pl.BlockSpec((s, LANES), lambda bi, tab_ref: (bi, 0)),
            pl.BlockSpec(memory_space=pl.ANY),
        ],
        out_specs=[row_block(D_MODEL), row_block(KV_LORA), row_block(QK_ROPE)],
        scratch_shapes=[
            pltpu.VMEM((2, s, D_C), BF16),
            pltpu.VMEM((s, hq), BF16),
            pltpu.VMEM((s, hq), BF16),
            pltpu.VMEM((s, hv), BF16),
            pltpu.VMEM((s, hv), BF16),
            pltpu.VMEM((s, D_MODEL), F32),
            pltpu.VMEM((2, RUN_ROWS, D_MODEL), F32),
            pltpu.SemaphoreType.DMA((2,)),
        ],
    )
    return pl.pallas_call(
        _odd_whole_kernel,
        grid_spec=grid_spec,
        out_shape=[
            jax.ShapeDtypeStruct((b, s, D_MODEL), F32),
            jax.ShapeDtypeStruct((b, s, KV_LORA), F32),
            jax.ShapeDtypeStruct((b, s, QK_ROPE), F32),
        ],
        compiler_params=_cparams("arbitrary"),
        name="odd_whole",
    )(tab, x3, mod, g, w_in, q_g, w_q, kv_g, w_k, w_v, cs, fmat, w_out, mod, dloc, ys)


def _odd_mix(q, k, v, kc, vc, y, fmat, x3, mod, w_out, stream, layer):
    b, s, hq = q.shape
    tq = min(s, 256)
    n_i = s // tq
    with_cache = kc is not None
    hv = N_HEADS * V_DIM
    mode = dict(pipeline_mode=pl.Buffered(1)) if with_cache else {}

    def per_batch(rows, cols):
        return pl.BlockSpec((None, rows, cols), lambda bi, i: (bi, 0, 0), **mode)

    in_specs = [pl.BlockSpec((None, tq, hq), lambda bi, i: (bi, i, 0)), per_batch(s, hq), per_batch(s, hv)]
    args = [q, k, v]
    if with_cache:
        p = kc.shape[1]
        in_specs += [per_batch(p, hq), per_batch(p, hv)]
        args += [kc, vc]
    in_specs += [
        per_batch(2 * s, D_C),
        pl.BlockSpec((tq, 2 * s), lambda bi, i: (i, 0)),
        pl.BlockSpec((None, tq, D_MODEL), lambda bi, i: (bi, i, 0)),
        _mod_spec(stream, layer),
        pl.BlockSpec(w_out.shape, lambda bi, i: (0, 0), **mode),
    ]
    args += [y, fmat, x3, mod, w_out]
    return pl.pallas_call(
        functools.partial(_odd_mix_kernel, with_cache=with_cache),
        grid=(b, n_i),
        in_specs=in_specs,
        out_specs=pl.BlockSpec((None, tq, D_MODEL), lambda bi, i: (bi, i, 0)),
        out_shape=jax.ShapeDtypeStruct((b, s, D_MODEL), F32),
        scratch_shapes=[pltpu.VMEM((tq, hv), BF16)] + (
            [pltpu.VMEM((s, VX_W), BF16), pltpu.VMEM((kc.shape[1], VX_W), BF16)] if with_cache else []),
        compiler_params=_cparams("parallel", "arbitrary"),
        name="odd_mix",
    )(*args)


def _dft_tables(seq):
    jc = np.arange(C_GW)
    ang_c = 2.0 * np.pi * np.outer(jc, jc) / C_GW
    eye = np.eye(C_GROUPS)
    cs = np.concatenate([np.kron(eye, np.cos(ang_c)), np.kron(eye, np.sin(ang_c))], axis=1)
    jn = np.arange(seq)
    ang_n = 2.0 * np.pi * (np.outer(jn, jn) % seq) / seq
    scale = 1.0 / math.sqrt(seq * C_GW)
    fmat = np.concatenate([np.cos(ang_n), -np.sin(ang_n)], axis=1) * scale
    return jnp.asarray(cs, F32).astype(BF16), jnp.asarray(fmat, F32).astype(BF16)


def _odd_weights(w_in, w_uq, w_ukv):
    d = w_in.shape[0]
    base = D_C + Q_LORA + KV_LORA
    kpe_blk = jnp.zeros((d, HEAD_PAD), w_in.dtype).at[:, ROPE_OFF:ROPE_OFF + QK_ROPE].set(w_in[:, base:])
    w_in_p = jnp.concatenate([w_in[:, :base], kpe_blk], axis=1).astype(BF16)
    qh = w_uq.reshape(Q_LORA, N_HEADS, QK_NOPE + QK_ROPE)
    w_q = jnp.pad(qh, ((0, 0), (0, 0), (0, HEAD_PAD - QK_NOPE - QK_ROPE))).reshape(Q_LORA, -1).astype(BF16)
    kvh = w_ukv.reshape(KV_LORA, N_HEADS, QK_NOPE + V_DIM)
    w_k = jnp.pad(kvh[:, :, :QK_NOPE], ((0, 0), (0, 0), (0, HEAD_PAD - QK_NOPE))).reshape(KV_LORA, -1)
    w_v = kvh[:, :, QK_NOPE:].reshape(KV_LORA, -1)
    return w_in_p, w_q, w_k.astype(BF16), w_v.astype(BF16)


ROUTER_ROWS = 32


def _router_weights(wg, bg, we, be):
    d = wg.shape[0]
    w = jnp.concatenate([wg, we.reshape(d, N_EXPERTS)], axis=1).T
    w = jnp.pad(w, ((0, ROUTER_ROWS - w.shape[0]), (0, 0))).astype(BF16)
    b = jnp.concatenate([bg, be.reshape(N_EXPERTS)])
    b = jnp.pad(b, (0, ROUTER_ROWS - b.shape[0])).reshape(ROUTER_ROWS, 1).astype(F32)
    return w, b


def kernel(x_prompt, x_sample, cache_ckv, cache_kpe, c, c_ctx, mod_w, mod_b, norm1_g, norm2_g,
           ev_w_in, ev_conv_w, ev_sgu_norm_g, ev_sgu_w, ev_sgu_b, ev_w_out,
           od_w_in, od_q_norm_g, od_w_uq, od_kv_norm_g, od_w_ukv, od_w_out,
           moe_wg, moe_bg, moe_we, moe_be, moe_w1, moe_w3, moe_w2, final_norm_g):
    bp, n_p, d = x_prompt.shape
    bs, n_s, _ = x_sample.shape
    streams = [(_Stream(bp, n_p, True), x_prompt), (_Stream(bs, n_s, False), x_sample)]

    n_rows = 1 + bs
    cond_t = jnp.concatenate([c_ctx[None, :], c], axis=0).T
    mod = _adaln(cond_t, mod_w, mod_b, n_rows)

    final_g = final_norm_g.reshape(1, d)
    xs = [x for _, x in streams]
    new_ckv, new_kpe = [], []
    pending = None
    for l in range(DEPTH):
        j = l // 2
        g1 = norm1_g[l].reshape(1, d)
        g2 = norm2_g[l].reshape(1, d)
        w_r, b_r = _router_weights(moe_wg[l], moe_bg[l], moe_we[l], moe_be[l])
        last = l == DEPTH - 1
        if l % 2 == 0:
            w_in = ev_w_in[j].astype(BF16)
            w_out = ev_w_out[j].astype(BF16)
            sgu_w = ev_sgu_w[j].astype(BF16)
            sgu_g = ev_sgu_norm_g[j].reshape(1, D_B)
            sgu_bias = jnp.repeat(ev_sgu_b[j].T, D_B // B_GROUPS, axis=1)
            for si, (st, _) in enumerate(streams):
                xs[si] = _even_layer(xs[si], mod, g1, w_in, ev_conv_w[j], sgu_g, sgu_w, sgu_bias, w_out, st, l)
        else:
            w_in, w_q, w_k, w_v = _odd_weights(od_w_in[j], od_w_uq[j], od_w_ukv[j])
            w_out = od_w_out[j].astype(BF16)
            q_g = od_q_norm_g[j].reshape(1, Q_LORA)
            kv_g = od_kv_norm_g[j].reshape(1, KV_LORA)
            for si, (st, _) in enumerate(streams):
                x3 = xs[si]
                cs, fmat = _dft_tables(st.seq)
                if st.shared_cond:
                    xs[si], ckv, kpe = _odd_whole(x3, pending, mod, g1, w_in, q_g, w_q, kv_g, w_k, w_v, cs, fmat,
                                                  w_out, st, l)
                    new_ckv.append(ckv)
                    new_kpe.append(kpe)
                    continue
                y, q, k, v = _odd_in(x3, mod, g1, w_in, q_g, w_q, kv_g, w_k, w_v, cs, st, l,
                                     _rope_tables(st.seq), False)
                kpe_blk = jnp.pad(cache_kpe[:, j], ((0, 0), (0, 0), (ROPE_OFF, HEAD_PAD - ROPE_OFF - QK_ROPE)))
                kc, vc = _cache_kv(cache_ckv[:, j], kpe_blk, w_k, w_v)
                xs[si] = _odd_mix(q, k, v, kc, vc, y.reshape(st.batch, 2 * st.seq, D_C), fmat, x3, mod,
                                  w_out, st, l)
        defer = not last and (l + 1) % 2 == 1
        x2p, x2s, pending = _moe(xs[0].reshape(bp * n_p, d), xs[1].reshape(bs * n_s, d), mod, g2, w_r, b_r,
                                 moe_w1, moe_w3, moe_w2, final_g, l, n_s, last, defer)
        xs = [xs[0] if defer else x2p.reshape(bp, n_p, d), x2s.reshape(bs, n_s, d)]
    return (xs[0], xs[1], jnp.stack(new_ckv, axis=1), jnp.stack(new_kpe, axis=1))
```

```python
import functools
import math

import numpy as np
import jax
import jax.numpy as jnp
from jax import lax
from jax.experimental import pallas as pl
from jax.experimental.pallas import tpu as pltpu

D_MODEL = 1024
DEPTH = 2
GRID_W = 64
D_A = D_MODEL // 2
D_B = D_MODEL // 2
B_GROUPS = 4
CHUNK = 128
D_EVEN_IN = 3 * D_A + 2 * D_B
D_C = D_MODEL // 4
C_GROUPS = 4
C_GW = D_C // C_GROUPS
N_HEADS = 12
QK_NOPE = 64
QK_ROPE = 32
V_DIM = 64
Q_LORA = 384
KV_LORA = 256
ROPE_BASE = 10000.0
N_GROUPS_MOE = 4
EXPERTS_PER_GROUP = 4
N_EXPERTS = N_GROUPS_MOE * EXPERTS_PER_GROUP
D_EXPERT = 256
EPS = 1e-6

LANES = 128
HEAD_PAD = 128
PAIR_W = 256
DEN_COL = 2 * 64
VX_W = 6 * PAIR_W
ROPE_OFF = QK_NOPE
GATE_OFF = N_GROUPS_MOE
NEG_BIG = -1e30
F32 = jnp.float32
BF16 = jnp.bfloat16
VMEM_LIMIT = 56 * 1024 * 1024


def _cparams(*sem):
    return pltpu.CompilerParams(dimension_semantics=sem, vmem_limit_bytes=VMEM_LIMIT)


def _rms(x, g):
    return x * lax.rsqrt(jnp.mean(x * x, axis=-1, keepdims=True) + EPS) * g


def _bdot(a, b):
    return jnp.dot(a, b, preferred_element_type=F32)


NT_DIMS = (((1,), (1,)), ((), ()))
TN_DIMS = (((0,), (0,)), ((), ()))


def _mod_kernel(ct_ref, w_ref, b_ref, o_ref):
    c = ct_ref[...]
    s = c * jax.nn.sigmoid(c)
    w = w_ref[...]
    b = b_ref[...]
    for r in range(o_ref.shape[0]):
        o_ref[r:r + 1, :] = jnp.sum(s[:, r:r + 1] * w, axis=0, keepdims=True) + b


def _adaln(cond_t, mod_w, mod_b, n_rows):
    nt = 2048
    d6 = mod_w.shape[-1]
    out = pl.pallas_call(
        _mod_kernel,
        grid=(DEPTH, d6 // nt),
        in_specs=[
            pl.BlockSpec(cond_t.shape, lambda l, n: (0, 0)),
            pl.BlockSpec((None, D_MODEL, nt), lambda l, n: (l, 0, n)),
            pl.BlockSpec((None, 1, nt), lambda l, n: (l, 0, n)),
        ],
        out_specs=pl.BlockSpec((None, n_rows, nt), lambda l, n: (l, 0, n)),
        out_shape=jax.ShapeDtypeStruct((DEPTH, n_rows, d6), F32),
        compiler_params=_cparams("parallel", "parallel"),
        name="adaln",
    )(cond_t, mod_w, mod_b.reshape(DEPTH, 1, d6))
    return out.reshape(DEPTH, n_rows, 6, D_MODEL)


class _Stream:
    def __init__(self, batch, seq, shared_cond):
        self.batch = batch
        self.seq = seq
        self.tokens = batch * seq
        self.shared_cond = shared_cond

    def row_of_batch(self, b):
        return 0 if self.shared_cond else b + 1

    def row_of_tile(self, i, tm):
        return 0 if self.shared_cond else (i * tm) // self.seq + 1


def _mod_spec(stream, layer, tm=None):
    if tm is None:
        return pl.BlockSpec((None, None, 6, D_MODEL), lambda b, i: (layer, stream.row_of_batch(b), 0, 0))
    return pl.BlockSpec((None, None, 6, D_MODEL), lambda i, *_: (layer, stream.row_of_tile(i, tm), 0, 0))


HALO = 8


def _even_kernel(*refs, has_halo, seq_rows, nc):
    if has_halo:
        x_ref, xp_ref, xn_ref = refs[:3]
        refs = refs[3:]
    else:
        x_ref = refs[0]
        refs = refs[1:]
    mod_ref, g_ref, wi_ref, cw_ref, sg_ref, sw_ref, sb_ref, wo_ref = refs[:8]
    plan_in, (o_ref, *plan_out), (z_ref, y_ref) = refs[8:12], refs[12:17], refs[17:]
    i = pl.program_id(1)
    n_i = pl.num_programs(1)
    ts = x_ref.shape[0]
    m = mod_ref[...]
    g = g_ref[...]

    def modulate(x):
        return (_rms(x, g) * (1.0 + m[1:2]) + m[0:1]).astype(BF16)

    x = x_ref[...]
    hb = modulate(x)
    for n in range(D_EVEN_IN // nc):
        z_ref[:, n * nc:(n + 1) * nc] = _bdot(hb, wi_ref[:, n * nc:(n + 1) * nc]).astype(BF16)

    gate_b = z_ref[:, 0:D_A].astype(F32)
    gate_c = z_ref[:, D_A:2 * D_A].astype(F32)
    xa = z_ref[:, 2 * D_A:3 * D_A].astype(F32)
    t = gate_c * xa
    t_prev = pltpu.roll(t, 1, axis=0)
    t_next = pltpu.roll(t, ts - 1, axis=0)
    row = lax.broadcasted_iota(jnp.int32, (ts, 1), 0) & (seq_rows - 1)
    if has_halo:
        hh = modulate(jnp.concatenate([xp_ref[...], xn_ref[...]], axis=0))
        zh = _bdot(hh, wi_ref[:, D_A:3 * D_A]).astype(BF16).astype(F32)
        th = zh[:, 0:D_A] * zh[:, D_A:2 * D_A]
        tp = th[HALO - 1:HALO] * (i > 0).astype(F32)
        tn = th[HALO:HALO + 1] * (i < n_i - 1).astype(F32)
    else:
        tp = tn = 0.0
    t_prev = jnp.where(row == 0, tp, t_prev)
    t_next = jnp.where(row == seq_rows - 1, tn, t_next)
    cw = cw_ref[...]
    y_a = gate_b * (t_prev * cw[0:1] + t * cw[1:2] + t_next * cw[2:3])
    y_ref[:, 0:D_A] = y_a.astype(BF16)

    u = z_ref[:, 3 * D_A:3 * D_A + D_B].astype(F32)
    v = z_ref[:, 3 * D_A + D_B:3 * D_A + 2 * D_B].astype(F32)
    vb = _rms(v, sg_ref[...]).astype(BF16)
    gw = D_B // B_GROUPS
    for c in range(ts // CHUNK):
        rows = slice(c * CHUNK, (c + 1) * CHUNK)
        for gi in range(B_GROUPS):
            cols = slice(gi * gw, (gi + 1) * gw)
            sv = _bdot(sw_ref[gi], vb[rows, cols]) + sb_ref[:, cols]
            y_ref[rows, D_A + gi * gw:D_A + (gi + 1) * gw] = (u[rows, cols] * sv).astype(BF16)

    x1 = x + m[2:3] * _bdot(y_ref[...], wo_ref[...])
    o_ref[...] = x1
    _route_plan(x1, m, plan_in, plan_out, (pl.program_id(0) * n_i + i) * (ts // ROUTE_TM))


def _even_layer(x3, mod, g, w_in, conv_w, sgu_g, sgu_w, sgu_bias, w_out, plan_consts, stream, layer):
    out_shape = x3.shape
    b, s, _ = x3.shape
    ts = min(s, 256)
    n_i = s // ts
    has_halo = n_i > 1
    seq_rows = ts
    if not has_halo and stream.shared_cond and b % 2 == 0:
        b, s, ts = b // 2, 2 * s, 2 * ts
        x3 = x3.reshape(b, s, D_MODEL)
    hb = ts // HALO
    last_h = s // HALO - 1
    const = lambda a: pl.BlockSpec(a.shape, lambda bi, i: (0,) * a.ndim)
    in_specs = [pl.BlockSpec((None, ts, D_MODEL), lambda bi, i: (bi, i, 0))]
    args = [x3]
    if has_halo:
        in_specs += [
            pl.BlockSpec((None, HALO, D_MODEL), lambda bi, i: (bi, jnp.maximum(i * hb - 1, 0), 0)),
            pl.BlockSpec((None, HALO, D_MODEL), lambda bi, i: (bi, jnp.minimum((i + 1) * hb, last_h), 0)),
        ]
        args += [x3, x3]
    in_specs += [_mod_spec(stream, layer), const(g), const(w_in), const(conv_w), const(sgu_g), const(sgu_w),
                 const(sgu_bias), const(w_out)] + [const(a) for a in plan_consts]
    args += [mod, g, w_in, conv_w, sgu_g, sgu_w, sgu_bias, w_out] + plan_consts
    plan_shapes, plan_specs = _plan_out(b, s, ts, lambda bi, i: (bi, i))
    out = pl.pallas_call(
        functools.partial(_even_kernel, has_halo=has_halo, seq_rows=seq_rows, nc=512),
        grid=(b, n_i),
        in_specs=in_specs,
        out_specs=[pl.BlockSpec((None, ts, D_MODEL), lambda bi, i: (bi, i, 0))] + plan_specs,
        out_shape=[jax.ShapeDtypeStruct((b, s, D_MODEL), F32)] + plan_shapes,
        scratch_shapes=[pltpu.VMEM((ts, D_EVEN_IN), BF16), pltpu.VMEM((ts, D_A + D_B), BF16)],
        compiler_params=_cparams("arbitrary", "arbitrary"),
        name="even_layer",
    )(*args)
    return out[0].reshape(out_shape), out[1:]


ROUTE_TM = 256
ROUTE_SUBS = 2
ROUTE_STEP = ROUTE_TM * ROUTE_SUBS
ROUTE_PAD = 8
SORT_ROWS = ROUTE_TM + LANES
RUN_ROWS = SORT_ROWS + 32
ROUTE_ROWS = ROUTE_TM + 4 * ROUTE_PAD
XS_W = D_MODEL + LANES
GATE_LO = EXPERTS_PER_GROUP
DLOC_HI = 2 * EXPERTS_PER_GROUP
DLOC_RADIX = 16.0
FFN_BM = 512
FFN_HALF = FFN_BM // 2
RUN_SIZES = (256, 128, 64, 32, 16, 8)
TAB_W = 2 * N_GROUPS_MOE


def _round_up(x, m):
    return lax.div(x + (m - 1), m) * m


def _run_copies(tab_ref, tile, hbm_ref, vmem_ref, sem, to_hbm, wait):
    off = 0
    for g in range(N_GROUPS_MOE):
        start = tab_ref[tile * TAB_W + g]
        n = tab_ref[tile * TAB_W + N_GROUPS_MOE + g]
        for p in RUN_SIZES:
            done = n & (-2 * p)

            @pl.when((n & p) != 0)
            def _():
                v = vmem_ref.at[pl.ds(pl.multiple_of(off + done, ROUTE_PAD), p)]
                h = hbm_ref.at[pl.ds(pl.multiple_of(start + done, ROUTE_PAD), p)]
                cp = pltpu.make_async_copy(v, h, sem) if to_hbm else pltpu.make_async_copy(h, v, sem)
                if wait:
                    cp.wait()
                else:
                    cp.start()
        off = off + n


def _zero_fill(tab_ref, meta, zeros_ref, hbm_ref, sem, n_rows, wait):
    def copy(rows, dst_row):
        cp = pltpu.make_async_copy(zeros_ref.at[pl.ds(0, rows)],
                                   hbm_ref.at[pl.ds(pl.multiple_of(dst_row, ROUTE_PAD), rows)], sem)
        if wait:
            cp.wait()
        else:
            cp.start()

    end = 0
    for g in range(N_GROUPS_MOE):
        fill = tab_ref[meta + g]
        start = tab_ref[meta + N_GROUPS_MOE + g]
        end = start + _round_up(fill, FFN_BM)
        tail = end - start - fill
        for p in RUN_SIZES:
            pl.when((tail & p) != 0)(functools.partial(copy, p, start + fill + (tail & (-2 * p))))
    for k in range(n_rows // FFN_BM):
        pl.when(end + k * FFN_BM < n_rows)(functools.partial(copy, FFN_BM, end + k * FFN_BM))


def _wait_rows(n, hbm_ref, vmem_ref, sem, to_hbm):
    for p in RUN_SIZES:
        @pl.when((n & p) != 0)
        def _():
            v = vmem_ref.at[pl.ds(0, p)]
            h = hbm_ref.at[pl.ds(0, p)]
            (pltpu.make_async_copy(v, h, sem) if to_hbm else pltpu.make_async_copy(h, v, sem)).wait()


def _tile_rows(tab_ref, tile):
    n = 0
    for g in range(N_GROUPS_MOE):
        n = n + tab_ref[tile * TAB_W + N_GROUPS_MOE + g]
    return n


def _max4(v):
    return jnp.maximum(jnp.maximum(v[0], v[1]), jnp.maximum(v[2], v[3]))


def _first_of4(v, top):
    return jnp.where(v[0] == top, 0.0, jnp.where(v[1] == top, 1.0, jnp.where(v[2] == top, 2.0, 3.0)))


def _route_plan(x, m, plan_in, plan_out, tile0):
    g_ref, wrt_ref, brt_ref, upper_ref = plan_in
    haug_ref, dloc_ref, drow_ref, cnt_ref = plan_out
    rows = x.shape[0]
    tm = ROUTE_TM
    ng = N_GROUPS_MOE
    h = _rms(x, g_ref[...]) * (1.0 + m[4:5]) + m[3:4]
    hb = h.astype(BF16)
    lt = lax.dot_general(wrt_ref[...], hb, NT_DIMS, preferred_element_type=F32) + brt_ref[...]
    gl = [lt[r:r + 1, :] for r in range(ng)]
    g_top = _max4(gl)
    g_idx = _first_of4(gl, g_top)
    g_w = 1.0 / (jnp.exp(gl[0] - g_top) + jnp.exp(gl[1] - g_top) + jnp.exp(gl[2] - g_top) + jnp.exp(gl[3] - g_top))
    ev = []
    for k in range(EXPERTS_PER_GROUP):
        cand = [lt[GATE_OFF + EXPERTS_PER_GROUP * r + k:GATE_OFF + EXPERTS_PER_GROUP * r + k + 1, :]
                for r in range(ng)]
        ev.append(jnp.where(g_idx == 0.0, cand[0], jnp.where(g_idx == 1.0, cand[1],
                            jnp.where(g_idx == 2.0, cand[2], cand[3]))))
    v1 = _max4(ev)
    i1 = _first_of4(ev, v1)
    rest = [jnp.where(i1 == float(k), NEG_BIG, ev[k]) for k in range(EXPERTS_PER_GROUP)]
    v2 = _max4(rest)
    i2 = _first_of4(rest, v2)
    e2 = jnp.exp(v2 - v1)
    w1 = 1.0 / (1.0 + e2)
    w2 = e2 * w1
    gates = [g_w * (jnp.where(i1 == float(k), w1, 0.0) + jnp.where(i2 == float(k), w2, 0.0))
             for k in range(EXPERTS_PER_GROUP)]

    sub8 = lax.broadcasted_iota(jnp.int32, (8, tm), 0).astype(F32)
    dlocs = []
    for sub in range(rows // tm):
        gi = g_idx[:, sub * tm:(sub + 1) * tm]
        hot = jnp.where(sub8 == gi, 1.0, 0.0)
        before = _bdot(hot.astype(BF16), upper_ref[...])
        dl = jnp.sum(before * hot, axis=0, keepdims=True)
        off = 0
        for g in range(ng):
            n_g = _round_up(jnp.sum(hot[g:g + 1, :]).astype(jnp.int32), ROUTE_PAD)
            cnt_ref[(tile0 + sub) * ng + g] = n_g
            dl = dl + jnp.where(gi == float(g), off.astype(F32) if g else 0.0, 0.0)
            off = off + n_g
        drow_ref[sub] = jnp.broadcast_to(dl, (8, tm))
        dlocs.append(dl)
    dloc = jnp.concatenate(dlocs, axis=1)
    d_hi = jnp.floor(dloc * (1.0 / DLOC_RADIX))
    g_hi = [gt.astype(BF16).astype(F32) for gt in gates]
    ex_rows = g_hi + [gt - gh for gt, gh in zip(gates, g_hi)] + [d_hi, dloc - DLOC_RADIX * d_hi]
    sub16 = lax.broadcasted_iota(jnp.int32, (16, rows), 0)
    ex_t = jnp.zeros((16, rows), F32)
    for r, row in enumerate(ex_rows):
        ex_t = jnp.where(sub16 == r, row, ex_t)
    ex_t = jnp.concatenate([ex_t, jnp.zeros((LANES - 16, rows), F32)], axis=0)
    extras = ex_t.T
    dloc_ref[...] = jnp.broadcast_to(
        DLOC_RADIX * extras[:, DLOC_HI:DLOC_HI + 1] + extras[:, DLOC_HI + 1:DLOC_HI + 2], (rows, LANES))
    haug_ref[...] = jnp.concatenate([hb, extras.astype(BF16)], axis=1)


def _plan_consts(g2, w_rt, b_rt):
    upper = jnp.asarray(np.triu(np.ones((ROUTE_TM, ROUTE_TM), np.float32), 1), BF16)
    return [g2, w_rt, b_rt, upper]


def _plan_out(b, s, rows, index):
    tiles = rows // ROUTE_TM
    shapes = [jax.ShapeDtypeStruct((b, s, XS_W), BF16), jax.ShapeDtypeStruct((b, s, LANES), F32),
              jax.ShapeDtypeStruct((b, s // ROUTE_TM, 8, ROUTE_TM), F32),
              jax.ShapeDtypeStruct((b * (s // ROUTE_TM) * N_GROUPS_MOE,), jnp.int32)]
    specs = [pl.BlockSpec((None, rows, XS_W), lambda *idx: (*index(*idx), 0)),
             pl.BlockSpec((None, rows, LANES), lambda *idx: (*index(*idx), 0)),
             pl.BlockSpec((None, tiles, 8, ROUTE_TM), lambda *idx: (*index(*idx), 0, 0)),
             pl.BlockSpec(memory_space=pltpu.SMEM)]
    return shapes, specs


def _flat_plan(plan):
    haug, dloc, drow, cnt = plan
    return (haug.reshape(-1, XS_W), dloc.reshape(-1, LANES), drow.reshape(-1, 8, ROUTE_TM), cnt)


def _dispatch_kernel(cp_ref, cs_ref, hp_ref, hs_ref, dp_ref, ds_ref, tab_ref, sorted_hbm,
                     sorted_ref, zeros_ref, fill_ref, sem_ref, zsem_ref, *, n_steps, n_p_steps):
    i = pl.program_id(0)
    tm = ROUTE_TM
    ng = N_GROUPS_MOE
    n_tiles = n_steps * ROUTE_SUBS
    n_p_tiles = n_p_steps * ROUTE_SUBS
    meta = n_tiles * TAB_W
    n_rows = sorted_hbm.shape[0]

    def count(tile, g):
        if isinstance(tile, int):
            return cp_ref[tile * ng + g] if tile < n_p_tiles else cs_ref[(tile - n_p_tiles) * ng + g]
        return jnp.where(tile < n_p_tiles, cp_ref[jnp.minimum(tile, n_p_tiles - 1) * ng + g],
                         cs_ref[jnp.maximum(tile - n_p_tiles, 0) * ng + g])

    @pl.when(i == 0)
    def _():
        start = 0
        for g in range(ng):
            fill = sum(count(t, g) for t in range(n_tiles))
            tab_ref[meta + g] = fill
            tab_ref[meta + ng + g] = start
            start = start + _round_up(fill, FFN_BM)
            fill_ref[g] = 0
        zeros_ref[...] = jnp.zeros_like(zeros_ref)

    is_prompt = i < n_p_steps
    step_rows = jnp.where(is_prompt, hp_ref[...], hs_ref[...])
    step_drow = jnp.where(is_prompt, dp_ref[...], ds_ref[...])
    row_f = lax.broadcasted_iota(jnp.int32, (ROUTE_ROWS, tm), 0).astype(F32)
    for sub in range(ROUTE_SUBS):
        tile = i * ROUTE_SUBS + sub

        @pl.when(i >= 1)
        def _():
            _wait_rows(_tile_rows(tab_ref, tile - ROUTE_SUBS), sorted_hbm, sorted_ref.at[sub], sem_ref.at[sub], True)

        onehot = jnp.where(row_f == step_drow[sub][0:1, :], 1.0, 0.0).astype(BF16)
        sorted_ref[sub] = _bdot(onehot, step_rows[sub * tm:(sub + 1) * tm])
        for g in range(ng):
            n_g = count(tile, g)
            tab_ref[tile * TAB_W + g] = tab_ref[meta + ng + g] + fill_ref[g]
            tab_ref[tile * TAB_W + ng + g] = n_g
            fill_ref[g] = fill_ref[g] + n_g
        _run_copies(tab_ref, tile, sorted_hbm, sorted_ref.at[sub], sem_ref.at[sub], to_hbm=True, wait=False)

        @pl.when(i == n_steps - 1)
        def _():
            _wait_rows(_tile_rows(tab_ref, tile), sorted_hbm, sorted_ref.at[sub], sem_ref.at[sub], True)

    @pl.when(i == n_steps - 1)
    def _():
        _zero_fill(tab_ref, meta, zeros_ref, sorted_hbm, zsem_ref, n_rows, wait=False)
        _zero_fill(tab_ref, meta, zeros_ref, sorted_hbm, zsem_ref, n_rows, wait=True)


def _merged_specs(n_p_steps, n_s_steps, layer, sample_seq, step_of, width=D_MODEL):
    def p_map(*idx):
        return (jnp.minimum(step_of(*idx), n_p_steps - 1), 0)

    def s_map(*idx):
        return (jnp.clip(step_of(*idx) - n_p_steps, 0, n_s_steps - 1), 0)

    def mod_map(*idx):
        j = step_of(*idx)
        row = jnp.where(j < n_p_steps, 0, 1 + lax.div(jnp.maximum(j - n_p_steps, 0) * ROUTE_STEP, sample_seq))
        return (layer, row, 0, 0)

    return (pl.BlockSpec((ROUTE_STEP, width), p_map), pl.BlockSpec((ROUTE_STEP, width), s_map),
            pl.BlockSpec((None, None, 6, D_MODEL), mod_map))


def _dispatch(plan_p, plan_s, n_rows):
    haug_p, _, drow_p, cnt_p = plan_p
    haug_s, _, drow_s, cnt_s = plan_s
    n_p_steps = haug_p.shape[0] // ROUTE_STEP
    n_s_steps = haug_s.shape[0] // ROUTE_STEP
    n_steps = n_p_steps + n_s_steps
    n_tiles = n_steps * ROUTE_SUBS
    step_of = lambda i, cp, cs: i
    hp_spec, hs_spec, _ = _merged_specs(n_p_steps, n_s_steps, 0, 1, step_of, XS_W)
    drow_block = (ROUTE_SUBS, 8, ROUTE_TM)
    grid_spec = pltpu.PrefetchScalarGridSpec(
        num_scalar_prefetch=2,
        grid=(n_steps,),
        in_specs=[
            hp_spec, hs_spec,
            pl.BlockSpec(drow_block, lambda i, cp, cs: (jnp.minimum(i, n_p_steps - 1), 0, 0)),
            pl.BlockSpec(drow_block, lambda i, cp, cs: (jnp.clip(i - n_p_steps, 0, n_s_steps - 1), 0, 0)),
        ],
        out_specs=[pl.BlockSpec(memory_space=pltpu.SMEM), pl.BlockSpec(memory_space=pl.ANY)],
        scratch_shapes=[
            pltpu.VMEM((ROUTE_SUBS, ROUTE_ROWS, XS_W), F32),
            pltpu.VMEM((FFN_BM, XS_W), F32),
            pltpu.SMEM((N_GROUPS_MOE,), jnp.int32),
            pltpu.SemaphoreType.DMA((ROUTE_SUBS,)),
            pltpu.SemaphoreType.DMA(()),
        ],
    )
    return pl.pallas_call(
        functools.partial(_dispatch_kernel, n_steps=n_steps, n_p_steps=n_p_steps),
        grid_spec=grid_spec,
        out_shape=[
            jax.ShapeDtypeStruct(((n_tiles + 1) * TAB_W,), jnp.int32),
            jax.ShapeDtypeStruct((n_rows, XS_W), F32),
        ],
        compiler_params=_cparams("arbitrary"),
        name="moe_dispatch",
    )(cnt_p, cnt_s, haug_p, haug_s, drow_p, drow_s)


def _ffn_lookup(i, tab_ref, meta):
    fills = [tab_ref[meta + g] for g in range(N_GROUPS_MOE)]
    edges = []
    acc = 0
    for f in fills:
        acc = acc + lax.div(f + (FFN_BM - 1), FFN_BM)
        edges.append(acc)
    total = edges[-1]
    ii = jnp.minimum(i, total - 1)
    grp = sum((ii >= e).astype(jnp.int32) for e in edges[:-1])

    def pick(vals):
        return jnp.where(grp == 0, vals[0], jnp.where(grp == 1, vals[1], jnp.where(grp == 2, vals[2], vals[3])))

    first = pick([0] + edges[:-1])
    return grp, total, ii == first, pick(fills) - (ii - first) * FFN_BM


def _ffn_group_kernel(tab_ref, xs_ref, w1_ref, w3_ref, w2_ref, ys_ref, w1b_ref, w3b_ref, w2b_ref, *, meta):
    i = pl.program_id(0)
    _, total, first_of_group, valid = _ffn_lookup(i, tab_ref, meta)
    active = i < total

    @pl.when(jnp.logical_and(active, first_of_group))
    def _():
        w1b_ref[...] = w1_ref[...].astype(BF16)
        w3b_ref[...] = w3_ref[...].astype(BF16)
        w2b_ref[...] = w2_ref[...].astype(BF16)

    def run(rows):
        hb = xs_ref[0:rows, 0:D_MODEL].astype(BF16)
        ex = xs_ref[0:rows, D_MODEL:XS_W]
        acc = None
        for e in range(EXPERTS_PER_GROUP):
            a = _bdot(hb, w1b_ref[e])
            b = _bdot(hb, w3b_ref[e])
            gate = ex[:, e:e + 1] + ex[:, GATE_LO + e:GATE_LO + e + 1]
            hid = (a * jax.nn.sigmoid(a)) * b * gate
            part = _bdot(hid.astype(BF16), w2b_ref[e])
            acc = part if acc is None else acc + part
        ys_ref[0:rows, :] = acc

    @pl.when(jnp.logical_and(active, valid > FFN_HALF))
    def _():
        run(FFN_BM)

    @pl.when(jnp.logical_and(active, valid <= FFN_HALF))
    def _():
        run(FFN_HALF)
        ys_ref[FFN_HALF:, :] = jnp.zeros((FFN_BM - FFN_HALF, D_MODEL), F32)

    @pl.when(jnp.logical_not(active))
    def _():
        ys_ref[...] = jnp.zeros_like(ys_ref)


def _ffn_group(tab, xs, w1, w3, w2, layer, meta):
    e4 = EXPERTS_PER_GROUP
    n_blocks = xs.shape[0] // FFN_BM
    group_of = lambda i, tab_ref: _ffn_lookup(i, tab_ref, meta)[0]
    grid_spec = pltpu.PrefetchScalarGridSpec(
        num_scalar_prefetch=1,
        grid=(n_blocks,),
        in_specs=[
            pl.BlockSpec((FFN_BM, XS_W), lambda i, tab_ref: (i, 0)),
            pl.BlockSpec((None, e4, D_MODEL, D_EXPERT), lambda i, tab_ref: (layer, group_of(i, tab_ref), 0, 0)),
            pl.BlockSpec((None, e4, D_MODEL, D_EXPERT), lambda i, tab_ref: (layer, group_of(i, tab_ref), 0, 0)),
            pl.BlockSpec((None, e4, D_EXPERT, D_MODEL), lambda i, tab_ref: (layer, group_of(i, tab_ref), 0, 0)),
        ],
        out_specs=pl.BlockSpec((FFN_BM, D_MODEL), lambda i, tab_ref: (i, 0)),
        scratch_shapes=[
            pltpu.VMEM((e4, D_MODEL, D_EXPERT), BF16),
            pltpu.VMEM((e4, D_MODEL, D_EXPERT), BF16),
            pltpu.VMEM((e4, D_EXPERT, D_MODEL), BF16),
        ],
    )
    return pl.pallas_call(
        functools.partial(_ffn_group_kernel, meta=meta),
        grid_spec=grid_spec,
        out_shape=jax.ShapeDtypeStruct((xs.shape[0], D_MODEL), F32),
        compiler_params=_cparams("arbitrary"),
        name="moe_ffn",
    )(tab, xs, w1, w3, w2)


def _fetch_runs(tab_ref, ys_ref, runs_ref, sem_ref, tile, slot):
    _run_copies(tab_ref, tile, ys_ref, runs_ref.at[slot], sem_ref.at[slot], to_hbm=False, wait=False)


def _unsort(tab_ref, ys_ref, runs_ref, sem_ref, dloc_col, tile, slot):
    covered = _tile_rows(tab_ref, tile)
    _wait_rows(covered, ys_ref, runs_ref.at[slot], sem_ref.at[slot], False)
    runs_ref[slot, pl.ds(pl.multiple_of(covered, ROUTE_PAD), LANES), :] = jnp.zeros((LANES, D_MODEL), F32)
    yb = runs_ref[slot, 0:SORT_ROWS, :].astype(BF16)
    row_f = lax.broadcasted_iota(jnp.int32, (ROUTE_TM, SORT_ROWS), 1).astype(F32)
    return _bdot(jnp.where(row_f == dloc_col, 1.0, 0.0).astype(BF16), yb)


def _combine_kernel(tab_ref, *refs, n_steps, first_step, n_p_steps, final_norm):
    with_prompt = first_step < n_p_steps
    if with_prompt:
        xp_ref, xs_ref, mod_ref, dp_ref, ds_ref, fg_ref, ys_ref, op_ref, os_ref, runs_ref, sem_ref = refs
    else:
        xs_ref, mod_ref, ds_ref, fg_ref, ys_ref, os_ref, runs_ref, sem_ref = refs
    i = pl.program_id(0)
    par = lax.rem(i, 2)
    step = first_step + i
    tm = ROUTE_TM

    def fetch(st, parity):
        for sub in range(ROUTE_SUBS):
            _fetch_runs(tab_ref, ys_ref, runs_ref, sem_ref, st * ROUTE_SUBS + sub, parity * ROUTE_SUBS + sub)

    @pl.when(i == 0)
    def _():
        fetch(first_step, 0)

    @pl.when(i + 1 < n_steps)
    def _():
        fetch(step + 1, 1 - par)

    is_prompt = step < n_p_steps
    dloc = jnp.where(is_prompt, dp_ref[:, 0:1], ds_ref[:, 0:1]) if with_prompt else ds_ref[:, 0:1]
    parts = [_unsort(tab_ref, ys_ref, runs_ref, sem_ref, dloc[sub * tm:(sub + 1) * tm],
                     step * ROUTE_SUBS + sub, par * ROUTE_SUBS + sub) for sub in range(ROUTE_SUBS)]
    x = jnp.where(is_prompt, xp_ref[...], xs_ref[...]) if with_prompt else xs_ref[...]
    x2 = x + mod_ref[5:6, :] * jnp.concatenate(parts, axis=0)
    if final_norm:
        x2 = _rms(x2, fg_ref[...])
    if with_prompt:
        @pl.when(is_prompt)
        def _():
            op_ref[...] = x2

        @pl.when(jnp.logical_not(is_prompt))
        def _():
            os_ref[...] = x2
    else:
        os_ref[...] = x2


def _combine(tab, xp, xs, mod, dloc_p, dloc_s, final_g, ys, layer, sample_seq, final_norm, with_prompt):
    n_p_steps = xp.shape[0] // ROUTE_STEP
    n_s_steps = xs.shape[0] // ROUTE_STEP
    first_step = 0 if with_prompt else n_p_steps
    n_steps = n_p_steps + n_s_steps - first_step
    step_of = lambda i, tab_ref: i + first_step
    p_spec, s_spec, mod_spec = _merged_specs(n_p_steps, n_s_steps, layer, sample_seq, step_of)
    dp_spec, ds_spec, _ = _merged_specs(n_p_steps, n_s_steps, layer, sample_seq, step_of, LANES)
    streams = [p_spec, s_spec] if with_prompt else [s_spec]
    dlocs = [dp_spec, ds_spec] if with_prompt else [ds_spec]
    grid_spec = pltpu.PrefetchScalarGridSpec(
        num_scalar_prefetch=1,
        grid=(n_steps,),
        in_specs=streams + [mod_spec] + dlocs + [
            pl.BlockSpec((1, D_MODEL), lambda i, tab_ref: (0, 0)),
            pl.BlockSpec(memory_space=pl.ANY),
        ],
        out_specs=streams,
        scratch_shapes=[
            pltpu.VMEM((2 * ROUTE_SUBS, RUN_ROWS, D_MODEL), F32),
            pltpu.SemaphoreType.DMA((2 * ROUTE_SUBS,)),
        ],
    )
    arrays = [xp, xs] if with_prompt else [xs]
    dloc_arrays = [dloc_p, dloc_s] if with_prompt else [dloc_s]
    return pl.pallas_call(
        functools.partial(_combine_kernel, n_steps=n_steps, first_step=first_step, n_p_steps=n_p_steps,
                          final_norm=final_norm),
        grid_spec=grid_spec,
        out_shape=[jax.ShapeDtypeStruct(a.shape, F32) for a in arrays],
        compiler_params=_cparams("arbitrary"),
        name="moe_combine",
    )(tab, *arrays, mod, *dloc_arrays, final_g, ys)


def _moe(xp, xs, plan_p, plan_s, mod, w1, w3, w2, final_g, layer, sample_seq, final_norm, defer_prompt):
    plan_p, plan_s = _flat_plan(plan_p), _flat_plan(plan_s)
    t = xp.shape[0] + xs.shape[0]
    n_tiles = t // ROUTE_TM
    max_rows = t + N_GROUPS_MOE * (ROUTE_PAD - 1) * n_tiles
    n_rows = (-(-max_rows // FFN_BM) + N_GROUPS_MOE) * FFN_BM
    tab, sorted_x = _dispatch(plan_p, plan_s, n_rows)
    ys = _ffn_group(tab, sorted_x, w1, w3, w2, layer, n_tiles * TAB_W)
    out = _combine(tab, xp, xs, mod, plan_p[1], plan_s[1], final_g, ys, layer, sample_seq, final_norm,
                   not defer_prompt)
    if defer_prompt:
        return None, out[0], (tab, plan_p[1], ys)
    return out[0], out[1], None


def _rope_tables(seq):
    half = QK_ROPE // 2
    nf = half // 2
    inv = ROPE_BASE ** (-np.arange(nf, dtype=np.float64) / nf)
    pos = np.arange(seq)
    row = (pos // GRID_W).astype(np.float64)
    col = (pos % GRID_W).astype(np.float64)
    cos = np.ones((seq, HEAD_PAD), np.float64)
    sin_a = np.zeros((seq, HEAD_PAD), np.float64)
    sin_b = np.zeros((seq, HEAD_PAD), np.float64)
    for part, p in enumerate((row, col)):
        ang = p[:, None] * inv[None, :]
        base = ROPE_OFF + part * half
        cos[:, base:base + nf] = np.cos(ang)
        cos[:, base + nf:base + half] = np.cos(ang)
        sin_a[:, base:base + nf] = -np.sin(ang)
        sin_b[:, base + nf:base + half] = np.sin(ang)
    return tuple(jnp.asarray(a, F32) for a in (cos, sin_a, sin_b))


def _apply_rope(x, cos, sin_a, sin_b, reps):
    nf = QK_ROPE // 4
    width = x.shape[1]
    if reps > 1:
        cos, sin_a, sin_b = (jnp.concatenate([a] * reps, axis=1) for a in (cos, sin_a, sin_b))
    return x * cos + pltpu.roll(x, width - nf, axis=1) * sin_a + pltpu.roll(x, nf, axis=1) * sin_b


def _odd_in_kernel(*refs, rope, emit_cache):
    x_ref, mod_ref, g_ref, w_ref, qg_ref, wq_ref, kg_ref, wk_ref, wv_ref, cs_ref = refs[:10]
    refs = refs[10:]
    if rope:
        cos_ref, sa_ref, sb_ref = refs[:3]
        refs = refs[3:]
    y_ref, q_ref, k_ref, v_ref = refs[:4]
    refs = refs[4:]
    m = mod_ref[...]
    h = _rms(x_ref[...], g_ref[...]) * (1.0 + m[1:2]) + m[0:1]
    z = _bdot(h.astype(BF16), w_ref[...])
    zc = z[:, 0:D_C]
    qc = z[:, D_C:D_C + Q_LORA]
    kvc = z[:, D_C + Q_LORA:D_C + Q_LORA + KV_LORA]
    kpe = z[:, D_C + Q_LORA + KV_LORA:]
    q = _bdot(_rms(qc, qg_ref[...]).astype(BF16), wq_ref[...])
    kvn = _rms(kvc, kg_ref[...])
    if emit_cache:
        ckv_ref, kpe_ref = refs
        ckv_ref[...] = kvn
        kpe_ref[...] = kpe[:, ROPE_OFF:ROPE_OFF + QK_ROPE]
    if rope:
        tabs = (cos_ref[...], sa_ref[...], sb_ref[...])
        q = _apply_rope(q, *tabs, reps=N_HEADS)
        kpe = _apply_rope(kpe, *tabs, reps=1)
    kvb = kvn.astype(BF16)
    k = _bdot(kvb, wk_ref[...]) + jnp.concatenate([kpe] * N_HEADS, axis=1)
    scale = math.log2(math.e) / math.sqrt(QK_NOPE + QK_ROPE)
    q_ref[...] = (q * scale).astype(BF16)
    k_ref[...] = k.astype(BF16)
    v_ref[...] = _bdot(kvb, wv_ref[...]).astype(BF16)
    y = _bdot(zc.astype(BF16), cs_ref[...])
    y_ref[0, :, :] = y[:, 0:D_C].astype(BF16)
    y_ref[1, :, :] = y[:, D_C:2 * D_C].astype(BF16)


def _odd_in(x3, mod, g, w_in, q_g, w_q, kv_g, w_k, w_v, cs, stream, layer, rope_tabs, emit_cache):
    b, s, _ = x3.shape
    tm = min(s, 512)
    n_i = s // tm
    rope = rope_tabs is not None
    const = lambda a: pl.BlockSpec(a.shape, lambda bi, i: (0,) * a.ndim)
    in_specs = [
        pl.BlockSpec((None, tm, D_MODEL), lambda bi, i: (bi, i, 0)),
        _mod_spec(stream, layer),
        const(g), const(w_in), const(q_g), const(w_q), const(kv_g), const(w_k), const(w_v), const(cs),
    ]
    args = [x3, mod, g, w_in, q_g, w_q, kv_g, w_k, w_v, cs]
    if rope:
        in_specs += [pl.BlockSpec((tm, HEAD_PAD), lambda bi, i: (i, 0))] * 3
        args += list(rope_tabs)
    hq = N_HEADS * HEAD_PAD
    out_specs = [
        pl.BlockSpec((None, 2, tm, D_C), lambda bi, i: (bi, 0, i, 0)),
        pl.BlockSpec((None, tm, hq), lambda bi, i: (bi, i, 0)),
        pl.BlockSpec((None, tm, hq), lambda bi, i: (bi, i, 0)),
        pl.BlockSpec((None, tm, N_HEADS * V_DIM), lambda bi, i: (bi, i, 0)),
    ]
    out_shape = [
        jax.ShapeDtypeStruct((b, 2, s, D_C), BF16),
        jax.ShapeDtypeStruct((b, s, hq), BF16),
        jax.ShapeDtypeStruct((b, s, hq), BF16),
        jax.ShapeDtypeStruct((b, s, N_HEADS * V_DIM), BF16),
    ]
    if emit_cache:
        out_specs += [
            pl.BlockSpec((None, tm, KV_LORA), lambda bi, i: (bi, i, 0)),
            pl.BlockSpec((None, tm, QK_ROPE), lambda bi, i: (bi, i, 0)),
        ]
        out_shape += [
            jax.ShapeDtypeStruct((b, s, KV_LORA), F32),
            jax.ShapeDtypeStruct((b, s, QK_ROPE), F32),
        ]
    return pl.pallas_call(
        functools.partial(_odd_in_kernel, rope=rope, emit_cache=emit_cache),
        grid=(b, n_i),
        in_specs=in_specs,
        out_specs=out_specs,
        out_shape=out_shape,
        compiler_params=_cparams("parallel", "parallel"),
        name="odd_in",
    )(*args)


def _cache_kv_kernel(c_ref, p_ref, wk_ref, wv_ref, k_ref, v_ref):
    cb = c_ref[...].astype(BF16)
    k = _bdot(cb, wk_ref[...]) + jnp.concatenate([p_ref[...]] * N_HEADS, axis=1)
    k_ref[...] = k.astype(BF16)
    v_ref[...] = _bdot(cb, wv_ref[...]).astype(BF16)


def _cache_kv(ckv, kpe_blk, w_k, w_v):
    b, p, _ = ckv.shape
    hq = N_HEADS * HEAD_PAD
    return pl.pallas_call(
        _cache_kv_kernel,
        grid=(b,),
        in_specs=[
            pl.BlockSpec((None, p, KV_LORA), lambda bi: (bi, 0, 0)),
            pl.BlockSpec((None, p, HEAD_PAD), lambda bi: (bi, 0, 0)),
            pl.BlockSpec(w_k.shape, lambda bi: (0, 0)),
            pl.BlockSpec(w_v.shape, lambda bi: (0, 0)),
        ],
        out_specs=[
            pl.BlockSpec((None, p, hq), lambda bi: (bi, 0, 0)),
            pl.BlockSpec((None, p, N_HEADS * V_DIM), lambda bi: (bi, 0, 0)),
        ],
        out_shape=[
            jax.ShapeDtypeStruct((b, p, hq), BF16),
            jax.ShapeDtypeStruct((b, p, N_HEADS * V_DIM), BF16),
        ],
        compiler_params=_cparams("parallel"),
        name="cache_kv",
    )(ckv, kpe_blk, w_k, w_v)


def _odd_mix_kernel(*refs, with_cache):
    q_ref, k_ref, v_ref = refs[:3]
    refs = refs[3:]
    if with_cache:
        kc_ref, vc_ref = refs[:2]
        refs = refs[2:]
    plan_tile = None
    if with_cache:
        y_ref, f_ref, x_ref, mod_ref, wo_ref = refs[:5]
        plan_in, (o_ref, *plan_out), (a_ref, vx_ref, vcx_ref) = refs[5:9], refs[9:14], refs[14:]
        plan_tile = pl.program_id(0) * pl.num_programs(1) + pl.program_id(1)

        @pl.when(pl.program_id(1) == 0)
        def _():
            for src, dst in ((v_ref, vx_ref), (vc_ref, vcx_ref)):
                one = lax.broadcasted_iota(jnp.int32, (src.shape[0], PAIR_W - DEN_COL), 1) == 0
                for pair in range(N_HEADS // 2):
                    dst[:, pair * PAIR_W:pair * PAIR_W + DEN_COL] = src[:, pair * DEN_COL:(pair + 1) * DEN_COL]
                    dst[:, pair * PAIR_W + DEN_COL:(pair + 1) * PAIR_W] = jnp.where(one, 1.0, 0.0).astype(BF16)
    else:
        y_ref, f_ref, x_ref, mod_ref, wo_ref, o_ref, a_ref = refs[:7]
        if len(refs) > 7:
            plan_in, plan_out, plan_tile = refs[7:]
    tq = q_ref.shape[0]
    lane = lax.broadcasted_iota(jnp.int32, (tq, 2 * V_DIM), 1)
    for pair in range(N_HEADS // 2):
        vcols = slice(pair * PAIR_W, (pair + 1) * PAIR_W)
        outs = []
        for h in (2 * pair, 2 * pair + 1):
            hcols = slice(h * HEAD_PAD, (h + 1) * HEAD_PAD)
            qh = q_ref[:, hcols]
            s = lax.dot_general(qh, k_ref[:, hcols], NT_DIMS, preferred_element_type=F32)
            top = jnp.max(s, axis=-1, keepdims=True)
            if with_cache:
                sc = lax.dot_general(qh, kc_ref[:, hcols], NT_DIMS, preferred_element_type=F32)
                top = jnp.maximum(top, jnp.max(sc, axis=-1, keepdims=True))
            if with_cache:
                acc = _bdot(jnp.exp2((s - top).astype(BF16)), vx_ref[:, vcols])
                acc = acc + _bdot(jnp.exp2((sc - top).astype(BF16)), vcx_ref[:, vcols])
                outs.append(acc[:, 0:2 * V_DIM] / acc[:, DEN_COL:DEN_COL + 1])
            else:
                p = jnp.exp2(s - top)
                den = jnp.sum(p, axis=-1, keepdims=True)
                outs.append(_bdot(p.astype(BF16), v_ref[:, pair * DEN_COL:(pair + 1) * DEN_COL]) / den)
        a_ref[:, pair * 2 * V_DIM:(pair + 1) * 2 * V_DIM] = jnp.where(lane < V_DIM, outs[0], outs[1]).astype(BF16)
    f = _bdot(f_ref[...], y_ref[...].reshape(-1, D_C))
    o = _bdot(f.astype(BF16), wo_ref[0:D_C, :]) + _bdot(a_ref[...], wo_ref[D_C:, :])
    x1 = x_ref[...] + mod_ref[2:3, :] * o
    o_ref[...] = x1
    if plan_tile is not None:
        _route_plan(x1, mod_ref[...], plan_in, plan_out, plan_tile)


def _odd_whole_kernel(tab_ref, x_ref, mod_ref, g_ref, w_ref, qg_ref, wq_ref, kg_ref, wk_ref, wv_ref, cs_ref, f_ref,
                      wo_ref, modp_ref, dloc_ref, ys_ref, pg_ref, pw_ref, pb_ref, pu_ref,
                      o_ref, ckv_ref, kpe_ref, ph_ref, pd_ref, pr_ref, pc_ref,
                      y_ref, q_ref, k_ref, v_ref, a_ref, x2_ref, runs_ref, sem_ref):
    bi = pl.program_id(0)
    par = lax.rem(bi, 2)

    @pl.when(bi == 0)
    def _():
        _fetch_runs(tab_ref, ys_ref, runs_ref, sem_ref, 0, 0)

    @pl.when(bi + 1 < pl.num_programs(0))
    def _():
        _fetch_runs(tab_ref, ys_ref, runs_ref, sem_ref, bi + 1, 1 - par)

    moe = _unsort(tab_ref, ys_ref, runs_ref, sem_ref, dloc_ref[:, 0:1], bi, par)
    x2_ref[...] = x_ref[...] + modp_ref[5:6, :] * moe
    _odd_in_kernel(x2_ref, mod_ref, g_ref, w_ref, qg_ref, wq_ref, kg_ref, wk_ref, wv_ref, cs_ref,
                   y_ref, q_ref, k_ref, v_ref, ckv_ref, kpe_ref, rope=False, emit_cache=True)
    _odd_mix_kernel(q_ref, k_ref, v_ref, y_ref, f_ref, x2_ref, mod_ref, wo_ref, o_ref, a_ref,
                    (pg_ref, pw_ref, pb_ref, pu_ref), (ph_ref, pd_ref, pr_ref, pc_ref), bi, with_cache=False)


def _odd_whole(x3, pending, mod, g, w_in, q_g, w_q, kv_g, w_k, w_v, cs, fmat, w_out, plan_consts, stream, layer):
    tab, dloc, ys = pending
    b, s, _ = x3.shape
    assert s == ROUTE_TM, "one sequence must be one MoE sort tile"
    hq = N_HEADS * HEAD_PAD
    hv = N_HEADS * V_DIM
    const = lambda a: pl.BlockSpec(a.shape, lambda bi, tab_ref: (0,) * a.ndim)
    row_block = lambda w: pl.BlockSpec((None, s, w), lambda bi, tab_ref: (bi, 0, 0))
    mod_block = lambda lyr: pl.BlockSpec((None, None, 6, D_MODEL),
                                         lambda bi, tab_ref: (lyr, stream.row_of_batch(bi), 0, 0))
    plan_shapes, plan_specs = _plan_out(b, s, s, lambda bi, tab_ref: (bi, 0))
    grid_spec = pltpu.PrefetchScalarGridSpec(
        num_scalar_prefetch=1,
        grid=(b,),
        in_specs=[
            row_block(D_MODEL), mod_block(layer),
            const(g), const(w_in), const(q_g), const(w_q), const(kv_g), const(w_k), const(w_v), const(cs),
            const(fmat), const(w_out),
            mod_block(layer - 1),
            pl.BlockSpec((s, LANES), lambda bi, tab_ref: (bi, 0)),
            pl.BlockSpec(memory_space=pl.ANY),
        ] + [const(a) for a in plan_consts],
        out_specs=[row_block(D_MODEL), row_block(KV_LORA), row_block(QK_ROPE)] + plan_specs,
        scratch_shapes=[
            pltpu.VMEM((2, s, D_C), BF16),
            pltpu.VMEM((s, hq), BF16),
            pltpu.VMEM((s, hq), BF16),
            pltpu.VMEM((s, hv), BF16),
            pltpu.VMEM((s, hv), BF16),
            pltpu.VMEM((s, D_MODEL), F32),
            pltpu.VMEM((2, RUN_ROWS, D_MODEL), F32),
            pltpu.SemaphoreType.DMA((2,)),
        ],
    )
    out = pl.pallas_call(
        _odd_whole_kernel,
        grid_spec=grid_spec,
        out_shape=[
            jax.ShapeDtypeStruct((b, s, D_MODEL), F32),
            jax.ShapeDtypeStruct((b, s, KV_LORA), F32),
            jax.ShapeDtypeStruct((b, s, QK_ROPE), F32),
        ] + plan_shapes,
        compiler_params=_cparams("arbitrary"),
        name="odd_whole",
    )(tab, x3, mod, g, w_in, q_g, w_q, kv_g, w_k, w_v, cs, fmat, w_out, mod, dloc, ys, *plan_consts)
    return out[0], out[1], out[2], out[3:]


def _odd_mix(q, k, v, kc, vc, y, fmat, x3, mod, w_out, plan_consts, stream, layer):
    b, s, hq = q.shape
    tq = min(s, 256)
    n_i = s // tq
    with_cache = kc is not None
    hv = N_HEADS * V_DIM
    mode = dict(pipeline_mode=pl.Buffered(1)) if with_cache else {}

    def per_batch(rows, cols):
        return pl.BlockSpec((None, rows, cols), lambda bi, i: (bi, 0, 0), **mode)

    in_specs = [pl.BlockSpec((None, tq, hq), lambda bi, i: (bi, i, 0)), per_batch(s, hq), per_batch(s, hv)]
    args = [q, k, v]
    if with_cache:
        p = kc.shape[1]
        in_specs += [per_batch(p, hq), per_batch(p, hv)]
        args += [kc, vc]
    in_specs += [
        per_batch(2 * s, D_C),
        pl.BlockSpec((tq, 2 * s), lambda bi, i: (i, 0)),
        pl.BlockSpec((None, tq, D_MODEL), lambda bi, i: (bi, i, 0)),
        _mod_spec(stream, layer),
        pl.BlockSpec(w_out.shape, lambda bi, i: (0, 0), **mode),
    ]
    in_specs += [pl.BlockSpec(a.shape, lambda bi, i: (0,) * a.ndim) for a in plan_consts]
    args += [y, fmat, x3, mod, w_out] + plan_consts
    plan_shapes, plan_specs = _plan_out(b, s, tq, lambda bi, i: (bi, i))
    out = pl.pallas_call(
        functools.partial(_odd_mix_kernel, with_cache=with_cache),
        grid=(b, n_i),
        in_specs=in_specs,
        out_specs=[pl.BlockSpec((None, tq, D_MODEL), lambda bi, i: (bi, i, 0))] + plan_specs,
        out_shape=[jax.ShapeDtypeStruct((b, s, D_MODEL), F32)] + plan_shapes,
        scratch_shapes=[pltpu.VMEM((tq, hv), BF16)] + (
            [pltpu.VMEM((s, VX_W), BF16), pltpu.VMEM((kc.shape[1], VX_W), BF16)] if with_cache else []),
        compiler_params=_cparams("arbitrary", "arbitrary"),
        name="odd_mix",
    )(*args)
    return out[0], out[1:]


def _dft_tables(seq):
    jc = np.arange(C_GW)
    ang_c = 2.0 * np.pi * np.outer(jc, jc) / C_GW
    eye = np.eye(C_GROUPS)
    cs = np.concatenate([np.kron(eye, np.cos(ang_c)), np.kron(eye, np.sin(ang_c))], axis=1)
    jn = np.arange(seq)
    ang_n = 2.0 * np.pi * (np.outer(jn, jn) % seq) / seq
    scale = 1.0 / math.sqrt(seq * C_GW)
    fmat = np.concatenate([np.cos(ang_n), -np.sin(ang_n)], axis=1) * scale
    return jnp.asarray(cs, F32).astype(BF16), jnp.asarray(fmat, F32).astype(BF16)


def _odd_weights(w_in, w_uq, w_ukv):
    d = w_in.shape[0]
    base = D_C + Q_LORA + KV_LORA
    kpe_blk = jnp.zeros((d, HEAD_PAD), w_in.dtype).at[:, ROPE_OFF:ROPE_OFF + QK_ROPE].set(w_in[:, base:])
    w_in_p = jnp.concatenate([w_in[:, :base], kpe_blk], axis=1).astype(BF16)
    qh = w_uq.reshape(Q_LORA, N_HEADS, QK_NOPE + QK_ROPE)
    w_q = jnp.pad(qh, ((0, 0), (0, 0), (0, HEAD_PAD - QK_NOPE - QK_ROPE))).reshape(Q_LORA, -1).astype(BF16)
    kvh = w_ukv.reshape(KV_LORA, N_HEADS, QK_NOPE + V_DIM)
    w_k = jnp.pad(kvh[:, :, :QK_NOPE], ((0, 0), (0, 0), (0, HEAD_PAD - QK_NOPE))).reshape(KV_LORA, -1)
    w_v = kvh[:, :, QK_NOPE:].reshape(KV_LORA, -1)
    return w_in_p, w_q, w_k.astype(BF16), w_v.astype(BF16)


ROUTER_ROWS = 32


def _router_weights(wg, bg, we, be):
    d = wg.shape[0]
    w = jnp.concatenate([wg, we.reshape(d, N_EXPERTS)], axis=1).T
    w = jnp.pad(w, ((0, ROUTER_ROWS - w.shape[0]), (0, 0))).astype(BF16)
    b = jnp.concatenate([bg, be.reshape(N_EXPERTS)])
    b = jnp.pad(b, (0, ROUTER_ROWS - b.shape[0])).reshape(ROUTER_ROWS, 1).astype(F32)
    return w, b


def kernel(x_prompt, x_sample, cache_ckv, cache_kpe, c, c_ctx, mod_w, mod_b, norm1_g, norm2_g,
           ev_w_in, ev_conv_w, ev_sgu_norm_g, ev_sgu_w, ev_sgu_b, ev_w_out,
           od_w_in, od_q_norm_g, od_w_uq, od_kv_norm_g, od_w_ukv, od_w_out,
           moe_wg, moe_bg, moe_we, moe_be, moe_w1, moe_w3, moe_w2, final_norm_g):
    bp, n_p, d = x_prompt.shape
    bs, n_s, _ = x_sample.shape
    streams = [(_Stream(bp, n_p, True), x_prompt), (_Stream(bs, n_s, False), x_sample)]

    n_rows = 1 + bs
    cond_t = jnp.concatenate([c_ctx[None, :], c], axis=0).T
    mod = _adaln(cond_t, mod_w, mod_b, n_rows)

    final_g = final_norm_g.reshape(1, d)
    xs = [x for _, x in streams]
    new_ckv, new_kpe = [], []
    pending = None
    plans = [None, None]
    for l in range(DEPTH):
        j = l // 2
        g1 = norm1_g[l].reshape(1, d)
        g2 = norm2_g[l].reshape(1, d)
        plan_consts = _plan_consts(g2, *_router_weights(moe_wg[l], moe_bg[l], moe_we[l], moe_be[l]))
        last = l == DEPTH - 1
        if l % 2 == 0:
            w_in = ev_w_in[j].astype(BF16)
            w_out = ev_w_out[j].astype(BF16)
            sgu_w = ev_sgu_w[j].astype(BF16)
            sgu_g = ev_sgu_norm_g[j].reshape(1, D_B)
            sgu_bias = jnp.repeat(ev_sgu_b[j].T, D_B // B_GROUPS, axis=1)
            for si, (st, _) in enumerate(streams):
                xs[si], plans[si] = _even_layer(xs[si], mod, g1, w_in, ev_conv_w[j], sgu_g, sgu_w, sgu_bias, w_out,
                                                plan_consts, st, l)
        else:
            w_in, w_q, w_k, w_v = _odd_weights(od_w_in[j], od_w_uq[j], od_w_ukv[j])
            w_out = od_w_out[j].astype(BF16)
            q_g = od_q_norm_g[j].reshape(1, Q_LORA)
            kv_g = od_kv_norm_g[j].reshape(1, KV_LORA)
            for si, (st, _) in enumerate(streams):
                x3 = xs[si]
                cs, fmat = _dft_tables(st.seq)
                if st.shared_cond:
                    xs[si], ckv, kpe, plans[si] = _odd_whole(x3, pending, mod, g1, w_in, q_g, w_q, kv_g, w_k, w_v, cs,
                                                             fmat, w_out, plan_consts, st, l)
                    new_ckv.append(ckv)
                    new_kpe.append(kpe)
                    continue
                y, q, k, v = _odd_in(x3, mod, g1, w_in, q_g, w_q, kv_g, w_k, w_v, cs, st, l,
                                     _rope_tables(st.seq), False)
                kpe_blk = jnp.pad(cache_kpe[:, j], ((0, 0), (0, 0), (ROPE_OFF, HEAD_PAD - ROPE_OFF - QK_ROPE)))
                kc, vc = _cache_kv(cache_ckv[:, j], kpe_blk, w_k, w_v)
                xs[si], plans[si] = _odd_mix(q, k, v, kc, vc, y.reshape(st.batch, 2 * st.seq, D_C), fmat, x3, mod,
                                             w_out, plan_consts, st, l)
        defer = not last and (l + 1) % 2 == 1
        x2p, x2s, pending = _moe(xs[0].reshape(bp * n_p, d), xs[1].reshape(bs * n_s, d), plans[0], plans[1], mod,
                                 moe_w1, moe_w3, moe_w2, final_g, l, n_s, last, defer)
        xs = [xs[0] if defer else x2p.reshape(bp, n_p, d), x2s.reshape(bs, n_s, d)]
    return (xs[0], xs[1], jnp.stack(new_ckv, axis=1), jnp.stack(new_kpe, axis=1))
```

```python
import functools
import math

import numpy as np
import jax
import jax.numpy as jnp
from jax import lax
from jax.experimental import pallas as pl
from jax.experimental.pallas import tpu as pltpu

D_MODEL = 1024
DEPTH = 2
GRID_W = 64
D_A = D_MODEL // 2
D_B = D_MODEL // 2
B_GROUPS = 4
CHUNK = 128
D_EVEN_IN = 3 * D_A + 2 * D_B
D_C = D_MODEL // 4
C_GROUPS = 4
C_GW = D_C // C_GROUPS
N_HEADS = 12
QK_NOPE = 64
QK_ROPE = 32
V_DIM = 64
Q_LORA = 384
KV_LORA = 256
ROPE_BASE = 10000.0
N_GROUPS_MOE = 4
EXPERTS_PER_GROUP = 4
N_EXPERTS = N_GROUPS_MOE * EXPERTS_PER_GROUP
D_EXPERT = 256
EPS = 1e-6

LANES = 128
HEAD_PAD = 128
PAIR_W = 256
DEN_COL = 2 * 64
VX_W = 6 * PAIR_W
ROPE_OFF = QK_NOPE
GATE_OFF = N_GROUPS_MOE
NEG_BIG = -1e30
F32 = jnp.float32
BF16 = jnp.bfloat16
VMEM_LIMIT = 56 * 1024 * 1024


def _cparams(*sem):
    return pltpu.CompilerParams(dimension_semantics=sem, vmem_limit_bytes=VMEM_LIMIT)


def _rms(x, g):
    return x * lax.rsqrt(jnp.mean(x * x, axis=-1, keepdims=True) + EPS) * g


def _bdot(a, b):
    return jnp.dot(a, b, preferred_element_type=F32)


NT_DIMS = (((1,), (1,)), ((), ()))
TN_DIMS = (((0,), (0,)), ((), ()))


def _mod_kernel(ct_ref, w_ref, b_ref, o_ref):
    c = ct_ref[...]
    s = c * jax.nn.sigmoid(c)
    w = w_ref[...]
    b = b_ref[...]
    for r in range(o_ref.shape[0]):
        o_ref[r:r + 1, :] = jnp.sum(s[:, r:r + 1] * w, axis=0, keepdims=True) + b


def _adaln(cond_t, mod_w, mod_b, n_rows):
    nt = 2048
    d6 = mod_w.shape[-1]
    out = pl.pallas_call(
        _mod_kernel,
        grid=(DEPTH, d6 // nt),
        in_specs=[
            pl.BlockSpec(cond_t.shape, lambda l, n: (0, 0)),
            pl.BlockSpec((None, D_MODEL, nt), lambda l, n: (l, 0, n)),
            pl.BlockSpec((None, 1, nt), lambda l, n: (l, 0, n)),
        ],
        out_specs=pl.BlockSpec((None, n_rows, nt), lambda l, n: (l, 0, n)),
        out_shape=jax.ShapeDtypeStruct((DEPTH, n_rows, d6), F32),
        compiler_params=_cparams("parallel", "parallel"),
        name="adaln",
    )(cond_t, mod_w, mod_b.reshape(DEPTH, 1, d6))
    return out.reshape(DEPTH, n_rows, 6, D_MODEL)


class _Stream:
    def __init__(self, batch, seq, shared_cond):
        self.batch = batch
        self.seq = seq
        self.tokens = batch * seq
        self.shared_cond = shared_cond

    def row_of_batch(self, b):
        return 0 if self.shared_cond else b + 1

    def row_of_tile(self, i, tm):
        return 0 if self.shared_cond else (i * tm) // self.seq + 1


def _mod_spec(stream, layer, tm=None):
    if tm is None:
        return pl.BlockSpec((None, None, 6, D_MODEL), lambda b, i: (layer, stream.row_of_batch(b), 0, 0))
    return pl.BlockSpec((None, None, 6, D_MODEL), lambda i, *_: (layer, stream.row_of_tile(i, tm), 0, 0))


HALO = 8


def _even_kernel(*refs, has_halo, seq_rows, nc):
    if has_halo:
        x_ref, xp_ref, xn_ref = refs[:3]
        refs = refs[3:]
    else:
        x_ref = refs[0]
        refs = refs[1:]
    mod_ref, g_ref, wi_ref, cw_ref, sg_ref, sw_ref, sb_ref, wo_ref = refs[:8]
    plan_in, (o_ref, *plan_out), (z_ref, y_ref) = refs[8:12], refs[12:17], refs[17:]
    i = pl.program_id(1)
    n_i = pl.num_programs(1)
    ts = x_ref.shape[0]
    m = mod_ref[...]
    g = g_ref[...]

    def modulate(x):
        return (_rms(x, g) * (1.0 + m[1:2]) + m[0:1]).astype(BF16)

    x = x_ref[...]
    hb = modulate(x)
    for n in range(D_EVEN_IN // nc):
        z_ref[:, n * nc:(n + 1) * nc] = _bdot(hb, wi_ref[:, n * nc:(n + 1) * nc]).astype(BF16)

    gate_b = z_ref[:, 0:D_A].astype(F32)
    gate_c = z_ref[:, D_A:2 * D_A].astype(F32)
    xa = z_ref[:, 2 * D_A:3 * D_A].astype(F32)
    t = gate_c * xa
    t_prev = pltpu.roll(t, 1, axis=0)
    t_next = pltpu.roll(t, ts - 1, axis=0)
    row = lax.broadcasted_iota(jnp.int32, (ts, 1), 0) & (seq_rows - 1)
    if has_halo:
        hh = modulate(jnp.concatenate([xp_ref[...], xn_ref[...]], axis=0))
        zh = _bdot(hh, wi_ref[:, D_A:3 * D_A]).astype(BF16).astype(F32)
        th = zh[:, 0:D_A] * zh[:, D_A:2 * D_A]
        tp = th[HALO - 1:HALO] * (i > 0).astype(F32)
        tn = th[HALO:HALO + 1] * (i < n_i - 1).astype(F32)
    else:
        tp = tn = 0.0
    t_prev = jnp.where(row == 0, tp, t_prev)
    t_next = jnp.where(row == seq_rows - 1, tn, t_next)
    cw = cw_ref[...]
    y_a = gate_b * (t_prev * cw[0:1] + t * cw[1:2] + t_next * cw[2:3])
    y_ref[:, 0:D_A] = y_a.astype(BF16)

    u = z_ref[:, 3 * D_A:3 * D_A + D_B].astype(F32)
    v = z_ref[:, 3 * D_A + D_B:3 * D_A + 2 * D_B].astype(F32)
    vb = _rms(v, sg_ref[...]).astype(BF16)
    gw = D_B // B_GROUPS
    for c in range(ts // CHUNK):
        rows = slice(c * CHUNK, (c + 1) * CHUNK)
        for gi in range(B_GROUPS):
            cols = slice(gi * gw, (gi + 1) * gw)
            sv = _bdot(sw_ref[gi], vb[rows, cols]) + sb_ref[:, cols]
            y_ref[rows, D_A + gi * gw:D_A + (gi + 1) * gw] = (u[rows, cols] * sv).astype(BF16)

    x1 = x + m[2:3] * _bdot(y_ref[...], wo_ref[...])
    o_ref[...] = x1
    _route_plan(x1, m, plan_in, plan_out, (pl.program_id(0) * n_i + i) * (ts // ROUTE_TM))


def _even_layer(x3, mod, g, w_in, conv_w, sgu_g, sgu_w, sgu_bias, w_out, plan_consts, stream, layer):
    out_shape = x3.shape
    b, s, _ = x3.shape
    ts = min(s, 256)
    n_i = s // ts
    has_halo = n_i > 1
    seq_rows = ts
    if not has_halo and stream.shared_cond and b % 2 == 0:
        b, s, ts = b // 2, 2 * s, 2 * ts
        x3 = x3.reshape(b, s, D_MODEL)
    hb = ts // HALO
    last_h = s // HALO - 1
    const = lambda a: pl.BlockSpec(a.shape, lambda bi, i: (0,) * a.ndim)
    in_specs = [pl.BlockSpec((None, ts, D_MODEL), lambda bi, i: (bi, i, 0))]
    args = [x3]
    if has_halo:
        in_specs += [
            pl.BlockSpec((None, HALO, D_MODEL), lambda bi, i: (bi, jnp.maximum(i * hb - 1, 0), 0)),
            pl.BlockSpec((None, HALO, D_MODEL), lambda bi, i: (bi, jnp.minimum((i + 1) * hb, last_h), 0)),
        ]
        args += [x3, x3]
    in_specs += [_mod_spec(stream, layer), const(g), const(w_in), const(conv_w), const(sgu_g), const(sgu_w),
                 const(sgu_bias), const(w_out)] + [const(a) for a in plan_consts]
    args += [mod, g, w_in, conv_w, sgu_g, sgu_w, sgu_bias, w_out] + plan_consts
    plan_shapes, plan_specs = _plan_out(b, s, ts, lambda bi, i: (bi, i))
    out = pl.pallas_call(
        functools.partial(_even_kernel, has_halo=has_halo, seq_rows=seq_rows, nc=512),
        grid=(b, n_i),
        in_specs=in_specs,
        out_specs=[pl.BlockSpec((None, ts, D_MODEL), lambda bi, i: (bi, i, 0))] + plan_specs,
        out_shape=[jax.ShapeDtypeStruct((b, s, D_MODEL), F32)] + plan_shapes,
        scratch_shapes=[pltpu.VMEM((ts, D_EVEN_IN), BF16), pltpu.VMEM((ts, D_A + D_B), BF16)],
        compiler_params=_cparams("arbitrary", "arbitrary"),
        name="even_layer",
    )(*args)
    return out[0].reshape(out_shape), out[1:]


ROUTE_TM = 256
ROUTE_SUBS = 4
ROUTE_STEP = ROUTE_TM * ROUTE_SUBS
ROUTE_PAD = 8
SORT_ROWS = ROUTE_TM + LANES
RUN_ROWS = SORT_ROWS + 32
ROUTE_ROWS = ROUTE_TM + 4 * ROUTE_PAD
XS_W = D_MODEL + LANES
GATE_LO = EXPERTS_PER_GROUP
DLOC_HI = 2 * EXPERTS_PER_GROUP
DLOC_RADIX = 16.0
FFN_BM = 512
FFN_HALF = FFN_BM // 2
RUN_SIZES = (256, 128, 64, 32, 16, 8)
TAB_W = 2 * N_GROUPS_MOE


def _round_up(x, m):
    return lax.div(x + (m - 1), m) * m


def _run_copies(tab_ref, tile, hbm_ref, vmem_ref, sem, to_hbm, wait):
    off = 0
    for g in range(N_GROUPS_MOE):
        start = tab_ref[tile * TAB_W + g]
        n = tab_ref[tile * TAB_W + N_GROUPS_MOE + g]
        for p in RUN_SIZES:
            done = n & (-2 * p)

            @pl.when((n & p) != 0)
            def _():
                v = vmem_ref.at[pl.ds(pl.multiple_of(off + done, ROUTE_PAD), p)]
                h = hbm_ref.at[pl.ds(pl.multiple_of(start + done, ROUTE_PAD), p)]
                cp = pltpu.make_async_copy(v, h, sem) if to_hbm else pltpu.make_async_copy(h, v, sem)
                if wait:
                    cp.wait()
                else:
                    cp.start()
        off = off + n


def _zero_fill(tab_ref, meta, zeros_ref, hbm_ref, sem, n_rows, wait):
    def copy(rows, dst_row):
        cp = pltpu.make_async_copy(zeros_ref.at[pl.ds(0, rows)],
                                   hbm_ref.at[pl.ds(pl.multiple_of(dst_row, ROUTE_PAD), rows)], sem)
        if wait:
            cp.wait()
        else:
            cp.start()

    end = 0
    for g in range(N_GROUPS_MOE):
        fill = tab_ref[meta + g]
        start = tab_ref[meta + N_GROUPS_MOE + g]
        end = start + _round_up(fill, FFN_BM)
        tail = end - start - fill
        for p in RUN_SIZES:
            pl.when((tail & p) != 0)(functools.partial(copy, p, start + fill + (tail & (-2 * p))))
    for k in range(n_rows // FFN_BM):
        pl.when(end + k * FFN_BM < n_rows)(functools.partial(copy, FFN_BM, end + k * FFN_BM))


def _wait_rows(n, hbm_ref, vmem_ref, sem, to_hbm):
    for p in RUN_SIZES:
        @pl.when((n & p) != 0)
        def _():
            v = vmem_ref.at[pl.ds(0, p)]
            h = hbm_ref.at[pl.ds(0, p)]
            (pltpu.make_async_copy(v, h, sem) if to_hbm else pltpu.make_async_copy(h, v, sem)).wait()


def _tile_rows(tab_ref, tile):
    n = 0
    for g in range(N_GROUPS_MOE):
        n = n + tab_ref[tile * TAB_W + N_GROUPS_MOE + g]
    return n


def _max4(v):
    return jnp.maximum(jnp.maximum(v[0], v[1]), jnp.maximum(v[2], v[3]))


def _first_of4(v, top):
    return jnp.where(v[0] == top, 0.0, jnp.where(v[1] == top, 1.0, jnp.where(v[2] == top, 2.0, 3.0)))


def _route_plan(x, m, plan_in, plan_out, tile0):
    g_ref, wrt_ref, brt_ref, upper_ref = plan_in
    haug_ref, dloc_ref, drow_ref, cnt_ref = plan_out
    rows = x.shape[0]
    tm = ROUTE_TM
    ng = N_GROUPS_MOE
    h = _rms(x, g_ref[...]) * (1.0 + m[4:5]) + m[3:4]
    hb = h.astype(BF16)
    lt = lax.dot_general(wrt_ref[...], hb, NT_DIMS, preferred_element_type=F32) + brt_ref[...]
    gl = [lt[r:r + 1, :] for r in range(ng)]
    g_top = _max4(gl)
    g_idx = _first_of4(gl, g_top)
    g_w = 1.0 / (jnp.exp(gl[0] - g_top) + jnp.exp(gl[1] - g_top) + jnp.exp(gl[2] - g_top) + jnp.exp(gl[3] - g_top))
    ev = []
    for k in range(EXPERTS_PER_GROUP):
        cand = [lt[GATE_OFF + EXPERTS_PER_GROUP * r + k:GATE_OFF + EXPERTS_PER_GROUP * r + k + 1, :]
                for r in range(ng)]
        ev.append(jnp.where(g_idx == 0.0, cand[0], jnp.where(g_idx == 1.0, cand[1],
                            jnp.where(g_idx == 2.0, cand[2], cand[3]))))
    v1 = _max4(ev)
    i1 = _first_of4(ev, v1)
    rest = [jnp.where(i1 == float(k), NEG_BIG, ev[k]) for k in range(EXPERTS_PER_GROUP)]
    v2 = _max4(rest)
    i2 = _first_of4(rest, v2)
    e2 = jnp.exp(v2 - v1)
    w1 = 1.0 / (1.0 + e2)
    w2 = e2 * w1
    gates = [g_w * (jnp.where(i1 == float(k), w1, 0.0) + jnp.where(i2 == float(k), w2, 0.0))
             for k in range(EXPERTS_PER_GROUP)]

    sub8 = lax.broadcasted_iota(jnp.int32, (8, tm), 0).astype(F32)
    dlocs = []
    for sub in range(rows // tm):
        gi = g_idx[:, sub * tm:(sub + 1) * tm]
        hot = jnp.where(sub8 == gi, 1.0, 0.0)
        before = _bdot(hot.astype(BF16), upper_ref[...])
        dl = jnp.sum(before * hot, axis=0, keepdims=True)
        off = 0
        for g in range(ng):
            n_g = _round_up(jnp.sum(hot[g:g + 1, :]).astype(jnp.int32), ROUTE_PAD)
            cnt_ref[(tile0 + sub) * ng + g] = n_g
            dl = dl + jnp.where(gi == float(g), off.astype(F32) if g else 0.0, 0.0)
            off = off + n_g
        drow_ref[sub] = jnp.broadcast_to(dl, (8, tm))
        dlocs.append(dl)
    dloc = jnp.concatenate(dlocs, axis=1)
    d_hi = jnp.floor(dloc * (1.0 / DLOC_RADIX))
    g_hi = [gt.astype(BF16).astype(F32) for gt in gates]
    ex_rows = g_hi + [gt - gh for gt, gh in zip(gates, g_hi)] + [d_hi, dloc - DLOC_RADIX * d_hi]
    sub16 = lax.broadcasted_iota(jnp.int32, (16, rows), 0)
    ex_t = jnp.zeros((16, rows), F32)
    for r, row in enumerate(ex_rows):
        ex_t = jnp.where(sub16 == r, row, ex_t)
    ex_t = jnp.concatenate([ex_t, jnp.zeros((LANES - 16, rows), F32)], axis=0)
    extras = ex_t.T
    dloc_ref[...] = jnp.broadcast_to(
        DLOC_RADIX * extras[:, DLOC_HI:DLOC_HI + 1] + extras[:, DLOC_HI + 1:DLOC_HI + 2], (rows, LANES))
    haug_ref[...] = jnp.concatenate([hb, extras.astype(BF16)], axis=1)


def _plan_consts(g2, w_rt, b_rt):
    upper = jnp.asarray(np.triu(np.ones((ROUTE_TM, ROUTE_TM), np.float32), 1), BF16)
    return [g2, w_rt, b_rt, upper]


def _plan_out(b, s, rows, index):
    tiles = rows // ROUTE_TM
    shapes = [jax.ShapeDtypeStruct((b, s, XS_W), BF16), jax.ShapeDtypeStruct((b, s, LANES), F32),
              jax.ShapeDtypeStruct((b, s // ROUTE_TM, 8, ROUTE_TM), F32),
              jax.ShapeDtypeStruct((b * (s // ROUTE_TM) * N_GROUPS_MOE,), jnp.int32)]
    specs = [pl.BlockSpec((None, rows, XS_W), lambda *idx: (*index(*idx), 0)),
             pl.BlockSpec((None, rows, LANES), lambda *idx: (*index(*idx), 0)),
             pl.BlockSpec((None, tiles, 8, ROUTE_TM), lambda *idx: (*index(*idx), 0, 0)),
             pl.BlockSpec(memory_space=pltpu.SMEM)]
    return shapes, specs


def _flat_plan(plan):
    haug, dloc, drow, cnt = plan
    return (haug.reshape(-1, XS_W), dloc.reshape(-1, LANES), drow.reshape(-1, 8, ROUTE_TM), cnt)


def _dispatch_kernel(cp_ref, cs_ref, hp_ref, hs_ref, dp_ref, ds_ref, tab_ref, sorted_hbm,
                     sorted_ref, zeros_ref, fill_ref, sem_ref, zsem_ref, *, n_steps, n_p_steps):
    i = pl.program_id(0)
    tm = ROUTE_TM
    ng = N_GROUPS_MOE
    n_tiles = n_steps * ROUTE_SUBS
    n_p_tiles = n_p_steps * ROUTE_SUBS
    meta = n_tiles * TAB_W
    n_rows = sorted_hbm.shape[0]

    def count(tile, g):
        if isinstance(tile, int):
            return cp_ref[tile * ng + g] if tile < n_p_tiles else cs_ref[(tile - n_p_tiles) * ng + g]
        return jnp.where(tile < n_p_tiles, cp_ref[jnp.minimum(tile, n_p_tiles - 1) * ng + g],
                         cs_ref[jnp.maximum(tile - n_p_tiles, 0) * ng + g])

    @pl.when(i == 0)
    def _():
        start = 0
        for g in range(ng):
            fill = sum(count(t, g) for t in range(n_tiles))
            tab_ref[meta + g] = fill
            tab_ref[meta + ng + g] = start
            start = start + _round_up(fill, FFN_BM)
            fill_ref[g] = 0
        zeros_ref[...] = jnp.zeros_like(zeros_ref)

    is_prompt = i < n_p_steps
    step_rows = jnp.where(is_prompt, hp_ref[...], hs_ref[...])
    step_drow = jnp.where(is_prompt, dp_ref[...], ds_ref[...])
    row_f = lax.broadcasted_iota(jnp.int32, (ROUTE_ROWS, tm), 0).astype(F32)
    for sub in range(ROUTE_SUBS):
        tile = i * ROUTE_SUBS + sub

        @pl.when(i >= 1)
        def _():
            _wait_rows(_tile_rows(tab_ref, tile - ROUTE_SUBS), sorted_hbm, sorted_ref.at[sub], sem_ref.at[sub], True)

        onehot = jnp.where(row_f == step_drow[sub][0:1, :], 1.0, 0.0).astype(BF16)
        sorted_ref[sub] = _bdot(onehot, step_rows[sub * tm:(sub + 1) * tm])
        for g in range(ng):
            n_g = count(tile, g)
            tab_ref[tile * TAB_W + g] = tab_ref[meta + ng + g] + fill_ref[g]
            tab_ref[tile * TAB_W + ng + g] = n_g
            fill_ref[g] = fill_ref[g] + n_g
        _run_copies(tab_ref, tile, sorted_hbm, sorted_ref.at[sub], sem_ref.at[sub], to_hbm=True, wait=False)

        @pl.when(i == n_steps - 1)
        def _():
            _wait_rows(_tile_rows(tab_ref, tile), sorted_hbm, sorted_ref.at[sub], sem_ref.at[sub], True)

    @pl.when(i == n_steps - 1)
    def _():
        _zero_fill(tab_ref, meta, zeros_ref, sorted_hbm, zsem_ref, n_rows, wait=False)
        _zero_fill(tab_ref, meta, zeros_ref, sorted_hbm, zsem_ref, n_rows, wait=True)


def _merged_specs(n_p_steps, n_s_steps, layer, sample_seq, step_of, width=D_MODEL):
    def p_map(*idx):
        return (jnp.minimum(step_of(*idx), n_p_steps - 1), 0)

    def s_map(*idx):
        return (jnp.clip(step_of(*idx) - n_p_steps, 0, n_s_steps - 1), 0)

    def mod_map(*idx):
        j = step_of(*idx)
        row = jnp.where(j < n_p_steps, 0, 1 + lax.div(jnp.maximum(j - n_p_steps, 0) * ROUTE_STEP, sample_seq))
        return (layer, row, 0, 0)

    return (pl.BlockSpec((ROUTE_STEP, width), p_map), pl.BlockSpec((ROUTE_STEP, width), s_map),
            pl.BlockSpec((None, None, 6, D_MODEL), mod_map))


def _dispatch(plan_p, plan_s, n_rows):
    haug_p, _, drow_p, cnt_p = plan_p
    haug_s, _, drow_s, cnt_s = plan_s
    n_p_steps = haug_p.shape[0] // ROUTE_STEP
    n_s_steps = haug_s.shape[0] // ROUTE_STEP
    n_steps = n_p_steps + n_s_steps
    n_tiles = n_steps * ROUTE_SUBS
    step_of = lambda i, cp, cs: i
    hp_spec, hs_spec, _ = _merged_specs(n_p_steps, n_s_steps, 0, 1, step_of, XS_W)
    drow_block = (ROUTE_SUBS, 8, ROUTE_TM)
    grid_spec = pltpu.PrefetchScalarGridSpec(
        num_scalar_prefetch=2,
        grid=(n_steps,),
        in_specs=[
            hp_spec, hs_spec,
            pl.BlockSpec(drow_block, lambda i, cp, cs: (jnp.minimum(i, n_p_steps - 1), 0, 0)),
            pl.BlockSpec(drow_block, lambda i, cp, cs: (jnp.clip(i - n_p_steps, 0, n_s_steps - 1), 0, 0)),
        ],
        out_specs=[pl.BlockSpec(memory_space=pltpu.SMEM), pl.BlockSpec(memory_space=pl.ANY)],
        scratch_shapes=[
            pltpu.VMEM((ROUTE_SUBS, ROUTE_ROWS, XS_W), F32),
            pltpu.VMEM((FFN_BM, XS_W), F32),
            pltpu.SMEM((N_GROUPS_MOE,), jnp.int32),
            pltpu.SemaphoreType.DMA((ROUTE_SUBS,)),
            pltpu.SemaphoreType.DMA(()),
        ],
    )
    return pl.pallas_call(
        functools.partial(_dispatch_kernel, n_steps=n_steps, n_p_steps=n_p_steps),
        grid_spec=grid_spec,
        out_shape=[
            jax.ShapeDtypeStruct(((n_tiles + 1) * TAB_W,), jnp.int32),
            jax.ShapeDtypeStruct((n_rows, XS_W), F32),
        ],
        compiler_params=_cparams("arbitrary"),
        name="moe_dispatch",
    )(cnt_p, cnt_s, haug_p, haug_s, drow_p, drow_s)


def _ffn_lookup(i, tab_ref, meta):
    fills = [tab_ref[meta + g] for g in range(N_GROUPS_MOE)]
    edges = []
    acc = 0
    for f in fills:
        acc = acc + lax.div(f + (FFN_BM - 1), FFN_BM)
        edges.append(acc)
    total = edges[-1]
    ii = jnp.minimum(i, total - 1)
    grp = sum((ii >= e).astype(jnp.int32) for e in edges[:-1])

    def pick(vals):
        return jnp.where(grp == 0, vals[0], jnp.where(grp == 1, vals[1], jnp.where(grp == 2, vals[2], vals[3])))

    first = pick([0] + edges[:-1])
    return grp, total, ii == first, pick(fills) - (ii - first) * FFN_BM


def _ffn_group_kernel(tab_ref, xs_ref, w1_ref, w3_ref, w2_ref, ys_ref, w1b_ref, w3b_ref, w2b_ref, *, meta):
    i = pl.program_id(0)
    _, total, first_of_group, valid = _ffn_lookup(i, tab_ref, meta)
    active = i < total

    @pl.when(jnp.logical_and(active, first_of_group))
    def _():
        w1b_ref[...] = w1_ref[...].astype(BF16)
        w3b_ref[...] = w3_ref[...].astype(BF16)
        w2b_ref[...] = w2_ref[...].astype(BF16)

    def run(rows):
        hb = xs_ref[0:rows, 0:D_MODEL].astype(BF16)
        ex = xs_ref[0:rows, D_MODEL:XS_W]
        acc = None
        for e in range(EXPERTS_PER_GROUP):
            a = _bdot(hb, w1b_ref[e])
            b = _bdot(hb, w3b_ref[e])
            gate = ex[:, e:e + 1] + ex[:, GATE_LO + e:GATE_LO + e + 1]
            hid = (a * jax.nn.sigmoid(a)) * b * gate
            part = _bdot(hid.astype(BF16), w2b_ref[e])
            acc = part if acc is None else acc + part
        ys_ref[0:rows, :] = acc

    @pl.when(jnp.logical_and(active, valid > FFN_HALF))
    def _():
        run(FFN_BM)

    @pl.when(jnp.logical_and(active, valid <= FFN_HALF))
    def _():
        run(FFN_HALF)
        ys_ref[FFN_HALF:, :] = jnp.zeros((FFN_BM - FFN_HALF, D_MODEL), F32)

    @pl.when(jnp.logical_not(active))
    def _():
        ys_ref[...] = jnp.zeros_like(ys_ref)


def _ffn_group(tab, xs, w1, w3, w2, layer, meta):
    e4 = EXPERTS_PER_GROUP
    n_blocks = xs.shape[0] // FFN_BM
    group_of = lambda i, tab_ref: _ffn_lookup(i, tab_ref, meta)[0]
    grid_spec = pltpu.PrefetchScalarGridSpec(
        num_scalar_prefetch=1,
        grid=(n_blocks,),
        in_specs=[
            pl.BlockSpec((FFN_BM, XS_W), lambda i, tab_ref: (i, 0)),
            pl.BlockSpec((None, e4, D_MODEL, D_EXPERT), lambda i, tab_ref: (layer, group_of(i, tab_ref), 0, 0)),
            pl.BlockSpec((None, e4, D_MODEL, D_EXPERT), lambda i, tab_ref: (layer, group_of(i, tab_ref), 0, 0)),
            pl.BlockSpec((None, e4, D_EXPERT, D_MODEL), lambda i, tab_ref: (layer, group_of(i, tab_ref), 0, 0)),
        ],
        out_specs=pl.BlockSpec((FFN_BM, D_MODEL), lambda i, tab_ref: (i, 0)),
        scratch_shapes=[
            pltpu.VMEM((e4, D_MODEL, D_EXPERT), BF16),
            pltpu.VMEM((e4, D_MODEL, D_EXPERT), BF16),
            pltpu.VMEM((e4, D_EXPERT, D_MODEL), BF16),
        ],
    )
    return pl.pallas_call(
        functools.partial(_ffn_group_kernel, meta=meta),
        grid_spec=grid_spec,
        out_shape=jax.ShapeDtypeStruct((xs.shape[0], D_MODEL), F32),
        compiler_params=_cparams("arbitrary"),
        name="moe_ffn",
    )(tab, xs, w1, w3, w2)


def _fetch_runs(tab_ref, ys_ref, runs_ref, sem_ref, tile, slot):
    _run_copies(tab_ref, tile, ys_ref, runs_ref.at[slot], sem_ref.at[slot], to_hbm=False, wait=False)


def _unsort(tab_ref, ys_ref, runs_ref, sem_ref, dloc_col, tile, slot):
    covered = _tile_rows(tab_ref, tile)
    _wait_rows(covered, ys_ref, runs_ref.at[slot], sem_ref.at[slot], False)
    runs_ref[slot, pl.ds(pl.multiple_of(covered, ROUTE_PAD), LANES), :] = jnp.zeros((LANES, D_MODEL), F32)
    yb = runs_ref[slot, 0:SORT_ROWS, :].astype(BF16)
    row_f = lax.broadcasted_iota(jnp.int32, (ROUTE_TM, SORT_ROWS), 1).astype(F32)
    return _bdot(jnp.where(row_f == dloc_col, 1.0, 0.0).astype(BF16), yb)


def _combine_kernel(tab_ref, *refs, n_steps, first_step, n_p_steps, final_norm):
    with_prompt = first_step < n_p_steps
    if with_prompt:
        xp_ref, xs_ref, mod_ref, dp_ref, ds_ref, fg_ref, ys_ref, op_ref, os_ref, runs_ref, sem_ref = refs
    else:
        xs_ref, mod_ref, ds_ref, fg_ref, ys_ref, os_ref, runs_ref, sem_ref = refs
    i = pl.program_id(0)
    par = lax.rem(i, 2)
    step = first_step + i
    tm = ROUTE_TM

    def fetch(st, parity):
        for sub in range(ROUTE_SUBS):
            _fetch_runs(tab_ref, ys_ref, runs_ref, sem_ref, st * ROUTE_SUBS + sub, parity * ROUTE_SUBS + sub)

    @pl.when(i == 0)
    def _():
        fetch(first_step, 0)

    @pl.when(i + 1 < n_steps)
    def _():
        fetch(step + 1, 1 - par)

    is_prompt = step < n_p_steps
    dloc = jnp.where(is_prompt, dp_ref[:, 0:1], ds_ref[:, 0:1]) if with_prompt else ds_ref[:, 0:1]
    parts = [_unsort(tab_ref, ys_ref, runs_ref, sem_ref, dloc[sub * tm:(sub + 1) * tm],
                     step * ROUTE_SUBS + sub, par * ROUTE_SUBS + sub) for sub in range(ROUTE_SUBS)]
    x = jnp.where(is_prompt, xp_ref[...], xs_ref[...]) if with_prompt else xs_ref[...]
    x2 = x + mod_ref[5:6, :] * jnp.concatenate(parts, axis=0)
    if final_norm:
        x2 = _rms(x2, fg_ref[...])
    if with_prompt:
        @pl.when(is_prompt)
        def _():
            op_ref[...] = x2

        @pl.when(jnp.logical_not(is_prompt))
        def _():
            os_ref[...] = x2
    else:
        os_ref[...] = x2


def _combine(tab, xp, xs, mod, dloc_p, dloc_s, final_g, ys, layer, sample_seq, final_norm, with_prompt):
    n_p_steps = xp.shape[0] // ROUTE_STEP
    n_s_steps = xs.shape[0] // ROUTE_STEP
    first_step = 0 if with_prompt else n_p_steps
    n_steps = n_p_steps + n_s_steps - first_step
    step_of = lambda i, tab_ref: i + first_step
    p_spec, s_spec, mod_spec = _merged_specs(n_p_steps, n_s_steps, layer, sample_seq, step_of)
    dp_spec, ds_spec, _ = _merged_specs(n_p_steps, n_s_steps, layer, sample_seq, step_of, LANES)
    streams = [p_spec, s_spec] if with_prompt else [s_spec]
    dlocs = [dp_spec, ds_spec] if with_prompt else [ds_spec]
    grid_spec = pltpu.PrefetchScalarGridSpec(
        num_scalar_prefetch=1,
        grid=(n_steps,),
        in_specs=streams + [mod_spec] + dlocs + [
            pl.BlockSpec((1, D_MODEL), lambda i, tab_ref: (0, 0)),
            pl.BlockSpec(memory_space=pl.ANY),
        ],
        out_specs=streams,
        scratch_shapes=[
            pltpu.VMEM((2 * ROUTE_SUBS, RUN_ROWS, D_MODEL), F32),
            pltpu.SemaphoreType.DMA((2 * ROUTE_SUBS,)),
        ],
    )
    arrays = [xp, xs] if with_prompt else [xs]
    dloc_arrays = [dloc_p, dloc_s] if with_prompt else [dloc_s]
    return pl.pallas_call(
        functools.partial(_combine_kernel, n_steps=n_steps, first_step=first_step, n_p_steps=n_p_steps,
                          final_norm=final_norm),
        grid_spec=grid_spec,
        out_shape=[jax.ShapeDtypeStruct(a.shape, F32) for a in arrays],
        compiler_params=_cparams("arbitrary"),
        name="moe_combine",
    )(tab, *arrays, mod, *dloc_arrays, final_g, ys)


def _moe(xp, xs, plan_p, plan_s, mod, w1, w3, w2, final_g, layer, sample_seq, final_norm, defer_prompt):
    plan_p, plan_s = _flat_plan(plan_p), _flat_plan(plan_s)
    t = xp.shape[0] + xs.shape[0]
    n_tiles = t // ROUTE_TM
    max_rows = t + N_GROUPS_MOE * (ROUTE_PAD - 1) * n_tiles
    n_rows = (-(-max_rows // FFN_BM) + N_GROUPS_MOE) * FFN_BM
    tab, sorted_x = _dispatch(plan_p, plan_s, n_rows)
    ys = _ffn_group(tab, sorted_x, w1, w3, w2, layer, n_tiles * TAB_W)
    out = _combine(tab, xp, xs, mod, plan_p[1], plan_s[1], final_g, ys, layer, sample_seq, final_norm,
                   not defer_prompt)
    if defer_prompt:
        return None, out[0], (tab, plan_p[1], ys)
    return out[0], out[1], None


def _rope_tables(seq):
    half = QK_ROPE // 2
    nf = half // 2
    inv = ROPE_BASE ** (-np.arange(nf, dtype=np.float64) / nf)
    pos = np.arange(seq)
    row = (pos // GRID_W).astype(np.float64)
    col = (pos % GRID_W).astype(np.float64)
    cos = np.ones((seq, HEAD_PAD), np.float64)
    sin_a = np.zeros((seq, HEAD_PAD), np.float64)
    sin_b = np.zeros((seq, HEAD_PAD), np.float64)
    for part, p in enumerate((row, col)):
        ang = p[:, None] * inv[None, :]
        base = ROPE_OFF + part * half
        cos[:, base:base + nf] = np.cos(ang)
        cos[:, base + nf:base + half] = np.cos(ang)
        sin_a[:, base:base + nf] = -np.sin(ang)
        sin_b[:, base + nf:base + half] = np.sin(ang)
    return tuple(jnp.asarray(a, F32) for a in (cos, sin_a, sin_b))


def _apply_rope(x, cos, sin_a, sin_b, reps):
    nf = QK_ROPE // 4
    width = x.shape[1]
    if reps > 1:
        cos, sin_a, sin_b = (jnp.concatenate([a] * reps, axis=1) for a in (cos, sin_a, sin_b))
    return x * cos + pltpu.roll(x, width - nf, axis=1) * sin_a + pltpu.roll(x, nf, axis=1) * sin_b


def _odd_in_kernel(*refs, rope, emit_cache):
    x_ref, mod_ref, g_ref, w_ref, qg_ref, wq_ref, kg_ref, wk_ref, wv_ref, cs_ref = refs[:10]
    refs = refs[10:]
    if rope:
        cos_ref, sa_ref, sb_ref = refs[:3]
        refs = refs[3:]
    y_ref, q_ref, k_ref, v_ref = refs[:4]
    refs = refs[4:]
    m = mod_ref[...]
    h = _rms(x_ref[...], g_ref[...]) * (1.0 + m[1:2]) + m[0:1]
    z = _bdot(h.astype(BF16), w_ref[...])
    zc = z[:, 0:D_C]
    qc = z[:, D_C:D_C + Q_LORA]
    kvc = z[:, D_C + Q_LORA:D_C + Q_LORA + KV_LORA]
    kpe = z[:, D_C + Q_LORA + KV_LORA:]
    q = _bdot(_rms(qc, qg_ref[...]).astype(BF16), wq_ref[...])
    kvn = _rms(kvc, kg_ref[...])
    if emit_cache:
        ckv_ref, kpe_ref = refs
        ckv_ref[...] = kvn
        kpe_ref[...] = kpe[:, ROPE_OFF:ROPE_OFF + QK_ROPE]
    if rope:
        tabs = (cos_ref[...], sa_ref[...], sb_ref[...])
        q = _apply_rope(q, *tabs, reps=N_HEADS)
        kpe = _apply_rope(kpe, *tabs, reps=1)
    kvb = kvn.astype(BF16)
    k = _bdot(kvb, wk_ref[...]) + jnp.concatenate([kpe] * N_HEADS, axis=1)
    scale = math.log2(math.e) / math.sqrt(QK_NOPE + QK_ROPE)
    q_ref[...] = (q * scale).astype(BF16)
    k_ref[...] = k.astype(BF16)
    v_ref[...] = _bdot(kvb, wv_ref[...]).astype(BF16)
    y = _bdot(zc.astype(BF16), cs_ref[...])
    y_ref[0, :, :] = y[:, 0:D_C].astype(BF16)
    y_ref[1, :, :] = y[:, D_C:2 * D_C].astype(BF16)


def _odd_in(x3, mod, g, w_in, q_g, w_q, kv_g, w_k, w_v, cs, stream, layer, rope_tabs, emit_cache):
    b, s, _ = x3.shape
    tm = min(s, 512)
    n_i = s // tm
    rope = rope_tabs is not None
    const = lambda a: pl.BlockSpec(a.shape, lambda bi, i: (0,) * a.ndim)
    in_specs = [
        pl.BlockSpec((None, tm, D_MODEL), lambda bi, i: (bi, i, 0)),
        _mod_spec(stream, layer),
        const(g), const(w_in), const(q_g), const(w_q), const(kv_g), const(w_k), const(w_v), const(cs),
    ]
    args = [x3, mod, g, w_in, q_g, w_q, kv_g, w_k, w_v, cs]
    if rope:
        in_specs += [pl.BlockSpec((tm, HEAD_PAD), lambda bi, i: (i, 0))] * 3
        args += list(rope_tabs)
    hq = N_HEADS * HEAD_PAD
    out_specs = [
        pl.BlockSpec((None, 2, tm, D_C), lambda bi, i: (bi, 0, i, 0)),
        pl.BlockSpec((None, tm, hq), lambda bi, i: (bi, i, 0)),
        pl.BlockSpec((None, tm, hq), lambda bi, i: (bi, i, 0)),
        pl.BlockSpec((None, tm, N_HEADS * V_DIM), lambda bi, i: (bi, i, 0)),
    ]
    out_shape = [
        jax.ShapeDtypeStruct((b, 2, s, D_C), BF16),
        jax.ShapeDtypeStruct((b, s, hq), BF16),
        jax.ShapeDtypeStruct((b, s, hq), BF16),
        jax.ShapeDtypeStruct((b, s, N_HEADS * V_DIM), BF16),
    ]
    if emit_cache:
        out_specs += [
            pl.BlockSpec((None, tm, KV_LORA), lambda bi, i: (bi, i, 0)),
            pl.BlockSpec((None, tm, QK_ROPE), lambda bi, i: (bi, i, 0)),
        ]
        out_shape += [
            jax.ShapeDtypeStruct((b, s, KV_LORA), F32),
            jax.ShapeDtypeStruct((b, s, QK_ROPE), F32),
        ]
    return pl.pallas_call(
        functools.partial(_odd_in_kernel, rope=rope, emit_cache=emit_cache),
        grid=(b, n_i),
        in_specs=in_specs,
        out_specs=out_specs,
        out_shape=out_shape,
        compiler_params=_cparams("parallel", "parallel"),
        name="odd_in",
    )(*args)


def _cache_kv_kernel(c_ref, p_ref, wk_ref, wv_ref, k_ref, v_ref):
    cb = c_ref[...].astype(BF16)
    k = _bdot(cb, wk_ref[...]) + jnp.concatenate([p_ref[...]] * N_HEADS, axis=1)
    k_ref[...] = k.astype(BF16)
    v_ref[...] = _bdot(cb, wv_ref[...]).astype(BF16)


def _cache_kv(ckv, kpe_blk, w_k, w_v):
    b, p, _ = ckv.shape
    hq = N_HEADS * HEAD_PAD
    return pl.pallas_call(
        _cache_kv_kernel,
        grid=(b,),
        in_specs=[
            pl.BlockSpec((None, p, KV_LORA), lambda bi: (bi, 0, 0)),
            pl.BlockSpec((None, p, HEAD_PAD), lambda bi: (bi, 0, 0)),
            pl.BlockSpec(w_k.shape, lambda bi: (0, 0)),
            pl.BlockSpec(w_v.shape, lambda bi: (0, 0)),
        ],
        out_specs=[
            pl.BlockSpec((None, p, hq), lambda bi: (bi, 0, 0)),
            pl.BlockSpec((None, p, N_HEADS * V_DIM), lambda bi: (bi, 0, 0)),
        ],
        out_shape=[
            jax.ShapeDtypeStruct((b, p, hq), BF16),
            jax.ShapeDtypeStruct((b, p, N_HEADS * V_DIM), BF16),
        ],
        compiler_params=_cparams("parallel"),
        name="cache_kv",
    )(ckv, kpe_blk, w_k, w_v)


def _odd_mix_kernel(*refs, with_cache):
    q_ref, k_ref, v_ref = refs[:3]
    refs = refs[3:]
    if with_cache:
        kc_ref, vc_ref = refs[:2]
        refs = refs[2:]
    plan_tile = None
    if with_cache:
        y_ref, f_ref, x_ref, mod_ref, wo_ref = refs[:5]
        plan_in, (o_ref, *plan_out), (a_ref, vx_ref, vcx_ref) = refs[5:9], refs[9:14], refs[14:]
        plan_tile = pl.program_id(0) * pl.num_programs(1) + pl.program_id(1)

        @pl.when(pl.program_id(1) == 0)
        def _():
            for src, dst in ((v_ref, vx_ref), (vc_ref, vcx_ref)):
                one = lax.broadcasted_iota(jnp.int32, (src.shape[0], PAIR_W - DEN_COL), 1) == 0
                for pair in range(N_HEADS // 2):
                    dst[:, pair * PAIR_W:pair * PAIR_W + DEN_COL] = src[:, pair * DEN_COL:(pair + 1) * DEN_COL]
                    dst[:, pair * PAIR_W + DEN_COL:(pair + 1) * PAIR_W] = jnp.where(one, 1.0, 0.0).astype(BF16)
    else:
        y_ref, f_ref, x_ref, mod_ref, wo_ref, o_ref, a_ref = refs[:7]
        if len(refs) > 7:
            plan_in, plan_out, plan_tile = refs[7:]
    tq = q_ref.shape[0]
    lane = lax.broadcasted_iota(jnp.int32, (tq, 2 * V_DIM), 1)
    for pair in range(N_HEADS // 2):
        vcols = slice(pair * PAIR_W, (pair + 1) * PAIR_W)
        outs = []
        for h in (2 * pair, 2 * pair + 1):
            hcols = slice(h * HEAD_PAD, (h + 1) * HEAD_PAD)
            qh = q_ref[:, hcols]
            s = lax.dot_general(qh, k_ref[:, hcols], NT_DIMS, preferred_element_type=F32)
            top = jnp.max(s, axis=-1, keepdims=True)
            if with_cache:
                sc = lax.dot_general(qh, kc_ref[:, hcols], NT_DIMS, preferred_element_type=F32)
                top = jnp.maximum(top, jnp.max(sc, axis=-1, keepdims=True))
            if with_cache:
                acc = _bdot(jnp.exp2((s - top).astype(BF16)), vx_ref[:, vcols])
                acc = acc + _bdot(jnp.exp2((sc - top).astype(BF16)), vcx_ref[:, vcols])
                outs.append(acc[:, 0:2 * V_DIM] / acc[:, DEN_COL:DEN_COL + 1])
            else:
                p = jnp.exp2(s - top)
                den = jnp.sum(p, axis=-1, keepdims=True)
                outs.append(_bdot(p.astype(BF16), v_ref[:, pair * DEN_COL:(pair + 1) * DEN_COL]) / den)
        a_ref[:, pair * 2 * V_DIM:(pair + 1) * 2 * V_DIM] = jnp.where(lane < V_DIM, outs[0], outs[1]).astype(BF16)
    f = _bdot(f_ref[...], y_ref[...].reshape(-1, D_C))
    o = _bdot(f.astype(BF16), wo_ref[0:D_C, :]) + _bdot(a_ref[...], wo_ref[D_C:, :])
    x1 = x_ref[...] + mod_ref[2:3, :] * o
    o_ref[...] = x1
    if plan_tile is not None:
        _route_plan(x1, mod_ref[...], plan_in, plan_out, plan_tile)


def _odd_whole_kernel(tab_ref, x_ref, mod_ref, g_ref, w_ref, qg_ref, wq_ref, kg_ref, wk_ref, wv_ref, cs_ref, f_ref,
                      wo_ref, modp_ref, dloc_ref, ys_ref, pg_ref, pw_ref, pb_ref, pu_ref,
                      o_ref, ckv_ref, kpe_ref, ph_ref, pd_ref, pr_ref, pc_ref,
                      y_ref, q_ref, k_ref, v_ref, a_ref, x2_ref, runs_ref, sem_ref):
    bi = pl.program_id(0)
    par = lax.rem(bi, 2)

    @pl.when(bi == 0)
    def _():
        _fetch_runs(tab_ref, ys_ref, runs_ref, sem_ref, 0, 0)

    @pl.when(bi + 1 < pl.num_programs(0))
    def _():
        _fetch_runs(tab_ref, ys_ref, runs_ref, sem_ref, bi + 1, 1 - par)

    moe = _unsort(tab_ref, ys_ref, runs_ref, sem_ref, dloc_ref[:, 0:1], bi, par)
    x2_ref[...] = x_ref[...] + modp_ref[5:6, :] * moe
    _odd_in_kernel(x2_ref, mod_ref, g_ref, w_ref, qg_ref, wq_ref, kg_ref, wk_ref, wv_ref, cs_ref,
                   y_ref, q_ref, k_ref, v_ref, ckv_ref, kpe_ref, rope=False, emit_cache=True)
    _odd_mix_kernel(q_ref, k_ref, v_ref, y_ref, f_ref, x2_ref, mod_ref, wo_ref, o_ref, a_ref,
                    (pg_ref, pw_ref, pb_ref, pu_ref), (ph_ref, pd_ref, pr_ref, pc_ref), bi, with_cache=False)


def _odd_whole(x3, pending, mod, g, w_in, q_g, w_q, kv_g, w_k, w_v, cs, fmat, w_out, plan_consts, stream, layer):
    tab, dloc, ys = pending
    b, s, _ = x3.shape
    assert s == ROUTE_TM, "one sequence must be one MoE sort tile"
    hq = N_HEADS * HEAD_PAD
    hv = N_HEADS * V_DIM
    const = lambda a: pl.BlockSpec(a.shape, lambda bi, tab_ref: (0,) * a.ndim)
    row_block = lambda w: pl.BlockSpec((None, s, w), lambda bi, tab_ref: (bi, 0, 0))
    mod_block = lambda lyr: pl.BlockSpec((None, None, 6, D_MODEL),
                                         lambda bi, tab_ref: (lyr, stream.row_of_batch(bi), 0, 0))
    plan_shapes, plan_specs = _plan_out(b, s, s, lambda bi, tab_ref: (bi, 0))
    grid_spec = pltpu.PrefetchScalarGridSpec(
        num_scalar_prefetch=1,
        grid=(b,),
        in_specs=[
            row_block(D_MODEL), mod_block(layer),
            const(g), const(w_in), const(q_g), const(w_q), const(kv_g), const(w_k), const(w_v), const(cs),
            const(fmat), const(w_out),
            mod_block(layer - 1),
            pl.BlockSpec((s, LANES), lambda bi, tab_ref: (bi, 0)),
            pl.BlockSpec(memory_space=pl.ANY),
        ] + [const(a) for a in plan_consts],
        out_specs=[row_block(D_MODEL), row_block(KV_LORA), row_block(QK_ROPE)] + plan_specs,
        scratch_shapes=[
            pltpu.VMEM((2, s, D_C), BF16),
            pltpu.VMEM((s, hq), BF16),
            pltpu.VMEM((s, hq), BF16),
            pltpu.VMEM((s, hv), BF16),
            pltpu.VMEM((s, hv), BF16),
            pltpu.VMEM((s, D_MODEL), F32),
            pltpu.VMEM((2, RUN_ROWS, D_MODEL), F32),
            pltpu.SemaphoreType.DMA((2,)),
        ],
    )
    out = pl.pallas_call(
        _odd_whole_kernel,
        grid_spec=grid_spec,
        out_shape=[
            jax.ShapeDtypeStruct((b, s, D_MODEL), F32),
            jax.ShapeDtypeStruct((b, s, KV_LORA), F32),
            jax.ShapeDtypeStruct((b, s, QK_ROPE), F32),
        ] + plan_shapes,
        compiler_params=_cparams("arbitrary"),
        name="odd_whole",
    )(tab, x3, mod, g, w_in, q_g, w_q, kv_g, w_k, w_v, cs, fmat, w_out, mod, dloc, ys, *plan_consts)
    return out[0], out[1], out[2], out[3:]


def _odd_mix(q, k, v, kc, vc, y, fmat, x3, mod, w_out, plan_consts, stream, layer):
    b, s, hq = q.shape
    tq = min(s, 256)
    n_i = s // tq
    with_cache = kc is not None
    hv = N_HEADS * V_DIM
    mode = dict(pipeline_mode=pl.Buffered(1)) if with_cache else {}

    def per_batch(rows, cols):
        return pl.BlockSpec((None, rows, cols), lambda bi, i: (bi, 0, 0), **mode)

    in_specs = [pl.BlockSpec((None, tq, hq), lambda bi, i: (bi, i, 0)), per_batch(s, hq), per_batch(s, hv)]
    args = [q, k, v]
    if with_cache:
        p = kc.shape[1]
        in_specs += [per_batch(p, hq), per_batch(p, hv)]
        args += [kc, vc]
    in_specs += [
        per_batch(2 * s, D_C),
        pl.BlockSpec((tq, 2 * s), lambda bi, i: (i, 0)),
        pl.BlockSpec((None, tq, D_MODEL), lambda bi, i: (bi, i, 0)),
        _mod_spec(stream, layer),
        pl.BlockSpec(w_out.shape, lambda bi, i: (0, 0), **mode),
    ]
    in_specs += [pl.BlockSpec(a.shape, lambda bi, i: (0,) * a.ndim) for a in plan_consts]
    args += [y, fmat, x3, mod, w_out] + plan_consts
    plan_shapes, plan_specs = _plan_out(b, s, tq, lambda bi, i: (bi, i))
    out = pl.pallas_call(
        functools.partial(_odd_mix_kernel, with_cache=with_cache),
        grid=(b, n_i),
        in_specs=in_specs,
        out_specs=[pl.BlockSpec((None, tq, D_MODEL), lambda bi, i: (bi, i, 0))] + plan_specs,
        out_shape=[jax.ShapeDtypeStruct((b, s, D_MODEL), F32)] + plan_shapes,
        scratch_shapes=[pltpu.VMEM((tq, hv), BF16)] + (
            [pltpu.VMEM((s, VX_W), BF16), pltpu.VMEM((kc.shape[1], VX_W), BF16)] if with_cache else []),
        compiler_params=_cparams("arbitrary", "arbitrary"),
        name="odd_mix",
    )(*args)
    return out[0], out[1:]


def _dft_tables(seq):
    jc = np.arange(C_GW)
    ang_c = 2.0 * np.pi * np.outer(jc, jc) / C_GW
    eye = np.eye(C_GROUPS)
    cs = np.concatenate([np.kron(eye, np.cos(ang_c)), np.kron(eye, np.sin(ang_c))], axis=1)
    jn = np.arange(seq)
    ang_n = 2.0 * np.pi * (np.outer(jn, jn) % seq) / seq
    scale = 1.0 / math.sqrt(seq * C_GW)
    fmat = np.concatenate([np.cos(ang_n), -np.sin(ang_n)], axis=1) * scale
    return jnp.asarray(cs, F32).astype(BF16), jnp.asarray(fmat, F32).astype(BF16)


def _odd_weights(w_in, w_uq, w_ukv):
    d = w_in.shape[0]
    base = D_C + Q_LORA + KV_LORA
    kpe_blk = jnp.zeros((d, HEAD_PAD), w_in.dtype).at[:, ROPE_OFF:ROPE_OFF + QK_ROPE].set(w_in[:, base:])
    w_in_p = jnp.concatenate([w_in[:, :base], kpe_blk], axis=1).astype(BF16)
    qh = w_uq.reshape(Q_LORA, N_HEADS, QK_NOPE + QK_ROPE)
    w_q = jnp.pad(qh, ((0, 0), (0, 0), (0, HEAD_PAD - QK_NOPE - QK_ROPE))).reshape(Q_LORA, -1).astype(BF16)
    kvh = w_ukv.reshape(KV_LORA, N_HEADS, QK_NOPE + V_DIM)
    w_k = jnp.pad(kvh[:, :, :QK_NOPE], ((0, 0), (0, 0), (0, HEAD_PAD - QK_NOPE))).reshape(KV_LORA, -1)
    w_v = kvh[:, :, QK_NOPE:].reshape(KV_LORA, -1)
    return w_in_p, w_q, w_k.astype(BF16), w_v.astype(BF16)


ROUTER_ROWS = 32


def _router_weights(wg, bg, we, be):
    d = wg.shape[0]
    w = jnp.concatenate([wg, we.reshape(d, N_EXPERTS)], axis=1).T
    w = jnp.pad(w, ((0, ROUTER_ROWS - w.shape[0]), (0, 0))).astype(BF16)
    b = jnp.concatenate([bg, be.reshape(N_EXPERTS)])
    b = jnp.pad(b, (0, ROUTER_ROWS - b.shape[0])).reshape(ROUTER_ROWS, 1).astype(F32)
    return w, b


def kernel(x_prompt, x_sample, cache_ckv, cache_kpe, c, c_ctx, mod_w, mod_b, norm1_g, norm2_g,
           ev_w_in, ev_conv_w, ev_sgu_norm_g, ev_sgu_w, ev_sgu_b, ev_w_out,
           od_w_in, od_q_norm_g, od_w_uq, od_kv_norm_g, od_w_ukv, od_w_out,
           moe_wg, moe_bg, moe_we, moe_be, moe_w1, moe_w3, moe_w2, final_norm_g):
    bp, n_p, d = x_prompt.shape
    bs, n_s, _ = x_sample.shape
    streams = [(_Stream(bp, n_p, True), x_prompt), (_Stream(bs, n_s, False), x_sample)]

    n_rows = 1 + bs
    cond_t = jnp.concatenate([c_ctx[None, :], c], axis=0).T
    mod = _adaln(cond_t, mod_w, mod_b, n_rows)

    final_g = final_norm_g.reshape(1, d)
    xs = [x for _, x in streams]
    new_ckv, new_kpe = [], []
    pending = None
    plans = [None, None]
    for l in range(DEPTH):
        j = l // 2
        g1 = norm1_g[l].reshape(1, d)
        g2 = norm2_g[l].reshape(1, d)
        plan_consts = _plan_consts(g2, *_router_weights(moe_wg[l], moe_bg[l], moe_we[l], moe_be[l]))
        last = l == DEPTH - 1
        if l % 2 == 0:
            w_in = ev_w_in[j].astype(BF16)
            w_out = ev_w_out[j].astype(BF16)
            sgu_w = ev_sgu_w[j].astype(BF16)
            sgu_g = ev_sgu_norm_g[j].reshape(1, D_B)
            sgu_bias = jnp.repeat(ev_sgu_b[j].T, D_B // B_GROUPS, axis=1)
            for si, (st, _) in enumerate(streams):
                xs[si], plans[si] = _even_layer(xs[si], mod, g1, w_in, ev_conv_w[j], sgu_g, sgu_w, sgu_bias, w_out,
                                                plan_consts, st, l)
        else:
            w_in, w_q, w_k, w_v = _odd_weights(od_w_in[j], od_w_uq[j], od_w_ukv[j])
            w_out = od_w_out[j].astype(BF16)
            q_g = od_q_norm_g[j].reshape(1, Q_LORA)
            kv_g = od_kv_norm_g[j].reshape(1, KV_LORA)
            for si, (st, _) in enumerate(streams):
                x3 = xs[si]
                cs, fmat = _dft_tables(st.seq)
                if st.shared_cond:
                    xs[si], ckv, kpe, plans[si] = _odd_whole(x3, pending, mod, g1, w_in, q_g, w_q, kv_g, w_k, w_v, cs,
                                                             fmat, w_out, plan_consts, st, l)
                    new_ckv.append(ckv)
                    new_kpe.append(kpe)
                    continue
                y, q, k, v = _odd_in(x3, mod, g1, w_in, q_g, w_q, kv_g, w_k, w_v, cs, st, l,
                                     _rope_tables(st.seq), False)
                kpe_blk = jnp.pad(cache_kpe[:, j], ((0, 0), (0, 0), (ROPE_OFF, HEAD_PAD - ROPE_OFF - QK_ROPE)))
                kc, vc = _cache_kv(cache_ckv[:, j], kpe_blk, w_k, w_v)
                xs[si], plans[si] = _odd_mix(q, k, v, kc, vc, y.reshape(st.batch, 2 * st.seq, D_C), fmat, x3, mod,
                                             w_out, plan_consts, st, l)
        defer = not last and (l + 1) % 2 == 1
        x2p, x2s, pending = _moe(xs[0].reshape(bp * n_p, d), xs[1].reshape(bs * n_s, d), plans[0], plans[1], mod,
                                 moe_w1, moe_w3, moe_w2, final_g, l, n_s, last, defer)
        xs = [xs[0] if defer else x2p.reshape(bp, n_p, d), x2s.reshape(bs, n_s, d)]
    return (xs[0], xs[1], jnp.stack(new_ckv, axis=1), jnp.stack(new_kpe, axis=1))
```

```python
import functools
import math

import numpy as np
import jax
import jax.numpy as jnp
from jax import lax
from jax.experimental import pallas as pl
from jax.experimental.pallas import tpu as pltpu

D_MODEL = 1024
DEPTH = 2
GRID_W = 64
D_A = D_MODEL // 2
D_B = D_MODEL // 2
B_GROUPS = 4
CHUNK = 128
D_EVEN_IN = 3 * D_A + 2 * D_B
D_C = D_MODEL // 4
C_GROUPS = 4
C_GW = D_C // C_GROUPS
N_HEADS = 12
QK_NOPE = 64
QK_ROPE = 32
V_DIM = 64
Q_LORA = 384
KV_LORA = 256
ROPE_BASE = 10000.0
N_GROUPS_MOE = 4
EXPERTS_PER_GROUP = 4
N_EXPERTS = N_GROUPS_MOE * EXPERTS_PER_GROUP
D_EXPERT = 256
EPS = 1e-6

LANES = 128
HEAD_PAD = 128
PAIR_W = 256
DEN_COL = 2 * 64
VX_W = 6 * PAIR_W
ROPE_OFF = QK_NOPE
GATE_OFF = N_GROUPS_MOE
NEG_BIG = -1e30
F32 = jnp.float32
BF16 = jnp.bfloat16
VMEM_LIMIT = 56 * 1024 * 1024


def _cparams(*sem):
    return pltpu.CompilerParams(dimension_semantics=sem, vmem_limit_bytes=VMEM_LIMIT)


def _rms(x, g):
    return x * lax.rsqrt(jnp.mean(x * x, axis=-1, keepdims=True) + EPS) * g


def _norm_mod(x, g, scale, shift):
    rs = lax.rsqrt(jnp.mean(x * x, axis=-1, keepdims=True) + EPS)
    return x * rs * (g * (1.0 + scale)) + shift


def _bdot(a, b):
    return jnp.dot(a, b, preferred_element_type=F32)


NT_DIMS = (((1,), (1,)), ((), ()))
TN_DIMS = (((0,), (0,)), ((), ()))


def _mod_kernel(ct_ref, w_ref, b_ref, o_ref):
    c = ct_ref[...]
    s = c * jax.nn.sigmoid(c)
    w = w_ref[...]
    b = b_ref[...]
    for r in range(o_ref.shape[0]):
        o_ref[r:r + 1, :] = jnp.sum(s[:, r:r + 1] * w, axis=0, keepdims=True) + b


def _adaln(cond_t, mod_w, mod_b, n_rows):
    nt = 2048
    d6 = mod_w.shape[-1]
    out = pl.pallas_call(
        _mod_kernel,
        grid=(DEPTH, d6 // nt),
        in_specs=[
            pl.BlockSpec(cond_t.shape, lambda l, n: (0, 0)),
            pl.BlockSpec((None, D_MODEL, nt), lambda l, n: (l, 0, n)),
            pl.BlockSpec((None, 1, nt), lambda l, n: (l, 0, n)),
        ],
        out_specs=pl.BlockSpec((None, n_rows, nt), lambda l, n: (l, 0, n)),
        out_shape=jax.ShapeDtypeStruct((DEPTH, n_rows, d6), F32),
        compiler_params=_cparams("parallel", "parallel"),
        name="adaln",
    )(cond_t, mod_w, mod_b.reshape(DEPTH, 1, d6))
    return out.reshape(DEPTH, n_rows, 6, D_MODEL)


class _Stream:
    def __init__(self, batch, seq, shared_cond):
        self.batch = batch
        self.seq = seq
        self.tokens = batch * seq
        self.shared_cond = shared_cond

    def row_of_batch(self, b):
        return 0 if self.shared_cond else b + 1

    def row_of_tile(self, i, tm):
        return 0 if self.shared_cond else (i * tm) // self.seq + 1


def _mod_spec(stream, layer, tm=None):
    if tm is None:
        return pl.BlockSpec((None, None, 6, D_MODEL), lambda b, i: (layer, stream.row_of_batch(b), 0, 0))
    return pl.BlockSpec((None, None, 6, D_MODEL), lambda i, *_: (layer, stream.row_of_tile(i, tm), 0, 0))


HALO = 8


def _even_kernel(*refs, has_halo, seq_rows, nc):
    if has_halo:
        x_ref, xp_ref, xn_ref = refs[:3]
        refs = refs[3:]
    else:
        x_ref = refs[0]
        refs = refs[1:]
    mod_ref, g_ref, wi_ref, cw_ref, sg_ref, sw_ref, sb_ref, wo_ref = refs[:8]
    plan_in, (o_ref, *plan_out), (z_ref, y_ref) = refs[8:12], refs[12:17], refs[17:]
    i = pl.program_id(1)
    n_i = pl.num_programs(1)
    ts = x_ref.shape[0]
    m = mod_ref[...]
    g = g_ref[...]

    def modulate(x):
        return _norm_mod(x, g, m[1:2], m[0:1]).astype(BF16)

    x = x_ref[...]
    hb = modulate(x)
    for n in range(D_EVEN_IN // nc):
        z_ref[:, n * nc:(n + 1) * nc] = _bdot(hb, wi_ref[:, n * nc:(n + 1) * nc]).astype(BF16)

    gate_b = z_ref[:, 0:D_A].astype(F32)
    gate_c = z_ref[:, D_A:2 * D_A].astype(F32)
    xa = z_ref[:, 2 * D_A:3 * D_A].astype(F32)
    t = gate_c * xa
    t_prev = pltpu.roll(t, 1, axis=0)
    t_next = pltpu.roll(t, ts - 1, axis=0)
    row = lax.broadcasted_iota(jnp.int32, (ts, 1), 0) & (seq_rows - 1)
    if has_halo:
        hh = modulate(jnp.concatenate([xp_ref[...], xn_ref[...]], axis=0))
        zh = _bdot(hh, wi_ref[:, D_A:3 * D_A]).astype(BF16).astype(F32)
        th = zh[:, 0:D_A] * zh[:, D_A:2 * D_A]
        tp = th[HALO - 1:HALO] * (i > 0).astype(F32)
        tn = th[HALO:HALO + 1] * (i < n_i - 1).astype(F32)
    else:
        tp = tn = 0.0
    t_prev = jnp.where(row == 0, tp, t_prev)
    t_next = jnp.where(row == seq_rows - 1, tn, t_next)
    cw = cw_ref[...]
    y_a = gate_b * (t_prev * cw[0:1] + t * cw[1:2] + t_next * cw[2:3])
    y_ref[:, 0:D_A] = y_a.astype(BF16)

    u = z_ref[:, 3 * D_A:3 * D_A + D_B].astype(F32)
    v = z_ref[:, 3 * D_A + D_B:3 * D_A + 2 * D_B].astype(F32)
    vb = _rms(v, sg_ref[...]).astype(BF16)
    gw = D_B // B_GROUPS
    for c in range(ts // CHUNK):
        rows = slice(c * CHUNK, (c + 1) * CHUNK)
        for gi in range(B_GROUPS):
            cols = slice(gi * gw, (gi + 1) * gw)
            sv = _bdot(sw_ref[gi], vb[rows, cols]) + sb_ref[:, cols]
            y_ref[rows, D_A + gi * gw:D_A + (gi + 1) * gw] = (u[rows, cols] * sv).astype(BF16)

    x1 = x + m[2:3] * _bdot(y_ref[...], wo_ref[...])
    o_ref[...] = x1
    _route_plan(x1, m, plan_in, plan_out, (pl.program_id(0) * n_i + i) * (ts // ROUTE_TM))


def _even_layer(x3, mod, g, w_in, conv_w, sgu_g, sgu_w, sgu_bias, w_out, plan_consts, stream, layer):
    out_shape = x3.shape
    b, s, _ = x3.shape
    ts = min(s, 256) if stream.shared_cond else min(s, 512)
    n_i = s // ts
    has_halo = n_i > 1
    seq_rows = ts
    if not has_halo and stream.shared_cond and b % 2 == 0:
        b, s, ts = b // 2, 2 * s, 2 * ts
        x3 = x3.reshape(b, s, D_MODEL)
    hb = ts // HALO
    last_h = s // HALO - 1
    const = lambda a: pl.BlockSpec(a.shape, lambda bi, i: (0,) * a.ndim)
    in_specs = [pl.BlockSpec((None, ts, D_MODEL), lambda bi, i: (bi, i, 0))]
    args = [x3]
    if has_halo:
        in_specs += [
            pl.BlockSpec((None, HALO, D_MODEL), lambda bi, i: (bi, jnp.maximum(i * hb - 1, 0), 0)),
            pl.BlockSpec((None, HALO, D_MODEL), lambda bi, i: (bi, jnp.minimum((i + 1) * hb, last_h), 0)),
        ]
        args += [x3, x3]
    in_specs += [_mod_spec(stream, layer), const(g), const(w_in), const(conv_w), const(sgu_g), const(sgu_w),
                 const(sgu_bias), const(w_out)] + [const(a) for a in plan_consts]
    args += [mod, g, w_in, conv_w, sgu_g, sgu_w, sgu_bias, w_out] + plan_consts
    plan_shapes, plan_specs = _plan_out(b, s, ts, lambda bi, i: (bi, i))
    out = pl.pallas_call(
        functools.partial(_even_kernel, has_halo=has_halo, seq_rows=seq_rows, nc=512),
        grid=(b, n_i),
        in_specs=in_specs,
        out_specs=[pl.BlockSpec((None, ts, D_MODEL), lambda bi, i: (bi, i, 0))] + plan_specs,
        out_shape=[jax.ShapeDtypeStruct((b, s, D_MODEL), F32)] + plan_shapes,
        scratch_shapes=[pltpu.VMEM((ts, D_EVEN_IN), BF16), pltpu.VMEM((ts, D_A + D_B), BF16)],
        compiler_params=_cparams("arbitrary", "arbitrary"),
        name="even_layer",
    )(*args)
    return out[0].reshape(out_shape), out[1:]


ROUTE_TM = 256
ROUTE_SUBS = 4
ROUTE_STEP = ROUTE_TM * ROUTE_SUBS
ROUTE_PAD = 8
SORT_ROWS = ROUTE_TM + LANES
RUN_ROWS = SORT_ROWS + 32
ROUTE_ROWS = ROUTE_TM + 4 * ROUTE_PAD
XS_W = D_MODEL + LANES
GATE_LO = EXPERTS_PER_GROUP
DLOC_HI = 2 * EXPERTS_PER_GROUP
DLOC_RADIX = 16.0
FFN_BM = 512
FFN_HALF = FFN_BM // 2
RUN_SIZES = (256, 128, 64, 32, 16, 8)
TAB_W = 2 * N_GROUPS_MOE


def _round_up(x, m):
    return lax.div(x + (m - 1), m) * m


def _run_copies(tab_ref, tile, hbm_ref, vmem_ref, sem, to_hbm, wait):
    off = 0
    for g in range(N_GROUPS_MOE):
        start = tab_ref[tile * TAB_W + g]
        n = tab_ref[tile * TAB_W + N_GROUPS_MOE + g]
        for p in RUN_SIZES:
            done = n & (-2 * p)

            @pl.when((n & p) != 0)
            def _():
                v = vmem_ref.at[pl.ds(pl.multiple_of(off + done, ROUTE_PAD), p)]
                h = hbm_ref.at[pl.ds(pl.multiple_of(start + done, ROUTE_PAD), p)]
                cp = pltpu.make_async_copy(v, h, sem) if to_hbm else pltpu.make_async_copy(h, v, sem)
                if wait:
                    cp.wait()
                else:
                    cp.start()
        off = off + n


def _zero_fill(tab_ref, meta, zeros_ref, hbm_ref, sem, n_rows, wait):
    def copy(rows, dst_row):
        cp = pltpu.make_async_copy(zeros_ref.at[pl.ds(0, rows)],
                                   hbm_ref.at[pl.ds(pl.multiple_of(dst_row, ROUTE_PAD), rows)], sem)
        if wait:
            cp.wait()
        else:
            cp.start()

    end = 0
    for g in range(N_GROUPS_MOE):
        fill = tab_ref[meta + g]
        start = tab_ref[meta + N_GROUPS_MOE + g]
        end = start + _round_up(fill, FFN_BM)
        tail = end - start - fill
        for p in RUN_SIZES:
            pl.when((tail & p) != 0)(functools.partial(copy, p, start + fill + (tail & (-2 * p))))
    for k in range(n_rows // FFN_BM):
        pl.when(end + k * FFN_BM < n_rows)(functools.partial(copy, FFN_BM, end + k * FFN_BM))


def _wait_rows(n, hbm_ref, vmem_ref, sem, to_hbm):
    for p in RUN_SIZES:
        @pl.when((n & p) != 0)
        def _():
            v = vmem_ref.at[pl.ds(0, p)]
            h = hbm_ref.at[pl.ds(0, p)]
            (pltpu.make_async_copy(v, h, sem) if to_hbm else pltpu.make_async_copy(h, v, sem)).wait()


def _tile_rows(tab_ref, tile):
    n = 0
    for g in range(N_GROUPS_MOE):
        n = n + tab_ref[tile * TAB_W + N_GROUPS_MOE + g]
    return n


def _max4(v):
    return jnp.maximum(jnp.maximum(v[0], v[1]), jnp.maximum(v[2], v[3]))


def _first_of4(v, top):
    return jnp.where(v[0] == top, 0.0, jnp.where(v[1] == top, 1.0, jnp.where(v[2] == top, 2.0, 3.0)))


def _route_plan(x, m, plan_in, plan_out, tile0):
    g_ref, wrt_ref, brt_ref, upper_ref = plan_in
    haug_ref, dloc_ref, drow_ref, cnt_ref = plan_out
    rows = x.shape[0]
    tm = ROUTE_TM
    ng = N_GROUPS_MOE
    h = _norm_mod(x, g_ref[...], m[4:5], m[3:4])
    hb = h.astype(BF16)
    lt = lax.dot_general(wrt_ref[...], hb, NT_DIMS, preferred_element_type=F32) + brt_ref[...]
    gl = [lt[r:r + 1, :] for r in range(ng)]
    g_top = _max4(gl)
    g_idx = _first_of4(gl, g_top)
    g_w = 1.0 / (jnp.exp(gl[0] - g_top) + jnp.exp(gl[1] - g_top) + jnp.exp(gl[2] - g_top) + jnp.exp(gl[3] - g_top))
    ev = []
    for k in range(EXPERTS_PER_GROUP):
        cand = [lt[GATE_OFF + EXPERTS_PER_GROUP * r + k:GATE_OFF + EXPERTS_PER_GROUP * r + k + 1, :]
                for r in range(ng)]
        ev.append(jnp.where(g_idx == 0.0, cand[0], jnp.where(g_idx == 1.0, cand[1],
                            jnp.where(g_idx == 2.0, cand[2], cand[3]))))
    v1 = _max4(ev)
    i1 = _first_of4(ev, v1)
    rest = [jnp.where(i1 == float(k), NEG_BIG, ev[k]) for k in range(EXPERTS_PER_GROUP)]
    v2 = _max4(rest)
    i2 = _first_of4(rest, v2)
    e2 = jnp.exp(v2 - v1)
    w1 = 1.0 / (1.0 + e2)
    w2 = e2 * w1
    gates = [g_w * (jnp.where(i1 == float(k), w1, 0.0) + jnp.where(i2 == float(k), w2, 0.0))
             for k in range(EXPERTS_PER_GROUP)]

    sub8 = lax.broadcasted_iota(jnp.int32, (8, tm), 0).astype(F32)
    dlocs = []
    for sub in range(rows // tm):
        gi = g_idx[:, sub * tm:(sub + 1) * tm]
        hot = jnp.where(sub8 == gi, 1.0, 0.0)
        before = _bdot(hot.astype(BF16), upper_ref[...])
        dl = jnp.sum(before * hot, axis=0, keepdims=True)
        off = 0
        for g in range(ng):
            n_g = _round_up(jnp.sum(hot[g:g + 1, :]).astype(jnp.int32), ROUTE_PAD)
            cnt_ref[(tile0 + sub) * ng + g] = n_g
            dl = dl + jnp.where(gi == float(g), off.astype(F32) if g else 0.0, 0.0)
            off = off + n_g
        drow_ref[sub] = jnp.broadcast_to(dl, (8, tm))
        dlocs.append(dl)
    dloc = jnp.concatenate(dlocs, axis=1)
    d_hi = jnp.floor(dloc * (1.0 / DLOC_RADIX))
    g_hi = [gt.astype(BF16).astype(F32) for gt in gates]
    ex_rows = g_hi + [gt - gh for gt, gh in zip(gates, g_hi)] + [d_hi, dloc - DLOC_RADIX * d_hi]
    sub16 = lax.broadcasted_iota(jnp.int32, (16, rows), 0)
    ex_t = jnp.zeros((16, rows), F32)
    for r, row in enumerate(ex_rows):
        ex_t = jnp.where(sub16 == r, row, ex_t)
    ex_t = jnp.concatenate([ex_t, jnp.zeros((LANES - 16, rows), F32)], axis=0)
    extras = ex_t.T
    dloc_ref[...] = jnp.broadcast_to(
        DLOC_RADIX * extras[:, DLOC_HI:DLOC_HI + 1] + extras[:, DLOC_HI + 1:DLOC_HI + 2], (rows, LANES))
    haug_ref[...] = jnp.concatenate([hb, extras.astype(BF16)], axis=1)


def _plan_consts(g2, w_rt, b_rt):
    upper = jnp.asarray(np.triu(np.ones((ROUTE_TM, ROUTE_TM), np.float32), 1), BF16)
    return [g2, w_rt, b_rt, upper]


def _plan_out(b, s, rows, index):
    tiles = rows // ROUTE_TM
    shapes = [jax.ShapeDtypeStruct((b, s, XS_W), BF16), jax.ShapeDtypeStruct((b, s, LANES), F32),
              jax.ShapeDtypeStruct((b, s // ROUTE_TM, 8, ROUTE_TM), F32),
              jax.ShapeDtypeStruct((b * (s // ROUTE_TM) * N_GROUPS_MOE,), jnp.int32)]
    specs = [pl.BlockSpec((None, rows, XS_W), lambda *idx: (*index(*idx), 0)),
             pl.BlockSpec((None, rows, LANES), lambda *idx: (*index(*idx), 0)),
             pl.BlockSpec((None, tiles, 8, ROUTE_TM), lambda *idx: (*index(*idx), 0, 0)),
             pl.BlockSpec(memory_space=pltpu.SMEM)]
    return shapes, specs


def _flat_plan(plan):
    haug, dloc, drow, cnt = plan
    return (haug.reshape(-1, XS_W), dloc.reshape(-1, LANES), drow.reshape(-1, 8, ROUTE_TM), cnt)


def _dispatch_kernel(cp_ref, cs_ref, hp_ref, hs_ref, dp_ref, ds_ref, tab_ref, sorted_hbm,
                     sorted_ref, zeros_ref, fill_ref, sem_ref, zsem_ref, *, n_steps, n_p_steps):
    i = pl.program_id(0)
    tm = ROUTE_TM
    ng = N_GROUPS_MOE
    n_tiles = n_steps * ROUTE_SUBS
    n_p_tiles = n_p_steps * ROUTE_SUBS
    meta = n_tiles * TAB_W
    n_rows = sorted_hbm.shape[0]

    def count(tile, g):
        if isinstance(tile, int):
            return cp_ref[tile * ng + g] if tile < n_p_tiles else cs_ref[(tile - n_p_tiles) * ng + g]
        return jnp.where(tile < n_p_tiles, cp_ref[jnp.minimum(tile, n_p_tiles - 1) * ng + g],
                         cs_ref[jnp.maximum(tile - n_p_tiles, 0) * ng + g])

    @pl.when(i == 0)
    def _():
        start = 0
        for g in range(ng):
            fill = sum(count(t, g) for t in range(n_tiles))
            tab_ref[meta + g] = fill
            tab_ref[meta + ng + g] = start
            start = start + _round_up(fill, FFN_BM)
            fill_ref[g] = 0
        zeros_ref[...] = jnp.zeros_like(zeros_ref)

    is_prompt = i < n_p_steps
    row_f = lax.broadcasted_iota(jnp.int32, (ROUTE_ROWS, tm), 0).astype(F32)

    def sort_tile(h_ref, d_ref, sub):
        onehot = jnp.where(row_f == d_ref[sub][0:1, :], 1.0, 0.0).astype(BF16)
        sorted_ref[sub] = _bdot(onehot, h_ref[sub * tm:(sub + 1) * tm, :])

    for sub in range(ROUTE_SUBS):
        tile = i * ROUTE_SUBS + sub

        @pl.when(i >= 1)
        def _():
            _wait_rows(_tile_rows(tab_ref, tile - ROUTE_SUBS), sorted_hbm, sorted_ref.at[sub], sem_ref.at[sub], True)

        pl.when(is_prompt)(functools.partial(sort_tile, hp_ref, dp_ref, sub))
        pl.when(jnp.logical_not(is_prompt))(functools.partial(sort_tile, hs_ref, ds_ref, sub))
        for g in range(ng):
            n_g = count(tile, g)
            tab_ref[tile * TAB_W + g] = tab_ref[meta + ng + g] + fill_ref[g]
            tab_ref[tile * TAB_W + ng + g] = n_g
            fill_ref[g] = fill_ref[g] + n_g
        _run_copies(tab_ref, tile, sorted_hbm, sorted_ref.at[sub], sem_ref.at[sub], to_hbm=True, wait=False)

        @pl.when(i == n_steps - 1)
        def _():
            _wait_rows(_tile_rows(tab_ref, tile), sorted_hbm, sorted_ref.at[sub], sem_ref.at[sub], True)

    @pl.when(i == n_steps - 1)
    def _():
        _zero_fill(tab_ref, meta, zeros_ref, sorted_hbm, zsem_ref, n_rows, wait=False)
        _zero_fill(tab_ref, meta, zeros_ref, sorted_hbm, zsem_ref, n_rows, wait=True)


def _merged_specs(n_p_steps, n_s_steps, layer, sample_seq, step_of, width=D_MODEL):
    def p_map(*idx):
        return (jnp.minimum(step_of(*idx), n_p_steps - 1), 0)

    def s_map(*idx):
        return (jnp.clip(step_of(*idx) - n_p_steps, 0, n_s_steps - 1), 0)

    def mod_map(*idx):
        j = step_of(*idx)
        row = jnp.where(j < n_p_steps, 0, 1 + lax.div(jnp.maximum(j - n_p_steps, 0) * ROUTE_STEP, sample_seq))
        return (layer, row, 0, 0)

    return (pl.BlockSpec((ROUTE_STEP, width), p_map), pl.BlockSpec((ROUTE_STEP, width), s_map),
            pl.BlockSpec((None, None, 6, D_MODEL), mod_map))


def _dispatch(plan_p, plan_s, n_rows):
    haug_p, _, drow_p, cnt_p = plan_p
    haug_s, _, drow_s, cnt_s = plan_s
    n_p_steps = haug_p.shape[0] // ROUTE_STEP
    n_s_steps = haug_s.shape[0] // ROUTE_STEP
    n_steps = n_p_steps + n_s_steps
    n_tiles = n_steps * ROUTE_SUBS
    step_of = lambda i, cp, cs: i
    hp_spec, hs_spec, _ = _merged_specs(n_p_steps, n_s_steps, 0, 1, step_of, XS_W)
    drow_block = (ROUTE_SUBS, 8, ROUTE_TM)
    grid_spec = pltpu.PrefetchScalarGridSpec(
        num_scalar_prefetch=2,
        grid=(n_steps,),
        in_specs=[
            hp_spec, hs_spec,
            pl.BlockSpec(drow_block, lambda i, cp, cs: (jnp.minimum(i, n_p_steps - 1), 0, 0)),
            pl.BlockSpec(drow_block, lambda i, cp, cs: (jnp.clip(i - n_p_steps, 0, n_s_steps - 1), 0, 0)),
        ],
        out_specs=[pl.BlockSpec(memory_space=pltpu.SMEM), pl.BlockSpec(memory_space=pl.ANY)],
        scratch_shapes=[
            pltpu.VMEM((ROUTE_SUBS, ROUTE_ROWS, XS_W), F32),
            pltpu.VMEM((FFN_BM, XS_W), F32),
            pltpu.SMEM((N_GROUPS_MOE,), jnp.int32),
            pltpu.SemaphoreType.DMA((ROUTE_SUBS,)),
            pltpu.SemaphoreType.DMA(()),
        ],
    )
    return pl.pallas_call(
        functools.partial(_dispatch_kernel, n_steps=n_steps, n_p_steps=n_p_steps),
        grid_spec=grid_spec,
        out_shape=[
            jax.ShapeDtypeStruct(((n_tiles + 1) * TAB_W,), jnp.int32),
            jax.ShapeDtypeStruct((n_rows, XS_W), F32),
        ],
        compiler_params=_cparams("arbitrary"),
        name="moe_dispatch",
    )(cnt_p, cnt_s, haug_p, haug_s, drow_p, drow_s)


def _ffn_lookup(i, tab_ref, meta):
    fills = [tab_ref[meta + g] for g in range(N_GROUPS_MOE)]
    edges = []
    acc = 0
    for f in fills:
        acc = acc + lax.div(f + (FFN_BM - 1), FFN_BM)
        edges.append(acc)
    total = edges[-1]
    ii = jnp.minimum(i, total - 1)
    grp = sum((ii >= e).astype(jnp.int32) for e in edges[:-1])

    def pick(vals):
        return jnp.where(grp == 0, vals[0], jnp.where(grp == 1, vals[1], jnp.where(grp == 2, vals[2], vals[3])))

    first = pick([0] + edges[:-1])
    return grp, total, ii == first, pick(fills) - (ii - first) * FFN_BM


def _ffn_group_kernel(tab_ref, xs_ref, w1_ref, w3_ref, w2_ref, ys_ref, w1b_ref, w3b_ref, w2b_ref, *, meta):
    i = pl.program_id(0)
    _, total, first_of_group, valid = _ffn_lookup(i, tab_ref, meta)
    active = i < total

    @pl.when(jnp.logical_and(active, first_of_group))
    def _():
        w1b_ref[...] = w1_ref[...].astype(BF16)
        w3b_ref[...] = w3_ref[...].astype(BF16)
        w2b_ref[...] = w2_ref[...].astype(BF16)

    def run(rows):
        hb = xs_ref[0:rows, 0:D_MODEL].astype(BF16)
        ex = xs_ref[0:rows, D_MODEL:XS_W]
        acc = None
        for e in range(EXPERTS_PER_GROUP):
            a = _bdot(hb, w1b_ref[e])
            b = _bdot(hb, w3b_ref[e])
            gate = ex[:, e:e + 1] + ex[:, GATE_LO + e:GATE_LO + e + 1]
            hid = (a * jax.nn.sigmoid(a)) * b * gate
            part = _bdot(hid.astype(BF16), w2b_ref[e])
            acc = part if acc is None else acc + part
        ys_ref[0:rows, :] = acc

    @pl.when(jnp.logical_and(active, valid > FFN_HALF))
    def _():
        run(FFN_BM)

    @pl.when(jnp.logical_and(active, valid <= FFN_HALF))
    def _():
        run(FFN_HALF)
        ys_ref[FFN_HALF:, :] = jnp.zeros((FFN_BM - FFN_HALF, D_MODEL), F32)

    @pl.when(jnp.logical_not(active))
    def _():
        ys_ref[...] = jnp.zeros_like(ys_ref)


def _ffn_group(tab, xs, w1, w3, w2, layer, meta):
    e4 = EXPERTS_PER_GROUP
    n_blocks = xs.shape[0] // FFN_BM
    group_of = lambda i, tab_ref: _ffn_lookup(i, tab_ref, meta)[0]
    grid_spec = pltpu.PrefetchScalarGridSpec(
        num_scalar_prefetch=1,
        grid=(n_blocks,),
        in_specs=[
            pl.BlockSpec((FFN_BM, XS_W), lambda i, tab_ref: (i, 0)),
            pl.BlockSpec((None, e4, D_MODEL, D_EXPERT), lambda i, tab_ref: (layer, group_of(i, tab_ref), 0, 0)),
            pl.BlockSpec((None, e4, D_MODEL, D_EXPERT), lambda i, tab_ref: (layer, group_of(i, tab_ref), 0, 0)),
            pl.BlockSpec((None, e4, D_EXPERT, D_MODEL), lambda i, tab_ref: (layer, group_of(i, tab_ref), 0, 0)),
        ],
        out_specs=pl.BlockSpec((FFN_BM, D_MODEL), lambda i, tab_ref: (i, 0)),
        scratch_shapes=[
            pltpu.VMEM((e4, D_MODEL, D_EXPERT), BF16),
            pltpu.VMEM((e4, D_MODEL, D_EXPERT), BF16),
            pltpu.VMEM((e4, D_EXPERT, D_MODEL), BF16),
        ],
    )
    return pl.pallas_call(
        functools.partial(_ffn_group_kernel, meta=meta),
        grid_spec=grid_spec,
        out_shape=jax.ShapeDtypeStruct((xs.shape[0], D_MODEL), F32),
        compiler_params=_cparams("arbitrary"),
        name="moe_ffn",
    )(tab, xs, w1, w3, w2)


def _fetch_runs(tab_ref, ys_ref, runs_ref, sem_ref, tile, slot):
    _run_copies(tab_ref, tile, ys_ref, runs_ref.at[slot], sem_ref.at[slot], to_hbm=False, wait=False)


def _unsort(tab_ref, ys_ref, runs_ref, sem_ref, dloc_col, tile, slot):
    covered = _tile_rows(tab_ref, tile)
    _wait_rows(covered, ys_ref, runs_ref.at[slot], sem_ref.at[slot], False)
    runs_ref[slot, pl.ds(pl.multiple_of(covered, ROUTE_PAD), LANES), :] = jnp.zeros((LANES, D_MODEL), F32)
    yb = runs_ref[slot, 0:SORT_ROWS, :].astype(BF16)
    row_f = lax.broadcasted_iota(jnp.int32, (ROUTE_TM, SORT_ROWS), 1).astype(F32)
    return _bdot(jnp.where(row_f == dloc_col, 1.0, 0.0).astype(BF16), yb)


def _combine_kernel(tab_ref, *refs, n_steps, first_step, n_p_steps, final_norm):
    with_prompt = first_step < n_p_steps
    if with_prompt:
        xp_ref, xs_ref, mod_ref, dp_ref, ds_ref, fg_ref, ys_ref, op_ref, os_ref, runs_ref, sem_ref = refs
    else:
        xs_ref, mod_ref, ds_ref, fg_ref, ys_ref, os_ref, runs_ref, sem_ref = refs
    i = pl.program_id(0)
    par = lax.rem(i, 2)
    step = first_step + i
    tm = ROUTE_TM

    def fetch(st, parity):
        for sub in range(ROUTE_SUBS):
            _fetch_runs(tab_ref, ys_ref, runs_ref, sem_ref, st * ROUTE_SUBS + sub, parity * ROUTE_SUBS + sub)

    @pl.when(i == 0)
    def _():
        fetch(first_step, 0)

    @pl.when(i + 1 < n_steps)
    def _():
        fetch(step + 1, 1 - par)

    is_prompt = step < n_p_steps
    dloc = jnp.where(is_prompt, dp_ref[:, 0:1], ds_ref[:, 0:1]) if with_prompt else ds_ref[:, 0:1]
    parts = [_unsort(tab_ref, ys_ref, runs_ref, sem_ref, dloc[sub * tm:(sub + 1) * tm],
                     step * ROUTE_SUBS + sub, par * ROUTE_SUBS + sub) for sub in range(ROUTE_SUBS)]
    delta = mod_ref[5:6, :] * jnp.concatenate(parts, axis=0)

    def finish(x_ref, o_ref):
        x2 = x_ref[...] + delta
        o_ref[...] = _rms(x2, fg_ref[...]) if final_norm else x2

    if with_prompt:
        pl.when(is_prompt)(functools.partial(finish, xp_ref, op_ref))
        pl.when(jnp.logical_not(is_prompt))(functools.partial(finish, xs_ref, os_ref))
    else:
        finish(xs_ref, os_ref)


def _combine(tab, xp, xs, mod, dloc_p, dloc_s, final_g, ys, layer, sample_seq, final_norm, with_prompt):
    n_p_steps = xp.shape[0] // ROUTE_STEP
    n_s_steps = xs.shape[0] // ROUTE_STEP
    first_step = 0 if with_prompt else n_p_steps
    n_steps = n_p_steps + n_s_steps - first_step
    step_of = lambda i, tab_ref: i + first_step
    p_spec, s_spec, mod_spec = _merged_specs(n_p_steps, n_s_steps, layer, sample_seq, step_of)
    dp_spec, ds_spec, _ = _merged_specs(n_p_steps, n_s_steps, layer, sample_seq, step_of, LANES)
    streams = [p_spec, s_spec] if with_prompt else [s_spec]
    dlocs = [dp_spec, ds_spec] if with_prompt else [ds_spec]
    grid_spec = pltpu.PrefetchScalarGridSpec(
        num_scalar_prefetch=1,
        grid=(n_steps,),
        in_specs=streams + [mod_spec] + dlocs + [
            pl.BlockSpec((1, D_MODEL), lambda i, tab_ref: (0, 0)),
            pl.BlockSpec(memory_space=pl.ANY),
        ],
        out_specs=streams,
        scratch_shapes=[
            pltpu.VMEM((2 * ROUTE_SUBS, RUN_ROWS, D_MODEL), F32),
            pltpu.SemaphoreType.DMA((2 * ROUTE_SUBS,)),
        ],
    )
    arrays = [xp, xs] if with_prompt else [xs]
    dloc_arrays = [dloc_p, dloc_s] if with_prompt else [dloc_s]
    return pl.pallas_call(
        functools.partial(_combine_kernel, n_steps=n_steps, first_step=first_step, n_p_steps=n_p_steps,
                          final_norm=final_norm),
        grid_spec=grid_spec,
        out_shape=[jax.ShapeDtypeStruct(a.shape, F32) for a in arrays],
        compiler_params=_cparams("arbitrary"),
        name="moe_combine",
    )(tab, *arrays, mod, *dloc_arrays, final_g, ys)


def _moe(xp, xs, plan_p, plan_s, mod, w1, w3, w2, final_g, layer, sample_seq, final_norm, defer_prompt):
    plan_p, plan_s = _flat_plan(plan_p), _flat_plan(plan_s)
    t = xp.shape[0] + xs.shape[0]
    n_tiles = t // ROUTE_TM
    max_rows = t + N_GROUPS_MOE * (ROUTE_PAD - 1) * n_tiles
    n_rows = (-(-max_rows // FFN_BM) + N_GROUPS_MOE) * FFN_BM
    tab, sorted_x = _dispatch(plan_p, plan_s, n_rows)
    ys = _ffn_group(tab, sorted_x, w1, w3, w2, layer, n_tiles * TAB_W)
    out = _combine(tab, xp, xs, mod, plan_p[1], plan_s[1], final_g, ys, layer, sample_seq, final_norm,
                   not defer_prompt)
    if defer_prompt:
        return None, out[0], (tab, plan_p[1], ys)
    return out[0], out[1], None


def _rope_tables(seq):
    half = QK_ROPE // 2
    nf = half // 2
    inv = ROPE_BASE ** (-np.arange(nf, dtype=np.float64) / nf)
    pos = np.arange(seq)
    row = (pos // GRID_W).astype(np.float64)
    col = (pos % GRID_W).astype(np.float64)
    cos = np.ones((seq, HEAD_PAD), np.float64)
    sin_a = np.zeros((seq, HEAD_PAD), np.float64)
    sin_b = np.zeros((seq, HEAD_PAD), np.float64)
    for part, p in enumerate((row, col)):
        ang = p[:, None] * inv[None, :]
        base = ROPE_OFF + part * half
        cos[:, base:base + nf] = np.cos(ang)
        cos[:, base + nf:base + half] = np.cos(ang)
        sin_a[:, base:base + nf] = -np.sin(ang)
        sin_b[:, base + nf:base + half] = np.sin(ang)
    return tuple(jnp.asarray(a, F32) for a in (cos, sin_a, sin_b))


def _apply_rope(x, cos, sin_a, sin_b, reps):
    nf = QK_ROPE // 4
    width = x.shape[1]
    if reps > 1:
        cos, sin_a, sin_b = (jnp.concatenate([a] * reps, axis=1) for a in (cos, sin_a, sin_b))
    return x * cos + pltpu.roll(x, width - nf, axis=1) * sin_a + pltpu.roll(x, nf, axis=1) * sin_b


def _odd_in_kernel(*refs, rope, emit_cache):
    x_ref, mod_ref, g_ref, w_ref, qg_ref, wq_ref, kg_ref, wk_ref, wv_ref, cs_ref = refs[:10]
    refs = refs[10:]
    if rope:
        cos_ref, sa_ref, sb_ref = refs[:3]
        refs = refs[3:]
    y_ref, q_ref, k_ref, v_ref = refs[:4]
    refs = refs[4:]
    m = mod_ref[...]
    h = _norm_mod(x_ref[...], g_ref[...], m[1:2], m[0:1])
    z = _bdot(h.astype(BF16), w_ref[...])
    zc = z[:, 0:D_C]
    qc = z[:, D_C:D_C + Q_LORA]
    kvc = z[:, D_C + Q_LORA:D_C + Q_LORA + KV_LORA]
    kpe = z[:, D_C + Q_LORA + KV_LORA:]
    q = _bdot(_rms(qc, qg_ref[...]).astype(BF16), wq_ref[...])
    kvn = _rms(kvc, kg_ref[...])
    if emit_cache:
        ckv_ref, kpe_ref = refs
        ckv_ref[...] = kvn
        kpe_ref[...] = kpe[:, ROPE_OFF:ROPE_OFF + QK_ROPE]
    if rope:
        tabs = (cos_ref[...], sa_ref[...], sb_ref[...])
        q = _apply_rope(q, *tabs, reps=N_HEADS)
        kpe = _apply_rope(kpe, *tabs, reps=1)
    kvb = kvn.astype(BF16)
    k = _bdot(kvb, wk_ref[...]) + jnp.concatenate([kpe] * N_HEADS, axis=1)
    scale = math.log2(math.e) / math.sqrt(QK_NOPE + QK_ROPE)
    q_ref[...] = (q * scale).astype(BF16)
    k_ref[...] = k.astype(BF16)
    v_ref[...] = _bdot(kvb, wv_ref[...]).astype(BF16)
    y = _bdot(zc.astype(BF16), cs_ref[...])
    y_ref[0, :, :] = y[:, 0:D_C].astype(BF16)
    y_ref[1, :, :] = y[:, D_C:2 * D_C].astype(BF16)


def _odd_in(x3, mod, g, w_in, q_g, w_q, kv_g, w_k, w_v, cs, stream, layer, rope_tabs, emit_cache):
    b, s, _ = x3.shape
    tm = min(s, 512)
    n_i = s // tm
    rope = rope_tabs is not None
    const = lambda a: pl.BlockSpec(a.shape, lambda bi, i: (0,) * a.ndim)
    in_specs = [
        pl.BlockSpec((None, tm, D_MODEL), lambda bi, i: (bi, i, 0)),
        _mod_spec(stream, layer),
        const(g), const(w_in), const(q_g), const(w_q), const(kv_g), const(w_k), const(w_v), const(cs),
    ]
    args = [x3, mod, g, w_in, q_g, w_q, kv_g, w_k, w_v, cs]
    if rope:
        in_specs += [pl.BlockSpec((tm, HEAD_PAD), lambda bi, i: (i, 0))] * 3
        args += list(rope_tabs)
    hq = N_HEADS * HEAD_PAD
    out_specs = [
        pl.BlockSpec((None, 2, tm, D_C), lambda bi, i: (bi, 0, i, 0)),
        pl.BlockSpec((None, tm, hq), lambda bi, i: (bi, i, 0)),
        pl.BlockSpec((None, tm, hq), lambda bi, i: (bi, i, 0)),
        pl.BlockSpec((None, tm, N_HEADS * V_DIM), lambda bi, i: (bi, i, 0)),
    ]
    out_shape = [
        jax.ShapeDtypeStruct((b, 2, s, D_C), BF16),
        jax.ShapeDtypeStruct((b, s, hq), BF16),
        jax.ShapeDtypeStruct((b, s, hq), BF16),
        jax.ShapeDtypeStruct((b, s, N_HEADS * V_DIM), BF16),
    ]
    if emit_cache:
        out_specs += [
            pl.BlockSpec((None, tm, KV_LORA), lambda bi, i: (bi, i, 0)),
            pl.BlockSpec((None, tm, QK_ROPE), lambda bi, i: (bi, i, 0)),
        ]
        out_shape += [
            jax.ShapeDtypeStruct((b, s, KV_LORA), F32),
            jax.ShapeDtypeStruct((b, s, QK_ROPE), F32),
        ]
    return pl.pallas_call(
        functools.partial(_odd_in_kernel, rope=rope, emit_cache=emit_cache),
        grid=(b, n_i),
        in_specs=in_specs,
        out_specs=out_specs,
        out_shape=out_shape,
        compiler_params=_cparams("parallel", "parallel"),
        name="odd_in",
    )(*args)


def _cache_kv_kernel(c_ref, p_ref, wk_ref, wv_ref, k_ref, v_ref):
    cb = c_ref[...].astype(BF16)
    k = _bdot(cb, wk_ref[...]) + jnp.concatenate([p_ref[...]] * N_HEADS, axis=1)
    k_ref[...] = k.astype(BF16)
    v_ref[...] = _bdot(cb, wv_ref[...]).astype(BF16)


def _cache_kv(ckv, kpe_blk, w_k, w_v):
    b, p, _ = ckv.shape
    hq = N_HEADS * HEAD_PAD
    return pl.pallas_call(
        _cache_kv_kernel,
        grid=(b,),
        in_specs=[
            pl.BlockSpec((None, p, KV_LORA), lambda bi: (bi, 0, 0)),
            pl.BlockSpec((None, p, HEAD_PAD), lambda bi: (bi, 0, 0)),
            pl.BlockSpec(w_k.shape, lambda bi: (0, 0)),
            pl.BlockSpec(w_v.shape, lambda bi: (0, 0)),
        ],
        out_specs=[
            pl.BlockSpec((None, p, hq), lambda bi: (bi, 0, 0)),
            pl.BlockSpec((None, p, N_HEADS * V_DIM), lambda bi: (bi, 0, 0)),
        ],
        out_shape=[
            jax.ShapeDtypeStruct((b, p, hq), BF16),
            jax.ShapeDtypeStruct((b, p, N_HEADS * V_DIM), BF16),
        ],
        compiler_params=_cparams("parallel"),
        name="cache_kv",
    )(ckv, kpe_blk, w_k, w_v)


def _odd_mix_kernel(*refs, with_cache):
    q_ref, k_ref, v_ref = refs[:3]
    refs = refs[3:]
    if with_cache:
        kc_ref, vc_ref = refs[:2]
        refs = refs[2:]
    plan_tile = None
    if with_cache:
        y_ref, f_ref, x_ref, mod_ref, wo_ref = refs[:5]
        plan_in, (o_ref, *plan_out), (a_ref, vx_ref, vcx_ref) = refs[5:9], refs[9:14], refs[14:]
        plan_tile = pl.program_id(0) * pl.num_programs(1) + pl.program_id(1)

        @pl.when(pl.program_id(1) == 0)
        def _():
            for src, dst in ((v_ref, vx_ref), (vc_ref, vcx_ref)):
                one = lax.broadcasted_iota(jnp.int32, (src.shape[0], PAIR_W - DEN_COL), 1) == 0
                for pair in range(N_HEADS // 2):
                    dst[:, pair * PAIR_W:pair * PAIR_W + DEN_COL] = src[:, pair * DEN_COL:(pair + 1) * DEN_COL]
                    dst[:, pair * PAIR_W + DEN_COL:(pair + 1) * PAIR_W] = jnp.where(one, 1.0, 0.0).astype(BF16)
    else:
        y_ref, f_ref, x_ref, mod_ref, wo_ref, o_ref, a_ref = refs[:7]
        if len(refs) > 7:
            plan_in, plan_out, plan_tile = refs[7:]
    tq = q_ref.shape[0]
    lane = lax.broadcasted_iota(jnp.int32, (tq, 2 * V_DIM), 1)
    for pair in range(N_HEADS // 2):
        vcols = slice(pair * PAIR_W, (pair + 1) * PAIR_W)
        outs = []
        for h in (2 * pair, 2 * pair + 1):
            hcols = slice(h * HEAD_PAD, (h + 1) * HEAD_PAD)
            qh = q_ref[:, hcols]
            s = lax.dot_general(qh, k_ref[:, hcols], NT_DIMS, preferred_element_type=F32)
            top = jnp.max(s, axis=-1, keepdims=True)
            if with_cache:
                sc = lax.dot_general(qh, kc_ref[:, hcols], NT_DIMS, preferred_element_type=F32)
                top = jnp.maximum(top, jnp.max(sc, axis=-1, keepdims=True))
            if with_cache:
                acc = _bdot(jnp.exp2((s - top).astype(BF16)), vx_ref[:, vcols])
                acc = acc + _bdot(jnp.exp2((sc - top).astype(BF16)), vcx_ref[:, vcols])
                outs.append(acc[:, 0:2 * V_DIM] / acc[:, DEN_COL:DEN_COL + 1])
            else:
                p = jnp.exp2(s - top)
                den = jnp.sum(p, axis=-1, keepdims=True)
                outs.append(_bdot(p.astype(BF16), v_ref[:, pair * DEN_COL:(pair + 1) * DEN_COL]) / den)
        a_ref[:, pair * 2 * V_DIM:(pair + 1) * 2 * V_DIM] = jnp.where(lane < V_DIM, outs[0], outs[1]).astype(BF16)
    f = _bdot(f_ref[...], y_ref[...].reshape(-1, D_C))
    o = _bdot(f.astype(BF16), wo_ref[0:D_C, :]) + _bdot(a_ref[...], wo_ref[D_C:, :])
    x1 = x_ref[...] + mod_ref[2:3, :] * o
    o_ref[...] = x1
    if plan_tile is not None:
        _route_plan(x1, mod_ref[...], plan_in, plan_out, plan_tile)


def _odd_whole_kernel(tab_ref, x_ref, mod_ref, g_ref, w_ref, qg_ref, wq_ref, kg_ref, wk_ref, wv_ref, cs_ref, f_ref,
                      wo_ref, modp_ref, dloc_ref, ys_ref, pg_ref, pw_ref, pb_ref, pu_ref,
                      o_ref, ckv_ref, kpe_ref, ph_ref, pd_ref, pr_ref, pc_ref,
                      y_ref, q_ref, k_ref, v_ref, a_ref, x2_ref, runs_ref, sem_ref):
    bi = pl.program_id(0)
    par = lax.rem(bi, 2)

    @pl.when(bi == 0)
    def _():
        _fetch_runs(tab_ref, ys_ref, runs_ref, sem_ref, 0, 0)

    @pl.when(bi + 1 < pl.num_programs(0))
    def _():
        _fetch_runs(tab_ref, ys_ref, runs_ref, sem_ref, bi + 1, 1 - par)

    moe = _unsort(tab_ref, ys_ref, runs_ref, sem_ref, dloc_ref[:, 0:1], bi, par)
    x2_ref[...] = x_ref[...] + modp_ref[5:6, :] * moe
    _odd_in_kernel(x2_ref, mod_ref, g_ref, w_ref, qg_ref, wq_ref, kg_ref, wk_ref, wv_ref, cs_ref,
                   y_ref, q_ref, k_ref, v_ref, ckv_ref, kpe_ref, rope=False, emit_cache=True)
    _odd_mix_kernel(q_ref, k_ref, v_ref, y_ref, f_ref, x2_ref, mod_ref, wo_ref, o_ref, a_ref,
                    (pg_ref, pw_ref, pb_ref, pu_ref), (ph_ref, pd_ref, pr_ref, pc_ref), bi, with_cache=False)


def _odd_whole(x3, pending, mod, g, w_in, q_g, w_q, kv_g, w_k, w_v, cs, fmat, w_out, plan_consts, stream, layer):
    tab, dloc, ys = pending
    b, s, _ = x3.shape
    assert s == ROUTE_TM, "one sequence must be one MoE sort tile"
    hq = N_HEADS * HEAD_PAD
    hv = N_HEADS * V_DIM
    const = lambda a: pl.BlockSpec(a.shape, lambda bi, tab_ref: (0,) * a.ndim)
    row_block = lambda w: pl.BlockSpec((None, s, w), lambda bi, tab_ref: (bi, 0, 0))
    mod_block = lambda lyr: pl.BlockSpec((None, None, 6, D_MODEL),
                                         lambda bi, tab_ref: (lyr, stream.row_of_batch(bi), 0, 0))
    plan_shapes, plan_specs = _plan_out(b, s, s, lambda bi, tab_ref: (bi, 0))
    grid_spec = pltpu.PrefetchScalarGridSpec(
        num_scalar_prefetch=1,
        grid=(b,),
        in_specs=[
            row_block(D_MODEL), mod_block(layer),
            const(g), const(w_in), const(q_g), const(w_q), const(kv_g), const(w_k), const(w_v), const(cs),
            const(fmat), const(w_out),
            mod_block(layer - 1),
            pl.BlockSpec((s, LANES), lambda bi, tab_ref: (bi, 0)),
            pl.BlockSpec(memory_space=pl.ANY),
        ] + [const(a) for a in plan_consts],
        out_specs=[row_block(D_MODEL), row_block(KV_LORA), row_block(QK_ROPE)] + plan_specs,
        scratch_shapes=[
            pltpu.VMEM((2, s, D_C), BF16),
            pltpu.VMEM((s, hq), BF16),
            pltpu.VMEM((s, hq), BF16),
            pltpu.VMEM((s, hv), BF16),
            pltpu.VMEM((s, hv), BF16),
            pltpu.VMEM((s, D_MODEL), F32),
            pltpu.VMEM((2, RUN_ROWS, D_MODEL), F32),
            pltpu.SemaphoreType.DMA((2,)),
        ],
    )
    out = pl.pallas_call(
        _odd_whole_kernel,
        grid_spec=grid_spec,
        out_shape=[
            jax.ShapeDtypeStruct((b, s, D_MODEL), F32),
            jax.ShapeDtypeStruct((b, s, KV_LORA), F32),
            jax.ShapeDtypeStruct((b, s, QK_ROPE), F32),
        ] + plan_shapes,
        compiler_params=_cparams("arbitrary"),
        name="odd_whole",
    )(tab, x3, mod, g, w_in, q_g, w_q, kv_g, w_k, w_v, cs, fmat, w_out, mod, dloc, ys, *plan_consts)
    return out[0], out[1], out[2], out[3:]


def _odd_mix(q, k, v, kc, vc, y, fmat, x3, mod, w_out, plan_consts, stream, layer):
    b, s, hq = q.shape
    tq = min(s, 256)
    n_i = s // tq
    with_cache = kc is not None
    hv = N_HEADS * V_DIM
    mode = dict(pipeline_mode=pl.Buffered(1)) if with_cache else {}

    def per_batch(rows, cols):
        return pl.BlockSpec((None, rows, cols), lambda bi, i: (bi, 0, 0), **mode)

    in_specs = [pl.BlockSpec((None, tq, hq), lambda bi, i: (bi, i, 0)), per_batch(s, hq), per_batch(s, hv)]
    args = [q, k, v]
    if with_cache:
        p = kc.shape[1]
        in_specs += [per_batch(p, hq), per_batch(p, hv)]
        args += [kc, vc]
    in_specs += [
        per_batch(2 * s, D_C),
        pl.BlockSpec((tq, 2 * s), lambda bi, i: (i, 0)),
        pl.BlockSpec((None, tq, D_MODEL), lambda bi, i: (bi, i, 0)),
        _mod_spec(stream, layer),
        pl.BlockSpec(w_out.shape, lambda bi, i: (0, 0), **mode),
    ]
    in_specs += [pl.BlockSpec(a.shape, lambda bi, i: (0,) * a.ndim) for a in plan_consts]
    args += [y, fmat, x3, mod, w_out] + plan_consts
    plan_shapes, plan_specs = _plan_out(b, s, tq, lambda bi, i: (bi, i))
    out = pl.pallas_call(
        functools.partial(_odd_mix_kernel, with_cache=with_cache),
        grid=(b, n_i),
        in_specs=in_specs,
        out_specs=[pl.BlockSpec((None, tq, D_MODEL), lambda bi, i: (bi, i, 0))] + plan_specs,
        out_shape=[jax.ShapeDtypeStruct((b, s, D_MODEL), F32)] + plan_shapes,
        scratch_shapes=[pltpu.VMEM((tq, hv), BF16)] + (
            [pltpu.VMEM((s, VX_W), BF16), pltpu.VMEM((kc.shape[1], VX_W), BF16)] if with_cache else []),
        compiler_params=_cparams("arbitrary", "arbitrary"),
        name="odd_mix",
    )(*args)
    return out[0], out[1:]


def _dft_tables(seq):
    jc = np.arange(C_GW)
    ang_c = 2.0 * np.pi * np.outer(jc, jc) / C_GW
    eye = np.eye(C_GROUPS)
    cs = np.concatenate([np.kron(eye, np.cos(ang_c)), np.kron(eye, np.sin(ang_c))], axis=1)
    jn = np.arange(seq)
    ang_n = 2.0 * np.pi * (np.outer(jn, jn) % seq) / seq
    scale = 1.0 / math.sqrt(seq * C_GW)
    fmat = np.concatenate([np.cos(ang_n), -np.sin(ang_n)], axis=1) * scale
    return jnp.asarray(cs, F32).astype(BF16), jnp.asarray(fmat, F32).astype(BF16)


def _odd_weights(w_in, w_uq, w_ukv):
    d = w_in.shape[0]
    base = D_C + Q_LORA + KV_LORA
    kpe_blk = jnp.zeros((d, HEAD_PAD), w_in.dtype).at[:, ROPE_OFF:ROPE_OFF + QK_ROPE].set(w_in[:, base:])
    w_in_p = jnp.concatenate([w_in[:, :base], kpe_blk], axis=1).astype(BF16)
    qh = w_uq.reshape(Q_LORA, N_HEADS, QK_NOPE + QK_ROPE)
    w_q = jnp.pad(qh, ((0, 0), (0, 0), (0, HEAD_PAD - QK_NOPE - QK_ROPE))).reshape(Q_LORA, -1).astype(BF16)
    kvh = w_ukv.reshape(KV_LORA, N_HEADS, QK_NOPE + V_DIM)
    w_k = jnp.pad(kvh[:, :, :QK_NOPE], ((0, 0), (0, 0), (0, HEAD_PAD - QK_NOPE))).reshape(KV_LORA, -1)
    w_v = kvh[:, :, QK_NOPE:].reshape(KV_LORA, -1)
    return w_in_p, w_q, w_k.astype(BF16), w_v.astype(BF16)


ROUTER_ROWS = 32


def _router_weights(wg, bg, we, be):
    d = wg.shape[0]
    w = jnp.concatenate([wg, we.reshape(d, N_EXPERTS)], axis=1).T
    w = jnp.pad(w, ((0, ROUTER_ROWS - w.shape[0]), (0, 0))).astype(BF16)
    b = jnp.concatenate([bg, be.reshape(N_EXPERTS)])
    b = jnp.pad(b, (0, ROUTER_ROWS - b.shape[0])).reshape(ROUTER_ROWS, 1).astype(F32)
    return w, b


def kernel(x_prompt, x_sample, cache_ckv, cache_kpe, c, c_ctx, mod_w, mod_b, norm1_g, norm2_g,
           ev_w_in, ev_conv_w, ev_sgu_norm_g, ev_sgu_w, ev_sgu_b, ev_w_out,
           od_w_in, od_q_norm_g, od_w_uq, od_kv_norm_g, od_w_ukv, od_w_out,
           moe_wg, moe_bg, moe_we, moe_be, moe_w1, moe_w3, moe_w2, final_norm_g):
    bp, n_p, d = x_prompt.shape
    bs, n_s, _ = x_sample.shape
    streams = [(_Stream(bp, n_p, True), x_prompt), (_Stream(bs, n_s, False), x_sample)]

    n_rows = 1 + bs
    cond_t = jnp.concatenate([c_ctx[None, :], c], axis=0).T
    mod = _adaln(cond_t, mod_w, mod_b, n_rows)

    final_g = final_norm_g.reshape(1, d)
    xs = [x for _, x in streams]
    new_ckv, new_kpe = [], []
    pending = None
    plans = [None, None]
    for l in range(DEPTH):
        j = l // 2
        g1 = norm1_g[l].reshape(1, d)
        g2 = norm2_g[l].reshape(1, d)
        plan_consts = _plan_consts(g2, *_router_weights(moe_wg[l], moe_bg[l], moe_we[l], moe_be[l]))
        last = l == DEPTH - 1
        if l % 2 == 0:
            w_in = ev_w_in[j].astype(BF16)
            w_out = ev_w_out[j].astype(BF16)
            sgu_w = ev_sgu_w[j].astype(BF16)
            sgu_g = ev_sgu_norm_g[j].reshape(1, D_B)
            sgu_bias = jnp.repeat(ev_sgu_b[j].T, D_B // B_GROUPS, axis=1)
            for si, (st, _) in enumerate(streams):
                xs[si], plans[si] = _even_layer(xs[si], mod, g1, w_in, ev_conv_w[j], sgu_g, sgu_w, sgu_bias, w_out,
                                                plan_consts, st, l)
        else:
            w_in, w_q, w_k, w_v = _odd_weights(od_w_in[j], od_w_uq[j], od_w_ukv[j])
            w_out = od_w_out[j].astype(BF16)
            q_g = od_q_norm_g[j].reshape(1, Q_LORA)
            kv_g = od_kv_norm_g[j].reshape(1, KV_LORA)
            for si, (st, _) in enumerate(streams):
                x3 = xs[si]
                cs, fmat = _dft_tables(st.seq)
                if st.shared_cond:
                    xs[si], ckv, kpe, plans[si] = _odd_whole(x3, pending, mod, g1, w_in, q_g, w_q, kv_g, w_k, w_v, cs,
                                                             fmat, w_out, plan_consts, st, l)
                    new_ckv.append(ckv)
                    new_kpe.append(kpe)
                    continue
                y, q, k, v = _odd_in(x3, mod, g1, w_in, q_g, w_q, kv_g, w_k, w_v, cs, st, l,
                                     _rope_tables(st.seq), False)
                kpe_blk = jnp.pad(cache_kpe[:, j], ((0, 0), (0, 0), (ROPE_OFF, HEAD_PAD - ROPE_OFF - QK_ROPE)))
                kc, vc = _cache_kv(cache_ckv[:, j], kpe_blk, w_k, w_v)
                xs[si], plans[si] = _odd_mix(q, k, v, kc, vc, y.reshape(st.batch, 2 * st.seq, D_C), fmat, x3, mod,
                                             w_out, plan_consts, st, l)
        defer = not last and (l + 1) % 2 == 1
        x2p, x2s, pending = _moe(xs[0].reshape(bp * n_p, d), xs[1].reshape(bs * n_s, d), plans[0], plans[1], mod,
                                 moe_w1, moe_w3, moe_w2, final_g, l, n_s, last, defer)
        xs = [xs[0] if defer else x2p.reshape(bp, n_p, d), x2s.reshape(bs, n_s, d)]
    return (xs[0], xs[1], jnp.stack(new_ckv, axis=1), jnp.stack(new_kpe, axis=1))
```

```python
import functools
import math

import numpy as np
import jax
import jax.numpy as jnp
from jax import lax
from jax.experimental import pallas as pl
from jax.experimental.pallas import tpu as pltpu

D_MODEL = 1024
DEPTH = 2
GRID_W = 64
D_A = D_MODEL // 2
D_B = D_MODEL // 2
B_GROUPS = 4
CHUNK = 128
D_EVEN_IN = 3 * D_A + 2 * D_B
D_C = D_MODEL // 4
C_GROUPS = 4
C_GW = D_C // C_GROUPS
N_HEADS = 12
QK_NOPE = 64
QK_ROPE = 32
V_DIM = 64
Q_LORA = 384
KV_LORA = 256
ROPE_BASE = 10000.0
N_GROUPS_MOE = 4
EXPERTS_PER_GROUP = 4
N_EXPERTS = N_GROUPS_MOE * EXPERTS_PER_GROUP
D_EXPERT = 256
EPS = 1e-6

LANES = 128
HEAD_PAD = 128
PAIR_W = 256
DEN_COL = 2 * 64
VX_W = 6 * PAIR_W
ROPE_OFF = QK_NOPE
GATE_OFF = N_GROUPS_MOE
NEG_BIG = -1e30
F32 = jnp.float32
BF16 = jnp.bfloat16
VMEM_LIMIT = 56 * 1024 * 1024


def _cparams(*sem):
    return pltpu.CompilerParams(dimension_semantics=sem, vmem_limit_bytes=VMEM_LIMIT)


def _rms(x, g):
    return x * lax.rsqrt(jnp.mean(x * x, axis=-1, keepdims=True) + EPS) * g


def _norm_mod(x, g, scale, shift):
    rs = lax.rsqrt(jnp.mean(x * x, axis=-1, keepdims=True) + EPS)
    return x * rs * (g * (1.0 + scale)) + shift


def _bdot(a, b):
    return jnp.dot(a, b, preferred_element_type=F32)


NT_DIMS = (((1,), (1,)), ((), ()))
TN_DIMS = (((0,), (0,)), ((), ()))


def _mod_kernel(ct_ref, w_ref, b_ref, o_ref):
    c = ct_ref[...]
    s = c * jax.nn.sigmoid(c)
    w = w_ref[...]
    b = b_ref[...]
    for r in range(o_ref.shape[0]):
        o_ref[r:r + 1, :] = jnp.sum(s[:, r:r + 1] * w, axis=0, keepdims=True) + b


def _adaln(cond_t, mod_w, mod_b, n_rows):
    nt = 2048
    d6 = mod_w.shape[-1]
    out = pl.pallas_call(
        _mod_kernel,
        grid=(DEPTH, d6 // nt),
        in_specs=[
            pl.BlockSpec(cond_t.shape, lambda l, n: (0, 0)),
            pl.BlockSpec((None, D_MODEL, nt), lambda l, n: (l, 0, n)),
            pl.BlockSpec((None, 1, nt), lambda l, n: (l, 0, n)),
        ],
        out_specs=pl.BlockSpec((None, n_rows, nt), lambda l, n: (l, 0, n)),
        out_shape=jax.ShapeDtypeStruct((DEPTH, n_rows, d6), F32),
        compiler_params=_cparams("parallel", "parallel"),
        name="adaln",
    )(cond_t, mod_w, mod_b.reshape(DEPTH, 1, d6))
    return out.reshape(DEPTH, n_rows, 6, D_MODEL)


class _Stream:
    def __init__(self, batch, seq, shared_cond):
        self.batch = batch
        self.seq = seq
        self.tokens = batch * seq
        self.shared_cond = shared_cond

    def row_of_batch(self, b):
        return 0 if self.shared_cond else b + 1

    def row_of_tile(self, i, tm):
        return 0 if self.shared_cond else (i * tm) // self.seq + 1


def _mod_spec(stream, layer, tm=None):
    if tm is None:
        return pl.BlockSpec((None, None, 6, D_MODEL), lambda b, i: (layer, stream.row_of_batch(b), 0, 0))
    return pl.BlockSpec((None, None, 6, D_MODEL), lambda i, *_: (layer, stream.row_of_tile(i, tm), 0, 0))


HALO = 8


def _even_kernel(*refs, has_halo, seq_rows, nc):
    if has_halo:
        x_ref, xp_ref, xn_ref = refs[:3]
        refs = refs[3:]
    else:
        x_ref = refs[0]
        refs = refs[1:]
    mod_ref, g_ref, wi_ref, cw_ref, sg_ref, sw_ref, sb_ref, wo_ref = refs[:8]
    plan_in, (o_ref, *plan_out), (z_ref, y_ref) = refs[8:12], refs[12:17], refs[17:]
    i = pl.program_id(1)
    n_i = pl.num_programs(1)
    ts = x_ref.shape[0]
    m = mod_ref[...]
    g = g_ref[...]

    def modulate(x):
        return _norm_mod(x, g, m[1:2], m[0:1]).astype(BF16)

    x = x_ref[...]
    hb = modulate(x)
    for n in range(D_EVEN_IN // nc):
        z_ref[:, n * nc:(n + 1) * nc] = _bdot(hb, wi_ref[:, n * nc:(n + 1) * nc]).astype(BF16)

    gate_b = z_ref[:, 0:D_A].astype(F32)
    gate_c = z_ref[:, D_A:2 * D_A].astype(F32)
    xa = z_ref[:, 2 * D_A:3 * D_A].astype(F32)
    t = gate_c * xa
    t_prev = pltpu.roll(t, 1, axis=0)
    t_next = pltpu.roll(t, ts - 1, axis=0)
    row = lax.broadcasted_iota(jnp.int32, (ts, 1), 0) & (seq_rows - 1)
    if has_halo:
        hh = modulate(jnp.concatenate([xp_ref[...], xn_ref[...]], axis=0))
        zh = _bdot(hh, wi_ref[:, D_A:3 * D_A]).astype(BF16).astype(F32)
        th = zh[:, 0:D_A] * zh[:, D_A:2 * D_A]
        tp = th[HALO - 1:HALO] * (i > 0).astype(F32)
        tn = th[HALO:HALO + 1] * (i < n_i - 1).astype(F32)
    else:
        tp = tn = 0.0
    t_prev = jnp.where(row == 0, tp, t_prev)
    t_next = jnp.where(row == seq_rows - 1, tn, t_next)
    cw = cw_ref[...]
    y_a = gate_b * (t_prev * cw[0:1] + t * cw[1:2] + t_next * cw[2:3])
    y_ref[:, 0:D_A] = y_a.astype(BF16)

    u = z_ref[:, 3 * D_A:3 * D_A + D_B].astype(F32)
    v = z_ref[:, 3 * D_A + D_B:3 * D_A + 2 * D_B].astype(F32)
    vb = _rms(v, sg_ref[...]).astype(BF16)
    gw = D_B // B_GROUPS
    for c in range(ts // CHUNK):
        rows = slice(c * CHUNK, (c + 1) * CHUNK)
        for gi in range(B_GROUPS):
            cols = slice(gi * gw, (gi + 1) * gw)
            sv = _bdot(sw_ref[gi], vb[rows, cols]) + sb_ref[:, cols]
            y_ref[rows, D_A + gi * gw:D_A + (gi + 1) * gw] = (u[rows, cols] * sv).astype(BF16)

    x1 = x + m[2:3] * _bdot(y_ref[...], wo_ref[...])
    o_ref[...] = x1
    _route_plan(x1, m, plan_in, plan_out, (pl.program_id(0) * n_i + i) * (ts // ROUTE_TM))


def _even_layer(x3, mod, g, w_in, conv_w, sgu_g, sgu_w, sgu_bias, w_out, plan_consts, stream, layer):
    out_shape = x3.shape
    b, s, _ = x3.shape
    ts = min(s, 256) if stream.shared_cond else min(s, 512)
    n_i = s // ts
    has_halo = n_i > 1
    seq_rows = ts
    if not has_halo and stream.shared_cond and b % 2 == 0:
        b, s, ts = b // 2, 2 * s, 2 * ts
        x3 = x3.reshape(b, s, D_MODEL)
    hb = ts // HALO
    last_h = s // HALO - 1
    const = lambda a: pl.BlockSpec(a.shape, lambda bi, i: (0,) * a.ndim)
    in_specs = [pl.BlockSpec((None, ts, D_MODEL), lambda bi, i: (bi, i, 0))]
    args = [x3]
    if has_halo:
        in_specs += [
            pl.BlockSpec((None, HALO, D_MODEL), lambda bi, i: (bi, jnp.maximum(i * hb - 1, 0), 0)),
            pl.BlockSpec((None, HALO, D_MODEL), lambda bi, i: (bi, jnp.minimum((i + 1) * hb, last_h), 0)),
        ]
        args += [x3, x3]
    in_specs += [_mod_spec(stream, layer), const(g), const(w_in), const(conv_w), const(sgu_g), const(sgu_w),
                 const(sgu_bias), const(w_out)] + [const(a) for a in plan_consts]
    args += [mod, g, w_in, conv_w, sgu_g, sgu_w, sgu_bias, w_out] + plan_consts
    plan_shapes, plan_specs = _plan_out(b, s, ts, lambda bi, i: (bi, i))
    out = pl.pallas_call(
        functools.partial(_even_kernel, has_halo=has_halo, seq_rows=seq_rows, nc=512),
        grid=(b, n_i),
        in_specs=in_specs,
        out_specs=[pl.BlockSpec((None, ts, D_MODEL), lambda bi, i: (bi, i, 0))] + plan_specs,
        out_shape=[jax.ShapeDtypeStruct((b, s, D_MODEL), F32)] + plan_shapes,
        scratch_shapes=[pltpu.VMEM((ts, D_EVEN_IN), BF16), pltpu.VMEM((ts, D_A + D_B), BF16)],
        compiler_params=_cparams("arbitrary", "arbitrary"),
        name="even_layer",
    )(*args)
    return out[0].reshape(out_shape), out[1:]


ROUTE_TM = 256
ROUTE_SUBS = 4
ROUTE_STEP = ROUTE_TM * ROUTE_SUBS
ROUTE_PAD = 8
SORT_ROWS = ROUTE_TM + LANES
RUN_ROWS = SORT_ROWS + 32
ROUTE_ROWS = ROUTE_TM + 4 * ROUTE_PAD
XS_W = D_MODEL + LANES
GATE_LO = EXPERTS_PER_GROUP
DLOC_HI = 2 * EXPERTS_PER_GROUP
DLOC_RADIX = 16.0
FFN_BM = 512
FFN_HALF = FFN_BM // 2
RUN_SIZES = (256, 128, 64, 32, 16, 8)
TAB_W = 2 * N_GROUPS_MOE


def _round_up(x, m):
    return lax.div(x + (m - 1), m) * m


def _run_copies(tab_ref, tile, hbm_ref, vmem_ref, sem, to_hbm, wait):
    off = 0
    for g in range(N_GROUPS_MOE):
        start = tab_ref[tile * TAB_W + g]
        n = tab_ref[tile * TAB_W + N_GROUPS_MOE + g]
        for p in RUN_SIZES:
            done = n & (-2 * p)

            @pl.when((n & p) != 0)
            def _():
                v = vmem_ref.at[pl.ds(pl.multiple_of(off + done, ROUTE_PAD), p)]
                h = hbm_ref.at[pl.ds(pl.multiple_of(start + done, ROUTE_PAD), p)]
                cp = pltpu.make_async_copy(v, h, sem) if to_hbm else pltpu.make_async_copy(h, v, sem)
                if wait:
                    cp.wait()
                else:
                    cp.start()
        off = off + n


def _zero_fill(tab_ref, meta, zeros_ref, hbm_ref, sem, n_rows, wait):
    def copy(rows, dst_row):
        cp = pltpu.make_async_copy(zeros_ref.at[pl.ds(0, rows)],
                                   hbm_ref.at[pl.ds(pl.multiple_of(dst_row, ROUTE_PAD), rows)], sem)
        if wait:
            cp.wait()
        else:
            cp.start()

    end = 0
    for g in range(N_GROUPS_MOE):
        fill = tab_ref[meta + g]
        start = tab_ref[meta + N_GROUPS_MOE + g]
        end = start + _round_up(fill, FFN_BM)
        tail = end - start - fill
        for p in RUN_SIZES:
            pl.when((tail & p) != 0)(functools.partial(copy, p, start + fill + (tail & (-2 * p))))
    for k in range(n_rows // FFN_BM):
        pl.when(end + k * FFN_BM < n_rows)(functools.partial(copy, FFN_BM, end + k * FFN_BM))


def _wait_rows(n, hbm_ref, vmem_ref, sem, to_hbm):
    for p in RUN_SIZES:
        @pl.when((n & p) != 0)
        def _():
            v = vmem_ref.at[pl.ds(0, p)]
            h = hbm_ref.at[pl.ds(0, p)]
            (pltpu.make_async_copy(v, h, sem) if to_hbm else pltpu.make_async_copy(h, v, sem)).wait()


def _tile_rows(tab_ref, tile):
    n = 0
    for g in range(N_GROUPS_MOE):
        n = n + tab_ref[tile * TAB_W + N_GROUPS_MOE + g]
    return n


def _max4(v):
    return jnp.maximum(jnp.maximum(v[0], v[1]), jnp.maximum(v[2], v[3]))


def _first_of4(v, top):
    return jnp.where(v[0] == top, 0.0, jnp.where(v[1] == top, 1.0, jnp.where(v[2] == top, 2.0, 3.0)))


def _route_plan(x, m, plan_in, plan_out, tile0):
    g_ref, wrt_ref, brt_ref, upper_ref = plan_in
    haug_ref, dloc_ref, drow_ref, cnt_ref = plan_out
    rows = x.shape[0]
    tm = ROUTE_TM
    ng = N_GROUPS_MOE
    h = _norm_mod(x, g_ref[...], m[4:5], m[3:4])
    hb = h.astype(BF16)
    lt = lax.dot_general(wrt_ref[...], hb, NT_DIMS, preferred_element_type=F32) + brt_ref[...]
    gl = [lt[r:r + 1, :] for r in range(ng)]
    g_top = _max4(gl)
    g_idx = _first_of4(gl, g_top)
    g_w = 1.0 / (jnp.exp(gl[0] - g_top) + jnp.exp(gl[1] - g_top) + jnp.exp(gl[2] - g_top) + jnp.exp(gl[3] - g_top))
    ev = []
    for k in range(EXPERTS_PER_GROUP):
        cand = [lt[GATE_OFF + EXPERTS_PER_GROUP * r + k:GATE_OFF + EXPERTS_PER_GROUP * r + k + 1, :]
                for r in range(ng)]
        ev.append(jnp.where(g_idx == 0.0, cand[0], jnp.where(g_idx == 1.0, cand[1],
                            jnp.where(g_idx == 2.0, cand[2], cand[3]))))
    v1 = _max4(ev)
    i1 = _first_of4(ev, v1)
    rest = [jnp.where(i1 == float(k), NEG_BIG, ev[k]) for k in range(EXPERTS_PER_GROUP)]
    v2 = _max4(rest)
    i2 = _first_of4(rest, v2)
    e2 = jnp.exp(v2 - v1)
    w1 = 1.0 / (1.0 + e2)
    w2 = e2 * w1
    gates = [g_w * (jnp.where(i1 == float(k), w1, 0.0) + jnp.where(i2 == float(k), w2, 0.0))
             for k in range(EXPERTS_PER_GROUP)]

    sub8 = lax.broadcasted_iota(jnp.int32, (8, tm), 0).astype(F32)
    dlocs = []
    for sub in range(rows // tm):
        gi = g_idx[:, sub * tm:(sub + 1) * tm]
        hot = jnp.where(sub8 == gi, 1.0, 0.0)
        before = _bdot(hot.astype(BF16), upper_ref[...])
        dl = jnp.sum(before * hot, axis=0, keepdims=True)
        off = 0
        for g in range(ng):
            n_g = _round_up(jnp.sum(hot[g:g + 1, :]).astype(jnp.int32), ROUTE_PAD)
            cnt_ref[(tile0 + sub) * ng + g] = n_g
            dl = dl + jnp.where(gi == float(g), off.astype(F32) if g else 0.0, 0.0)
            off = off + n_g
        drow_ref[sub] = jnp.broadcast_to(dl, (8, tm))
        dlocs.append(dl)
    dloc = jnp.concatenate(dlocs, axis=1)
    d_hi = jnp.floor(dloc * (1.0 / DLOC_RADIX))
    g_hi = [gt.astype(BF16).astype(F32) for gt in gates]
    ex_rows = g_hi + [gt - gh for gt, gh in zip(gates, g_hi)] + [d_hi, dloc - DLOC_RADIX * d_hi]
    sub16 = lax.broadcasted_iota(jnp.int32, (16, rows), 0)
    ex_t = jnp.zeros((16, rows), F32)
    for r, row in enumerate(ex_rows):
        ex_t = jnp.where(sub16 == r, row, ex_t)
    ex_t = jnp.concatenate([ex_t, jnp.zeros((LANES - 16, rows), F32)], axis=0)
    extras = ex_t.T
    dloc_ref[...] = jnp.broadcast_to(
        DLOC_RADIX * extras[:, DLOC_HI:DLOC_HI + 1] + extras[:, DLOC_HI + 1:DLOC_HI + 2], (rows, LANES))
    haug_ref[...] = jnp.concatenate([hb, extras.astype(BF16)], axis=1)


def _plan_consts(g2, w_rt, b_rt):
    upper = jnp.asarray(np.triu(np.ones((ROUTE_TM, ROUTE_TM), np.float32), 1), BF16)
    return [g2, w_rt, b_rt, upper]


def _plan_out(b, s, rows, index):
    tiles = rows // ROUTE_TM
    shapes = [jax.ShapeDtypeStruct((b, s, XS_W), BF16), jax.ShapeDtypeStruct((b, s, LANES), F32),
              jax.ShapeDtypeStruct((b, s // ROUTE_TM, 8, ROUTE_TM), F32),
              jax.ShapeDtypeStruct((b * (s // ROUTE_TM) * N_GROUPS_MOE,), jnp.int32)]
    specs = [pl.BlockSpec((None, rows, XS_W), lambda *idx: (*index(*idx), 0)),
             pl.BlockSpec((None, rows, LANES), lambda *idx: (*index(*idx), 0)),
             pl.BlockSpec((None, tiles, 8, ROUTE_TM), lambda *idx: (*index(*idx), 0, 0)),
             pl.BlockSpec(memory_space=pltpu.SMEM)]
    return shapes, specs


def _flat_plan(plan):
    haug, dloc, drow, cnt = plan
    return (haug.reshape(-1, XS_W), dloc.reshape(-1, LANES), drow.reshape(-1, 8, ROUTE_TM), cnt)


def _dispatch_kernel(cp_ref, cs_ref, hp_ref, hs_ref, dp_ref, ds_ref, tab_ref, sorted_hbm,
                     sorted_ref, zeros_ref, fill_ref, sem_ref, zsem_ref, *, n_steps, n_p_steps):
    i = pl.program_id(0)
    tm = ROUTE_TM
    ng = N_GROUPS_MOE
    n_tiles = n_steps * ROUTE_SUBS
    n_p_tiles = n_p_steps * ROUTE_SUBS
    meta = n_tiles * TAB_W
    n_rows = sorted_hbm.shape[0]

    def count(tile, g):
        if isinstance(tile, int):
            return cp_ref[tile * ng + g] if tile < n_p_tiles else cs_ref[(tile - n_p_tiles) * ng + g]
        return jnp.where(tile < n_p_tiles, cp_ref[jnp.minimum(tile, n_p_tiles - 1) * ng + g],
                         cs_ref[jnp.maximum(tile - n_p_tiles, 0) * ng + g])

    @pl.when(i == 0)
    def _():
        start = 0
        for g in range(ng):
            fill = sum(count(t, g) for t in range(n_tiles))
            tab_ref[meta + g] = fill
            tab_ref[meta + ng + g] = start
            start = start + _round_up(fill, FFN_BM)
            fill_ref[g] = 0
        zeros_ref[...] = jnp.zeros_like(zeros_ref)

    is_prompt = i < n_p_steps
    row_f = lax.broadcasted_iota(jnp.int32, (ROUTE_ROWS, tm), 0).astype(F32)

    def sort_tile(h_ref, d_ref, sub):
        onehot = jnp.where(row_f == d_ref[sub][0:1, :], 1.0, 0.0).astype(BF16)
        sorted_ref[sub] = _bdot(onehot, h_ref[sub * tm:(sub + 1) * tm, :])

    for sub in range(ROUTE_SUBS):
        tile = i * ROUTE_SUBS + sub

        @pl.when(i >= 1)
        def _():
            _wait_rows(_tile_rows(tab_ref, tile - ROUTE_SUBS), sorted_hbm, sorted_ref.at[sub], sem_ref.at[sub], True)

        pl.when(is_prompt)(functools.partial(sort_tile, hp_ref, dp_ref, sub))
        pl.when(jnp.logical_not(is_prompt))(functools.partial(sort_tile, hs_ref, ds_ref, sub))
        for g in range(ng):
            n_g = count(tile, g)
            tab_ref[tile * TAB_W + g] = tab_ref[meta + ng + g] + fill_ref[g]
            tab_ref[tile * TAB_W + ng + g] = n_g
            fill_ref[g] = fill_ref[g] + n_g
        _run_copies(tab_ref, tile, sorted_hbm, sorted_ref.at[sub], sem_ref.at[sub], to_hbm=True, wait=False)

        @pl.when(i == n_steps - 1)
        def _():
            _wait_rows(_tile_rows(tab_ref, tile), sorted_hbm, sorted_ref.at[sub], sem_ref.at[sub], True)

    @pl.when(i == n_steps - 1)
    def _():
        _zero_fill(tab_ref, meta, zeros_ref, sorted_hbm, zsem_ref, n_rows, wait=False)
        _zero_fill(tab_ref, meta, zeros_ref, sorted_hbm, zsem_ref, n_rows, wait=True)


def _merged_specs(n_p_steps, n_s_steps, layer, sample_seq, step_of, width=D_MODEL):
    def p_map(*idx):
        return (jnp.minimum(step_of(*idx), n_p_steps - 1), 0)

    def s_map(*idx):
        return (jnp.clip(step_of(*idx) - n_p_steps, 0, n_s_steps - 1), 0)

    def mod_map(*idx):
        j = step_of(*idx)
        row = jnp.where(j < n_p_steps, 0, 1 + lax.div(jnp.maximum(j - n_p_steps, 0) * ROUTE_STEP, sample_seq))
        return (layer, row, 0, 0)

    return (pl.BlockSpec((ROUTE_STEP, width), p_map), pl.BlockSpec((ROUTE_STEP, width), s_map),
            pl.BlockSpec((None, None, 6, D_MODEL), mod_map))


def _dispatch(plan_p, plan_s, n_rows):
    haug_p, _, drow_p, cnt_p = plan_p
    haug_s, _, drow_s, cnt_s = plan_s
    n_p_steps = haug_p.shape[0] // ROUTE_STEP
    n_s_steps = haug_s.shape[0] // ROUTE_STEP
    n_steps = n_p_steps + n_s_steps
    n_tiles = n_steps * ROUTE_SUBS
    step_of = lambda i, cp, cs: i
    hp_spec, hs_spec, _ = _merged_specs(n_p_steps, n_s_steps, 0, 1, step_of, XS_W)
    drow_block = (ROUTE_SUBS, 8, ROUTE_TM)
    grid_spec = pltpu.PrefetchScalarGridSpec(
        num_scalar_prefetch=2,
        grid=(n_steps,),
        in_specs=[
            hp_spec, hs_spec,
            pl.BlockSpec(drow_block, lambda i, cp, cs: (jnp.minimum(i, n_p_steps - 1), 0, 0)),
            pl.BlockSpec(drow_block, lambda i, cp, cs: (jnp.clip(i - n_p_steps, 0, n_s_steps - 1), 0, 0)),
        ],
        out_specs=[pl.BlockSpec(memory_space=pltpu.SMEM), pl.BlockSpec(memory_space=pl.ANY)],
        scratch_shapes=[
            pltpu.VMEM((ROUTE_SUBS, ROUTE_ROWS, XS_W), F32),
            pltpu.VMEM((FFN_BM, XS_W), F32),
            pltpu.SMEM((N_GROUPS_MOE,), jnp.int32),
            pltpu.SemaphoreType.DMA((ROUTE_SUBS,)),
            pltpu.SemaphoreType.DMA(()),
        ],
    )
    return pl.pallas_call(
        functools.partial(_dispatch_kernel, n_steps=n_steps, n_p_steps=n_p_steps),
        grid_spec=grid_spec,
        out_shape=[
            jax.ShapeDtypeStruct(((n_tiles + 1) * TAB_W,), jnp.int32),
            jax.ShapeDtypeStruct((n_rows, XS_W), F32),
        ],
        compiler_params=_cparams("arbitrary"),
        name="moe_dispatch",
    )(cnt_p, cnt_s, haug_p, haug_s, drow_p, drow_s)


def _ffn_lookup(i, tab_ref, meta):
    fills = [tab_ref[meta + g] for g in range(N_GROUPS_MOE)]
    edges = []
    acc = 0
    for f in fills:
        acc = acc + lax.div(f + (FFN_BM - 1), FFN_BM)
        edges.append(acc)
    total = edges[-1]
    ii = jnp.minimum(i, total - 1)
    grp = sum((ii >= e).astype(jnp.int32) for e in edges[:-1])

    def pick(vals):
        return jnp.where(grp == 0, vals[0], jnp.where(grp == 1, vals[1], jnp.where(grp == 2, vals[2], vals[3])))

    first = pick([0] + edges[:-1])
    return grp, total, ii == first, pick(fills) - (ii - first) * FFN_BM


def _ffn_group_kernel(tab_ref, xs_ref, w1_ref, w3_ref, w2_ref, ys_ref, w1b_ref, w3b_ref, w2b_ref, *, meta):
    i = pl.program_id(0)
    _, total, first_of_group, valid = _ffn_lookup(i, tab_ref, meta)
    active = i < total

    @pl.when(jnp.logical_and(active, first_of_group))
    def _():
        w1b_ref[...] = w1_ref[...].astype(BF16)
        w3b_ref[...] = w3_ref[...].astype(BF16)
        w2b_ref[...] = w2_ref[...].astype(BF16)

    def run(rows):
        hb = xs_ref[0:rows, 0:D_MODEL].astype(BF16)
        ex = xs_ref[0:rows, D_MODEL:XS_W]
        acc = None
        for e in range(EXPERTS_PER_GROUP):
            a = _bdot(hb, w1b_ref[e])
            b = _bdot(hb, w3b_ref[e])
            gate = ex[:, e:e + 1] + ex[:, GATE_LO + e:GATE_LO + e + 1]
            hid = (a * jax.nn.sigmoid(a)) * b * gate
            part = _bdot(hid.astype(BF16), w2b_ref[e])
            acc = part if acc is None else acc + part
        ys_ref[0:rows, :] = acc

    @pl.when(jnp.logical_and(active, valid > FFN_HALF))
    def _():
        run(FFN_BM)

    @pl.when(jnp.logical_and(active, valid <= FFN_HALF))
    def _():
        run(FFN_HALF)
        ys_ref[FFN_HALF:, :] = jnp.zeros((FFN_BM - FFN_HALF, D_MODEL), F32)

    @pl.when(jnp.logical_not(active))
    def _():
        ys_ref[...] = jnp.zeros_like(ys_ref)


def _ffn_group(tab, xs, w1, w3, w2, layer, meta):
    e4 = EXPERTS_PER_GROUP
    n_blocks = xs.shape[0] // FFN_BM
    group_of = lambda i, tab_ref: _ffn_lookup(i, tab_ref, meta)[0]
    grid_spec = pltpu.PrefetchScalarGridSpec(
        num_scalar_prefetch=1,
        grid=(n_blocks,),
        in_specs=[
            pl.BlockSpec((FFN_BM, XS_W), lambda i, tab_ref: (i, 0)),
            pl.BlockSpec((None, e4, D_MODEL, D_EXPERT), lambda i, tab_ref: (layer, group_of(i, tab_ref), 0, 0)),
            pl.BlockSpec((None, e4, D_MODEL, D_EXPERT), lambda i, tab_ref: (layer, group_of(i, tab_ref), 0, 0)),
            pl.BlockSpec((None, e4, D_EXPERT, D_MODEL), lambda i, tab_ref: (layer, group_of(i, tab_ref), 0, 0)),
        ],
        out_specs=pl.BlockSpec((FFN_BM, D_MODEL), lambda i, tab_ref: (i, 0)),
        scratch_shapes=[
            pltpu.VMEM((e4, D_MODEL, D_EXPERT), BF16),
            pltpu.VMEM((e4, D_MODEL, D_EXPERT), BF16),
            pltpu.VMEM((e4, D_EXPERT, D_MODEL), BF16),
        ],
    )
    return pl.pallas_call(
        functools.partial(_ffn_group_kernel, meta=meta),
        grid_spec=grid_spec,
        out_shape=jax.ShapeDtypeStruct((xs.shape[0], D_MODEL), F32),
        compiler_params=_cparams("arbitrary"),
        name="moe_ffn",
    )(tab, xs, w1, w3, w2)


def _fetch_runs(tab_ref, ys_ref, runs_ref, sem_ref, tile, slot):
    _run_copies(tab_ref, tile, ys_ref, runs_ref.at[slot], sem_ref.at[slot], to_hbm=False, wait=False)


def _unsort(tab_ref, ys_ref, runs_ref, sem_ref, dloc_col, tile, slot):
    covered = _tile_rows(tab_ref, tile)
    _wait_rows(covered, ys_ref, runs_ref.at[slot], sem_ref.at[slot], False)
    runs_ref[slot, pl.ds(pl.multiple_of(covered, ROUTE_PAD), LANES), :] = jnp.zeros((LANES, D_MODEL), F32)
    yb = runs_ref[slot, 0:SORT_ROWS, :].astype(BF16)
    row_f = lax.broadcasted_iota(jnp.int32, (ROUTE_TM, SORT_ROWS), 1).astype(F32)
    return _bdot(jnp.where(row_f == dloc_col, 1.0, 0.0).astype(BF16), yb)


def _combine_kernel(tab_ref, *refs, n_steps, first_step, n_p_steps, final_norm):
    with_prompt = first_step < n_p_steps
    if with_prompt:
        xp_ref, xs_ref, mod_ref, dp_ref, ds_ref, fg_ref, ys_ref, op_ref, os_ref, runs_ref, sem_ref = refs
    else:
        xs_ref, mod_ref, ds_ref, fg_ref, ys_ref, os_ref, runs_ref, sem_ref = refs
    i = pl.program_id(0)
    par = lax.rem(i, 2)
    step = first_step + i
    tm = ROUTE_TM

    def fetch(st, parity):
        for sub in range(ROUTE_SUBS):
            _fetch_runs(tab_ref, ys_ref, runs_ref, sem_ref, st * ROUTE_SUBS + sub, parity * ROUTE_SUBS + sub)

    @pl.when(i == 0)
    def _():
        fetch(first_step, 0)

    @pl.when(i + 1 < n_steps)
    def _():
        fetch(step + 1, 1 - par)

    is_prompt = step < n_p_steps
    dloc = jnp.where(is_prompt, dp_ref[:, 0:1], ds_ref[:, 0:1]) if with_prompt else ds_ref[:, 0:1]
    parts = [_unsort(tab_ref, ys_ref, runs_ref, sem_ref, dloc[sub * tm:(sub + 1) * tm],
                     step * ROUTE_SUBS + sub, par * ROUTE_SUBS + sub) for sub in range(ROUTE_SUBS)]
    delta = mod_ref[5:6, :] * jnp.concatenate(parts, axis=0)

    def finish(x_ref, o_ref):
        x2 = x_ref[...] + delta
        o_ref[...] = _rms(x2, fg_ref[...]) if final_norm else x2

    if with_prompt:
        pl.when(is_prompt)(functools.partial(finish, xp_ref, op_ref))
        pl.when(jnp.logical_not(is_prompt))(functools.partial(finish, xs_ref, os_ref))
    else:
        finish(xs_ref, os_ref)


def _combine(tab, xp, xs, mod, dloc_p, dloc_s, final_g, ys, layer, sample_seq, final_norm, with_prompt):
    n_p_steps = xp.shape[0] // ROUTE_STEP
    n_s_steps = xs.shape[0] // ROUTE_STEP
    first_step = 0 if with_prompt else n_p_steps
    n_steps = n_p_steps + n_s_steps - first_step
    step_of = lambda i, tab_ref: i + first_step
    p_spec, s_spec, mod_spec = _merged_specs(n_p_steps, n_s_steps, layer, sample_seq, step_of)
    dp_spec, ds_spec, _ = _merged_specs(n_p_steps, n_s_steps, layer, sample_seq, step_of, LANES)
    streams = [p_spec, s_spec] if with_prompt else [s_spec]
    dlocs = [dp_spec, ds_spec] if with_prompt else [ds_spec]
    grid_spec = pltpu.PrefetchScalarGridSpec(
        num_scalar_prefetch=1,
        grid=(n_steps,),
        in_specs=streams + [mod_spec] + dlocs + [
            pl.BlockSpec((1, D_MODEL), lambda i, tab_ref: (0, 0)),
            pl.BlockSpec(memory_space=pl.ANY),
        ],
        out_specs=streams,
        scratch_shapes=[
            pltpu.VMEM((2 * ROUTE_SUBS, RUN_ROWS, D_MODEL), F32),
            pltpu.SemaphoreType.DMA((2 * ROUTE_SUBS,)),
        ],
    )
    arrays = [xp, xs] if with_prompt else [xs]
    dloc_arrays = [dloc_p, dloc_s] if with_prompt else [dloc_s]
    return pl.pallas_call(
        functools.partial(_combine_kernel, n_steps=n_steps, first_step=first_step, n_p_steps=n_p_steps,
                          final_norm=final_norm),
        grid_spec=grid_spec,
        out_shape=[jax.ShapeDtypeStruct(a.shape, F32) for a in arrays],
        compiler_params=_cparams("arbitrary"),
        name="moe_combine",
    )(tab, *arrays, mod, *dloc_arrays, final_g, ys)


def _moe(xp, xs, plan_p, plan_s, mod, w1, w3, w2, final_g, layer, sample_seq, final_norm, defer_prompt):
    plan_p, plan_s = _flat_plan(plan_p), _flat_plan(plan_s)
    t = xp.shape[0] + xs.shape[0]
    n_tiles = t // ROUTE_TM
    max_rows = t + N_GROUPS_MOE * (ROUTE_PAD - 1) * n_tiles
    n_rows = (-(-max_rows // FFN_BM) + N_GROUPS_MOE) * FFN_BM
    tab, sorted_x = _dispatch(plan_p, plan_s, n_rows)
    ys = _ffn_group(tab, sorted_x, w1, w3, w2, layer, n_tiles * TAB_W)
    out = _combine(tab, xp, xs, mod, plan_p[1], plan_s[1], final_g, ys, layer, sample_seq, final_norm,
                   not defer_prompt)
    if defer_prompt:
        return None, out[0], (tab, plan_p[1], ys)
    return out[0], out[1], None


def _rope_tables(seq):
    half = QK_ROPE // 2
    nf = half // 2
    inv = ROPE_BASE ** (-np.arange(nf, dtype=np.float64) / nf)
    pos = np.arange(seq)
    row = (pos // GRID_W).astype(np.float64)
    col = (pos % GRID_W).astype(np.float64)
    cos = np.ones((seq, HEAD_PAD), np.float64)
    sin_a = np.zeros((seq, HEAD_PAD), np.float64)
    sin_b = np.zeros((seq, HEAD_PAD), np.float64)
    for part, p in enumerate((row, col)):
        ang = p[:, None] * inv[None, :]
        base = ROPE_OFF + part * half
        cos[:, base:base + nf] = np.cos(ang)
        cos[:, base + nf:base + half] = np.cos(ang)
        sin_a[:, base:base + nf] = -np.sin(ang)
        sin_b[:, base + nf:base + half] = np.sin(ang)
    return tuple(jnp.asarray(a, F32) for a in (cos, sin_a, sin_b))


def _apply_rope(x, cos, sin_a, sin_b, reps):
    nf = QK_ROPE // 4
    width = x.shape[1]
    if reps > 1:
        cos, sin_a, sin_b = (jnp.concatenate([a] * reps, axis=1) for a in (cos, sin_a, sin_b))
    return x * cos + pltpu.roll(x, width - nf, axis=1) * sin_a + pltpu.roll(x, nf, axis=1) * sin_b


def _odd_in_kernel(*refs, rope, emit_cache):
    x_ref, mod_ref, g_ref, w_ref, qg_ref, wq_ref, kg_ref, wk_ref, wv_ref, cs_ref = refs[:10]
    refs = refs[10:]
    if rope:
        cos_ref, sa_ref, sb_ref = refs[:3]
        refs = refs[3:]
    y_ref, q_ref, k_ref, v_ref = refs[:4]
    refs = refs[4:]
    m = mod_ref[...]
    h = _norm_mod(x_ref[...], g_ref[...], m[1:2], m[0:1])
    z = _bdot(h.astype(BF16), w_ref[...])
    zc = z[:, 0:D_C]
    qc = z[:, D_C:D_C + Q_LORA]
    kvc = z[:, D_C + Q_LORA:D_C + Q_LORA + KV_LORA]
    kpe = z[:, D_C + Q_LORA + KV_LORA:]
    q = _bdot(_rms(qc, qg_ref[...]).astype(BF16), wq_ref[...])
    kvn = _rms(kvc, kg_ref[...])
    if emit_cache:
        ckv_ref, kpe_ref = refs
        ckv_ref[...] = kvn
        kpe_ref[...] = kpe[:, ROPE_OFF:ROPE_OFF + QK_ROPE]
    if rope:
        tabs = (cos_ref[...], sa_ref[...], sb_ref[...])
        q = _apply_rope(q, *tabs, reps=N_HEADS)
        kpe = _apply_rope(kpe, *tabs, reps=1)
    kvb = kvn.astype(BF16)
    k = _bdot(kvb, wk_ref[...]) + jnp.concatenate([kpe] * N_HEADS, axis=1)
    scale = math.log2(math.e) / math.sqrt(QK_NOPE + QK_ROPE)
    q_ref[...] = (q * scale).astype(BF16)
    k_ref[...] = k.astype(BF16)
    v_ref[...] = _bdot(kvb, wv_ref[...]).astype(BF16)
    y = _bdot(zc.astype(BF16), cs_ref[...])
    y_ref[0, :, :] = y[:, 0:D_C].astype(BF16)
    y_ref[1, :, :] = y[:, D_C:2 * D_C].astype(BF16)


def _odd_in(x3, mod, g, w_in, q_g, w_q, kv_g, w_k, w_v, cs, stream, layer, rope_tabs, emit_cache):
    b, s, _ = x3.shape
    tm = min(s, 512)
    n_i = s // tm
    rope = rope_tabs is not None
    const = lambda a: pl.BlockSpec(a.shape, lambda bi, i: (0,) * a.ndim)
    in_specs = [
        pl.BlockSpec((None, tm, D_MODEL), lambda bi, i: (bi, i, 0)),
        _mod_spec(stream, layer),
        const(g), const(w_in), const(q_g), const(w_q), const(kv_g), const(w_k), const(w_v), const(cs),
    ]
    args = [x3, mod, g, w_in, q_g, w_q, kv_g, w_k, w_v, cs]
    if rope:
        in_specs += [pl.BlockSpec((tm, HEAD_PAD), lambda bi, i: (i, 0))] * 3
        args += list(rope_tabs)
    hq = N_HEADS * HEAD_PAD
    out_specs = [
        pl.BlockSpec((None, 2, tm, D_C), lambda bi, i: (bi, 0, i, 0)),
        pl.BlockSpec((None, tm, hq), lambda bi, i: (bi, i, 0)),
        pl.BlockSpec((None, tm, hq), lambda bi, i: (bi, i, 0)),
        pl.BlockSpec((None, tm, N_HEADS * V_DIM), lambda bi, i: (bi, i, 0)),
    ]
    out_shape = [
        jax.ShapeDtypeStruct((b, 2, s, D_C), BF16),
        jax.ShapeDtypeStruct((b, s, hq), BF16),
        jax.ShapeDtypeStruct((b, s, hq), BF16),
        jax.ShapeDtypeStruct((b, s, N_HEADS * V_DIM), BF16),
    ]
    if emit_cache:
        out_specs += [
            pl.BlockSpec((None, tm, KV_LORA), lambda bi, i: (bi, i, 0)),
            pl.BlockSpec((None, tm, QK_ROPE), lambda bi, i: (bi, i, 0)),
        ]
        out_shape += [
            jax.ShapeDtypeStruct((b, s, KV_LORA), F32),
            jax.ShapeDtypeStruct((b, s, QK_ROPE), F32),
        ]
    return pl.pallas_call(
        functools.partial(_odd_in_kernel, rope=rope, emit_cache=emit_cache),
        grid=(b, n_i),
        in_specs=in_specs,
        out_specs=out_specs,
        out_shape=out_shape,
        compiler_params=_cparams("parallel", "parallel"),
        name="odd_in",
    )(*args)


def _cache_kv_kernel(c_ref, p_ref, wk_ref, wv_ref, k_ref, v_ref):
    cb = c_ref[...].astype(BF16)
    k = _bdot(cb, wk_ref[...]) + jnp.concatenate([p_ref[...]] * N_HEADS, axis=1)
    k_ref[...] = k.astype(BF16)
    v_ref[...] = _bdot(cb, wv_ref[...]).astype(BF16)


def _cache_kv(ckv, kpe_blk, w_k, w_v):
    b, p, _ = ckv.shape
    hq = N_HEADS * HEAD_PAD
    return pl.pallas_call(
        _cache_kv_kernel,
        grid=(b,),
        in_specs=[
            pl.BlockSpec((None, p, KV_LORA), lambda bi: (bi, 0, 0)),
            pl.BlockSpec((None, p, HEAD_PAD), lambda bi: (bi, 0, 0)),
            pl.BlockSpec(w_k.shape, lambda bi: (0, 0)),
            pl.BlockSpec(w_v.shape, lambda bi: (0, 0)),
        ],
        out_specs=[
            pl.BlockSpec((None, p, hq), lambda bi: (bi, 0, 0)),
            pl.BlockSpec((None, p, N_HEADS * V_DIM), lambda bi: (bi, 0, 0)),
        ],
        out_shape=[
            jax.ShapeDtypeStruct((b, p, hq), BF16),
            jax.ShapeDtypeStruct((b, p, N_HEADS * V_DIM), BF16),
        ],
        compiler_params=_cparams("parallel"),
        name="cache_kv",
    )(ckv, kpe_blk, w_k, w_v)


def _odd_mix_kernel(*refs, with_cache, n_seq=1):
    q_ref, k_ref, v_ref = refs[:3]
    refs = refs[3:]
    if with_cache:
        kc_ref, vc_ref = refs[:2]
        refs = refs[2:]
    plan_tile = None
    if with_cache:
        y_ref, f_ref, x_ref, mod_ref, wo_ref = refs[:5]
        plan_in, (o_ref, *plan_out), (a_ref, vx_ref, vcx_ref) = refs[5:9], refs[9:14], refs[14:]
        plan_tile = pl.program_id(0) * pl.num_programs(1) + pl.program_id(1)

        @pl.when(pl.program_id(1) == 0)
        def _():
            for src, dst in ((v_ref, vx_ref), (vc_ref, vcx_ref)):
                one = lax.broadcasted_iota(jnp.int32, (src.shape[0], PAIR_W - DEN_COL), 1) == 0
                for pair in range(N_HEADS // 2):
                    dst[:, pair * PAIR_W:pair * PAIR_W + DEN_COL] = src[:, pair * DEN_COL:(pair + 1) * DEN_COL]
                    dst[:, pair * PAIR_W + DEN_COL:(pair + 1) * PAIR_W] = jnp.where(one, 1.0, 0.0).astype(BF16)
    else:
        y_ref, f_ref, x_ref, mod_ref, wo_ref, o_ref, a_ref = refs[:7]
        if len(refs) > 7:
            plan_in, plan_out, plan_tile = refs[7:]
    tq = q_ref.shape[0] // n_seq
    lane = lax.broadcasted_iota(jnp.int32, (tq, 2 * V_DIM), 1)
    f_parts = []
    for sq in range(n_seq):
        rows = slice(sq * tq, (sq + 1) * tq)
        for pair in range(N_HEADS // 2):
            vcols = slice(pair * PAIR_W, (pair + 1) * PAIR_W)
            outs = []
            for h in (2 * pair, 2 * pair + 1):
                hcols = slice(h * HEAD_PAD, (h + 1) * HEAD_PAD)
                qh = q_ref[rows, hcols]
                if with_cache:
                    s = lax.dot_general(qh, k_ref[:, hcols], NT_DIMS, preferred_element_type=F32)
                    sc = lax.dot_general(qh, kc_ref[:, hcols], NT_DIMS, preferred_element_type=F32)
                    top = jnp.maximum(jnp.max(s, axis=-1, keepdims=True), jnp.max(sc, axis=-1, keepdims=True))
                    acc = _bdot(jnp.exp2((s - top).astype(BF16)), vx_ref[:, vcols])
                    acc = acc + _bdot(jnp.exp2((sc - top).astype(BF16)), vcx_ref[:, vcols])
                    outs.append(acc[:, 0:2 * V_DIM] / acc[:, DEN_COL:DEN_COL + 1])
                else:
                    s = lax.dot_general(qh, k_ref[rows, hcols], NT_DIMS, preferred_element_type=F32)
                    p = jnp.exp2(s - jnp.max(s, axis=-1, keepdims=True))
                    den = jnp.sum(p, axis=-1, keepdims=True)
                    outs.append(_bdot(p.astype(BF16), v_ref[rows, pair * DEN_COL:(pair + 1) * DEN_COL]) / den)
            a_ref[rows, pair * 2 * V_DIM:(pair + 1) * 2 * V_DIM] = (
                jnp.where(lane < V_DIM, outs[0], outs[1]).astype(BF16))
        if n_seq > 1:
            f_parts.append(_bdot(f_ref[:, 0:tq], y_ref[0, rows, :]) + _bdot(f_ref[:, tq:], y_ref[1, rows, :]))
    f = jnp.concatenate(f_parts, axis=0) if n_seq > 1 else _bdot(f_ref[...], y_ref[...].reshape(-1, D_C))
    o = _bdot(f.astype(BF16), wo_ref[0:D_C, :]) + _bdot(a_ref[...], wo_ref[D_C:, :])
    x1 = x_ref[...] + mod_ref[2:3, :] * o
    o_ref[...] = x1
    if plan_tile is not None:
        _route_plan(x1, mod_ref[...], plan_in, plan_out, plan_tile)


WHOLE_SEQS = 2


def _odd_whole_kernel(tab_ref, x_ref, mod_ref, g_ref, w_ref, qg_ref, wq_ref, kg_ref, wk_ref, wv_ref, cs_ref, f_ref,
                      wo_ref, modp_ref, dloc_ref, ys_ref, pg_ref, pw_ref, pb_ref, pu_ref,
                      o_ref, ckv_ref, kpe_ref, ph_ref, pd_ref, pr_ref, pc_ref,
                      y_ref, q_ref, k_ref, v_ref, a_ref, x2_ref, runs_ref, sem_ref):
    bi = pl.program_id(0)
    par = lax.rem(bi, 2)
    tm = ROUTE_TM

    def fetch(step, parity):
        for sq in range(WHOLE_SEQS):
            _fetch_runs(tab_ref, ys_ref, runs_ref, sem_ref, step * WHOLE_SEQS + sq, parity * WHOLE_SEQS + sq)

    @pl.when(bi == 0)
    def _():
        fetch(0, 0)

    @pl.when(bi + 1 < pl.num_programs(0))
    def _():
        fetch(bi + 1, 1 - par)

    for sq in range(WHOLE_SEQS):
        rows = slice(sq * tm, (sq + 1) * tm)
        moe = _unsort(tab_ref, ys_ref, runs_ref, sem_ref, dloc_ref[rows, 0:1], bi * WHOLE_SEQS + sq,
                      par * WHOLE_SEQS + sq)
        x2_ref[rows, :] = x_ref[rows, :] + modp_ref[5:6, :] * moe
    _odd_in_kernel(x2_ref, mod_ref, g_ref, w_ref, qg_ref, wq_ref, kg_ref, wk_ref, wv_ref, cs_ref,
                   y_ref, q_ref, k_ref, v_ref, ckv_ref, kpe_ref, rope=False, emit_cache=True)
    _odd_mix_kernel(q_ref, k_ref, v_ref, y_ref, f_ref, x2_ref, mod_ref, wo_ref, o_ref, a_ref,
                    (pg_ref, pw_ref, pb_ref, pu_ref), (ph_ref, pd_ref, pr_ref, pc_ref), bi * WHOLE_SEQS,
                    with_cache=False, n_seq=WHOLE_SEQS)


def _odd_whole(x3, pending, mod, g, w_in, q_g, w_q, kv_g, w_k, w_v, cs, fmat, w_out, plan_consts, stream, layer):
    tab, dloc, ys = pending
    b, seq, _ = x3.shape
    assert seq == ROUTE_TM and b % WHOLE_SEQS == 0 and stream.shared_cond
    b, s = b // WHOLE_SEQS, seq * WHOLE_SEQS
    x3 = x3.reshape(b, s, D_MODEL)
    hq = N_HEADS * HEAD_PAD
    hv = N_HEADS * V_DIM
    const = lambda a: pl.BlockSpec(a.shape, lambda bi, tab_ref: (0,) * a.ndim)
    row_block = lambda w: pl.BlockSpec((None, s, w), lambda bi, tab_ref: (bi, 0, 0))
    mod_block = lambda lyr: pl.BlockSpec((None, None, 6, D_MODEL), lambda bi, tab_ref: (lyr, 0, 0, 0))
    plan_shapes, plan_specs = _plan_out(b, s, s, lambda bi, tab_ref: (bi, 0))
    grid_spec = pltpu.PrefetchScalarGridSpec(
        num_scalar_prefetch=1,
        grid=(b,),
        in_specs=[
            row_block(D_MODEL), mod_block(layer),
            const(g), const(w_in), const(q_g), const(w_q), const(kv_g), const(w_k), const(w_v), const(cs),
            const(fmat), const(w_out),
            mod_block(layer - 1),
            pl.BlockSpec((s, LANES), lambda bi, tab_ref: (bi, 0)),
            pl.BlockSpec(memory_space=pl.ANY),
        ] + [const(a) for a in plan_consts],
        out_specs=[row_block(D_MODEL), row_block(KV_LORA), row_block(QK_ROPE)] + plan_specs,
        scratch_shapes=[
            pltpu.VMEM((2, s, D_C), BF16),
            pltpu.VMEM((s, hq), BF16),
            pltpu.VMEM((s, hq), BF16),
            pltpu.VMEM((s, hv), BF16),
            pltpu.VMEM((s, hv), BF16),
            pltpu.VMEM((s, D_MODEL), F32),
            pltpu.VMEM((2 * WHOLE_SEQS, RUN_ROWS, D_MODEL), F32),
            pltpu.SemaphoreType.DMA((2 * WHOLE_SEQS,)),
        ],
    )
    out = pl.pallas_call(
        _odd_whole_kernel,
        grid_spec=grid_spec,
        out_shape=[
            jax.ShapeDtypeStruct((b, s, D_MODEL), F32),
            jax.ShapeDtypeStruct((b, s, KV_LORA), F32),
            jax.ShapeDtypeStruct((b, s, QK_ROPE), F32),
        ] + plan_shapes,
        compiler_params=_cparams("arbitrary"),
        name="odd_whole",
    )(tab, x3, mod, g, w_in, q_g, w_q, kv_g, w_k, w_v, cs, fmat, w_out, mod, dloc, ys, *plan_consts)
    shape = lambda w: (b * WHOLE_SEQS, seq, w)
    return out[0].reshape(shape(D_MODEL)), out[1].reshape(shape(KV_LORA)), out[2].reshape(shape(QK_ROPE)), out[3:]


def _odd_mix(q, k, v, kc, vc, y, fmat, x3, mod, w_out, plan_consts, stream, layer):
    b, s, hq = q.shape
    tq = min(s, 256)
    n_i = s // tq
    with_cache = kc is not None
    hv = N_HEADS * V_DIM
    mode = dict(pipeline_mode=pl.Buffered(1)) if with_cache else {}

    def per_batch(rows, cols):
        return pl.BlockSpec((None, rows, cols), lambda bi, i: (bi, 0, 0), **mode)

    in_specs = [pl.BlockSpec((None, tq, hq), lambda bi, i: (bi, i, 0)), per_batch(s, hq), per_batch(s, hv)]
    args = [q, k, v]
    if with_cache:
        p = kc.shape[1]
        in_specs += [per_batch(p, hq), per_batch(p, hv)]
        args += [kc, vc]
    in_specs += [
        per_batch(2 * s, D_C),
        pl.BlockSpec((tq, 2 * s), lambda bi, i: (i, 0)),
        pl.BlockSpec((None, tq, D_MODEL), lambda bi, i: (bi, i, 0)),
        _mod_spec(stream, layer),
        pl.BlockSpec(w_out.shape, lambda bi, i: (0, 0), **mode),
    ]
    in_specs += [pl.BlockSpec(a.shape, lambda bi, i: (0,) * a.ndim) for a in plan_consts]
    args += [y, fmat, x3, mod, w_out] + plan_consts
    plan_shapes, plan_specs = _plan_out(b, s, tq, lambda bi, i: (bi, i))
    out = pl.pallas_call(
        functools.partial(_odd_mix_kernel, with_cache=with_cache),
        grid=(b, n_i),
        in_specs=in_specs,
        out_specs=[pl.BlockSpec((None, tq, D_MODEL), lambda bi, i: (bi, i, 0))] + plan_specs,
        out_shape=[jax.ShapeDtypeStruct((b, s, D_MODEL), F32)] + plan_shapes,
        scratch_shapes=[pltpu.VMEM((tq, hv), BF16)] + (
            [pltpu.VMEM((s, VX_W), BF16), pltpu.VMEM((kc.shape[1], VX_W), BF16)] if with_cache else []),
        compiler_params=_cparams("arbitrary", "arbitrary"),
        name="odd_mix",
    )(*args)
    return out[0], out[1:]


def _dft_tables(seq):
    jc = np.arange(C_GW)
    ang_c = 2.0 * np.pi * np.outer(jc, jc) / C_GW
    eye = np.eye(C_GROUPS)
    cs = np.concatenate([np.kron(eye, np.cos(ang_c)), np.kron(eye, np.sin(ang_c))], axis=1)
    jn = np.arange(seq)
    ang_n = 2.0 * np.pi * (np.outer(jn, jn) % seq) / seq
    scale = 1.0 / math.sqrt(seq * C_GW)
    fmat = np.concatenate([np.cos(ang_n), -np.sin(ang_n)], axis=1) * scale
    return jnp.asarray(cs, F32).astype(BF16), jnp.asarray(fmat, F32).astype(BF16)


def _odd_weights(w_in, w_uq, w_ukv):
    d = w_in.shape[0]
    base = D_C + Q_LORA + KV_LORA
    kpe_blk = jnp.zeros((d, HEAD_PAD), w_in.dtype).at[:, ROPE_OFF:ROPE_OFF + QK_ROPE].set(w_in[:, base:])
    w_in_p = jnp.concatenate([w_in[:, :base], kpe_blk], axis=1).astype(BF16)
    qh = w_uq.reshape(Q_LORA, N_HEADS, QK_NOPE + QK_ROPE)
    w_q = jnp.pad(qh, ((0, 0), (0, 0), (0, HEAD_PAD - QK_NOPE - QK_ROPE))).reshape(Q_LORA, -1).astype(BF16)
    kvh = w_ukv.reshape(KV_LORA, N_HEADS, QK_NOPE + V_DIM)
    w_k = jnp.pad(kvh[:, :, :QK_NOPE], ((0, 0), (0, 0), (0, HEAD_PAD - QK_NOPE))).reshape(KV_LORA, -1)
    w_v = kvh[:, :, QK_NOPE:].reshape(KV_LORA, -1)
    return w_in_p, w_q, w_k.astype(BF16), w_v.astype(BF16)


ROUTER_ROWS = 32


def _router_weights(wg, bg, we, be):
    d = wg.shape[0]
    w = jnp.concatenate([wg, we.reshape(d, N_EXPERTS)], axis=1).T
    w = jnp.pad(w, ((0, ROUTER_ROWS - w.shape[0]), (0, 0))).astype(BF16)
    b = jnp.concatenate([bg, be.reshape(N_EXPERTS)])
    b = jnp.pad(b, (0, ROUTER_ROWS - b.shape[0])).reshape(ROUTER_ROWS, 1).astype(F32)
    return w, b


def kernel(x_prompt, x_sample, cache_ckv, cache_kpe, c, c_ctx, mod_w, mod_b, norm1_g, norm2_g,
           ev_w_in, ev_conv_w, ev_sgu_norm_g, ev_sgu_w, ev_sgu_b, ev_w_out,
           od_w_in, od_q_norm_g, od_w_uq, od_kv_norm_g, od_w_ukv, od_w_out,
           moe_wg, moe_bg, moe_we, moe_be, moe_w1, moe_w3, moe_w2, final_norm_g):
    bp, n_p, d = x_prompt.shape
    bs, n_s, _ = x_sample.shape
    streams = [(_Stream(bp, n_p, True), x_prompt), (_Stream(bs, n_s, False), x_sample)]

    n_rows = 1 + bs
    cond_t = jnp.concatenate([c_ctx[None, :], c], axis=0).T
    mod = _adaln(cond_t, mod_w, mod_b, n_rows)

    final_g = final_norm_g.reshape(1, d)
    xs = [x for _, x in streams]
    new_ckv, new_kpe = [], []
    pending = None
    plans = [None, None]
    for l in range(DEPTH):
        j = l // 2
        g1 = norm1_g[l].reshape(1, d)
        g2 = norm2_g[l].reshape(1, d)
        plan_consts = _plan_consts(g2, *_router_weights(moe_wg[l], moe_bg[l], moe_we[l], moe_be[l]))
        last = l == DEPTH - 1
        if l % 2 == 0:
            w_in = ev_w_in[j].astype(BF16)
            w_out = ev_w_out[j].astype(BF16)
            sgu_w = ev_sgu_w[j].astype(BF16)
            sgu_g = ev_sgu_norm_g[j].reshape(1, D_B)
            sgu_bias = jnp.repeat(ev_sgu_b[j].T, D_B // B_GROUPS, axis=1)
            for si, (st, _) in enumerate(streams):
                xs[si], plans[si] = _even_layer(xs[si], mod, g1, w_in, ev_conv_w[j], sgu_g, sgu_w, sgu_bias, w_out,
                                                plan_consts, st, l)
        else:
            w_in, w_q, w_k, w_v = _odd_weights(od_w_in[j], od_w_uq[j], od_w_ukv[j])
            w_out = od_w_out[j].astype(BF16)
            q_g = od_q_norm_g[j].reshape(1, Q_LORA)
            kv_g = od_kv_norm_g[j].reshape(1, KV_LORA)
            for si, (st, _) in enumerate(streams):
                x3 = xs[si]
                cs, fmat = _dft_tables(st.seq)
                if st.shared_cond:
                    xs[si], ckv, kpe, plans[si] = _odd_whole(x3, pending, mod, g1, w_in, q_g, w_q, kv_g, w_k, w_v, cs,
                                                             fmat, w_out, plan_consts, st, l)
                    new_ckv.append(ckv)
                    new_kpe.append(kpe)
                    continue
                y, q, k, v = _odd_in(x3, mod, g1, w_in, q_g, w_q, kv_g, w_k, w_v, cs, st, l,
                                     _rope_tables(st.seq), False)
                kpe_blk = jnp.pad(cache_kpe[:, j], ((0, 0), (0, 0), (ROPE_OFF, HEAD_PAD - ROPE_OFF - QK_ROPE)))
                kc, vc = _cache_kv(cache_ckv[:, j], kpe_blk, w_k, w_v)
                xs[si], plans[si] = _odd_mix(q, k, v, kc, vc, y.reshape(st.batch, 2 * st.seq, D_C), fmat, x3, mod,
                                             w_out, plan_consts, st, l)
        defer = not last and (l + 1) % 2 == 1
        x2p, x2s, pending = _moe(xs[0].reshape(bp * n_p, d), xs[1].reshape(bs * n_s, d), plans[0], plans[1], mod,
                                 moe_w1, moe_w3, moe_w2, final_g, l, n_s, last, defer)
        xs = [xs[0] if defer else x2p.reshape(bp, n_p, d), x2s.reshape(bs, n_s, d)]
    return (xs[0], xs[1], jnp.stack(new_ckv, axis=1), jnp.stack(new_kpe, axis=1))
```

```python
import functools
import math

import numpy as np
import jax
import jax.numpy as jnp
from jax import lax
from jax.experimental import pallas as pl
from jax.experimental.pallas import tpu as pltpu

D_MODEL = 1024
DEPTH = 2
GRID_W = 64
D_A = D_MODEL // 2
D_B = D_MODEL // 2
B_GROUPS = 4
CHUNK = 128
D_EVEN_IN = 3 * D_A + 2 * D_B
D_C = D_MODEL // 4
C_GROUPS = 4
C_GW = D_C // C_GROUPS
N_HEADS = 12
QK_NOPE = 64
QK_ROPE = 32
V_DIM = 64
Q_LORA = 384
KV_LORA = 256
ROPE_BASE = 10000.0
N_GROUPS_MOE = 4
EXPERTS_PER_GROUP = 4
N_EXPERTS = N_GROUPS_MOE * EXPERTS_PER_GROUP
D_EXPERT = 256
EPS = 1e-6

LANES = 128
HEAD_PAD = 128
PAIR_W = 256
DEN_COL = 2 * 64
VX_W = 6 * PAIR_W
ROPE_OFF = QK_NOPE
GATE_OFF = N_GROUPS_MOE
NEG_BIG = -1e30
F32 = jnp.float32
BF16 = jnp.bfloat16
VMEM_LIMIT = 56 * 1024 * 1024


def _cparams(*sem):
    return pltpu.CompilerParams(dimension_semantics=sem, vmem_limit_bytes=VMEM_LIMIT)


def _rms(x, g):
    return x * lax.rsqrt(jnp.mean(x * x, axis=-1, keepdims=True) + EPS) * g


def _norm_mod(x, g, scale, shift):
    rs = lax.rsqrt(jnp.mean(x * x, axis=-1, keepdims=True) + EPS)
    return x * rs * (g * (1.0 + scale)) + shift


def _bdot(a, b):
    return jnp.dot(a, b, preferred_element_type=F32)


NT_DIMS = (((1,), (1,)), ((), ()))
TN_DIMS = (((0,), (0,)), ((), ()))


def _mod_kernel(ct_ref, w_ref, b_ref, o_ref):
    c = ct_ref[...]
    s = c * jax.nn.sigmoid(c)
    w = w_ref[...]
    b = b_ref[...]
    for r in range(o_ref.shape[0]):
        o_ref[r:r + 1, :] = jnp.sum(s[:, r:r + 1] * w, axis=0, keepdims=True) + b


def _adaln(cond_t, mod_w, mod_b, n_rows):
    nt = 2048
    d6 = mod_w.shape[-1]
    out = pl.pallas_call(
        _mod_kernel,
        grid=(DEPTH, d6 // nt),
        in_specs=[
            pl.BlockSpec(cond_t.shape, lambda l, n: (0, 0)),
            pl.BlockSpec((None, D_MODEL, nt), lambda l, n: (l, 0, n)),
            pl.BlockSpec((None, 1, nt), lambda l, n: (l, 0, n)),
        ],
        out_specs=pl.BlockSpec((None, n_rows, nt), lambda l, n: (l, 0, n)),
        out_shape=jax.ShapeDtypeStruct((DEPTH, n_rows, d6), F32),
        compiler_params=_cparams("parallel", "parallel"),
        name="adaln",
    )(cond_t, mod_w, mod_b.reshape(DEPTH, 1, d6))
    return out.reshape(DEPTH, n_rows, 6, D_MODEL)


class _Stream:
    def __init__(self, batch, seq, shared_cond):
        self.batch = batch
        self.seq = seq
        self.tokens = batch * seq
        self.shared_cond = shared_cond

    def row_of_batch(self, b):
        return 0 if self.shared_cond else b + 1

    def row_of_tile(self, i, tm):
        return 0 if self.shared_cond else (i * tm) // self.seq + 1


def _mod_spec(stream, layer, tm=None):
    if tm is None:
        return pl.BlockSpec((None, None, 6, D_MODEL), lambda b, i: (layer, stream.row_of_batch(b), 0, 0))
    return pl.BlockSpec((None, None, 6, D_MODEL), lambda i, *_: (layer, stream.row_of_tile(i, tm), 0, 0))


HALO = 8


def _even_kernel(*refs, has_halo, seq_rows, nc):
    if has_halo:
        x_ref, xp_ref, xn_ref = refs[:3]
        refs = refs[3:]
    else:
        x_ref = refs[0]
        refs = refs[1:]
    mod_ref, g_ref, wi_ref, cw_ref, sg_ref, sw_ref, sb_ref, wo_ref = refs[:8]
    plan_in, (o_ref, *plan_out), (z_ref, y_ref) = refs[8:12], refs[12:17], refs[17:]
    i = pl.program_id(1)
    n_i = pl.num_programs(1)
    ts = x_ref.shape[0]
    m = mod_ref[...]
    g = g_ref[...]

    def modulate(x):
        return _norm_mod(x, g, m[1:2], m[0:1]).astype(BF16)

    x = x_ref[...]
    hb = modulate(x)
    for n in range(D_EVEN_IN // nc):
        z_ref[:, n * nc:(n + 1) * nc] = _bdot(hb, wi_ref[:, n * nc:(n + 1) * nc]).astype(BF16)

    gate_b = z_ref[:, 0:D_A].astype(F32)
    gate_c = z_ref[:, D_A:2 * D_A].astype(F32)
    xa = z_ref[:, 2 * D_A:3 * D_A].astype(F32)
    t = gate_c * xa
    t_prev = pltpu.roll(t, 1, axis=0)
    t_next = pltpu.roll(t, ts - 1, axis=0)
    row = lax.broadcasted_iota(jnp.int32, (ts, 1), 0) & (seq_rows - 1)
    if has_halo:
        hh = modulate(jnp.concatenate([xp_ref[...], xn_ref[...]], axis=0))
        zh = _bdot(hh, wi_ref[:, D_A:3 * D_A]).astype(BF16).astype(F32)
        th = zh[:, 0:D_A] * zh[:, D_A:2 * D_A]
        tp = th[HALO - 1:HALO] * (i > 0).astype(F32)
        tn = th[HALO:HALO + 1] * (i < n_i - 1).astype(F32)
    else:
        tp = tn = 0.0
    t_prev = jnp.where(row == 0, tp, t_prev)
    t_next = jnp.where(row == seq_rows - 1, tn, t_next)
    cw = cw_ref[...]
    y_a = gate_b * (t_prev * cw[0:1] + t * cw[1:2] + t_next * cw[2:3])
    y_ref[:, 0:D_A] = y_a.astype(BF16)

    u = z_ref[:, 3 * D_A:3 * D_A + D_B].astype(F32)
    v = z_ref[:, 3 * D_A + D_B:3 * D_A + 2 * D_B].astype(F32)
    vb = _rms(v, sg_ref[...]).astype(BF16)
    gw = D_B // B_GROUPS
    for c in range(ts // CHUNK):
        rows = slice(c * CHUNK, (c + 1) * CHUNK)
        for gi in range(B_GROUPS):
            cols = slice(gi * gw, (gi + 1) * gw)
            sv = _bdot(sw_ref[gi], vb[rows, cols]) + sb_ref[:, cols]
            y_ref[rows, D_A + gi * gw:D_A + (gi + 1) * gw] = (u[rows, cols] * sv).astype(BF16)

    x1 = x + m[2:3] * _bdot(y_ref[...], wo_ref[...])
    o_ref[...] = x1
    _route_plan(x1, m, plan_in, plan_out, (pl.program_id(0) * n_i + i) * (ts // ROUTE_TM))


def _even_layer(x3, mod, g, w_in, conv_w, sgu_g, sgu_w, sgu_bias, w_out, plan_consts, stream, layer):
    out_shape = x3.shape
    b, s, _ = x3.shape
    ts = min(s, 256) if stream.shared_cond else min(s, 512)
    n_i = s // ts
    has_halo = n_i > 1
    seq_rows = ts
    if not has_halo and stream.shared_cond and b % 2 == 0:
        b, s, ts = b // 2, 2 * s, 2 * ts
        x3 = x3.reshape(b, s, D_MODEL)
    hb = ts // HALO
    last_h = s // HALO - 1
    const = lambda a: pl.BlockSpec(a.shape, lambda bi, i: (0,) * a.ndim)
    in_specs = [pl.BlockSpec((None, ts, D_MODEL), lambda bi, i: (bi, i, 0))]
    args = [x3]
    if has_halo:
        in_specs += [
            pl.BlockSpec((None, HALO, D_MODEL), lambda bi, i: (bi, jnp.maximum(i * hb - 1, 0), 0)),
            pl.BlockSpec((None, HALO, D_MODEL), lambda bi, i: (bi, jnp.minimum((i + 1) * hb, last_h), 0)),
        ]
        args += [x3, x3]
    in_specs += [_mod_spec(stream, layer), const(g), const(w_in), const(conv_w), const(sgu_g), const(sgu_w),
                 const(sgu_bias), const(w_out)] + [const(a) for a in plan_consts]
    args += [mod, g, w_in, conv_w, sgu_g, sgu_w, sgu_bias, w_out] + plan_consts
    plan_shapes, plan_specs = _plan_out(b, s, ts, lambda bi, i: (bi, i))
    out = pl.pallas_call(
        functools.partial(_even_kernel, has_halo=has_halo, seq_rows=seq_rows, nc=512),
        grid=(b, n_i),
        in_specs=in_specs,
        out_specs=[pl.BlockSpec((None, ts, D_MODEL), lambda bi, i: (bi, i, 0))] + plan_specs,
        out_shape=[jax.ShapeDtypeStruct((b, s, D_MODEL), F32)] + plan_shapes,
        scratch_shapes=[pltpu.VMEM((ts, D_EVEN_IN), BF16), pltpu.VMEM((ts, D_A + D_B), BF16)],
        compiler_params=_cparams("arbitrary", "arbitrary"),
        name="even_layer",
    )(*args)
    return out[0].reshape(out_shape), out[1:]


ROUTE_TM = 256
ROUTE_SUBS = 4
ROUTE_STEP = ROUTE_TM * ROUTE_SUBS
ROUTE_PAD = 8
SORT_ROWS = ROUTE_TM + LANES
RUN_ROWS = SORT_ROWS + 32
ROUTE_ROWS = ROUTE_TM + 4 * ROUTE_PAD
XS_W = D_MODEL + LANES
GATE_LO = EXPERTS_PER_GROUP
DLOC_HI = 2 * EXPERTS_PER_GROUP
DLOC_RADIX = 16.0
FFN_BM = 512
FFN_HALF = FFN_BM // 2
RUN_SIZES = (256, 128, 64, 32, 16, 8)
TAB_W = 2 * N_GROUPS_MOE


def _round_up(x, m):
    return lax.div(x + (m - 1), m) * m


def _run_copies(tab_ref, tile, hbm_ref, vmem_ref, sem, to_hbm, wait):
    off = 0
    for g in range(N_GROUPS_MOE):
        start = tab_ref[tile * TAB_W + g]
        n = tab_ref[tile * TAB_W + N_GROUPS_MOE + g]
        for p in RUN_SIZES:
            done = n & (-2 * p)

            @pl.when((n & p) != 0)
            def _():
                v = vmem_ref.at[pl.ds(pl.multiple_of(off + done, ROUTE_PAD), p)]
                h = hbm_ref.at[pl.ds(pl.multiple_of(start + done, ROUTE_PAD), p)]
                cp = pltpu.make_async_copy(v, h, sem) if to_hbm else pltpu.make_async_copy(h, v, sem)
                if wait:
                    cp.wait()
                else:
                    cp.start()
        off = off + n


def _zero_fill(tab_ref, meta, zeros_ref, hbm_ref, sem, n_rows, wait):
    def copy(rows, dst_row):
        cp = pltpu.make_async_copy(zeros_ref.at[pl.ds(0, rows)],
                                   hbm_ref.at[pl.ds(pl.multiple_of(dst_row, ROUTE_PAD), rows)], sem)
        if wait:
            cp.wait()
        else:
            cp.start()

    end = 0
    for g in range(N_GROUPS_MOE):
        fill = tab_ref[meta + g]
        start = tab_ref[meta + N_GROUPS_MOE + g]
        end = start + _round_up(fill, FFN_BM)
        tail = end - start - fill
        for p in RUN_SIZES:
            pl.when((tail & p) != 0)(functools.partial(copy, p, start + fill + (tail & (-2 * p))))
    for k in range(n_rows // FFN_BM):
        pl.when(end + k * FFN_BM < n_rows)(functools.partial(copy, FFN_BM, end + k * FFN_BM))


def _wait_rows(n, hbm_ref, vmem_ref, sem, to_hbm):
    for p in RUN_SIZES:
        @pl.when((n & p) != 0)
        def _():
            v = vmem_ref.at[pl.ds(0, p)]
            h = hbm_ref.at[pl.ds(0, p)]
            (pltpu.make_async_copy(v, h, sem) if to_hbm else pltpu.make_async_copy(h, v, sem)).wait()


def _tile_rows(tab_ref, tile):
    n = 0
    for g in range(N_GROUPS_MOE):
        n = n + tab_ref[tile * TAB_W + N_GROUPS_MOE + g]
    return n


def _max4(v):
    return jnp.maximum(jnp.maximum(v[0], v[1]), jnp.maximum(v[2], v[3]))


def _first_of4(v, top):
    return jnp.where(v[0] == top, 0.0, jnp.where(v[1] == top, 1.0, jnp.where(v[2] == top, 2.0, 3.0)))


def _route_plan(x, m, plan_in, plan_out, tile0):
    g_ref, wrt_ref, brt_ref, upper_ref = plan_in
    haug_ref, dloc_ref, drow_ref, cnt_ref = plan_out
    rows = x.shape[0]
    tm = ROUTE_TM
    ng = N_GROUPS_MOE
    h = _norm_mod(x, g_ref[...], m[4:5], m[3:4])
    hb = h.astype(BF16)
    lt = lax.dot_general(wrt_ref[...], hb, NT_DIMS, preferred_element_type=F32) + brt_ref[...]
    gl = [lt[r:r + 1, :] for r in range(ng)]
    g_top = _max4(gl)
    g_idx = _first_of4(gl, g_top)
    g_w = 1.0 / (jnp.exp(gl[0] - g_top) + jnp.exp(gl[1] - g_top) + jnp.exp(gl[2] - g_top) + jnp.exp(gl[3] - g_top))
    ev = []
    for k in range(EXPERTS_PER_GROUP):
        cand = [lt[GATE_OFF + EXPERTS_PER_GROUP * r + k:GATE_OFF + EXPERTS_PER_GROUP * r + k + 1, :]
                for r in range(ng)]
        ev.append(jnp.where(g_idx == 0.0, cand[0], jnp.where(g_idx == 1.0, cand[1],
                            jnp.where(g_idx == 2.0, cand[2], cand[3]))))
    v1 = _max4(ev)
    i1 = _first_of4(ev, v1)
    rest = [jnp.where(i1 == float(k), NEG_BIG, ev[k]) for k in range(EXPERTS_PER_GROUP)]
    v2 = _max4(rest)
    i2 = _first_of4(rest, v2)
    e2 = jnp.exp(v2 - v1)
    w1 = 1.0 / (1.0 + e2)
    w2 = e2 * w1
    gates = [g_w * (jnp.where(i1 == float(k), w1, 0.0) + jnp.where(i2 == float(k), w2, 0.0))
             for k in range(EXPERTS_PER_GROUP)]

    sub8 = lax.broadcasted_iota(jnp.int32, (8, tm), 0).astype(F32)
    dlocs = []
    for sub in range(rows // tm):
        gi = g_idx[:, sub * tm:(sub + 1) * tm]
        hot = jnp.where(sub8 == gi, 1.0, 0.0)
        before = _bdot(hot.astype(BF16), upper_ref[...])
        dl = jnp.sum(before * hot, axis=0, keepdims=True)
        off = 0
        for g in range(ng):
            n_g = _round_up(jnp.sum(hot[g:g + 1, :]).astype(jnp.int32), ROUTE_PAD)
            cnt_ref[(tile0 + sub) * ng + g] = n_g
            dl = dl + jnp.where(gi == float(g), off.astype(F32) if g else 0.0, 0.0)
            off = off + n_g
        drow_ref[sub] = jnp.broadcast_to(dl, (8, tm))
        dlocs.append(dl)
    dloc = jnp.concatenate(dlocs, axis=1)
    d_hi = jnp.floor(dloc * (1.0 / DLOC_RADIX))
    g_hi = [gt.astype(BF16).astype(F32) for gt in gates]
    ex_rows = g_hi + [gt - gh for gt, gh in zip(gates, g_hi)] + [d_hi, dloc - DLOC_RADIX * d_hi]
    sub16 = lax.broadcasted_iota(jnp.int32, (16, rows), 0)
    ex_t = jnp.zeros((16, rows), F32)
    for r, row in enumerate(ex_rows):
        ex_t = jnp.where(sub16 == r, row, ex_t)
    ex_t = jnp.concatenate([ex_t, jnp.zeros((LANES - 16, rows), F32)], axis=0)
    extras = ex_t.T
    dloc_ref[...] = jnp.broadcast_to(
        DLOC_RADIX * extras[:, DLOC_HI:DLOC_HI + 1] + extras[:, DLOC_HI + 1:DLOC_HI + 2], (rows, LANES))
    haug_ref[...] = jnp.concatenate([hb, extras.astype(BF16)], axis=1)


def _plan_consts(g2, w_rt, b_rt):
    upper = jnp.asarray(np.triu(np.ones((ROUTE_TM, ROUTE_TM), np.float32), 1), BF16)
    return [g2, w_rt, b_rt, upper]


def _plan_out(b, s, rows, index):
    tiles = rows // ROUTE_TM
    shapes = [jax.ShapeDtypeStruct((b, s, XS_W), BF16), jax.ShapeDtypeStruct((b, s, LANES), F32),
              jax.ShapeDtypeStruct((b, s // ROUTE_TM, 8, ROUTE_TM), F32),
              jax.ShapeDtypeStruct((b * (s // ROUTE_TM) * N_GROUPS_MOE,), jnp.int32)]
    specs = [pl.BlockSpec((None, rows, XS_W), lambda *idx: (*index(*idx), 0)),
             pl.BlockSpec((None, rows, LANES), lambda *idx: (*index(*idx), 0)),
             pl.BlockSpec((None, tiles, 8, ROUTE_TM), lambda *idx: (*index(*idx), 0, 0)),
             pl.BlockSpec(memory_space=pltpu.SMEM)]
    return shapes, specs


def _flat_plan(plan):
    haug, dloc, drow, cnt = plan
    return (haug.reshape(-1, XS_W), dloc.reshape(-1, LANES), drow.reshape(-1, 8, ROUTE_TM), cnt)


def _dispatch_kernel(cp_ref, cs_ref, hp_ref, hs_ref, dp_ref, ds_ref, tab_ref, sorted_hbm,
                     sorted_ref, zeros_ref, fill_ref, sem_ref, zsem_ref, *, n_steps, n_p_steps):
    i = pl.program_id(0)
    tm = ROUTE_TM
    ng = N_GROUPS_MOE
    n_tiles = n_steps * ROUTE_SUBS
    n_p_tiles = n_p_steps * ROUTE_SUBS
    meta = n_tiles * TAB_W
    n_rows = sorted_hbm.shape[0]

    def count(tile, g):
        if isinstance(tile, int):
            return cp_ref[tile * ng + g] if tile < n_p_tiles else cs_ref[(tile - n_p_tiles) * ng + g]
        return jnp.where(tile < n_p_tiles, cp_ref[jnp.minimum(tile, n_p_tiles - 1) * ng + g],
                         cs_ref[jnp.maximum(tile - n_p_tiles, 0) * ng + g])

    @pl.when(i == 0)
    def _():
        start = 0
        for g in range(ng):
            fill = sum(count(t, g) for t in range(n_tiles))
            tab_ref[meta + g] = fill
            tab_ref[meta + ng + g] = start
            start = start + _round_up(fill, FFN_BM)
            fill_ref[g] = 0
        zeros_ref[...] = jnp.zeros_like(zeros_ref)

    is_prompt = i < n_p_steps
    row_f = lax.broadcasted_iota(jnp.int32, (ROUTE_ROWS, tm), 0).astype(F32)

    def sort_tile(h_ref, d_ref, sub):
        onehot = jnp.where(row_f == d_ref[sub][0:1, :], 1.0, 0.0).astype(BF16)
        sorted_ref[sub] = _bdot(onehot, h_ref[sub * tm:(sub + 1) * tm, :])

    for sub in range(ROUTE_SUBS):
        tile = i * ROUTE_SUBS + sub

        @pl.when(i >= 1)
        def _():
            _wait_rows(_tile_rows(tab_ref, tile - ROUTE_SUBS), sorted_hbm, sorted_ref.at[sub], sem_ref.at[sub], True)

        pl.when(is_prompt)(functools.partial(sort_tile, hp_ref, dp_ref, sub))
        pl.when(jnp.logical_not(is_prompt))(functools.partial(sort_tile, hs_ref, ds_ref, sub))
        for g in range(ng):
            n_g = count(tile, g)
            tab_ref[tile * TAB_W + g] = tab_ref[meta + ng + g] + fill_ref[g]
            tab_ref[tile * TAB_W + ng + g] = n_g
            fill_ref[g] = fill_ref[g] + n_g
        _run_copies(tab_ref, tile, sorted_hbm, sorted_ref.at[sub], sem_ref.at[sub], to_hbm=True, wait=False)

        @pl.when(i == n_steps - 1)
        def _():
            _wait_rows(_tile_rows(tab_ref, tile), sorted_hbm, sorted_ref.at[sub], sem_ref.at[sub], True)

    @pl.when(i == n_steps - 1)
    def _():
        _zero_fill(tab_ref, meta, zeros_ref, sorted_hbm, zsem_ref, n_rows, wait=False)
        _zero_fill(tab_ref, meta, zeros_ref, sorted_hbm, zsem_ref, n_rows, wait=True)


def _merged_specs(n_p_steps, n_s_steps, layer, sample_seq, step_of, width=D_MODEL):
    def p_map(*idx):
        return (jnp.minimum(step_of(*idx), n_p_steps - 1), 0)

    def s_map(*idx):
        return (jnp.clip(step_of(*idx) - n_p_steps, 0, n_s_steps - 1), 0)

    def mod_map(*idx):
        j = step_of(*idx)
        row = jnp.where(j < n_p_steps, 0, 1 + lax.div(jnp.maximum(j - n_p_steps, 0) * ROUTE_STEP, sample_seq))
        return (layer, row, 0, 0)

    return (pl.BlockSpec((ROUTE_STEP, width), p_map), pl.BlockSpec((ROUTE_STEP, width), s_map),
            pl.BlockSpec((None, None, 6, D_MODEL), mod_map))


def _dispatch(plan_p, plan_s, n_rows):
    haug_p, _, drow_p, cnt_p = plan_p
    haug_s, _, drow_s, cnt_s = plan_s
    n_p_steps = haug_p.shape[0] // ROUTE_STEP
    n_s_steps = haug_s.shape[0] // ROUTE_STEP
    n_steps = n_p_steps + n_s_steps
    n_tiles = n_steps * ROUTE_SUBS
    step_of = lambda i, cp, cs: i
    hp_spec, hs_spec, _ = _merged_specs(n_p_steps, n_s_steps, 0, 1, step_of, XS_W)
    drow_block = (ROUTE_SUBS, 8, ROUTE_TM)
    grid_spec = pltpu.PrefetchScalarGridSpec(
        num_scalar_prefetch=2,
        grid=(n_steps,),
        in_specs=[
            hp_spec, hs_spec,
            pl.BlockSpec(drow_block, lambda i, cp, cs: (jnp.minimum(i, n_p_steps - 1), 0, 0)),
            pl.BlockSpec(drow_block, lambda i, cp, cs: (jnp.clip(i - n_p_steps, 0, n_s_steps - 1), 0, 0)),
        ],
        out_specs=[pl.BlockSpec(memory_space=pltpu.SMEM), pl.BlockSpec(memory_space=pl.ANY)],
        scratch_shapes=[
            pltpu.VMEM((ROUTE_SUBS, ROUTE_ROWS, XS_W), F32),
            pltpu.VMEM((FFN_BM, XS_W), F32),
            pltpu.SMEM((N_GROUPS_MOE,), jnp.int32),
            pltpu.SemaphoreType.DMA((ROUTE_SUBS,)),
            pltpu.SemaphoreType.DMA(()),
        ],
    )
    return pl.pallas_call(
        functools.partial(_dispatch_kernel, n_steps=n_steps, n_p_steps=n_p_steps),
        grid_spec=grid_spec,
        out_shape=[
            jax.ShapeDtypeStruct(((n_tiles + 1) * TAB_W,), jnp.int32),
            jax.ShapeDtypeStruct((n_rows, XS_W), F32),
        ],
        compiler_params=_cparams("arbitrary"),
        name="moe_dispatch",
    )(cnt_p, cnt_s, haug_p, haug_s, drow_p, drow_s)


def _ffn_lookup(i, tab_ref, meta):
    fills = [tab_ref[meta + g] for g in range(N_GROUPS_MOE)]
    edges = []
    acc = 0
    for f in fills:
        acc = acc + lax.div(f + (FFN_BM - 1), FFN_BM)
        edges.append(acc)
    total = edges[-1]
    ii = jnp.minimum(i, total - 1)
    grp = sum((ii >= e).astype(jnp.int32) for e in edges[:-1])

    def pick(vals):
        return jnp.where(grp == 0, vals[0], jnp.where(grp == 1, vals[1], jnp.where(grp == 2, vals[2], vals[3])))

    first = pick([0] + edges[:-1])
    return grp, total, ii == first, pick(fills) - (ii - first) * FFN_BM


def _ffn_group_kernel(tab_ref, xs_ref, w1_ref, w3_ref, w2_ref, ys_ref, w1b_ref, w3b_ref, w2b_ref, *, meta):
    i = pl.program_id(0)
    _, total, first_of_group, valid = _ffn_lookup(i, tab_ref, meta)
    active = i < total

    @pl.when(jnp.logical_and(active, first_of_group))
    def _():
        w1b_ref[...] = w1_ref[...].astype(BF16)
        w3b_ref[...] = w3_ref[...].astype(BF16)
        w2b_ref[...] = w2_ref[...].astype(BF16)

    def run(rows):
        hb = xs_ref[0:rows, 0:D_MODEL].astype(BF16)
        ex = xs_ref[0:rows, D_MODEL:XS_W]
        hid = []
        for e in range(EXPERTS_PER_GROUP):
            a = _bdot(hb, w1b_ref[e])
            b = _bdot(hb, w3b_ref[e])
            gate = ex[:, e:e + 1] + ex[:, GATE_LO + e:GATE_LO + e + 1]
            hid.append(((a * jax.nn.sigmoid(a)) * b * gate).astype(BF16))
        ys_ref[0:rows, :] = _bdot(jnp.concatenate(hid, axis=1), w2b_ref[...].reshape(-1, D_MODEL))

    @pl.when(jnp.logical_and(active, valid > FFN_HALF))
    def _():
        run(FFN_BM)

    @pl.when(jnp.logical_and(active, valid <= FFN_HALF))
    def _():
        run(FFN_HALF)
        ys_ref[FFN_HALF:, :] = jnp.zeros((FFN_BM - FFN_HALF, D_MODEL), F32)

    @pl.when(jnp.logical_not(active))
    def _():
        ys_ref[...] = jnp.zeros_like(ys_ref)


def _ffn_group(tab, xs, w1, w3, w2, layer, meta):
    e4 = EXPERTS_PER_GROUP
    n_blocks = xs.shape[0] // FFN_BM
    group_of = lambda i, tab_ref: _ffn_lookup(i, tab_ref, meta)[0]
    grid_spec = pltpu.PrefetchScalarGridSpec(
        num_scalar_prefetch=1,
        grid=(n_blocks,),
        in_specs=[
            pl.BlockSpec((FFN_BM, XS_W), lambda i, tab_ref: (i, 0)),
            pl.BlockSpec((None, e4, D_MODEL, D_EXPERT), lambda i, tab_ref: (layer, group_of(i, tab_ref), 0, 0)),
            pl.BlockSpec((None, e4, D_MODEL, D_EXPERT), lambda i, tab_ref: (layer, group_of(i, tab_ref), 0, 0)),
            pl.BlockSpec((None, e4, D_EXPERT, D_MODEL), lambda i, tab_ref: (layer, group_of(i, tab_ref), 0, 0)),
        ],
        out_specs=pl.BlockSpec((FFN_BM, D_MODEL), lambda i, tab_ref: (i, 0)),
        scratch_shapes=[
            pltpu.VMEM((e4, D_MODEL, D_EXPERT), BF16),
            pltpu.VMEM((e4, D_MODEL, D_EXPERT), BF16),
            pltpu.VMEM((e4, D_EXPERT, D_MODEL), BF16),
        ],
    )
    return pl.pallas_call(
        functools.partial(_ffn_group_kernel, meta=meta),
        grid_spec=grid_spec,
        out_shape=jax.ShapeDtypeStruct((xs.shape[0], D_MODEL), F32),
        compiler_params=_cparams("arbitrary"),
        name="moe_ffn",
    )(tab, xs, w1, w3, w2)


def _fetch_runs(tab_ref, ys_ref, runs_ref, sem_ref, tile, slot):
    _run_copies(tab_ref, tile, ys_ref, runs_ref.at[slot], sem_ref.at[slot], to_hbm=False, wait=False)


def _unsort(tab_ref, ys_ref, runs_ref, sem_ref, dloc_col, tile, slot):
    covered = _tile_rows(tab_ref, tile)
    _wait_rows(covered, ys_ref, runs_ref.at[slot], sem_ref.at[slot], False)
    runs_ref[slot, pl.ds(pl.multiple_of(covered, ROUTE_PAD), LANES), :] = jnp.zeros((LANES, D_MODEL), F32)
    yb = runs_ref[slot, 0:SORT_ROWS, :].astype(BF16)
    row_f = lax.broadcasted_iota(jnp.int32, (ROUTE_TM, SORT_ROWS), 1).astype(F32)
    return _bdot(jnp.where(row_f == dloc_col, 1.0, 0.0).astype(BF16), yb)


def _combine_kernel(tab_ref, *refs, n_steps, first_step, n_p_steps, final_norm):
    with_prompt = first_step < n_p_steps
    if with_prompt:
        xp_ref, xs_ref, mod_ref, dp_ref, ds_ref, fg_ref, ys_ref, op_ref, os_ref, runs_ref, sem_ref = refs
    else:
        xs_ref, mod_ref, ds_ref, fg_ref, ys_ref, os_ref, runs_ref, sem_ref = refs
    i = pl.program_id(0)
    par = lax.rem(i, 2)
    step = first_step + i
    tm = ROUTE_TM

    def fetch(st, parity):
        for sub in range(ROUTE_SUBS):
            _fetch_runs(tab_ref, ys_ref, runs_ref, sem_ref, st * ROUTE_SUBS + sub, parity * ROUTE_SUBS + sub)

    @pl.when(i == 0)
    def _():
        fetch(first_step, 0)

    @pl.when(i + 1 < n_steps)
    def _():
        fetch(step + 1, 1 - par)

    is_prompt = step < n_p_steps
    dloc = jnp.where(is_prompt, dp_ref[:, 0:1], ds_ref[:, 0:1]) if with_prompt else ds_ref[:, 0:1]
    parts = [_unsort(tab_ref, ys_ref, runs_ref, sem_ref, dloc[sub * tm:(sub + 1) * tm],
                     step * ROUTE_SUBS + sub, par * ROUTE_SUBS + sub) for sub in range(ROUTE_SUBS)]
    delta = mod_ref[5:6, :] * jnp.concatenate(parts, axis=0)

    def finish(x_ref, o_ref):
        x2 = x_ref[...] + delta
        o_ref[...] = _rms(x2, fg_ref[...]) if final_norm else x2

    if with_prompt:
        pl.when(is_prompt)(functools.partial(finish, xp_ref, op_ref))
        pl.when(jnp.logical_not(is_prompt))(functools.partial(finish, xs_ref, os_ref))
    else:
        finish(xs_ref, os_ref)


def _combine(tab, xp, xs, mod, dloc_p, dloc_s, final_g, ys, layer, sample_seq, final_norm, with_prompt):
    n_p_steps = xp.shape[0] // ROUTE_STEP
    n_s_steps = xs.shape[0] // ROUTE_STEP
    first_step = 0 if with_prompt else n_p_steps
    n_steps = n_p_steps + n_s_steps - first_step
    step_of = lambda i, tab_ref: i + first_step
    p_spec, s_spec, mod_spec = _merged_specs(n_p_steps, n_s_steps, layer, sample_seq, step_of)
    dp_spec, ds_spec, _ = _merged_specs(n_p_steps, n_s_steps, layer, sample_seq, step_of, LANES)
    streams = [p_spec, s_spec] if with_prompt else [s_spec]
    dlocs = [dp_spec, ds_spec] if with_prompt else [ds_spec]
    grid_spec = pltpu.PrefetchScalarGridSpec(
        num_scalar_prefetch=1,
        grid=(n_steps,),
        in_specs=streams + [mod_spec] + dlocs + [
            pl.BlockSpec((1, D_MODEL), lambda i, tab_ref: (0, 0)),
            pl.BlockSpec(memory_space=pl.ANY),
        ],
        out_specs=streams,
        scratch_shapes=[
            pltpu.VMEM((2 * ROUTE_SUBS, RUN_ROWS, D_MODEL), F32),
            pltpu.SemaphoreType.DMA((2 * ROUTE_SUBS,)),
        ],
    )
    arrays = [xp, xs] if with_prompt else [xs]
    dloc_arrays = [dloc_p, dloc_s] if with_prompt else [dloc_s]
    return pl.pallas_call(
        functools.partial(_combine_kernel, n_steps=n_steps, first_step=first_step, n_p_steps=n_p_steps,
                          final_norm=final_norm),
        grid_spec=grid_spec,
        out_shape=[jax.ShapeDtypeStruct(a.shape, F32) for a in arrays],
        compiler_params=_cparams("arbitrary"),
        name="moe_combine",
    )(tab, *arrays, mod, *dloc_arrays, final_g, ys)


def _moe(xp, xs, plan_p, plan_s, mod, w1, w3, w2, final_g, layer, sample_seq, final_norm, defer_prompt):
    plan_p, plan_s = _flat_plan(plan_p), _flat_plan(plan_s)
    t = xp.shape[0] + xs.shape[0]
    n_tiles = t // ROUTE_TM
    max_rows = t + N_GROUPS_MOE * (ROUTE_PAD - 1) * n_tiles
    n_rows = (-(-max_rows // FFN_BM) + N_GROUPS_MOE) * FFN_BM
    tab, sorted_x = _dispatch(plan_p, plan_s, n_rows)
    ys = _ffn_group(tab, sorted_x, w1, w3, w2, layer, n_tiles * TAB_W)
    out = _combine(tab, xp, xs, mod, plan_p[1], plan_s[1], final_g, ys, layer, sample_seq, final_norm,
                   not defer_prompt)
    if defer_prompt:
        return None, out[0], (tab, plan_p[1], ys)
    return out[0], out[1], None


def _rope_tables(seq):
    half = QK_ROPE // 2
    nf = half // 2
    inv = ROPE_BASE ** (-np.arange(nf, dtype=np.float64) / nf)
    pos = np.arange(seq)
    row = (pos // GRID_W).astype(np.float64)
    col = (pos % GRID_W).astype(np.float64)
    cos = np.ones((seq, HEAD_PAD), np.float64)
    sin_a = np.zeros((seq, HEAD_PAD), np.float64)
    sin_b = np.zeros((seq, HEAD_PAD), np.float64)
    for part, p in enumerate((row, col)):
        ang = p[:, None] * inv[None, :]
        base = ROPE_OFF + part * half
        cos[:, base:base + nf] = np.cos(ang)
        cos[:, base + nf:base + half] = np.cos(ang)
        sin_a[:, base:base + nf] = -np.sin(ang)
        sin_b[:, base + nf:base + half] = np.sin(ang)
    return tuple(jnp.asarray(a, F32) for a in (cos, sin_a, sin_b))


def _apply_rope(x, cos, sin_a, sin_b, reps):
    nf = QK_ROPE // 4
    width = x.shape[1]
    if reps > 1:
        cos, sin_a, sin_b = (jnp.concatenate([a] * reps, axis=1) for a in (cos, sin_a, sin_b))
    return x * cos + pltpu.roll(x, width - nf, axis=1) * sin_a + pltpu.roll(x, nf, axis=1) * sin_b


def _odd_in_kernel(*refs, rope, emit_cache):
    x_ref, mod_ref, g_ref, w_ref, qg_ref, wq_ref, kg_ref, wk_ref, wv_ref, cs_ref = refs[:10]
    refs = refs[10:]
    if rope:
        cos_ref, sa_ref, sb_ref = refs[:3]
        refs = refs[3:]
    y_ref, q_ref, k_ref, v_ref = refs[:4]
    refs = refs[4:]
    m = mod_ref[...]
    h = _norm_mod(x_ref[...], g_ref[...], m[1:2], m[0:1])
    z = _bdot(h.astype(BF16), w_ref[...])
    zc = z[:, 0:D_C]
    qc = z[:, D_C:D_C + Q_LORA]
    kvc = z[:, D_C + Q_LORA:D_C + Q_LORA + KV_LORA]
    kpe = z[:, D_C + Q_LORA + KV_LORA:]
    q = _bdot(_rms(qc, qg_ref[...]).astype(BF16), wq_ref[...])
    kvn = _rms(kvc, kg_ref[...])
    if emit_cache:
        ckv_ref, kpe_ref = refs
        ckv_ref[...] = kvn
        kpe_ref[...] = kpe[:, ROPE_OFF:ROPE_OFF + QK_ROPE]
    if rope:
        tabs = (cos_ref[...], sa_ref[...], sb_ref[...])
        q = _apply_rope(q, *tabs, reps=N_HEADS)
        kpe = _apply_rope(kpe, *tabs, reps=1)
    kvb = kvn.astype(BF16)
    k = _bdot(kvb, wk_ref[...]) + jnp.concatenate([kpe] * N_HEADS, axis=1)
    scale = math.log2(math.e) / math.sqrt(QK_NOPE + QK_ROPE)
    q_ref[...] = (q * scale).astype(BF16)
    k_ref[...] = k.astype(BF16)
    v_ref[...] = _bdot(kvb, wv_ref[...]).astype(BF16)
    y = _bdot(zc.astype(BF16), cs_ref[...])
    y_ref[0, :, :] = y[:, 0:D_C].astype(BF16)
    y_ref[1, :, :] = y[:, D_C:2 * D_C].astype(BF16)


def _odd_in(x3, mod, g, w_in, q_g, w_q, kv_g, w_k, w_v, cs, stream, layer, rope_tabs, emit_cache):
    b, s, _ = x3.shape
    tm = min(s, 512)
    n_i = s // tm
    rope = rope_tabs is not None
    const = lambda a: pl.BlockSpec(a.shape, lambda bi, i: (0,) * a.ndim)
    in_specs = [
        pl.BlockSpec((None, tm, D_MODEL), lambda bi, i: (bi, i, 0)),
        _mod_spec(stream, layer),
        const(g), const(w_in), const(q_g), const(w_q), const(kv_g), const(w_k), const(w_v), const(cs),
    ]
    args = [x3, mod, g, w_in, q_g, w_q, kv_g, w_k, w_v, cs]
    if rope:
        in_specs += [pl.BlockSpec((tm, HEAD_PAD), lambda bi, i: (i, 0))] * 3
        args += list(rope_tabs)
    hq = N_HEADS * HEAD_PAD
    out_specs = [
        pl.BlockSpec((None, 2, tm, D_C), lambda bi, i: (bi, 0, i, 0)),
        pl.BlockSpec((None, tm, hq), lambda bi, i: (bi, i, 0)),
        pl.BlockSpec((None, tm, hq), lambda bi, i: (bi, i, 0)),
        pl.BlockSpec((None, tm, N_HEADS * V_DIM), lambda bi, i: (bi, i, 0)),
    ]
    out_shape = [
        jax.ShapeDtypeStruct((b, 2, s, D_C), BF16),
        jax.ShapeDtypeStruct((b, s, hq), BF16),
        jax.ShapeDtypeStruct((b, s, hq), BF16),
        jax.ShapeDtypeStruct((b, s, N_HEADS * V_DIM), BF16),
    ]
    if emit_cache:
        out_specs += [
            pl.BlockSpec((None, tm, KV_LORA), lambda bi, i: (bi, i, 0)),
            pl.BlockSpec((None, tm, QK_ROPE), lambda bi, i: (bi, i, 0)),
        ]
        out_shape += [
            jax.ShapeDtypeStruct((b, s, KV_LORA), F32),
            jax.ShapeDtypeStruct((b, s, QK_ROPE), F32),
        ]
    return pl.pallas_call(
        functools.partial(_odd_in_kernel, rope=rope, emit_cache=emit_cache),
        grid=(b, n_i),
        in_specs=in_specs,
        out_specs=out_specs,
        out_shape=out_shape,
        compiler_params=_cparams("parallel", "parallel"),
        name="odd_in",
    )(*args)


def _cache_kv_kernel(c_ref, p_ref, wk_ref, wv_ref, k_ref, v_ref):
    cb = c_ref[...].astype(BF16)
    k = _bdot(cb, wk_ref[...]) + jnp.concatenate([p_ref[...]] * N_HEADS, axis=1)
    k_ref[...] = k.astype(BF16)
    v_ref[...] = _bdot(cb, wv_ref[...]).astype(BF16)


def _cache_kv(ckv, kpe_blk, w_k, w_v):
    b, p, _ = ckv.shape
    hq = N_HEADS * HEAD_PAD
    return pl.pallas_call(
        _cache_kv_kernel,
        grid=(b,),
        in_specs=[
            pl.BlockSpec((None, p, KV_LORA), lambda bi: (bi, 0, 0)),
            pl.BlockSpec((None, p, HEAD_PAD), lambda bi: (bi, 0, 0)),
            pl.BlockSpec(w_k.shape, lambda bi: (0, 0)),
            pl.BlockSpec(w_v.shape, lambda bi: (0, 0)),
        ],
        out_specs=[
            pl.BlockSpec((None, p, hq), lambda bi: (bi, 0, 0)),
            pl.BlockSpec((None, p, N_HEADS * V_DIM), lambda bi: (bi, 0, 0)),
        ],
        out_shape=[
            jax.ShapeDtypeStruct((b, p, hq), BF16),
            jax.ShapeDtypeStruct((b, p, N_HEADS * V_DIM), BF16),
        ],
        compiler_params=_cparams("parallel"),
        name="cache_kv",
    )(ckv, kpe_blk, w_k, w_v)


def _odd_mix_kernel(*refs, with_cache, n_seq=1):
    q_ref, k_ref, v_ref = refs[:3]
    refs = refs[3:]
    if with_cache:
        kc_ref, vc_ref = refs[:2]
        refs = refs[2:]
    plan_tile = None
    if with_cache:
        y_ref, f_ref, x_ref, mod_ref, wo_ref = refs[:5]
        plan_in, (o_ref, *plan_out), (a_ref, vx_ref, vcx_ref) = refs[5:9], refs[9:14], refs[14:]
        plan_tile = pl.program_id(0) * pl.num_programs(1) + pl.program_id(1)

        @pl.when(pl.program_id(1) == 0)
        def _():
            for src, dst in ((v_ref, vx_ref), (vc_ref, vcx_ref)):
                one = lax.broadcasted_iota(jnp.int32, (src.shape[0], PAIR_W - DEN_COL), 1) == 0
                for pair in range(N_HEADS // 2):
                    dst[:, pair * PAIR_W:pair * PAIR_W + DEN_COL] = src[:, pair * DEN_COL:(pair + 1) * DEN_COL]
                    dst[:, pair * PAIR_W + DEN_COL:(pair + 1) * PAIR_W] = jnp.where(one, 1.0, 0.0).astype(BF16)
    else:
        y_ref, f_ref, x_ref, mod_ref, wo_ref, o_ref, a_ref = refs[:7]
        if len(refs) > 7:
            plan_in, plan_out, plan_tile = refs[7:]
    tq = q_ref.shape[0] // n_seq
    lane = lax.broadcasted_iota(jnp.int32, (tq, 2 * V_DIM), 1)
    f_parts = []
    for sq in range(n_seq):
        rows = slice(sq * tq, (sq + 1) * tq)
        for pair in range(N_HEADS // 2):
            vcols = slice(pair * PAIR_W, (pair + 1) * PAIR_W)
            outs = []
            for h in (2 * pair, 2 * pair + 1):
                hcols = slice(h * HEAD_PAD, (h + 1) * HEAD_PAD)
                qh = q_ref[rows, hcols]
                if with_cache:
                    s = lax.dot_general(qh, k_ref[:, hcols], NT_DIMS, preferred_element_type=F32)
                    sc = lax.dot_general(qh, kc_ref[:, hcols], NT_DIMS, preferred_element_type=F32)
                    top = jnp.maximum(jnp.max(s, axis=-1, keepdims=True), jnp.max(sc, axis=-1, keepdims=True))
                    acc = _bdot(jnp.exp2((s - top).astype(BF16)), vx_ref[:, vcols])
                    acc = acc + _bdot(jnp.exp2((sc - top).astype(BF16)), vcx_ref[:, vcols])
                    outs.append(acc[:, 0:2 * V_DIM] / acc[:, DEN_COL:DEN_COL + 1])
                else:
                    s = lax.dot_general(qh, k_ref[rows, hcols], NT_DIMS, preferred_element_type=F32)
                    p = jnp.exp2(s - jnp.max(s, axis=-1, keepdims=True))
                    den = jnp.sum(p, axis=-1, keepdims=True)
                    outs.append(_bdot(p.astype(BF16), v_ref[rows, pair * DEN_COL:(pair + 1) * DEN_COL]) / den)
            a_ref[rows, pair * 2 * V_DIM:(pair + 1) * 2 * V_DIM] = (
                jnp.where(lane < V_DIM, outs[0], outs[1]).astype(BF16))
        if n_seq > 1:
            f_parts.append(_bdot(f_ref[:, 0:tq], y_ref[0, rows, :]) + _bdot(f_ref[:, tq:], y_ref[1, rows, :]))
    f = jnp.concatenate(f_parts, axis=0) if n_seq > 1 else _bdot(f_ref[...], y_ref[...].reshape(-1, D_C))
    o = _bdot(f.astype(BF16), wo_ref[0:D_C, :]) + _bdot(a_ref[...], wo_ref[D_C:, :])
    x1 = x_ref[...] + mod_ref[2:3, :] * o
    o_ref[...] = x1
    if plan_tile is not None:
        _route_plan(x1, mod_ref[...], plan_in, plan_out, plan_tile)


WHOLE_SEQS = 2


def _odd_whole_kernel(tab_ref, x_ref, mod_ref, g_ref, w_ref, qg_ref, wq_ref, kg_ref, wk_ref, wv_ref, cs_ref, f_ref,
                      wo_ref, modp_ref, dloc_ref, ys_ref, pg_ref, pw_ref, pb_ref, pu_ref,
                      o_ref, ckv_ref, kpe_ref, ph_ref, pd_ref, pr_ref, pc_ref,
                      y_ref, q_ref, k_ref, v_ref, a_ref, x2_ref, runs_ref, sem_ref):
    bi = pl.program_id(0)
    par = lax.rem(bi, 2)
    tm = ROUTE_TM

    def fetch(step, parity):
        for sq in range(WHOLE_SEQS):
            _fetch_runs(tab_ref, ys_ref, runs_ref, sem_ref, step * WHOLE_SEQS + sq, parity * WHOLE_SEQS + sq)

    @pl.when(bi == 0)
    def _():
        fetch(0, 0)

    @pl.when(bi + 1 < pl.num_programs(0))
    def _():
        fetch(bi + 1, 1 - par)

    for sq in range(WHOLE_SEQS):
        rows = slice(sq * tm, (sq + 1) * tm)
        moe = _unsort(tab_ref, ys_ref, runs_ref, sem_ref, dloc_ref[rows, 0:1], bi * WHOLE_SEQS + sq,
                      par * WHOLE_SEQS + sq)
        x2_ref[rows, :] = x_ref[rows, :] + modp_ref[5:6, :] * moe
    _odd_in_kernel(x2_ref, mod_ref, g_ref, w_ref, qg_ref, wq_ref, kg_ref, wk_ref, wv_ref, cs_ref,
                   y_ref, q_ref, k_ref, v_ref, ckv_ref, kpe_ref, rope=False, emit_cache=True)
    _odd_mix_kernel(q_ref, k_ref, v_ref, y_ref, f_ref, x2_ref, mod_ref, wo_ref, o_ref, a_ref,
                    (pg_ref, pw_ref, pb_ref, pu_ref), (ph_ref, pd_ref, pr_ref, pc_ref), bi * WHOLE_SEQS,
                    with_cache=False, n_seq=WHOLE_SEQS)


def _odd_whole(x3, pending, mod, g, w_in, q_g, w_q, kv_g, w_k, w_v, cs, fmat, w_out, plan_consts, stream, layer):
    tab, dloc, ys = pending
    b, seq, _ = x3.shape
    assert seq == ROUTE_TM and b % WHOLE_SEQS == 0 and stream.shared_cond
    b, s = b // WHOLE_SEQS, seq * WHOLE_SEQS
    x3 = x3.reshape(b, s, D_MODEL)
    hq = N_HEADS * HEAD_PAD
    hv = N_HEADS * V_DIM
    const = lambda a: pl.BlockSpec(a.shape, lambda bi, tab_ref: (0,) * a.ndim)
    row_block = lambda w: pl.BlockSpec((None, s, w), lambda bi, tab_ref: (bi, 0, 0))
    mod_block = lambda lyr: pl.BlockSpec((None, None, 6, D_MODEL), lambda bi, tab_ref: (lyr, 0, 0, 0))
    plan_shapes, plan_specs = _plan_out(b, s, s, lambda bi, tab_ref: (bi, 0))
    grid_spec = pltpu.PrefetchScalarGridSpec(
        num_scalar_prefetch=1,
        grid=(b,),
        in_specs=[
            row_block(D_MODEL), mod_block(layer),
            const(g), const(w_in), const(q_g), const(w_q), const(kv_g), const(w_k), const(w_v), const(cs),
            const(fmat), const(w_out),
            mod_block(layer - 1),
            pl.BlockSpec((s, LANES), lambda bi, tab_ref: (bi, 0)),
            pl.BlockSpec(memory_space=pl.ANY),
        ] + [const(a) for a in plan_consts],
        out_specs=[row_block(D_MODEL), row_block(KV_LORA), row_block(QK_ROPE)] + plan_specs,
        scratch_shapes=[
            pltpu.VMEM((2, s, D_C), BF16),
            pltpu.VMEM((s, hq), BF16),
            pltpu.VMEM((s, hq), BF16),
            pltpu.VMEM((s, hv), BF16),
            pltpu.VMEM((s, hv), BF16),
            pltpu.VMEM((s, D_MODEL), F32),
            pltpu.VMEM((2 * WHOLE_SEQS, RUN_ROWS, D_MODEL), F32),
            pltpu.SemaphoreType.DMA((2 * WHOLE_SEQS,)),
        ],
    )
    out = pl.pallas_call(
        _odd_whole_kernel,
        grid_spec=grid_spec,
        out_shape=[
            jax.ShapeDtypeStruct((b, s, D_MODEL), F32),
            jax.ShapeDtypeStruct((b, s, KV_LORA), F32),
            jax.ShapeDtypeStruct((b, s, QK_ROPE), F32),
        ] + plan_shapes,
        compiler_params=_cparams("arbitrary"),
        name="odd_whole",
    )(tab, x3, mod, g, w_in, q_g, w_q, kv_g, w_k, w_v, cs, fmat, w_out, mod, dloc, ys, *plan_consts)
    shape = lambda w: (b * WHOLE_SEQS, seq, w)
    return out[0].reshape(shape(D_MODEL)), out[1].reshape(shape(KV_LORA)), out[2].reshape(shape(QK_ROPE)), out[3:]


def _odd_mix(q, k, v, kc, vc, y, fmat, x3, mod, w_out, plan_consts, stream, layer):
    b, s, hq = q.shape
    tq = min(s, 256)
    n_i = s // tq
    with_cache = kc is not None
    hv = N_HEADS * V_DIM
    mode = dict(pipeline_mode=pl.Buffered(1)) if with_cache else {}

    def per_batch(rows, cols):
        return pl.BlockSpec((None, rows, cols), lambda bi, i: (bi, 0, 0), **mode)

    in_specs = [pl.BlockSpec((None, tq, hq), lambda bi, i: (bi, i, 0)), per_batch(s, hq), per_batch(s, hv)]
    args = [q, k, v]
    if with_cache:
        p = kc.shape[1]
        in_specs += [per_batch(p, hq), per_batch(p, hv)]
        args += [kc, vc]
    in_specs += [
        per_batch(2 * s, D_C),
        pl.BlockSpec((tq, 2 * s), lambda bi, i: (i, 0)),
        pl.BlockSpec((None, tq, D_MODEL), lambda bi, i: (bi, i, 0)),
        _mod_spec(stream, layer),
        pl.BlockSpec(w_out.shape, lambda bi, i: (0, 0), **mode),
    ]
    in_specs += [pl.BlockSpec(a.shape, lambda bi, i: (0,) * a.ndim) for a in plan_consts]
    args += [y, fmat, x3, mod, w_out] + plan_consts
    plan_shapes, plan_specs = _plan_out(b, s, tq, lambda bi, i: (bi, i))
    out = pl.pallas_call(
        functools.partial(_odd_mix_kernel, with_cache=with_cache),
        grid=(b, n_i),
        in_specs=in_specs,
        out_specs=[pl.BlockSpec((None, tq, D_MODEL), lambda bi, i: (bi, i, 0))] + plan_specs,
        out_shape=[jax.ShapeDtypeStruct((b, s, D_MODEL), F32)] + plan_shapes,
        scratch_shapes=[pltpu.VMEM((tq, hv), BF16)] + (
            [pltpu.VMEM((s, VX_W), BF16), pltpu.VMEM((kc.shape[1], VX_W), BF16)] if with_cache else []),
        compiler_params=_cparams("arbitrary", "arbitrary"),
        name="odd_mix",
    )(*args)
    return out[0], out[1:]


def _dft_tables(seq):
    jc = np.arange(C_GW)
    ang_c = 2.0 * np.pi * np.outer(jc, jc) / C_GW
    eye = np.eye(C_GROUPS)
    cs = np.concatenate([np.kron(eye, np.cos(ang_c)), np.kron(eye, np.sin(ang_c))], axis=1)
    jn = np.arange(seq)
    ang_n = 2.0 * np.pi * (np.outer(jn, jn) % seq) / seq
    scale = 1.0 / math.sqrt(seq * C_GW)
    fmat = np.concatenate([np.cos(ang_n), -np.sin(ang_n)], axis=1) * scale
    return jnp.asarray(cs, F32).astype(BF16), jnp.asarray(fmat, F32).astype(BF16)


def _odd_weights(w_in, w_uq, w_ukv):
    d = w_in.shape[0]
    base = D_C + Q_LORA + KV_LORA
    kpe_blk = jnp.zeros((d, HEAD_PAD), w_in.dtype).at[:, ROPE_OFF:ROPE_OFF + QK_ROPE].set(w_in[:, base:])
    w_in_p = jnp.concatenate([w_in[:, :base], kpe_blk], axis=1).astype(BF16)
    qh = w_uq.reshape(Q_LORA, N_HEADS, QK_NOPE + QK_ROPE)
    w_q = jnp.pad(qh, ((0, 0), (0, 0), (0, HEAD_PAD - QK_NOPE - QK_ROPE))).reshape(Q_LORA, -1).astype(BF16)
    kvh = w_ukv.reshape(KV_LORA, N_HEADS, QK_NOPE + V_DIM)
    w_k = jnp.pad(kvh[:, :, :QK_NOPE], ((0, 0), (0, 0), (0, HEAD_PAD - QK_NOPE))).reshape(KV_LORA, -1)
    w_v = kvh[:, :, QK_NOPE:].reshape(KV_LORA, -1)
    return w_in_p, w_q, w_k.astype(BF16), w_v.astype(BF16)


ROUTER_ROWS = 32


def _router_weights(wg, bg, we, be):
    d = wg.shape[0]
    w = jnp.concatenate([wg, we.reshape(d, N_EXPERTS)], axis=1).T
    w = jnp.pad(w, ((0, ROUTER_ROWS - w.shape[0]), (0, 0))).astype(BF16)
    b = jnp.concatenate([bg, be.reshape(N_EXPERTS)])
    b = jnp.pad(b, (0, ROUTER_ROWS - b.shape[0])).reshape(ROUTER_ROWS, 1).astype(F32)
    return w, b


def kernel(x_prompt, x_sample, cache_ckv, cache_kpe, c, c_ctx, mod_w, mod_b, norm1_g, norm2_g,
           ev_w_in, ev_conv_w, ev_sgu_norm_g, ev_sgu_w, ev_sgu_b, ev_w_out,
           od_w_in, od_q_norm_g, od_w_uq, od_kv_norm_g, od_w_ukv, od_w_out,
           moe_wg, moe_bg, moe_we, moe_be, moe_w1, moe_w3, moe_w2, final_norm_g):
    bp, n_p, d = x_prompt.shape
    bs, n_s, _ = x_sample.shape
    streams = [(_Stream(bp, n_p, True), x_prompt), (_Stream(bs, n_s, False), x_sample)]

    n_rows = 1 + bs
    cond_t = jnp.concatenate([c_ctx[None, :], c], axis=0).T
    mod = _adaln(cond_t, mod_w, mod_b, n_rows)

    final_g = final_norm_g.reshape(1, d)
    xs = [x for _, x in streams]
    new_ckv, new_kpe = [], []
    pending = None
    plans = [None, None]
    for l in range(DEPTH):
        j = l // 2
        g1 = norm1_g[l].reshape(1, d)
        g2 = norm2_g[l].reshape(1, d)
        plan_consts = _plan_consts(g2, *_router_weights(moe_wg[l], moe_bg[l], moe_we[l], moe_be[l]))
        last = l == DEPTH - 1
        if l % 2 == 0:
            w_in = ev_w_in[j].astype(BF16)
            w_out = ev_w_out[j].astype(BF16)
            sgu_w = ev_sgu_w[j].astype(BF16)
            sgu_g = ev_sgu_norm_g[j].reshape(1, D_B)
            sgu_bias = jnp.repeat(ev_sgu_b[j].T, D_B // B_GROUPS, axis=1)
            for si, (st, _) in enumerate(streams):
                xs[si], plans[si] = _even_layer(xs[si], mod, g1, w_in, ev_conv_w[j], sgu_g, sgu_w, sgu_bias, w_out,
                                                plan_consts, st, l)
        else:
            w_in, w_q, w_k, w_v = _odd_weights(od_w_in[j], od_w_uq[j], od_w_ukv[j])
            w_out = od_w_out[j].astype(BF16)
            q_g = od_q_norm_g[j].reshape(1, Q_LORA)
            kv_g = od_kv_norm_g[j].reshape(1, KV_LORA)
            for si, (st, _) in enumerate(streams):
                x3 = xs[si]
                cs, fmat = _dft_tables(st.seq)
                if st.shared_cond:
                    xs[si], ckv, kpe, plans[si] = _odd_whole(x3, pending, mod, g1, w_in, q_g, w_q, kv_g, w_k, w_v, cs,
                                                             fmat, w_out, plan_consts, st, l)
                    new_ckv.append(ckv)
                    new_kpe.append(kpe)
                    continue
                y, q, k, v = _odd_in(x3, mod, g1, w_in, q_g, w_q, kv_g, w_k, w_v, cs, st, l,
                                     _rope_tables(st.seq), False)
                kpe_blk = jnp.pad(cache_kpe[:, j], ((0, 0), (0, 0), (ROPE_OFF, HEAD_PAD - ROPE_OFF - QK_ROPE)))
                kc, vc = _cache_kv(cache_ckv[:, j], kpe_blk, w_k, w_v)
                xs[si], plans[si] = _odd_mix(q, k, v, kc, vc, y.reshape(st.batch, 2 * st.seq, D_C), fmat, x3, mod,
                                             w_out, plan_consts, st, l)
        defer = not last and (l + 1) % 2 == 1
        x2p, x2s, pending = _moe(xs[0].reshape(bp * n_p, d), xs[1].reshape(bs * n_s, d), plans[0], plans[1], mod,
                                 moe_w1, moe_w3, moe_w2, final_g, l, n_s, last, defer)
        xs = [xs[0] if defer else x2p.reshape(bp, n_p, d), x2s.reshape(bs, n_s, d)]
    return (xs[0], xs[1], jnp.stack(new_ckv, axis=1), jnp.stack(new_kpe, axis=1))
```

```python
import functools
import math

import numpy as np
import jax
import jax.numpy as jnp
from jax import lax
from jax.experimental import pallas as pl
from jax.experimental.pallas import tpu as pltpu

D_MODEL = 1024
DEPTH = 2
GRID_W = 64
D_A = D_MODEL // 2
D_B = D_MODEL // 2
B_GROUPS = 4
CHUNK = 128
D_EVEN_IN = 3 * D_A + 2 * D_B
D_C = D_MODEL // 4
C_GROUPS = 4
C_GW = D_C // C_GROUPS
N_HEADS = 12
QK_NOPE = 64
QK_ROPE = 32
V_DIM = 64
Q_LORA = 384
KV_LORA = 256
ROPE_BASE = 10000.0
N_GROUPS_MOE = 4
EXPERTS_PER_GROUP = 4
N_EXPERTS = N_GROUPS_MOE * EXPERTS_PER_GROUP
D_EXPERT = 256
EPS = 1e-6

LANES = 128
HEAD_PAD = 128
PAIR_W = 256
DEN_COL = 2 * 64
VX_W = 6 * PAIR_W
ROPE_OFF = QK_NOPE
GATE_OFF = N_GROUPS_MOE
NEG_BIG = -1e30
F32 = jnp.float32
BF16 = jnp.bfloat16
VMEM_LIMIT = 56 * 1024 * 1024


def _cparams(*sem):
    return pltpu.CompilerParams(dimension_semantics=sem, vmem_limit_bytes=VMEM_LIMIT)


def _rms(x, g):
    return x * lax.rsqrt(jnp.mean(x * x, axis=-1, keepdims=True) + EPS) * g


def _norm_mod(x, g, scale, shift):
    rs = lax.rsqrt(jnp.mean(x * x, axis=-1, keepdims=True) + EPS)
    return x * rs * (g * (1.0 + scale)) + shift


def _bdot(a, b):
    return jnp.dot(a, b, preferred_element_type=F32)


NT_DIMS = (((1,), (1,)), ((), ()))
TN_DIMS = (((0,), (0,)), ((), ()))


MOD_ROWS = 8


def _mod_kernel(c_ref, w_ref, b_ref, o_ref):
    c = c_ref[...]
    s = c * jax.nn.sigmoid(c)
    s_hi = s.astype(BF16).astype(F32)
    pair = jnp.concatenate([s_hi, s - s_hi], axis=0).astype(BF16)
    two = _bdot(pair, w_ref[...].astype(BF16))
    o_ref[...] = two[0:MOD_ROWS] + two[MOD_ROWS:] + b_ref[...]


def _adaln(cond, mod_w, mod_b, n_rows):
    nt = 2048
    d6 = mod_w.shape[-1]
    cond = jnp.pad(cond, ((0, MOD_ROWS - n_rows), (0, 0)))
    out = pl.pallas_call(
        _mod_kernel,
        grid=(DEPTH, d6 // nt),
        in_specs=[
            pl.BlockSpec(cond.shape, lambda l, n: (0, 0)),
            pl.BlockSpec((None, D_MODEL, nt), lambda l, n: (l, 0, n)),
            pl.BlockSpec((None, 1, nt), lambda l, n: (l, 0, n)),
        ],
        out_specs=pl.BlockSpec((None, MOD_ROWS, nt), lambda l, n: (l, 0, n)),
        out_shape=jax.ShapeDtypeStruct((DEPTH, MOD_ROWS, d6), F32),
        compiler_params=_cparams("parallel", "parallel"),
        name="adaln",
    )(cond, mod_w, mod_b.reshape(DEPTH, 1, d6))
    return out[:, :n_rows].reshape(DEPTH, n_rows, 6, D_MODEL)


class _Stream:
    def __init__(self, batch, seq, shared_cond):
        self.batch = batch
        self.seq = seq
        self.tokens = batch * seq
        self.shared_cond = shared_cond

    def row_of_batch(self, b):
        return 0 if self.shared_cond else b + 1

    def row_of_tile(self, i, tm):
        return 0 if self.shared_cond else (i * tm) // self.seq + 1


def _mod_spec(stream, layer, tm=None):
    if tm is None:
        return pl.BlockSpec((None, None, 6, D_MODEL), lambda b, i: (layer, stream.row_of_batch(b), 0, 0))
    return pl.BlockSpec((None, None, 6, D_MODEL), lambda i, *_: (layer, stream.row_of_tile(i, tm), 0, 0))


HALO = 8


def _even_kernel(*refs, has_halo, seq_rows, nc):
    if has_halo:
        x_ref, xp_ref, xn_ref = refs[:3]
        refs = refs[3:]
    else:
        x_ref = refs[0]
        refs = refs[1:]
    mod_ref, g_ref, wi_ref, cw_ref, sg_ref, sw_ref, sb_ref, wo_ref = refs[:8]
    plan_in, (o_ref, *plan_out), (z_ref, y_ref) = refs[8:12], refs[12:17], refs[17:]
    i = pl.program_id(1)
    n_i = pl.num_programs(1)
    ts = x_ref.shape[0]
    m = mod_ref[...]
    g = g_ref[...]

    def modulate(x):
        return _norm_mod(x, g, m[1:2], m[0:1]).astype(BF16)

    x = x_ref[...]
    hb = modulate(x)
    for n in range(D_EVEN_IN // nc):
        z_ref[:, n * nc:(n + 1) * nc] = _bdot(hb, wi_ref[:, n * nc:(n + 1) * nc]).astype(BF16)

    gate_b = z_ref[:, 0:D_A].astype(F32)
    gate_c = z_ref[:, D_A:2 * D_A].astype(F32)
    xa = z_ref[:, 2 * D_A:3 * D_A].astype(F32)
    t = gate_c * xa
    t_prev = pltpu.roll(t, 1, axis=0)
    t_next = pltpu.roll(t, ts - 1, axis=0)
    row = lax.broadcasted_iota(jnp.int32, (ts, 1), 0) & (seq_rows - 1)
    if has_halo:
        hh = modulate(jnp.concatenate([xp_ref[...], xn_ref[...]], axis=0))
        zh = _bdot(hh, wi_ref[:, D_A:3 * D_A]).astype(BF16).astype(F32)
        th = zh[:, 0:D_A] * zh[:, D_A:2 * D_A]
        tp = th[HALO - 1:HALO] * (i > 0).astype(F32)
        tn = th[HALO:HALO + 1] * (i < n_i - 1).astype(F32)
    else:
        tp = tn = 0.0
    t_prev = jnp.where(row == 0, tp, t_prev)
    t_next = jnp.where(row == seq_rows - 1, tn, t_next)
    cw = cw_ref[...]
    y_a = gate_b * (t_prev * cw[0:1] + t * cw[1:2] + t_next * cw[2:3])
    y_ref[:, 0:D_A] = y_a.astype(BF16)

    u = z_ref[:, 3 * D_A:3 * D_A + D_B].astype(F32)
    v = z_ref[:, 3 * D_A + D_B:3 * D_A + 2 * D_B].astype(F32)
    vb = _rms(v, sg_ref[...]).astype(BF16)
    gw = D_B // B_GROUPS
    for c in range(ts // CHUNK):
        rows = slice(c * CHUNK, (c + 1) * CHUNK)
        for gi in range(B_GROUPS):
            cols = slice(gi * gw, (gi + 1) * gw)
            sv = _bdot(sw_ref[gi], vb[rows, cols]) + sb_ref[:, cols]
            y_ref[rows, D_A + gi * gw:D_A + (gi + 1) * gw] = (u[rows, cols] * sv).astype(BF16)

    x1 = x + m[2:3] * _bdot(y_ref[...], wo_ref[...])
    o_ref[...] = x1
    _route_plan(x1, m, plan_in, plan_out, (pl.program_id(0) * n_i + i) * (ts // ROUTE_TM))


def _even_layer(x3, mod, g, w_in, conv_w, sgu_g, sgu_w, sgu_bias, w_out, plan_consts, stream, layer):
    out_shape = x3.shape
    b, s, _ = x3.shape
    ts = min(s, 256) if stream.shared_cond else min(s, 512)
    n_i = s // ts
    has_halo = n_i > 1
    seq_rows = ts
    if not has_halo and stream.shared_cond and b % 2 == 0:
        b, s, ts = b // 2, 2 * s, 2 * ts
        x3 = x3.reshape(b, s, D_MODEL)
    hb = ts // HALO
    last_h = s // HALO - 1
    const = lambda a: pl.BlockSpec(a.shape, lambda bi, i: (0,) * a.ndim)
    in_specs = [pl.BlockSpec((None, ts, D_MODEL), lambda bi, i: (bi, i, 0))]
    args = [x3]
    if has_halo:
        in_specs += [
            pl.BlockSpec((None, HALO, D_MODEL), lambda bi, i: (bi, jnp.maximum(i * hb - 1, 0), 0)),
            pl.BlockSpec((None, HALO, D_MODEL), lambda bi, i: (bi, jnp.minimum((i + 1) * hb, last_h), 0)),
        ]
        args += [x3, x3]
    in_specs += [_mod_spec(stream, layer), const(g), const(w_in), const(conv_w), const(sgu_g), const(sgu_w),
                 const(sgu_bias), const(w_out)] + [const(a) for a in plan_consts]
    args += [mod, g, w_in, conv_w, sgu_g, sgu_w, sgu_bias, w_out] + plan_consts
    plan_shapes, plan_specs = _plan_out(b, s, ts, lambda bi, i: (bi, i))
    out = pl.pallas_call(
        functools.partial(_even_kernel, has_halo=has_halo, seq_rows=seq_rows, nc=512),
        grid=(b, n_i),
        in_specs=in_specs,
        out_specs=[pl.BlockSpec((None, ts, D_MODEL), lambda bi, i: (bi, i, 0))] + plan_specs,
        out_shape=[jax.ShapeDtypeStruct((b, s, D_MODEL), F32)] + plan_shapes,
        scratch_shapes=[pltpu.VMEM((ts, D_EVEN_IN), BF16), pltpu.VMEM((ts, D_A + D_B), BF16)],
        compiler_params=_cparams("arbitrary", "arbitrary"),
        name="even_layer",
    )(*args)
    return out[0].reshape(out_shape), out[1:]


ROUTE_TM = 256
ROUTE_SUBS = 4
ROUTE_STEP = ROUTE_TM * ROUTE_SUBS
ROUTE_PAD = 8
SORT_ROWS = ROUTE_TM + LANES
RUN_ROWS = SORT_ROWS + 32
ROUTE_ROWS = ROUTE_TM + 4 * ROUTE_PAD
XS_W = D_MODEL + LANES
GATE_LO = EXPERTS_PER_GROUP
DLOC_HI = 2 * EXPERTS_PER_GROUP
DLOC_RADIX = 16.0
FFN_BM = 512
FFN_HALF = FFN_BM // 2
RUN_SIZES = (256, 128, 64, 32, 16, 8)
TAB_W = 2 * N_GROUPS_MOE


def _round_up(x, m):
    return lax.div(x + (m - 1), m) * m


def _run_copies(tab_ref, tile, hbm_ref, vmem_ref, sem, to_hbm, wait):
    off = 0
    for g in range(N_GROUPS_MOE):
        start = tab_ref[tile * TAB_W + g]
        n = tab_ref[tile * TAB_W + N_GROUPS_MOE + g]
        for p in RUN_SIZES:
            done = n & (-2 * p)

            @pl.when((n & p) != 0)
            def _():
                v = vmem_ref.at[pl.ds(pl.multiple_of(off + done, ROUTE_PAD), p)]
                h = hbm_ref.at[pl.ds(pl.multiple_of(start + done, ROUTE_PAD), p)]
                cp = pltpu.make_async_copy(v, h, sem) if to_hbm else pltpu.make_async_copy(h, v, sem)
                if wait:
                    cp.wait()
                else:
                    cp.start()
        off = off + n


def _zero_fill(tab_ref, meta, zeros_ref, hbm_ref, sem, n_rows, wait):
    def copy(rows, dst_row):
        cp = pltpu.make_async_copy(zeros_ref.at[pl.ds(0, rows)],
                                   hbm_ref.at[pl.ds(pl.multiple_of(dst_row, ROUTE_PAD), rows)], sem)
        if wait:
            cp.wait()
        else:
            cp.start()

    end = 0
    for g in range(N_GROUPS_MOE):
        fill = tab_ref[meta + g]
        start = tab_ref[meta + N_GROUPS_MOE + g]
        end = start + _round_up(fill, FFN_BM)
        tail = end - start - fill
        for p in RUN_SIZES:
            pl.when((tail & p) != 0)(functools.partial(copy, p, start + fill + (tail & (-2 * p))))
    for k in range(n_rows // FFN_BM):
        pl.when(end + k * FFN_BM < n_rows)(functools.partial(copy, FFN_BM, end + k * FFN_BM))


def _wait_rows(n, hbm_ref, vmem_ref, sem, to_hbm):
    for p in RUN_SIZES:
        @pl.when((n & p) != 0)
        def _():
            v = vmem_ref.at[pl.ds(0, p)]
            h = hbm_ref.at[pl.ds(0, p)]
            (pltpu.make_async_copy(v, h, sem) if to_hbm else pltpu.make_async_copy(h, v, sem)).wait()


def _tile_rows(tab_ref, tile):
    n = 0
    for g in range(N_GROUPS_MOE):
        n = n + tab_ref[tile * TAB_W + N_GROUPS_MOE + g]
    return n


def _max4(v):
    return jnp.maximum(jnp.maximum(v[0], v[1]), jnp.maximum(v[2], v[3]))


def _first_of4(v, top):
    return jnp.where(v[0] == top, 0.0, jnp.where(v[1] == top, 1.0, jnp.where(v[2] == top, 2.0, 3.0)))


def _route_plan(x, m, plan_in, plan_out, tile0):
    g_ref, wrt_ref, brt_ref, upper_ref = plan_in
    haug_ref, dloc_ref, drow_ref, cnt_ref = plan_out
    rows = x.shape[0]
    tm = ROUTE_TM
    ng = N_GROUPS_MOE
    h = _norm_mod(x, g_ref[...], m[4:5], m[3:4])
    hb = h.astype(BF16)
    lt = lax.dot_general(wrt_ref[...], hb, NT_DIMS, preferred_element_type=F32) + brt_ref[...]
    gl = [lt[r:r + 1, :] for r in range(ng)]
    g_top = _max4(gl)
    g_idx = _first_of4(gl, g_top)
    g_w = 1.0 / (jnp.exp(gl[0] - g_top) + jnp.exp(gl[1] - g_top) + jnp.exp(gl[2] - g_top) + jnp.exp(gl[3] - g_top))
    ev = []
    for k in range(EXPERTS_PER_GROUP):
        cand = [lt[GATE_OFF + EXPERTS_PER_GROUP * r + k:GATE_OFF + EXPERTS_PER_GROUP * r + k + 1, :]
                for r in range(ng)]
        ev.append(jnp.where(g_idx == 0.0, cand[0], jnp.where(g_idx == 1.0, cand[1],
                            jnp.where(g_idx == 2.0, cand[2], cand[3]))))
    v1 = _max4(ev)
    i1 = _first_of4(ev, v1)
    rest = [jnp.where(i1 == float(k), NEG_BIG, ev[k]) for k in range(EXPERTS_PER_GROUP)]
    v2 = _max4(rest)
    i2 = _first_of4(rest, v2)
    e2 = jnp.exp(v2 - v1)
    w1 = 1.0 / (1.0 + e2)
    w2 = e2 * w1
    gates = [g_w * (jnp.where(i1 == float(k), w1, 0.0) + jnp.where(i2 == float(k), w2, 0.0))
             for k in range(EXPERTS_PER_GROUP)]

    sub8 = lax.broadcasted_iota(jnp.int32, (8, tm), 0).astype(F32)
    dlocs = []
    for sub in range(rows // tm):
        gi = g_idx[:, sub * tm:(sub + 1) * tm]
        hot = jnp.where(sub8 == gi, 1.0, 0.0)
        before = _bdot(hot.astype(BF16), upper_ref[...])
        dl = jnp.sum(before * hot, axis=0, keepdims=True)
        off = 0
        for g in range(ng):
            n_g = _round_up(jnp.sum(hot[g:g + 1, :]).astype(jnp.int32), ROUTE_PAD)
            cnt_ref[(tile0 + sub) * ng + g] = n_g
            dl = dl + jnp.where(gi == float(g), off.astype(F32) if g else 0.0, 0.0)
            off = off + n_g
        drow_ref[sub] = jnp.broadcast_to(dl, (8, tm))
        dlocs.append(dl)
    dloc = jnp.concatenate(dlocs, axis=1)
    d_hi = jnp.floor(dloc * (1.0 / DLOC_RADIX))
    g_hi = [gt.astype(BF16).astype(F32) for gt in gates]
    ex_rows = g_hi + [gt - gh for gt, gh in zip(gates, g_hi)] + [d_hi, dloc - DLOC_RADIX * d_hi]
    sub16 = lax.broadcasted_iota(jnp.int32, (16, rows), 0)
    ex_t = jnp.zeros((16, rows), F32)
    for r, row in enumerate(ex_rows):
        ex_t = jnp.where(sub16 == r, row, ex_t)
    ex_t = jnp.concatenate([ex_t, jnp.zeros((LANES - 16, rows), F32)], axis=0)
    extras = ex_t.T
    dloc_ref[...] = jnp.broadcast_to(
        DLOC_RADIX * extras[:, DLOC_HI:DLOC_HI + 1] + extras[:, DLOC_HI + 1:DLOC_HI + 2], (rows, LANES))
    haug_ref[...] = jnp.concatenate([hb, extras.astype(BF16)], axis=1)


def _plan_consts(g2, w_rt, b_rt):
    upper = jnp.asarray(np.triu(np.ones((ROUTE_TM, ROUTE_TM), np.float32), 1), BF16)
    return [g2, w_rt, b_rt, upper]


def _plan_out(b, s, rows, index):
    tiles = rows // ROUTE_TM
    shapes = [jax.ShapeDtypeStruct((b, s, XS_W), BF16), jax.ShapeDtypeStruct((b, s, LANES), F32),
              jax.ShapeDtypeStruct((b, s // ROUTE_TM, 8, ROUTE_TM), F32),
              jax.ShapeDtypeStruct((b * (s // ROUTE_TM) * N_GROUPS_MOE,), jnp.int32)]
    specs = [pl.BlockSpec((None, rows, XS_W), lambda *idx: (*index(*idx), 0)),
             pl.BlockSpec((None, rows, LANES), lambda *idx: (*index(*idx), 0)),
             pl.BlockSpec((None, tiles, 8, ROUTE_TM), lambda *idx: (*index(*idx), 0, 0)),
             pl.BlockSpec(memory_space=pltpu.SMEM)]
    return shapes, specs


def _flat_plan(plan):
    haug, dloc, drow, cnt = plan
    return (haug.reshape(-1, XS_W), dloc.reshape(-1, LANES), drow.reshape(-1, 8, ROUTE_TM), cnt)


def _dispatch_kernel(cp_ref, cs_ref, hp_ref, hs_ref, dp_ref, ds_ref, tab_ref, sorted_hbm,
                     sorted_ref, zeros_ref, fill_ref, sem_ref, zsem_ref, *, n_steps, n_p_steps):
    i = pl.program_id(0)
    tm = ROUTE_TM
    ng = N_GROUPS_MOE
    n_tiles = n_steps * ROUTE_SUBS
    n_p_tiles = n_p_steps * ROUTE_SUBS
    meta = n_tiles * TAB_W
    n_rows = sorted_hbm.shape[0]

    def count(tile, g):
        if isinstance(tile, int):
            return cp_ref[tile * ng + g] if tile < n_p_tiles else cs_ref[(tile - n_p_tiles) * ng + g]
        return jnp.where(tile < n_p_tiles, cp_ref[jnp.minimum(tile, n_p_tiles - 1) * ng + g],
                         cs_ref[jnp.maximum(tile - n_p_tiles, 0) * ng + g])

    @pl.when(i == 0)
    def _():
        start = 0
        for g in range(ng):
            fill = sum(count(t, g) for t in range(n_tiles))
            tab_ref[meta + g] = fill
            tab_ref[meta + ng + g] = start
            start = start + _round_up(fill, FFN_BM)
            fill_ref[g] = 0
        zeros_ref[...] = jnp.zeros_like(zeros_ref)

    is_prompt = i < n_p_steps
    row_f = lax.broadcasted_iota(jnp.int32, (ROUTE_ROWS, tm), 0).astype(F32)

    def sort_tile(h_ref, d_ref, sub):
        onehot = jnp.where(row_f == d_ref[sub][0:1, :], 1.0, 0.0).astype(BF16)
        sorted_ref[sub] = _bdot(onehot, h_ref[sub * tm:(sub + 1) * tm, :])

    for sub in range(ROUTE_SUBS):
        tile = i * ROUTE_SUBS + sub

        @pl.when(i >= 1)
        def _():
            _wait_rows(_tile_rows(tab_ref, tile - ROUTE_SUBS), sorted_hbm, sorted_ref.at[sub], sem_ref.at[sub], True)

        pl.when(is_prompt)(functools.partial(sort_tile, hp_ref, dp_ref, sub))
        pl.when(jnp.logical_not(is_prompt))(functools.partial(sort_tile, hs_ref, ds_ref, sub))
        for g in range(ng):
            n_g = count(tile, g)
            tab_ref[tile * TAB_W + g] = tab_ref[meta + ng + g] + fill_ref[g]
            tab_ref[tile * TAB_W + ng + g] = n_g
            fill_ref[g] = fill_ref[g] + n_g
        _run_copies(tab_ref, tile, sorted_hbm, sorted_ref.at[sub], sem_ref.at[sub], to_hbm=True, wait=False)

        @pl.when(i == n_steps - 1)
        def _():
            _wait_rows(_tile_rows(tab_ref, tile), sorted_hbm, sorted_ref.at[sub], sem_ref.at[sub], True)

    @pl.when(i == n_steps - 1)
    def _():
        _zero_fill(tab_ref, meta, zeros_ref, sorted_hbm, zsem_ref, n_rows, wait=False)
        _zero_fill(tab_ref, meta, zeros_ref, sorted_hbm, zsem_ref, n_rows, wait=True)


def _merged_specs(n_p_steps, n_s_steps, layer, sample_seq, step_of, width=D_MODEL):
    def p_map(*idx):
        return (jnp.minimum(step_of(*idx), n_p_steps - 1), 0)

    def s_map(*idx):
        return (jnp.clip(step_of(*idx) - n_p_steps, 0, n_s_steps - 1), 0)

    def mod_map(*idx):
        j = step_of(*idx)
        row = jnp.where(j < n_p_steps, 0, 1 + lax.div(jnp.maximum(j - n_p_steps, 0) * ROUTE_STEP, sample_seq))
        return (layer, row, 0, 0)

    return (pl.BlockSpec((ROUTE_STEP, width), p_map), pl.BlockSpec((ROUTE_STEP, width), s_map),
            pl.BlockSpec((None, None, 6, D_MODEL), mod_map))


def _dispatch(plan_p, plan_s, n_rows):
    haug_p, _, drow_p, cnt_p = plan_p
    haug_s, _, drow_s, cnt_s = plan_s
    n_p_steps = haug_p.shape[0] // ROUTE_STEP
    n_s_steps = haug_s.shape[0] // ROUTE_STEP
    n_steps = n_p_steps + n_s_steps
    n_tiles = n_steps * ROUTE_SUBS
    step_of = lambda i, cp, cs: i
    hp_spec, hs_spec, _ = _merged_specs(n_p_steps, n_s_steps, 0, 1, step_of, XS_W)
    drow_block = (ROUTE_SUBS, 8, ROUTE_TM)
    grid_spec = pltpu.PrefetchScalarGridSpec(
        num_scalar_prefetch=2,
        grid=(n_steps,),
        in_specs=[
            hp_spec, hs_spec,
            pl.BlockSpec(drow_block, lambda i, cp, cs: (jnp.minimum(i, n_p_steps - 1), 0, 0)),
            pl.BlockSpec(drow_block, lambda i, cp, cs: (jnp.clip(i - n_p_steps, 0, n_s_steps - 1), 0, 0)),
        ],
        out_specs=[pl.BlockSpec(memory_space=pltpu.SMEM), pl.BlockSpec(memory_space=pl.ANY)],
        scratch_shapes=[
            pltpu.VMEM((ROUTE_SUBS, ROUTE_ROWS, XS_W), F32),
            pltpu.VMEM((FFN_BM, XS_W), F32),
            pltpu.SMEM((N_GROUPS_MOE,), jnp.int32),
            pltpu.SemaphoreType.DMA((ROUTE_SUBS,)),
            pltpu.SemaphoreType.DMA(()),
        ],
    )
    return pl.pallas_call(
        functools.partial(_dispatch_kernel, n_steps=n_steps, n_p_steps=n_p_steps),
        grid_spec=grid_spec,
        out_shape=[
            jax.ShapeDtypeStruct(((n_tiles + 1) * TAB_W,), jnp.int32),
            jax.ShapeDtypeStruct((n_rows, XS_W), F32),
        ],
        compiler_params=_cparams("arbitrary"),
        name="moe_dispatch",
    )(cnt_p, cnt_s, haug_p, haug_s, drow_p, drow_s)


def _ffn_lookup(i, tab_ref, meta):
    fills = [tab_ref[meta + g] for g in range(N_GROUPS_MOE)]
    edges = []
    acc = 0
    for f in fills:
        acc = acc + lax.div(f + (FFN_BM - 1), FFN_BM)
        edges.append(acc)
    total = edges[-1]
    ii = jnp.minimum(i, total - 1)
    grp = sum((ii >= e).astype(jnp.int32) for e in edges[:-1])

    def pick(vals):
        return jnp.where(grp == 0, vals[0], jnp.where(grp == 1, vals[1], jnp.where(grp == 2, vals[2], vals[3])))

    first = pick([0] + edges[:-1])
    return grp, total, ii == first, pick(fills) - (ii - first) * FFN_BM


def _ffn_group_kernel(tab_ref, xs_ref, w1_ref, w3_ref, w2_ref, ys_ref, w1b_ref, w3b_ref, w2b_ref, *, meta):
    i = pl.program_id(0)
    _, total, first_of_group, valid = _ffn_lookup(i, tab_ref, meta)
    active = i < total

    @pl.when(jnp.logical_and(active, first_of_group))
    def _():
        w1b_ref[...] = w1_ref[...].astype(BF16)
        w3b_ref[...] = w3_ref[...].astype(BF16)
        w2b_ref[...] = w2_ref[...].astype(BF16)

    def run(rows):
        hb = xs_ref[0:rows, 0:D_MODEL].astype(BF16)
        ex = xs_ref[0:rows, D_MODEL:XS_W]
        hid = []
        for e in range(EXPERTS_PER_GROUP):
            a = _bdot(hb, w1b_ref[e])
            b = _bdot(hb, w3b_ref[e])
            gate = ex[:, e:e + 1] + ex[:, GATE_LO + e:GATE_LO + e + 1]
            hid.append(((a * jax.nn.sigmoid(a)) * b * gate).astype(BF16))
        ys_ref[0:rows, :] = _bdot(jnp.concatenate(hid, axis=1), w2b_ref[...].reshape(-1, D_MODEL))

    @pl.when(jnp.logical_and(active, valid > FFN_HALF))
    def _():
        run(FFN_BM)

    @pl.when(jnp.logical_and(active, valid <= FFN_HALF))
    def _():
        run(FFN_HALF)
        ys_ref[FFN_HALF:, :] = jnp.zeros((FFN_BM - FFN_HALF, D_MODEL), F32)

    @pl.when(jnp.logical_not(active))
    def _():
        ys_ref[...] = jnp.zeros_like(ys_ref)


def _ffn_group(tab, xs, w1, w3, w2, layer, meta):
    e4 = EXPERTS_PER_GROUP
    n_blocks = xs.shape[0] // FFN_BM
    group_of = lambda i, tab_ref: _ffn_lookup(i, tab_ref, meta)[0]
    grid_spec = pltpu.PrefetchScalarGridSpec(
        num_scalar_prefetch=1,
        grid=(n_blocks,),
        in_specs=[
            pl.BlockSpec((FFN_BM, XS_W), lambda i, tab_ref: (i, 0)),
            pl.BlockSpec((None, e4, D_MODEL, D_EXPERT), lambda i, tab_ref: (layer, group_of(i, tab_ref), 0, 0)),
            pl.BlockSpec((None, e4, D_MODEL, D_EXPERT), lambda i, tab_ref: (layer, group_of(i, tab_ref), 0, 0)),
            pl.BlockSpec((None, e4, D_EXPERT, D_MODEL), lambda i, tab_ref: (layer, group_of(i, tab_ref), 0, 0)),
        ],
        out_specs=pl.BlockSpec((FFN_BM, D_MODEL), lambda i, tab_ref: (i, 0)),
        scratch_shapes=[
            pltpu.VMEM((e4, D_MODEL, D_EXPERT), BF16),
            pltpu.VMEM((e4, D_MODEL, D_EXPERT), BF16),
            pltpu.VMEM((e4, D_EXPERT, D_MODEL), BF16),
        ],
    )
    return pl.pallas_call(
        functools.partial(_ffn_group_kernel, meta=meta),
        grid_spec=grid_spec,
        out_shape=jax.ShapeDtypeStruct((xs.shape[0], D_MODEL), F32),
        compiler_params=_cparams("arbitrary"),
        name="moe_ffn",
    )(tab, xs, w1, w3, w2)


def _fetch_runs(tab_ref, ys_ref, runs_ref, sem_ref, tile, slot):
    _run_copies(tab_ref, tile, ys_ref, runs_ref.at[slot], sem_ref.at[slot], to_hbm=False, wait=False)


def _unsort(tab_ref, ys_ref, runs_ref, sem_ref, dloc_col, tile, slot):
    covered = _tile_rows(tab_ref, tile)
    _wait_rows(covered, ys_ref, runs_ref.at[slot], sem_ref.at[slot], False)
    runs_ref[slot, pl.ds(pl.multiple_of(covered, ROUTE_PAD), LANES), :] = jnp.zeros((LANES, D_MODEL), F32)
    yb = runs_ref[slot, 0:SORT_ROWS, :].astype(BF16)
    row_f = lax.broadcasted_iota(jnp.int32, (ROUTE_TM, SORT_ROWS), 1).astype(F32)
    return _bdot(jnp.where(row_f == dloc_col, 1.0, 0.0).astype(BF16), yb)


def _combine_kernel(tab_ref, *refs, n_steps, first_step, n_p_steps, final_norm):
    with_prompt = first_step < n_p_steps
    if with_prompt:
        xp_ref, xs_ref, mod_ref, dp_ref, ds_ref, fg_ref, ys_ref, op_ref, os_ref, runs_ref, sem_ref = refs
    else:
        xs_ref, mod_ref, ds_ref, fg_ref, ys_ref, os_ref, runs_ref, sem_ref = refs
    i = pl.program_id(0)
    par = lax.rem(i, 2)
    step = first_step + i
    tm = ROUTE_TM

    def fetch(st, parity):
        for sub in range(ROUTE_SUBS):
            _fetch_runs(tab_ref, ys_ref, runs_ref, sem_ref, st * ROUTE_SUBS + sub, parity * ROUTE_SUBS + sub)

    @pl.when(i == 0)
    def _():
        fetch(first_step, 0)

    @pl.when(i + 1 < n_steps)
    def _():
        fetch(step + 1, 1 - par)

    is_prompt = step < n_p_steps
    dloc = jnp.where(is_prompt, dp_ref[:, 0:1], ds_ref[:, 0:1]) if with_prompt else ds_ref[:, 0:1]
    parts = [_unsort(tab_ref, ys_ref, runs_ref, sem_ref, dloc[sub * tm:(sub + 1) * tm],
                     step * ROUTE_SUBS + sub, par * ROUTE_SUBS + sub) for sub in range(ROUTE_SUBS)]
    delta = mod_ref[5:6, :] * jnp.concatenate(parts, axis=0)

    def finish(x_ref, o_ref):
        x2 = x_ref[...] + delta
        o_ref[...] = _rms(x2, fg_ref[...]) if final_norm else x2

    if with_prompt:
        pl.when(is_prompt)(functools.partial(finish, xp_ref, op_ref))
        pl.when(jnp.logical_not(is_prompt))(functools.partial(finish, xs_ref, os_ref))
    else:
        finish(xs_ref, os_ref)


def _combine(tab, xp, xs, mod, dloc_p, dloc_s, final_g, ys, layer, sample_seq, final_norm, with_prompt):
    n_p_steps = xp.shape[0] // ROUTE_STEP
    n_s_steps = xs.shape[0] // ROUTE_STEP
    first_step = 0 if with_prompt else n_p_steps
    n_steps = n_p_steps + n_s_steps - first_step
    step_of = lambda i, tab_ref: i + first_step
    p_spec, s_spec, mod_spec = _merged_specs(n_p_steps, n_s_steps, layer, sample_seq, step_of)
    dp_spec, ds_spec, _ = _merged_specs(n_p_steps, n_s_steps, layer, sample_seq, step_of, LANES)
    streams = [p_spec, s_spec] if with_prompt else [s_spec]
    dlocs = [dp_spec, ds_spec] if with_prompt else [ds_spec]
    grid_spec = pltpu.PrefetchScalarGridSpec(
        num_scalar_prefetch=1,
        grid=(n_steps,),
        in_specs=streams + [mod_spec] + dlocs + [
            pl.BlockSpec((1, D_MODEL), lambda i, tab_ref: (0, 0)),
            pl.BlockSpec(memory_space=pl.ANY),
        ],
        out_specs=streams,
        scratch_shapes=[
            pltpu.VMEM((2 * ROUTE_SUBS, RUN_ROWS, D_MODEL), F32),
            pltpu.SemaphoreType.DMA((2 * ROUTE_SUBS,)),
        ],
    )
    arrays = [xp, xs] if with_prompt else [xs]
    dloc_arrays = [dloc_p, dloc_s] if with_prompt else [dloc_s]
    return pl.pallas_call(
        functools.partial(_combine_kernel, n_steps=n_steps, first_step=first_step, n_p_steps=n_p_steps,
                          final_norm=final_norm),
        grid_spec=grid_spec,
        out_shape=[jax.ShapeDtypeStruct(a.shape, F32) for a in arrays],
        compiler_params=_cparams("arbitrary"),
        name="moe_combine",
    )(tab, *arrays, mod, *dloc_arrays, final_g, ys)


def _moe(xp, xs, plan_p, plan_s, mod, w1, w3, w2, final_g, layer, sample_seq, final_norm, defer_prompt):
    plan_p, plan_s = _flat_plan(plan_p), _flat_plan(plan_s)
    t = xp.shape[0] + xs.shape[0]
    n_tiles = t // ROUTE_TM
    max_rows = t + N_GROUPS_MOE * (ROUTE_PAD - 1) * n_tiles
    n_rows = (-(-max_rows // FFN_BM) + N_GROUPS_MOE) * FFN_BM
    tab, sorted_x = _dispatch(plan_p, plan_s, n_rows)
    ys = _ffn_group(tab, sorted_x, w1, w3, w2, layer, n_tiles * TAB_W)
    out = _combine(tab, xp, xs, mod, plan_p[1], plan_s[1], final_g, ys, layer, sample_seq, final_norm,
                   not defer_prompt)
    if defer_prompt:
        return None, out[0], (tab, plan_p[1], ys)
    return out[0], out[1], None


def _rope_tables(seq):
    half = QK_ROPE // 2
    nf = half // 2
    inv = ROPE_BASE ** (-np.arange(nf, dtype=np.float64) / nf)
    pos = np.arange(seq)
    row = (pos // GRID_W).astype(np.float64)
    col = (pos % GRID_W).astype(np.float64)
    cos = np.ones((seq, HEAD_PAD), np.float64)
    sin_a = np.zeros((seq, HEAD_PAD), np.float64)
    sin_b = np.zeros((seq, HEAD_PAD), np.float64)
    for part, p in enumerate((row, col)):
        ang = p[:, None] * inv[None, :]
        base = ROPE_OFF + part * half
        cos[:, base:base + nf] = np.cos(ang)
        cos[:, base + nf:base + half] = np.cos(ang)
        sin_a[:, base:base + nf] = -np.sin(ang)
        sin_b[:, base + nf:base + half] = np.sin(ang)
    return tuple(jnp.asarray(a, F32) for a in (cos, sin_a, sin_b))


def _apply_rope(x, cos, sin_a, sin_b, reps):
    nf = QK_ROPE // 4
    width = x.shape[1]
    if reps > 1:
        cos, sin_a, sin_b = (jnp.concatenate([a] * reps, axis=1) for a in (cos, sin_a, sin_b))
    return x * cos + pltpu.roll(x, width - nf, axis=1) * sin_a + pltpu.roll(x, nf, axis=1) * sin_b


def _odd_in_kernel(*refs, rope, emit_cache):
    x_ref, mod_ref, g_ref, w_ref, qg_ref, wq_ref, kg_ref, wk_ref, wv_ref, cs_ref = refs[:10]
    refs = refs[10:]
    if rope:
        cos_ref, sa_ref, sb_ref = refs[:3]
        refs = refs[3:]
    y_ref, q_ref, k_ref, v_ref = refs[:4]
    refs = refs[4:]
    m = mod_ref[...]
    h = _norm_mod(x_ref[...], g_ref[...], m[1:2], m[0:1])
    z = _bdot(h.astype(BF16), w_ref[...])
    zc = z[:, 0:D_C]
    qc = z[:, D_C:D_C + Q_LORA]
    kvc = z[:, D_C + Q_LORA:D_C + Q_LORA + KV_LORA]
    kpe = z[:, D_C + Q_LORA + KV_LORA:]
    q = _bdot(_rms(qc, qg_ref[...]).astype(BF16), wq_ref[...])
    kvn = _rms(kvc, kg_ref[...])
    if emit_cache:
        ckv_ref, kpe_ref = refs
        ckv_ref[...] = kvn
        kpe_ref[...] = kpe[:, ROPE_OFF:ROPE_OFF + QK_ROPE]
    if rope:
        tabs = (cos_ref[...], sa_ref[...], sb_ref[...])
        q = _apply_rope(q, *tabs, reps=N_HEADS)
        kpe = _apply_rope(kpe, *tabs, reps=1)
    kvb = kvn.astype(BF16)
    k = _bdot(kvb, wk_ref[...]) + jnp.concatenate([kpe] * N_HEADS, axis=1)
    scale = math.log2(math.e) / math.sqrt(QK_NOPE + QK_ROPE)
    q_ref[...] = (q * scale).astype(BF16)
    k_ref[...] = k.astype(BF16)
    v_ref[...] = _bdot(kvb, wv_ref[...]).astype(BF16)
    y = _bdot(zc.astype(BF16), cs_ref[...])
    y_ref[0, :, :] = y[:, 0:D_C].astype(BF16)
    y_ref[1, :, :] = y[:, D_C:2 * D_C].astype(BF16)


def _odd_in(x3, mod, g, w_in, q_g, w_q, kv_g, w_k, w_v, cs, stream, layer, rope_tabs, emit_cache):
    b, s, _ = x3.shape
    tm = min(s, 512)
    n_i = s // tm
    rope = rope_tabs is not None
    const = lambda a: pl.BlockSpec(a.shape, lambda bi, i: (0,) * a.ndim)
    in_specs = [
        pl.BlockSpec((None, tm, D_MODEL), lambda bi, i: (bi, i, 0)),
        _mod_spec(stream, layer),
        const(g), const(w_in), const(q_g), const(w_q), const(kv_g), const(w_k), const(w_v), const(cs),
    ]
    args = [x3, mod, g, w_in, q_g, w_q, kv_g, w_k, w_v, cs]
    if rope:
        in_specs += [pl.BlockSpec((tm, HEAD_PAD), lambda bi, i: (i, 0))] * 3
        args += list(rope_tabs)
    hq = N_HEADS * HEAD_PAD
    out_specs = [
        pl.BlockSpec((None, 2, tm, D_C), lambda bi, i: (bi, 0, i, 0)),
        pl.BlockSpec((None, tm, hq), lambda bi, i: (bi, i, 0)),
        pl.BlockSpec((None, tm, hq), lambda bi, i: (bi, i, 0)),
        pl.BlockSpec((None, tm, N_HEADS * V_DIM), lambda bi, i: (bi, i, 0)),
    ]
    out_shape = [
        jax.ShapeDtypeStruct((b, 2, s, D_C), BF16),
        jax.ShapeDtypeStruct((b, s, hq), BF16),
        jax.ShapeDtypeStruct((b, s, hq), BF16),
        jax.ShapeDtypeStruct((b, s, N_HEADS * V_DIM), BF16),
    ]
    if emit_cache:
        out_specs += [
            pl.BlockSpec((None, tm, KV_LORA), lambda bi, i: (bi, i, 0)),
            pl.BlockSpec((None, tm, QK_ROPE), lambda bi, i: (bi, i, 0)),
        ]
        out_shape += [
            jax.ShapeDtypeStruct((b, s, KV_LORA), F32),
            jax.ShapeDtypeStruct((b, s, QK_ROPE), F32),
        ]
    return pl.pallas_call(
        functools.partial(_odd_in_kernel, rope=rope, emit_cache=emit_cache),
        grid=(b, n_i),
        in_specs=in_specs,
        out_specs=out_specs,
        out_shape=out_shape,
        compiler_params=_cparams("parallel", "parallel"),
        name="odd_in",
    )(*args)


def _cache_kv_kernel(c_ref, p_ref, wk_ref, wv_ref, k_ref, v_ref):
    cb = c_ref[...].astype(BF16)
    k = _bdot(cb, wk_ref[...]) + jnp.concatenate([p_ref[...]] * N_HEADS, axis=1)
    k_ref[...] = k.astype(BF16)
    v_ref[...] = _bdot(cb, wv_ref[...]).astype(BF16)


def _cache_kv(ckv, kpe_blk, w_k, w_v):
    b, p, _ = ckv.shape
    hq = N_HEADS * HEAD_PAD
    return pl.pallas_call(
        _cache_kv_kernel,
        grid=(b,),
        in_specs=[
            pl.BlockSpec((None, p, KV_LORA), lambda bi: (bi, 0, 0)),
            pl.BlockSpec((None, p, HEAD_PAD), lambda bi: (bi, 0, 0)),
            pl.BlockSpec(w_k.shape, lambda bi: (0, 0)),
            pl.BlockSpec(w_v.shape, lambda bi: (0, 0)),
        ],
        out_specs=[
            pl.BlockSpec((None, p, hq), lambda bi: (bi, 0, 0)),
            pl.BlockSpec((None, p, N_HEADS * V_DIM), lambda bi: (bi, 0, 0)),
        ],
        out_shape=[
            jax.ShapeDtypeStruct((b, p, hq), BF16),
            jax.ShapeDtypeStruct((b, p, N_HEADS * V_DIM), BF16),
        ],
        compiler_params=_cparams("parallel"),
        name="cache_kv",
    )(ckv, kpe_blk, w_k, w_v)


def _odd_mix_kernel(*refs, with_cache, n_seq=1):
    q_ref, k_ref, v_ref = refs[:3]
    refs = refs[3:]
    if with_cache:
        kc_ref, vc_ref = refs[:2]
        refs = refs[2:]
    plan_tile = None
    if with_cache:
        y_ref, f_ref, x_ref, mod_ref, wo_ref = refs[:5]
        plan_in, (o_ref, *plan_out), (a_ref, vx_ref, vcx_ref) = refs[5:9], refs[9:14], refs[14:]
        plan_tile = pl.program_id(0) * pl.num_programs(1) + pl.program_id(1)

        @pl.when(pl.program_id(1) == 0)
        def _():
            for src, dst in ((v_ref, vx_ref), (vc_ref, vcx_ref)):
                one = lax.broadcasted_iota(jnp.int32, (src.shape[0], PAIR_W - DEN_COL), 1) == 0
                for pair in range(N_HEADS // 2):
                    dst[:, pair * PAIR_W:pair * PAIR_W + DEN_COL] = src[:, pair * DEN_COL:(pair + 1) * DEN_COL]
                    dst[:, pair * PAIR_W + DEN_COL:(pair + 1) * PAIR_W] = jnp.where(one, 1.0, 0.0).astype(BF16)
    else:
        y_ref, f_ref, x_ref, mod_ref, wo_ref, o_ref, a_ref = refs[:7]
        if len(refs) > 7:
            plan_in, plan_out, plan_tile = refs[7:]
    tq = q_ref.shape[0] // n_seq
    lane = lax.broadcasted_iota(jnp.int32, (tq, 2 * V_DIM), 1)
    f_parts = []
    for sq in range(n_seq):
        rows = slice(sq * tq, (sq + 1) * tq)
        for pair in range(N_HEADS // 2):
            vcols = slice(pair * PAIR_W, (pair + 1) * PAIR_W)
            outs = []
            for h in (2 * pair, 2 * pair + 1):
                hcols = slice(h * HEAD_PAD, (h + 1) * HEAD_PAD)
                qh = q_ref[rows, hcols]
                if with_cache:
                    s = lax.dot_general(qh, k_ref[:, hcols], NT_DIMS, preferred_element_type=F32)
                    sc = lax.dot_general(qh, kc_ref[:, hcols], NT_DIMS, preferred_element_type=F32)
                    top = jnp.maximum(jnp.max(s, axis=-1, keepdims=True), jnp.max(sc, axis=-1, keepdims=True))
                    acc = _bdot(jnp.exp2((s - top).astype(BF16)), vx_ref[:, vcols])
                    acc = acc + _bdot(jnp.exp2((sc - top).astype(BF16)), vcx_ref[:, vcols])
                    outs.append(acc[:, 0:2 * V_DIM] / acc[:, DEN_COL:DEN_COL + 1])
                else:
                    s = lax.dot_general(qh, k_ref[rows, hcols], NT_DIMS, preferred_element_type=F32)
                    p = jnp.exp2(s - jnp.max(s, axis=-1, keepdims=True))
                    den = jnp.sum(p, axis=-1, keepdims=True)
                    outs.append(_bdot(p.astype(BF16), v_ref[rows, pair * DEN_COL:(pair + 1) * DEN_COL]) / den)
            a_ref[rows, pair * 2 * V_DIM:(pair + 1) * 2 * V_DIM] = (
                jnp.where(lane < V_DIM, outs[0], outs[1]).astype(BF16))
        if n_seq > 1:
            f_parts.append(_bdot(f_ref[:, 0:tq], y_ref[0, rows, :]) + _bdot(f_ref[:, tq:], y_ref[1, rows, :]))
    f = jnp.concatenate(f_parts, axis=0) if n_seq > 1 else _bdot(f_ref[...], y_ref[...].reshape(-1, D_C))
    o = _bdot(f.astype(BF16), wo_ref[0:D_C, :]) + _bdot(a_ref[...], wo_ref[D_C:, :])
    x1 = x_ref[...] + mod_ref[2:3, :] * o
    o_ref[...] = x1
    if plan_tile is not None:
        _route_plan(x1, mod_ref[...], plan_in, plan_out, plan_tile)


WHOLE_SEQS = 2


def _odd_whole_kernel(tab_ref, x_ref, mod_ref, g_ref, w_ref, qg_ref, wq_ref, kg_ref, wk_ref, wv_ref, cs_ref, f_ref,
                      wo_ref, modp_ref, dloc_ref, ys_ref, pg_ref, pw_ref, pb_ref, pu_ref,
                      o_ref, ckv_ref, kpe_ref, ph_ref, pd_ref, pr_ref, pc_ref,
                      y_ref, q_ref, k_ref, v_ref, a_ref, x2_ref, runs_ref, sem_ref):
    bi = pl.program_id(0)
    par = lax.rem(bi, 2)
    tm = ROUTE_TM

    def fetch(step, parity):
        for sq in range(WHOLE_SEQS):
            _fetch_runs(tab_ref, ys_ref, runs_ref, sem_ref, step * WHOLE_SEQS + sq, parity * WHOLE_SEQS + sq)

    @pl.when(bi == 0)
    def _():
        fetch(0, 0)

    @pl.when(bi + 1 < pl.num_programs(0))
    def _():
        fetch(bi + 1, 1 - par)

    for sq in range(WHOLE_SEQS):
        rows = slice(sq * tm, (sq + 1) * tm)
        moe = _unsort(tab_ref, ys_ref, runs_ref, sem_ref, dloc_ref[rows, 0:1], bi * WHOLE_SEQS + sq,
                      par * WHOLE_SEQS + sq)
        x2_ref[rows, :] = x_ref[rows, :] + modp_ref[5:6, :] * moe
    _odd_in_kernel(x2_ref, mod_ref, g_ref, w_ref, qg_ref, wq_ref, kg_ref, wk_ref, wv_ref, cs_ref,
                   y_ref, q_ref, k_ref, v_ref, ckv_ref, kpe_ref, rope=False, emit_cache=True)
    _odd_mix_kernel(q_ref, k_ref, v_ref, y_ref, f_ref, x2_ref, mod_ref, wo_ref, o_ref, a_ref,
                    (pg_ref, pw_ref, pb_ref, pu_ref), (ph_ref, pd_ref, pr_ref, pc_ref), bi * WHOLE_SEQS,
                    with_cache=False, n_seq=WHOLE_SEQS)


def _odd_whole(x3, pending, mod, g, w_in, q_g, w_q, kv_g, w_k, w_v, cs, fmat, w_out, plan_consts, stream, layer):
    tab, dloc, ys = pending
    b, seq, _ = x3.shape
    assert seq == ROUTE_TM and b % WHOLE_SEQS == 0 and stream.shared_cond
    b, s = b // WHOLE_SEQS, seq * WHOLE_SEQS
    x3 = x3.reshape(b, s, D_MODEL)
    hq = N_HEADS * HEAD_PAD
    hv = N_HEADS * V_DIM
    const = lambda a: pl.BlockSpec(a.shape, lambda bi, tab_ref: (0,) * a.ndim)
    row_block = lambda w: pl.BlockSpec((None, s, w), lambda bi, tab_ref: (bi, 0, 0))
    mod_block = lambda lyr: pl.BlockSpec((None, None, 6, D_MODEL), lambda bi, tab_ref: (lyr, 0, 0, 0))
    plan_shapes, plan_specs = _plan_out(b, s, s, lambda bi, tab_ref: (bi, 0))
    grid_spec = pltpu.PrefetchScalarGridSpec(
        num_scalar_prefetch=1,
        grid=(b,),
        in_specs=[
            row_block(D_MODEL), mod_block(layer),
            const(g), const(w_in), const(q_g), const(w_q), const(kv_g), const(w_k), const(w_v), const(cs),
            const(fmat), const(w_out),
            mod_block(layer - 1),
            pl.BlockSpec((s, LANES), lambda bi, tab_ref: (bi, 0)),
            pl.BlockSpec(memory_space=pl.ANY),
        ] + [const(a) for a in plan_consts],
        out_specs=[row_block(D_MODEL), row_block(KV_LORA), row_block(QK_ROPE)] + plan_specs,
        scratch_shapes=[
            pltpu.VMEM((2, s, D_C), BF16),
            pltpu.VMEM((s, hq), BF16),
            pltpu.VMEM((s, hq), BF16),
            pltpu.VMEM((s, hv), BF16),
            pltpu.VMEM((s, hv), BF16),
            pltpu.VMEM((s, D_MODEL), F32),
            pltpu.VMEM((2 * WHOLE_SEQS, RUN_ROWS, D_MODEL), F32),
            pltpu.SemaphoreType.DMA((2 * WHOLE_SEQS,)),
        ],
    )
    out = pl.pallas_call(
        _odd_whole_kernel,
        grid_spec=grid_spec,
        out_shape=[
            jax.ShapeDtypeStruct((b, s, D_MODEL), F32),
            jax.ShapeDtypeStruct((b, s, KV_LORA), F32),
            jax.ShapeDtypeStruct((b, s, QK_ROPE), F32),
        ] + plan_shapes,
        compiler_params=_cparams("arbitrary"),
        name="odd_whole",
    )(tab, x3, mod, g, w_in, q_g, w_q, kv_g, w_k, w_v, cs, fmat, w_out, mod, dloc, ys, *plan_consts)
    shape = lambda w: (b * WHOLE_SEQS, seq, w)
    return out[0].reshape(shape(D_MODEL)), out[1].reshape(shape(KV_LORA)), out[2].reshape(shape(QK_ROPE)), out[3:]


def _odd_mix(q, k, v, kc, vc, y, fmat, x3, mod, w_out, plan_consts, stream, layer):
    b, s, hq = q.shape
    tq = min(s, 256)
    n_i = s // tq
    with_cache = kc is not None
    hv = N_HEADS * V_DIM
    mode = dict(pipeline_mode=pl.Buffered(1)) if with_cache else {}

    def per_batch(rows, cols):
        return pl.BlockSpec((None, rows, cols), lambda bi, i: (bi, 0, 0), **mode)

    in_specs = [pl.BlockSpec((None, tq, hq), lambda bi, i: (bi, i, 0)), per_batch(s, hq), per_batch(s, hv)]
    args = [q, k, v]
    if with_cache:
        p = kc.shape[1]
        in_specs += [per_batch(p, hq), per_batch(p, hv)]
        args += [kc, vc]
    in_specs += [
        per_batch(2 * s, D_C),
        pl.BlockSpec((tq, 2 * s), lambda bi, i: (i, 0)),
        pl.BlockSpec((None, tq, D_MODEL), lambda bi, i: (bi, i, 0)),
        _mod_spec(stream, layer),
        pl.BlockSpec(w_out.shape, lambda bi, i: (0, 0), **mode),
    ]
    in_specs += [pl.BlockSpec(a.shape, lambda bi, i: (0,) * a.ndim) for a in plan_consts]
    args += [y, fmat, x3, mod, w_out] + plan_consts
    plan_shapes, plan_specs = _plan_out(b, s, tq, lambda bi, i: (bi, i))
    out = pl.pallas_call(
        functools.partial(_odd_mix_kernel, with_cache=with_cache),
        grid=(b, n_i),
        in_specs=in_specs,
        out_specs=[pl.BlockSpec((None, tq, D_MODEL), lambda bi, i: (bi, i, 0))] + plan_specs,
        out_shape=[jax.ShapeDtypeStruct((b, s, D_MODEL), F32)] + plan_shapes,
        scratch_shapes=[pltpu.VMEM((tq, hv), BF16)] + (
            [pltpu.VMEM((s, VX_W), BF16), pltpu.VMEM((kc.shape[1], VX_W), BF16)] if with_cache else []),
        compiler_params=_cparams("arbitrary", "arbitrary"),
        name="odd_mix",
    )(*args)
    return out[0], out[1:]


def _dft_tables(seq):
    jc = np.arange(C_GW)
    ang_c = 2.0 * np.pi * np.outer(jc, jc) / C_GW
    eye = np.eye(C_GROUPS)
    cs = np.concatenate([np.kron(eye, np.cos(ang_c)), np.kron(eye, np.sin(ang_c))], axis=1)
    jn = np.arange(seq)
    ang_n = 2.0 * np.pi * (np.outer(jn, jn) % seq) / seq
    scale = 1.0 / math.sqrt(seq * C_GW)
    fmat = np.concatenate([np.cos(ang_n), -np.sin(ang_n)], axis=1) * scale
    return jnp.asarray(cs, F32).astype(BF16), jnp.asarray(fmat, F32).astype(BF16)


def _odd_weights(w_in, w_uq, w_ukv):
    d = w_in.shape[0]
    base = D_C + Q_LORA + KV_LORA
    kpe_blk = jnp.zeros((d, HEAD_PAD), w_in.dtype).at[:, ROPE_OFF:ROPE_OFF + QK_ROPE].set(w_in[:, base:])
    w_in_p = jnp.concatenate([w_in[:, :base], kpe_blk], axis=1).astype(BF16)
    qh = w_uq.reshape(Q_LORA, N_HEADS, QK_NOPE + QK_ROPE)
    w_q = jnp.pad(qh, ((0, 0), (0, 0), (0, HEAD_PAD - QK_NOPE - QK_ROPE))).reshape(Q_LORA, -1).astype(BF16)
    kvh = w_ukv.reshape(KV_LORA, N_HEADS, QK_NOPE + V_DIM)
    w_k = jnp.pad(kvh[:, :, :QK_NOPE], ((0, 0), (0, 0), (0, HEAD_PAD - QK_NOPE))).reshape(KV_LORA, -1)
    w_v = kvh[:, :, QK_NOPE:].reshape(KV_LORA, -1)
    return w_in_p, w_q, w_k.astype(BF16), w_v.astype(BF16)


ROUTER_ROWS = 32


def _router_weights(wg, bg, we, be):
    d = wg.shape[0]
    w = jnp.concatenate([wg, we.reshape(d, N_EXPERTS)], axis=1).T
    w = jnp.pad(w, ((0, ROUTER_ROWS - w.shape[0]), (0, 0))).astype(BF16)
    b = jnp.concatenate([bg, be.reshape(N_EXPERTS)])
    b = jnp.pad(b, (0, ROUTER_ROWS - b.shape[0])).reshape(ROUTER_ROWS, 1).astype(F32)
    return w, b


def kernel(x_prompt, x_sample, cache_ckv, cache_kpe, c, c_ctx, mod_w, mod_b, norm1_g, norm2_g,
           ev_w_in, ev_conv_w, ev_sgu_norm_g, ev_sgu_w, ev_sgu_b, ev_w_out,
           od_w_in, od_q_norm_g, od_w_uq, od_kv_norm_g, od_w_ukv, od_w_out,
           moe_wg, moe_bg, moe_we, moe_be, moe_w1, moe_w3, moe_w2, final_norm_g):
    bp, n_p, d = x_prompt.shape
    bs, n_s, _ = x_sample.shape
    streams = [(_Stream(bp, n_p, True), x_prompt), (_Stream(bs, n_s, False), x_sample)]

    n_rows = 1 + bs
    mod = _adaln(jnp.concatenate([c_ctx[None, :], c], axis=0), mod_w, mod_b, n_rows)

    final_g = final_norm_g.reshape(1, d)
    xs = [x for _, x in streams]
    new_ckv, new_kpe = [], []
    pending = None
    plans = [None, None]
    for l in range(DEPTH):
        j = l // 2
        g1 = norm1_g[l].reshape(1, d)
        g2 = norm2_g[l].reshape(1, d)
        plan_consts = _plan_consts(g2, *_router_weights(moe_wg[l], moe_bg[l], moe_we[l], moe_be[l]))
        last = l == DEPTH - 1
        if l % 2 == 0:
            w_in = ev_w_in[j].astype(BF16)
            w_out = ev_w_out[j].astype(BF16)
            sgu_w = ev_sgu_w[j].astype(BF16)
            sgu_g = ev_sgu_norm_g[j].reshape(1, D_B)
            sgu_bias = jnp.repeat(ev_sgu_b[j].T, D_B // B_GROUPS, axis=1)
            for si, (st, _) in enumerate(streams):
                xs[si], plans[si] = _even_layer(xs[si], mod, g1, w_in, ev_conv_w[j], sgu_g, sgu_w, sgu_bias, w_out,
                                                plan_consts, st, l)
        else:
            w_in, w_q, w_k, w_v = _odd_weights(od_w_in[j], od_w_uq[j], od_w_ukv[j])
            w_out = od_w_out[j].astype(BF16)
            q_g = od_q_norm_g[j].reshape(1, Q_LORA)
            kv_g = od_kv_norm_g[j].reshape(1, KV_LORA)
            for si, (st, _) in enumerate(streams):
                x3 = xs[si]
                cs, fmat = _dft_tables(st.seq)
                if st.shared_cond:
                    xs[si], ckv, kpe, plans[si] = _odd_whole(x3, pending, mod, g1, w_in, q_g, w_q, kv_g, w_k, w_v, cs,
                                                             fmat, w_out, plan_consts, st, l)
                    new_ckv.append(ckv)
                    new_kpe.append(kpe)
                    continue
                y, q, k, v = _odd_in(x3, mod, g1, w_in, q_g, w_q, kv_g, w_k, w_v, cs, st, l,
                                     _rope_tables(st.seq), False)
                kpe_blk = jnp.pad(cache_kpe[:, j], ((0, 0), (0, 0), (ROPE_OFF, HEAD_PAD - ROPE_OFF - QK_ROPE)))
                kc, vc = _cache_kv(cache_ckv[:, j], kpe_blk, w_k, w_v)
                xs[si], plans[si] = _odd_mix(q, k, v, kc, vc, y.reshape(st.batch, 2 * st.seq, D_C), fmat, x3, mod,
                                             w_out, plan_consts, st, l)
        defer = not last and (l + 1) % 2 == 1
        x2p, x2s, pending = _moe(xs[0].reshape(bp * n_p, d), xs[1].reshape(bs * n_s, d), plans[0], plans[1], mod,
                                 moe_w1, moe_w3, moe_w2, final_g, l, n_s, last, defer)
        xs = [xs[0] if defer else x2p.reshape(bp, n_p, d), x2s.reshape(bs, n_s, d)]
    return (xs[0], xs[1], jnp.stack(new_ckv, axis=1), jnp.stack(new_kpe, axis=1))
```

```python
import functools
import math

import numpy as np
import jax
import jax.numpy as jnp
from jax import lax
from jax.experimental import pallas as pl
from jax.experimental.pallas import tpu as pltpu

D_MODEL = 1024
DEPTH = 2
GRID_W = 64
D_A = D_MODEL // 2
D_B = D_MODEL // 2
B_GROUPS = 4
CHUNK = 128
D_EVEN_IN = 3 * D_A + 2 * D_B
D_C = D_MODEL // 4
C_GROUPS = 4
C_GW = D_C // C_GROUPS
N_HEADS = 12
QK_NOPE = 64
QK_ROPE = 32
V_DIM = 64
Q_LORA = 384
KV_LORA = 256
ROPE_BASE = 10000.0
N_GROUPS_MOE = 4
EXPERTS_PER_GROUP = 4
N_EXPERTS = N_GROUPS_MOE * EXPERTS_PER_GROUP
D_EXPERT = 256
EPS = 1e-6

LANES = 128
HEAD_PAD = 128
PAIR_W = 256
DEN_COL = 2 * 64
VX_W = 6 * PAIR_W
ROPE_OFF = QK_NOPE
GATE_OFF = N_GROUPS_MOE
NEG_BIG = -1e30
F32 = jnp.float32
BF16 = jnp.bfloat16
VMEM_LIMIT = 56 * 1024 * 1024


def _cparams(*sem):
    return pltpu.CompilerParams(dimension_semantics=sem, vmem_limit_bytes=VMEM_LIMIT)


def _rms(x, g):
    return x * lax.rsqrt(jnp.mean(x * x, axis=-1, keepdims=True) + EPS) * g


def _norm_mod(x, g, scale, shift):
    rs = lax.rsqrt(jnp.mean(x * x, axis=-1, keepdims=True) + EPS)
    return x * rs * (g * (1.0 + scale)) + shift


def _bdot(a, b):
    return jnp.dot(a, b, preferred_element_type=F32)


NT_DIMS = (((1,), (1,)), ((), ()))


MOD_ROWS = 8


def _mod_kernel(c_ref, w_ref, b_ref, o_ref):
    c = c_ref[...]
    s = c * jax.nn.sigmoid(c)
    s_hi = s.astype(BF16).astype(F32)
    pair = jnp.concatenate([s_hi, s - s_hi], axis=0).astype(BF16)
    two = _bdot(pair, w_ref[...].astype(BF16))
    o_ref[...] = two[0:MOD_ROWS] + two[MOD_ROWS:] + b_ref[...]


def _adaln(cond, mod_w, mod_b, n_rows):
    nt = 2048
    d6 = mod_w.shape[-1]
    cond = jnp.pad(cond, ((0, MOD_ROWS - n_rows), (0, 0)))
    out = pl.pallas_call(
        _mod_kernel,
        grid=(DEPTH, d6 // nt),
        in_specs=[
            pl.BlockSpec(cond.shape, lambda l, n: (0, 0)),
            pl.BlockSpec((None, D_MODEL, nt), lambda l, n: (l, 0, n)),
            pl.BlockSpec((None, 1, nt), lambda l, n: (l, 0, n)),
        ],
        out_specs=pl.BlockSpec((None, MOD_ROWS, nt), lambda l, n: (l, 0, n)),
        out_shape=jax.ShapeDtypeStruct((DEPTH, MOD_ROWS, d6), F32),
        compiler_params=_cparams("parallel", "parallel"),
        name="adaln",
    )(cond, mod_w, mod_b.reshape(DEPTH, 1, d6))
    return out[:, :n_rows].reshape(DEPTH, n_rows, 6, D_MODEL)


class _Stream:
    def __init__(self, batch, seq, shared_cond):
        self.batch = batch
        self.seq = seq
        self.shared_cond = shared_cond

    def row_of_batch(self, b):
        return 0 if self.shared_cond else b + 1


def _mod_spec(stream, layer):
    return pl.BlockSpec((None, None, 6, D_MODEL), lambda b, i: (layer, stream.row_of_batch(b), 0, 0))


HALO = 8


def _even_kernel(*refs, has_halo, seq_rows, nc):
    if has_halo:
        x_ref, xp_ref, xn_ref = refs[:3]
        refs = refs[3:]
    else:
        x_ref = refs[0]
        refs = refs[1:]
    mod_ref, g_ref, wi_ref, cw_ref, sg_ref, sw_ref, sb_ref, wo_ref = refs[:8]
    plan_in, (o_ref, *plan_out), (z_ref, y_ref) = refs[8:12], refs[12:17], refs[17:]
    i = pl.program_id(1)
    n_i = pl.num_programs(1)
    ts = x_ref.shape[0]
    m = mod_ref[...]
    g = g_ref[...]

    def modulate(x):
        return _norm_mod(x, g, m[1:2], m[0:1]).astype(BF16)

    x = x_ref[...]
    hb = modulate(x)
    for n in range(D_EVEN_IN // nc):
        z_ref[:, n * nc:(n + 1) * nc] = _bdot(hb, wi_ref[:, n * nc:(n + 1) * nc]).astype(BF16)

    gate_b = z_ref[:, 0:D_A].astype(F32)
    gate_c = z_ref[:, D_A:2 * D_A].astype(F32)
    xa = z_ref[:, 2 * D_A:3 * D_A].astype(F32)
    t = gate_c * xa
    t_prev = pltpu.roll(t, 1, axis=0)
    t_next = pltpu.roll(t, ts - 1, axis=0)
    row = lax.broadcasted_iota(jnp.int32, (ts, 1), 0) & (seq_rows - 1)
    if has_halo:
        hh = modulate(jnp.concatenate([xp_ref[...], xn_ref[...]], axis=0))
        zh = _bdot(hh, wi_ref[:, D_A:3 * D_A]).astype(BF16).astype(F32)
        th = zh[:, 0:D_A] * zh[:, D_A:2 * D_A]
        tp = th[HALO - 1:HALO] * (i > 0).astype(F32)
        tn = th[HALO:HALO + 1] * (i < n_i - 1).astype(F32)
    else:
        tp = tn = 0.0
    t_prev = jnp.where(row == 0, tp, t_prev)
    t_next = jnp.where(row == seq_rows - 1, tn, t_next)
    cw = cw_ref[...]
    y_a = gate_b * (t_prev * cw[0:1] + t * cw[1:2] + t_next * cw[2:3])
    y_ref[:, 0:D_A] = y_a.astype(BF16)

    u = z_ref[:, 3 * D_A:3 * D_A + D_B].astype(F32)
    v = z_ref[:, 3 * D_A + D_B:3 * D_A + 2 * D_B].astype(F32)
    vb = _rms(v, sg_ref[...]).astype(BF16)
    gw = D_B // B_GROUPS
    for c in range(ts // CHUNK):
        rows = slice(c * CHUNK, (c + 1) * CHUNK)
        for gi in range(B_GROUPS):
            cols = slice(gi * gw, (gi + 1) * gw)
            sv = _bdot(sw_ref[gi], vb[rows, cols]) + sb_ref[:, cols]
            y_ref[rows, D_A + gi * gw:D_A + (gi + 1) * gw] = (u[rows, cols] * sv).astype(BF16)

    x1 = x + m[2:3] * _bdot(y_ref[...], wo_ref[...])
    o_ref[...] = x1
    _route_plan(x1, m, plan_in, plan_out, (pl.program_id(0) * n_i + i) * (ts // ROUTE_TM))


def _even_layer(x3, mod, g, w_in, conv_w, sgu_g, sgu_w, sgu_bias, w_out, plan_consts, stream, layer):
    out_shape = x3.shape
    b, s, _ = x3.shape
    ts = min(s, 256) if stream.shared_cond else min(s, 512)
    n_i = s // ts
    has_halo = n_i > 1
    seq_rows = ts
    if not has_halo and stream.shared_cond and b % 2 == 0:
        b, s, ts = b // 2, 2 * s, 2 * ts
        x3 = x3.reshape(b, s, D_MODEL)
    hb = ts // HALO
    last_h = s // HALO - 1
    const = lambda a: pl.BlockSpec(a.shape, lambda bi, i: (0,) * a.ndim)
    in_specs = [pl.BlockSpec((None, ts, D_MODEL), lambda bi, i: (bi, i, 0))]
    args = [x3]
    if has_halo:
        in_specs += [
            pl.BlockSpec((None, HALO, D_MODEL), lambda bi, i: (bi, jnp.maximum(i * hb - 1, 0), 0)),
            pl.BlockSpec((None, HALO, D_MODEL), lambda bi, i: (bi, jnp.minimum((i + 1) * hb, last_h), 0)),
        ]
        args += [x3, x3]
    in_specs += [_mod_spec(stream, layer), const(g), const(w_in), const(conv_w), const(sgu_g), const(sgu_w),
                 const(sgu_bias), const(w_out)] + [const(a) for a in plan_consts]
    args += [mod, g, w_in, conv_w, sgu_g, sgu_w, sgu_bias, w_out] + plan_consts
    plan_shapes, plan_specs = _plan_out(b, s, ts, lambda bi, i: (bi, i))
    out = pl.pallas_call(
        functools.partial(_even_kernel, has_halo=has_halo, seq_rows=seq_rows, nc=512),
        grid=(b, n_i),
        in_specs=in_specs,
        out_specs=[pl.BlockSpec((None, ts, D_MODEL), lambda bi, i: (bi, i, 0))] + plan_specs,
        out_shape=[jax.ShapeDtypeStruct((b, s, D_MODEL), F32)] + plan_shapes,
        scratch_shapes=[pltpu.VMEM((ts, D_EVEN_IN), BF16), pltpu.VMEM((ts, D_A + D_B), BF16)],
        compiler_params=_cparams("arbitrary", "arbitrary"),
        name="even_layer",
    )(*args)
    return out[0].reshape(out_shape), out[1:]


ROUTE_TM = 256
ROUTE_SUBS = 4
ROUTE_STEP = ROUTE_TM * ROUTE_SUBS
ROUTE_PAD = 8
SORT_ROWS = ROUTE_TM + LANES
RUN_ROWS = SORT_ROWS + 32
ROUTE_ROWS = ROUTE_TM + 4 * ROUTE_PAD
XS_W = D_MODEL + LANES
GATE_LO = EXPERTS_PER_GROUP
DLOC_HI = 2 * EXPERTS_PER_GROUP
DLOC_RADIX = 16.0
FFN_BM = 512
FFN_HALF = FFN_BM // 2
RUN_SIZES = (256, 128, 64, 32, 16, 8)
TAB_W = 2 * N_GROUPS_MOE


def _round_up(x, m):
    return lax.div(x + (m - 1), m) * m


def _run_copies(tab_ref, tile, hbm_ref, vmem_ref, sem, to_hbm, wait):
    off = 0
    for g in range(N_GROUPS_MOE):
        start = tab_ref[tile * TAB_W + g]
        n = tab_ref[tile * TAB_W + N_GROUPS_MOE + g]
        for p in RUN_SIZES:
            done = n & (-2 * p)

            @pl.when((n & p) != 0)
            def _():
                v = vmem_ref.at[pl.ds(pl.multiple_of(off + done, ROUTE_PAD), p)]
                h = hbm_ref.at[pl.ds(pl.multiple_of(start + done, ROUTE_PAD), p)]
                cp = pltpu.make_async_copy(v, h, sem) if to_hbm else pltpu.make_async_copy(h, v, sem)
                if wait:
                    cp.wait()
                else:
                    cp.start()
        off = off + n


def _zero_fill(tab_ref, meta, zeros_ref, hbm_ref, sem, n_rows, wait):
    def copy(rows, dst_row):
        cp = pltpu.make_async_copy(zeros_ref.at[pl.ds(0, rows)],
                                   hbm_ref.at[pl.ds(pl.multiple_of(dst_row, ROUTE_PAD), rows)], sem)
        if wait:
            cp.wait()
        else:
            cp.start()

    end = 0
    for g in range(N_GROUPS_MOE):
        fill = tab_ref[meta + g]
        start = tab_ref[meta + N_GROUPS_MOE + g]
        end = start + _round_up(fill, FFN_BM)
        tail = end - start - fill
        for p in RUN_SIZES:
            pl.when((tail & p) != 0)(functools.partial(copy, p, start + fill + (tail & (-2 * p))))
    for k in range(n_rows // FFN_BM):
        pl.when(end + k * FFN_BM < n_rows)(functools.partial(copy, FFN_BM, end + k * FFN_BM))


def _wait_rows(n, hbm_ref, vmem_ref, sem, to_hbm):
    for p in RUN_SIZES:
        @pl.when((n & p) != 0)
        def _():
            v = vmem_ref.at[pl.ds(0, p)]
            h = hbm_ref.at[pl.ds(0, p)]
            (pltpu.make_async_copy(v, h, sem) if to_hbm else pltpu.make_async_copy(h, v, sem)).wait()


def _tile_rows(tab_ref, tile):
    n = 0
    for g in range(N_GROUPS_MOE):
        n = n + tab_ref[tile * TAB_W + N_GROUPS_MOE + g]
    return n


def _max4(v):
    return jnp.maximum(jnp.maximum(v[0], v[1]), jnp.maximum(v[2], v[3]))


def _first_of4(v, top):
    return jnp.where(v[0] == top, 0.0, jnp.where(v[1] == top, 1.0, jnp.where(v[2] == top, 2.0, 3.0)))


def _route_plan(x, m, plan_in, plan_out, tile0):
    g_ref, wrt_ref, brt_ref, upper_ref = plan_in
    haug_ref, dloc_ref, drow_ref, cnt_ref = plan_out
    rows = x.shape[0]
    tm = ROUTE_TM
    ng = N_GROUPS_MOE
    h = _norm_mod(x, g_ref[...], m[4:5], m[3:4])
    hb = h.astype(BF16)
    lt = lax.dot_general(wrt_ref[...], hb, NT_DIMS, preferred_element_type=F32) + brt_ref[...]
    gl = [lt[r:r + 1, :] for r in range(ng)]
    g_top = _max4(gl)
    g_idx = _first_of4(gl, g_top)
    g_w = 1.0 / (jnp.exp(gl[0] - g_top) + jnp.exp(gl[1] - g_top) + jnp.exp(gl[2] - g_top) + jnp.exp(gl[3] - g_top))
    ev = []
    for k in range(EXPERTS_PER_GROUP):
        cand = [lt[GATE_OFF + EXPERTS_PER_GROUP * r + k:GATE_OFF + EXPERTS_PER_GROUP * r + k + 1, :]
                for r in range(ng)]
        ev.append(jnp.where(g_idx == 0.0, cand[0], jnp.where(g_idx == 1.0, cand[1],
                            jnp.where(g_idx == 2.0, cand[2], cand[3]))))
    v1 = _max4(ev)
    i1 = _first_of4(ev, v1)
    rest = [jnp.where(i1 == float(k), NEG_BIG, ev[k]) for k in range(EXPERTS_PER_GROUP)]
    v2 = _max4(rest)
    i2 = _first_of4(rest, v2)
    e2 = jnp.exp(v2 - v1)
    w1 = 1.0 / (1.0 + e2)
    w2 = e2 * w1
    gates = [g_w * (jnp.where(i1 == float(k), w1, 0.0) + jnp.where(i2 == float(k), w2, 0.0))
             for k in range(EXPERTS_PER_GROUP)]

    sub8 = lax.broadcasted_iota(jnp.int32, (8, tm), 0).astype(F32)
    dlocs = []
    for sub in range(rows // tm):
        gi = g_idx[:, sub * tm:(sub + 1) * tm]
        hot = jnp.where(sub8 == gi, 1.0, 0.0)
        before = _bdot(hot.astype(BF16), upper_ref[...])
        dl = jnp.sum(before * hot, axis=0, keepdims=True)
        off = 0
        for g in range(ng):
            n_g = _round_up(jnp.sum(hot[g:g + 1, :]).astype(jnp.int32), ROUTE_PAD)
            cnt_ref[(tile0 + sub) * ng + g] = n_g
            dl = dl + jnp.where(gi == float(g), off.astype(F32) if g else 0.0, 0.0)
            off = off + n_g
        drow_ref[sub] = jnp.broadcast_to(dl, (8, tm))
        dlocs.append(dl)
    dloc = jnp.concatenate(dlocs, axis=1)
    d_hi = jnp.floor(dloc * (1.0 / DLOC_RADIX))
    g_hi = [gt.astype(BF16).astype(F32) for gt in gates]
    ex_rows = g_hi + [gt - gh for gt, gh in zip(gates, g_hi)] + [d_hi, dloc - DLOC_RADIX * d_hi]
    sub16 = lax.broadcasted_iota(jnp.int32, (16, rows), 0)
    ex_t = jnp.zeros((16, rows), F32)
    for r, row in enumerate(ex_rows):
        ex_t = jnp.where(sub16 == r, row, ex_t)
    ex_t = jnp.concatenate([ex_t, jnp.zeros((LANES - 16, rows), F32)], axis=0)
    extras = ex_t.T
    dloc_ref[...] = jnp.broadcast_to(
        DLOC_RADIX * extras[:, DLOC_HI:DLOC_HI + 1] + extras[:, DLOC_HI + 1:DLOC_HI + 2], (rows, LANES))
    haug_ref[...] = jnp.concatenate([hb, extras.astype(BF16)], axis=1)


def _plan_consts(g2, w_rt, b_rt):
    upper = jnp.asarray(np.triu(np.ones((ROUTE_TM, ROUTE_TM), np.float32), 1), BF16)
    return [g2, w_rt, b_rt, upper]


def _plan_out(b, s, rows, index):
    tiles = rows // ROUTE_TM
    shapes = [jax.ShapeDtypeStruct((b, s, XS_W), BF16), jax.ShapeDtypeStruct((b, s, LANES), F32),
              jax.ShapeDtypeStruct((b, s // ROUTE_TM, 8, ROUTE_TM), F32),
              jax.ShapeDtypeStruct((b * (s // ROUTE_TM) * N_GROUPS_MOE,), jnp.int32)]
    specs = [pl.BlockSpec((None, rows, XS_W), lambda *idx: (*index(*idx), 0)),
             pl.BlockSpec((None, rows, LANES), lambda *idx: (*index(*idx), 0)),
             pl.BlockSpec((None, tiles, 8, ROUTE_TM), lambda *idx: (*index(*idx), 0, 0)),
             pl.BlockSpec(memory_space=pltpu.SMEM)]
    return shapes, specs


def _flat_plan(plan):
    haug, dloc, drow, cnt = plan
    return (haug.reshape(-1, XS_W), dloc.reshape(-1, LANES), drow.reshape(-1, 8, ROUTE_TM), cnt)


def _dispatch_kernel(cp_ref, cs_ref, hp_ref, hs_ref, dp_ref, ds_ref, tab_ref, sorted_hbm,
                     sorted_ref, zeros_ref, fill_ref, sem_ref, zsem_ref, *, n_steps, n_p_steps):
    i = pl.program_id(0)
    tm = ROUTE_TM
    ng = N_GROUPS_MOE
    n_tiles = n_steps * ROUTE_SUBS
    n_p_tiles = n_p_steps * ROUTE_SUBS
    meta = n_tiles * TAB_W
    n_rows = sorted_hbm.shape[0]

    def count(tile, g):
        if isinstance(tile, int):
            return cp_ref[tile * ng + g] if tile < n_p_tiles else cs_ref[(tile - n_p_tiles) * ng + g]
        return jnp.where(tile < n_p_tiles, cp_ref[jnp.minimum(tile, n_p_tiles - 1) * ng + g],
                         cs_ref[jnp.maximum(tile - n_p_tiles, 0) * ng + g])

    @pl.when(i == 0)
    def _():
        start = 0
        for g in range(ng):
            fill = sum(count(t, g) for t in range(n_tiles))
            tab_ref[meta + g] = fill
            tab_ref[meta + ng + g] = start
            start = start + _round_up(fill, FFN_BM)
            fill_ref[g] = 0
        zeros_ref[...] = jnp.zeros_like(zeros_ref)

    is_prompt = i < n_p_steps
    row_f = lax.broadcasted_iota(jnp.int32, (ROUTE_ROWS, tm), 0).astype(F32)

    def sort_tile(h_ref, d_ref, sub):
        onehot = jnp.where(row_f == d_ref[sub][0:1, :], 1.0, 0.0).astype(BF16)
        sorted_ref[sub] = _bdot(onehot, h_ref[sub * tm:(sub + 1) * tm, :])

    for sub in range(ROUTE_SUBS):
        tile = i * ROUTE_SUBS + sub

        @pl.when(i >= 1)
        def _():
            _wait_rows(_tile_rows(tab_ref, tile - ROUTE_SUBS), sorted_hbm, sorted_ref.at[sub], sem_ref.at[sub], True)

        pl.when(is_prompt)(functools.partial(sort_tile, hp_ref, dp_ref, sub))
        pl.when(jnp.logical_not(is_prompt))(functools.partial(sort_tile, hs_ref, ds_ref, sub))
        for g in range(ng):
            n_g = count(tile, g)
            tab_ref[tile * TAB_W + g] = tab_ref[meta + ng + g] + fill_ref[g]
            tab_ref[tile * TAB_W + ng + g] = n_g
            fill_ref[g] = fill_ref[g] + n_g
        _run_copies(tab_ref, tile, sorted_hbm, sorted_ref.at[sub], sem_ref.at[sub], to_hbm=True, wait=False)

        @pl.when(i == n_steps - 1)
        def _():
            _wait_rows(_tile_rows(tab_ref, tile), sorted_hbm, sorted_ref.at[sub], sem_ref.at[sub], True)

    @pl.when(i == n_steps - 1)
    def _():
        _zero_fill(tab_ref, meta, zeros_ref, sorted_hbm, zsem_ref, n_rows, wait=False)
        _zero_fill(tab_ref, meta, zeros_ref, sorted_hbm, zsem_ref, n_rows, wait=True)


def _merged_specs(n_p_steps, n_s_steps, layer, sample_seq, step_of, width=D_MODEL):
    def p_map(*idx):
        return (jnp.minimum(step_of(*idx), n_p_steps - 1), 0)

    def s_map(*idx):
        return (jnp.clip(step_of(*idx) - n_p_steps, 0, n_s_steps - 1), 0)

    def mod_map(*idx):
        j = step_of(*idx)
        row = jnp.where(j < n_p_steps, 0, 1 + lax.div(jnp.maximum(j - n_p_steps, 0) * ROUTE_STEP, sample_seq))
        return (layer, row, 0, 0)

    return (pl.BlockSpec((ROUTE_STEP, width), p_map), pl.BlockSpec((ROUTE_STEP, width), s_map),
            pl.BlockSpec((None, None, 6, D_MODEL), mod_map))


def _dispatch(plan_p, plan_s, n_rows):
    haug_p, _, drow_p, cnt_p = plan_p
    haug_s, _, drow_s, cnt_s = plan_s
    n_p_steps = haug_p.shape[0] // ROUTE_STEP
    n_s_steps = haug_s.shape[0] // ROUTE_STEP
    n_steps = n_p_steps + n_s_steps
    n_tiles = n_steps * ROUTE_SUBS
    step_of = lambda i, cp, cs: i
    hp_spec, hs_spec, _ = _merged_specs(n_p_steps, n_s_steps, 0, 1, step_of, XS_W)
    drow_block = (ROUTE_SUBS, 8, ROUTE_TM)
    grid_spec = pltpu.PrefetchScalarGridSpec(
        num_scalar_prefetch=2,
        grid=(n_steps,),
        in_specs=[
            hp_spec, hs_spec,
            pl.BlockSpec(drow_block, lambda i, cp, cs: (jnp.minimum(i, n_p_steps - 1), 0, 0)),
            pl.BlockSpec(drow_block, lambda i, cp, cs: (jnp.clip(i - n_p_steps, 0, n_s_steps - 1), 0, 0)),
        ],
        out_specs=[pl.BlockSpec(memory_space=pltpu.SMEM), pl.BlockSpec(memory_space=pl.ANY)],
        scratch_shapes=[
            pltpu.VMEM((ROUTE_SUBS, ROUTE_ROWS, XS_W), F32),
            pltpu.VMEM((FFN_BM, XS_W), F32),
            pltpu.SMEM((N_GROUPS_MOE,), jnp.int32),
            pltpu.SemaphoreType.DMA((ROUTE_SUBS,)),
            pltpu.SemaphoreType.DMA(()),
        ],
    )
    return pl.pallas_call(
        functools.partial(_dispatch_kernel, n_steps=n_steps, n_p_steps=n_p_steps),
        grid_spec=grid_spec,
        out_shape=[
            jax.ShapeDtypeStruct(((n_tiles + 1) * TAB_W,), jnp.int32),
            jax.ShapeDtypeStruct((n_rows, XS_W), F32),
        ],
        compiler_params=_cparams("arbitrary"),
        name="moe_dispatch",
    )(cnt_p, cnt_s, haug_p, haug_s, drow_p, drow_s)


def _ffn_lookup(i, tab_ref, meta):
    fills = [tab_ref[meta + g] for g in range(N_GROUPS_MOE)]
    edges = []
    acc = 0
    for f in fills:
        acc = acc + lax.div(f + (FFN_BM - 1), FFN_BM)
        edges.append(acc)
    total = edges[-1]
    ii = jnp.minimum(i, total - 1)
    grp = sum((ii >= e).astype(jnp.int32) for e in edges[:-1])

    def pick(vals):
        return jnp.where(grp == 0, vals[0], jnp.where(grp == 1, vals[1], jnp.where(grp == 2, vals[2], vals[3])))

    first = pick([0] + edges[:-1])
    return grp, total, ii == first, pick(fills) - (ii - first) * FFN_BM


def _ffn_group_kernel(tab_ref, xs_ref, w1_ref, w3_ref, w2_ref, ys_ref, w1b_ref, w3b_ref, w2b_ref, *, meta):
    i = pl.program_id(0)
    _, total, first_of_group, valid = _ffn_lookup(i, tab_ref, meta)
    active = i < total

    @pl.when(jnp.logical_and(active, first_of_group))
    def _():
        w1b_ref[...] = w1_ref[...].astype(BF16)
        w3b_ref[...] = w3_ref[...].astype(BF16)
        w2b_ref[...] = w2_ref[...].astype(BF16)

    def run(rows):
        hb = xs_ref[0:rows, 0:D_MODEL].astype(BF16)
        ex = xs_ref[0:rows, D_MODEL:XS_W]
        hid = []
        for e in range(EXPERTS_PER_GROUP):
            a = _bdot(hb, w1b_ref[e])
            b = _bdot(hb, w3b_ref[e])
            gate = ex[:, e:e + 1] + ex[:, GATE_LO + e:GATE_LO + e + 1]
            hid.append(((a * jax.nn.sigmoid(a)) * b * gate).astype(BF16))
        ys_ref[0:rows, :] = _bdot(jnp.concatenate(hid, axis=1), w2b_ref[...].reshape(-1, D_MODEL))

    @pl.when(jnp.logical_and(active, valid > FFN_HALF))
    def _():
        run(FFN_BM)

    @pl.when(jnp.logical_and(active, valid <= FFN_HALF))
    def _():
        run(FFN_HALF)
        ys_ref[FFN_HALF:, :] = jnp.zeros((FFN_BM - FFN_HALF, D_MODEL), F32)

    @pl.when(jnp.logical_not(active))
    def _():
        ys_ref[...] = jnp.zeros_like(ys_ref)


def _ffn_group(tab, xs, w1, w3, w2, layer, meta):
    e4 = EXPERTS_PER_GROUP
    n_blocks = xs.shape[0] // FFN_BM
    group_of = lambda i, tab_ref: _ffn_lookup(i, tab_ref, meta)[0]
    grid_spec = pltpu.PrefetchScalarGridSpec(
        num_scalar_prefetch=1,
        grid=(n_blocks,),
        in_specs=[
            pl.BlockSpec((FFN_BM, XS_W), lambda i, tab_ref: (i, 0)),
            pl.BlockSpec((None, e4, D_MODEL, D_EXPERT), lambda i, tab_ref: (layer, group_of(i, tab_ref), 0, 0)),
            pl.BlockSpec((None, e4, D_MODEL, D_EXPERT), lambda i, tab_ref: (layer, group_of(i, tab_ref), 0, 0)),
            pl.BlockSpec((None, e4, D_EXPERT, D_MODEL), lambda i, tab_ref: (layer, group_of(i, tab_ref), 0, 0)),
        ],
        out_specs=pl.BlockSpec((FFN_BM, D_MODEL), lambda i, tab_ref: (i, 0)),
        scratch_shapes=[
            pltpu.VMEM((e4, D_MODEL, D_EXPERT), BF16),
            pltpu.VMEM((e4, D_MODEL, D_EXPERT), BF16),
            pltpu.VMEM((e4, D_EXPERT, D_MODEL), BF16),
        ],
    )
    return pl.pallas_call(
        functools.partial(_ffn_group_kernel, meta=meta),
        grid_spec=grid_spec,
        out_shape=jax.ShapeDtypeStruct((xs.shape[0], D_MODEL), F32),
        compiler_params=_cparams("arbitrary"),
        name="moe_ffn",
    )(tab, xs, w1, w3, w2)


def _fetch_runs(tab_ref, ys_ref, runs_ref, sem_ref, tile, slot):
    _run_copies(tab_ref, tile, ys_ref, runs_ref.at[slot], sem_ref.at[slot], to_hbm=False, wait=False)


def _unsort(tab_ref, ys_ref, runs_ref, sem_ref, dloc_col, tile, slot):
    covered = _tile_rows(tab_ref, tile)
    _wait_rows(covered, ys_ref, runs_ref.at[slot], sem_ref.at[slot], False)
    runs_ref[slot, pl.ds(pl.multiple_of(covered, ROUTE_PAD), LANES), :] = jnp.zeros((LANES, D_MODEL), F32)
    yb = runs_ref[slot, 0:SORT_ROWS, :].astype(BF16)
    row_f = lax.broadcasted_iota(jnp.int32, (ROUTE_TM, SORT_ROWS), 1).astype(F32)
    return _bdot(jnp.where(row_f == dloc_col, 1.0, 0.0).astype(BF16), yb)


def _combine_kernel(tab_ref, *refs, n_steps, first_step, n_p_steps, final_norm):
    with_prompt = first_step < n_p_steps
    if with_prompt:
        xp_ref, xs_ref, mod_ref, dp_ref, ds_ref, fg_ref, ys_ref, op_ref, os_ref, runs_ref, sem_ref = refs
    else:
        xs_ref, mod_ref, ds_ref, fg_ref, ys_ref, os_ref, runs_ref, sem_ref = refs
    i = pl.program_id(0)
    par = lax.rem(i, 2)
    step = first_step + i
    tm = ROUTE_TM

    def fetch(st, parity):
        for sub in range(ROUTE_SUBS):
            _fetch_runs(tab_ref, ys_ref, runs_ref, sem_ref, st * ROUTE_SUBS + sub, parity * ROUTE_SUBS + sub)

    @pl.when(i == 0)
    def _():
        fetch(first_step, 0)

    @pl.when(i + 1 < n_steps)
    def _():
        fetch(step + 1, 1 - par)

    is_prompt = step < n_p_steps
    dloc = jnp.where(is_prompt, dp_ref[:, 0:1], ds_ref[:, 0:1]) if with_prompt else ds_ref[:, 0:1]
    parts = [_unsort(tab_ref, ys_ref, runs_ref, sem_ref, dloc[sub * tm:(sub + 1) * tm],
                     step * ROUTE_SUBS + sub, par * ROUTE_SUBS + sub) for sub in range(ROUTE_SUBS)]
    delta = mod_ref[5:6, :] * jnp.concatenate(parts, axis=0)

    def finish(x_ref, o_ref):
        x2 = x_ref[...] + delta
        o_ref[...] = _rms(x2, fg_ref[...]) if final_norm else x2

    if with_prompt:
        pl.when(is_prompt)(functools.partial(finish, xp_ref, op_ref))
        pl.when(jnp.logical_not(is_prompt))(functools.partial(finish, xs_ref, os_ref))
    else:
        finish(xs_ref, os_ref)


def _combine(tab, xp, xs, mod, dloc_p, dloc_s, final_g, ys, layer, sample_seq, final_norm, with_prompt):
    n_p_steps = xp.shape[0] // ROUTE_STEP
    n_s_steps = xs.shape[0] // ROUTE_STEP
    first_step = 0 if with_prompt else n_p_steps
    n_steps = n_p_steps + n_s_steps - first_step
    step_of = lambda i, tab_ref: i + first_step
    p_spec, s_spec, mod_spec = _merged_specs(n_p_steps, n_s_steps, layer, sample_seq, step_of)
    dp_spec, ds_spec, _ = _merged_specs(n_p_steps, n_s_steps, layer, sample_seq, step_of, LANES)
    streams = [p_spec, s_spec] if with_prompt else [s_spec]
    dlocs = [dp_spec, ds_spec] if with_prompt else [ds_spec]
    grid_spec = pltpu.PrefetchScalarGridSpec(
        num_scalar_prefetch=1,
        grid=(n_steps,),
        in_specs=streams + [mod_spec] + dlocs + [
            pl.BlockSpec((1, D_MODEL), lambda i, tab_ref: (0, 0)),
            pl.BlockSpec(memory_space=pl.ANY),
        ],
        out_specs=streams,
        scratch_shapes=[
            pltpu.VMEM((2 * ROUTE_SUBS, RUN_ROWS, D_MODEL), F32),
            pltpu.SemaphoreType.DMA((2 * ROUTE_SUBS,)),
        ],
    )
    arrays = [xp, xs] if with_prompt else [xs]
    dloc_arrays = [dloc_p, dloc_s] if with_prompt else [dloc_s]
    return pl.pallas_call(
        functools.partial(_combine_kernel, n_steps=n_steps, first_step=first_step, n_p_steps=n_p_steps,
                          final_norm=final_norm),
        grid_spec=grid_spec,
        out_shape=[jax.ShapeDtypeStruct(a.shape, F32) for a in arrays],
        compiler_params=_cparams("arbitrary"),
        name="moe_combine",
    )(tab, *arrays, mod, *dloc_arrays, final_g, ys)


def _moe(xp, xs, plan_p, plan_s, mod, w1, w3, w2, final_g, layer, sample_seq, final_norm, defer_prompt):
    plan_p, plan_s = _flat_plan(plan_p), _flat_plan(plan_s)
    t = xp.shape[0] + xs.shape[0]
    n_tiles = t // ROUTE_TM
    max_rows = t + N_GROUPS_MOE * (ROUTE_PAD - 1) * n_tiles
    n_rows = (-(-max_rows // FFN_BM) + N_GROUPS_MOE) * FFN_BM
    tab, sorted_x = _dispatch(plan_p, plan_s, n_rows)
    ys = _ffn_group(tab, sorted_x, w1, w3, w2, layer, n_tiles * TAB_W)
    out = _combine(tab, xp, xs, mod, plan_p[1], plan_s[1], final_g, ys, layer, sample_seq, final_norm,
                   not defer_prompt)
    if defer_prompt:
        return None, out[0], (tab, plan_p[1], ys)
    return out[0], out[1], None


def _rope_tables(seq):
    half = QK_ROPE // 2
    nf = half // 2
    inv = ROPE_BASE ** (-np.arange(nf, dtype=np.float64) / nf)
    pos = np.arange(seq)
    row = (pos // GRID_W).astype(np.float64)
    col = (pos % GRID_W).astype(np.float64)
    cos = np.ones((seq, HEAD_PAD), np.float64)
    sin_a = np.zeros((seq, HEAD_PAD), np.float64)
    sin_b = np.zeros((seq, HEAD_PAD), np.float64)
    for part, p in enumerate((row, col)):
        ang = p[:, None] * inv[None, :]
        base = ROPE_OFF + part * half
        cos[:, base:base + nf] = np.cos(ang)
        cos[:, base + nf:base + half] = np.cos(ang)
        sin_a[:, base:base + nf] = -np.sin(ang)
        sin_b[:, base + nf:base + half] = np.sin(ang)
    return tuple(jnp.asarray(a, F32) for a in (cos, sin_a, sin_b))


def _apply_rope(x, cos, sin_a, sin_b, reps):
    nf = QK_ROPE // 4
    width = x.shape[1]
    if reps > 1:
        cos, sin_a, sin_b = (jnp.concatenate([a] * reps, axis=1) for a in (cos, sin_a, sin_b))
    return x * cos + pltpu.roll(x, width - nf, axis=1) * sin_a + pltpu.roll(x, nf, axis=1) * sin_b


def _odd_in_kernel(*refs, rope, emit_cache):
    x_ref, mod_ref, g_ref, w_ref, qg_ref, wq_ref, kg_ref, wk_ref, wv_ref, cs_ref = refs[:10]
    refs = refs[10:]
    if rope:
        cos_ref, sa_ref, sb_ref = refs[:3]
        refs = refs[3:]
    y_ref, q_ref, k_ref, v_ref = refs[:4]
    refs = refs[4:]
    m = mod_ref[...]
    h = _norm_mod(x_ref[...], g_ref[...], m[1:2], m[0:1])
    z = _bdot(h.astype(BF16), w_ref[...])
    zc = z[:, 0:D_C]
    qc = z[:, D_C:D_C + Q_LORA]
    kvc = z[:, D_C + Q_LORA:D_C + Q_LORA + KV_LORA]
    kpe = z[:, D_C + Q_LORA + KV_LORA:]
    q = _bdot(_rms(qc, qg_ref[...]).astype(BF16), wq_ref[...])
    kvn = _rms(kvc, kg_ref[...])
    if emit_cache:
        ckv_ref, kpe_ref = refs
        ckv_ref[...] = kvn.reshape(ckv_ref.shape)
        kpe_ref[...] = kpe[:, ROPE_OFF:ROPE_OFF + QK_ROPE].reshape(kpe_ref.shape)
    if rope:
        tabs = (cos_ref[...], sa_ref[...], sb_ref[...])
        q = _apply_rope(q, *tabs, reps=N_HEADS)
        kpe = _apply_rope(kpe, *tabs, reps=1)
    kvb = kvn.astype(BF16)
    k = _bdot(kvb, wk_ref[...]) + jnp.concatenate([kpe] * N_HEADS, axis=1)
    scale = math.log2(math.e) / math.sqrt(QK_NOPE + QK_ROPE)
    q_ref[...] = (q * scale).astype(BF16)
    k_ref[...] = k.astype(BF16)
    v_ref[...] = _bdot(kvb, wv_ref[...]).astype(BF16)
    y = _bdot(zc.astype(BF16), cs_ref[...])
    y_ref[0, :, :] = y[:, 0:D_C].astype(BF16)
    y_ref[1, :, :] = y[:, D_C:2 * D_C].astype(BF16)


def _odd_in(x3, mod, g, w_in, q_g, w_q, kv_g, w_k, w_v, cs, stream, layer, rope_tabs, emit_cache):
    b, s, _ = x3.shape
    tm = min(s, 512)
    n_i = s // tm
    rope = rope_tabs is not None
    const = lambda a: pl.BlockSpec(a.shape, lambda bi, i: (0,) * a.ndim)
    in_specs = [
        pl.BlockSpec((None, tm, D_MODEL), lambda bi, i: (bi, i, 0)),
        _mod_spec(stream, layer),
        const(g), const(w_in), const(q_g), const(w_q), const(kv_g), const(w_k), const(w_v), const(cs),
    ]
    args = [x3, mod, g, w_in, q_g, w_q, kv_g, w_k, w_v, cs]
    if rope:
        in_specs += [pl.BlockSpec((tm, HEAD_PAD), lambda bi, i: (i, 0))] * 3
        args += list(rope_tabs)
    hq = N_HEADS * HEAD_PAD
    out_specs = [
        pl.BlockSpec((None, 2, tm, D_C), lambda bi, i: (bi, 0, i, 0)),
        pl.BlockSpec((None, tm, hq), lambda bi, i: (bi, i, 0)),
        pl.BlockSpec((None, tm, hq), lambda bi, i: (bi, i, 0)),
        pl.BlockSpec((None, tm, N_HEADS * V_DIM), lambda bi, i: (bi, i, 0)),
    ]
    out_shape = [
        jax.ShapeDtypeStruct((b, 2, s, D_C), BF16),
        jax.ShapeDtypeStruct((b, s, hq), BF16),
        jax.ShapeDtypeStruct((b, s, hq), BF16),
        jax.ShapeDtypeStruct((b, s, N_HEADS * V_DIM), BF16),
    ]
    if emit_cache:
        out_specs += [
            pl.BlockSpec((None, tm, KV_LORA), lambda bi, i: (bi, i, 0)),
            pl.BlockSpec((None, tm, QK_ROPE), lambda bi, i: (bi, i, 0)),
        ]
        out_shape += [
            jax.ShapeDtypeStruct((b, s, KV_LORA), F32),
            jax.ShapeDtypeStruct((b, s, QK_ROPE), F32),
        ]
    return pl.pallas_call(
        functools.partial(_odd_in_kernel, rope=rope, emit_cache=emit_cache),
        grid=(b, n_i),
        in_specs=in_specs,
        out_specs=out_specs,
        out_shape=out_shape,
        compiler_params=_cparams("parallel", "parallel"),
        name="odd_in",
    )(*args)


def _cache_kv_kernel(c_ref, p_ref, wk_ref, wv_ref, k_ref, v_ref):
    cb = c_ref[...].astype(BF16)
    k = _bdot(cb, wk_ref[...]) + jnp.concatenate([p_ref[...]] * N_HEADS, axis=1)
    k_ref[...] = k.astype(BF16)
    v_ref[...] = _bdot(cb, wv_ref[...]).astype(BF16)


def _cache_kv(ckv, kpe_blk, w_k, w_v):
    b, p, _ = ckv.shape
    hq = N_HEADS * HEAD_PAD
    return pl.pallas_call(
        _cache_kv_kernel,
        grid=(b,),
        in_specs=[
            pl.BlockSpec((None, p, KV_LORA), lambda bi: (bi, 0, 0)),
            pl.BlockSpec((None, p, HEAD_PAD), lambda bi: (bi, 0, 0)),
            pl.BlockSpec(w_k.shape, lambda bi: (0, 0)),
            pl.BlockSpec(w_v.shape, lambda bi: (0, 0)),
        ],
        out_specs=[
            pl.BlockSpec((None, p, hq), lambda bi: (bi, 0, 0)),
            pl.BlockSpec((None, p, N_HEADS * V_DIM), lambda bi: (bi, 0, 0)),
        ],
        out_shape=[
            jax.ShapeDtypeStruct((b, p, hq), BF16),
            jax.ShapeDtypeStruct((b, p, N_HEADS * V_DIM), BF16),
        ],
        compiler_params=_cparams("parallel"),
        name="cache_kv",
    )(ckv, kpe_blk, w_k, w_v)


def _odd_mix_kernel(*refs, with_cache, n_seq=1):
    q_ref, k_ref, v_ref = refs[:3]
    refs = refs[3:]
    if with_cache:
        kc_ref, vc_ref = refs[:2]
        refs = refs[2:]
    plan_tile = None
    if with_cache:
        y_ref, f_ref, x_ref, mod_ref, wo_ref = refs[:5]
        plan_in, (o_ref, *plan_out), (a_ref, vx_ref, vcx_ref) = refs[5:9], refs[9:14], refs[14:]
        plan_tile = pl.program_id(0) * pl.num_programs(1) + pl.program_id(1)

        @pl.when(pl.program_id(1) == 0)
        def _():
            for src, dst in ((v_ref, vx_ref), (vc_ref, vcx_ref)):
                one = lax.broadcasted_iota(jnp.int32, (src.shape[0], PAIR_W - DEN_COL), 1) == 0
                for pair in range(N_HEADS // 2):
                    dst[:, pair * PAIR_W:pair * PAIR_W + DEN_COL] = src[:, pair * DEN_COL:(pair + 1) * DEN_COL]
                    dst[:, pair * PAIR_W + DEN_COL:(pair + 1) * PAIR_W] = jnp.where(one, 1.0, 0.0).astype(BF16)
    else:
        y_ref, f_ref, x_ref, mod_ref, wo_ref, o_ref, a_ref = refs[:7]
        if len(refs) > 7:
            plan_in, plan_out, plan_tile = refs[7:]
    tq = q_ref.shape[0] // n_seq
    lane = lax.broadcasted_iota(jnp.int32, (tq, 2 * V_DIM), 1)
    f_parts = []
    for sq in range(n_seq):
        rows = slice(sq * tq, (sq + 1) * tq)
        for pair in range(N_HEADS // 2):
            vcols = slice(pair * PAIR_W, (pair + 1) * PAIR_W)
            outs = []
            for h in (2 * pair, 2 * pair + 1):
                hcols = slice(h * HEAD_PAD, (h + 1) * HEAD_PAD)
                qh = q_ref[rows, hcols]
                if with_cache:
                    s = lax.dot_general(qh, k_ref[:, hcols], NT_DIMS, preferred_element_type=F32)
                    sc = lax.dot_general(qh, kc_ref[:, hcols], NT_DIMS, preferred_element_type=F32)
                    top = jnp.maximum(jnp.max(s, axis=-1, keepdims=True), jnp.max(sc, axis=-1, keepdims=True))
                    acc = _bdot(jnp.exp2((s - top).astype(BF16)), vx_ref[:, vcols])
                    acc = acc + _bdot(jnp.exp2((sc - top).astype(BF16)), vcx_ref[:, vcols])
                    outs.append(acc[:, 0:2 * V_DIM] / acc[:, DEN_COL:DEN_COL + 1])
                else:
                    s = lax.dot_general(qh, k_ref[rows, hcols], NT_DIMS, preferred_element_type=F32)
                    p = jnp.exp2(s - jnp.max(s, axis=-1, keepdims=True))
                    den = jnp.sum(p, axis=-1, keepdims=True)
                    outs.append(_bdot(p.astype(BF16), v_ref[rows, pair * DEN_COL:(pair + 1) * DEN_COL]) / den)
            a_ref[rows, pair * 2 * V_DIM:(pair + 1) * 2 * V_DIM] = (
                jnp.where(lane < V_DIM, outs[0], outs[1]).astype(BF16))
        if n_seq > 1:
            f_parts.append(_bdot(f_ref[:, 0:tq], y_ref[0, rows, :]) + _bdot(f_ref[:, tq:], y_ref[1, rows, :]))
    f = jnp.concatenate(f_parts, axis=0) if n_seq > 1 else _bdot(f_ref[...], y_ref[...].reshape(-1, D_C))
    o = _bdot(f.astype(BF16), wo_ref[0:D_C, :]) + _bdot(a_ref[...], wo_ref[D_C:, :])
    x1 = x_ref[...] + mod_ref[2:3, :] * o
    o_ref[...] = x1
    if plan_tile is not None:
        _route_plan(x1, mod_ref[...], plan_in, plan_out, plan_tile)


WHOLE_SEQS = 2


def _odd_whole_kernel(tab_ref, x_ref, mod_ref, g_ref, w_ref, qg_ref, wq_ref, kg_ref, wk_ref, wv_ref, cs_ref, f_ref,
                      wo_ref, modp_ref, dloc_ref, ys_ref, pg_ref, pw_ref, pb_ref, pu_ref,
                      o_ref, ckv_ref, kpe_ref, ph_ref, pd_ref, pr_ref, pc_ref,
                      y_ref, q_ref, k_ref, v_ref, a_ref, x2_ref, runs_ref, sem_ref):
    bi = pl.program_id(0)
    par = lax.rem(bi, 2)
    tm = ROUTE_TM

    def fetch(step, parity):
        for sq in range(WHOLE_SEQS):
            _fetch_runs(tab_ref, ys_ref, runs_ref, sem_ref, step * WHOLE_SEQS + sq, parity * WHOLE_SEQS + sq)

    @pl.when(bi == 0)
    def _():
        fetch(0, 0)

    @pl.when(bi + 1 < pl.num_programs(0))
    def _():
        fetch(bi + 1, 1 - par)

    for sq in range(WHOLE_SEQS):
        rows = slice(sq * tm, (sq + 1) * tm)
        moe = _unsort(tab_ref, ys_ref, runs_ref, sem_ref, dloc_ref[rows, 0:1], bi * WHOLE_SEQS + sq,
                      par * WHOLE_SEQS + sq)
        x2_ref[rows, :] = x_ref[rows, :] + modp_ref[5:6, :] * moe
    _odd_in_kernel(x2_ref, mod_ref, g_ref, w_ref, qg_ref, wq_ref, kg_ref, wk_ref, wv_ref, cs_ref,
                   y_ref, q_ref, k_ref, v_ref, ckv_ref, kpe_ref, rope=False, emit_cache=True)
    _odd_mix_kernel(q_ref, k_ref, v_ref, y_ref, f_ref, x2_ref, mod_ref, wo_ref, o_ref, a_ref,
                    (pg_ref, pw_ref, pb_ref, pu_ref), (ph_ref, pd_ref, pr_ref, pc_ref), bi * WHOLE_SEQS,
                    with_cache=False, n_seq=WHOLE_SEQS)


def _odd_whole(x3, pending, mod, g, w_in, q_g, w_q, kv_g, w_k, w_v, cs, fmat, w_out, plan_consts, stream, layer):
    tab, dloc, ys = pending
    b, seq, _ = x3.shape
    assert seq == ROUTE_TM and b % WHOLE_SEQS == 0 and stream.shared_cond
    b, s = b // WHOLE_SEQS, seq * WHOLE_SEQS
    x3 = x3.reshape(b, s, D_MODEL)
    hq = N_HEADS * HEAD_PAD
    hv = N_HEADS * V_DIM
    const = lambda a: pl.BlockSpec(a.shape, lambda bi, tab_ref: (0,) * a.ndim)
    row_block = lambda w: pl.BlockSpec((None, s, w), lambda bi, tab_ref: (bi, 0, 0))
    mod_block = lambda lyr: pl.BlockSpec((None, None, 6, D_MODEL), lambda bi, tab_ref: (lyr, 0, 0, 0))
    cache_block = lambda w: pl.BlockSpec((WHOLE_SEQS, 1, seq, w), lambda bi, tab_ref: (bi, 0, 0, 0))
    plan_shapes, plan_specs = _plan_out(b, s, s, lambda bi, tab_ref: (bi, 0))
    grid_spec = pltpu.PrefetchScalarGridSpec(
        num_scalar_prefetch=1,
        grid=(b,),
        in_specs=[
            row_block(D_MODEL), mod_block(layer),
            const(g), const(w_in), const(q_g), const(w_q), const(kv_g), const(w_k), const(w_v), const(cs),
            const(fmat), const(w_out),
            mod_block(layer - 1),
            pl.BlockSpec((s, LANES), lambda bi, tab_ref: (bi, 0)),
            pl.BlockSpec(memory_space=pl.ANY),
        ] + [const(a) for a in plan_consts],
        out_specs=[row_block(D_MODEL), cache_block(KV_LORA), cache_block(QK_ROPE)] + plan_specs,
        scratch_shapes=[
            pltpu.VMEM((2, s, D_C), BF16),
            pltpu.VMEM((s, hq), BF16),
            pltpu.VMEM((s, hq), BF16),
            pltpu.VMEM((s, hv), BF16),
            pltpu.VMEM((s, hv), BF16),
            pltpu.VMEM((s, D_MODEL), F32),
            pltpu.VMEM((2 * WHOLE_SEQS, RUN_ROWS, D_MODEL), F32),
            pltpu.SemaphoreType.DMA((2 * WHOLE_SEQS,)),
        ],
    )
    out = pl.pallas_call(
        _odd_whole_kernel,
        grid_spec=grid_spec,
        out_shape=[
            jax.ShapeDtypeStruct((b, s, D_MODEL), F32),
            jax.ShapeDtypeStruct((b * WHOLE_SEQS, 1, seq, KV_LORA), F32),
            jax.ShapeDtypeStruct((b * WHOLE_SEQS, 1, seq, QK_ROPE), F32),
        ] + plan_shapes,
        compiler_params=_cparams("arbitrary"),
        name="odd_whole",
    )(tab, x3, mod, g, w_in, q_g, w_q, kv_g, w_k, w_v, cs, fmat, w_out, mod, dloc, ys, *plan_consts)
    return out[0].reshape(b * WHOLE_SEQS, seq, D_MODEL), out[1], out[2], out[3:]


def _odd_mix(q, k, v, kc, vc, y, fmat, x3, mod, w_out, plan_consts, stream, layer):
    b, s, hq = q.shape
    tq = min(s, 256)
    n_i = s // tq
    with_cache = kc is not None
    hv = N_HEADS * V_DIM
    mode = dict(pipeline_mode=pl.Buffered(1)) if with_cache else {}

    def per_batch(rows, cols):
        return pl.BlockSpec((None, rows, cols), lambda bi, i: (bi, 0, 0), **mode)

    in_specs = [pl.BlockSpec((None, tq, hq), lambda bi, i: (bi, i, 0)), per_batch(s, hq), per_batch(s, hv)]
    args = [q, k, v]
    if with_cache:
        p = kc.shape[1]
        in_specs += [per_batch(p, hq), per_batch(p, hv)]
        args += [kc, vc]
    in_specs += [
        per_batch(2 * s, D_C),
        pl.BlockSpec((tq, 2 * s), lambda bi, i: (i, 0)),
        pl.BlockSpec((None, tq, D_MODEL), lambda bi, i: (bi, i, 0)),
        _mod_spec(stream, layer),
        pl.BlockSpec(w_out.shape, lambda bi, i: (0, 0), **mode),
    ]
    in_specs += [pl.BlockSpec(a.shape, lambda bi, i: (0,) * a.ndim) for a in plan_consts]
    args += [y, fmat, x3, mod, w_out] + plan_consts
    plan_shapes, plan_specs = _plan_out(b, s, tq, lambda bi, i: (bi, i))
    out = pl.pallas_call(
        functools.partial(_odd_mix_kernel, with_cache=with_cache),
        grid=(b, n_i),
        in_specs=in_specs,
        out_specs=[pl.BlockSpec((None, tq, D_MODEL), lambda bi, i: (bi, i, 0))] + plan_specs,
        out_shape=[jax.ShapeDtypeStruct((b, s, D_MODEL), F32)] + plan_shapes,
        scratch_shapes=[pltpu.VMEM((tq, hv), BF16)] + (
            [pltpu.VMEM((s, VX_W), BF16), pltpu.VMEM((kc.shape[1], VX_W), BF16)] if with_cache else []),
        compiler_params=_cparams("arbitrary", "arbitrary"),
        name="odd_mix",
    )(*args)
    return out[0], out[1:]


def _dft_tables(seq):
    jc = np.arange(C_GW)
    ang_c = 2.0 * np.pi * np.outer(jc, jc) / C_GW
    eye = np.eye(C_GROUPS)
    cs = np.concatenate([np.kron(eye, np.cos(ang_c)), np.kron(eye, np.sin(ang_c))], axis=1)
    jn = np.arange(seq)
    ang_n = 2.0 * np.pi * (np.outer(jn, jn) % seq) / seq
    scale = 1.0 / math.sqrt(seq * C_GW)
    fmat = np.concatenate([np.cos(ang_n), -np.sin(ang_n)], axis=1) * scale
    return jnp.asarray(cs, F32).astype(BF16), jnp.asarray(fmat, F32).astype(BF16)


def _odd_weights(w_in, w_uq, w_ukv):
    d = w_in.shape[0]
    base = D_C + Q_LORA + KV_LORA
    kpe_blk = jnp.zeros((d, HEAD_PAD), w_in.dtype).at[:, ROPE_OFF:ROPE_OFF + QK_ROPE].set(w_in[:, base:])
    w_in_p = jnp.concatenate([w_in[:, :base], kpe_blk], axis=1).astype(BF16)
    qh = w_uq.reshape(Q_LORA, N_HEADS, QK_NOPE + QK_ROPE)
    w_q = jnp.pad(qh, ((0, 0), (0, 0), (0, HEAD_PAD - QK_NOPE - QK_ROPE))).reshape(Q_LORA, -1).astype(BF16)
    kvh = w_ukv.reshape(KV_LORA, N_HEADS, QK_NOPE + V_DIM)
    w_k = jnp.pad(kvh[:, :, :QK_NOPE], ((0, 0), (0, 0), (0, HEAD_PAD - QK_NOPE))).reshape(KV_LORA, -1)
    w_v = kvh[:, :, QK_NOPE:].reshape(KV_LORA, -1)
    return w_in_p, w_q, w_k.astype(BF16), w_v.astype(BF16)


ROUTER_ROWS = 32


def _router_weights(wg, bg, we, be):
    d = wg.shape[0]
    w = jnp.concatenate([wg, we.reshape(d, N_EXPERTS)], axis=1).T
    w = jnp.pad(w, ((0, ROUTER_ROWS - w.shape[0]), (0, 0))).astype(BF16)
    b = jnp.concatenate([bg, be.reshape(N_EXPERTS)])
    b = jnp.pad(b, (0, ROUTER_ROWS - b.shape[0])).reshape(ROUTER_ROWS, 1).astype(F32)
    return w, b


def kernel(x_prompt, x_sample, cache_ckv, cache_kpe, c, c_ctx, mod_w, mod_b, norm1_g, norm2_g,
           ev_w_in, ev_conv_w, ev_sgu_norm_g, ev_sgu_w, ev_sgu_b, ev_w_out,
           od_w_in, od_q_norm_g, od_w_uq, od_kv_norm_g, od_w_ukv, od_w_out,
           moe_wg, moe_bg, moe_we, moe_be, moe_w1, moe_w3, moe_w2, final_norm_g):
    bp, n_p, d = x_prompt.shape
    bs, n_s, _ = x_sample.shape
    streams = [(_Stream(bp, n_p, True), x_prompt), (_Stream(bs, n_s, False), x_sample)]

    n_rows = 1 + bs
    mod = _adaln(jnp.concatenate([c_ctx[None, :], c], axis=0), mod_w, mod_b, n_rows)

    final_g = final_norm_g.reshape(1, d)
    xs = [x for _, x in streams]
    new_ckv, new_kpe = [], []
    pending = None
    plans = [None, None]
    for l in range(DEPTH):
        j = l // 2
        g1 = norm1_g[l].reshape(1, d)
        g2 = norm2_g[l].reshape(1, d)
        plan_consts = _plan_consts(g2, *_router_weights(moe_wg[l], moe_bg[l], moe_we[l], moe_be[l]))
        last = l == DEPTH - 1
        if l % 2 == 0:
            w_in = ev_w_in[j].astype(BF16)
            w_out = ev_w_out[j].astype(BF16)
            sgu_w = ev_sgu_w[j].astype(BF16)
            sgu_g = ev_sgu_norm_g[j].reshape(1, D_B)
            sgu_bias = jnp.repeat(ev_sgu_b[j].T, D_B // B_GROUPS, axis=1)
            for si, (st, _) in enumerate(streams):
                xs[si], plans[si] = _even_layer(xs[si], mod, g1, w_in, ev_conv_w[j], sgu_g, sgu_w, sgu_bias, w_out,
                                                plan_consts, st, l)
        else:
            w_in, w_q, w_k, w_v = _odd_weights(od_w_in[j], od_w_uq[j], od_w_ukv[j])
            w_out = od_w_out[j].astype(BF16)
            q_g = od_q_norm_g[j].reshape(1, Q_LORA)
            kv_g = od_kv_norm_g[j].reshape(1, KV_LORA)
            for si, (st, _) in enumerate(streams):
                x3 = xs[si]
                cs, fmat = _dft_tables(st.seq)
                if st.shared_cond:
                    xs[si], ckv, kpe, plans[si] = _odd_whole(x3, pending, mod, g1, w_in, q_g, w_q, kv_g, w_k, w_v, cs,
                                                             fmat, w_out, plan_consts, st, l)
                    new_ckv.append(ckv)
                    new_kpe.append(kpe)
                    continue
                y, q, k, v = _odd_in(x3, mod, g1, w_in, q_g, w_q, kv_g, w_k, w_v, cs, st, l,
                                     _rope_tables(st.seq), False)
                kpe_blk = jnp.pad(cache_kpe[:, j], ((0, 0), (0, 0), (ROPE_OFF, HEAD_PAD - ROPE_OFF - QK_ROPE)))
                kc, vc = _cache_kv(cache_ckv[:, j], kpe_blk, w_k, w_v)
                xs[si], plans[si] = _odd_mix(q, k, v, kc, vc, y.reshape(st.batch, 2 * st.seq, D_C), fmat, x3, mod,
                                             w_out, plan_consts, st, l)
        defer = not last and (l + 1) % 2 == 1
        x2p, x2s, pending = _moe(xs[0].reshape(bp * n_p, d), xs[1].reshape(bs * n_s, d), plans[0], plans[1], mod,
                                 moe_w1, moe_w3, moe_w2, final_g, l, n_s, last, defer)
        xs = [xs[0] if defer else x2p.reshape(bp, n_p, d), x2s.reshape(bs, n_s, d)]
    return (xs[0], xs[1], jnp.concatenate(new_ckv, axis=1), jnp.concatenate(new_kpe, axis=1))
```

```python
import functools
import math

import numpy as np
import jax
import jax.numpy as jnp
from jax import lax
from jax.experimental import pallas as pl
from jax.experimental.pallas import tpu as pltpu

D_MODEL = 1024
DEPTH = 2
GRID_W = 64
D_A = D_MODEL // 2
D_B = D_MODEL // 2
B_GROUPS = 4
CHUNK = 128
D_EVEN_IN = 3 * D_A + 2 * D_B
D_C = D_MODEL // 4
C_GROUPS = 4
C_GW = D_C // C_GROUPS
N_HEADS = 12
QK_NOPE = 64
QK_ROPE = 32
V_DIM = 64
Q_LORA = 384
KV_LORA = 256
ROPE_BASE = 10000.0
N_GROUPS_MOE = 4
EXPERTS_PER_GROUP = 4
N_EXPERTS = N_GROUPS_MOE * EXPERTS_PER_GROUP
D_EXPERT = 256
EPS = 1e-6

LANES = 128
HEAD_PAD = 128
PAIR_W = 256
DEN_COL = 2 * 64
VX_W = 6 * PAIR_W
ROPE_OFF = QK_NOPE
GATE_OFF = N_GROUPS_MOE
NEG_BIG = -1e30
F32 = jnp.float32
BF16 = jnp.bfloat16
VMEM_LIMIT = 56 * 1024 * 1024


def _cparams(*sem):
    return pltpu.CompilerParams(dimension_semantics=sem, vmem_limit_bytes=VMEM_LIMIT)


def _rms(x, g):
    return x * lax.rsqrt(jnp.mean(x * x, axis=-1, keepdims=True) + EPS) * g


def _norm_mod(x, g, scale, shift):
    rs = lax.rsqrt(jnp.mean(x * x, axis=-1, keepdims=True) + EPS)
    return x * rs * (g * (1.0 + scale)) + shift


def _bdot(a, b):
    return jnp.dot(a, b, preferred_element_type=F32)


NT_DIMS = (((1,), (1,)), ((), ()))


MOD_ROWS = 8


def _mod_kernel(c_ref, w_ref, b_ref, o_ref):
    c = c_ref[...]
    s = c * jax.nn.sigmoid(c)
    s_hi = s.astype(BF16).astype(F32)
    pair = jnp.concatenate([s_hi, s - s_hi], axis=0).astype(BF16)
    two = _bdot(pair, w_ref[...].astype(BF16))
    o_ref[...] = two[0:MOD_ROWS] + two[MOD_ROWS:] + b_ref[...]


def _adaln(cond, mod_w, mod_b, n_rows):
    nt = 2048
    d6 = mod_w.shape[-1]
    cond = jnp.pad(cond, ((0, MOD_ROWS - n_rows), (0, 0)))
    out = pl.pallas_call(
        _mod_kernel,
        grid=(DEPTH, d6 // nt),
        in_specs=[
            pl.BlockSpec(cond.shape, lambda l, n: (0, 0)),
            pl.BlockSpec((None, D_MODEL, nt), lambda l, n: (l, 0, n)),
            pl.BlockSpec((None, 1, nt), lambda l, n: (l, 0, n)),
        ],
        out_specs=pl.BlockSpec((None, MOD_ROWS, nt), lambda l, n: (l, 0, n)),
        out_shape=jax.ShapeDtypeStruct((DEPTH, MOD_ROWS, d6), F32),
        compiler_params=_cparams("parallel", "parallel"),
        name="adaln",
    )(cond, mod_w, mod_b.reshape(DEPTH, 1, d6))
    return out[:, :n_rows].reshape(DEPTH, n_rows, 6, D_MODEL)


class _Stream:
    def __init__(self, batch, seq, shared_cond):
        self.batch = batch
        self.seq = seq
        self.shared_cond = shared_cond

    def row_of_batch(self, b):
        return 0 if self.shared_cond else b + 1


def _mod_spec(stream, layer):
    return pl.BlockSpec((None, None, 6, D_MODEL), lambda b, i: (layer, stream.row_of_batch(b), 0, 0))


HALO = 8
EVEN_SEQS = 4


def _even_kernel(*refs, has_halo, seq_rows, nc):
    if has_halo:
        x_ref, xp_ref, xn_ref = refs[:3]
        refs = refs[3:]
    else:
        x_ref = refs[0]
        refs = refs[1:]
    mod_ref, g_ref, wi_ref, cw_ref, sg_ref, sw_ref, sb_ref, wo_ref = refs[:8]
    plan_in, (o_ref, *plan_out), (z_ref, y_ref) = refs[8:12], refs[12:17], refs[17:]
    i = pl.program_id(1)
    n_i = pl.num_programs(1)
    ts = x_ref.shape[0]
    m = mod_ref[...]
    g = g_ref[...]

    def modulate(x):
        return _norm_mod(x, g, m[1:2], m[0:1]).astype(BF16)

    x = x_ref[...]
    hb = modulate(x)
    for n in range(D_EVEN_IN // nc):
        z_ref[:, n * nc:(n + 1) * nc] = _bdot(hb, wi_ref[:, n * nc:(n + 1) * nc]).astype(BF16)

    gate_b = z_ref[:, 0:D_A].astype(F32)
    gate_c = z_ref[:, D_A:2 * D_A].astype(F32)
    xa = z_ref[:, 2 * D_A:3 * D_A].astype(F32)
    t = gate_c * xa
    t_prev = pltpu.roll(t, 1, axis=0)
    t_next = pltpu.roll(t, ts - 1, axis=0)
    row = lax.broadcasted_iota(jnp.int32, (ts, 1), 0) & (seq_rows - 1)
    if has_halo:
        hh = modulate(jnp.concatenate([xp_ref[...], xn_ref[...]], axis=0))
        zh = _bdot(hh, wi_ref[:, D_A:3 * D_A]).astype(BF16).astype(F32)
        th = zh[:, 0:D_A] * zh[:, D_A:2 * D_A]
        tp = th[HALO - 1:HALO] * (i > 0).astype(F32)
        tn = th[HALO:HALO + 1] * (i < n_i - 1).astype(F32)
    else:
        tp = tn = 0.0
    t_prev = jnp.where(row == 0, tp, t_prev)
    t_next = jnp.where(row == seq_rows - 1, tn, t_next)
    cw = cw_ref[...]
    y_a = gate_b * (t_prev * cw[0:1] + t * cw[1:2] + t_next * cw[2:3])
    y_ref[:, 0:D_A] = y_a.astype(BF16)

    u = z_ref[:, 3 * D_A:3 * D_A + D_B].astype(F32)
    v = z_ref[:, 3 * D_A + D_B:3 * D_A + 2 * D_B].astype(F32)
    vb = _rms(v, sg_ref[...]).astype(BF16)
    gw = D_B // B_GROUPS
    for c in range(ts // CHUNK):
        rows = slice(c * CHUNK, (c + 1) * CHUNK)
        for gi in range(B_GROUPS):
            cols = slice(gi * gw, (gi + 1) * gw)
            sv = _bdot(sw_ref[gi], vb[rows, cols]) + sb_ref[:, cols]
            y_ref[rows, D_A + gi * gw:D_A + (gi + 1) * gw] = (u[rows, cols] * sv).astype(BF16)

    x1 = x + m[2:3] * _bdot(y_ref[...], wo_ref[...])
    o_ref[...] = x1
    _route_plan(x1, m, plan_in, plan_out, (pl.program_id(0) * n_i + i) * (ts // ROUTE_TM))


def _even_layer(x3, mod, g, w_in, conv_w, sgu_g, sgu_w, sgu_bias, w_out, plan_consts, stream, layer):
    out_shape = x3.shape
    b, s, _ = x3.shape
    ts = min(s, 256) if stream.shared_cond else min(s, 512)
    n_i = s // ts
    has_halo = n_i > 1
    seq_rows = ts
    if not has_halo and stream.shared_cond and b % EVEN_SEQS == 0:
        b, s, ts = b // EVEN_SEQS, EVEN_SEQS * s, EVEN_SEQS * ts
        x3 = x3.reshape(b, s, D_MODEL)
    hb = ts // HALO
    last_h = s // HALO - 1
    const = lambda a: pl.BlockSpec(a.shape, lambda bi, i: (0,) * a.ndim)
    in_specs = [pl.BlockSpec((None, ts, D_MODEL), lambda bi, i: (bi, i, 0))]
    args = [x3]
    if has_halo:
        in_specs += [
            pl.BlockSpec((None, HALO, D_MODEL), lambda bi, i: (bi, jnp.maximum(i * hb - 1, 0), 0)),
            pl.BlockSpec((None, HALO, D_MODEL), lambda bi, i: (bi, jnp.minimum((i + 1) * hb, last_h), 0)),
        ]
        args += [x3, x3]
    in_specs += [_mod_spec(stream, layer), const(g), const(w_in), const(conv_w), const(sgu_g), const(sgu_w),
                 const(sgu_bias), const(w_out)] + [const(a) for a in plan_consts]
    args += [mod, g, w_in, conv_w, sgu_g, sgu_w, sgu_bias, w_out] + plan_consts
    plan_shapes, plan_specs = _plan_out(b, s, ts, lambda bi, i: (bi, i))
    out = pl.pallas_call(
        functools.partial(_even_kernel, has_halo=has_halo, seq_rows=seq_rows, nc=512),
        grid=(b, n_i),
        in_specs=in_specs,
        out_specs=[pl.BlockSpec((None, ts, D_MODEL), lambda bi, i: (bi, i, 0))] + plan_specs,
        out_shape=[jax.ShapeDtypeStruct((b, s, D_MODEL), F32)] + plan_shapes,
        scratch_shapes=[pltpu.VMEM((ts, D_EVEN_IN), BF16), pltpu.VMEM((ts, D_A + D_B), BF16)],
        compiler_params=_cparams("arbitrary", "arbitrary"),
        name="even_layer",
    )(*args)
    return out[0].reshape(out_shape), out[1:]


ROUTE_TM = 256
ROUTE_SUBS = 4
ROUTE_STEP = ROUTE_TM * ROUTE_SUBS
ROUTE_PAD = 8
SORT_ROWS = ROUTE_TM + LANES
RUN_ROWS = SORT_ROWS + 32
ROUTE_ROWS = ROUTE_TM + 4 * ROUTE_PAD
XS_W = D_MODEL + LANES
GATE_LO = EXPERTS_PER_GROUP
DLOC_HI = 2 * EXPERTS_PER_GROUP
DLOC_RADIX = 16.0
FFN_BM = 512
FFN_HALF = FFN_BM // 2
RUN_SIZES = (256, 128, 64, 32, 16, 8)
TAB_W = 2 * N_GROUPS_MOE


def _round_up(x, m):
    return lax.div(x + (m - 1), m) * m


def _run_copies(tab_ref, tile, hbm_ref, vmem_ref, sem, to_hbm, wait):
    off = 0
    for g in range(N_GROUPS_MOE):
        start = tab_ref[tile * TAB_W + g]
        n = tab_ref[tile * TAB_W + N_GROUPS_MOE + g]
        for p in RUN_SIZES:
            done = n & (-2 * p)

            @pl.when((n & p) != 0)
            def _():
                v = vmem_ref.at[pl.ds(pl.multiple_of(off + done, ROUTE_PAD), p)]
                h = hbm_ref.at[pl.ds(pl.multiple_of(start + done, ROUTE_PAD), p)]
                cp = pltpu.make_async_copy(v, h, sem) if to_hbm else pltpu.make_async_copy(h, v, sem)
                if wait:
                    cp.wait()
                else:
                    cp.start()
        off = off + n


def _zero_fill(tab_ref, meta, zeros_ref, hbm_ref, sem, n_rows, wait):
    def copy(rows, dst_row):
        cp = pltpu.make_async_copy(zeros_ref.at[pl.ds(0, rows)],
                                   hbm_ref.at[pl.ds(pl.multiple_of(dst_row, ROUTE_PAD), rows)], sem)
        if wait:
            cp.wait()
        else:
            cp.start()

    end = 0
    for g in range(N_GROUPS_MOE):
        fill = tab_ref[meta + g]
        start = tab_ref[meta + N_GROUPS_MOE + g]
        end = start + _round_up(fill, FFN_BM)
        tail = end - start - fill
        for p in RUN_SIZES:
            pl.when((tail & p) != 0)(functools.partial(copy, p, start + fill + (tail & (-2 * p))))
    for k in range(n_rows // FFN_BM):
        pl.when(end + k * FFN_BM < n_rows)(functools.partial(copy, FFN_BM, end + k * FFN_BM))


def _wait_rows(n, hbm_ref, vmem_ref, sem, to_hbm):
    for p in RUN_SIZES:
        @pl.when((n & p) != 0)
        def _():
            v = vmem_ref.at[pl.ds(0, p)]
            h = hbm_ref.at[pl.ds(0, p)]
            (pltpu.make_async_copy(v, h, sem) if to_hbm else pltpu.make_async_copy(h, v, sem)).wait()


def _tile_rows(tab_ref, tile):
    n = 0
    for g in range(N_GROUPS_MOE):
        n = n + tab_ref[tile * TAB_W + N_GROUPS_MOE + g]
    return n


def _max4(v):
    return jnp.maximum(jnp.maximum(v[0], v[1]), jnp.maximum(v[2], v[3]))


def _first_of4(v, top):
    return jnp.where(v[0] == top, 0.0, jnp.where(v[1] == top, 1.0, jnp.where(v[2] == top, 2.0, 3.0)))


def _route_plan(x, m, plan_in, plan_out, tile0):
    g_ref, wrt_ref, brt_ref, upper_ref = plan_in
    haug_ref, dloc_ref, drow_ref, cnt_ref = plan_out
    rows = x.shape[0]
    tm = ROUTE_TM
    ng = N_GROUPS_MOE
    h = _norm_mod(x, g_ref[...], m[4:5], m[3:4])
    hb = h.astype(BF16)
    lt = lax.dot_general(wrt_ref[...], hb, NT_DIMS, preferred_element_type=F32) + brt_ref[...]
    gl = [lt[r:r + 1, :] for r in range(ng)]
    g_top = _max4(gl)
    g_idx = _first_of4(gl, g_top)
    g_w = 1.0 / (jnp.exp(gl[0] - g_top) + jnp.exp(gl[1] - g_top) + jnp.exp(gl[2] - g_top) + jnp.exp(gl[3] - g_top))
    ev = []
    for k in range(EXPERTS_PER_GROUP):
        cand = [lt[GATE_OFF + EXPERTS_PER_GROUP * r + k:GATE_OFF + EXPERTS_PER_GROUP * r + k + 1, :]
                for r in range(ng)]
        ev.append(jnp.where(g_idx == 0.0, cand[0], jnp.where(g_idx == 1.0, cand[1],
                            jnp.where(g_idx == 2.0, cand[2], cand[3]))))
    v1 = _max4(ev)
    i1 = _first_of4(ev, v1)
    rest = [jnp.where(i1 == float(k), NEG_BIG, ev[k]) for k in range(EXPERTS_PER_GROUP)]
    v2 = _max4(rest)
    i2 = _first_of4(rest, v2)
    e2 = jnp.exp(v2 - v1)
    w1 = 1.0 / (1.0 + e2)
    w2 = e2 * w1
    gates = [g_w * (jnp.where(i1 == float(k), w1, 0.0) + jnp.where(i2 == float(k), w2, 0.0))
             for k in range(EXPERTS_PER_GROUP)]

    sub8 = lax.broadcasted_iota(jnp.int32, (8, tm), 0).astype(F32)
    dlocs = []
    for sub in range(rows // tm):
        gi = g_idx[:, sub * tm:(sub + 1) * tm]
        hot = jnp.where(sub8 == gi, 1.0, 0.0)
        before = _bdot(hot.astype(BF16), upper_ref[...])
        dl = jnp.sum(before * hot, axis=0, keepdims=True)
        off = 0
        for g in range(ng):
            n_g = _round_up(jnp.sum(hot[g:g + 1, :]).astype(jnp.int32), ROUTE_PAD)
            cnt_ref[(tile0 + sub) * ng + g] = n_g
            dl = dl + jnp.where(gi == float(g), off.astype(F32) if g else 0.0, 0.0)
            off = off + n_g
        drow_ref[sub] = jnp.broadcast_to(dl, (8, tm))
        dlocs.append(dl)
    dloc = jnp.concatenate(dlocs, axis=1)
    d_hi = jnp.floor(dloc * (1.0 / DLOC_RADIX))
    g_hi = [gt.astype(BF16).astype(F32) for gt in gates]
    ex_rows = g_hi + [gt - gh for gt, gh in zip(gates, g_hi)] + [d_hi, dloc - DLOC_RADIX * d_hi]
    sub16 = lax.broadcasted_iota(jnp.int32, (16, rows), 0)
    ex_t = jnp.zeros((16, rows), F32)
    for r, row in enumerate(ex_rows):
        ex_t = jnp.where(sub16 == r, row, ex_t)
    ex_t = jnp.concatenate([ex_t, jnp.zeros((LANES - 16, rows), F32)], axis=0)
    extras = ex_t.T
    dloc_ref[...] = jnp.broadcast_to(
        DLOC_RADIX * extras[:, DLOC_HI:DLOC_HI + 1] + extras[:, DLOC_HI + 1:DLOC_HI + 2], (rows, LANES))
    haug_ref[...] = jnp.concatenate([hb, extras.astype(BF16)], axis=1)


def _plan_consts(g2, w_rt, b_rt):
    upper = jnp.asarray(np.triu(np.ones((ROUTE_TM, ROUTE_TM), np.float32), 1), BF16)
    return [g2, w_rt, b_rt, upper]


def _plan_out(b, s, rows, index):
    tiles = rows // ROUTE_TM
    shapes = [jax.ShapeDtypeStruct((b, s, XS_W), BF16), jax.ShapeDtypeStruct((b, s, LANES), F32),
              jax.ShapeDtypeStruct((b, s // ROUTE_TM, 8, ROUTE_TM), F32),
              jax.ShapeDtypeStruct((b * (s // ROUTE_TM) * N_GROUPS_MOE,), jnp.int32)]
    specs = [pl.BlockSpec((None, rows, XS_W), lambda *idx: (*index(*idx), 0)),
             pl.BlockSpec((None, rows, LANES), lambda *idx: (*index(*idx), 0)),
             pl.BlockSpec((None, tiles, 8, ROUTE_TM), lambda *idx: (*index(*idx), 0, 0)),
             pl.BlockSpec(memory_space=pltpu.SMEM)]
    return shapes, specs


def _flat_plan(plan):
    haug, dloc, drow, cnt = plan
    return (haug.reshape(-1, XS_W), dloc.reshape(-1, LANES), drow.reshape(-1, 8, ROUTE_TM), cnt)


def _dispatch_kernel(cp_ref, cs_ref, hp_ref, hs_ref, dp_ref, ds_ref, tab_ref, sorted_hbm,
                     sorted_ref, zeros_ref, fill_ref, sem_ref, zsem_ref, *, n_steps, n_p_steps):
    i = pl.program_id(0)
    tm = ROUTE_TM
    ng = N_GROUPS_MOE
    n_tiles = n_steps * ROUTE_SUBS
    n_p_tiles = n_p_steps * ROUTE_SUBS
    meta = n_tiles * TAB_W
    n_rows = sorted_hbm.shape[0]

    def count(tile, g):
        if isinstance(tile, int):
            return cp_ref[tile * ng + g] if tile < n_p_tiles else cs_ref[(tile - n_p_tiles) * ng + g]
        return jnp.where(tile < n_p_tiles, cp_ref[jnp.minimum(tile, n_p_tiles - 1) * ng + g],
                         cs_ref[jnp.maximum(tile - n_p_tiles, 0) * ng + g])

    @pl.when(i == 0)
    def _():
        start = 0
        for g in range(ng):
            fill = sum(count(t, g) for t in range(n_tiles))
            tab_ref[meta + g] = fill
            tab_ref[meta + ng + g] = start
            start = start + _round_up(fill, FFN_BM)
            fill_ref[g] = 0
        zeros_ref[...] = jnp.zeros_like(zeros_ref)

    is_prompt = i < n_p_steps
    row_f = lax.broadcasted_iota(jnp.int32, (ROUTE_ROWS, tm), 0).astype(F32)

    def sort_tile(h_ref, d_ref, sub):
        onehot = jnp.where(row_f == d_ref[sub][0:1, :], 1.0, 0.0).astype(BF16)
        sorted_ref[sub] = _bdot(onehot, h_ref[sub * tm:(sub + 1) * tm, :])

    for sub in range(ROUTE_SUBS):
        tile = i * ROUTE_SUBS + sub

        @pl.when(i >= 1)
        def _():
            _wait_rows(_tile_rows(tab_ref, tile - ROUTE_SUBS), sorted_hbm, sorted_ref.at[sub], sem_ref.at[sub], True)

        pl.when(is_prompt)(functools.partial(sort_tile, hp_ref, dp_ref, sub))
        pl.when(jnp.logical_not(is_prompt))(functools.partial(sort_tile, hs_ref, ds_ref, sub))
        for g in range(ng):
            n_g = count(tile, g)
            tab_ref[tile * TAB_W + g] = tab_ref[meta + ng + g] + fill_ref[g]
            tab_ref[tile * TAB_W + ng + g] = n_g
            fill_ref[g] = fill_ref[g] + n_g
        _run_copies(tab_ref, tile, sorted_hbm, sorted_ref.at[sub], sem_ref.at[sub], to_hbm=True, wait=False)

        @pl.when(i == n_steps - 1)
        def _():
            _wait_rows(_tile_rows(tab_ref, tile), sorted_hbm, sorted_ref.at[sub], sem_ref.at[sub], True)

    @pl.when(i == n_steps - 1)
    def _():
        _zero_fill(tab_ref, meta, zeros_ref, sorted_hbm, zsem_ref, n_rows, wait=False)
        _zero_fill(tab_ref, meta, zeros_ref, sorted_hbm, zsem_ref, n_rows, wait=True)


def _merged_specs(n_p_steps, n_s_steps, layer, sample_seq, step_of, width=D_MODEL):
    def p_map(*idx):
        return (jnp.minimum(step_of(*idx), n_p_steps - 1), 0)

    def s_map(*idx):
        return (jnp.clip(step_of(*idx) - n_p_steps, 0, n_s_steps - 1), 0)

    def mod_map(*idx):
        j = step_of(*idx)
        row = jnp.where(j < n_p_steps, 0, 1 + lax.div(jnp.maximum(j - n_p_steps, 0) * ROUTE_STEP, sample_seq))
        return (layer, row, 0, 0)

    return (pl.BlockSpec((ROUTE_STEP, width), p_map), pl.BlockSpec((ROUTE_STEP, width), s_map),
            pl.BlockSpec((None, None, 6, D_MODEL), mod_map))


def _dispatch(plan_p, plan_s, n_rows):
    haug_p, _, drow_p, cnt_p = plan_p
    haug_s, _, drow_s, cnt_s = plan_s
    n_p_steps = haug_p.shape[0] // ROUTE_STEP
    n_s_steps = haug_s.shape[0] // ROUTE_STEP
    n_steps = n_p_steps + n_s_steps
    n_tiles = n_steps * ROUTE_SUBS
    step_of = lambda i, cp, cs: i
    hp_spec, hs_spec, _ = _merged_specs(n_p_steps, n_s_steps, 0, 1, step_of, XS_W)
    drow_block = (ROUTE_SUBS, 8, ROUTE_TM)
    grid_spec = pltpu.PrefetchScalarGridSpec(
        num_scalar_prefetch=2,
        grid=(n_steps,),
        in_specs=[
            hp_spec, hs_spec,
            pl.BlockSpec(drow_block, lambda i, cp, cs: (jnp.minimum(i, n_p_steps - 1), 0, 0)),
            pl.BlockSpec(drow_block, lambda i, cp, cs: (jnp.clip(i - n_p_steps, 0, n_s_steps - 1), 0, 0)),
        ],
        out_specs=[pl.BlockSpec(memory_space=pltpu.SMEM), pl.BlockSpec(memory_space=pl.ANY)],
        scratch_shapes=[
            pltpu.VMEM((ROUTE_SUBS, ROUTE_ROWS, XS_W), F32),
            pltpu.VMEM((FFN_BM, XS_W), F32),
            pltpu.SMEM((N_GROUPS_MOE,), jnp.int32),
            pltpu.SemaphoreType.DMA((ROUTE_SUBS,)),
            pltpu.SemaphoreType.DMA(()),
        ],
    )
    return pl.pallas_call(
        functools.partial(_dispatch_kernel, n_steps=n_steps, n_p_steps=n_p_steps),
        grid_spec=grid_spec,
        out_shape=[
            jax.ShapeDtypeStruct(((n_tiles + 1) * TAB_W,), jnp.int32),
            jax.ShapeDtypeStruct((n_rows, XS_W), F32),
        ],
        compiler_params=_cparams("arbitrary"),
        name="moe_dispatch",
    )(cnt_p, cnt_s, haug_p, haug_s, drow_p, drow_s)


def _ffn_lookup(i, tab_ref, meta):
    fills = [tab_ref[meta + g] for g in range(N_GROUPS_MOE)]
    edges = []
    acc = 0
    for f in fills:
        acc = acc + lax.div(f + (FFN_BM - 1), FFN_BM)
        edges.append(acc)
    total = edges[-1]
    ii = jnp.minimum(i, total - 1)
    grp = sum((ii >= e).astype(jnp.int32) for e in edges[:-1])

    def pick(vals):
        return jnp.where(grp == 0, vals[0], jnp.where(grp == 1, vals[1], jnp.where(grp == 2, vals[2], vals[3])))

    first = pick([0] + edges[:-1])
    return grp, total, ii == first, pick(fills) - (ii - first) * FFN_BM


def _ffn_group_kernel(tab_ref, xs_ref, w1_ref, w3_ref, w2_ref, ys_ref, w1b_ref, w3b_ref, w2b_ref, *, meta):
    i = pl.program_id(0)
    _, total, first_of_group, valid = _ffn_lookup(i, tab_ref, meta)
    active = i < total

    @pl.when(jnp.logical_and(active, first_of_group))
    def _():
        w1b_ref[...] = w1_ref[...].astype(BF16)
        w3b_ref[...] = w3_ref[...].astype(BF16)
        w2b_ref[...] = w2_ref[...].astype(BF16)

    def run(rows):
        hb = xs_ref[0:rows, 0:D_MODEL].astype(BF16)
        ex = xs_ref[0:rows, D_MODEL:XS_W]
        hid = []
        for e in range(EXPERTS_PER_GROUP):
            a = _bdot(hb, w1b_ref[e])
            b = _bdot(hb, w3b_ref[e])
            gate = ex[:, e:e + 1] + ex[:, GATE_LO + e:GATE_LO + e + 1]
            hid.append(((a * jax.nn.sigmoid(a)) * b * gate).astype(BF16))
        ys_ref[0:rows, :] = _bdot(jnp.concatenate(hid, axis=1), w2b_ref[...].reshape(-1, D_MODEL))

    @pl.when(jnp.logical_and(active, valid > FFN_HALF))
    def _():
        run(FFN_BM)

    @pl.when(jnp.logical_and(active, valid <= FFN_HALF))
    def _():
        run(FFN_HALF)
        ys_ref[FFN_HALF:, :] = jnp.zeros((FFN_BM - FFN_HALF, D_MODEL), F32)

    @pl.when(jnp.logical_not(active))
    def _():
        ys_ref[...] = jnp.zeros_like(ys_ref)


def _ffn_group(tab, xs, w1, w3, w2, layer, meta):
    e4 = EXPERTS_PER_GROUP
    n_blocks = xs.shape[0] // FFN_BM
    group_of = lambda i, tab_ref: _ffn_lookup(i, tab_ref, meta)[0]
    grid_spec = pltpu.PrefetchScalarGridSpec(
        num_scalar_prefetch=1,
        grid=(n_blocks,),
        in_specs=[
            pl.BlockSpec((FFN_BM, XS_W), lambda i, tab_ref: (i, 0)),
            pl.BlockSpec((None, e4, D_MODEL, D_EXPERT), lambda i, tab_ref: (layer, group_of(i, tab_ref), 0, 0)),
            pl.BlockSpec((None, e4, D_MODEL, D_EXPERT), lambda i, tab_ref: (layer, group_of(i, tab_ref), 0, 0)),
            pl.BlockSpec((None, e4, D_EXPERT, D_MODEL), lambda i, tab_ref: (layer, group_of(i, tab_ref), 0, 0)),
        ],
        out_specs=pl.BlockSpec((FFN_BM, D_MODEL), lambda i, tab_ref: (i, 0)),
        scratch_shapes=[
            pltpu.VMEM((e4, D_MODEL, D_EXPERT), BF16),
            pltpu.VMEM((e4, D_MODEL, D_EXPERT), BF16),
            pltpu.VMEM((e4, D_EXPERT, D_MODEL), BF16),
        ],
    )
    return pl.pallas_call(
        functools.partial(_ffn_group_kernel, meta=meta),
        grid_spec=grid_spec,
        out_shape=jax.ShapeDtypeStruct((xs.shape[0], D_MODEL), F32),
        compiler_params=_cparams("arbitrary"),
        name="moe_ffn",
    )(tab, xs, w1, w3, w2)


def _fetch_runs(tab_ref, ys_ref, runs_ref, sem_ref, tile, slot):
    _run_copies(tab_ref, tile, ys_ref, runs_ref.at[slot], sem_ref.at[slot], to_hbm=False, wait=False)


def _unsort(tab_ref, ys_ref, runs_ref, sem_ref, dloc_col, tile, slot):
    covered = _tile_rows(tab_ref, tile)
    _wait_rows(covered, ys_ref, runs_ref.at[slot], sem_ref.at[slot], False)
    runs_ref[slot, pl.ds(pl.multiple_of(covered, ROUTE_PAD), LANES), :] = jnp.zeros((LANES, D_MODEL), F32)
    yb = runs_ref[slot, 0:SORT_ROWS, :].astype(BF16)
    row_f = lax.broadcasted_iota(jnp.int32, (ROUTE_TM, SORT_ROWS), 1).astype(F32)
    return _bdot(jnp.where(row_f == dloc_col, 1.0, 0.0).astype(BF16), yb)


def _combine_kernel(tab_ref, *refs, n_steps, first_step, n_p_steps, final_norm):
    with_prompt = first_step < n_p_steps
    if with_prompt:
        xp_ref, xs_ref, mod_ref, dp_ref, ds_ref, fg_ref, ys_ref, op_ref, os_ref, runs_ref, sem_ref = refs
    else:
        xs_ref, mod_ref, ds_ref, fg_ref, ys_ref, os_ref, runs_ref, sem_ref = refs
    i = pl.program_id(0)
    par = lax.rem(i, 2)
    step = first_step + i
    tm = ROUTE_TM

    def fetch(st, parity):
        for sub in range(ROUTE_SUBS):
            _fetch_runs(tab_ref, ys_ref, runs_ref, sem_ref, st * ROUTE_SUBS + sub, parity * ROUTE_SUBS + sub)

    @pl.when(i == 0)
    def _():
        fetch(first_step, 0)

    @pl.when(i + 1 < n_steps)
    def _():
        fetch(step + 1, 1 - par)

    is_prompt = step < n_p_steps
    dloc = jnp.where(is_prompt, dp_ref[:, 0:1], ds_ref[:, 0:1]) if with_prompt else ds_ref[:, 0:1]
    parts = [_unsort(tab_ref, ys_ref, runs_ref, sem_ref, dloc[sub * tm:(sub + 1) * tm],
                     step * ROUTE_SUBS + sub, par * ROUTE_SUBS + sub) for sub in range(ROUTE_SUBS)]
    delta = mod_ref[5:6, :] * jnp.concatenate(parts, axis=0)

    def finish(x_ref, o_ref):
        x2 = x_ref[...] + delta
        o_ref[...] = _rms(x2, fg_ref[...]) if final_norm else x2

    if with_prompt:
        pl.when(is_prompt)(functools.partial(finish, xp_ref, op_ref))
        pl.when(jnp.logical_not(is_prompt))(functools.partial(finish, xs_ref, os_ref))
    else:
        finish(xs_ref, os_ref)


def _combine(tab, xp, xs, mod, dloc_p, dloc_s, final_g, ys, layer, sample_seq, final_norm, with_prompt):
    n_p_steps = xp.shape[0] // ROUTE_STEP
    n_s_steps = xs.shape[0] // ROUTE_STEP
    first_step = 0 if with_prompt else n_p_steps
    n_steps = n_p_steps + n_s_steps - first_step
    step_of = lambda i, tab_ref: i + first_step
    p_spec, s_spec, mod_spec = _merged_specs(n_p_steps, n_s_steps, layer, sample_seq, step_of)
    dp_spec, ds_spec, _ = _merged_specs(n_p_steps, n_s_steps, layer, sample_seq, step_of, LANES)
    streams = [p_spec, s_spec] if with_prompt else [s_spec]
    dlocs = [dp_spec, ds_spec] if with_prompt else [ds_spec]
    grid_spec = pltpu.PrefetchScalarGridSpec(
        num_scalar_prefetch=1,
        grid=(n_steps,),
        in_specs=streams + [mod_spec] + dlocs + [
            pl.BlockSpec((1, D_MODEL), lambda i, tab_ref: (0, 0)),
            pl.BlockSpec(memory_space=pl.ANY),
        ],
        out_specs=streams,
        scratch_shapes=[
            pltpu.VMEM((2 * ROUTE_SUBS, RUN_ROWS, D_MODEL), F32),
            pltpu.SemaphoreType.DMA((2 * ROUTE_SUBS,)),
        ],
    )
    arrays = [xp, xs] if with_prompt else [xs]
    dloc_arrays = [dloc_p, dloc_s] if with_prompt else [dloc_s]
    return pl.pallas_call(
        functools.partial(_combine_kernel, n_steps=n_steps, first_step=first_step, n_p_steps=n_p_steps,
                          final_norm=final_norm),
        grid_spec=grid_spec,
        out_shape=[jax.ShapeDtypeStruct(a.shape, F32) for a in arrays],
        compiler_params=_cparams("arbitrary"),
        name="moe_combine",
    )(tab, *arrays, mod, *dloc_arrays, final_g, ys)


def _moe(xp, xs, plan_p, plan_s, mod, w1, w3, w2, final_g, layer, sample_seq, final_norm, defer_prompt):
    plan_p, plan_s = _flat_plan(plan_p), _flat_plan(plan_s)
    t = xp.shape[0] + xs.shape[0]
    n_tiles = t // ROUTE_TM
    max_rows = t + N_GROUPS_MOE * (ROUTE_PAD - 1) * n_tiles
    n_rows = (-(-max_rows // FFN_BM) + N_GROUPS_MOE) * FFN_BM
    tab, sorted_x = _dispatch(plan_p, plan_s, n_rows)
    ys = _ffn_group(tab, sorted_x, w1, w3, w2, layer, n_tiles * TAB_W)
    out = _combine(tab, xp, xs, mod, plan_p[1], plan_s[1], final_g, ys, layer, sample_seq, final_norm,
                   not defer_prompt)
    if defer_prompt:
        return None, out[0], (tab, plan_p[1], ys)
    return out[0], out[1], None


def _rope_tables(seq):
    half = QK_ROPE // 2
    nf = half // 2
    inv = ROPE_BASE ** (-np.arange(nf, dtype=np.float64) / nf)
    pos = np.arange(seq)
    row = (pos // GRID_W).astype(np.float64)
    col = (pos % GRID_W).astype(np.float64)
    cos = np.ones((seq, HEAD_PAD), np.float64)
    sin_a = np.zeros((seq, HEAD_PAD), np.float64)
    sin_b = np.zeros((seq, HEAD_PAD), np.float64)
    for part, p in enumerate((row, col)):
        ang = p[:, None] * inv[None, :]
        base = ROPE_OFF + part * half
        cos[:, base:base + nf] = np.cos(ang)
        cos[:, base + nf:base + half] = np.cos(ang)
        sin_a[:, base:base + nf] = -np.sin(ang)
        sin_b[:, base + nf:base + half] = np.sin(ang)
    return tuple(jnp.asarray(a, F32) for a in (cos, sin_a, sin_b))


def _apply_rope(x, cos, sin_a, sin_b, reps):
    nf = QK_ROPE // 4
    width = x.shape[1]
    if reps > 1:
        cos, sin_a, sin_b = (jnp.concatenate([a] * reps, axis=1) for a in (cos, sin_a, sin_b))
    return x * cos + pltpu.roll(x, width - nf, axis=1) * sin_a + pltpu.roll(x, nf, axis=1) * sin_b


def _odd_in_kernel(*refs, rope, emit_cache):
    x_ref, mod_ref, g_ref, w_ref, qg_ref, wq_ref, kg_ref, wk_ref, wv_ref, cs_ref = refs[:10]
    refs = refs[10:]
    if rope:
        cos_ref, sa_ref, sb_ref = refs[:3]
        refs = refs[3:]
    y_ref, q_ref, k_ref, v_ref = refs[:4]
    refs = refs[4:]
    m = mod_ref[...]
    h = _norm_mod(x_ref[...], g_ref[...], m[1:2], m[0:1])
    z = _bdot(h.astype(BF16), w_ref[...])
    zc = z[:, 0:D_C]
    qc = z[:, D_C:D_C + Q_LORA]
    kvc = z[:, D_C + Q_LORA:D_C + Q_LORA + KV_LORA]
    kpe = z[:, D_C + Q_LORA + KV_LORA:]
    q = _bdot(_rms(qc, qg_ref[...]).astype(BF16), wq_ref[...])
    kvn = _rms(kvc, kg_ref[...])
    if emit_cache:
        ckv_ref, kpe_ref = refs
        ckv_ref[...] = kvn.reshape(ckv_ref.shape)
        kpe_ref[...] = kpe[:, ROPE_OFF:ROPE_OFF + QK_ROPE].reshape(kpe_ref.shape)
    if rope:
        tabs = (cos_ref[...], sa_ref[...], sb_ref[...])
        q = _apply_rope(q, *tabs, reps=N_HEADS)
        kpe = _apply_rope(kpe, *tabs, reps=1)
    kvb = kvn.astype(BF16)
    k = _bdot(kvb, wk_ref[...]) + jnp.concatenate([kpe] * N_HEADS, axis=1)
    scale = math.log2(math.e) / math.sqrt(QK_NOPE + QK_ROPE)
    q_ref[...] = (q * scale).astype(BF16)
    k_ref[...] = k.astype(BF16)
    v_ref[...] = _bdot(kvb, wv_ref[...]).astype(BF16)
    y = _bdot(zc.astype(BF16), cs_ref[...])
    y_ref[0, :, :] = y[:, 0:D_C].astype(BF16)
    y_ref[1, :, :] = y[:, D_C:2 * D_C].astype(BF16)


def _odd_in(x3, mod, g, w_in, q_g, w_q, kv_g, w_k, w_v, cs, stream, layer, rope_tabs, emit_cache):
    b, s, _ = x3.shape
    tm = min(s, 512)
    n_i = s // tm
    rope = rope_tabs is not None
    const = lambda a: pl.BlockSpec(a.shape, lambda bi, i: (0,) * a.ndim)
    in_specs = [
        pl.BlockSpec((None, tm, D_MODEL), lambda bi, i: (bi, i, 0)),
        _mod_spec(stream, layer),
        const(g), const(w_in), const(q_g), const(w_q), const(kv_g), const(w_k), const(w_v), const(cs),
    ]
    args = [x3, mod, g, w_in, q_g, w_q, kv_g, w_k, w_v, cs]
    if rope:
        in_specs += [pl.BlockSpec((tm, HEAD_PAD), lambda bi, i: (i, 0))] * 3
        args += list(rope_tabs)
    hq = N_HEADS * HEAD_PAD
    out_specs = [
        pl.BlockSpec((None, 2, tm, D_C), lambda bi, i: (bi, 0, i, 0)),
        pl.BlockSpec((None, tm, hq), lambda bi, i: (bi, i, 0)),
        pl.BlockSpec((None, tm, hq), lambda bi, i: (bi, i, 0)),
        pl.BlockSpec((None, tm, N_HEADS * V_DIM), lambda bi, i: (bi, i, 0)),
    ]
    out_shape = [
        jax.ShapeDtypeStruct((b, 2, s, D_C), BF16),
        jax.ShapeDtypeStruct((b, s, hq), BF16),
        jax.ShapeDtypeStruct((b, s, hq), BF16),
        jax.ShapeDtypeStruct((b, s, N_HEADS * V_DIM), BF16),
    ]
    if emit_cache:
        out_specs += [
            pl.BlockSpec((None, tm, KV_LORA), lambda bi, i: (bi, i, 0)),
            pl.BlockSpec((None, tm, QK_ROPE), lambda bi, i: (bi, i, 0)),
        ]
        out_shape += [
            jax.ShapeDtypeStruct((b, s, KV_LORA), F32),
            jax.ShapeDtypeStruct((b, s, QK_ROPE), F32),
        ]
    return pl.pallas_call(
        functools.partial(_odd_in_kernel, rope=rope, emit_cache=emit_cache),
        grid=(b, n_i),
        in_specs=in_specs,
        out_specs=out_specs,
        out_shape=out_shape,
        compiler_params=_cparams("parallel", "parallel"),
        name="odd_in",
    )(*args)


def _cache_kv_kernel(c_ref, p_ref, wk_ref, wv_ref, k_ref, v_ref):
    cb = c_ref[...].astype(BF16)
    k = _bdot(cb, wk_ref[...]) + jnp.concatenate([p_ref[...]] * N_HEADS, axis=1)
    k_ref[...] = k.astype(BF16)
    v_ref[...] = _bdot(cb, wv_ref[...]).astype(BF16)


def _cache_kv(ckv, kpe_blk, w_k, w_v):
    b, p, _ = ckv.shape
    hq = N_HEADS * HEAD_PAD
    return pl.pallas_call(
        _cache_kv_kernel,
        grid=(b,),
        in_specs=[
            pl.BlockSpec((None, p, KV_LORA), lambda bi: (bi, 0, 0)),
            pl.BlockSpec((None, p, HEAD_PAD), lambda bi: (bi, 0, 0)),
            pl.BlockSpec(w_k.shape, lambda bi: (0, 0)),
            pl.BlockSpec(w_v.shape, lambda bi: (0, 0)),
        ],
        out_specs=[
            pl.BlockSpec((None, p, hq), lambda bi: (bi, 0, 0)),
            pl.BlockSpec((None, p, N_HEADS * V_DIM), lambda bi: (bi, 0, 0)),
        ],
        out_shape=[
            jax.ShapeDtypeStruct((b, p, hq), BF16),
            jax.ShapeDtypeStruct((b, p, N_HEADS * V_DIM), BF16),
        ],
        compiler_params=_cparams("parallel"),
        name="cache_kv",
    )(ckv, kpe_blk, w_k, w_v)


def _odd_mix_kernel(*refs, with_cache, n_seq=1):
    q_ref, k_ref, v_ref = refs[:3]
    refs = refs[3:]
    if with_cache:
        kc_ref, vc_ref = refs[:2]
        refs = refs[2:]
    plan_tile = None
    if with_cache:
        y_ref, f_ref, x_ref, mod_ref, wo_ref = refs[:5]
        plan_in, (o_ref, *plan_out), (a_ref, vx_ref, vcx_ref) = refs[5:9], refs[9:14], refs[14:]
        plan_tile = pl.program_id(0) * pl.num_programs(1) + pl.program_id(1)

        @pl.when(pl.program_id(1) == 0)
        def _():
            for src, dst in ((v_ref, vx_ref), (vc_ref, vcx_ref)):
                one = lax.broadcasted_iota(jnp.int32, (src.shape[0], PAIR_W - DEN_COL), 1) == 0
                for pair in range(N_HEADS // 2):
                    dst[:, pair * PAIR_W:pair * PAIR_W + DEN_COL] = src[:, pair * DEN_COL:(pair + 1) * DEN_COL]
                    dst[:, pair * PAIR_W + DEN_COL:(pair + 1) * PAIR_W] = jnp.where(one, 1.0, 0.0).astype(BF16)
    else:
        y_ref, f_ref, x_ref, mod_ref, wo_ref, o_ref, a_ref = refs[:7]
        if len(refs) > 7:
            plan_in, plan_out, plan_tile = refs[7:]
    tq = q_ref.shape[0] // n_seq
    lane = lax.broadcasted_iota(jnp.int32, (tq, 2 * V_DIM), 1)
    f_parts = []
    for sq in range(n_seq):
        rows = slice(sq * tq, (sq + 1) * tq)
        for pair in range(N_HEADS // 2):
            vcols = slice(pair * PAIR_W, (pair + 1) * PAIR_W)
            outs = []
            for h in (2 * pair, 2 * pair + 1):
                hcols = slice(h * HEAD_PAD, (h + 1) * HEAD_PAD)
                qh = q_ref[rows, hcols]
                if with_cache:
                    s = lax.dot_general(qh, k_ref[:, hcols], NT_DIMS, preferred_element_type=F32)
                    sc = lax.dot_general(qh, kc_ref[:, hcols], NT_DIMS, preferred_element_type=F32)
                    top = jnp.maximum(jnp.max(s, axis=-1, keepdims=True), jnp.max(sc, axis=-1, keepdims=True))
                    acc = _bdot(jnp.exp2((s - top).astype(BF16)), vx_ref[:, vcols])
                    acc = acc + _bdot(jnp.exp2((sc - top).astype(BF16)), vcx_ref[:, vcols])
                    outs.append(acc[:, 0:2 * V_DIM] / acc[:, DEN_COL:DEN_COL + 1])
                else:
                    s = lax.dot_general(qh, k_ref[rows, hcols], NT_DIMS, preferred_element_type=F32)
                    p = jnp.exp2(s - jnp.max(s, axis=-1, keepdims=True))
                    den = jnp.sum(p, axis=-1, keepdims=True)
                    outs.append(_bdot(p.astype(BF16), v_ref[rows, pair * DEN_COL:(pair + 1) * DEN_COL]) / den)
            a_ref[rows, pair * 2 * V_DIM:(pair + 1) * 2 * V_DIM] = (
                jnp.where(lane < V_DIM, outs[0], outs[1]).astype(BF16))
        if n_seq > 1:
            f_parts.append(_bdot(f_ref[:, 0:tq], y_ref[0, rows, :]) + _bdot(f_ref[:, tq:], y_ref[1, rows, :]))
    f = jnp.concatenate(f_parts, axis=0) if n_seq > 1 else _bdot(f_ref[...], y_ref[...].reshape(-1, D_C))
    o = _bdot(f.astype(BF16), wo_ref[0:D_C, :]) + _bdot(a_ref[...], wo_ref[D_C:, :])
    x1 = x_ref[...] + mod_ref[2:3, :] * o
    o_ref[...] = x1
    if plan_tile is not None:
        _route_plan(x1, mod_ref[...], plan_in, plan_out, plan_tile)


WHOLE_SEQS = 2


def _odd_whole_kernel(tab_ref, x_ref, mod_ref, g_ref, w_ref, qg_ref, wq_ref, kg_ref, wk_ref, wv_ref, cs_ref, f_ref,
                      wo_ref, modp_ref, dloc_ref, ys_ref, pg_ref, pw_ref, pb_ref, pu_ref,
                      o_ref, ckv_ref, kpe_ref, ph_ref, pd_ref, pr_ref, pc_ref,
                      y_ref, q_ref, k_ref, v_ref, a_ref, x2_ref, runs_ref, sem_ref):
    bi = pl.program_id(0)
    par = lax.rem(bi, 2)
    tm = ROUTE_TM

    def fetch(step, parity):
        for sq in range(WHOLE_SEQS):
            _fetch_runs(tab_ref, ys_ref, runs_ref, sem_ref, step * WHOLE_SEQS + sq, parity * WHOLE_SEQS + sq)

    @pl.when(bi == 0)
    def _():
        fetch(0, 0)

    @pl.when(bi + 1 < pl.num_programs(0))
    def _():
        fetch(bi + 1, 1 - par)

    for sq in range(WHOLE_SEQS):
        rows = slice(sq * tm, (sq + 1) * tm)
        moe = _unsort(tab_ref, ys_ref, runs_ref, sem_ref, dloc_ref[rows, 0:1], bi * WHOLE_SEQS + sq,
                      par * WHOLE_SEQS + sq)
        x2_ref[rows, :] = x_ref[rows, :] + modp_ref[5:6, :] * moe
    _odd_in_kernel(x2_ref, mod_ref, g_ref, w_ref, qg_ref, wq_ref, kg_ref, wk_ref, wv_ref, cs_ref,
                   y_ref, q_ref, k_ref, v_ref, ckv_ref, kpe_ref, rope=False, emit_cache=True)
    _odd_mix_kernel(q_ref, k_ref, v_ref, y_ref, f_ref, x2_ref, mod_ref, wo_ref, o_ref, a_ref,
                    (pg_ref, pw_ref, pb_ref, pu_ref), (ph_ref, pd_ref, pr_ref, pc_ref), bi * WHOLE_SEQS,
                    with_cache=False, n_seq=WHOLE_SEQS)


def _odd_whole(x3, pending, mod, g, w_in, q_g, w_q, kv_g, w_k, w_v, cs, fmat, w_out, plan_consts, stream, layer):
    tab, dloc, ys = pending
    b, seq, _ = x3.shape
    assert seq == ROUTE_TM and b % WHOLE_SEQS == 0 and stream.shared_cond
    b, s = b // WHOLE_SEQS, seq * WHOLE_SEQS
    x3 = x3.reshape(b, s, D_MODEL)
    hq = N_HEADS * HEAD_PAD
    hv = N_HEADS * V_DIM
    const = lambda a: pl.BlockSpec(a.shape, lambda bi, tab_ref: (0,) * a.ndim)
    row_block = lambda w: pl.BlockSpec((None, s, w), lambda bi, tab_ref: (bi, 0, 0))
    mod_block = lambda lyr: pl.BlockSpec((None, None, 6, D_MODEL), lambda bi, tab_ref: (lyr, 0, 0, 0))
    cache_block = lambda w: pl.BlockSpec((WHOLE_SEQS, 1, seq, w), lambda bi, tab_ref: (bi, 0, 0, 0))
    plan_shapes, plan_specs = _plan_out(b, s, s, lambda bi, tab_ref: (bi, 0))
    grid_spec = pltpu.PrefetchScalarGridSpec(
        num_scalar_prefetch=1,
        grid=(b,),
        in_specs=[
            row_block(D_MODEL), mod_block(layer),
            const(g), const(w_in), const(q_g), const(w_q), const(kv_g), const(w_k), const(w_v), const(cs),
            const(fmat), const(w_out),
            mod_block(layer - 1),
            pl.BlockSpec((s, LANES), lambda bi, tab_ref: (bi, 0)),
            pl.BlockSpec(memory_space=pl.ANY),
        ] + [const(a) for a in plan_consts],
        out_specs=[row_block(D_MODEL), cache_block(KV_LORA), cache_block(QK_ROPE)] + plan_specs,
        scratch_shapes=[
            pltpu.VMEM((2, s, D_C), BF16),
            pltpu.VMEM((s, hq), BF16),
            pltpu.VMEM((s, hq), BF16),
            pltpu.VMEM((s, hv), BF16),
            pltpu.VMEM((s, hv), BF16),
            pltpu.VMEM((s, D_MODEL), F32),
            pltpu.VMEM((2 * WHOLE_SEQS, RUN_ROWS, D_MODEL), F32),
            pltpu.SemaphoreType.DMA((2 * WHOLE_SEQS,)),
        ],
    )
    out = pl.pallas_call(
        _odd_whole_kernel,
        grid_spec=grid_spec,
        out_shape=[
            jax.ShapeDtypeStruct((b, s, D_MODEL), F32),
            jax.ShapeDtypeStruct((b * WHOLE_SEQS, 1, seq, KV_LORA), F32),
            jax.ShapeDtypeStruct((b * WHOLE_SEQS, 1, seq, QK_ROPE), F32),
        ] + plan_shapes,
        compiler_params=_cparams("arbitrary"),
        name="odd_whole",
    )(tab, x3, mod, g, w_in, q_g, w_q, kv_g, w_k, w_v, cs, fmat, w_out, mod, dloc, ys, *plan_consts)
    return out[0].reshape(b * WHOLE_SEQS, seq, D_MODEL), out[1], out[2], out[3:]


def _odd_mix(q, k, v, kc, vc, y, fmat, x3, mod, w_out, plan_consts, stream, layer):
    b, s, hq = q.shape
    tq = min(s, 256)
    n_i = s // tq
    with_cache = kc is not None
    hv = N_HEADS * V_DIM
    mode = dict(pipeline_mode=pl.Buffered(1)) if with_cache else {}

    def per_batch(rows, cols):
        return pl.BlockSpec((None, rows, cols), lambda bi, i: (bi, 0, 0), **mode)

    in_specs = [pl.BlockSpec((None, tq, hq), lambda bi, i: (bi, i, 0)), per_batch(s, hq), per_batch(s, hv)]
    args = [q, k, v]
    if with_cache:
        p = kc.shape[1]
        in_specs += [per_batch(p, hq), per_batch(p, hv)]
        args += [kc, vc]
    in_specs += [
        per_batch(2 * s, D_C),
        pl.BlockSpec((tq, 2 * s), lambda bi, i: (i, 0)),
        pl.BlockSpec((None, tq, D_MODEL), lambda bi, i: (bi, i, 0)),
        _mod_spec(stream, layer),
        pl.BlockSpec(w_out.shape, lambda bi, i: (0, 0), **mode),
    ]
    in_specs += [pl.BlockSpec(a.shape, lambda bi, i: (0,) * a.ndim) for a in plan_consts]
    args += [y, fmat, x3, mod, w_out] + plan_consts
    plan_shapes, plan_specs = _plan_out(b, s, tq, lambda bi, i: (bi, i))
    out = pl.pallas_call(
        functools.partial(_odd_mix_kernel, with_cache=with_cache),
        grid=(b, n_i),
        in_specs=in_specs,
        out_specs=[pl.BlockSpec((None, tq, D_MODEL), lambda bi, i: (bi, i, 0))] + plan_specs,
        out_shape=[jax.ShapeDtypeStruct((b, s, D_MODEL), F32)] + plan_shapes,
        scratch_shapes=[pltpu.VMEM((tq, hv), BF16)] + (
            [pltpu.VMEM((s, VX_W), BF16), pltpu.VMEM((kc.shape[1], VX_W), BF16)] if with_cache else []),
        compiler_params=_cparams("arbitrary", "arbitrary"),
        name="odd_mix",
    )(*args)
    return out[0], out[1:]


def _dft_tables(seq):
    jc = np.arange(C_GW)
    ang_c = 2.0 * np.pi * np.outer(jc, jc) / C_GW
    eye = np.eye(C_GROUPS)
    cs = np.concatenate([np.kron(eye, np.cos(ang_c)), np.kron(eye, np.sin(ang_c))], axis=1)
    jn = np.arange(seq)
    ang_n = 2.0 * np.pi * (np.outer(jn, jn) % seq) / seq
    scale = 1.0 / math.sqrt(seq * C_GW)
    fmat = np.concatenate([np.cos(ang_n), -np.sin(ang_n)], axis=1) * scale
    return jnp.asarray(cs, F32).astype(BF16), jnp.asarray(fmat, F32).astype(BF16)


def _odd_weights(w_in, w_uq, w_ukv):
    d = w_in.shape[0]
    base = D_C + Q_LORA + KV_LORA
    kpe_blk = jnp.zeros((d, HEAD_PAD), w_in.dtype).at[:, ROPE_OFF:ROPE_OFF + QK_ROPE].set(w_in[:, base:])
    w_in_p = jnp.concatenate([w_in[:, :base], kpe_blk], axis=1).astype(BF16)
    qh = w_uq.reshape(Q_LORA, N_HEADS, QK_NOPE + QK_ROPE)
    w_q = jnp.pad(qh, ((0, 0), (0, 0), (0, HEAD_PAD - QK_NOPE - QK_ROPE))).reshape(Q_LORA, -1).astype(BF16)
    kvh = w_ukv.reshape(KV_LORA, N_HEADS, QK_NOPE + V_DIM)
    w_k = jnp.pad(kvh[:, :, :QK_NOPE], ((0, 0), (0, 0), (0, HEAD_PAD - QK_NOPE))).reshape(KV_LORA, -1)
    w_v = kvh[:, :, QK_NOPE:].reshape(KV_LORA, -1)
    return w_in_p, w_q, w_k.astype(BF16), w_v.astype(BF16)


ROUTER_ROWS = 32


def _router_weights(wg, bg, we, be):
    d = wg.shape[0]
    w = jnp.concatenate([wg, we.reshape(d, N_EXPERTS)], axis=1).T
    w = jnp.pad(w, ((0, ROUTER_ROWS - w.shape[0]), (0, 0))).astype(BF16)
    b = jnp.concatenate([bg, be.reshape(N_EXPERTS)])
    b = jnp.pad(b, (0, ROUTER_ROWS - b.shape[0])).reshape(ROUTER_ROWS, 1).astype(F32)
    return w, b


def kernel(x_prompt, x_sample, cache_ckv, cache_kpe, c, c_ctx, mod_w, mod_b, norm1_g, norm2_g,
           ev_w_in, ev_conv_w, ev_sgu_norm_g, ev_sgu_w, ev_sgu_b, ev_w_out,
           od_w_in, od_q_norm_g, od_w_uq, od_kv_norm_g, od_w_ukv, od_w_out,
           moe_wg, moe_bg, moe_we, moe_be, moe_w1, moe_w3, moe_w2, final_norm_g):
    bp, n_p, d = x_prompt.shape
    bs, n_s, _ = x_sample.shape
    streams = [(_Stream(bp, n_p, True), x_prompt), (_Stream(bs, n_s, False), x_sample)]

    n_rows = 1 + bs
    mod = _adaln(jnp.concatenate([c_ctx[None, :], c], axis=0), mod_w, mod_b, n_rows)

    final_g = final_norm_g.reshape(1, d)
    xs = [x for _, x in streams]
    new_ckv, new_kpe = [], []
    pending = None
    plans = [None, None]
    for l in range(DEPTH):
        j = l // 2
        g1 = norm1_g[l].reshape(1, d)
        g2 = norm2_g[l].reshape(1, d)
        plan_consts = _plan_consts(g2, *_router_weights(moe_wg[l], moe_bg[l], moe_we[l], moe_be[l]))
        last = l == DEPTH - 1
        if l % 2 == 0:
            w_in = ev_w_in[j].astype(BF16)
            w_out = ev_w_out[j].astype(BF16)
            sgu_w = ev_sgu_w[j].astype(BF16)
            sgu_g = ev_sgu_norm_g[j].reshape(1, D_B)
            sgu_bias = jnp.repeat(ev_sgu_b[j].T, D_B // B_GROUPS, axis=1)
            for si, (st, _) in enumerate(streams):
                xs[si], plans[si] = _even_layer(xs[si], mod, g1, w_in, ev_conv_w[j], sgu_g, sgu_w, sgu_bias, w_out,
                                                plan_consts, st, l)
        else:
            w_in, w_q, w_k, w_v = _odd_weights(od_w_in[j], od_w_uq[j], od_w_ukv[j])
            w_out = od_w_out[j].astype(BF16)
            q_g = od_q_norm_g[j].reshape(1, Q_LORA)
            kv_g = od_kv_norm_g[j].reshape(1, KV_LORA)
            for si, (st, _) in enumerate(streams):
                x3 = xs[si]
                cs, fmat = _dft_tables(st.seq)
                if st.shared_cond:
                    xs[si], ckv, kpe, plans[si] = _odd_whole(x3, pending, mod, g1, w_in, q_g, w_q, kv_g, w_k, w_v, cs,
                                                             fmat, w_out, plan_consts, st, l)
                    new_ckv.append(ckv)
                    new_kpe.append(kpe)
                    continue
                y, q, k, v = _odd_in(x3, mod, g1, w_in, q_g, w_q, kv_g, w_k, w_v, cs, st, l,
                                     _rope_tables(st.seq), False)
                kpe_blk = jnp.pad(cache_kpe[:, j], ((0, 0), (0, 0), (ROPE_OFF, HEAD_PAD - ROPE_OFF - QK_ROPE)))
                kc, vc = _cache_kv(cache_ckv[:, j], kpe_blk, w_k, w_v)
                xs[si], plans[si] = _odd_mix(q, k, v, kc, vc, y.reshape(st.batch, 2 * st.seq, D_C), fmat, x3, mod,
                                             w_out, plan_consts, st, l)
        defer = not last and (l + 1) % 2 == 1
        x2p, x2s, pending = _moe(xs[0].reshape(bp * n_p, d), xs[1].reshape(bs * n_s, d), plans[0], plans[1], mod,
                                 moe_w1, moe_w3, moe_w2, final_g, l, n_s, last, defer)
        xs = [xs[0] if defer else x2p.reshape(bp, n_p, d), x2s.reshape(bs, n_s, d)]
    return (xs[0], xs[1], jnp.concatenate(new_ckv, axis=1), jnp.concatenate(new_kpe, axis=1))
```

```python
import functools
import math

import numpy as np
import jax
import jax.numpy as jnp
from jax import lax
from jax.experimental import pallas as pl
from jax.experimental.pallas import tpu as pltpu

D_MODEL = 1024
DEPTH = 2
GRID_W = 64
D_A = D_MODEL // 2
D_B = D_MODEL // 2
B_GROUPS = 4
CHUNK = 128
D_EVEN_IN = 3 * D_A + 2 * D_B
D_C = D_MODEL // 4
C_GROUPS = 4
C_GW = D_C // C_GROUPS
N_HEADS = 12
QK_NOPE = 64
QK_ROPE = 32
V_DIM = 64
Q_LORA = 384
KV_LORA = 256
ROPE_BASE = 10000.0
N_GROUPS_MOE = 4
EXPERTS_PER_GROUP = 4
N_EXPERTS = N_GROUPS_MOE * EXPERTS_PER_GROUP
D_EXPERT = 256
EPS = 1e-6

LANES = 128
HEAD_PAD = 128
PAIR_W = 256
DEN_COL = 2 * 64
VX_W = 6 * PAIR_W
ROPE_OFF = QK_NOPE
GATE_OFF = N_GROUPS_MOE
NEG_BIG = -1e30
F32 = jnp.float32
BF16 = jnp.bfloat16
VMEM_LIMIT = 56 * 1024 * 1024


def _cparams(*sem):
    return pltpu.CompilerParams(dimension_semantics=sem, vmem_limit_bytes=VMEM_LIMIT)


def _rms(x, g):
    return x * lax.rsqrt(jnp.mean(x * x, axis=-1, keepdims=True) + EPS) * g


def _norm_mod(x, g, scale, shift):
    rs = lax.rsqrt(jnp.mean(x * x, axis=-1, keepdims=True) + EPS)
    return x * rs * (g * (1.0 + scale)) + shift


def _bdot(a, b):
    return jnp.dot(a, b, preferred_element_type=F32)


NT_DIMS = (((1,), (1,)), ((), ()))


MOD_ROWS = 8


def _mod_kernel(c_ref, w_ref, b_ref, o_ref):
    c = c_ref[...]
    s = c * jax.nn.sigmoid(c)
    s_hi = s.astype(BF16).astype(F32)
    pair = jnp.concatenate([s_hi, s - s_hi], axis=0).astype(BF16)
    two = _bdot(pair, w_ref[...].astype(BF16))
    o_ref[...] = two[0:MOD_ROWS] + two[MOD_ROWS:] + b_ref[...]


def _adaln(cond, mod_w, mod_b, n_rows):
    nt = 2048
    d6 = mod_w.shape[-1]
    cond = jnp.pad(cond, ((0, MOD_ROWS - n_rows), (0, 0)))
    out = pl.pallas_call(
        _mod_kernel,
        grid=(DEPTH, d6 // nt),
        in_specs=[
            pl.BlockSpec(cond.shape, lambda l, n: (0, 0)),
            pl.BlockSpec((None, D_MODEL, nt), lambda l, n: (l, 0, n)),
            pl.BlockSpec((None, 1, nt), lambda l, n: (l, 0, n)),
        ],
        out_specs=pl.BlockSpec((None, MOD_ROWS, nt), lambda l, n: (l, 0, n)),
        out_shape=jax.ShapeDtypeStruct((DEPTH, MOD_ROWS, d6), F32),
        compiler_params=_cparams("parallel", "parallel"),
        name="adaln",
    )(cond, mod_w, mod_b.reshape(DEPTH, 1, d6))
    return out[:, :n_rows].reshape(DEPTH, n_rows, 6, D_MODEL)


class _Stream:
    def __init__(self, batch, seq, shared_cond):
        self.batch = batch
        self.seq = seq
        self.shared_cond = shared_cond

    def row_of_batch(self, b):
        return 0 if self.shared_cond else b + 1


def _mod_spec(stream, layer):
    return pl.BlockSpec((None, None, 6, D_MODEL), lambda b, i: (layer, stream.row_of_batch(b), 0, 0))


HALO = 8
EVEN_SEQS = 4


def _even_kernel(*refs, has_halo, seq_rows, nc):
    if has_halo:
        x_ref, xp_ref, xn_ref = refs[:3]
        refs = refs[3:]
    else:
        x_ref = refs[0]
        refs = refs[1:]
    mod_ref, g_ref, wi_ref, cw_ref, sg_ref, sw_ref, sb_ref, wo_ref = refs[:8]
    plan_in, (o_ref, *plan_out), (z_ref, y_ref) = refs[8:12], refs[12:17], refs[17:]
    i = pl.program_id(1)
    n_i = pl.num_programs(1)
    ts = x_ref.shape[0]
    m = mod_ref[...]
    g = g_ref[...]

    def modulate(x):
        return _norm_mod(x, g, m[1:2], m[0:1]).astype(BF16)

    x = x_ref[...]
    hb = modulate(x)
    for n in range(D_EVEN_IN // nc):
        z_ref[:, n * nc:(n + 1) * nc] = _bdot(hb, wi_ref[:, n * nc:(n + 1) * nc]).astype(BF16)

    gate_b = z_ref[:, 0:D_A].astype(F32)
    gate_c = z_ref[:, D_A:2 * D_A].astype(F32)
    xa = z_ref[:, 2 * D_A:3 * D_A].astype(F32)
    t = gate_c * xa
    t_prev = pltpu.roll(t, 1, axis=0)
    t_next = pltpu.roll(t, ts - 1, axis=0)
    row = lax.broadcasted_iota(jnp.int32, (ts, 1), 0) & (seq_rows - 1)
    if has_halo:
        hh = modulate(jnp.concatenate([xp_ref[...], xn_ref[...]], axis=0))
        zh = _bdot(hh, wi_ref[:, D_A:3 * D_A]).astype(BF16).astype(F32)
        th = zh[:, 0:D_A] * zh[:, D_A:2 * D_A]
        tp = th[HALO - 1:HALO] * (i > 0).astype(F32)
        tn = th[HALO:HALO + 1] * (i < n_i - 1).astype(F32)
    else:
        tp = tn = 0.0
    t_prev = jnp.where(row == 0, tp, t_prev)
    t_next = jnp.where(row == seq_rows - 1, tn, t_next)
    cw = cw_ref[...]
    y_a = gate_b * (t_prev * cw[0:1] + t * cw[1:2] + t_next * cw[2:3])
    y_ref[:, 0:D_A] = y_a.astype(BF16)

    u = z_ref[:, 3 * D_A:3 * D_A + D_B].astype(F32)
    v = z_ref[:, 3 * D_A + D_B:3 * D_A + 2 * D_B].astype(F32)
    vb = _rms(v, sg_ref[...]).astype(BF16)
    gw = D_B // B_GROUPS
    for c in range(ts // CHUNK):
        rows = slice(c * CHUNK, (c + 1) * CHUNK)
        for gi in range(B_GROUPS):
            cols = slice(gi * gw, (gi + 1) * gw)
            sv = _bdot(sw_ref[gi], vb[rows, cols]) + sb_ref[:, cols]
            y_ref[rows, D_A + gi * gw:D_A + (gi + 1) * gw] = (u[rows, cols] * sv).astype(BF16)

    x1 = x + m[2:3] * _bdot(y_ref[...], wo_ref[...])
    o_ref[...] = x1
    _route_plan(x1, m, plan_in, plan_out, (pl.program_id(0) * n_i + i) * (ts // ROUTE_TM))


def _even_layer(x3, mod, g, w_in, conv_w, sgu_g, sgu_w, sgu_bias, w_out, plan_consts, stream, layer):
    out_shape = x3.shape
    b, s, _ = x3.shape
    ts = min(s, 256) if stream.shared_cond else min(s, 1024)
    n_i = s // ts
    has_halo = n_i > 1
    seq_rows = ts
    if not has_halo and stream.shared_cond and b % EVEN_SEQS == 0:
        b, s, ts = b // EVEN_SEQS, EVEN_SEQS * s, EVEN_SEQS * ts
        x3 = x3.reshape(b, s, D_MODEL)
    hb = ts // HALO
    last_h = s // HALO - 1
    const = lambda a: pl.BlockSpec(a.shape, lambda bi, i: (0,) * a.ndim)
    in_specs = [pl.BlockSpec((None, ts, D_MODEL), lambda bi, i: (bi, i, 0))]
    args = [x3]
    if has_halo:
        in_specs += [
            pl.BlockSpec((None, HALO, D_MODEL), lambda bi, i: (bi, jnp.maximum(i * hb - 1, 0), 0)),
            pl.BlockSpec((None, HALO, D_MODEL), lambda bi, i: (bi, jnp.minimum((i + 1) * hb, last_h), 0)),
        ]
        args += [x3, x3]
    in_specs += [_mod_spec(stream, layer), const(g), const(w_in), const(conv_w), const(sgu_g), const(sgu_w),
                 const(sgu_bias), const(w_out)] + [const(a) for a in plan_consts]
    args += [mod, g, w_in, conv_w, sgu_g, sgu_w, sgu_bias, w_out] + plan_consts
    plan_shapes, plan_specs = _plan_out(b, s, ts, lambda bi, i: (bi, i))
    out = pl.pallas_call(
        functools.partial(_even_kernel, has_halo=has_halo, seq_rows=seq_rows, nc=512),
        grid=(b, n_i),
        in_specs=in_specs,
        out_specs=[pl.BlockSpec((None, ts, D_MODEL), lambda bi, i: (bi, i, 0))] + plan_specs,
        out_shape=[jax.ShapeDtypeStruct((b, s, D_MODEL), F32)] + plan_shapes,
        scratch_shapes=[pltpu.VMEM((ts, D_EVEN_IN), BF16), pltpu.VMEM((ts, D_A + D_B), BF16)],
        compiler_params=_cparams("arbitrary", "arbitrary"),
        name="even_layer",
    )(*args)
    return out[0].reshape(out_shape), out[1:]


ROUTE_TM = 256
ROUTE_SUBS = 4
ROUTE_STEP = ROUTE_TM * ROUTE_SUBS
ROUTE_PAD = 8
SORT_ROWS = ROUTE_TM + LANES
RUN_ROWS = SORT_ROWS + 32
ROUTE_ROWS = ROUTE_TM + 4 * ROUTE_PAD
XS_W = D_MODEL + LANES
GATE_LO = EXPERTS_PER_GROUP
DLOC_HI = 2 * EXPERTS_PER_GROUP
DLOC_RADIX = 16.0
FFN_BM = 512
FFN_HALF = FFN_BM // 2
RUN_SIZES = (256, 128, 64, 32, 16, 8)
TAB_W = 2 * N_GROUPS_MOE


def _round_up(x, m):
    return lax.div(x + (m - 1), m) * m


def _run_copies(tab_ref, tile, hbm_ref, vmem_ref, sem, to_hbm, wait):
    off = 0
    for g in range(N_GROUPS_MOE):
        start = tab_ref[tile * TAB_W + g]
        n = tab_ref[tile * TAB_W + N_GROUPS_MOE + g]
        for p in RUN_SIZES:
            done = n & (-2 * p)

            @pl.when((n & p) != 0)
            def _():
                v = vmem_ref.at[pl.ds(pl.multiple_of(off + done, ROUTE_PAD), p)]
                h = hbm_ref.at[pl.ds(pl.multiple_of(start + done, ROUTE_PAD), p)]
                cp = pltpu.make_async_copy(v, h, sem) if to_hbm else pltpu.make_async_copy(h, v, sem)
                if wait:
                    cp.wait()
                else:
                    cp.start()
        off = off + n


def _zero_fill(tab_ref, meta, zeros_ref, hbm_ref, sem, n_rows, wait):
    def copy(rows, dst_row):
        cp = pltpu.make_async_copy(zeros_ref.at[pl.ds(0, rows)],
                                   hbm_ref.at[pl.ds(pl.multiple_of(dst_row, ROUTE_PAD), rows)], sem)
        if wait:
            cp.wait()
        else:
            cp.start()

    end = 0
    for g in range(N_GROUPS_MOE):
        fill = tab_ref[meta + g]
        start = tab_ref[meta + N_GROUPS_MOE + g]
        end = start + _round_up(fill, FFN_BM)
        tail = end - start - fill
        for p in RUN_SIZES:
            pl.when((tail & p) != 0)(functools.partial(copy, p, start + fill + (tail & (-2 * p))))
    for k in range(n_rows // FFN_BM):
        pl.when(end + k * FFN_BM < n_rows)(functools.partial(copy, FFN_BM, end + k * FFN_BM))


def _wait_rows(n, hbm_ref, vmem_ref, sem, to_hbm):
    for p in RUN_SIZES:
        @pl.when((n & p) != 0)
        def _():
            v = vmem_ref.at[pl.ds(0, p)]
            h = hbm_ref.at[pl.ds(0, p)]
            (pltpu.make_async_copy(v, h, sem) if to_hbm else pltpu.make_async_copy(h, v, sem)).wait()


def _tile_rows(tab_ref, tile):
    n = 0
    for g in range(N_GROUPS_MOE):
        n = n + tab_ref[tile * TAB_W + N_GROUPS_MOE + g]
    return n


def _max4(v):
    return jnp.maximum(jnp.maximum(v[0], v[1]), jnp.maximum(v[2], v[3]))


def _first_of4(v, top):
    return jnp.where(v[0] == top, 0.0, jnp.where(v[1] == top, 1.0, jnp.where(v[2] == top, 2.0, 3.0)))


def _route_plan(x, m, plan_in, plan_out, tile0):
    g_ref, wrt_ref, brt_ref, upper_ref = plan_in
    haug_ref, dloc_ref, drow_ref, cnt_ref = plan_out
    rows = x.shape[0]
    tm = ROUTE_TM
    ng = N_GROUPS_MOE
    h = _norm_mod(x, g_ref[...], m[4:5], m[3:4])
    hb = h.astype(BF16)
    lt = lax.dot_general(wrt_ref[...], hb, NT_DIMS, preferred_element_type=F32) + brt_ref[...]
    gl = [lt[r:r + 1, :] for r in range(ng)]
    g_top = _max4(gl)
    g_idx = _first_of4(gl, g_top)
    g_w = 1.0 / (jnp.exp(gl[0] - g_top) + jnp.exp(gl[1] - g_top) + jnp.exp(gl[2] - g_top) + jnp.exp(gl[3] - g_top))
    ev = []
    for k in range(EXPERTS_PER_GROUP):
        cand = [lt[GATE_OFF + EXPERTS_PER_GROUP * r + k:GATE_OFF + EXPERTS_PER_GROUP * r + k + 1, :]
                for r in range(ng)]
        ev.append(jnp.where(g_idx == 0.0, cand[0], jnp.where(g_idx == 1.0, cand[1],
                            jnp.where(g_idx == 2.0, cand[2], cand[3]))))
    v1 = _max4(ev)
    i1 = _first_of4(ev, v1)
    rest = [jnp.where(i1 == float(k), NEG_BIG, ev[k]) for k in range(EXPERTS_PER_GROUP)]
    v2 = _max4(rest)
    i2 = _first_of4(rest, v2)
    e2 = jnp.exp(v2 - v1)
    w1 = 1.0 / (1.0 + e2)
    w2 = e2 * w1
    gates = [g_w * (jnp.where(i1 == float(k), w1, 0.0) + jnp.where(i2 == float(k), w2, 0.0))
             for k in range(EXPERTS_PER_GROUP)]

    sub8 = lax.broadcasted_iota(jnp.int32, (8, tm), 0).astype(F32)
    dlocs = []
    for sub in range(rows // tm):
        gi = g_idx[:, sub * tm:(sub + 1) * tm]
        hot = jnp.where(sub8 == gi, 1.0, 0.0)
        before = _bdot(hot.astype(BF16), upper_ref[...])
        dl = jnp.sum(before * hot, axis=0, keepdims=True)
        off = 0
        for g in range(ng):
            n_g = _round_up(jnp.sum(hot[g:g + 1, :]).astype(jnp.int32), ROUTE_PAD)
            cnt_ref[(tile0 + sub) * ng + g] = n_g
            dl = dl + jnp.where(gi == float(g), off.astype(F32) if g else 0.0, 0.0)
            off = off + n_g
        drow_ref[sub] = jnp.broadcast_to(dl, (8, tm))
        dlocs.append(dl)
    dloc = jnp.concatenate(dlocs, axis=1)
    d_hi = jnp.floor(dloc * (1.0 / DLOC_RADIX))
    g_hi = [gt.astype(BF16).astype(F32) for gt in gates]
    ex_rows = g_hi + [gt - gh for gt, gh in zip(gates, g_hi)] + [d_hi, dloc - DLOC_RADIX * d_hi]
    sub16 = lax.broadcasted_iota(jnp.int32, (16, rows), 0)
    ex_t = jnp.zeros((16, rows), F32)
    for r, row in enumerate(ex_rows):
        ex_t = jnp.where(sub16 == r, row, ex_t)
    ex_t = jnp.concatenate([ex_t, jnp.zeros((LANES - 16, rows), F32)], axis=0)
    extras = ex_t.T
    dloc_ref[...] = jnp.broadcast_to(
        DLOC_RADIX * extras[:, DLOC_HI:DLOC_HI + 1] + extras[:, DLOC_HI + 1:DLOC_HI + 2], (rows, LANES))
    haug_ref[...] = jnp.concatenate([hb, extras.astype(BF16)], axis=1)


def _plan_consts(g2, w_rt, b_rt):
    upper = jnp.asarray(np.triu(np.ones((ROUTE_TM, ROUTE_TM), np.float32), 1), BF16)
    return [g2, w_rt, b_rt, upper]


def _plan_out(b, s, rows, index):
    tiles = rows // ROUTE_TM
    shapes = [jax.ShapeDtypeStruct((b, s, XS_W), BF16), jax.ShapeDtypeStruct((b, s, LANES), F32),
              jax.ShapeDtypeStruct((b, s // ROUTE_TM, 8, ROUTE_TM), F32),
              jax.ShapeDtypeStruct((b * (s // ROUTE_TM) * N_GROUPS_MOE,), jnp.int32)]
    specs = [pl.BlockSpec((None, rows, XS_W), lambda *idx: (*index(*idx), 0)),
             pl.BlockSpec((None, rows, LANES), lambda *idx: (*index(*idx), 0)),
             pl.BlockSpec((None, tiles, 8, ROUTE_TM), lambda *idx: (*index(*idx), 0, 0)),
             pl.BlockSpec(memory_space=pltpu.SMEM)]
    return shapes, specs


def _flat_plan(plan):
    haug, dloc, drow, cnt = plan
    return (haug.reshape(-1, XS_W), dloc.reshape(-1, LANES), drow.reshape(-1, 8, ROUTE_TM), cnt)


def _dispatch_kernel(cp_ref, cs_ref, hp_ref, hs_ref, dp_ref, ds_ref, tab_ref, sorted_hbm,
                     sorted_ref, zeros_ref, fill_ref, sem_ref, zsem_ref, *, n_steps, n_p_steps):
    i = pl.program_id(0)
    tm = ROUTE_TM
    ng = N_GROUPS_MOE
    n_tiles = n_steps * ROUTE_SUBS
    n_p_tiles = n_p_steps * ROUTE_SUBS
    meta = n_tiles * TAB_W
    n_rows = sorted_hbm.shape[0]

    def count(tile, g):
        if isinstance(tile, int):
            return cp_ref[tile * ng + g] if tile < n_p_tiles else cs_ref[(tile - n_p_tiles) * ng + g]
        return jnp.where(tile < n_p_tiles, cp_ref[jnp.minimum(tile, n_p_tiles - 1) * ng + g],
                         cs_ref[jnp.maximum(tile - n_p_tiles, 0) * ng + g])

    @pl.when(i == 0)
    def _():
        start = 0
        for g in range(ng):
            fill = sum(count(t, g) for t in range(n_tiles))
            tab_ref[meta + g] = fill
            tab_ref[meta + ng + g] = start
            start = start + _round_up(fill, FFN_BM)
            fill_ref[g] = 0
        zeros_ref[...] = jnp.zeros_like(zeros_ref)

    is_prompt = i < n_p_steps
    row_f = lax.broadcasted_iota(jnp.int32, (ROUTE_ROWS, tm), 0).astype(F32)

    def sort_tile(h_ref, d_ref, sub):
        onehot = jnp.where(row_f == d_ref[sub][0:1, :], 1.0, 0.0).astype(BF16)
        sorted_ref[sub] = _bdot(onehot, h_ref[sub * tm:(sub + 1) * tm, :])

    for sub in range(ROUTE_SUBS):
        tile = i * ROUTE_SUBS + sub

        @pl.when(i >= 1)
        def _():
            _wait_rows(_tile_rows(tab_ref, tile - ROUTE_SUBS), sorted_hbm, sorted_ref.at[sub], sem_ref.at[sub], True)

        pl.when(is_prompt)(functools.partial(sort_tile, hp_ref, dp_ref, sub))
        pl.when(jnp.logical_not(is_prompt))(functools.partial(sort_tile, hs_ref, ds_ref, sub))
        for g in range(ng):
            n_g = count(tile, g)
            tab_ref[tile * TAB_W + g] = tab_ref[meta + ng + g] + fill_ref[g]
            tab_ref[tile * TAB_W + ng + g] = n_g
            fill_ref[g] = fill_ref[g] + n_g
        _run_copies(tab_ref, tile, sorted_hbm, sorted_ref.at[sub], sem_ref.at[sub], to_hbm=True, wait=False)

        @pl.when(i == n_steps - 1)
        def _():
            _wait_rows(_tile_rows(tab_ref, tile), sorted_hbm, sorted_ref.at[sub], sem_ref.at[sub], True)

    @pl.when(i == n_steps - 1)
    def _():
        _zero_fill(tab_ref, meta, zeros_ref, sorted_hbm, zsem_ref, n_rows, wait=False)
        _zero_fill(tab_ref, meta, zeros_ref, sorted_hbm, zsem_ref, n_rows, wait=True)


def _merged_specs(n_p_steps, n_s_steps, layer, sample_seq, step_of, width=D_MODEL):
    def p_map(*idx):
        return (jnp.minimum(step_of(*idx), n_p_steps - 1), 0)

    def s_map(*idx):
        return (jnp.clip(step_of(*idx) - n_p_steps, 0, n_s_steps - 1), 0)

    def mod_map(*idx):
        j = step_of(*idx)
        row = jnp.where(j < n_p_steps, 0, 1 + lax.div(jnp.maximum(j - n_p_steps, 0) * ROUTE_STEP, sample_seq))
        return (layer, row, 0, 0)

    return (pl.BlockSpec((ROUTE_STEP, width), p_map), pl.BlockSpec((ROUTE_STEP, width), s_map),
            pl.BlockSpec((None, None, 6, D_MODEL), mod_map))


def _dispatch(plan_p, plan_s, n_rows):
    haug_p, _, drow_p, cnt_p = plan_p
    haug_s, _, drow_s, cnt_s = plan_s
    n_p_steps = haug_p.shape[0] // ROUTE_STEP
    n_s_steps = haug_s.shape[0] // ROUTE_STEP
    n_steps = n_p_steps + n_s_steps
    n_tiles = n_steps * ROUTE_SUBS
    step_of = lambda i, cp, cs: i
    hp_spec, hs_spec, _ = _merged_specs(n_p_steps, n_s_steps, 0, 1, step_of, XS_W)
    drow_block = (ROUTE_SUBS, 8, ROUTE_TM)
    grid_spec = pltpu.PrefetchScalarGridSpec(
        num_scalar_prefetch=2,
        grid=(n_steps,),
        in_specs=[
            hp_spec, hs_spec,
            pl.BlockSpec(drow_block, lambda i, cp, cs: (jnp.minimum(i, n_p_steps - 1), 0, 0)),
            pl.BlockSpec(drow_block, lambda i, cp, cs: (jnp.clip(i - n_p_steps, 0, n_s_steps - 1), 0, 0)),
        ],
        out_specs=[pl.BlockSpec(memory_space=pltpu.SMEM), pl.BlockSpec(memory_space=pl.ANY)],
        scratch_shapes=[
            pltpu.VMEM((ROUTE_SUBS, ROUTE_ROWS, XS_W), F32),
            pltpu.VMEM((FFN_BM, XS_W), F32),
            pltpu.SMEM((N_GROUPS_MOE,), jnp.int32),
            pltpu.SemaphoreType.DMA((ROUTE_SUBS,)),
            pltpu.SemaphoreType.DMA(()),
        ],
    )
    return pl.pallas_call(
        functools.partial(_dispatch_kernel, n_steps=n_steps, n_p_steps=n_p_steps),
        grid_spec=grid_spec,
        out_shape=[
            jax.ShapeDtypeStruct(((n_tiles + 1) * TAB_W,), jnp.int32),
            jax.ShapeDtypeStruct((n_rows, XS_W), F32),
        ],
        compiler_params=_cparams("arbitrary"),
        name="moe_dispatch",
    )(cnt_p, cnt_s, haug_p, haug_s, drow_p, drow_s)


def _ffn_lookup(i, tab_ref, meta):
    fills = [tab_ref[meta + g] for g in range(N_GROUPS_MOE)]
    edges = []
    acc = 0
    for f in fills:
        acc = acc + lax.div(f + (FFN_BM - 1), FFN_BM)
        edges.append(acc)
    total = edges[-1]
    ii = jnp.minimum(i, total - 1)
    grp = sum((ii >= e).astype(jnp.int32) for e in edges[:-1])

    def pick(vals):
        return jnp.where(grp == 0, vals[0], jnp.where(grp == 1, vals[1], jnp.where(grp == 2, vals[2], vals[3])))

    first = pick([0] + edges[:-1])
    return grp, total, ii == first, pick(fills) - (ii - first) * FFN_BM


def _ffn_group_kernel(tab_ref, xs_ref, w1_ref, w3_ref, w2_ref, ys_ref, w1b_ref, w3b_ref, w2b_ref, *, meta):
    i = pl.program_id(0)
    _, total, first_of_group, valid = _ffn_lookup(i, tab_ref, meta)
    active = i < total

    @pl.when(jnp.logical_and(active, first_of_group))
    def _():
        w1b_ref[...] = w1_ref[...].astype(BF16)
        w3b_ref[...] = w3_ref[...].astype(BF16)
        w2b_ref[...] = w2_ref[...].astype(BF16)

    def run(rows):
        hb = xs_ref[0:rows, 0:D_MODEL].astype(BF16)
        ex = xs_ref[0:rows, D_MODEL:XS_W]
        hid = []
        for e in range(EXPERTS_PER_GROUP):
            a = _bdot(hb, w1b_ref[e])
            b = _bdot(hb, w3b_ref[e])
            gate = ex[:, e:e + 1] + ex[:, GATE_LO + e:GATE_LO + e + 1]
            hid.append(((a * jax.nn.sigmoid(a)) * b * gate).astype(BF16))
        ys_ref[0:rows, :] = _bdot(jnp.concatenate(hid, axis=1), w2b_ref[...].reshape(-1, D_MODEL))

    @pl.when(jnp.logical_and(active, valid > FFN_HALF))
    def _():
        run(FFN_BM)

    @pl.when(jnp.logical_and(active, valid <= FFN_HALF))
    def _():
        run(FFN_HALF)
        ys_ref[FFN_HALF:, :] = jnp.zeros((FFN_BM - FFN_HALF, D_MODEL), F32)

    @pl.when(jnp.logical_not(active))
    def _():
        ys_ref[...] = jnp.zeros_like(ys_ref)


def _ffn_group(tab, xs, w1, w3, w2, layer, meta):
    e4 = EXPERTS_PER_GROUP
    n_blocks = xs.shape[0] // FFN_BM
    group_of = lambda i, tab_ref: _ffn_lookup(i, tab_ref, meta)[0]
    grid_spec = pltpu.PrefetchScalarGridSpec(
        num_scalar_prefetch=1,
        grid=(n_blocks,),
        in_specs=[
            pl.BlockSpec((FFN_BM, XS_W), lambda i, tab_ref: (i, 0)),
            pl.BlockSpec((None, e4, D_MODEL, D_EXPERT), lambda i, tab_ref: (layer, group_of(i, tab_ref), 0, 0)),
            pl.BlockSpec((None, e4, D_MODEL, D_EXPERT), lambda i, tab_ref: (layer, group_of(i, tab_ref), 0, 0)),
            pl.BlockSpec((None, e4, D_EXPERT, D_MODEL), lambda i, tab_ref: (layer, group_of(i, tab_ref), 0, 0)),
        ],
        out_specs=pl.BlockSpec((FFN_BM, D_MODEL), lambda i, tab_ref: (i, 0)),
        scratch_shapes=[
            pltpu.VMEM((e4, D_MODEL, D_EXPERT), BF16),
            pltpu.VMEM((e4, D_MODEL, D_EXPERT), BF16),
            pltpu.VMEM((e4, D_EXPERT, D_MODEL), BF16),
        ],
    )
    return pl.pallas_call(
        functools.partial(_ffn_group_kernel, meta=meta),
        grid_spec=grid_spec,
        out_shape=jax.ShapeDtypeStruct((xs.shape[0], D_MODEL), F32),
        compiler_params=_cparams("arbitrary"),
        name="moe_ffn",
    )(tab, xs, w1, w3, w2)


def _fetch_runs(tab_ref, ys_ref, runs_ref, sem_ref, tile, slot):
    _run_copies(tab_ref, tile, ys_ref, runs_ref.at[slot], sem_ref.at[slot], to_hbm=False, wait=False)


def _unsort(tab_ref, ys_ref, runs_ref, sem_ref, dloc_col, tile, slot):
    covered = _tile_rows(tab_ref, tile)
    _wait_rows(covered, ys_ref, runs_ref.at[slot], sem_ref.at[slot], False)
    runs_ref[slot, pl.ds(pl.multiple_of(covered, ROUTE_PAD), LANES), :] = jnp.zeros((LANES, D_MODEL), F32)
    yb = runs_ref[slot, 0:SORT_ROWS, :].astype(BF16)
    row_f = lax.broadcasted_iota(jnp.int32, (ROUTE_TM, SORT_ROWS), 1).astype(F32)
    return _bdot(jnp.where(row_f == dloc_col, 1.0, 0.0).astype(BF16), yb)


def _combine_kernel(tab_ref, *refs, n_steps, first_step, n_p_steps, final_norm):
    with_prompt = first_step < n_p_steps
    if with_prompt:
        xp_ref, xs_ref, mod_ref, dp_ref, ds_ref, fg_ref, ys_ref, op_ref, os_ref, runs_ref, sem_ref = refs
    else:
        xs_ref, mod_ref, ds_ref, fg_ref, ys_ref, os_ref, runs_ref, sem_ref = refs
    i = pl.program_id(0)
    par = lax.rem(i, 2)
    step = first_step + i
    tm = ROUTE_TM

    def fetch(st, parity):
        for sub in range(ROUTE_SUBS):
            _fetch_runs(tab_ref, ys_ref, runs_ref, sem_ref, st * ROUTE_SUBS + sub, parity * ROUTE_SUBS + sub)

    @pl.when(i == 0)
    def _():
        fetch(first_step, 0)

    @pl.when(i + 1 < n_steps)
    def _():
        fetch(step + 1, 1 - par)

    is_prompt = step < n_p_steps
    dloc = jnp.where(is_prompt, dp_ref[:, 0:1], ds_ref[:, 0:1]) if with_prompt else ds_ref[:, 0:1]
    parts = [_unsort(tab_ref, ys_ref, runs_ref, sem_ref, dloc[sub * tm:(sub + 1) * tm],
                     step * ROUTE_SUBS + sub, par * ROUTE_SUBS + sub) for sub in range(ROUTE_SUBS)]
    delta = mod_ref[5:6, :] * jnp.concatenate(parts, axis=0)

    def finish(x_ref, o_ref):
        x2 = x_ref[...] + delta
        o_ref[...] = _rms(x2, fg_ref[...]) if final_norm else x2

    if with_prompt:
        pl.when(is_prompt)(functools.partial(finish, xp_ref, op_ref))
        pl.when(jnp.logical_not(is_prompt))(functools.partial(finish, xs_ref, os_ref))
    else:
        finish(xs_ref, os_ref)


def _combine(tab, xp, xs, mod, dloc_p, dloc_s, final_g, ys, layer, sample_seq, final_norm, with_prompt):
    n_p_steps = xp.shape[0] // ROUTE_STEP
    n_s_steps = xs.shape[0] // ROUTE_STEP
    first_step = 0 if with_prompt else n_p_steps
    n_steps = n_p_steps + n_s_steps - first_step
    step_of = lambda i, tab_ref: i + first_step
    p_spec, s_spec, mod_spec = _merged_specs(n_p_steps, n_s_steps, layer, sample_seq, step_of)
    dp_spec, ds_spec, _ = _merged_specs(n_p_steps, n_s_steps, layer, sample_seq, step_of, LANES)
    streams = [p_spec, s_spec] if with_prompt else [s_spec]
    dlocs = [dp_spec, ds_spec] if with_prompt else [ds_spec]
    grid_spec = pltpu.PrefetchScalarGridSpec(
        num_scalar_prefetch=1,
        grid=(n_steps,),
        in_specs=streams + [mod_spec] + dlocs + [
            pl.BlockSpec((1, D_MODEL), lambda i, tab_ref: (0, 0)),
            pl.BlockSpec(memory_space=pl.ANY),
        ],
        out_specs=streams,
        scratch_shapes=[
            pltpu.VMEM((2 * ROUTE_SUBS, RUN_ROWS, D_MODEL), F32),
            pltpu.SemaphoreType.DMA((2 * ROUTE_SUBS,)),
        ],
    )
    arrays = [xp, xs] if with_prompt else [xs]
    dloc_arrays = [dloc_p, dloc_s] if with_prompt else [dloc_s]
    return pl.pallas_call(
        functools.partial(_combine_kernel, n_steps=n_steps, first_step=first_step, n_p_steps=n_p_steps,
                          final_norm=final_norm),
        grid_spec=grid_spec,
        out_shape=[jax.ShapeDtypeStruct(a.shape, F32) for a in arrays],
        compiler_params=_cparams("arbitrary"),
        name="moe_combine",
    )(tab, *arrays, mod, *dloc_arrays, final_g, ys)


def _moe(xp, xs, plan_p, plan_s, mod, w1, w3, w2, final_g, layer, sample_seq, final_norm, defer_prompt):
    plan_p, plan_s = _flat_plan(plan_p), _flat_plan(plan_s)
    t = xp.shape[0] + xs.shape[0]
    n_tiles = t // ROUTE_TM
    max_rows = t + N_GROUPS_MOE * (ROUTE_PAD - 1) * n_tiles
    n_rows = (-(-max_rows // FFN_BM) + N_GROUPS_MOE) * FFN_BM
    tab, sorted_x = _dispatch(plan_p, plan_s, n_rows)
    ys = _ffn_group(tab, sorted_x, w1, w3, w2, layer, n_tiles * TAB_W)
    out = _combine(tab, xp, xs, mod, plan_p[1], plan_s[1], final_g, ys, layer, sample_seq, final_norm,
                   not defer_prompt)
    if defer_prompt:
        return None, out[0], (tab, plan_p[1], ys)
    return out[0], out[1], None


def _rope_tables(seq):
    half = QK_ROPE // 2
    nf = half // 2
    inv = ROPE_BASE ** (-np.arange(nf, dtype=np.float64) / nf)
    pos = np.arange(seq)
    row = (pos // GRID_W).astype(np.float64)
    col = (pos % GRID_W).astype(np.float64)
    cos = np.ones((seq, HEAD_PAD), np.float64)
    sin_a = np.zeros((seq, HEAD_PAD), np.float64)
    sin_b = np.zeros((seq, HEAD_PAD), np.float64)
    for part, p in enumerate((row, col)):
        ang = p[:, None] * inv[None, :]
        base = ROPE_OFF + part * half
        cos[:, base:base + nf] = np.cos(ang)
        cos[:, base + nf:base + half] = np.cos(ang)
        sin_a[:, base:base + nf] = -np.sin(ang)
        sin_b[:, base + nf:base + half] = np.sin(ang)
    return tuple(jnp.asarray(a, F32) for a in (cos, sin_a, sin_b))


def _apply_rope(x, cos, sin_a, sin_b, reps):
    nf = QK_ROPE // 4
    width = x.shape[1]
    if reps > 1:
        cos, sin_a, sin_b = (jnp.concatenate([a] * reps, axis=1) for a in (cos, sin_a, sin_b))
    return x * cos + pltpu.roll(x, width - nf, axis=1) * sin_a + pltpu.roll(x, nf, axis=1) * sin_b


def _odd_in_kernel(*refs, rope, emit_cache):
    x_ref, mod_ref, g_ref, w_ref, qg_ref, wq_ref, kg_ref, wk_ref, wv_ref, cs_ref = refs[:10]
    refs = refs[10:]
    if rope:
        cos_ref, sa_ref, sb_ref = refs[:3]
        refs = refs[3:]
    y_ref, q_ref, k_ref, v_ref = refs[:4]
    refs = refs[4:]
    m = mod_ref[...]
    h = _norm_mod(x_ref[...], g_ref[...], m[1:2], m[0:1])
    z = _bdot(h.astype(BF16), w_ref[...])
    zc = z[:, 0:D_C]
    qc = z[:, D_C:D_C + Q_LORA]
    kvc = z[:, D_C + Q_LORA:D_C + Q_LORA + KV_LORA]
    kpe = z[:, D_C + Q_LORA + KV_LORA:]
    q = _bdot(_rms(qc, qg_ref[...]).astype(BF16), wq_ref[...])
    kvn = _rms(kvc, kg_ref[...])
    if emit_cache:
        ckv_ref, kpe_ref = refs
        ckv_ref[...] = kvn.reshape(ckv_ref.shape)
        kpe_ref[...] = kpe[:, ROPE_OFF:ROPE_OFF + QK_ROPE].reshape(kpe_ref.shape)
    if rope:
        tabs = (cos_ref[...], sa_ref[...], sb_ref[...])
        q = _apply_rope(q, *tabs, reps=N_HEADS)
        kpe = _apply_rope(kpe, *tabs, reps=1)
    kvb = kvn.astype(BF16)
    k = _bdot(kvb, wk_ref[...]) + jnp.concatenate([kpe] * N_HEADS, axis=1)
    scale = math.log2(math.e) / math.sqrt(QK_NOPE + QK_ROPE)
    q_ref[...] = (q * scale).astype(BF16)
    k_ref[...] = k.astype(BF16)
    v_ref[...] = _bdot(kvb, wv_ref[...]).astype(BF16)
    y = _bdot(zc.astype(BF16), cs_ref[...])
    y_ref[0, :, :] = y[:, 0:D_C].astype(BF16)
    y_ref[1, :, :] = y[:, D_C:2 * D_C].astype(BF16)


def _odd_in(x3, mod, g, w_in, q_g, w_q, kv_g, w_k, w_v, cs, stream, layer, rope_tabs, emit_cache):
    b, s, _ = x3.shape
    tm = min(s, 1024)
    n_i = s // tm
    rope = rope_tabs is not None
    const = lambda a: pl.BlockSpec(a.shape, lambda bi, i: (0,) * a.ndim)
    in_specs = [
        pl.BlockSpec((None, tm, D_MODEL), lambda bi, i: (bi, i, 0)),
        _mod_spec(stream, layer),
        const(g), const(w_in), const(q_g), const(w_q), const(kv_g), const(w_k), const(w_v), const(cs),
    ]
    args = [x3, mod, g, w_in, q_g, w_q, kv_g, w_k, w_v, cs]
    if rope:
        in_specs += [pl.BlockSpec((tm, HEAD_PAD), lambda bi, i: (i, 0))] * 3
        args += list(rope_tabs)
    hq = N_HEADS * HEAD_PAD
    out_specs = [
        pl.BlockSpec((None, 2, tm, D_C), lambda bi, i: (bi, 0, i, 0)),
        pl.BlockSpec((None, tm, hq), lambda bi, i: (bi, i, 0)),
        pl.BlockSpec((None, tm, hq), lambda bi, i: (bi, i, 0)),
        pl.BlockSpec((None, tm, N_HEADS * V_DIM), lambda bi, i: (bi, i, 0)),
    ]
    out_shape = [
        jax.ShapeDtypeStruct((b, 2, s, D_C), BF16),
        jax.ShapeDtypeStruct((b, s, hq), BF16),
        jax.ShapeDtypeStruct((b, s, hq), BF16),
        jax.ShapeDtypeStruct((b, s, N_HEADS * V_DIM), BF16),
    ]
    if emit_cache:
        out_specs += [
            pl.BlockSpec((None, tm, KV_LORA), lambda bi, i: (bi, i, 0)),
            pl.BlockSpec((None, tm, QK_ROPE), lambda bi, i: (bi, i, 0)),
        ]
        out_shape += [
            jax.ShapeDtypeStruct((b, s, KV_LORA), F32),
            jax.ShapeDtypeStruct((b, s, QK_ROPE), F32),
        ]
    return pl.pallas_call(
        functools.partial(_odd_in_kernel, rope=rope, emit_cache=emit_cache),
        grid=(b, n_i),
        in_specs=in_specs,
        out_specs=out_specs,
        out_shape=out_shape,
        compiler_params=_cparams("parallel", "parallel"),
        name="odd_in",
    )(*args)


def _cache_kv_kernel(c_ref, p_ref, wk_ref, wv_ref, k_ref, v_ref):
    cb = c_ref[...].astype(BF16)
    k = _bdot(cb, wk_ref[...]) + jnp.concatenate([p_ref[...]] * N_HEADS, axis=1)
    k_ref[...] = k.astype(BF16)
    v_ref[...] = _bdot(cb, wv_ref[...]).astype(BF16)


def _cache_kv(ckv, kpe_blk, w_k, w_v):
    b, p, _ = ckv.shape
    hq = N_HEADS * HEAD_PAD
    return pl.pallas_call(
        _cache_kv_kernel,
        grid=(b,),
        in_specs=[
            pl.BlockSpec((None, p, KV_LORA), lambda bi: (bi, 0, 0)),
            pl.BlockSpec((None, p, HEAD_PAD), lambda bi: (bi, 0, 0)),
            pl.BlockSpec(w_k.shape, lambda bi: (0, 0)),
            pl.BlockSpec(w_v.shape, lambda bi: (0, 0)),
        ],
        out_specs=[
            pl.BlockSpec((None, p, hq), lambda bi: (bi, 0, 0)),
            pl.BlockSpec((None, p, N_HEADS * V_DIM), lambda bi: (bi, 0, 0)),
        ],
        out_shape=[
            jax.ShapeDtypeStruct((b, p, hq), BF16),
            jax.ShapeDtypeStruct((b, p, N_HEADS * V_DIM), BF16),
        ],
        compiler_params=_cparams("parallel"),
        name="cache_kv",
    )(ckv, kpe_blk, w_k, w_v)


def _odd_mix_kernel(*refs, with_cache, n_seq=1):
    q_ref, k_ref, v_ref = refs[:3]
    refs = refs[3:]
    if with_cache:
        kc_ref, vc_ref = refs[:2]
        refs = refs[2:]
    plan_tile = None
    if with_cache:
        y_ref, f_ref, x_ref, mod_ref, wo_ref = refs[:5]
        plan_in, (o_ref, *plan_out), (a_ref, vx_ref, vcx_ref) = refs[5:9], refs[9:14], refs[14:]
        plan_tile = pl.program_id(0) * pl.num_programs(1) + pl.program_id(1)

        @pl.when(pl.program_id(1) == 0)
        def _():
            for src, dst in ((v_ref, vx_ref), (vc_ref, vcx_ref)):
                one = lax.broadcasted_iota(jnp.int32, (src.shape[0], PAIR_W - DEN_COL), 1) == 0
                for pair in range(N_HEADS // 2):
                    dst[:, pair * PAIR_W:pair * PAIR_W + DEN_COL] = src[:, pair * DEN_COL:(pair + 1) * DEN_COL]
                    dst[:, pair * PAIR_W + DEN_COL:(pair + 1) * PAIR_W] = jnp.where(one, 1.0, 0.0).astype(BF16)
    else:
        y_ref, f_ref, x_ref, mod_ref, wo_ref, o_ref, a_ref = refs[:7]
        if len(refs) > 7:
            plan_in, plan_out, plan_tile = refs[7:]
    tq = q_ref.shape[0] // n_seq
    lane = lax.broadcasted_iota(jnp.int32, (tq, 2 * V_DIM), 1)
    f_parts = []
    for sq in range(n_seq):
        rows = slice(sq * tq, (sq + 1) * tq)
        for pair in range(N_HEADS // 2):
            vcols = slice(pair * PAIR_W, (pair + 1) * PAIR_W)
            outs = []
            for h in (2 * pair, 2 * pair + 1):
                hcols = slice(h * HEAD_PAD, (h + 1) * HEAD_PAD)
                qh = q_ref[rows, hcols]
                if with_cache:
                    s = lax.dot_general(qh, k_ref[:, hcols], NT_DIMS, preferred_element_type=F32)
                    sc = lax.dot_general(qh, kc_ref[:, hcols], NT_DIMS, preferred_element_type=F32)
                    top = jnp.maximum(jnp.max(s, axis=-1, keepdims=True), jnp.max(sc, axis=-1, keepdims=True))
                    acc = _bdot(jnp.exp2((s - top).astype(BF16)), vx_ref[:, vcols])
                    acc = acc + _bdot(jnp.exp2((sc - top).astype(BF16)), vcx_ref[:, vcols])
                    outs.append(acc[:, 0:2 * V_DIM] / acc[:, DEN_COL:DEN_COL + 1])
                else:
                    s = lax.dot_general(qh, k_ref[rows, hcols], NT_DIMS, preferred_element_type=F32)
                    p = jnp.exp2(s - jnp.max(s, axis=-1, keepdims=True))
                    den = jnp.sum(p, axis=-1, keepdims=True)
                    outs.append(_bdot(p.astype(BF16), v_ref[rows, pair * DEN_COL:(pair + 1) * DEN_COL]) / den)
            a_ref[rows, pair * 2 * V_DIM:(pair + 1) * 2 * V_DIM] = (
                jnp.where(lane < V_DIM, outs[0], outs[1]).astype(BF16))
        if n_seq > 1:
            f_parts.append(_bdot(f_ref[:, 0:tq], y_ref[0, rows, :]) + _bdot(f_ref[:, tq:], y_ref[1, rows, :]))
    f = jnp.concatenate(f_parts, axis=0) if n_seq > 1 else _bdot(f_ref[...], y_ref[...].reshape(-1, D_C))
    o = _bdot(f.astype(BF16), wo_ref[0:D_C, :]) + _bdot(a_ref[...], wo_ref[D_C:, :])
    x1 = x_ref[...] + mod_ref[2:3, :] * o
    o_ref[...] = x1
    if plan_tile is not None:
        _route_plan(x1, mod_ref[...], plan_in, plan_out, plan_tile)


WHOLE_SEQS = 2


def _odd_whole_kernel(tab_ref, x_ref, mod_ref, g_ref, w_ref, qg_ref, wq_ref, kg_ref, wk_ref, wv_ref, cs_ref, f_ref,
                      wo_ref, modp_ref, dloc_ref, ys_ref, pg_ref, pw_ref, pb_ref, pu_ref,
                      o_ref, ckv_ref, kpe_ref, ph_ref, pd_ref, pr_ref, pc_ref,
                      y_ref, q_ref, k_ref, v_ref, a_ref, x2_ref, runs_ref, sem_ref):
    bi = pl.program_id(0)
    par = lax.rem(bi, 2)
    tm = ROUTE_TM

    def fetch(step, parity):
        for sq in range(WHOLE_SEQS):
            _fetch_runs(tab_ref, ys_ref, runs_ref, sem_ref, step * WHOLE_SEQS + sq, parity * WHOLE_SEQS + sq)

    @pl.when(bi == 0)
    def _():
        fetch(0, 0)

    @pl.when(bi + 1 < pl.num_programs(0))
    def _():
        fetch(bi + 1, 1 - par)

    for sq in range(WHOLE_SEQS):
        rows = slice(sq * tm, (sq + 1) * tm)
        moe = _unsort(tab_ref, ys_ref, runs_ref, sem_ref, dloc_ref[rows, 0:1], bi * WHOLE_SEQS + sq,
                      par * WHOLE_SEQS + sq)
        x2_ref[rows, :] = x_ref[rows, :] + modp_ref[5:6, :] * moe
    _odd_in_kernel(x2_ref, mod_ref, g_ref, w_ref, qg_ref, wq_ref, kg_ref, wk_ref, wv_ref, cs_ref,
                   y_ref, q_ref, k_ref, v_ref, ckv_ref, kpe_ref, rope=False, emit_cache=True)
    _odd_mix_kernel(q_ref, k_ref, v_ref, y_ref, f_ref, x2_ref, mod_ref, wo_ref, o_ref, a_ref,
                    (pg_ref, pw_ref, pb_ref, pu_ref), (ph_ref, pd_ref, pr_ref, pc_ref), bi * WHOLE_SEQS,
                    with_cache=False, n_seq=WHOLE_SEQS)


def _odd_whole(x3, pending, mod, g, w_in, q_g, w_q, kv_g, w_k, w_v, cs, fmat, w_out, plan_consts, stream, layer):
    tab, dloc, ys = pending
    b, seq, _ = x3.shape
    assert seq == ROUTE_TM and b % WHOLE_SEQS == 0 and stream.shared_cond
    b, s = b // WHOLE_SEQS, seq * WHOLE_SEQS
    x3 = x3.reshape(b, s, D_MODEL)
    hq = N_HEADS * HEAD_PAD
    hv = N_HEADS * V_DIM
    const = lambda a: pl.BlockSpec(a.shape, lambda bi, tab_ref: (0,) * a.ndim)
    row_block = lambda w: pl.BlockSpec((None, s, w), lambda bi, tab_ref: (bi, 0, 0))
    mod_block = lambda lyr: pl.BlockSpec((None, None, 6, D_MODEL), lambda bi, tab_ref: (lyr, 0, 0, 0))
    cache_block = lambda w: pl.BlockSpec((WHOLE_SEQS, 1, seq, w), lambda bi, tab_ref: (bi, 0, 0, 0))
    plan_shapes, plan_specs = _plan_out(b, s, s, lambda bi, tab_ref: (bi, 0))
    grid_spec = pltpu.PrefetchScalarGridSpec(
        num_scalar_prefetch=1,
        grid=(b,),
        in_specs=[
            row_block(D_MODEL), mod_block(layer),
            const(g), const(w_in), const(q_g), const(w_q), const(kv_g), const(w_k), const(w_v), const(cs),
            const(fmat), const(w_out),
            mod_block(layer - 1),
            pl.BlockSpec((s, LANES), lambda bi, tab_ref: (bi, 0)),
            pl.BlockSpec(memory_space=pl.ANY),
        ] + [const(a) for a in plan_consts],
        out_specs=[row_block(D_MODEL), cache_block(KV_LORA), cache_block(QK_ROPE)] + plan_specs,
        scratch_shapes=[
            pltpu.VMEM((2, s, D_C), BF16),
            pltpu.VMEM((s, hq), BF16),
            pltpu.VMEM((s, hq), BF16),
            pltpu.VMEM((s, hv), BF16),
            pltpu.VMEM((s, hv), BF16),
            pltpu.VMEM((s, D_MODEL), F32),
            pltpu.VMEM((2 * WHOLE_SEQS, RUN_ROWS, D_MODEL), F32),
            pltpu.SemaphoreType.DMA((2 * WHOLE_SEQS,)),
        ],
    )
    out = pl.pallas_call(
        _odd_whole_kernel,
        grid_spec=grid_spec,
        out_shape=[
            jax.ShapeDtypeStruct((b, s, D_MODEL), F32),
            jax.ShapeDtypeStruct((b * WHOLE_SEQS, 1, seq, KV_LORA), F32),
            jax.ShapeDtypeStruct((b * WHOLE_SEQS, 1, seq, QK_ROPE), F32),
        ] + plan_shapes,
        compiler_params=_cparams("arbitrary"),
        name="odd_whole",
    )(tab, x3, mod, g, w_in, q_g, w_q, kv_g, w_k, w_v, cs, fmat, w_out, mod, dloc, ys, *plan_consts)
    return out[0].reshape(b * WHOLE_SEQS, seq, D_MODEL), out[1], out[2], out[3:]


def _odd_mix(q, k, v, kc, vc, y, fmat, x3, mod, w_out, plan_consts, stream, layer):
    b, s, hq = q.shape
    tq = min(s, 256)
    n_i = s // tq
    with_cache = kc is not None
    hv = N_HEADS * V_DIM
    mode = dict(pipeline_mode=pl.Buffered(1)) if with_cache else {}

    def per_batch(rows, cols):
        return pl.BlockSpec((None, rows, cols), lambda bi, i: (bi, 0, 0), **mode)

    in_specs = [pl.BlockSpec((None, tq, hq), lambda bi, i: (bi, i, 0)), per_batch(s, hq), per_batch(s, hv)]
    args = [q, k, v]
    if with_cache:
        p = kc.shape[1]
        in_specs += [per_batch(p, hq), per_batch(p, hv)]
        args += [kc, vc]
    in_specs += [
        per_batch(2 * s, D_C),
        pl.BlockSpec((tq, 2 * s), lambda bi, i: (i, 0)),
        pl.BlockSpec((None, tq, D_MODEL), lambda bi, i: (bi, i, 0)),
        _mod_spec(stream, layer),
        pl.BlockSpec(w_out.shape, lambda bi, i: (0, 0), **mode),
    ]
    in_specs += [pl.BlockSpec(a.shape, lambda bi, i: (0,) * a.ndim) for a in plan_consts]
    args += [y, fmat, x3, mod, w_out] + plan_consts
    plan_shapes, plan_specs = _plan_out(b, s, tq, lambda bi, i: (bi, i))
    out = pl.pallas_call(
        functools.partial(_odd_mix_kernel, with_cache=with_cache),
        grid=(b, n_i),
        in_specs=in_specs,
        out_specs=[pl.BlockSpec((None, tq, D_MODEL), lambda bi, i: (bi, i, 0))] + plan_specs,
        out_shape=[jax.ShapeDtypeStruct((b, s, D_MODEL), F32)] + plan_shapes,
        scratch_shapes=[pltpu.VMEM((tq, hv), BF16)] + (
            [pltpu.VMEM((s, VX_W), BF16), pltpu.VMEM((kc.shape[1], VX_W), BF16)] if with_cache else []),
        compiler_params=_cparams("arbitrary", "arbitrary"),
        name="odd_mix",
    )(*args)
    return out[0], out[1:]


def _dft_tables(seq):
    jc = np.arange(C_GW)
    ang_c = 2.0 * np.pi * np.outer(jc, jc) / C_GW
    eye = np.eye(C_GROUPS)
    cs = np.concatenate([np.kron(eye, np.cos(ang_c)), np.kron(eye, np.sin(ang_c))], axis=1)
    jn = np.arange(seq)
    ang_n = 2.0 * np.pi * (np.outer(jn, jn) % seq) / seq
    scale = 1.0 / math.sqrt(seq * C_GW)
    fmat = np.concatenate([np.cos(ang_n), -np.sin(ang_n)], axis=1) * scale
    return jnp.asarray(cs, F32).astype(BF16), jnp.asarray(fmat, F32).astype(BF16)


def _odd_weights(w_in, w_uq, w_ukv):
    d = w_in.shape[0]
    base = D_C + Q_LORA + KV_LORA
    kpe_blk = jnp.zeros((d, HEAD_PAD), w_in.dtype).at[:, ROPE_OFF:ROPE_OFF + QK_ROPE].set(w_in[:, base:])
    w_in_p = jnp.concatenate([w_in[:, :base], kpe_blk], axis=1).astype(BF16)
    qh = w_uq.reshape(Q_LORA, N_HEADS, QK_NOPE + QK_ROPE)
    w_q = jnp.pad(qh, ((0, 0), (0, 0), (0, HEAD_PAD - QK_NOPE - QK_ROPE))).reshape(Q_LORA, -1).astype(BF16)
    kvh = w_ukv.reshape(KV_LORA, N_HEADS, QK_NOPE + V_DIM)
    w_k = jnp.pad(kvh[:, :, :QK_NOPE], ((0, 0), (0, 0), (0, HEAD_PAD - QK_NOPE))).reshape(KV_LORA, -1)
    w_v = kvh[:, :, QK_NOPE:].reshape(KV_LORA, -1)
    return w_in_p, w_q, w_k.astype(BF16), w_v.astype(BF16)


ROUTER_ROWS = 32


def _router_weights(wg, bg, we, be):
    d = wg.shape[0]
    w = jnp.concatenate([wg, we.reshape(d, N_EXPERTS)], axis=1).T
    w = jnp.pad(w, ((0, ROUTER_ROWS - w.shape[0]), (0, 0))).astype(BF16)
    b = jnp.concatenate([bg, be.reshape(N_EXPERTS)])
    b = jnp.pad(b, (0, ROUTER_ROWS - b.shape[0])).reshape(ROUTER_ROWS, 1).astype(F32)
    return w, b


def kernel(x_prompt, x_sample, cache_ckv, cache_kpe, c, c_ctx, mod_w, mod_b, norm1_g, norm2_g,
           ev_w_in, ev_conv_w, ev_sgu_norm_g, ev_sgu_w, ev_sgu_b, ev_w_out,
           od_w_in, od_q_norm_g, od_w_uq, od_kv_norm_g, od_w_ukv, od_w_out,
           moe_wg, moe_bg, moe_we, moe_be, moe_w1, moe_w3, moe_w2, final_norm_g):
    bp, n_p, d = x_prompt.shape
    bs, n_s, _ = x_sample.shape
    streams = [(_Stream(bp, n_p, True), x_prompt), (_Stream(bs, n_s, False), x_sample)]

    n_rows = 1 + bs
    mod = _adaln(jnp.concatenate([c_ctx[None, :], c], axis=0), mod_w, mod_b, n_rows)

    final_g = final_norm_g.reshape(1, d)
    xs = [x for _, x in streams]
    new_ckv, new_kpe = [], []
    pending = None
    plans = [None, None]
    for l in range(DEPTH):
        j = l // 2
        g1 = norm1_g[l].reshape(1, d)
        g2 = norm2_g[l].reshape(1, d)
        plan_consts = _plan_consts(g2, *_router_weights(moe_wg[l], moe_bg[l], moe_we[l], moe_be[l]))
        last = l == DEPTH - 1
        if l % 2 == 0:
            w_in = ev_w_in[j].astype(BF16)
            w_out = ev_w_out[j].astype(BF16)
            sgu_w = ev_sgu_w[j].astype(BF16)
            sgu_g = ev_sgu_norm_g[j].reshape(1, D_B)
            sgu_bias = jnp.repeat(ev_sgu_b[j].T, D_B // B_GROUPS, axis=1)
            for si, (st, _) in enumerate(streams):
                xs[si], plans[si] = _even_layer(xs[si], mod, g1, w_in, ev_conv_w[j], sgu_g, sgu_w, sgu_bias, w_out,
                                                plan_consts, st, l)
        else:
            w_in, w_q, w_k, w_v = _odd_weights(od_w_in[j], od_w_uq[j], od_w_ukv[j])
            w_out = od_w_out[j].astype(BF16)
            q_g = od_q_norm_g[j].reshape(1, Q_LORA)
            kv_g = od_kv_norm_g[j].reshape(1, KV_LORA)
            for si, (st, _) in enumerate(streams):
                x3 = xs[si]
                cs, fmat = _dft_tables(st.seq)
                if st.shared_cond:
                    xs[si], ckv, kpe, plans[si] = _odd_whole(x3, pending, mod, g1, w_in, q_g, w_q, kv_g, w_k, w_v, cs,
                                                             fmat, w_out, plan_consts, st, l)
                    new_ckv.append(ckv)
                    new_kpe.append(kpe)
                    continue
                y, q, k, v = _odd_in(x3, mod, g1, w_in, q_g, w_q, kv_g, w_k, w_v, cs, st, l,
                                     _rope_tables(st.seq), False)
                kpe_blk = jnp.pad(cache_kpe[:, j], ((0, 0), (0, 0), (ROPE_OFF, HEAD_PAD - ROPE_OFF - QK_ROPE)))
                kc, vc = _cache_kv(cache_ckv[:, j], kpe_blk, w_k, w_v)
                xs[si], plans[si] = _odd_mix(q, k, v, kc, vc, y.reshape(st.batch, 2 * st.seq, D_C), fmat, x3, mod,
                                             w_out, plan_consts, st, l)
        defer = not last and (l + 1) % 2 == 1
        x2p, x2s, pending = _moe(xs[0].reshape(bp * n_p, d), xs[1].reshape(bs * n_s, d), plans[0], plans[1], mod,
                                 moe_w1, moe_w3, moe_w2, final_g, l, n_s, last, defer)
        xs = [xs[0] if defer else x2p.reshape(bp, n_p, d), x2s.reshape(bs, n_s, d)]
    return (xs[0], xs[1], jnp.concatenate(new_ckv, axis=1), jnp.concatenate(new_kpe, axis=1))
```

```python
import functools
import math

import numpy as np
import jax
import jax.numpy as jnp
from jax import lax
from jax.experimental import pallas as pl
from jax.experimental.pallas import tpu as pltpu

D_MODEL = 1024
DEPTH = 2
GRID_W = 64
D_A = D_MODEL // 2
D_B = D_MODEL // 2
B_GROUPS = 4
CHUNK = 128
D_EVEN_IN = 3 * D_A + 2 * D_B
D_C = D_MODEL // 4
C_GROUPS = 4
C_GW = D_C // C_GROUPS
N_HEADS = 12
QK_NOPE = 64
QK_ROPE = 32
V_DIM = 64
Q_LORA = 384
KV_LORA = 256
ROPE_BASE = 10000.0
N_GROUPS_MOE = 4
EXPERTS_PER_GROUP = 4
N_EXPERTS = N_GROUPS_MOE * EXPERTS_PER_GROUP
D_EXPERT = 256
EPS = 1e-6

LANES = 128
HEAD_PAD = 128
PAIR_W = 256
DEN_COL = 2 * 64
VX_W = 6 * PAIR_W
ROPE_OFF = QK_NOPE
GATE_OFF = N_GROUPS_MOE
NEG_BIG = -1e30
F32 = jnp.float32
BF16 = jnp.bfloat16
VMEM_LIMIT = 56 * 1024 * 1024


def _cparams(*sem):
    return pltpu.CompilerParams(dimension_semantics=sem, vmem_limit_bytes=VMEM_LIMIT)


def _rms(x, g):
    return x * lax.rsqrt(jnp.mean(x * x, axis=-1, keepdims=True) + EPS) * g


def _norm_mod(x, g, scale, shift):
    rs = lax.rsqrt(jnp.mean(x * x, axis=-1, keepdims=True) + EPS)
    return x * rs * (g * (1.0 + scale)) + shift


def _bdot(a, b):
    return jnp.dot(a, b, preferred_element_type=F32)


NT_DIMS = (((1,), (1,)), ((), ()))


MOD_ROWS = 8


def _mod_kernel(c_ref, w_ref, b_ref, o_ref):
    c = c_ref[...]
    s = c * jax.nn.sigmoid(c)
    s_hi = s.astype(BF16).astype(F32)
    pair = jnp.concatenate([s_hi, s - s_hi], axis=0).astype(BF16)
    two = _bdot(pair, w_ref[...].astype(BF16))
    o_ref[...] = two[0:MOD_ROWS] + two[MOD_ROWS:] + b_ref[...]


def _adaln(cond, mod_w, mod_b, n_rows):
    nt = 2048
    d6 = mod_w.shape[-1]
    cond = jnp.pad(cond, ((0, MOD_ROWS - n_rows), (0, 0)))
    out = pl.pallas_call(
        _mod_kernel,
        grid=(DEPTH, d6 // nt),
        in_specs=[
            pl.BlockSpec(cond.shape, lambda l, n: (0, 0)),
            pl.BlockSpec((None, D_MODEL, nt), lambda l, n: (l, 0, n)),
            pl.BlockSpec((None, 1, nt), lambda l, n: (l, 0, n)),
        ],
        out_specs=pl.BlockSpec((None, MOD_ROWS, nt), lambda l, n: (l, 0, n)),
        out_shape=jax.ShapeDtypeStruct((DEPTH, MOD_ROWS, d6), F32),
        compiler_params=_cparams("parallel", "parallel"),
        name="adaln",
    )(cond, mod_w, mod_b.reshape(DEPTH, 1, d6))
    return out[:, :n_rows].reshape(DEPTH, n_rows, 6, D_MODEL)


class _Stream:
    def __init__(self, batch, seq, shared_cond):
        self.batch = batch
        self.seq = seq
        self.shared_cond = shared_cond

    def row_of_batch(self, b):
        return 0 if self.shared_cond else b + 1


def _mod_spec(stream, layer):
    return pl.BlockSpec((None, None, 6, D_MODEL), lambda b, i: (layer, stream.row_of_batch(b), 0, 0))


HALO = 8
EVEN_SEQS = 4


def _even_kernel(*refs, has_halo, seq_rows, nc):
    if has_halo:
        x_ref, xp_ref, xn_ref = refs[:3]
        refs = refs[3:]
    else:
        x_ref = refs[0]
        refs = refs[1:]
    mod_ref, g_ref, wi_ref, cw_ref, sg_ref, sw_ref, sb_ref, wo_ref = refs[:8]
    plan_in, (o_ref, *plan_out), (z_ref, y_ref) = refs[8:12], refs[12:17], refs[17:]
    i = pl.program_id(1)
    n_i = pl.num_programs(1)
    ts = x_ref.shape[0]
    m = mod_ref[...]
    g = g_ref[...]

    def modulate(x):
        return _norm_mod(x, g, m[1:2], m[0:1]).astype(BF16)

    x = x_ref[...]
    hb = modulate(x)
    for n in range(D_EVEN_IN // nc):
        z_ref[:, n * nc:(n + 1) * nc] = _bdot(hb, wi_ref[:, n * nc:(n + 1) * nc]).astype(BF16)

    gate_b = z_ref[:, 0:D_A].astype(F32)
    gate_c = z_ref[:, D_A:2 * D_A].astype(F32)
    xa = z_ref[:, 2 * D_A:3 * D_A].astype(F32)
    t = gate_c * xa
    t_prev = pltpu.roll(t, 1, axis=0)
    t_next = pltpu.roll(t, ts - 1, axis=0)
    row = lax.broadcasted_iota(jnp.int32, (ts, 1), 0) & (seq_rows - 1)
    if has_halo:
        hh = modulate(jnp.concatenate([xp_ref[...], xn_ref[...]], axis=0))
        zh = _bdot(hh, wi_ref[:, D_A:3 * D_A]).astype(BF16).astype(F32)
        th = zh[:, 0:D_A] * zh[:, D_A:2 * D_A]
        tp = th[HALO - 1:HALO] * (i > 0).astype(F32)
        tn = th[HALO:HALO + 1] * (i < n_i - 1).astype(F32)
    else:
        tp = tn = 0.0
    t_prev = jnp.where(row == 0, tp, t_prev)
    t_next = jnp.where(row == seq_rows - 1, tn, t_next)
    cw = cw_ref[...]
    y_a = gate_b * (t_prev * cw[0:1] + t * cw[1:2] + t_next * cw[2:3])
    y_ref[:, 0:D_A] = y_a.astype(BF16)

    u = z_ref[:, 3 * D_A:3 * D_A + D_B].astype(F32)
    v = z_ref[:, 3 * D_A + D_B:3 * D_A + 2 * D_B].astype(F32)
    vb = _rms(v, sg_ref[...]).astype(BF16)
    gw = D_B // B_GROUPS
    for c in range(ts // CHUNK):
        rows = slice(c * CHUNK, (c + 1) * CHUNK)
        for gi in range(B_GROUPS):
            cols = slice(gi * gw, (gi + 1) * gw)
            sv = _bdot(sw_ref[gi], vb[rows, cols]) + sb_ref[:, cols]
            y_ref[rows, D_A + gi * gw:D_A + (gi + 1) * gw] = (u[rows, cols] * sv).astype(BF16)

    x1 = x + m[2:3] * _bdot(y_ref[...], wo_ref[...])
    o_ref[...] = x1
    _route_plan(x1, m, plan_in, plan_out, (pl.program_id(0) * n_i + i) * (ts // ROUTE_TM))


def _even_layer(x3, mod, g, w_in, conv_w, sgu_g, sgu_w, sgu_bias, w_out, plan_consts, stream, layer):
    out_shape = x3.shape
    b, s, _ = x3.shape
    ts = min(s, 256) if stream.shared_cond else min(s, 1024)
    n_i = s // ts
    has_halo = n_i > 1
    seq_rows = ts
    if not has_halo and stream.shared_cond and b % EVEN_SEQS == 0:
        b, s, ts = b // EVEN_SEQS, EVEN_SEQS * s, EVEN_SEQS * ts
        x3 = x3.reshape(b, s, D_MODEL)
    hb = ts // HALO
    last_h = s // HALO - 1
    const = lambda a: pl.BlockSpec(a.shape, lambda bi, i: (0,) * a.ndim)
    in_specs = [pl.BlockSpec((None, ts, D_MODEL), lambda bi, i: (bi, i, 0))]
    args = [x3]
    if has_halo:
        in_specs += [
            pl.BlockSpec((None, HALO, D_MODEL), lambda bi, i: (bi, jnp.maximum(i * hb - 1, 0), 0)),
            pl.BlockSpec((None, HALO, D_MODEL), lambda bi, i: (bi, jnp.minimum((i + 1) * hb, last_h), 0)),
        ]
        args += [x3, x3]
    in_specs += [_mod_spec(stream, layer), const(g), const(w_in), const(conv_w), const(sgu_g), const(sgu_w),
                 const(sgu_bias), const(w_out)] + [const(a) for a in plan_consts]
    args += [mod, g, w_in, conv_w, sgu_g, sgu_w, sgu_bias, w_out] + plan_consts
    plan_shapes, plan_specs = _plan_out(b, s, ts, lambda bi, i: (bi, i))
    out = pl.pallas_call(
        functools.partial(_even_kernel, has_halo=has_halo, seq_rows=seq_rows, nc=512),
        grid=(b, n_i),
        in_specs=in_specs,
        out_specs=[pl.BlockSpec((None, ts, D_MODEL), lambda bi, i: (bi, i, 0))] + plan_specs,
        out_shape=[jax.ShapeDtypeStruct((b, s, D_MODEL), F32)] + plan_shapes,
        scratch_shapes=[pltpu.VMEM((ts, D_EVEN_IN), BF16), pltpu.VMEM((ts, D_A + D_B), BF16)],
        compiler_params=_cparams("arbitrary", "arbitrary"),
        name="even_layer",
    )(*args)
    return out[0].reshape(out_shape), out[1:]


ROUTE_TM = 256
ROUTE_SUBS = 4
ROUTE_STEP = ROUTE_TM * ROUTE_SUBS
ROUTE_PAD = 8
SORT_ROWS = ROUTE_TM + LANES
RUN_ROWS = SORT_ROWS + 32
ROUTE_ROWS = ROUTE_TM + 4 * ROUTE_PAD
XS_W = D_MODEL + LANES
GATE_LO = EXPERTS_PER_GROUP
DLOC_HI = 2 * EXPERTS_PER_GROUP
DLOC_RADIX = 16.0
FFN_BM = 512
FFN_HALF = FFN_BM // 2
RUN_SIZES = (256, 128, 64, 32, 16, 8)
TAB_W = 2 * N_GROUPS_MOE


def _round_up(x, m):
    return lax.div(x + (m - 1), m) * m


def _run_copies(tab_ref, tile, hbm_ref, vmem_ref, sem, to_hbm, wait):
    off = 0
    for g in range(N_GROUPS_MOE):
        start = tab_ref[tile * TAB_W + g]
        n = tab_ref[tile * TAB_W + N_GROUPS_MOE + g]
        for p in RUN_SIZES:
            done = n & (-2 * p)

            @pl.when((n & p) != 0)
            def _():
                v = vmem_ref.at[pl.ds(pl.multiple_of(off + done, ROUTE_PAD), p)]
                h = hbm_ref.at[pl.ds(pl.multiple_of(start + done, ROUTE_PAD), p)]
                cp = pltpu.make_async_copy(v, h, sem) if to_hbm else pltpu.make_async_copy(h, v, sem)
                if wait:
                    cp.wait()
                else:
                    cp.start()
        off = off + n


def _zero_fill(tab_ref, meta, zeros_ref, hbm_ref, sem, n_rows, wait):
    def copy(rows, dst_row):
        cp = pltpu.make_async_copy(zeros_ref.at[pl.ds(0, rows)],
                                   hbm_ref.at[pl.ds(pl.multiple_of(dst_row, ROUTE_PAD), rows)], sem)
        if wait:
            cp.wait()
        else:
            cp.start()

    end = 0
    for g in range(N_GROUPS_MOE):
        fill = tab_ref[meta + g]
        start = tab_ref[meta + N_GROUPS_MOE + g]
        end = start + _round_up(fill, FFN_BM)
        tail = end - start - fill
        for p in RUN_SIZES:
            pl.when((tail & p) != 0)(functools.partial(copy, p, start + fill + (tail & (-2 * p))))
    for k in range(n_rows // FFN_BM):
        pl.when(end + k * FFN_BM < n_rows)(functools.partial(copy, FFN_BM, end + k * FFN_BM))


def _wait_rows(n, hbm_ref, vmem_ref, sem, to_hbm):
    for p in RUN_SIZES:
        @pl.when((n & p) != 0)
        def _():
            v = vmem_ref.at[pl.ds(0, p)]
            h = hbm_ref.at[pl.ds(0, p)]
            (pltpu.make_async_copy(v, h, sem) if to_hbm else pltpu.make_async_copy(h, v, sem)).wait()


def _tile_rows(tab_ref, tile):
    n = 0
    for g in range(N_GROUPS_MOE):
        n = n + tab_ref[tile * TAB_W + N_GROUPS_MOE + g]
    return n


def _max4(v):
    return jnp.maximum(jnp.maximum(v[0], v[1]), jnp.maximum(v[2], v[3]))


def _first_of4(v, top):
    return jnp.where(v[0] == top, 0.0, jnp.where(v[1] == top, 1.0, jnp.where(v[2] == top, 2.0, 3.0)))


def _route_plan(x, m, plan_in, plan_out, tile0):
    g_ref, wrt_ref, brt_ref, upper_ref = plan_in
    haug_ref, dloc_ref, drow_ref, cnt_ref = plan_out
    rows = x.shape[0]
    tm = ROUTE_TM
    ng = N_GROUPS_MOE
    h = _norm_mod(x, g_ref[...], m[4:5], m[3:4])
    hb = h.astype(BF16)
    lt = lax.dot_general(wrt_ref[...], hb, NT_DIMS, preferred_element_type=F32) + brt_ref[...]
    gl = [lt[r:r + 1, :] for r in range(ng)]
    g_top = _max4(gl)
    g_idx = _first_of4(gl, g_top)
    g_w = 1.0 / (jnp.exp(gl[0] - g_top) + jnp.exp(gl[1] - g_top) + jnp.exp(gl[2] - g_top) + jnp.exp(gl[3] - g_top))
    ev = []
    for k in range(EXPERTS_PER_GROUP):
        cand = [lt[GATE_OFF + EXPERTS_PER_GROUP * r + k:GATE_OFF + EXPERTS_PER_GROUP * r + k + 1, :]
                for r in range(ng)]
        ev.append(jnp.where(g_idx == 0.0, cand[0], jnp.where(g_idx == 1.0, cand[1],
                            jnp.where(g_idx == 2.0, cand[2], cand[3]))))
    v1 = _max4(ev)
    i1 = _first_of4(ev, v1)
    rest = [jnp.where(i1 == float(k), NEG_BIG, ev[k]) for k in range(EXPERTS_PER_GROUP)]
    v2 = _max4(rest)
    i2 = _first_of4(rest, v2)
    e2 = jnp.exp(v2 - v1)
    w1 = 1.0 / (1.0 + e2)
    w2 = e2 * w1
    gates = [g_w * (jnp.where(i1 == float(k), w1, 0.0) + jnp.where(i2 == float(k), w2, 0.0))
             for k in range(EXPERTS_PER_GROUP)]

    sub8 = lax.broadcasted_iota(jnp.int32, (8, tm), 0).astype(F32)
    dlocs = []
    for sub in range(rows // tm):
        gi = g_idx[:, sub * tm:(sub + 1) * tm]
        hot = jnp.where(sub8 == gi, 1.0, 0.0)
        before = _bdot(hot.astype(BF16), upper_ref[...])
        dl = jnp.sum(before * hot, axis=0, keepdims=True)
        off = 0
        for g in range(ng):
            n_g = _round_up(jnp.sum(hot[g:g + 1, :]).astype(jnp.int32), ROUTE_PAD)
            cnt_ref[(tile0 + sub) * ng + g] = n_g
            dl = dl + jnp.where(gi == float(g), off.astype(F32) if g else 0.0, 0.0)
            off = off + n_g
        drow_ref[sub] = jnp.broadcast_to(dl, (8, tm))
        dlocs.append(dl)
    dloc = jnp.concatenate(dlocs, axis=1)
    d_hi = jnp.floor(dloc * (1.0 / DLOC_RADIX))
    g_hi = [gt.astype(BF16).astype(F32) for gt in gates]
    ex_rows = g_hi + [gt - gh for gt, gh in zip(gates, g_hi)] + [d_hi, dloc - DLOC_RADIX * d_hi]
    sub16 = lax.broadcasted_iota(jnp.int32, (16, rows), 0)
    ex_t = jnp.zeros((16, rows), F32)
    for r, row in enumerate(ex_rows):
        ex_t = jnp.where(sub16 == r, row, ex_t)
    ex_t = jnp.concatenate([ex_t, jnp.zeros((LANES - 16, rows), F32)], axis=0)
    extras = ex_t.T
    dloc_ref[...] = jnp.broadcast_to(
        DLOC_RADIX * extras[:, DLOC_HI:DLOC_HI + 1] + extras[:, DLOC_HI + 1:DLOC_HI + 2], (rows, LANES))
    haug_ref[...] = jnp.concatenate([hb, extras.astype(BF16)], axis=1)


def _plan_consts(g2, w_rt, b_rt):
    upper = jnp.asarray(np.triu(np.ones((ROUTE_TM, ROUTE_TM), np.float32), 1), BF16)
    return [g2, w_rt, b_rt, upper]


def _plan_out(b, s, rows, index):
    tiles = rows // ROUTE_TM
    shapes = [jax.ShapeDtypeStruct((b, s, XS_W), BF16), jax.ShapeDtypeStruct((b, s, LANES), F32),
              jax.ShapeDtypeStruct((b, s // ROUTE_TM, 8, ROUTE_TM), F32),
              jax.ShapeDtypeStruct((b * (s // ROUTE_TM) * N_GROUPS_MOE,), jnp.int32)]
    specs = [pl.BlockSpec((None, rows, XS_W), lambda *idx: (*index(*idx), 0)),
             pl.BlockSpec((None, rows, LANES), lambda *idx: (*index(*idx), 0)),
             pl.BlockSpec((None, tiles, 8, ROUTE_TM), lambda *idx: (*index(*idx), 0, 0)),
             pl.BlockSpec(memory_space=pltpu.SMEM)]
    return shapes, specs


def _flat_plan(plan):
    haug, dloc, drow, cnt = plan
    return (haug.reshape(-1, XS_W), dloc.reshape(-1, LANES), drow.reshape(-1, 8, ROUTE_TM), cnt)


def _dispatch_kernel(cp_ref, cs_ref, hp_ref, hs_ref, dp_ref, ds_ref, tab_ref, sorted_hbm,
                     sorted_ref, zeros_ref, fill_ref, sem_ref, zsem_ref, *, n_steps, n_p_steps):
    i = pl.program_id(0)
    tm = ROUTE_TM
    ng = N_GROUPS_MOE
    n_tiles = n_steps * ROUTE_SUBS
    n_p_tiles = n_p_steps * ROUTE_SUBS
    meta = n_tiles * TAB_W
    n_rows = sorted_hbm.shape[0]

    def count(tile, g):
        if isinstance(tile, int):
            return cp_ref[tile * ng + g] if tile < n_p_tiles else cs_ref[(tile - n_p_tiles) * ng + g]
        return jnp.where(tile < n_p_tiles, cp_ref[jnp.minimum(tile, n_p_tiles - 1) * ng + g],
                         cs_ref[jnp.maximum(tile - n_p_tiles, 0) * ng + g])

    @pl.when(i == 0)
    def _():
        start = 0
        for g in range(ng):
            fill = sum(count(t, g) for t in range(n_tiles))
            tab_ref[meta + g] = fill
            tab_ref[meta + ng + g] = start
            start = start + _round_up(fill, FFN_BM)
            fill_ref[g] = 0
        zeros_ref[...] = jnp.zeros_like(zeros_ref)

    is_prompt = i < n_p_steps
    row_f = lax.broadcasted_iota(jnp.int32, (ROUTE_ROWS, tm), 0).astype(F32)

    def sort_tile(h_ref, d_ref, sub):
        onehot = jnp.where(row_f == d_ref[sub][0:1, :], 1.0, 0.0).astype(BF16)
        sorted_ref[sub] = _bdot(onehot, h_ref[sub * tm:(sub + 1) * tm, :])

    for sub in range(ROUTE_SUBS):
        tile = i * ROUTE_SUBS + sub

        @pl.when(i >= 1)
        def _():
            _wait_rows(_tile_rows(tab_ref, tile - ROUTE_SUBS), sorted_hbm, sorted_ref.at[sub], sem_ref.at[sub], True)

        pl.when(is_prompt)(functools.partial(sort_tile, hp_ref, dp_ref, sub))
        pl.when(jnp.logical_not(is_prompt))(functools.partial(sort_tile, hs_ref, ds_ref, sub))
        for g in range(ng):
            n_g = count(tile, g)
            tab_ref[tile * TAB_W + g] = tab_ref[meta + ng + g] + fill_ref[g]
            tab_ref[tile * TAB_W + ng + g] = n_g
            fill_ref[g] = fill_ref[g] + n_g
        _run_copies(tab_ref, tile, sorted_hbm, sorted_ref.at[sub], sem_ref.at[sub], to_hbm=True, wait=False)

        @pl.when(i == n_steps - 1)
        def _():
            _wait_rows(_tile_rows(tab_ref, tile), sorted_hbm, sorted_ref.at[sub], sem_ref.at[sub], True)

    @pl.when(i == n_steps - 1)
    def _():
        _zero_fill(tab_ref, meta, zeros_ref, sorted_hbm, zsem_ref, n_rows, wait=False)
        _zero_fill(tab_ref, meta, zeros_ref, sorted_hbm, zsem_ref, n_rows, wait=True)


def _merged_specs(n_p_steps, n_s_steps, layer, sample_seq, step_of, width=D_MODEL):
    def p_map(*idx):
        return (jnp.minimum(step_of(*idx), n_p_steps - 1), 0)

    def s_map(*idx):
        return (jnp.clip(step_of(*idx) - n_p_steps, 0, n_s_steps - 1), 0)

    def mod_map(*idx):
        j = step_of(*idx)
        row = jnp.where(j < n_p_steps, 0, 1 + lax.div(jnp.maximum(j - n_p_steps, 0) * ROUTE_STEP, sample_seq))
        return (layer, row, 0, 0)

    return (pl.BlockSpec((ROUTE_STEP, width), p_map), pl.BlockSpec((ROUTE_STEP, width), s_map),
            pl.BlockSpec((None, None, 6, D_MODEL), mod_map))


def _dispatch(plan_p, plan_s, n_rows):
    haug_p, _, drow_p, cnt_p = plan_p
    haug_s, _, drow_s, cnt_s = plan_s
    n_p_steps = haug_p.shape[0] // ROUTE_STEP
    n_s_steps = haug_s.shape[0] // ROUTE_STEP
    n_steps = n_p_steps + n_s_steps
    n_tiles = n_steps * ROUTE_SUBS
    step_of = lambda i, cp, cs: i
    hp_spec, hs_spec, _ = _merged_specs(n_p_steps, n_s_steps, 0, 1, step_of, XS_W)
    drow_block = (ROUTE_SUBS, 8, ROUTE_TM)
    grid_spec = pltpu.PrefetchScalarGridSpec(
        num_scalar_prefetch=2,
        grid=(n_steps,),
        in_specs=[
            hp_spec, hs_spec,
            pl.BlockSpec(drow_block, lambda i, cp, cs: (jnp.minimum(i, n_p_steps - 1), 0, 0)),
            pl.BlockSpec(drow_block, lambda i, cp, cs: (jnp.clip(i - n_p_steps, 0, n_s_steps - 1), 0, 0)),
        ],
        out_specs=[pl.BlockSpec(memory_space=pltpu.SMEM), pl.BlockSpec(memory_space=pl.ANY)],
        scratch_shapes=[
            pltpu.VMEM((ROUTE_SUBS, ROUTE_ROWS, XS_W), F32),
            pltpu.VMEM((FFN_BM, XS_W), F32),
            pltpu.SMEM((N_GROUPS_MOE,), jnp.int32),
            pltpu.SemaphoreType.DMA((ROUTE_SUBS,)),
            pltpu.SemaphoreType.DMA(()),
        ],
    )
    return pl.pallas_call(
        functools.partial(_dispatch_kernel, n_steps=n_steps, n_p_steps=n_p_steps),
        grid_spec=grid_spec,
        out_shape=[
            jax.ShapeDtypeStruct(((n_tiles + 1) * TAB_W,), jnp.int32),
            jax.ShapeDtypeStruct((n_rows, XS_W), F32),
        ],
        compiler_params=_cparams("arbitrary"),
        name="moe_dispatch",
    )(cnt_p, cnt_s, haug_p, haug_s, drow_p, drow_s)


def _ffn_lookup(i, tab_ref, meta):
    fills = [tab_ref[meta + g] for g in range(N_GROUPS_MOE)]
    edges = []
    acc = 0
    for f in fills:
        acc = acc + lax.div(f + (FFN_BM - 1), FFN_BM)
        edges.append(acc)
    total = edges[-1]
    ii = jnp.minimum(i, total - 1)
    grp = sum((ii >= e).astype(jnp.int32) for e in edges[:-1])

    def pick(vals):
        return jnp.where(grp == 0, vals[0], jnp.where(grp == 1, vals[1], jnp.where(grp == 2, vals[2], vals[3])))

    first = pick([0] + edges[:-1])
    return grp, total, ii == first, pick(fills) - (ii - first) * FFN_BM


def _ffn_group_kernel(tab_ref, xs_ref, w1_ref, w3_ref, w2_ref, ys_ref, w1b_ref, w3b_ref, w2b_ref, *, meta):
    i = pl.program_id(0)
    _, total, first_of_group, valid = _ffn_lookup(i, tab_ref, meta)
    active = i < total

    @pl.when(jnp.logical_and(active, first_of_group))
    def _():
        w1b_ref[...] = w1_ref[...].astype(BF16)
        w3b_ref[...] = w3_ref[...].astype(BF16)
        w2b_ref[...] = w2_ref[...].astype(BF16)

    def run(rows):
        hb = xs_ref[0:rows, 0:D_MODEL].astype(BF16)
        ex = xs_ref[0:rows, D_MODEL:XS_W]
        hid = []
        for e in range(EXPERTS_PER_GROUP):
            a = _bdot(hb, w1b_ref[e])
            b = _bdot(hb, w3b_ref[e])
            gate = ex[:, e:e + 1] + ex[:, GATE_LO + e:GATE_LO + e + 1]
            hid.append(((a * jax.nn.sigmoid(a)) * b * gate).astype(BF16))
        ys_ref[0:rows, :] = _bdot(jnp.concatenate(hid, axis=1), w2b_ref[...].reshape(-1, D_MODEL))

    @pl.when(jnp.logical_and(active, valid > FFN_HALF))
    def _():
        run(FFN_BM)

    @pl.when(jnp.logical_and(active, valid <= FFN_HALF))
    def _():
        run(FFN_HALF)
        ys_ref[FFN_HALF:, :] = jnp.zeros((FFN_BM - FFN_HALF, D_MODEL), F32)

    @pl.when(jnp.logical_not(active))
    def _():
        ys_ref[...] = jnp.zeros_like(ys_ref)


def _ffn_group(tab, xs, w1, w3, w2, layer, meta):
    e4 = EXPERTS_PER_GROUP
    n_blocks = xs.shape[0] // FFN_BM
    group_of = lambda i, tab_ref: _ffn_lookup(i, tab_ref, meta)[0]
    grid_spec = pltpu.PrefetchScalarGridSpec(
        num_scalar_prefetch=1,
        grid=(n_blocks,),
        in_specs=[
            pl.BlockSpec((FFN_BM, XS_W), lambda i, tab_ref: (i, 0)),
            pl.BlockSpec((None, e4, D_MODEL, D_EXPERT), lambda i, tab_ref: (layer, group_of(i, tab_ref), 0, 0)),
            pl.BlockSpec((None, e4, D_MODEL, D_EXPERT), lambda i, tab_ref: (layer, group_of(i, tab_ref), 0, 0)),
            pl.BlockSpec((None, e4, D_EXPERT, D_MODEL), lambda i, tab_ref: (layer, group_of(i, tab_ref), 0, 0)),
        ],
        out_specs=pl.BlockSpec((FFN_BM, D_MODEL), lambda i, tab_ref: (i, 0)),
        scratch_shapes=[
            pltpu.VMEM((e4, D_MODEL, D_EXPERT), BF16),
            pltpu.VMEM((e4, D_MODEL, D_EXPERT), BF16),
            pltpu.VMEM((e4, D_EXPERT, D_MODEL), BF16),
        ],
    )
    return pl.pallas_call(
        functools.partial(_ffn_group_kernel, meta=meta),
        grid_spec=grid_spec,
        out_shape=jax.ShapeDtypeStruct((xs.shape[0], D_MODEL), F32),
        compiler_params=_cparams("arbitrary"),
        name="moe_ffn",
    )(tab, xs, w1, w3, w2)


def _fetch_runs(tab_ref, ys_ref, runs_ref, sem_ref, tile, slot):
    _run_copies(tab_ref, tile, ys_ref, runs_ref.at[slot], sem_ref.at[slot], to_hbm=False, wait=False)


def _unsort(tab_ref, ys_ref, runs_ref, sem_ref, dloc_col, tile, slot):
    covered = _tile_rows(tab_ref, tile)
    _wait_rows(covered, ys_ref, runs_ref.at[slot], sem_ref.at[slot], False)
    runs_ref[slot, pl.ds(pl.multiple_of(covered, ROUTE_PAD), LANES), :] = jnp.zeros((LANES, D_MODEL), F32)
    yb = runs_ref[slot, 0:SORT_ROWS, :].astype(BF16)
    row_f = lax.broadcasted_iota(jnp.int32, (ROUTE_TM, SORT_ROWS), 1).astype(F32)
    return _bdot(jnp.where(row_f == dloc_col, 1.0, 0.0).astype(BF16), yb)


def _combine_kernel(tab_ref, *refs, n_steps, first_step, n_p_steps, final_norm):
    with_prompt = first_step < n_p_steps
    if with_prompt:
        xp_ref, xs_ref, mod_ref, dp_ref, ds_ref, fg_ref, ys_ref, op_ref, os_ref, runs_ref, sem_ref = refs
    else:
        xs_ref, mod_ref, ds_ref, fg_ref, ys_ref, os_ref, runs_ref, sem_ref = refs
    i = pl.program_id(0)
    par = lax.rem(i, 2)
    step = first_step + i
    tm = ROUTE_TM

    def fetch(st, parity):
        for sub in range(ROUTE_SUBS):
            _fetch_runs(tab_ref, ys_ref, runs_ref, sem_ref, st * ROUTE_SUBS + sub, parity * ROUTE_SUBS + sub)

    @pl.when(i == 0)
    def _():
        fetch(first_step, 0)

    @pl.when(i + 1 < n_steps)
    def _():
        fetch(step + 1, 1 - par)

    is_prompt = step < n_p_steps
    dloc = jnp.where(is_prompt, dp_ref[:, 0:1], ds_ref[:, 0:1]) if with_prompt else ds_ref[:, 0:1]
    parts = [_unsort(tab_ref, ys_ref, runs_ref, sem_ref, dloc[sub * tm:(sub + 1) * tm],
                     step * ROUTE_SUBS + sub, par * ROUTE_SUBS + sub) for sub in range(ROUTE_SUBS)]
    delta = mod_ref[5:6, :] * jnp.concatenate(parts, axis=0)

    def finish(x_ref, o_ref):
        x2 = x_ref[...] + delta
        o_ref[...] = _rms(x2, fg_ref[...]) if final_norm else x2

    if with_prompt:
        pl.when(is_prompt)(functools.partial(finish, xp_ref, op_ref))
        pl.when(jnp.logical_not(is_prompt))(functools.partial(finish, xs_ref, os_ref))
    else:
        finish(xs_ref, os_ref)


def _combine(tab, xp, xs, mod, dloc_p, dloc_s, final_g, ys, layer, sample_seq, final_norm, with_prompt):
    n_p_steps = xp.shape[0] // ROUTE_STEP
    n_s_steps = xs.shape[0] // ROUTE_STEP
    first_step = 0 if with_prompt else n_p_steps
    n_steps = n_p_steps + n_s_steps - first_step
    step_of = lambda i, tab_ref: i + first_step
    p_spec, s_spec, mod_spec = _merged_specs(n_p_steps, n_s_steps, layer, sample_seq, step_of)
    dp_spec, ds_spec, _ = _merged_specs(n_p_steps, n_s_steps, layer, sample_seq, step_of, LANES)
    streams = [p_spec, s_spec] if with_prompt else [s_spec]
    dlocs = [dp_spec, ds_spec] if with_prompt else [ds_spec]
    grid_spec = pltpu.PrefetchScalarGridSpec(
        num_scalar_prefetch=1,
        grid=(n_steps,),
        in_specs=streams + [mod_spec] + dlocs + [
            pl.BlockSpec((1, D_MODEL), lambda i, tab_ref: (0, 0)),
            pl.BlockSpec(memory_space=pl.ANY),
        ],
        out_specs=streams,
        scratch_shapes=[
            pltpu.VMEM((2 * ROUTE_SUBS, RUN_ROWS, D_MODEL), F32),
            pltpu.SemaphoreType.DMA((2 * ROUTE_SUBS,)),
        ],
    )
    arrays = [xp, xs] if with_prompt else [xs]
    dloc_arrays = [dloc_p, dloc_s] if with_prompt else [dloc_s]
    return pl.pallas_call(
        functools.partial(_combine_kernel, n_steps=n_steps, first_step=first_step, n_p_steps=n_p_steps,
                          final_norm=final_norm),
        grid_spec=grid_spec,
        out_shape=[jax.ShapeDtypeStruct(a.shape, F32) for a in arrays],
        compiler_params=_cparams("arbitrary"),
        name="moe_combine",
    )(tab, *arrays, mod, *dloc_arrays, final_g, ys)


def _moe(xp, xs, plan_p, plan_s, mod, w1, w3, w2, final_g, layer, sample_seq, final_norm, defer_prompt):
    plan_p, plan_s = _flat_plan(plan_p), _flat_plan(plan_s)
    t = xp.shape[0] + xs.shape[0]
    n_tiles = t // ROUTE_TM
    max_rows = t + N_GROUPS_MOE * (ROUTE_PAD - 1) * n_tiles
    n_rows = (-(-max_rows // FFN_BM) + N_GROUPS_MOE) * FFN_BM
    tab, sorted_x = _dispatch(plan_p, plan_s, n_rows)
    ys = _ffn_group(tab, sorted_x, w1, w3, w2, layer, n_tiles * TAB_W)
    out = _combine(tab, xp, xs, mod, plan_p[1], plan_s[1], final_g, ys, layer, sample_seq, final_norm,
                   not defer_prompt)
    if defer_prompt:
        return None, out[0], (tab, plan_p[1], ys)
    return out[0], out[1], None


def _rope_tables(seq):
    half = QK_ROPE // 2
    nf = half // 2
    inv = ROPE_BASE ** (-np.arange(nf, dtype=np.float64) / nf)
    pos = np.arange(seq)
    row = (pos // GRID_W).astype(np.float64)
    col = (pos % GRID_W).astype(np.float64)
    cos = np.ones((seq, HEAD_PAD), np.float64)
    sin_a = np.zeros((seq, HEAD_PAD), np.float64)
    sin_b = np.zeros((seq, HEAD_PAD), np.float64)
    for part, p in enumerate((row, col)):
        ang = p[:, None] * inv[None, :]
        base = ROPE_OFF + part * half
        cos[:, base:base + nf] = np.cos(ang)
        cos[:, base + nf:base + half] = np.cos(ang)
        sin_a[:, base:base + nf] = -np.sin(ang)
        sin_b[:, base + nf:base + half] = np.sin(ang)
    return tuple(jnp.asarray(a, F32) for a in (cos, sin_a, sin_b))


def _apply_rope(x, cos, sin_a, sin_b, reps):
    nf = QK_ROPE // 4
    width = x.shape[1]
    if reps > 1:
        cos, sin_a, sin_b = (jnp.concatenate([a] * reps, axis=1) for a in (cos, sin_a, sin_b))
    return x * cos + pltpu.roll(x, width - nf, axis=1) * sin_a + pltpu.roll(x, nf, axis=1) * sin_b


def _odd_in_kernel(*refs, rope, emit_cache):
    x_ref, mod_ref, g_ref, w_ref, qg_ref, wq_ref, kg_ref, wk_ref, wv_ref, cs_ref = refs[:10]
    refs = refs[10:]
    if rope:
        cos_ref, sa_ref, sb_ref = refs[:3]
        refs = refs[3:]
    y_ref, q_ref, k_ref, v_ref = refs[:4]
    refs = refs[4:]
    m = mod_ref[...]
    h = _norm_mod(x_ref[...], g_ref[...], m[1:2], m[0:1])
    z = _bdot(h.astype(BF16), w_ref[...])
    zc = z[:, 0:D_C]
    qc = z[:, D_C:D_C + Q_LORA]
    kvc = z[:, D_C + Q_LORA:D_C + Q_LORA + KV_LORA]
    kpe = z[:, D_C + Q_LORA + KV_LORA:]
    q = _bdot(_rms(qc, qg_ref[...]).astype(BF16), wq_ref[...])
    kvn = _rms(kvc, kg_ref[...])
    if emit_cache:
        ckv_ref, kpe_ref = refs
        ckv_ref[...] = kvn.reshape(ckv_ref.shape)
        kpe_ref[...] = kpe[:, ROPE_OFF:ROPE_OFF + QK_ROPE].reshape(kpe_ref.shape)
    if rope:
        tabs = (cos_ref[...], sa_ref[...], sb_ref[...])
        q = _apply_rope(q, *tabs, reps=N_HEADS)
        kpe = _apply_rope(kpe, *tabs, reps=1)
    kvb = kvn.astype(BF16)
    k = _bdot(kvb, wk_ref[...]) + jnp.concatenate([kpe] * N_HEADS, axis=1)
    scale = math.log2(math.e) / math.sqrt(QK_NOPE + QK_ROPE)
    q_ref[...] = (q * scale).astype(BF16)
    k_ref[...] = k.astype(BF16)
    v_ref[...] = _bdot(kvb, wv_ref[...]).astype(BF16)
    y = _bdot(zc.astype(BF16), cs_ref[...])
    y_ref[0, :, :] = y[:, 0:D_C].astype(BF16)
    y_ref[1, :, :] = y[:, D_C:2 * D_C].astype(BF16)


def _odd_in(x3, mod, g, w_in, q_g, w_q, kv_g, w_k, w_v, cs, stream, layer, rope_tabs):
    b, s, _ = x3.shape
    tm = min(s, 512)
    n_i = s // tm
    const = lambda a: pl.BlockSpec(a.shape, lambda bi, i: (0,) * a.ndim)
    in_specs = [
        pl.BlockSpec((None, tm, D_MODEL), lambda bi, i: (bi, i, 0)),
        _mod_spec(stream, layer),
        const(g), const(w_in), const(q_g), const(w_q), const(kv_g), const(w_k), const(w_v), const(cs),
    ] + [pl.BlockSpec((tm, HEAD_PAD), lambda bi, i: (i, 0))] * 3
    args = [x3, mod, g, w_in, q_g, w_q, kv_g, w_k, w_v, cs, *rope_tabs]
    hq = N_HEADS * HEAD_PAD
    out_specs = [
        pl.BlockSpec((None, 2, tm, D_C), lambda bi, i: (bi, 0, i, 0)),
        pl.BlockSpec((None, tm, hq), lambda bi, i: (bi, i, 0)),
        pl.BlockSpec((None, tm, hq), lambda bi, i: (bi, i, 0)),
        pl.BlockSpec((None, tm, N_HEADS * V_DIM), lambda bi, i: (bi, i, 0)),
    ]
    out_shape = [
        jax.ShapeDtypeStruct((b, 2, s, D_C), BF16),
        jax.ShapeDtypeStruct((b, s, hq), BF16),
        jax.ShapeDtypeStruct((b, s, hq), BF16),
        jax.ShapeDtypeStruct((b, s, N_HEADS * V_DIM), BF16),
    ]
    return pl.pallas_call(
        functools.partial(_odd_in_kernel, rope=True, emit_cache=False),
        grid=(b, n_i),
        in_specs=in_specs,
        out_specs=out_specs,
        out_shape=out_shape,
        compiler_params=_cparams("parallel", "parallel"),
        name="odd_in",
    )(*args)


def _cache_kv_kernel(c_ref, p_ref, wk_ref, wv_ref, k_ref, v_ref):
    cb = c_ref[...].astype(BF16)
    k = _bdot(cb, wk_ref[...]) + jnp.concatenate([p_ref[...]] * N_HEADS, axis=1)
    k_ref[...] = k.astype(BF16)
    v_ref[...] = _bdot(cb, wv_ref[...]).astype(BF16)


def _cache_kv(ckv, kpe_blk, w_k, w_v):
    b, p, _ = ckv.shape
    hq = N_HEADS * HEAD_PAD
    return pl.pallas_call(
        _cache_kv_kernel,
        grid=(b,),
        in_specs=[
            pl.BlockSpec((None, p, KV_LORA), lambda bi: (bi, 0, 0)),
            pl.BlockSpec((None, p, HEAD_PAD), lambda bi: (bi, 0, 0)),
            pl.BlockSpec(w_k.shape, lambda bi: (0, 0)),
            pl.BlockSpec(w_v.shape, lambda bi: (0, 0)),
        ],
        out_specs=[
            pl.BlockSpec((None, p, hq), lambda bi: (bi, 0, 0)),
            pl.BlockSpec((None, p, N_HEADS * V_DIM), lambda bi: (bi, 0, 0)),
        ],
        out_shape=[
            jax.ShapeDtypeStruct((b, p, hq), BF16),
            jax.ShapeDtypeStruct((b, p, N_HEADS * V_DIM), BF16),
        ],
        compiler_params=_cparams("parallel"),
        name="cache_kv",
    )(ckv, kpe_blk, w_k, w_v)


def _odd_mix_kernel(*refs, with_cache, n_seq=1):
    q_ref, k_ref, v_ref = refs[:3]
    refs = refs[3:]
    if with_cache:
        kc_ref, vc_ref = refs[:2]
        refs = refs[2:]
    plan_tile = None
    if with_cache:
        y_ref, f_ref, x_ref, mod_ref, wo_ref = refs[:5]
        plan_in, (o_ref, *plan_out), (a_ref, vx_ref, vcx_ref) = refs[5:9], refs[9:14], refs[14:]
        plan_tile = (pl.program_id(0) * pl.num_programs(1) + pl.program_id(1)) * (q_ref.shape[0] // ROUTE_TM)

        @pl.when(pl.program_id(1) == 0)
        def _():
            for src, dst in ((v_ref, vx_ref), (vc_ref, vcx_ref)):
                one = lax.broadcasted_iota(jnp.int32, (src.shape[0], PAIR_W - DEN_COL), 1) == 0
                for pair in range(N_HEADS // 2):
                    dst[:, pair * PAIR_W:pair * PAIR_W + DEN_COL] = src[:, pair * DEN_COL:(pair + 1) * DEN_COL]
                    dst[:, pair * PAIR_W + DEN_COL:(pair + 1) * PAIR_W] = jnp.where(one, 1.0, 0.0).astype(BF16)
    else:
        y_ref, f_ref, x_ref, mod_ref, wo_ref, o_ref, a_ref = refs[:7]
        if len(refs) > 7:
            plan_in, plan_out, plan_tile = refs[7:]
    tq = q_ref.shape[0] // n_seq
    lane = lax.broadcasted_iota(jnp.int32, (tq, 2 * V_DIM), 1)
    f_parts = []
    for sq in range(n_seq):
        rows = slice(sq * tq, (sq + 1) * tq)
        for pair in range(N_HEADS // 2):
            vcols = slice(pair * PAIR_W, (pair + 1) * PAIR_W)
            outs = []
            for h in (2 * pair, 2 * pair + 1):
                hcols = slice(h * HEAD_PAD, (h + 1) * HEAD_PAD)
                qh = q_ref[rows, hcols]
                if with_cache:
                    s = lax.dot_general(qh, k_ref[:, hcols], NT_DIMS, preferred_element_type=F32)
                    sc = lax.dot_general(qh, kc_ref[:, hcols], NT_DIMS, preferred_element_type=F32)
                    top = jnp.maximum(jnp.max(s, axis=-1, keepdims=True), jnp.max(sc, axis=-1, keepdims=True))
                    acc = _bdot(jnp.exp2((s - top).astype(BF16)), vx_ref[:, vcols])
                    acc = acc + _bdot(jnp.exp2((sc - top).astype(BF16)), vcx_ref[:, vcols])
                    outs.append(acc[:, 0:2 * V_DIM] / acc[:, DEN_COL:DEN_COL + 1])
                else:
                    s = lax.dot_general(qh, k_ref[rows, hcols], NT_DIMS, preferred_element_type=F32)
                    p = jnp.exp2(s - jnp.max(s, axis=-1, keepdims=True))
                    den = jnp.sum(p, axis=-1, keepdims=True)
                    outs.append(_bdot(p.astype(BF16), v_ref[rows, pair * DEN_COL:(pair + 1) * DEN_COL]) / den)
            a_ref[rows, pair * 2 * V_DIM:(pair + 1) * 2 * V_DIM] = (
                jnp.where(lane < V_DIM, outs[0], outs[1]).astype(BF16))
        if n_seq > 1:
            f_parts.append(_bdot(f_ref[:, 0:tq], y_ref[0, rows, :]) + _bdot(f_ref[:, tq:], y_ref[1, rows, :]))
    f = jnp.concatenate(f_parts, axis=0) if n_seq > 1 else _bdot(f_ref[...], y_ref[...].reshape(-1, D_C))
    o = _bdot(f.astype(BF16), wo_ref[0:D_C, :]) + _bdot(a_ref[...], wo_ref[D_C:, :])
    x1 = x_ref[...] + mod_ref[2:3, :] * o
    o_ref[...] = x1
    if plan_tile is not None:
        _route_plan(x1, mod_ref[...], plan_in, plan_out, plan_tile)


WHOLE_SEQS = 2


def _odd_whole_kernel(tab_ref, x_ref, mod_ref, g_ref, w_ref, qg_ref, wq_ref, kg_ref, wk_ref, wv_ref, cs_ref, f_ref,
                      wo_ref, modp_ref, dloc_ref, ys_ref, pg_ref, pw_ref, pb_ref, pu_ref,
                      o_ref, ckv_ref, kpe_ref, ph_ref, pd_ref, pr_ref, pc_ref,
                      y_ref, q_ref, k_ref, v_ref, a_ref, x2_ref, runs_ref, sem_ref):
    bi = pl.program_id(0)
    par = lax.rem(bi, 2)
    tm = ROUTE_TM

    def fetch(step, parity):
        for sq in range(WHOLE_SEQS):
            _fetch_runs(tab_ref, ys_ref, runs_ref, sem_ref, step * WHOLE_SEQS + sq, parity * WHOLE_SEQS + sq)

    @pl.when(bi == 0)
    def _():
        fetch(0, 0)

    @pl.when(bi + 1 < pl.num_programs(0))
    def _():
        fetch(bi + 1, 1 - par)

    for sq in range(WHOLE_SEQS):
        rows = slice(sq * tm, (sq + 1) * tm)
        moe = _unsort(tab_ref, ys_ref, runs_ref, sem_ref, dloc_ref[rows, 0:1], bi * WHOLE_SEQS + sq,
                      par * WHOLE_SEQS + sq)
        x2_ref[rows, :] = x_ref[rows, :] + modp_ref[5:6, :] * moe
    _odd_in_kernel(x2_ref, mod_ref, g_ref, w_ref, qg_ref, wq_ref, kg_ref, wk_ref, wv_ref, cs_ref,
                   y_ref, q_ref, k_ref, v_ref, ckv_ref, kpe_ref, rope=False, emit_cache=True)
    _odd_mix_kernel(q_ref, k_ref, v_ref, y_ref, f_ref, x2_ref, mod_ref, wo_ref, o_ref, a_ref,
                    (pg_ref, pw_ref, pb_ref, pu_ref), (ph_ref, pd_ref, pr_ref, pc_ref), bi * WHOLE_SEQS,
                    with_cache=False, n_seq=WHOLE_SEQS)


def _odd_whole(x3, pending, mod, g, w_in, q_g, w_q, kv_g, w_k, w_v, cs, fmat, w_out, plan_consts, stream, layer):
    tab, dloc, ys = pending
    b, seq, _ = x3.shape
    assert seq == ROUTE_TM and b % WHOLE_SEQS == 0 and stream.shared_cond
    b, s = b // WHOLE_SEQS, seq * WHOLE_SEQS
    x3 = x3.reshape(b, s, D_MODEL)
    hq = N_HEADS * HEAD_PAD
    hv = N_HEADS * V_DIM
    const = lambda a: pl.BlockSpec(a.shape, lambda bi, tab_ref: (0,) * a.ndim)
    row_block = lambda w: pl.BlockSpec((None, s, w), lambda bi, tab_ref: (bi, 0, 0))
    mod_block = lambda lyr: pl.BlockSpec((None, None, 6, D_MODEL), lambda bi, tab_ref: (lyr, 0, 0, 0))
    cache_block = lambda w: pl.BlockSpec((WHOLE_SEQS, 1, seq, w), lambda bi, tab_ref: (bi, 0, 0, 0))
    plan_shapes, plan_specs = _plan_out(b, s, s, lambda bi, tab_ref: (bi, 0))
    grid_spec = pltpu.PrefetchScalarGridSpec(
        num_scalar_prefetch=1,
        grid=(b,),
        in_specs=[
            row_block(D_MODEL), mod_block(layer),
            const(g), const(w_in), const(q_g), const(w_q), const(kv_g), const(w_k), const(w_v), const(cs),
            const(fmat), const(w_out),
            mod_block(layer - 1),
            pl.BlockSpec((s, LANES), lambda bi, tab_ref: (bi, 0)),
            pl.BlockSpec(memory_space=pl.ANY),
        ] + [const(a) for a in plan_consts],
        out_specs=[row_block(D_MODEL), cache_block(KV_LORA), cache_block(QK_ROPE)] + plan_specs,
        scratch_shapes=[
            pltpu.VMEM((2, s, D_C), BF16),
            pltpu.VMEM((s, hq), BF16),
            pltpu.VMEM((s, hq), BF16),
            pltpu.VMEM((s, hv), BF16),
            pltpu.VMEM((s, hv), BF16),
            pltpu.VMEM((s, D_MODEL), F32),
            pltpu.VMEM((2 * WHOLE_SEQS, RUN_ROWS, D_MODEL), F32),
            pltpu.SemaphoreType.DMA((2 * WHOLE_SEQS,)),
        ],
    )
    out = pl.pallas_call(
        _odd_whole_kernel,
        grid_spec=grid_spec,
        out_shape=[
            jax.ShapeDtypeStruct((b, s, D_MODEL), F32),
            jax.ShapeDtypeStruct((b * WHOLE_SEQS, 1, seq, KV_LORA), F32),
            jax.ShapeDtypeStruct((b * WHOLE_SEQS, 1, seq, QK_ROPE), F32),
        ] + plan_shapes,
        compiler_params=_cparams("arbitrary"),
        name="odd_whole",
    )(tab, x3, mod, g, w_in, q_g, w_q, kv_g, w_k, w_v, cs, fmat, w_out, mod, dloc, ys, *plan_consts)
    return out[0].reshape(b * WHOLE_SEQS, seq, D_MODEL), out[1], out[2], out[3:]


def _odd_mix(q, k, v, kc, vc, y, fmat, x3, mod, w_out, plan_consts, stream, layer):
    b, s, hq = q.shape
    tq = min(s, 512)
    n_i = s // tq
    p = kc.shape[1]
    hv = N_HEADS * V_DIM

    def per_batch(rows, cols):
        return pl.BlockSpec((None, rows, cols), lambda bi, i: (bi, 0, 0), pipeline_mode=pl.Buffered(1))

    in_specs = [
        pl.BlockSpec((None, tq, hq), lambda bi, i: (bi, i, 0)),
        per_batch(s, hq), per_batch(s, hv), per_batch(p, hq), per_batch(p, hv),
        per_batch(2 * s, D_C),
        pl.BlockSpec((tq, 2 * s), lambda bi, i: (i, 0)),
        pl.BlockSpec((None, tq, D_MODEL), lambda bi, i: (bi, i, 0)),
        _mod_spec(stream, layer),
        pl.BlockSpec(w_out.shape, lambda bi, i: (0, 0), pipeline_mode=pl.Buffered(1)),
    ] + [pl.BlockSpec(a.shape, lambda bi, i: (0,) * a.ndim) for a in plan_consts]
    plan_shapes, plan_specs = _plan_out(b, s, tq, lambda bi, i: (bi, i))
    out = pl.pallas_call(
        functools.partial(_odd_mix_kernel, with_cache=True),
        grid=(b, n_i),
        in_specs=in_specs,
        out_specs=[pl.BlockSpec((None, tq, D_MODEL), lambda bi, i: (bi, i, 0))] + plan_specs,
        out_shape=[jax.ShapeDtypeStruct((b, s, D_MODEL), F32)] + plan_shapes,
        scratch_shapes=[pltpu.VMEM((tq, hv), BF16), pltpu.VMEM((s, VX_W), BF16), pltpu.VMEM((p, VX_W), BF16)],
        compiler_params=_cparams("arbitrary", "arbitrary"),
        name="odd_mix",
    )(q, k, v, kc, vc, y, fmat, x3, mod, w_out, *plan_consts)
    return out[0], out[1:]


def _dft_tables(seq):
    jc = np.arange(C_GW)
    ang_c = 2.0 * np.pi * np.outer(jc, jc) / C_GW
    eye = np.eye(C_GROUPS)
    cs = np.concatenate([np.kron(eye, np.cos(ang_c)), np.kron(eye, np.sin(ang_c))], axis=1)
    jn = np.arange(seq)
    ang_n = 2.0 * np.pi * (np.outer(jn, jn) % seq) / seq
    scale = 1.0 / math.sqrt(seq * C_GW)
    fmat = np.concatenate([np.cos(ang_n), -np.sin(ang_n)], axis=1) * scale
    return jnp.asarray(cs, F32).astype(BF16), jnp.asarray(fmat, F32).astype(BF16)


def _odd_weights(w_in, w_uq, w_ukv):
    d = w_in.shape[0]
    base = D_C + Q_LORA + KV_LORA
    kpe_blk = jnp.zeros((d, HEAD_PAD), w_in.dtype).at[:, ROPE_OFF:ROPE_OFF + QK_ROPE].set(w_in[:, base:])
    w_in_p = jnp.concatenate([w_in[:, :base], kpe_blk], axis=1).astype(BF16)
    qh = w_uq.reshape(Q_LORA, N_HEADS, QK_NOPE + QK_ROPE)
    w_q = jnp.pad(qh, ((0, 0), (0, 0), (0, HEAD_PAD - QK_NOPE - QK_ROPE))).reshape(Q_LORA, -1).astype(BF16)
    kvh = w_ukv.reshape(KV_LORA, N_HEADS, QK_NOPE + V_DIM)
    w_k = jnp.pad(kvh[:, :, :QK_NOPE], ((0, 0), (0, 0), (0, HEAD_PAD - QK_NOPE))).reshape(KV_LORA, -1)
    w_v = kvh[:, :, QK_NOPE:].reshape(KV_LORA, -1)
    return w_in_p, w_q, w_k.astype(BF16), w_v.astype(BF16)


ROUTER_ROWS = 32


def _router_weights(wg, bg, we, be):
    d = wg.shape[0]
    w = jnp.concatenate([wg, we.reshape(d, N_EXPERTS)], axis=1).T
    w = jnp.pad(w, ((0, ROUTER_ROWS - w.shape[0]), (0, 0))).astype(BF16)
    b = jnp.concatenate([bg, be.reshape(N_EXPERTS)])
    b = jnp.pad(b, (0, ROUTER_ROWS - b.shape[0])).reshape(ROUTER_ROWS, 1).astype(F32)
    return w, b


def kernel(x_prompt, x_sample, cache_ckv, cache_kpe, c, c_ctx, mod_w, mod_b, norm1_g, norm2_g,
           ev_w_in, ev_conv_w, ev_sgu_norm_g, ev_sgu_w, ev_sgu_b, ev_w_out,
           od_w_in, od_q_norm_g, od_w_uq, od_kv_norm_g, od_w_ukv, od_w_out,
           moe_wg, moe_bg, moe_we, moe_be, moe_w1, moe_w3, moe_w2, final_norm_g):
    bp, n_p, d = x_prompt.shape
    bs, n_s, _ = x_sample.shape
    streams = [(_Stream(bp, n_p, True), x_prompt), (_Stream(bs, n_s, False), x_sample)]

    n_rows = 1 + bs
    mod = _adaln(jnp.concatenate([c_ctx[None, :], c], axis=0), mod_w, mod_b, n_rows)

    final_g = final_norm_g.reshape(1, d)
    xs = [x for _, x in streams]
    new_ckv, new_kpe = [], []
    pending = None
    plans = [None, None]
    for l in range(DEPTH):
        j = l // 2
        g1 = norm1_g[l].reshape(1, d)
        g2 = norm2_g[l].reshape(1, d)
        plan_consts = _plan_consts(g2, *_router_weights(moe_wg[l], moe_bg[l], moe_we[l], moe_be[l]))
        last = l == DEPTH - 1
        if l % 2 == 0:
            w_in = ev_w_in[j].astype(BF16)
            w_out = ev_w_out[j].astype(BF16)
            sgu_w = ev_sgu_w[j].astype(BF16)
            sgu_g = ev_sgu_norm_g[j].reshape(1, D_B)
            sgu_bias = jnp.repeat(ev_sgu_b[j].T, D_B // B_GROUPS, axis=1)
            for si, (st, _) in enumerate(streams):
                xs[si], plans[si] = _even_layer(xs[si], mod, g1, w_in, ev_conv_w[j], sgu_g, sgu_w, sgu_bias, w_out,
                                                plan_consts, st, l)
        else:
            w_in, w_q, w_k, w_v = _odd_weights(od_w_in[j], od_w_uq[j], od_w_ukv[j])
            w_out = od_w_out[j].astype(BF16)
            q_g = od_q_norm_g[j].reshape(1, Q_LORA)
            kv_g = od_kv_norm_g[j].reshape(1, KV_LORA)
            for si, (st, _) in enumerate(streams):
                x3 = xs[si]
                cs, fmat = _dft_tables(st.seq)
                if st.shared_cond:
                    xs[si], ckv, kpe, plans[si] = _odd_whole(x3, pending, mod, g1, w_in, q_g, w_q, kv_g, w_k, w_v, cs,
                                                             fmat, w_out, plan_consts, st, l)
                    new_ckv.append(ckv)
                    new_kpe.append(kpe)
                    continue
                y, q, k, v = _odd_in(x3, mod, g1, w_in, q_g, w_q, kv_g, w_k, w_v, cs, st, l, _rope_tables(st.seq))
                kpe_blk = jnp.pad(cache_kpe[:, j], ((0, 0), (0, 0), (ROPE_OFF, HEAD_PAD - ROPE_OFF - QK_ROPE)))
                kc, vc = _cache_kv(cache_ckv[:, j], kpe_blk, w_k, w_v)
                xs[si], plans[si] = _odd_mix(q, k, v, kc, vc, y.reshape(st.batch, 2 * st.seq, D_C), fmat, x3, mod,
                                             w_out, plan_consts, st, l)
        defer = not last and (l + 1) % 2 == 1
        x2p, x2s, pending = _moe(xs[0].reshape(bp * n_p, d), xs[1].reshape(bs * n_s, d), plans[0], plans[1], mod,
                                 moe_w1, moe_w3, moe_w2, final_g, l, n_s, last, defer)
        xs = [xs[0] if defer else x2p.reshape(bp, n_p, d), x2s.reshape(bs, n_s, d)]
    return (xs[0], xs[1], jnp.concatenate(new_ckv, axis=1), jnp.concatenate(new_kpe, axis=1))
```

```python
import functools
import math

import numpy as np
import jax
import jax.numpy as jnp
from jax import lax
from jax.experimental import pallas as pl
from jax.experimental.pallas import tpu as pltpu

D_MODEL = 1024
DEPTH = 2
GRID_W = 64
D_A = D_MODEL // 2
D_B = D_MODEL // 2
B_GROUPS = 4
CHUNK = 128
D_EVEN_IN = 3 * D_A + 2 * D_B
D_C = D_MODEL // 4
C_GROUPS = 4
C_GW = D_C // C_GROUPS
N_HEADS = 12
QK_NOPE = 64
QK_ROPE = 32
V_DIM = 64
Q_LORA = 384
KV_LORA = 256
ROPE_BASE = 10000.0
N_GROUPS_MOE = 4
EXPERTS_PER_GROUP = 4
N_EXPERTS = N_GROUPS_MOE * EXPERTS_PER_GROUP
D_EXPERT = 256
EPS = 1e-6

LANES = 128
HEAD_PAD = 128
PAIR_W = 256
DEN_COL = 2 * 64
VX_W = 6 * PAIR_W
ROPE_OFF = QK_NOPE
GATE_OFF = N_GROUPS_MOE
NEG_BIG = -1e30
F32 = jnp.float32
BF16 = jnp.bfloat16
VMEM_LIMIT = 56 * 1024 * 1024


def _cparams(*sem):
    return pltpu.CompilerParams(dimension_semantics=sem, vmem_limit_bytes=VMEM_LIMIT)


def _rms(x, g):
    return x * lax.rsqrt(jnp.mean(x * x, axis=-1, keepdims=True) + EPS) * g


def _norm_mod(x, g, scale, shift):
    rs = lax.rsqrt(jnp.mean(x * x, axis=-1, keepdims=True) + EPS)
    return x * rs * (g * (1.0 + scale)) + shift


def _bdot(a, b):
    return jnp.dot(a, b, preferred_element_type=F32)


NT_DIMS = (((1,), (1,)), ((), ()))


MOD_ROWS = 8


def _mod_kernel(c_ref, w_ref, b_ref, o_ref):
    c = c_ref[...]
    s = c * jax.nn.sigmoid(c)
    s_hi = s.astype(BF16).astype(F32)
    pair = jnp.concatenate([s_hi, s - s_hi], axis=0).astype(BF16)
    two = _bdot(pair, w_ref[...].astype(BF16))
    o_ref[...] = two[0:MOD_ROWS] + two[MOD_ROWS:] + b_ref[...]


def _adaln(cond, mod_w, mod_b, n_rows):
    nt = 2048
    d6 = mod_w.shape[-1]
    cond = jnp.pad(cond, ((0, MOD_ROWS - n_rows), (0, 0)))
    out = pl.pallas_call(
        _mod_kernel,
        grid=(DEPTH, d6 // nt),
        in_specs=[
            pl.BlockSpec(cond.shape, lambda l, n: (0, 0)),
            pl.BlockSpec((None, D_MODEL, nt), lambda l, n: (l, 0, n)),
            pl.BlockSpec((None, 1, nt), lambda l, n: (l, 0, n)),
        ],
        out_specs=pl.BlockSpec((None, MOD_ROWS, nt), lambda l, n: (l, 0, n)),
        out_shape=jax.ShapeDtypeStruct((DEPTH, MOD_ROWS, d6), F32),
        compiler_params=_cparams("parallel", "parallel"),
        name="adaln",
    )(cond, mod_w, mod_b.reshape(DEPTH, 1, d6))
    return out[:, :n_rows].reshape(DEPTH, n_rows, 6, D_MODEL)


class _Stream:
    def __init__(self, batch, seq, shared_cond):
        self.batch = batch
        self.seq = seq
        self.shared_cond = shared_cond

    def row_of_batch(self, b):
        return 0 if self.shared_cond else b + 1


def _mod_spec(stream, layer):
    return pl.BlockSpec((None, None, 6, D_MODEL), lambda b, i: (layer, stream.row_of_batch(b), 0, 0))


HALO = 8
EVEN_SEQS = 4


def _even_kernel(*refs, has_halo, seq_rows, nc):
    if has_halo:
        x_ref, xp_ref, xn_ref = refs[:3]
        refs = refs[3:]
    else:
        x_ref = refs[0]
        refs = refs[1:]
    mod_ref, g_ref, wi_ref, cw_ref, sg_ref, sw_ref, sb_ref, wo_ref = refs[:8]
    plan_in, (o_ref, *plan_out), (z_ref, y_ref) = refs[8:12], refs[12:17], refs[17:]
    i = pl.program_id(1)
    n_i = pl.num_programs(1)
    ts = x_ref.shape[0]
    m = mod_ref[...]
    g = g_ref[...]

    def modulate(x):
        return _norm_mod(x, g, m[1:2], m[0:1]).astype(BF16)

    x = x_ref[...]
    hb = modulate(x)
    for n in range(D_EVEN_IN // nc):
        z_ref[:, n * nc:(n + 1) * nc] = _bdot(hb, wi_ref[:, n * nc:(n + 1) * nc]).astype(BF16)

    gate_b = z_ref[:, 0:D_A].astype(F32)
    gate_c = z_ref[:, D_A:2 * D_A].astype(F32)
    xa = z_ref[:, 2 * D_A:3 * D_A].astype(F32)
    t = gate_c * xa
    t_prev = pltpu.roll(t, 1, axis=0)
    t_next = pltpu.roll(t, ts - 1, axis=0)
    row = lax.broadcasted_iota(jnp.int32, (ts, 1), 0) & (seq_rows - 1)
    if has_halo:
        hh = modulate(jnp.concatenate([xp_ref[...], xn_ref[...]], axis=0))
        zh = _bdot(hh, wi_ref[:, D_A:3 * D_A]).astype(BF16).astype(F32)
        th = zh[:, 0:D_A] * zh[:, D_A:2 * D_A]
        tp = th[HALO - 1:HALO] * (i > 0).astype(F32)
        tn = th[HALO:HALO + 1] * (i < n_i - 1).astype(F32)
    else:
        tp = tn = 0.0
    t_prev = jnp.where(row == 0, tp, t_prev)
    t_next = jnp.where(row == seq_rows - 1, tn, t_next)
    cw = cw_ref[...]
    y_a = gate_b * (t_prev * cw[0:1] + t * cw[1:2] + t_next * cw[2:3])
    y_ref[:, 0:D_A] = y_a.astype(BF16)

    u = z_ref[:, 3 * D_A:3 * D_A + D_B].astype(F32)
    v = z_ref[:, 3 * D_A + D_B:3 * D_A + 2 * D_B].astype(F32)
    vb = _rms(v, sg_ref[...]).astype(BF16)
    gw = D_B // B_GROUPS
    for c in range(ts // CHUNK):
        rows = slice(c * CHUNK, (c + 1) * CHUNK)
        for gi in range(B_GROUPS):
            cols = slice(gi * gw, (gi + 1) * gw)
            sv = _bdot(sw_ref[gi], vb[rows, cols]) + sb_ref[:, cols]
            y_ref[rows, D_A + gi * gw:D_A + (gi + 1) * gw] = (u[rows, cols] * sv).astype(BF16)

    x1 = x + m[2:3] * _bdot(y_ref[...], wo_ref[...])
    o_ref[...] = x1
    _route_plan(x1, m, plan_in, plan_out, (pl.program_id(0) * n_i + i) * (ts // ROUTE_TM))


def _even_layer(x3, mod, g, w_in, conv_w, sgu_g, sgu_w, sgu_bias, w_out, plan_consts, stream, layer):
    out_shape = x3.shape
    b, s, _ = x3.shape
    ts = min(s, 256) if stream.shared_cond else min(s, 1024)
    n_i = s // ts
    has_halo = n_i > 1
    seq_rows = ts
    if not has_halo and stream.shared_cond and b % EVEN_SEQS == 0:
        b, s, ts = b // EVEN_SEQS, EVEN_SEQS * s, EVEN_SEQS * ts
        x3 = x3.reshape(b, s, D_MODEL)
    hb = ts // HALO
    last_h = s // HALO - 1
    const = lambda a: pl.BlockSpec(a.shape, lambda bi, i: (0,) * a.ndim)
    in_specs = [pl.BlockSpec((None, ts, D_MODEL), lambda bi, i: (bi, i, 0))]
    args = [x3]
    if has_halo:
        in_specs += [
            pl.BlockSpec((None, HALO, D_MODEL), lambda bi, i: (bi, jnp.maximum(i * hb - 1, 0), 0)),
            pl.BlockSpec((None, HALO, D_MODEL), lambda bi, i: (bi, jnp.minimum((i + 1) * hb, last_h), 0)),
        ]
        args += [x3, x3]
    in_specs += [_mod_spec(stream, layer), const(g), const(w_in), const(conv_w), const(sgu_g), const(sgu_w),
                 const(sgu_bias), const(w_out)] + [const(a) for a in plan_consts]
    args += [mod, g, w_in, conv_w, sgu_g, sgu_w, sgu_bias, w_out] + plan_consts
    plan_shapes, plan_specs = _plan_out(b, s, ts, lambda bi, i: (bi, i))
    out = pl.pallas_call(
        functools.partial(_even_kernel, has_halo=has_halo, seq_rows=seq_rows, nc=512),
        grid=(b, n_i),
        in_specs=in_specs,
        out_specs=[pl.BlockSpec((None, ts, D_MODEL), lambda bi, i: (bi, i, 0))] + plan_specs,
        out_shape=[jax.ShapeDtypeStruct((b, s, D_MODEL), F32)] + plan_shapes,
        scratch_shapes=[pltpu.VMEM((ts, D_EVEN_IN), BF16), pltpu.VMEM((ts, D_A + D_B), BF16)],
        compiler_params=_cparams("arbitrary", "arbitrary"),
        name="even_layer",
    )(*args)
    return out[0].reshape(out_shape), out[1:]


ROUTE_TM = 256
ROUTE_SUBS = 4
ROUTE_STEP = ROUTE_TM * ROUTE_SUBS
ROUTE_PAD = 8
SORT_ROWS = ROUTE_TM + LANES
RUN_ROWS = SORT_ROWS + 32
ROUTE_ROWS = ROUTE_TM + 4 * ROUTE_PAD
XS_W = D_MODEL + LANES
GATE_LO = EXPERTS_PER_GROUP
DLOC_HI = 2 * EXPERTS_PER_GROUP
DLOC_RADIX = 16.0
FFN_BM = 512
FFN_HALF = FFN_BM // 2
RUN_SIZES = (256, 128, 64, 32, 16, 8)
TAB_W = 2 * N_GROUPS_MOE


def _round_up(x, m):
    return lax.div(x + (m - 1), m) * m


def _run_copies(tab_ref, tile, hbm_ref, vmem_ref, sem, to_hbm, wait):
    off = 0
    for g in range(N_GROUPS_MOE):
        start = tab_ref[tile * TAB_W + g]
        n = tab_ref[tile * TAB_W + N_GROUPS_MOE + g]
        for k, p in enumerate(RUN_SIZES):
            done = n & (-2 * p)

            @pl.when((n & p) != 0)
            def _():
                v = vmem_ref.at[pl.ds(pl.multiple_of(off + done, ROUTE_PAD), p)]
                h = hbm_ref.at[pl.ds(pl.multiple_of(start + done, ROUTE_PAD), p)]
                cp = pltpu.make_async_copy(v, h, sem) if to_hbm else pltpu.make_async_copy(h, v, sem)
                if wait:
                    cp.wait()
                else:
                    cp.start(priority=(g + k) % 2)
        off = off + n


def _zero_fill(tab_ref, meta, zeros_ref, hbm_ref, sem, n_rows, wait):
    def copy(rows, dst_row):
        cp = pltpu.make_async_copy(zeros_ref.at[pl.ds(0, rows)],
                                   hbm_ref.at[pl.ds(pl.multiple_of(dst_row, ROUTE_PAD), rows)], sem)
        if wait:
            cp.wait()
        else:
            cp.start()

    end = 0
    for g in range(N_GROUPS_MOE):
        fill = tab_ref[meta + g]
        start = tab_ref[meta + N_GROUPS_MOE + g]
        end = start + _round_up(fill, FFN_BM)
        tail = end - start - fill
        for p in RUN_SIZES:
            pl.when((tail & p) != 0)(functools.partial(copy, p, start + fill + (tail & (-2 * p))))
    for k in range(n_rows // FFN_BM):
        pl.when(end + k * FFN_BM < n_rows)(functools.partial(copy, FFN_BM, end + k * FFN_BM))


def _wait_rows(n, hbm_ref, vmem_ref, sem, to_hbm):
    for p in RUN_SIZES:
        @pl.when((n & p) != 0)
        def _():
            v = vmem_ref.at[pl.ds(0, p)]
            h = hbm_ref.at[pl.ds(0, p)]
            (pltpu.make_async_copy(v, h, sem) if to_hbm else pltpu.make_async_copy(h, v, sem)).wait()


def _tile_rows(tab_ref, tile):
    n = 0
    for g in range(N_GROUPS_MOE):
        n = n + tab_ref[tile * TAB_W + N_GROUPS_MOE + g]
    return n


def _max4(v):
    return jnp.maximum(jnp.maximum(v[0], v[1]), jnp.maximum(v[2], v[3]))


def _first_of4(v, top):
    return jnp.where(v[0] == top, 0.0, jnp.where(v[1] == top, 1.0, jnp.where(v[2] == top, 2.0, 3.0)))


def _route_plan(x, m, plan_in, plan_out, tile0):
    g_ref, wrt_ref, brt_ref, upper_ref = plan_in
    haug_ref, dloc_ref, drow_ref, cnt_ref = plan_out
    rows = x.shape[0]
    tm = ROUTE_TM
    ng = N_GROUPS_MOE
    h = _norm_mod(x, g_ref[...], m[4:5], m[3:4])
    hb = h.astype(BF16)
    lt = lax.dot_general(wrt_ref[...], hb, NT_DIMS, preferred_element_type=F32) + brt_ref[...]
    gl = [lt[r:r + 1, :] for r in range(ng)]
    g_top = _max4(gl)
    g_idx = _first_of4(gl, g_top)
    g_w = 1.0 / (jnp.exp(gl[0] - g_top) + jnp.exp(gl[1] - g_top) + jnp.exp(gl[2] - g_top) + jnp.exp(gl[3] - g_top))
    ev = []
    for k in range(EXPERTS_PER_GROUP):
        cand = [lt[GATE_OFF + EXPERTS_PER_GROUP * r + k:GATE_OFF + EXPERTS_PER_GROUP * r + k + 1, :]
                for r in range(ng)]
        ev.append(jnp.where(g_idx == 0.0, cand[0], jnp.where(g_idx == 1.0, cand[1],
                            jnp.where(g_idx == 2.0, cand[2], cand[3]))))
    v1 = _max4(ev)
    i1 = _first_of4(ev, v1)
    rest = [jnp.where(i1 == float(k), NEG_BIG, ev[k]) for k in range(EXPERTS_PER_GROUP)]
    v2 = _max4(rest)
    i2 = _first_of4(rest, v2)
    e2 = jnp.exp(v2 - v1)
    w1 = 1.0 / (1.0 + e2)
    w2 = e2 * w1
    gates = [g_w * (jnp.where(i1 == float(k), w1, 0.0) + jnp.where(i2 == float(k), w2, 0.0))
             for k in range(EXPERTS_PER_GROUP)]

    sub8 = lax.broadcasted_iota(jnp.int32, (8, tm), 0).astype(F32)
    dlocs = []
    for sub in range(rows // tm):
        gi = g_idx[:, sub * tm:(sub + 1) * tm]
        hot = jnp.where(sub8 == gi, 1.0, 0.0)
        before = _bdot(hot.astype(BF16), upper_ref[...])
        dl = jnp.sum(before * hot, axis=0, keepdims=True)
        off = 0
        for g in range(ng):
            n_g = _round_up(jnp.sum(hot[g:g + 1, :]).astype(jnp.int32), ROUTE_PAD)
            cnt_ref[(tile0 + sub) * ng + g] = n_g
            dl = dl + jnp.where(gi == float(g), off.astype(F32) if g else 0.0, 0.0)
            off = off + n_g
        drow_ref[sub] = jnp.broadcast_to(dl, (8, tm))
        dlocs.append(dl)
    dloc = jnp.concatenate(dlocs, axis=1)
    d_hi = jnp.floor(dloc * (1.0 / DLOC_RADIX))
    g_hi = [gt.astype(BF16).astype(F32) for gt in gates]
    ex_rows = g_hi + [gt - gh for gt, gh in zip(gates, g_hi)] + [d_hi, dloc - DLOC_RADIX * d_hi]
    sub16 = lax.broadcasted_iota(jnp.int32, (16, rows), 0)
    ex_t = jnp.zeros((16, rows), F32)
    for r, row in enumerate(ex_rows):
        ex_t = jnp.where(sub16 == r, row, ex_t)
    ex_t = jnp.concatenate([ex_t, jnp.zeros((LANES - 16, rows), F32)], axis=0)
    extras = ex_t.T
    dloc_ref[...] = jnp.broadcast_to(
        DLOC_RADIX * extras[:, DLOC_HI:DLOC_HI + 1] + extras[:, DLOC_HI + 1:DLOC_HI + 2], (rows, LANES))
    haug_ref[...] = jnp.concatenate([hb, extras.astype(BF16)], axis=1)


def _plan_consts(g2, w_rt, b_rt):
    upper = jnp.asarray(np.triu(np.ones((ROUTE_TM, ROUTE_TM), np.float32), 1), BF16)
    return [g2, w_rt, b_rt, upper]


def _plan_out(b, s, rows, index):
    tiles = rows // ROUTE_TM
    shapes = [jax.ShapeDtypeStruct((b, s, XS_W), BF16), jax.ShapeDtypeStruct((b, s, LANES), F32),
              jax.ShapeDtypeStruct((b, s // ROUTE_TM, 8, ROUTE_TM), F32),
              jax.ShapeDtypeStruct((b * (s // ROUTE_TM) * N_GROUPS_MOE,), jnp.int32)]
    specs = [pl.BlockSpec((None, rows, XS_W), lambda *idx: (*index(*idx), 0)),
             pl.BlockSpec((None, rows, LANES), lambda *idx: (*index(*idx), 0)),
             pl.BlockSpec((None, tiles, 8, ROUTE_TM), lambda *idx: (*index(*idx), 0, 0)),
             pl.BlockSpec(memory_space=pltpu.SMEM)]
    return shapes, specs


def _flat_plan(plan):
    haug, dloc, drow, cnt = plan
    return (haug.reshape(-1, XS_W), dloc.reshape(-1, LANES), drow.reshape(-1, 8, ROUTE_TM), cnt)


def _dispatch_kernel(cp_ref, cs_ref, hp_ref, hs_ref, dp_ref, ds_ref, tab_ref, sorted_hbm,
                     sorted_ref, zeros_ref, fill_ref, sem_ref, zsem_ref, *, n_steps, n_p_steps):
    i = pl.program_id(0)
    tm = ROUTE_TM
    ng = N_GROUPS_MOE
    n_tiles = n_steps * ROUTE_SUBS
    n_p_tiles = n_p_steps * ROUTE_SUBS
    meta = n_tiles * TAB_W
    n_rows = sorted_hbm.shape[0]

    def count(tile, g):
        if isinstance(tile, int):
            return cp_ref[tile * ng + g] if tile < n_p_tiles else cs_ref[(tile - n_p_tiles) * ng + g]
        return jnp.where(tile < n_p_tiles, cp_ref[jnp.minimum(tile, n_p_tiles - 1) * ng + g],
                         cs_ref[jnp.maximum(tile - n_p_tiles, 0) * ng + g])

    @pl.when(i == 0)
    def _():
        start = 0
        for g in range(ng):
            fill = sum(count(t, g) for t in range(n_tiles))
            tab_ref[meta + g] = fill
            tab_ref[meta + ng + g] = start
            start = start + _round_up(fill, FFN_BM)
            fill_ref[g] = 0
        zeros_ref[...] = jnp.zeros_like(zeros_ref)

    is_prompt = i < n_p_steps
    row_f = lax.broadcasted_iota(jnp.int32, (ROUTE_ROWS, tm), 0).astype(F32)

    def sort_tile(h_ref, d_ref, sub):
        onehot = jnp.where(row_f == d_ref[sub][0:1, :], 1.0, 0.0).astype(BF16)
        sorted_ref[sub] = _bdot(onehot, h_ref[sub * tm:(sub + 1) * tm, :])

    for sub in range(ROUTE_SUBS):
        tile = i * ROUTE_SUBS + sub

        @pl.when(i >= 1)
        def _():
            _wait_rows(_tile_rows(tab_ref, tile - ROUTE_SUBS), sorted_hbm, sorted_ref.at[sub], sem_ref.at[sub], True)

        pl.when(is_prompt)(functools.partial(sort_tile, hp_ref, dp_ref, sub))
        pl.when(jnp.logical_not(is_prompt))(functools.partial(sort_tile, hs_ref, ds_ref, sub))
        for g in range(ng):
            n_g = count(tile, g)
            tab_ref[tile * TAB_W + g] = tab_ref[meta + ng + g] + fill_ref[g]
            tab_ref[tile * TAB_W + ng + g] = n_g
            fill_ref[g] = fill_ref[g] + n_g
        _run_copies(tab_ref, tile, sorted_hbm, sorted_ref.at[sub], sem_ref.at[sub], to_hbm=True, wait=False)

        @pl.when(i == n_steps - 1)
        def _():
            _wait_rows(_tile_rows(tab_ref, tile), sorted_hbm, sorted_ref.at[sub], sem_ref.at[sub], True)

    @pl.when(i == n_steps - 1)
    def _():
        _zero_fill(tab_ref, meta, zeros_ref, sorted_hbm, zsem_ref, n_rows, wait=False)
        _zero_fill(tab_ref, meta, zeros_ref, sorted_hbm, zsem_ref, n_rows, wait=True)


def _merged_specs(n_p_steps, n_s_steps, layer, sample_seq, step_of, width=D_MODEL):
    def p_map(*idx):
        return (jnp.minimum(step_of(*idx), n_p_steps - 1), 0)

    def s_map(*idx):
        return (jnp.clip(step_of(*idx) - n_p_steps, 0, n_s_steps - 1), 0)

    def mod_map(*idx):
        j = step_of(*idx)
        row = jnp.where(j < n_p_steps, 0, 1 + lax.div(jnp.maximum(j - n_p_steps, 0) * ROUTE_STEP, sample_seq))
        return (layer, row, 0, 0)

    return (pl.BlockSpec((ROUTE_STEP, width), p_map), pl.BlockSpec((ROUTE_STEP, width), s_map),
            pl.BlockSpec((None, None, 6, D_MODEL), mod_map))


def _dispatch(plan_p, plan_s, n_rows):
    haug_p, _, drow_p, cnt_p = plan_p
    haug_s, _, drow_s, cnt_s = plan_s
    n_p_steps = haug_p.shape[0] // ROUTE_STEP
    n_s_steps = haug_s.shape[0] // ROUTE_STEP
    n_steps = n_p_steps + n_s_steps
    n_tiles = n_steps * ROUTE_SUBS
    step_of = lambda i, cp, cs: i
    hp_spec, hs_spec, _ = _merged_specs(n_p_steps, n_s_steps, 0, 1, step_of, XS_W)
    drow_block = (ROUTE_SUBS, 8, ROUTE_TM)
    grid_spec = pltpu.PrefetchScalarGridSpec(
        num_scalar_prefetch=2,
        grid=(n_steps,),
        in_specs=[
            hp_spec, hs_spec,
            pl.BlockSpec(drow_block, lambda i, cp, cs: (jnp.minimum(i, n_p_steps - 1), 0, 0)),
            pl.BlockSpec(drow_block, lambda i, cp, cs: (jnp.clip(i - n_p_steps, 0, n_s_steps - 1), 0, 0)),
        ],
        out_specs=[pl.BlockSpec(memory_space=pltpu.SMEM), pl.BlockSpec(memory_space=pl.ANY)],
        scratch_shapes=[
            pltpu.VMEM((ROUTE_SUBS, ROUTE_ROWS, XS_W), F32),
            pltpu.VMEM((FFN_BM, XS_W), F32),
            pltpu.SMEM((N_GROUPS_MOE,), jnp.int32),
            pltpu.SemaphoreType.DMA((ROUTE_SUBS,)),
            pltpu.SemaphoreType.DMA(()),
        ],
    )
    return pl.pallas_call(
        functools.partial(_dispatch_kernel, n_steps=n_steps, n_p_steps=n_p_steps),
        grid_spec=grid_spec,
        out_shape=[
            jax.ShapeDtypeStruct(((n_tiles + 1) * TAB_W,), jnp.int32),
            jax.ShapeDtypeStruct((n_rows, XS_W), F32),
        ],
        compiler_params=_cparams("arbitrary"),
        name="moe_dispatch",
    )(cnt_p, cnt_s, haug_p, haug_s, drow_p, drow_s)


def _ffn_lookup(i, tab_ref, meta):
    fills = [tab_ref[meta + g] for g in range(N_GROUPS_MOE)]
    edges = []
    acc = 0
    for f in fills:
        acc = acc + lax.div(f + (FFN_BM - 1), FFN_BM)
        edges.append(acc)
    total = edges[-1]
    ii = jnp.minimum(i, total - 1)
    grp = sum((ii >= e).astype(jnp.int32) for e in edges[:-1])

    def pick(vals):
        return jnp.where(grp == 0, vals[0], jnp.where(grp == 1, vals[1], jnp.where(grp == 2, vals[2], vals[3])))

    first = pick([0] + edges[:-1])
    return grp, total, ii == first, pick(fills) - (ii - first) * FFN_BM


def _ffn_group_kernel(tab_ref, xs_ref, w1_ref, w3_ref, w2_ref, ys_ref, w1b_ref, w3b_ref, w2b_ref, *, meta):
    i = pl.program_id(0)
    _, total, first_of_group, valid = _ffn_lookup(i, tab_ref, meta)
    active = i < total

    @pl.when(jnp.logical_and(active, first_of_group))
    def _():
        w1b_ref[...] = w1_ref[...].astype(BF16)
        w3b_ref[...] = w3_ref[...].astype(BF16)
        w2b_ref[...] = w2_ref[...].astype(BF16)

    def run(rows):
        hb = xs_ref[0:rows, 0:D_MODEL].astype(BF16)
        ex = xs_ref[0:rows, D_MODEL:XS_W]
        hid = []
        for e in range(EXPERTS_PER_GROUP):
            a = _bdot(hb, w1b_ref[e])
            b = _bdot(hb, w3b_ref[e])
            gate = ex[:, e:e + 1] + ex[:, GATE_LO + e:GATE_LO + e + 1]
            hid.append(((a * jax.nn.sigmoid(a)) * b * gate).astype(BF16))
        ys_ref[0:rows, :] = _bdot(jnp.concatenate(hid, axis=1), w2b_ref[...].reshape(-1, D_MODEL))

    @pl.when(jnp.logical_and(active, valid > FFN_HALF))
    def _():
        run(FFN_BM)

    @pl.when(jnp.logical_and(active, valid <= FFN_HALF))
    def _():
        run(FFN_HALF)
        ys_ref[FFN_HALF:, :] = jnp.zeros((FFN_BM - FFN_HALF, D_MODEL), F32)

    @pl.when(jnp.logical_not(active))
    def _():
        ys_ref[...] = jnp.zeros_like(ys_ref)


def _ffn_group(tab, xs, w1, w3, w2, layer, meta):
    e4 = EXPERTS_PER_GROUP
    n_blocks = xs.shape[0] // FFN_BM
    group_of = lambda i, tab_ref: _ffn_lookup(i, tab_ref, meta)[0]
    grid_spec = pltpu.PrefetchScalarGridSpec(
        num_scalar_prefetch=1,
        grid=(n_blocks,),
        in_specs=[
            pl.BlockSpec((FFN_BM, XS_W), lambda i, tab_ref: (i, 0)),
            pl.BlockSpec((None, e4, D_MODEL, D_EXPERT), lambda i, tab_ref: (layer, group_of(i, tab_ref), 0, 0)),
            pl.BlockSpec((None, e4, D_MODEL, D_EXPERT), lambda i, tab_ref: (layer, group_of(i, tab_ref), 0, 0)),
            pl.BlockSpec((None, e4, D_EXPERT, D_MODEL), lambda i, tab_ref: (layer, group_of(i, tab_ref), 0, 0)),
        ],
        out_specs=pl.BlockSpec((FFN_BM, D_MODEL), lambda i, tab_ref: (i, 0)),
        scratch_shapes=[
            pltpu.VMEM((e4, D_MODEL, D_EXPERT), BF16),
            pltpu.VMEM((e4, D_MODEL, D_EXPERT), BF16),
            pltpu.VMEM((e4, D_EXPERT, D_MODEL), BF16),
        ],
    )
    return pl.pallas_call(
        functools.partial(_ffn_group_kernel, meta=meta),
        grid_spec=grid_spec,
        out_shape=jax.ShapeDtypeStruct((xs.shape[0], D_MODEL), F32),
        compiler_params=_cparams("arbitrary"),
        name="moe_ffn",
    )(tab, xs, w1, w3, w2)


def _fetch_runs(tab_ref, ys_ref, runs_ref, sem_ref, tile, slot):
    _run_copies(tab_ref, tile, ys_ref, runs_ref.at[slot], sem_ref.at[slot], to_hbm=False, wait=False)


def _unsort(tab_ref, ys_ref, runs_ref, sem_ref, dloc_col, tile, slot):
    covered = _tile_rows(tab_ref, tile)
    _wait_rows(covered, ys_ref, runs_ref.at[slot], sem_ref.at[slot], False)
    runs_ref[slot, pl.ds(pl.multiple_of(covered, ROUTE_PAD), LANES), :] = jnp.zeros((LANES, D_MODEL), F32)
    yb = runs_ref[slot, 0:SORT_ROWS, :].astype(BF16)
    row_f = lax.broadcasted_iota(jnp.int32, (ROUTE_TM, SORT_ROWS), 1).astype(F32)
    return _bdot(jnp.where(row_f == dloc_col, 1.0, 0.0).astype(BF16), yb)


def _combine_kernel(tab_ref, *refs, n_steps, first_step, n_p_steps, final_norm):
    with_prompt = first_step < n_p_steps
    if with_prompt:
        xp_ref, xs_ref, mod_ref, dp_ref, ds_ref, fg_ref, ys_ref, op_ref, os_ref, runs_ref, sem_ref = refs
    else:
        xs_ref, mod_ref, ds_ref, fg_ref, ys_ref, os_ref, runs_ref, sem_ref = refs
    i = pl.program_id(0)
    par = lax.rem(i, 2)
    step = first_step + i
    tm = ROUTE_TM

    def fetch(st, parity):
        for sub in range(ROUTE_SUBS):
            _fetch_runs(tab_ref, ys_ref, runs_ref, sem_ref, st * ROUTE_SUBS + sub, parity * ROUTE_SUBS + sub)

    @pl.when(i == 0)
    def _():
        fetch(first_step, 0)

    @pl.when(i + 1 < n_steps)
    def _():
        fetch(step + 1, 1 - par)

    is_prompt = step < n_p_steps
    dloc = jnp.where(is_prompt, dp_ref[:, 0:1], ds_ref[:, 0:1]) if with_prompt else ds_ref[:, 0:1]
    parts = [_unsort(tab_ref, ys_ref, runs_ref, sem_ref, dloc[sub * tm:(sub + 1) * tm],
                     step * ROUTE_SUBS + sub, par * ROUTE_SUBS + sub) for sub in range(ROUTE_SUBS)]
    delta = mod_ref[5:6, :] * jnp.concatenate(parts, axis=0)

    def finish(x_ref, o_ref):
        x2 = x_ref[...] + delta
        o_ref[...] = _rms(x2, fg_ref[...]) if final_norm else x2

    if with_prompt:
        pl.when(is_prompt)(functools.partial(finish, xp_ref, op_ref))
        pl.when(jnp.logical_not(is_prompt))(functools.partial(finish, xs_ref, os_ref))
    else:
        finish(xs_ref, os_ref)


def _combine(tab, xp, xs, mod, dloc_p, dloc_s, final_g, ys, layer, sample_seq, final_norm, with_prompt):
    n_p_steps = xp.shape[0] // ROUTE_STEP
    n_s_steps = xs.shape[0] // ROUTE_STEP
    first_step = 0 if with_prompt else n_p_steps
    n_steps = n_p_steps + n_s_steps - first_step
    step_of = lambda i, tab_ref: i + first_step
    p_spec, s_spec, mod_spec = _merged_specs(n_p_steps, n_s_steps, layer, sample_seq, step_of)
    dp_spec, ds_spec, _ = _merged_specs(n_p_steps, n_s_steps, layer, sample_seq, step_of, LANES)
    streams = [p_spec, s_spec] if with_prompt else [s_spec]
    dlocs = [dp_spec, ds_spec] if with_prompt else [ds_spec]
    grid_spec = pltpu.PrefetchScalarGridSpec(
        num_scalar_prefetch=1,
        grid=(n_steps,),
        in_specs=streams + [mod_spec] + dlocs + [
            pl.BlockSpec((1, D_MODEL), lambda i, tab_ref: (0, 0)),
            pl.BlockSpec(memory_space=pl.ANY),
        ],
        out_specs=streams,
        scratch_shapes=[
            pltpu.VMEM((2 * ROUTE_SUBS, RUN_ROWS, D_MODEL), F32),
            pltpu.SemaphoreType.DMA((2 * ROUTE_SUBS,)),
        ],
    )
    arrays = [xp, xs] if with_prompt else [xs]
    dloc_arrays = [dloc_p, dloc_s] if with_prompt else [dloc_s]
    return pl.pallas_call(
        functools.partial(_combine_kernel, n_steps=n_steps, first_step=first_step, n_p_steps=n_p_steps,
                          final_norm=final_norm),
        grid_spec=grid_spec,
        out_shape=[jax.ShapeDtypeStruct(a.shape, F32) for a in arrays],
        compiler_params=_cparams("arbitrary"),
        name="moe_combine",
    )(tab, *arrays, mod, *dloc_arrays, final_g, ys)


def _moe(xp, xs, plan_p, plan_s, mod, w1, w3, w2, final_g, layer, sample_seq, final_norm, defer_prompt):
    plan_p, plan_s = _flat_plan(plan_p), _flat_plan(plan_s)
    t = xp.shape[0] + xs.shape[0]
    n_tiles = t // ROUTE_TM
    max_rows = t + N_GROUPS_MOE * (ROUTE_PAD - 1) * n_tiles
    n_rows = (-(-max_rows // FFN_BM) + N_GROUPS_MOE) * FFN_BM
    tab, sorted_x = _dispatch(plan_p, plan_s, n_rows)
    ys = _ffn_group(tab, sorted_x, w1, w3, w2, layer, n_tiles * TAB_W)
    out = _combine(tab, xp, xs, mod, plan_p[1], plan_s[1], final_g, ys, layer, sample_seq, final_norm,
                   not defer_prompt)
    if defer_prompt:
        return None, out[0], (tab, plan_p[1], ys)
    return out[0], out[1], None


def _rope_tables(seq):
    half = QK_ROPE // 2
    nf = half // 2
    inv = ROPE_BASE ** (-np.arange(nf, dtype=np.float64) / nf)
    pos = np.arange(seq)
    row = (pos // GRID_W).astype(np.float64)
    col = (pos % GRID_W).astype(np.float64)
    cos = np.ones((seq, HEAD_PAD), np.float64)
    sin_a = np.zeros((seq, HEAD_PAD), np.float64)
    sin_b = np.zeros((seq, HEAD_PAD), np.float64)
    for part, p in enumerate((row, col)):
        ang = p[:, None] * inv[None, :]
        base = ROPE_OFF + part * half
        cos[:, base:base + nf] = np.cos(ang)
        cos[:, base + nf:base + half] = np.cos(ang)
        sin_a[:, base:base + nf] = -np.sin(ang)
        sin_b[:, base + nf:base + half] = np.sin(ang)
    return tuple(jnp.asarray(a, F32) for a in (cos, sin_a, sin_b))


def _apply_rope(x, cos, sin_a, sin_b, reps):
    nf = QK_ROPE // 4
    width = x.shape[1]
    if reps > 1:
        cos, sin_a, sin_b = (jnp.concatenate([a] * reps, axis=1) for a in (cos, sin_a, sin_b))
    return x * cos + pltpu.roll(x, width - nf, axis=1) * sin_a + pltpu.roll(x, nf, axis=1) * sin_b


def _odd_in_kernel(*refs, rope, emit_cache):
    x_ref, mod_ref, g_ref, w_ref, qg_ref, wq_ref, kg_ref, wk_ref, wv_ref, cs_ref = refs[:10]
    refs = refs[10:]
    if rope:
        cos_ref, sa_ref, sb_ref = refs[:3]
        refs = refs[3:]
    y_ref, q_ref, k_ref, v_ref = refs[:4]
    refs = refs[4:]
    m = mod_ref[...]
    h = _norm_mod(x_ref[...], g_ref[...], m[1:2], m[0:1])
    z = _bdot(h.astype(BF16), w_ref[...])
    zc = z[:, 0:D_C]
    qc = z[:, D_C:D_C + Q_LORA]
    kvc = z[:, D_C + Q_LORA:D_C + Q_LORA + KV_LORA]
    kpe = z[:, D_C + Q_LORA + KV_LORA:]
    q = _bdot(_rms(qc, qg_ref[...]).astype(BF16), wq_ref[...])
    kvn = _rms(kvc, kg_ref[...])
    if emit_cache:
        ckv_ref, kpe_ref = refs
        ckv_ref[...] = kvn.reshape(ckv_ref.shape)
        kpe_ref[...] = kpe[:, ROPE_OFF:ROPE_OFF + QK_ROPE].reshape(kpe_ref.shape)
    if rope:
        tabs = (cos_ref[...], sa_ref[...], sb_ref[...])
        q = _apply_rope(q, *tabs, reps=N_HEADS)
        kpe = _apply_rope(kpe, *tabs, reps=1)
    kvb = kvn.astype(BF16)
    k = _bdot(kvb, wk_ref[...]) + jnp.concatenate([kpe] * N_HEADS, axis=1)
    scale = math.log2(math.e) / math.sqrt(QK_NOPE + QK_ROPE)
    q_ref[...] = (q * scale).astype(BF16)
    k_ref[...] = k.astype(BF16)
    v_ref[...] = _bdot(kvb, wv_ref[...]).astype(BF16)
    y = _bdot(zc.astype(BF16), cs_ref[...])
    y_ref[0, :, :] = y[:, 0:D_C].astype(BF16)
    y_ref[1, :, :] = y[:, D_C:2 * D_C].astype(BF16)


def _odd_in(x3, mod, g, w_in, q_g, w_q, kv_g, w_k, w_v, cs, stream, layer, rope_tabs):
    b, s, _ = x3.shape
    tm = min(s, 512)
    n_i = s // tm
    const = lambda a: pl.BlockSpec(a.shape, lambda bi, i: (0,) * a.ndim)
    in_specs = [
        pl.BlockSpec((None, tm, D_MODEL), lambda bi, i: (bi, i, 0)),
        _mod_spec(stream, layer),
        const(g), const(w_in), const(q_g), const(w_q), const(kv_g), const(w_k), const(w_v), const(cs),
    ] + [pl.BlockSpec((tm, HEAD_PAD), lambda bi, i: (i, 0))] * 3
    args = [x3, mod, g, w_in, q_g, w_q, kv_g, w_k, w_v, cs, *rope_tabs]
    hq = N_HEADS * HEAD_PAD
    out_specs = [
        pl.BlockSpec((None, 2, tm, D_C), lambda bi, i: (bi, 0, i, 0)),
        pl.BlockSpec((None, tm, hq), lambda bi, i: (bi, i, 0)),
        pl.BlockSpec((None, tm, hq), lambda bi, i: (bi, i, 0)),
        pl.BlockSpec((None, tm, N_HEADS * V_DIM), lambda bi, i: (bi, i, 0)),
    ]
    out_shape = [
        jax.ShapeDtypeStruct((b, 2, s, D_C), BF16),
        jax.ShapeDtypeStruct((b, s, hq), BF16),
        jax.ShapeDtypeStruct((b, s, hq), BF16),
        jax.ShapeDtypeStruct((b, s, N_HEADS * V_DIM), BF16),
    ]
    return pl.pallas_call(
        functools.partial(_odd_in_kernel, rope=True, emit_cache=False),
        grid=(b, n_i),
        in_specs=in_specs,
        out_specs=out_specs,
        out_shape=out_shape,
        compiler_params=_cparams("parallel", "parallel"),
        name="odd_in",
    )(*args)


def _cache_kv_kernel(c_ref, p_ref, wk_ref, wv_ref, k_ref, v_ref):
    cb = c_ref[...].astype(BF16)
    k = _bdot(cb, wk_ref[...]) + jnp.concatenate([p_ref[...]] * N_HEADS, axis=1)
    k_ref[...] = k.astype(BF16)
    v_ref[...] = _bdot(cb, wv_ref[...]).astype(BF16)


def _cache_kv(ckv, kpe_blk, w_k, w_v):
    b, p, _ = ckv.shape
    hq = N_HEADS * HEAD_PAD
    return pl.pallas_call(
        _cache_kv_kernel,
        grid=(b,),
        in_specs=[
            pl.BlockSpec((None, p, KV_LORA), lambda bi: (bi, 0, 0)),
            pl.BlockSpec((None, p, HEAD_PAD), lambda bi: (bi, 0, 0)),
            pl.BlockSpec(w_k.shape, lambda bi: (0, 0)),
            pl.BlockSpec(w_v.shape, lambda bi: (0, 0)),
        ],
        out_specs=[
            pl.BlockSpec((None, p, hq), lambda bi: (bi, 0, 0)),
            pl.BlockSpec((None, p, N_HEADS * V_DIM), lambda bi: (bi, 0, 0)),
        ],
        out_shape=[
            jax.ShapeDtypeStruct((b, p, hq), BF16),
            jax.ShapeDtypeStruct((b, p, N_HEADS * V_DIM), BF16),
        ],
        compiler_params=_cparams("parallel"),
        name="cache_kv",
    )(ckv, kpe_blk, w_k, w_v)


def _odd_mix_kernel(*refs, with_cache, n_seq=1):
    q_ref, k_ref, v_ref = refs[:3]
    refs = refs[3:]
    if with_cache:
        kc_ref, vc_ref = refs[:2]
        refs = refs[2:]
    plan_tile = None
    if with_cache:
        y_ref, f_ref, x_ref, mod_ref, wo_ref = refs[:5]
        plan_in, (o_ref, *plan_out), (a_ref, vx_ref, vcx_ref) = refs[5:9], refs[9:14], refs[14:]
        plan_tile = (pl.program_id(0) * pl.num_programs(1) + pl.program_id(1)) * (q_ref.shape[0] // ROUTE_TM)

        @pl.when(pl.program_id(1) == 0)
        def _():
            for src, dst in ((v_ref, vx_ref), (vc_ref, vcx_ref)):
                one = lax.broadcasted_iota(jnp.int32, (src.shape[0], PAIR_W - DEN_COL), 1) == 0
                for pair in range(N_HEADS // 2):
                    dst[:, pair * PAIR_W:pair * PAIR_W + DEN_COL] = src[:, pair * DEN_COL:(pair + 1) * DEN_COL]
                    dst[:, pair * PAIR_W + DEN_COL:(pair + 1) * PAIR_W] = jnp.where(one, 1.0, 0.0).astype(BF16)
    else:
        y_ref, f_ref, x_ref, mod_ref, wo_ref, o_ref, a_ref = refs[:7]
        if len(refs) > 7:
            plan_in, plan_out, plan_tile = refs[7:]
    tq = q_ref.shape[0] // n_seq
    lane = lax.broadcasted_iota(jnp.int32, (tq, 2 * V_DIM), 1)
    f_parts = []
    for sq in range(n_seq):
        rows = slice(sq * tq, (sq + 1) * tq)
        for pair in range(N_HEADS // 2):
            vcols = slice(pair * PAIR_W, (pair + 1) * PAIR_W)
            outs = []
            for h in (2 * pair, 2 * pair + 1):
                hcols = slice(h * HEAD_PAD, (h + 1) * HEAD_PAD)
                qh = q_ref[rows, hcols]
                if with_cache:
                    s = lax.dot_general(qh, k_ref[:, hcols], NT_DIMS, preferred_element_type=F32)
                    sc = lax.dot_general(qh, kc_ref[:, hcols], NT_DIMS, preferred_element_type=F32)
                    top = jnp.maximum(jnp.max(s, axis=-1, keepdims=True), jnp.max(sc, axis=-1, keepdims=True))
                    acc = _bdot(jnp.exp2((s - top).astype(BF16)), vx_ref[:, vcols])
                    acc = acc + _bdot(jnp.exp2((sc - top).astype(BF16)), vcx_ref[:, vcols])
                    outs.append(acc[:, 0:2 * V_DIM] / acc[:, DEN_COL:DEN_COL + 1])
                else:
                    s = lax.dot_general(qh, k_ref[rows, hcols], NT_DIMS, preferred_element_type=F32)
                    p = jnp.exp2(s - jnp.max(s, axis=-1, keepdims=True))
                    den = jnp.sum(p, axis=-1, keepdims=True)
                    outs.append(_bdot(p.astype(BF16), v_ref[rows, pair * DEN_COL:(pair + 1) * DEN_COL]) / den)
            a_ref[rows, pair * 2 * V_DIM:(pair + 1) * 2 * V_DIM] = (
                jnp.where(lane < V_DIM, outs[0], outs[1]).astype(BF16))
        if n_seq > 1:
            f_parts.append(_bdot(f_ref[:, 0:tq], y_ref[0, rows, :]) + _bdot(f_ref[:, tq:], y_ref[1, rows, :]))
    f = jnp.concatenate(f_parts, axis=0) if n_seq > 1 else _bdot(f_ref[...], y_ref[...].reshape(-1, D_C))
    o = _bdot(f.astype(BF16), wo_ref[0:D_C, :]) + _bdot(a_ref[...], wo_ref[D_C:, :])
    x1 = x_ref[...] + mod_ref[2:3, :] * o
    o_ref[...] = x1
    if plan_tile is not None:
        _route_plan(x1, mod_ref[...], plan_in, plan_out, plan_tile)


WHOLE_SEQS = 2


def _odd_whole_kernel(tab_ref, x_ref, mod_ref, g_ref, w_ref, qg_ref, wq_ref, kg_ref, wk_ref, wv_ref, cs_ref, f_ref,
                      wo_ref, modp_ref, dloc_ref, ys_ref, pg_ref, pw_ref, pb_ref, pu_ref,
                      o_ref, ckv_ref, kpe_ref, ph_ref, pd_ref, pr_ref, pc_ref,
                      y_ref, q_ref, k_ref, v_ref, a_ref, x2_ref, runs_ref, sem_ref):
    bi = pl.program_id(0)
    par = lax.rem(bi, 2)
    tm = ROUTE_TM

    def fetch(step, parity):
        for sq in range(WHOLE_SEQS):
            _fetch_runs(tab_ref, ys_ref, runs_ref, sem_ref, step * WHOLE_SEQS + sq, parity * WHOLE_SEQS + sq)

    @pl.when(bi == 0)
    def _():
        fetch(0, 0)

    @pl.when(bi + 1 < pl.num_programs(0))
    def _():
        fetch(bi + 1, 1 - par)

    for sq in range(WHOLE_SEQS):
        rows = slice(sq * tm, (sq + 1) * tm)
        moe = _unsort(tab_ref, ys_ref, runs_ref, sem_ref, dloc_ref[rows, 0:1], bi * WHOLE_SEQS + sq,
                      par * WHOLE_SEQS + sq)
        x2_ref[rows, :] = x_ref[rows, :] + modp_ref[5:6, :] * moe
    _odd_in_kernel(x2_ref, mod_ref, g_ref, w_ref, qg_ref, wq_ref, kg_ref, wk_ref, wv_ref, cs_ref,
                   y_ref, q_ref, k_ref, v_ref, ckv_ref, kpe_ref, rope=False, emit_cache=True)
    _odd_mix_kernel(q_ref, k_ref, v_ref, y_ref, f_ref, x2_ref, mod_ref, wo_ref, o_ref, a_ref,
                    (pg_ref, pw_ref, pb_ref, pu_ref), (ph_ref, pd_ref, pr_ref, pc_ref), bi * WHOLE_SEQS,
                    with_cache=False, n_seq=WHOLE_SEQS)


def _odd_whole(x3, pending, mod, g, w_in, q_g, w_q, kv_g, w_k, w_v, cs, fmat, w_out, plan_consts, stream, layer):
    tab, dloc, ys = pending
    b, seq, _ = x3.shape
    assert seq == ROUTE_TM and b % WHOLE_SEQS == 0 and stream.shared_cond
    b, s = b // WHOLE_SEQS, seq * WHOLE_SEQS
    x3 = x3.reshape(b, s, D_MODEL)
    hq = N_HEADS * HEAD_PAD
    hv = N_HEADS * V_DIM
    const = lambda a: pl.BlockSpec(a.shape, lambda bi, tab_ref: (0,) * a.ndim)
    row_block = lambda w: pl.BlockSpec((None, s, w), lambda bi, tab_ref: (bi, 0, 0))
    mod_block = lambda lyr: pl.BlockSpec((None, None, 6, D_MODEL), lambda bi, tab_ref: (lyr, 0, 0, 0))
    cache_block = lambda w: pl.BlockSpec((WHOLE_SEQS, 1, seq, w), lambda bi, tab_ref: (bi, 0, 0, 0))
    plan_shapes, plan_specs = _plan_out(b, s, s, lambda bi, tab_ref: (bi, 0))
    grid_spec = pltpu.PrefetchScalarGridSpec(
        num_scalar_prefetch=1,
        grid=(b,),
        in_specs=[
            row_block(D_MODEL), mod_block(layer),
            const(g), const(w_in), const(q_g), const(w_q), const(kv_g), const(w_k), const(w_v), const(cs),
            const(fmat), const(w_out),
            mod_block(layer - 1),
            pl.BlockSpec((s, LANES), lambda bi, tab_ref: (bi, 0)),
            pl.BlockSpec(memory_space=pl.ANY),
        ] + [const(a) for a in plan_consts],
        out_specs=[row_block(D_MODEL), cache_block(KV_LORA), cache_block(QK_ROPE)] + plan_specs,
        scratch_shapes=[
            pltpu.VMEM((2, s, D_C), BF16),
            pltpu.VMEM((s, hq), BF16),
            pltpu.VMEM((s, hq), BF16),
            pltpu.VMEM((s, hv), BF16),
            pltpu.VMEM((s, hv), BF16),
            pltpu.VMEM((s, D_MODEL), F32),
            pltpu.VMEM((2 * WHOLE_SEQS, RUN_ROWS, D_MODEL), F32),
            pltpu.SemaphoreType.DMA((2 * WHOLE_SEQS,)),
        ],
    )
    out = pl.pallas_call(
        _odd_whole_kernel,
        grid_spec=grid_spec,
        out_shape=[
            jax.ShapeDtypeStruct((b, s, D_MODEL), F32),
            jax.ShapeDtypeStruct((b * WHOLE_SEQS, 1, seq, KV_LORA), F32),
            jax.ShapeDtypeStruct((b * WHOLE_SEQS, 1, seq, QK_ROPE), F32),
        ] + plan_shapes,
        compiler_params=_cparams("arbitrary"),
        name="odd_whole",
    )(tab, x3, mod, g, w_in, q_g, w_q, kv_g, w_k, w_v, cs, fmat, w_out, mod, dloc, ys, *plan_consts)
    return out[0].reshape(b * WHOLE_SEQS, seq, D_MODEL), out[1], out[2], out[3:]


def _odd_mix(q, k, v, kc, vc, y, fmat, x3, mod, w_out, plan_consts, stream, layer):
    b, s, hq = q.shape
    tq = min(s, 512)
    n_i = s // tq
    p = kc.shape[1]
    hv = N_HEADS * V_DIM

    def per_batch(rows, cols):
        return pl.BlockSpec((None, rows, cols), lambda bi, i: (bi, 0, 0), pipeline_mode=pl.Buffered(1))

    in_specs = [
        pl.BlockSpec((None, tq, hq), lambda bi, i: (bi, i, 0)),
        per_batch(s, hq), per_batch(s, hv), per_batch(p, hq), per_batch(p, hv),
        per_batch(2 * s, D_C),
        pl.BlockSpec((tq, 2 * s), lambda bi, i: (i, 0)),
        pl.BlockSpec((None, tq, D_MODEL), lambda bi, i: (bi, i, 0)),
        _mod_spec(stream, layer),
        pl.BlockSpec(w_out.shape, lambda bi, i: (0, 0), pipeline_mode=pl.Buffered(1)),
    ] + [pl.BlockSpec(a.shape, lambda bi, i: (0,) * a.ndim) for a in plan_consts]
    plan_shapes, plan_specs = _plan_out(b, s, tq, lambda bi, i: (bi, i))
    out = pl.pallas_call(
        functools.partial(_odd_mix_kernel, with_cache=True),
        grid=(b, n_i),
        in_specs=in_specs,
        out_specs=[pl.BlockSpec((None, tq, D_MODEL), lambda bi, i: (bi, i, 0))] + plan_specs,
        out_shape=[jax.ShapeDtypeStruct((b, s, D_MODEL), F32)] + plan_shapes,
        scratch_shapes=[pltpu.VMEM((tq, hv), BF16), pltpu.VMEM((s, VX_W), BF16), pltpu.VMEM((p, VX_W), BF16)],
        compiler_params=_cparams("arbitrary", "arbitrary"),
        name="odd_mix",
    )(q, k, v, kc, vc, y, fmat, x3, mod, w_out, *plan_consts)
    return out[0], out[1:]


def _dft_tables(seq):
    jc = np.arange(C_GW)
    ang_c = 2.0 * np.pi * np.outer(jc, jc) / C_GW
    eye = np.eye(C_GROUPS)
    cs = np.concatenate([np.kron(eye, np.cos(ang_c)), np.kron(eye, np.sin(ang_c))], axis=1)
    jn = np.arange(seq)
    ang_n = 2.0 * np.pi * (np.outer(jn, jn) % seq) / seq
    scale = 1.0 / math.sqrt(seq * C_GW)
    fmat = np.concatenate([np.cos(ang_n), -np.sin(ang_n)], axis=1) * scale
    return jnp.asarray(cs, F32).astype(BF16), jnp.asarray(fmat, F32).astype(BF16)


def _odd_weights(w_in, w_uq, w_ukv):
    d = w_in.shape[0]
    base = D_C + Q_LORA + KV_LORA
    kpe_blk = jnp.zeros((d, HEAD_PAD), w_in.dtype).at[:, ROPE_OFF:ROPE_OFF + QK_ROPE].set(w_in[:, base:])
    w_in_p = jnp.concatenate([w_in[:, :base], kpe_blk], axis=1).astype(BF16)
    qh = w_uq.reshape(Q_LORA, N_HEADS, QK_NOPE + QK_ROPE)
    w_q = jnp.pad(qh, ((0, 0), (0, 0), (0, HEAD_PAD - QK_NOPE - QK_ROPE))).reshape(Q_LORA, -1).astype(BF16)
    kvh = w_ukv.reshape(KV_LORA, N_HEADS, QK_NOPE + V_DIM)
    w_k = jnp.pad(kvh[:, :, :QK_NOPE], ((0, 0), (0, 0), (0, HEAD_PAD - QK_NOPE))).reshape(KV_LORA, -1)
    w_v = kvh[:, :, QK_NOPE:].reshape(KV_LORA, -1)
    return w_in_p, w_q, w_k.astype(BF16), w_v.astype(BF16)


ROUTER_ROWS = 32


def _router_weights(wg, bg, we, be):
    d = wg.shape[0]
    w = jnp.concatenate([wg, we.reshape(d, N_EXPERTS)], axis=1).T
    w = jnp.pad(w, ((0, ROUTER_ROWS - w.shape[0]), (0, 0))).astype(BF16)
    b = jnp.concatenate([bg, be.reshape(N_EXPERTS)])
    b = jnp.pad(b, (0, ROUTER_ROWS - b.shape[0])).reshape(ROUTER_ROWS, 1).astype(F32)
    return w, b


def kernel(x_prompt, x_sample, cache_ckv, cache_kpe, c, c_ctx, mod_w, mod_b, norm1_g, norm2_g,
           ev_w_in, ev_conv_w, ev_sgu_norm_g, ev_sgu_w, ev_sgu_b, ev_w_out,
           od_w_in, od_q_norm_g, od_w_uq, od_kv_norm_g, od_w_ukv, od_w_out,
           moe_wg, moe_bg, moe_we, moe_be, moe_w1, moe_w3, moe_w2, final_norm_g):
    bp, n_p, d = x_prompt.shape
    bs, n_s, _ = x_sample.shape
    streams = [(_Stream(bp, n_p, True), x_prompt), (_Stream(bs, n_s, False), x_sample)]

    n_rows = 1 + bs
    mod = _adaln(jnp.concatenate([c_ctx[None, :], c], axis=0), mod_w, mod_b, n_rows)

    final_g = final_norm_g.reshape(1, d)
    xs = [x for _, x in streams]
    new_ckv, new_kpe = [], []
    pending = None
    plans = [None, None]
    for l in range(DEPTH):
        j = l // 2
        g1 = norm1_g[l].reshape(1, d)
        g2 = norm2_g[l].reshape(1, d)
        plan_consts = _plan_consts(g2, *_router_weights(moe_wg[l], moe_bg[l], moe_we[l], moe_be[l]))
        last = l == DEPTH - 1
        if l % 2 == 0:
            w_in = ev_w_in[j].astype(BF16)
            w_out = ev_w_out[j].astype(BF16)
            sgu_w = ev_sgu_w[j].astype(BF16)
            sgu_g = ev_sgu_norm_g[j].reshape(1, D_B)
            sgu_bias = jnp.repeat(ev_sgu_b[j].T, D_B // B_GROUPS, axis=1)
            for si, (st, _) in enumerate(streams):
                xs[si], plans[si] = _even_layer(xs[si], mod, g1, w_in, ev_conv_w[j], sgu_g, sgu_w, sgu_bias, w_out,
                                                plan_consts, st, l)
        else:
            w_in, w_q, w_k, w_v = _odd_weights(od_w_in[j], od_w_uq[j], od_w_ukv[j])
            w_out = od_w_out[j].astype(BF16)
            q_g = od_q_norm_g[j].reshape(1, Q_LORA)
            kv_g = od_kv_norm_g[j].reshape(1, KV_LORA)
            for si, (st, _) in enumerate(streams):
                x3 = xs[si]
                cs, fmat = _dft_tables(st.seq)
                if st.shared_cond:
                    xs[si], ckv, kpe, plans[si] = _odd_whole(x3, pending, mod, g1, w_in, q_g, w_q, kv_g, w_k, w_v, cs,
                                                             fmat, w_out, plan_consts, st, l)
                    new_ckv.append(ckv)
                    new_kpe.append(kpe)
                    continue
                y, q, k, v = _odd_in(x3, mod, g1, w_in, q_g, w_q, kv_g, w_k, w_v, cs, st, l, _rope_tables(st.seq))
                kpe_blk = jnp.pad(cache_kpe[:, j], ((0, 0), (0, 0), (ROPE_OFF, HEAD_PAD - ROPE_OFF - QK_ROPE)))
                kc, vc = _cache_kv(cache_ckv[:, j], kpe_blk, w_k, w_v)
                xs[si], plans[si] = _odd_mix(q, k, v, kc, vc, y.reshape(st.batch, 2 * st.seq, D_C), fmat, x3, mod,
                                             w_out, plan_consts, st, l)
        defer = not last and (l + 1) % 2 == 1
        x2p, x2s, pending = _moe(xs[0].reshape(bp * n_p, d), xs[1].reshape(bs * n_s, d), plans[0], plans[1], mod,
                                 moe_w1, moe_w3, moe_w2, final_g, l, n_s, last, defer)
        xs = [xs[0] if defer else x2p.reshape(bp, n_p, d), x2s.reshape(bs, n_s, d)]
    return (xs[0], xs[1], jnp.concatenate(new_ckv, axis=1), jnp.concatenate(new_kpe, axis=1))
```
